```python
import jax
import jax.numpy as jnp
from jax import lax
import numpy as np

D_MODEL = 1024
BATCH = 2
SEQ = 16384
DEPTH = 2

HEAD_DIM = 64
ATTN_BLOCK = 128
DIL_PATTERNS = ((128, 1), (512, 4), (2048, 16))
N_DIL = 3
DIL_HEADS = 6
CONV_WIDTH = 384
CONV_K = 3
NSA_Q_HEADS = 6
NSA_KV_HEADS = 2
NSA_GROUP = NSA_Q_HEADS // NSA_KV_HEADS
CMP_BLOCK = 32
CMP_STRIDE = 16
CMP_HIDDEN = 128
SEL_BLOCK = 64
N_SEL = 16
NSA_WINDOW = 512
N_NSA_BRANCH = 3
SWA_Q_HEADS = 6
SWA_KV_HEADS = 2
SWA_GROUP = SWA_Q_HEADS // SWA_KV_HEADS
SWA_WINDOW = 128
BRANCH_WIDTH = 384
N_BRANCH = 4
D_FF = 2816
N_EXPERTS = 8
TOP_K = 2
D_FF_EXPERT = 3584
PLE_DIM = 256
LN_EPS = 1e-5
ALPHA = (2 * DEPTH) ** 0.25
BETA = (8 * DEPTH) ** -0.25
NEG_INF = -1e30
DIL_WIDTH = N_DIL * DIL_HEADS * HEAD_DIM
NSA_KV_WIDTH = NSA_KV_HEADS * HEAD_DIM
SWA_KV_WIDTH = SWA_KV_HEADS * HEAD_DIM
COLUMN_SIZES = (DIL_WIDTH, DIL_WIDTH, DIL_WIDTH,
                CONV_WIDTH, CONV_WIDTH, CONV_WIDTH,
                NSA_Q_HEADS * HEAD_DIM,
                NSA_KV_WIDTH, NSA_KV_WIDTH, NSA_KV_WIDTH, NSA_KV_WIDTH, NSA_KV_WIDTH, NSA_KV_WIDTH,
                N_NSA_BRANCH * NSA_Q_HEADS,
                SWA_Q_HEADS * HEAD_DIM, SWA_KV_WIDTH, SWA_KV_WIDTH)
N_IN = sum(COLUMN_SIZES)

kernel_name = 'hybrid_dilated_conv_nsa_swa_moe_deepnorm'


def layer_norm(x, g, b):
    xf = x.astype(jnp.float32)
    mu = jnp.mean(xf, axis=-1, keepdims=True)
    var = jnp.mean(jnp.square(xf - mu), axis=-1, keepdims=True)
    return ((xf - mu) * lax.rsqrt(var + LN_EPS) * g + b).astype(x.dtype)


def banded_attention(q, k, v, window):
    b, L, kv, g, hd = q.shape
    nb = -(-L // ATTN_BLOCK)
    lp = nb * ATTN_BLOCK
    n_prev = -(-window // ATTN_BLOCK)
    span = (n_prev + 1) * ATTN_BLOCK
    qb = jnp.pad(q, ((0, 0), (0, lp - L), (0, 0), (0, 0), (0, 0))).reshape(b, nb, ATTN_BLOCK, kv, g, hd)

    def key_windows(t):
        t = jnp.pad(t, ((0, 0), (n_prev * ATTN_BLOCK, lp - L), (0, 0), (0, 0)))
        t = t.reshape(b, nb + n_prev, ATTN_BLOCK, kv, hd)
        return jnp.concatenate([t[:, s:s + nb] for s in range(n_prev + 1)], axis=2)

    kw = key_windows(k)
    vw = key_windows(v)
    s = jnp.einsum('bnqkgd,bnjkd->bnkgqj', qb, kw, preferred_element_type=jnp.float32) * (hd ** -0.5)
    qi = jnp.arange(ATTN_BLOCK)[:, None]
    kj = jnp.arange(span)[None, :]
    dist = n_prev * ATTN_BLOCK + qi - kj
    kpos = (jnp.arange(nb)[:, None, None] - n_prev) * ATTN_BLOCK + kj[None]
    mask = (dist >= 0) & (dist <= window) & (kpos >= 0)
    s = jnp.where(mask[None, :, None, None], s, NEG_INF)
    m = jnp.max(s, axis=-1, keepdims=True)
    e = jnp.exp(s - m)
    l = jnp.sum(e, axis=-1, keepdims=True)
    o = jnp.einsum('bnkgqj,bnjkd->bnqkgd', e / l, vw.astype(jnp.float32))
    lse = (m + jnp.log(l))[..., 0].transpose(0, 1, 4, 2, 3)
    return o.reshape(b, lp, kv, g, hd)[:, :L], lse.reshape(b, lp, kv, g)[:, :L]


def dilated_attention(q, k, v):
    b, S = q.shape[:2]
    outs = []
    lses = []
    for g, (window, dil) in enumerate(DIL_PATTERNS):
        def by_stride(t):
            t = t[:, :, g].reshape(b, S // dil, dil, DIL_HEADS, HEAD_DIM).transpose(0, 2, 1, 3, 4)
            return t.reshape(b * dil, S // dil, DIL_HEADS, HEAD_DIM)
        o, lse = banded_attention(by_stride(q)[:, :, :, None], by_stride(k), by_stride(v), window // dil)
        o = o.reshape(b, dil, S // dil, DIL_HEADS, HEAD_DIM).transpose(0, 2, 1, 3, 4)
        lse = lse.reshape(b, dil, S // dil, DIL_HEADS).transpose(0, 2, 1, 3)
        outs.append(o.reshape(b, S, DIL_HEADS, HEAD_DIM))
        lses.append(lse.reshape(b, S, DIL_HEADS))
    wts = jax.nn.softmax(jnp.stack(lses), axis=0)
    return jnp.sum(wts[..., None] * jnp.stack(outs), axis=0)


def gated_short_conv(gate_b, gate_c, h, conv_w):
    u = gate_c * h
    y = lax.conv_general_dilated(u, conv_w[:, None, :].astype(u.dtype), window_strides=(1,),
                                 padding=((CONV_K - 1, 0),), dimension_numbers=('NWC', 'WIO', 'NWC'),
                                 feature_group_count=u.shape[-1])
    return gate_b * y


def compress_blocks(t, pos, w1, b1, w2, b2):
    b, S, kv, hd = t.shape
    r = CMP_BLOCK // CMP_STRIDE
    nc = S // CMP_STRIDE - r + 1
    chunks = t.reshape(b, S // CMP_STRIDE, CMP_STRIDE, kv, hd)
    blocks = jnp.concatenate([chunks[:, i:i + nc] for i in range(r)], axis=2) + pos[None, None, :, None, :]
    flat = blocks.transpose(0, 1, 3, 2, 4).reshape(b, nc, kv, CMP_BLOCK * hd)
    return jax.nn.gelu(flat @ w1 + b1) @ w2 + b2


def native_sparse_attention(q, k_cmp, v_cmp, k_slc, v_slc, k_win, v_win, gates,
                            cmp_pos, cmp_w1, cmp_b1, cmp_w2, cmp_b2):
    b, S = q.shape[:2]
    kc = compress_blocks(k_cmp, cmp_pos[0], cmp_w1[0], cmp_b1[0], cmp_w2[0], cmp_b2[0])
    vc = compress_blocks(v_cmp, cmp_pos[1], cmp_w1[1], cmp_b1[1], cmp_w2[1], cmp_b2[1])
    nc = kc.shape[1]
    ns = S // SEL_BLOCK
    n_sel = min(N_SEL, ns)
    ksb = k_slc.reshape(b, ns, SEL_BLOCK, NSA_KV_HEADS, HEAD_DIM).transpose(0, 3, 1, 2, 4)
    vsb = v_slc.reshape(b, ns, SEL_BLOCK, NSA_KV_HEADS, HEAD_DIM).transpose(0, 3, 1, 2, 4)
    c_start = jnp.arange(nc) * CMP_STRIDE
    c_end = c_start + CMP_BLOCK - 1
    s_start = jnp.arange(ns) * SEL_BLOCK
    overlap = ((c_start[:, None] < s_start[None, :] + SEL_BLOCK)
               & (c_end[:, None] >= s_start[None, :])).astype(jnp.float32)
    sel_ids = jnp.arange(ns)
    scale = HEAD_DIM ** -0.5
    gather = jax.vmap(jax.vmap(lambda blocks, ix: blocks[ix]))

    def query_block(args):
        qb, i = args
        t = i * ATTN_BLOCK + jnp.arange(ATTN_BLOCK)
        s = jnp.einsum('bqkgd,bckd->bkgqc', qb, kc, preferred_element_type=jnp.float32) * scale
        vis = c_end[None, :] <= t[:, None]
        s = jnp.where(vis, s, NEG_INF)
        e = jnp.exp(s - jnp.max(s, axis=-1, keepdims=True)) * vis
        p_cmp = e / jnp.maximum(jnp.sum(e, axis=-1, keepdims=True), 1e-30)
        o_cmp = jnp.einsum('bkgqc,bckd->bqkgd', p_cmp, vc.astype(jnp.float32))
        imp = jnp.einsum('bkgqc,cj->bkqj', p_cmp, overlap)
        cur = t // SEL_BLOCK
        causal = sel_ids[None, :] <= cur[:, None]
        forced = (sel_ids[None, :] == 0) | (sel_ids[None, :] == cur[:, None]) | (sel_ids[None, :] == cur[:, None] - 1)
        imp = jnp.where(forced, jnp.inf, jnp.where(causal, imp, -jnp.inf))
        top, idx = lax.top_k(imp, n_sel)
        valid = top > -jnp.inf
        kg = gather(ksb, idx)
        vg = gather(vsb, idx)
        s2 = jnp.einsum('bqkgd,bkqnjd->bkgqnj', qb, kg, preferred_element_type=jnp.float32) * scale
        kpos = idx[..., None] * SEL_BLOCK + jnp.arange(SEL_BLOCK)
        m2 = valid[..., None] & (kpos <= t[None, None, :, None, None])
        s2 = jnp.where(m2[:, :, None], s2, NEG_INF)
        sh = s2.shape
        p2 = jax.nn.softmax(s2.reshape(sh[:4] + (n_sel * SEL_BLOCK,)), axis=-1).reshape(sh)
        o_slc = jnp.einsum('bkgqnj,bkqnjd->bqkgd', p2, vg.astype(jnp.float32))
        return o_cmp, o_slc

    nq = S // ATTN_BLOCK
    qblk = q.reshape(b, nq, ATTN_BLOCK, NSA_KV_HEADS, NSA_GROUP, HEAD_DIM).transpose(1, 0, 2, 3, 4, 5)
    o_cmp, o_slc = lax.map(query_block, (qblk, jnp.arange(nq)))

    def unblock(o):
        return o.transpose(1, 0, 2, 3, 4, 5).reshape(b, S, NSA_KV_HEADS, NSA_GROUP, HEAD_DIM)

    o_win, _ = banded_attention(q, k_win, v_win, NSA_WINDOW - 1)
    return (gates[:, :, 0][..., None] * unblock(o_cmp)
            + gates[:, :, 1][..., None] * unblock(o_slc)
            + gates[:, :, 2][..., None] * o_win)


def token_mixers(x, w_in, conv_w, cmp_pos, cmp_w1, cmp_b1, cmp_w2, cmp_b2, sinks):
    b, S, _ = x.shape
    z = x @ w_in
    (qa, ka, va, gb, gc, hb, qc, kcc, vcc, ksc, vsc, kwc, vwc, gn, qd, kd, vd) = jnp.split(
        z, np.cumsum(COLUMN_SIZES)[:-1].tolist(), axis=-1)

    def dil(t):
        return t.reshape(b, S, N_DIL, DIL_HEADS, HEAD_DIM)

    def nkv(t):
        return t.reshape(b, S, NSA_KV_HEADS, HEAD_DIM)

    def skv(t):
        return t.reshape(b, S, SWA_KV_HEADS, HEAD_DIM)

    o_a = dilated_attention(dil(qa), dil(ka), dil(va)).reshape(b, S, BRANCH_WIDTH)
    o_b = gated_short_conv(gb, gc, hb, conv_w)
    nsa_gates = jax.nn.sigmoid(gn.astype(jnp.float32)).reshape(b, S, N_NSA_BRANCH, NSA_KV_HEADS, NSA_GROUP)
    o_c = native_sparse_attention(qc.reshape(b, S, NSA_KV_HEADS, NSA_GROUP, HEAD_DIM),
                                  nkv(kcc), nkv(vcc), nkv(ksc), nkv(vsc), nkv(kwc), nkv(vwc), nsa_gates,
                                  cmp_pos, cmp_w1, cmp_b1, cmp_w2, cmp_b2).reshape(b, S, BRANCH_WIDTH)
    o_d, lse_d = banded_attention(qd.reshape(b, S, SWA_KV_HEADS, SWA_GROUP, HEAD_DIM), skv(kd), skv(vd),
                                  SWA_WINDOW - 1)
    sink_keep = jax.nn.sigmoid(lse_d - sinks.reshape(SWA_KV_HEADS, SWA_GROUP).astype(jnp.float32))
    o_d = (o_d * sink_keep[..., None]).reshape(b, S, BRANCH_WIDTH)
    return [o_a.astype(x.dtype), o_b.astype(x.dtype), o_c.astype(x.dtype), o_d.astype(x.dtype)]


def swiglu(h, w_gate, w_up, w_down):
    return (jax.nn.silu(h @ w_gate) * (h @ w_up)) @ w_down


def moe_swiglu(h, w_router, b_router, w_gate, w_up, w_down):
    logits = (h @ w_router + b_router).astype(jnp.float32)
    top_v, top_i = lax.top_k(logits, TOP_K)
    top_w = jax.nn.softmax(top_v, axis=-1)
    comb = jnp.sum(jax.nn.one_hot(top_i, N_EXPERTS, dtype=jnp.float32) * top_w[..., None], axis=-2)
    comb = comb.astype(h.dtype)
    out = jnp.zeros_like(h)
    for e in range(N_EXPERTS):
        out = out + comb[..., e:e + 1] * swiglu(h, w_gate[e], w_up[e], w_down[e])
    return out


def setup_inputs(seed: int = 0) -> dict:
    key = jax.random.key(seed)
    ks = jax.random.split(key, 32)
    d = D_MODEL
    n_dense = (DEPTH + 1) // 2
    n_moe = DEPTH // 2

    def nrm(k, shape, scale):
        return jax.random.normal(k, shape, jnp.float32) * scale

    return {
        'x': nrm(ks[0], (BATCH, SEQ, d), 1.0),
        'p': nrm(ks[1], (DEPTH, BATCH, SEQ, PLE_DIM), 1.0),
        'w_in': nrm(ks[2], (DEPTH, d, N_IN), d ** -0.5),
        'conv_w': nrm(ks[3], (DEPTH, CONV_K, CONV_WIDTH), CONV_K ** -0.5),
        'cmp_pos': nrm(ks[4], (DEPTH, 2, CMP_BLOCK, HEAD_DIM), 0.1),
        'cmp_w1': nrm(ks[5], (DEPTH, 2, CMP_BLOCK * HEAD_DIM, CMP_HIDDEN), (CMP_BLOCK * HEAD_DIM) ** -0.5),
        'cmp_b1': nrm(ks[6], (DEPTH, 2, CMP_HIDDEN), 0.01),
        'cmp_w2': nrm(ks[7], (DEPTH, 2, CMP_HIDDEN, HEAD_DIM), CMP_HIDDEN ** -0.5),
        'cmp_b2': nrm(ks[8], (DEPTH, 2, HEAD_DIM), 0.01),
        'sinks': nrm(ks[9], (DEPTH, SWA_Q_HEADS), 1.0),
        'w_branch': nrm(ks[10], (DEPTH, N_BRANCH, BRANCH_WIDTH, d), BRANCH_WIDTH ** -0.5 * BETA),
        'w_merge_gate': nrm(ks[11], (DEPTH, N_BRANCH, d, d), d ** -0.5),
        'b_merge_gate': nrm(ks[12], (DEPTH, N_BRANCH, d), 0.01),
        'w_out': nrm(ks[13], (DEPTH, d, d), d ** -0.5 * BETA),
        'ln_mix_g': 1.0 + nrm(ks[14], (DEPTH, d), 0.01),
        'ln_mix_b': nrm(ks[15], (DEPTH, d), 0.01),
        'ffn_w_gate': nrm(ks[16], (n_dense, d, D_FF), d ** -0.5),
        'ffn_w_up': nrm(ks[17], (n_dense, d, D_FF), d ** -0.5 * BETA),
        'ffn_w_down': nrm(ks[18], (n_dense, D_FF, d), D_FF ** -0.5 * BETA),
        'w_router': nrm(ks[19], (n_moe, d, N_EXPERTS), d ** -0.5),
        'b_router': nrm(ks[20], (n_moe, N_EXPERTS), 0.01),
        'moe_w_gate': nrm(ks[21], (n_moe, N_EXPERTS, d, D_FF_EXPERT), d ** -0.5),
        'moe_w_up': nrm(ks[22], (n_moe, N_EXPERTS, d, D_FF_EXPERT), d ** -0.5 * BETA),
        'moe_w_down': nrm(ks[23], (n_moe, N_EXPERTS, D_FF_EXPERT, d), D_FF_EXPERT ** -0.5 * BETA),
        'ple_w': nrm(ks[24], (DEPTH, PLE_DIM, d), PLE_DIM ** -0.5),
        'ple_gate_w': nrm(ks[25], (DEPTH, d, d), d ** -0.5),
        'ple_gate_b': nrm(ks[26], (DEPTH, d), 0.01),
        'ln_ffn_g': 1.0 + nrm(ks[27], (DEPTH, d), 0.01),
        'ln_ffn_b': nrm(ks[28], (DEPTH, d), 0.01),
    }


def reference(x, p, w_in, conv_w, cmp_pos, cmp_w1, cmp_b1, cmp_w2, cmp_b2, sinks, w_branch,
              w_merge_gate, b_merge_gate, w_out, ln_mix_g, ln_mix_b, ffn_w_gate, ffn_w_up, ffn_w_down,
              w_router, b_router, moe_w_gate, moe_w_up, moe_w_down, ple_w, ple_gate_w, ple_gate_b,
              ln_ffn_g, ln_ffn_b):
    for i in range(DEPTH):
        branches = token_mixers(x, w_in[i], conv_w[i], cmp_pos[i], cmp_w1[i], cmp_b1[i], cmp_w2[i],
                                cmp_b2[i], sinks[i])
        merged = jnp.zeros_like(x)
        for m in range(N_BRANCH):
            gate = jax.nn.sigmoid(x @ w_merge_gate[i, m] + b_merge_gate[i, m])
            merged = merged + gate * (branches[m] @ w_branch[i, m])
        x = layer_norm(ALPHA * x + merged @ w_out[i], ln_mix_g[i], ln_mix_b[i])
        j = i // 2
        if i % 2 == 0:
            f = swiglu(x, ffn_w_gate[j], ffn_w_up[j], ffn_w_down[j])
        else:
            f = moe_swiglu(x, w_router[j], b_router[j], moe_w_gate[j], moe_w_up[j], moe_w_down[j])
        ple = jax.nn.sigmoid(x @ ple_gate_w[i] + ple_gate_b[i]) * (p[i] @ ple_w[i])
        x = layer_norm(ALPHA * x + f + ple, ln_ffn_g[i], ln_ffn_b[i])
    return x
```

```python
import functools

import numpy as np
import jax
import jax.numpy as jnp
from jax import lax
from jax.experimental import pallas as pl
from jax.experimental.pallas import tpu as pltpu

D_MODEL = 1024
HEAD_DIM = 64
DIL_PATTERNS = ((128, 1), (512, 4), (2048, 16))
N_DIL = 3
DIL_HEADS = 6
CONV_WIDTH = 384
CONV_K = 3
NSA_Q_HEADS = 6
NSA_KV_HEADS = 2
CMP_BLOCK = 32
CMP_STRIDE = 16
CMP_HIDDEN = 128
SEL_BLOCK = 64
N_SEL = 16
NSA_WINDOW = 512
SWA_Q_HEADS = 6
SWA_WINDOW = 128
BRANCH_WIDTH = 384
N_BRANCH = 4
N_EXPERTS = 8
LN_EPS = 1e-5
NEG_INF = -1e30
DIL_WIDTH = N_DIL * DIL_HEADS * HEAD_DIM
COLUMN_SIZES = (DIL_WIDTH, DIL_WIDTH, DIL_WIDTH, CONV_WIDTH, CONV_WIDTH, CONV_WIDTH,
                NSA_Q_HEADS * HEAD_DIM, 128, 128, 128, 128, 128, 128, 3 * NSA_Q_HEADS,
                SWA_Q_HEADS * HEAD_DIM, 128, 128)
COL_OFF = np.concatenate([[0], np.cumsum(COLUMN_SIZES)]).tolist()

LANES = 128
V7X_VMEM_BYTES = 64 * 1024 * 1024
MIB = 1024 * 1024

CDT = jnp.bfloat16
F32 = jnp.float32
QK_SCALE = HEAD_DIM ** -0.5
SUB_Q = 128
BIG = 1e30

_GQA_HEAD_ORDER = (0, 3, 1, 4, 2, 5)
GQA_COL_PERM = np.concatenate([np.arange(h * HEAD_DIM, (h + 1) * HEAD_DIM) for h in _GQA_HEAD_ORDER])
GQA_COL_HEAD = GQA_COL_PERM // HEAD_DIM


def _cparams(sem, vmem_mib):
    return pltpu.CompilerParams(dimension_semantics=sem, vmem_limit_bytes=int(vmem_mib * MIB))


def _nt_dot(a, b):
    return lax.dot_general(a, b, (((1,), (1,)), ((), ())), preferred_element_type=F32)


def _dot(a, b):
    return jnp.dot(a, b, preferred_element_type=F32)


def _layer_norm(r, g, b):
    mu = jnp.mean(r, axis=-1, keepdims=True)
    d = r - mu
    var = jnp.mean(d * d, axis=-1, keepdims=True)
    return d * lax.rsqrt(var + LN_EPS) * g + b


def _half_masks():
    lane = lax.broadcasted_iota(jnp.int32, (1, LANES), 1)
    return lane < HEAD_DIM


def _linear_kernel(x_ref, w_ref, *o_refs, splits, n_chunk):
    xb = x_ref[...].astype(CDT)
    for o_ref, (c0, c1) in zip(o_refs, splits):
        for a in range(c0, c1, n_chunk):
            b = min(a + n_chunk, c1)
            o_ref[:, a - c0:b - c0] = _dot(xb, w_ref[:, a:b]).astype(o_ref.dtype)


def _linear(x2d, w, splits, n_chunk, name, tm=512):
    t, k = x2d.shape
    tm = min(tm, t)
    n = w.shape[1]
    outs = [jax.ShapeDtypeStruct((t, c1 - c0), CDT) for c0, c1 in splits]
    return pl.pallas_call(
        functools.partial(_linear_kernel, splits=tuple(splits), n_chunk=n_chunk),
        grid=(t // tm,),
        in_specs=[pl.BlockSpec((tm, k), lambda i: (i, 0)),
                  pl.BlockSpec((k, n), lambda i: (0, 0))],
        out_specs=[pl.BlockSpec((tm, c1 - c0), lambda i: (i, 0)) for c0, c1 in splits],
        out_shape=outs,
        compiler_params=_cparams(("parallel",), 48),
        name=name,
    )(x2d, w)


def _conv_gate_kernel(x_ref, xh_ref, wc_ref, cw_ref, wg_ref, ob_ref, g_ref, *, tm):
    i = pl.program_id(1)
    w = CONV_WIDTH
    xb = x_ref[...].astype(CDT)
    z = _dot(xb, wc_ref[...])
    u = z[:, w:2 * w] * z[:, 2 * w:3 * w]
    zh = _dot(xh_ref[...].astype(CDT), wc_ref[:, w:3 * w])
    uh = zh[:, :w] * zh[:, w:]
    uh = jnp.where(i == 0, 0.0, uh)
    row = lax.broadcasted_iota(jnp.int32, (tm, w), 0)
    u1 = jnp.where(row == 0, uh[7:8, :], pltpu.roll(u, 1, 0))
    u2 = jnp.where(row == 0, uh[6:7, :], jnp.where(row == 1, uh[7:8, :], pltpu.roll(u, 2, 0)))
    y = cw_ref[0:1, :] * u2 + cw_ref[1:2, :] * u1 + cw_ref[2:3, :] * u
    ob_ref[...] = (z[:, :w] * y).astype(ob_ref.dtype)
    g_ref[...] = jax.nn.sigmoid(_dot(xb, wg_ref[...]))


def _conv_gate(x, wc, conv_w, wg, tm=512):
    b, s, d = x.shape
    tm = min(tm, s)
    hb = tm // 8
    return pl.pallas_call(
        functools.partial(_conv_gate_kernel, tm=tm),
        grid=(b, s // tm),
        in_specs=[pl.BlockSpec((None, tm, d), lambda bi, i: (bi, i, 0)),
                  pl.BlockSpec((None, 8, d), lambda bi, i: (bi, jnp.maximum(i * hb - 1, 0), 0)),
                  pl.BlockSpec(wc.shape, lambda bi, i: (0, 0)),
                  pl.BlockSpec(conv_w.shape, lambda bi, i: (0, 0)),
                  pl.BlockSpec(wg.shape, lambda bi, i: (0, 0))],
        out_specs=[pl.BlockSpec((None, tm, CONV_WIDTH), lambda bi, i: (bi, i, 0)),
                   pl.BlockSpec((None, tm, wg.shape[1]), lambda bi, i: (bi, i, 0))],
        out_shape=[jax.ShapeDtypeStruct((b, s, CONV_WIDTH), CDT),
                   jax.ShapeDtypeStruct((b, s, wg.shape[1]), F32)],
        compiler_params=_cparams(("parallel", "parallel"), 48),
        name="conv_gate",
    )(x, x, wc, conv_w, wg)


def _banded_kernel(*refs, window, pr, tq, kw, want_lse, has_sink):
    q_ref, kp_ref, kc_ref, vp_ref, vc_ref = refs[:5]
    n = 5
    sink_ref = None
    if has_sink:
        sink_ref = refs[n]
        n += 1
    o_ref = refs[n]
    n += 1
    lse_ref = None
    if want_lse:
        lse_ref = refs[n]
        n += 1
    kbuf, vbuf = refs[n], refs[n + 1]

    i = pl.program_id(2)
    kbuf[0:pr, :] = kp_ref[...]
    kbuf[pr:pr + tq, :] = kc_ref[...]
    vbuf[0:pr, :] = vp_ref[...]
    vbuf[pr:pr + tq, :] = vc_ref[...]

    span = SUB_Q + pr
    qi = lax.broadcasted_iota(jnp.int32, (SUB_Q, span), 0)
    kj = lax.broadcasted_iota(jnp.int32, (SUB_Q, span), 1)
    dist = pr + qi - kj
    band = (dist >= 0) & (dist <= window)
    lo = _half_masks()
    for sb in range(tq // SUB_Q):
        r0 = sb * SUB_Q
        mask = band & (i * tq + r0 - pr + kj >= 0)
        for p in range(3):
            c0 = p * LANES
            kc0 = c0 if kw == 3 * LANES else 0
            qp = q_ref[r0:r0 + SUB_Q, c0:c0 + LANES]
            kk = kbuf[r0:r0 + span, kc0:kc0 + LANES]
            vv = vbuf[r0:r0 + span, kc0:kc0 + LANES]
            o_half, lse_half = [], []
            for hm in (lo, jnp.logical_not(lo)):
                qm = (jnp.where(hm, qp, 0) * QK_SCALE).astype(CDT)
                s = jnp.where(mask, _nt_dot(qm, kk), NEG_INF)
                m = jnp.max(s, axis=-1, keepdims=True)
                e = jnp.exp(s - m)
                l = jnp.sum(e, axis=-1, keepdims=True)
                o_half.append(_dot(e.astype(CDT), vv) / l)
                lse_half.append(m + jnp.log(l))
            o_pair = jnp.where(lo, o_half[0], o_half[1])
            lse_pair = jnp.where(lo, lse_half[0], lse_half[1])
            if has_sink:
                o_pair = o_pair * jax.nn.sigmoid(lse_pair - sink_ref[:, c0:c0 + LANES])
            o_ref[r0:r0 + SUB_Q, c0:c0 + LANES] = o_pair.astype(o_ref.dtype)
            if want_lse:
                lse_ref[r0:r0 + SUB_Q, c0:c0 + LANES] = lse_pair


def _banded(qa, ka, va, *, nrep, qcol, kcol, vcol, kw, window, want_lse, sink_row=None, tq=256):
    b, l, _ = qa.shape
    pr = -(-window // SUB_Q) * SUB_Q
    tq = min(max(tq, pr), l)
    assert tq % pr == 0 and l % tq == 0, (tq, pr, l)
    ratio = tq // pr
    qw = 3 * LANES
    in_specs = [
        pl.BlockSpec((None, tq, qw), lambda bi, r, i: (bi, i, qcol(r))),
        pl.BlockSpec((None, pr, kw), lambda bi, r, i: (bi, jnp.maximum(i * ratio - 1, 0), kcol(r))),
        pl.BlockSpec((None, tq, kw), lambda bi, r, i: (bi, i, kcol(r))),
        pl.BlockSpec((None, pr, kw), lambda bi, r, i: (bi, jnp.maximum(i * ratio - 1, 0), vcol(r))),
        pl.BlockSpec((None, tq, kw), lambda bi, r, i: (bi, i, vcol(r))),
    ]
    args = [qa, ka, ka, va, va]
    if sink_row is not None:
        in_specs.append(pl.BlockSpec(sink_row.shape, lambda bi, r, i: (0, 0)))
        args.append(sink_row)
    out_specs = [pl.BlockSpec((None, tq, qw), lambda bi, r, i: (bi, i, r))]
    out_shape = [jax.ShapeDtypeStruct((b, l, nrep * qw), CDT)]
    if want_lse:
        out_specs.append(pl.BlockSpec((None, tq, qw), lambda bi, r, i: (bi, i, r)))
        out_shape.append(jax.ShapeDtypeStruct((b, l, nrep * qw), F32))
    res = pl.pallas_call(
        functools.partial(_banded_kernel, window=window, pr=pr, tq=tq, kw=kw, want_lse=want_lse,
                          has_sink=sink_row is not None),
        grid=(b, nrep, l // tq),
        in_specs=in_specs,
        out_specs=out_specs,
        out_shape=out_shape,
        scratch_shapes=[pltpu.VMEM((pr + tq, kw), ka.dtype), pltpu.VMEM((pr + tq, kw), va.dtype)],
        compiler_params=_cparams(("parallel", "parallel", "parallel"), 32),
        name=f"banded_w{window}_k{kw}_r{nrep}",
    )(*args)
    return res


def _gelu_tanh(x):
    return 0.5 * x * (1.0 + jnp.tanh(0.7978845608028654 * (x + 0.044715 * (x * x * x))))


def _compress_kernel(x_ref, pa_ref, pb_ref, w1a_ref, w1b_ref, b1_ref, w2_ref, b2_ref, o_ref):
    x = x_ref[...].astype(F32)
    n = x.shape[0]
    a = _dot((x + pa_ref[...]).astype(CDT), w1a_ref[...])
    bm = _dot((x + pb_ref[...]).astype(CDT), w1b_ref[...])
    h = a + pltpu.roll(bm, n - 1, 0) + b1_ref[...]
    o_ref[...] = (_dot(_gelu_tanh(h).astype(CDT), w2_ref[...]) + b2_ref[...]).astype(o_ref.dtype)


def _compress(t, pos, w1, b1, w2, b2):
    b, s, _ = t.shape
    nch = s // CMP_STRIDE
    xw = CMP_STRIDE * LANES
    x = t.reshape(b, nch, xw)
    eye = jnp.eye(NSA_KV_HEADS, dtype=F32)
    w1r = w1.reshape(CMP_BLOCK, HEAD_DIM, CMP_HIDDEN)

    def expand_w1(part):
        return jnp.einsum('tdj,kl->tkdlj', part, eye).reshape(xw, NSA_KV_HEADS * CMP_HIDDEN).astype(CDT)

    def expand_pos(part):
        return jnp.broadcast_to(part[:, None, :], (CMP_STRIDE, NSA_KV_HEADS, HEAD_DIM)).reshape(1, xw)

    w1a, w1b = expand_w1(w1r[:CMP_STRIDE]), expand_w1(w1r[CMP_STRIDE:])
    pa, pb = expand_pos(pos[:CMP_STRIDE]), expand_pos(pos[CMP_STRIDE:])
    b1e = jnp.tile(b1, NSA_KV_HEADS).reshape(1, -1)
    w2e = jnp.einsum('jd,kl->kjld', w2, eye).reshape(NSA_KV_HEADS * CMP_HIDDEN, LANES).astype(CDT)
    b2e = jnp.tile(b2, NSA_KV_HEADS).reshape(1, -1)
    consts = [pa, pb, w1a, w1b, b1e, w2e, b2e]
    return pl.pallas_call(
        _compress_kernel,
        grid=(b,),
        in_specs=[pl.BlockSpec((None, nch, xw), lambda bi: (bi, 0, 0))]
        + [pl.BlockSpec(c.shape, lambda bi: (0, 0)) for c in consts],
        out_specs=pl.BlockSpec((None, nch, LANES), lambda bi: (bi, 0, 0)),
        out_shape=jax.ShapeDtypeStruct((b, nch, LANES), CDT),
        compiler_params=_cparams(("parallel",), 48),
        name="nsa_compress",
    )(x, *consts)


def _cmp_select_kernel(q_ref, kc_ref, vc_ref, ov_ref, o_ref, sel_ref, *, tq, n_sel):
    i = pl.program_id(1)
    ncp = kc_ref.shape[0]
    ns = ov_ref.shape[1]
    lo = _half_masks()
    t_col = i * tq + lax.broadcasted_iota(jnp.int32, (tq, 1), 0)
    c_end = lax.broadcasted_iota(jnp.int32, (tq, ncp), 1) * CMP_STRIDE + (CMP_BLOCK - 1)
    vis = c_end <= t_col
    blk = lax.broadcasted_iota(jnp.int32, (tq, ns), 1)
    cur = t_col // SEL_BLOCK
    causal = blk <= cur
    forced = (blk == 0) | (blk == cur) | (blk == cur - 1)
    kc = kc_ref[...]
    vc = vc_ref[...]
    o_kv = []
    for kv, hm in enumerate((lo, jnp.logical_not(lo))):
        psum = jnp.zeros((tq, ncp), F32)
        o_p = []
        for p in range(3):
            qm = (jnp.where(hm, q_ref[:, p * LANES:(p + 1) * LANES], 0) * QK_SCALE).astype(CDT)
            s = jnp.where(vis, _nt_dot(qm, kc), NEG_INF)
            m = jnp.max(s, axis=-1, keepdims=True)
            e = jnp.where(vis, jnp.exp(s - m), 0.0)
            pn = e / jnp.maximum(jnp.sum(e, axis=-1, keepdims=True), 1e-30)
            o_p.append(_dot(pn.astype(CDT), vc))
            psum = psum + pn
        o_kv.append(o_p)
        p_hi = psum.astype(CDT)
        p_lo = (psum - p_hi.astype(F32)).astype(CDT)
        imp = _dot(p_hi, ov_ref[...]) + _dot(p_lo, ov_ref[...])
        work = jnp.where(forced, BIG, jnp.where(causal, imp, -BIG))

        def pick(_, carry):
            work, sel = carry
            m = jnp.max(work, axis=-1, keepdims=True)
            idx = jnp.min(jnp.where(work == m, blk, ns), axis=-1, keepdims=True)
            hit = blk == idx
            sel = jnp.where(hit & (m > -BIG), 1.0, sel)
            return jnp.where(hit, -2.0 * BIG, work), sel

        _, sel = lax.fori_loop(0, n_sel, pick, (work, jnp.zeros((tq, ns), F32)))
        sel_ref[:, kv * ns:(kv + 1) * ns] = sel.astype(sel_ref.dtype)
    for p in range(3):
        o_ref[:, p * LANES:(p + 1) * LANES] = jnp.where(lo, o_kv[0][p], o_kv[1][p]).astype(o_ref.dtype)


def _cmp_select(zr, qcol, kc, vc, tq=128):
    b, s, _ = zr.shape
    ncp = kc.shape[1]
    ns = s // SEL_BLOCK
    n_sel = min(N_SEL, ns)
    tq = min(tq, s)
    c = np.arange(ncp)[:, None] * CMP_STRIDE
    j = np.arange(ns)[None, :] * SEL_BLOCK
    overlap = ((c < j + SEL_BLOCK) & (c + CMP_BLOCK - 1 >= j)).astype(np.float32)
    overlap[ncp - 1:, :] = 0.0
    ov = jnp.asarray(overlap, CDT)
    qw = 3 * LANES
    return pl.pallas_call(
        functools.partial(_cmp_select_kernel, tq=tq, n_sel=n_sel),
        grid=(b, s // tq),
        in_specs=[pl.BlockSpec((None, tq, qw), lambda bi, i: (bi, i, qcol)),
                  pl.BlockSpec((None, ncp, LANES), lambda bi, i: (bi, 0, 0)),
                  pl.BlockSpec((None, ncp, LANES), lambda bi, i: (bi, 0, 0)),
                  pl.BlockSpec(ov.shape, lambda bi, i: (0, 0))],
        out_specs=[pl.BlockSpec((None, tq, qw), lambda bi, i: (bi, i, 0)),
                   pl.BlockSpec((None, tq, 2 * ns), lambda bi, i: (bi, i, 0))],
        out_shape=[jax.ShapeDtypeStruct((b, s, qw), CDT),
                   jax.ShapeDtypeStruct((b, s, 2 * ns), CDT)],
        compiler_params=_cparams(("parallel", "parallel"), 48),
        name="nsa_cmp_select",
    )(zr, kc, vc, ov)


def _slc_kernel(q_ref, k_ref, v_ref, sel_ref, o_ref, *, tq, tk):
    i = pl.program_id(1)
    ns = sel_ref.shape[1] // 2
    bpt = tk // SEL_BLOCK
    lo = _half_masks()
    t_col = i * tq + lax.broadcasted_iota(jnp.int32, (tq, 1), 0)
    kcol = lax.broadcasted_iota(jnp.int32, (tq, tk), 1)
    e_row = lax.broadcasted_iota(jnp.int32, (ns, tk), 0)
    e_col = lax.broadcasted_iota(jnp.int32, (ns, tk), 1) // SEL_BLOCK
    n_tiles = ((i + 1) * tq + tk - 1) // tk
    o_kv = []
    for kv, hm in enumerate((lo, jnp.logical_not(lo))):
        qs = jnp.concatenate(
            [(jnp.where(hm, q_ref[:, p * LANES:(p + 1) * LANES], 0) * QK_SCALE).astype(CDT) for p in range(3)],
            axis=0)
        sel = sel_ref[:, kv * ns:(kv + 1) * ns]

        def step(j, carry):
            m, l, acc = carry
            k0 = pl.multiple_of(j * tk, tk)
            kt = k_ref[pl.ds(k0, tk), :]
            vt = v_ref[pl.ds(k0, tk), :]
            expand = (e_row == e_col + j * bpt).astype(CDT)
            allowed = (_dot(sel, expand) > 0.5) & (kcol + j * tk <= t_col)
            bias = jnp.where(allowed, 0.0, NEG_INF)
            s = _nt_dot(qs, kt) + jnp.concatenate([bias, bias, bias], axis=0)
            m_new = jnp.maximum(m, jnp.max(s, axis=-1, keepdims=True))
            alpha = jnp.exp(m - m_new)
            e = jnp.exp(s - m_new)
            l = alpha * l + jnp.sum(e, axis=-1, keepdims=True)
            acc = alpha * acc + _dot(e.astype(CDT), vt)
            return m_new, l, acc

        init = (jnp.full((3 * tq, 1), NEG_INF, F32), jnp.zeros((3 * tq, 1), F32), jnp.zeros((3 * tq, LANES), F32))
        m, l, acc = lax.fori_loop(0, n_tiles, step, init)
        o_kv.append(acc / l)
    for p in range(3):
        o_ref[:, p * LANES:(p + 1) * LANES] = jnp.where(
            lo, o_kv[0][p * tq:(p + 1) * tq], o_kv[1][p * tq:(p + 1) * tq]).astype(o_ref.dtype)


def _slc(zr, qcol, kcol, vcol, sel, tq=256, tk=512):
    b, s, _ = zr.shape
    tq = min(tq, s)
    tk = min(tk, s)
    qw = 3 * LANES
    ns2 = sel.shape[2]
    return pl.pallas_call(
        functools.partial(_slc_kernel, tq=tq, tk=tk),
        grid=(b, s // tq),
        in_specs=[pl.BlockSpec((None, tq, qw), lambda bi, i: (bi, i, qcol)),
                  pl.BlockSpec((None, s, LANES), lambda bi, i: (bi, 0, kcol)),
                  pl.BlockSpec((None, s, LANES), lambda bi, i: (bi, 0, vcol)),
                  pl.BlockSpec((None, tq, ns2), lambda bi, i: (bi, i, 0))],
        out_specs=pl.BlockSpec((None, tq, qw), lambda bi, i: (bi, i, 0)),
        out_shape=jax.ShapeDtypeStruct((b, s, qw), CDT),
        compiler_params=_cparams(("parallel", "arbitrary"), 48),
        name="nsa_slc",
    )(zr, zr, zr, sel)


def _merge_kernel(x_ref, oa0, oa1, oa2, la0, la1, la2, ob, ocmp, oslc, owin, gates, od,
                  wg_ref, bg_ref, wb_ref, wo_ref, g_ref, b_ref, o_ref, *, alpha):
    x = x_ref[...]
    xb = x.astype(CDT)
    bw = BRANCH_WIDTH
    l0, l1, l2 = la0[...], la1[...], la2[...]
    mx = jnp.maximum(jnp.maximum(l0, l1), l2)
    w0, w1, w2 = jnp.exp(l0 - mx), jnp.exp(l1 - mx), jnp.exp(l2 - mx)
    o_a = (w0 * oa0[...].astype(F32) + w1 * oa1[...].astype(F32) + w2 * oa2[...].astype(F32)) / (w0 + w1 + w2)
    o_c = (gates[:, 0:bw] * ocmp[...].astype(F32) + gates[:, bw:2 * bw] * oslc[...].astype(F32)
           + gates[:, 2 * bw:3 * bw] * owin[...].astype(F32))
    branches = (o_a.astype(CDT), ob[...], o_c.astype(CDT), od[...])
    d = x.shape[1]
    merged = jnp.zeros(x.shape, F32)
    for m in range(N_BRANCH):
        gate = jax.nn.sigmoid(_dot(xb, wg_ref[:, m * d:(m + 1) * d]) + bg_ref[:, m * d:(m + 1) * d])
        merged = merged + gate * _dot(branches[m], wb_ref[m])
    r = alpha * x + _dot(merged.astype(CDT), wo_ref[...])
    o_ref[...] = _layer_norm(r, g_ref[...], b_ref[...])


def _merge(x2d, branch_inputs, wg, bg, wb, wo, g, b, alpha, tm=256):
    t, d = x2d.shape
    tm = min(tm, t)
    row = lambda i: (i, 0)
    const2 = lambda i: (0, 0)
    in_specs = [pl.BlockSpec((tm, d), row)]
    in_specs += [pl.BlockSpec((tm, a.shape[1]), row) for a in branch_inputs]
    in_specs += [pl.BlockSpec(wg.shape, const2), pl.BlockSpec(bg.shape, const2),
                 pl.BlockSpec(wb.shape, lambda i: (0, 0, 0)), pl.BlockSpec(wo.shape, const2),
                 pl.BlockSpec(g.shape, const2), pl.BlockSpec(b.shape, const2)]
    return pl.pallas_call(
        functools.partial(_merge_kernel, alpha=alpha),
        grid=(t // tm,),
        in_specs=in_specs,
        out_specs=pl.BlockSpec((tm, d), row),
        out_shape=jax.ShapeDtypeStruct((t, d), F32),
        compiler_params=_cparams(("parallel",), 56),
        name="merge_ln",
    )(x2d, *branch_inputs, wg, bg, wb, wo, g, b)


def _ple_ln(x, xb, f, p_ref, plw_ref, pgw_ref, pgb_ref, g_ref, b_ref, alpha):
    ple = jax.nn.sigmoid(_dot(xb, pgw_ref[...]) + pgb_ref[...]) * _dot(p_ref[...].astype(CDT), plw_ref[...])
    return _layer_norm(alpha * x + f + ple, g_ref[...], b_ref[...])


def _ffn_kernel(x_ref, p_ref, wg_ref, wu_ref, wd_ref, plw_ref, pgw_ref, pgb_ref, g_ref, b_ref, o_ref, *, alpha):
    x = x_ref[...]
    xb = x.astype(CDT)

    def chunk(c, acc):
        h = jax.nn.silu(_dot(xb, wg_ref[c])) * _dot(xb, wu_ref[c])
        return acc + _dot(h.astype(CDT), wd_ref[c])

    f = lax.fori_loop(0, wg_ref.shape[0], chunk, jnp.zeros(x.shape, F32))
    o_ref[...] = _ple_ln(x, xb, f, p_ref, plw_ref, pgw_ref, pgb_ref, g_ref, b_ref, alpha)


FFN_CHUNK = 256


def _ffn(x2d, p2d, wg, wu, wd, plw, pgw, pgb, g, b, alpha, tm=512):
    t, d = x2d.shape
    tm = min(tm, t)
    dff = wg.shape[1]
    nck = dff // FFN_CHUNK
    wg3 = wg.reshape(d, nck, FFN_CHUNK).transpose(1, 0, 2).astype(CDT)
    wu3 = wu.reshape(d, nck, FFN_CHUNK).transpose(1, 0, 2).astype(CDT)
    wd3 = wd.reshape(nck, FFN_CHUNK, d).astype(CDT)
    row = lambda i: (i, 0)
    c2 = lambda i: (0, 0)
    c3 = lambda i: (0, 0, 0)
    return pl.pallas_call(
        functools.partial(_ffn_kernel, alpha=alpha),
        grid=(t // tm,),
        in_specs=[pl.BlockSpec((tm, d), row), pl.BlockSpec((tm, p2d.shape[1]), row),
                  pl.BlockSpec(wg3.shape, c3), pl.BlockSpec(wu3.shape, c3), pl.BlockSpec(wd3.shape, c3),
                  pl.BlockSpec(plw.shape, c2), pl.BlockSpec(pgw.shape, c2), pl.BlockSpec(pgb.shape, c2),
                  pl.BlockSpec(g.shape, c2), pl.BlockSpec(b.shape, c2)],
        out_specs=pl.BlockSpec((tm, d), row),
        out_shape=jax.ShapeDtypeStruct((t, d), F32),
        compiler_params=_cparams(("parallel",), 60),
        name="ffn_ple_ln",
    )(x2d, p2d, wg3, wu3, wd3, plw, pgw, pgb, g, b)


def _router_kernel(x_ref, wh_ref, wl_ref, b_ref, comb_ref):
    x = x_ref[...]
    xh = x.astype(CDT)
    xl = (x - xh.astype(F32)).astype(CDT)
    logits = _dot(xh, wh_ref[...]) + _dot(xh, wl_ref[...]) + _dot(xl, wh_ref[...]) + b_ref[...]
    lane = lax.broadcasted_iota(jnp.int32, logits.shape, 1)
    v1 = jnp.max(logits, axis=-1, keepdims=True)
    i1 = jnp.min(jnp.where(logits == v1, lane, LANES), axis=-1, keepdims=True)
    rest = jnp.where(lane == i1, -jnp.inf, logits)
    v2 = jnp.max(rest, axis=-1, keepdims=True)
    i2 = jnp.min(jnp.where(rest == v2, lane, LANES), axis=-1, keepdims=True)
    e2 = jnp.exp(v2 - v1)
    comb_ref[...] = jnp.where(lane == i1, 1.0 / (1.0 + e2), 0.0) + jnp.where(lane == i2, e2 / (1.0 + e2), 0.0)


def _router(x2d, w_router, b_router, tm=1024):
    t, d = x2d.shape
    tm = min(tm, t)
    ne = w_router.shape[1]
    wp = jnp.zeros((d, LANES), F32).at[:, :ne].set(w_router)
    wh = wp.astype(CDT)
    wl = (wp - wh.astype(F32)).astype(CDT)
    bp = jnp.full((1, LANES), -BIG, F32).at[0, :ne].set(b_router)
    return pl.pallas_call(
        _router_kernel,
        grid=(t // tm,),
        in_specs=[pl.BlockSpec((tm, d), lambda i: (i, 0)), pl.BlockSpec(wh.shape, lambda i: (0, 0)),
                  pl.BlockSpec(wl.shape, lambda i: (0, 0)), pl.BlockSpec(bp.shape, lambda i: (0, 0))],
        out_specs=pl.BlockSpec((tm, LANES), lambda i: (i, 0)),
        out_shape=jax.ShapeDtypeStruct((t, LANES), F32),
        compiler_params=_cparams(("parallel",), 32),
        name="moe_router",
    )(x2d, wh, wl, bp)


def _moe_kernel(x_ref, comb_ref, p_ref, wg_ref, wu_ref, wd_ref, plw_ref, pgw_ref, pgb_ref, g_ref, b_ref,
                o_ref, acc_ref, *, alpha):
    e = pl.program_id(1)
    c = pl.program_id(2)
    first = (e == 0) & (c == 0)
    last = (e == pl.num_programs(1) - 1) & (c == pl.num_programs(2) - 1)

    @pl.when(first)
    def _():
        acc_ref[...] = jnp.zeros_like(acc_ref)

    x = x_ref[...]
    xb = x.astype(CDT)
    h = jax.nn.silu(_dot(xb, wg_ref[0])) * _dot(xb, wu_ref[0])
    y = _dot(h.astype(CDT), wd_ref[0])
    lane = lax.broadcasted_iota(jnp.int32, comb_ref.shape, 1)
    cw = jnp.sum(jnp.where(lane == e, comb_ref[...], 0.0), axis=-1, keepdims=True)
    acc_ref[...] += cw * y

    @pl.when(last)
    def _():
        o_ref[...] = _ple_ln(x, xb, acc_ref[...], p_ref, plw_ref, pgw_ref, pgb_ref, g_ref, b_ref, alpha)


MOE_CHUNK = 512


def _moe(x2d, comb, p2d, wg, wu, wd, plw, pgw, pgb, g, b, alpha, tm=1024):
    t, d = x2d.shape
    tm = min(tm, t)
    ne, _, dff = wg.shape
    ck = min(MOE_CHUNK, dff)
    row = lambda i, e, c: (i, 0)
    c2 = lambda i, e, c: (0, 0)
    return pl.pallas_call(
        functools.partial(_moe_kernel, alpha=alpha),
        grid=(t // tm, ne, dff // ck),
        in_specs=[pl.BlockSpec((tm, d), row), pl.BlockSpec((tm, LANES), row), pl.BlockSpec((tm, p2d.shape[1]), row),
                  pl.BlockSpec((1, d, ck), lambda i, e, c: (e, 0, c)),
                  pl.BlockSpec((1, d, ck), lambda i, e, c: (e, 0, c)),
                  pl.BlockSpec((1, ck, d), lambda i, e, c: (e, c, 0)),
                  pl.BlockSpec(plw.shape, c2), pl.BlockSpec(pgw.shape, c2), pl.BlockSpec(pgb.shape, c2),
                  pl.BlockSpec(g.shape, c2), pl.BlockSpec(b.shape, c2)],
        out_specs=pl.BlockSpec((tm, d), row),
        out_shape=jax.ShapeDtypeStruct((t, d), F32),
        scratch_shapes=[pltpu.VMEM((tm, d), F32)],
        compiler_params=_cparams(("parallel", "arbitrary", "arbitrary"), 56),
        name="moe_ple_ln",
    )(x2d, comb, p2d, wg, wu, wd, plw, pgw, pgb, g, b)


def _prep_in_weights(w_in):
    o = COL_OFF
    bw = BRANCH_WIDTH
    cols = lambda n: w_in[:, o[n]:o[n + 1]]
    qa, ka, va = cols(0), cols(1), cols(2)
    w_dil = jnp.concatenate(
        [t[:, g * bw:(g + 1) * bw] for g in range(N_DIL) for t in (qa, ka, va)], axis=1).astype(CDT)
    w_conv = jnp.concatenate([cols(3), cols(4), cols(5)], axis=1).astype(CDT)
    gn = cols(13)
    w_gate = jnp.concatenate([gn[:, br * NSA_Q_HEADS + GQA_COL_HEAD] for br in range(3)], axis=1).astype(CDT)
    w_rest = jnp.concatenate([cols(6)[:, GQA_COL_PERM], cols(14)[:, GQA_COL_PERM], cols(9), cols(10), cols(11),
                              cols(12), cols(15), cols(16), cols(7), cols(8)], axis=1).astype(CDT)
    return w_dil, w_conv, w_gate, w_rest


ZR_Q_NSA, ZR_Q_SWA = 0, 1
ZR_KSC, ZR_VSC, ZR_KWC, ZR_VWC, ZR_KD, ZR_VD = 6, 7, 8, 9, 10, 11
ZR_WIDTH = 2 * BRANCH_WIDTH + 6 * LANES


def _token_mixers(x, w_in, conv_w, cmp_pos, cmp_w1, cmp_b1, cmp_w2, cmp_b2, sinks):
    b, s, d = x.shape
    x2d = x.reshape(b * s, d)
    w_dil, w_conv, w_gate, w_rest = _prep_in_weights(w_in)
    gw = 3 * BRANCH_WIDTH

    (z_dil,) = _linear(x2d, w_dil, [(0, N_DIL * gw)], gw, "in_proj_dil")
    zr, kcc, vcc = _linear(x2d, w_rest, [(0, ZR_WIDTH), (ZR_WIDTH, ZR_WIDTH + LANES),
                                         (ZR_WIDTH + LANES, ZR_WIDTH + 2 * LANES)], 256, "in_proj_rest")
    zr = zr.reshape(b, s, ZR_WIDTH)
    o_b, gates = _conv_gate(x, w_conv, conv_w, w_gate)

    dil_o, dil_lse = [], []
    nblk = N_DIL * 3
    for g, (window, dil) in enumerate(DIL_PATTERNS):
        view = z_dil.reshape(b, s // dil, dil * N_DIL * gw)
        og, lg = _banded(view, view, view, nrep=dil,
                         qcol=lambda r, g=g: r * nblk + 3 * g,
                         kcol=lambda r, g=g: r * nblk + 3 * g + 1,
                         vcol=lambda r, g=g: r * nblk + 3 * g + 2,
                         kw=3 * LANES, window=window // dil, want_lse=True)
        dil_o.append(og.reshape(b * s, BRANCH_WIDTH))
        dil_lse.append(lg.reshape(b * s, BRANCH_WIDTH))

    kc = _compress(kcc.reshape(b, s, LANES), cmp_pos[0], cmp_w1[0], cmp_b1[0], cmp_w2[0], cmp_b2[0])
    vc = _compress(vcc.reshape(b, s, LANES), cmp_pos[1], cmp_w1[1], cmp_b1[1], cmp_w2[1], cmp_b2[1])
    o_cmp, sel = _cmp_select(zr, ZR_Q_NSA, kc, vc)
    o_slc = _slc(zr, ZR_Q_NSA, ZR_KSC, ZR_VSC, sel)
    (o_win,) = _banded(zr, zr, zr, nrep=1, qcol=lambda r: ZR_Q_NSA, kcol=lambda r: ZR_KWC, vcol=lambda r: ZR_VWC,
                       kw=LANES, window=NSA_WINDOW - 1, want_lse=False, tq=512)

    sink_row = sinks.astype(F32)[GQA_COL_HEAD].reshape(1, BRANCH_WIDTH)
    (o_d,) = _banded(zr, zr, zr, nrep=1, qcol=lambda r: ZR_Q_SWA, kcol=lambda r: ZR_KD, vcol=lambda r: ZR_VD,
                     kw=LANES, window=SWA_WINDOW - 1, want_lse=False, sink_row=sink_row)

    t = b * s
    flat = lambda a: a.reshape(t, a.shape[-1])
    return [dil_o[0], dil_o[1], dil_o[2], dil_lse[0], dil_lse[1], dil_lse[2], flat(o_b), flat(o_cmp), flat(o_slc),
            flat(o_win), flat(gates), flat(o_d)]


def kernel(x, p, w_in, conv_w, cmp_pos, cmp_w1, cmp_b1, cmp_w2, cmp_b2, sinks, w_branch, w_merge_gate, b_merge_gate, w_out, ln_mix_g, ln_mix_b, ffn_w_gate, ffn_w_up, ffn_w_down, w_router, b_router, moe_w_gate, moe_w_up, moe_w_down, ple_w, ple_gate_w, ple_gate_b, ln_ffn_g, ln_ffn_b):
    depth, b, s, _ = p.shape
    d = x.shape[-1]
    t = b * s
    alpha = (2 * depth) ** 0.25
    row = lambda v: v.reshape(1, -1).astype(F32)
    for i in range(depth):
        branch_inputs = _token_mixers(x, w_in[i], conv_w[i], cmp_pos[i], cmp_w1[i], cmp_b1[i], cmp_w2[i],
                                      cmp_b2[i], sinks[i])
        wg = jnp.concatenate([w_merge_gate[i, m] for m in range(N_BRANCH)], axis=1).astype(CDT)
        bg = b_merge_gate[i].reshape(1, N_BRANCH * d).astype(F32)
        wb = jnp.stack([w_branch[i, 0], w_branch[i, 1], w_branch[i, 2][GQA_COL_PERM],
                        w_branch[i, 3][GQA_COL_PERM]]).astype(CDT)
        x1 = _merge(x.reshape(t, d), branch_inputs, wg, bg, wb, w_out[i].astype(CDT),
                    row(ln_mix_g[i]), row(ln_mix_b[i]), alpha)
        p2d = p[i].reshape(t, -1)
        ple_args = (ple_w[i].astype(CDT), ple_gate_w[i].astype(CDT), row(ple_gate_b[i]),
                    row(ln_ffn_g[i]), row(ln_ffn_b[i]))
        j = i // 2
        if i % 2 == 0:
            x2 = _ffn(x1, p2d, ffn_w_gate[j], ffn_w_up[j], ffn_w_down[j], *ple_args, alpha)
        else:
            comb = _router(x1, w_router[j], b_router[j])
            x2 = _moe(x1, comb, p2d, moe_w_gate[j].astype(CDT), moe_w_up[j].astype(CDT),
                      moe_w_down[j].astype(CDT), *ple_args, alpha)
        x = x2.reshape(b, s, d)
    return x
```

```python
import functools

import numpy as np
import jax
import jax.numpy as jnp
from jax import lax
from jax.experimental import pallas as pl
from jax.experimental.pallas import tpu as pltpu

D_MODEL = 1024
HEAD_DIM = 64
DIL_PATTERNS = ((128, 1), (512, 4), (2048, 16))
N_DIL = 3
DIL_HEADS = 6
CONV_WIDTH = 384
CONV_K = 3
NSA_Q_HEADS = 6
NSA_KV_HEADS = 2
CMP_BLOCK = 32
CMP_STRIDE = 16
CMP_HIDDEN = 128
SEL_BLOCK = 64
N_SEL = 16
NSA_WINDOW = 512
SWA_Q_HEADS = 6
SWA_WINDOW = 128
BRANCH_WIDTH = 384
N_BRANCH = 4
N_EXPERTS = 8
LN_EPS = 1e-5
NEG_INF = -1e30
DIL_WIDTH = N_DIL * DIL_HEADS * HEAD_DIM
COLUMN_SIZES = (DIL_WIDTH, DIL_WIDTH, DIL_WIDTH, CONV_WIDTH, CONV_WIDTH, CONV_WIDTH,
                NSA_Q_HEADS * HEAD_DIM, 128, 128, 128, 128, 128, 128, 3 * NSA_Q_HEADS,
                SWA_Q_HEADS * HEAD_DIM, 128, 128)
COL_OFF = np.concatenate([[0], np.cumsum(COLUMN_SIZES)]).tolist()

LANES = 128
V7X_VMEM_BYTES = 64 * 1024 * 1024
MIB = 1024 * 1024

CDT = jnp.bfloat16
F32 = jnp.float32
QK_SCALE = HEAD_DIM ** -0.5
SUB_Q = 128
BIG = 1e30

_GQA_HEAD_ORDER = (0, 3, 1, 4, 2, 5)
GQA_COL_PERM = np.concatenate([np.arange(h * HEAD_DIM, (h + 1) * HEAD_DIM) for h in _GQA_HEAD_ORDER])
GQA_COL_HEAD = GQA_COL_PERM // HEAD_DIM


def _cparams(sem, vmem_mib):
    return pltpu.CompilerParams(dimension_semantics=sem, vmem_limit_bytes=int(vmem_mib * MIB))


def _nt_dot(a, b):
    return lax.dot_general(a, b, (((1,), (1,)), ((), ())), preferred_element_type=F32)


def _dot(a, b):
    return jnp.dot(a, b, preferred_element_type=F32)


def _layer_norm(r, g, b):
    mu = jnp.mean(r, axis=-1, keepdims=True)
    d = r - mu
    var = jnp.mean(d * d, axis=-1, keepdims=True)
    return d * lax.rsqrt(var + LN_EPS) * g + b


def _half_masks():
    lane = lax.broadcasted_iota(jnp.int32, (1, LANES), 1)
    return lane < HEAD_DIM


def _linear_kernel(x_ref, w_ref, *o_refs, splits, n_chunk):
    xb = x_ref[...].astype(CDT)
    for o_ref, (c0, c1) in zip(o_refs, splits):
        for a in range(c0, c1, n_chunk):
            b = min(a + n_chunk, c1)
            o_ref[:, a - c0:b - c0] = _dot(xb, w_ref[:, a:b]).astype(o_ref.dtype)


def _linear(x2d, w, splits, n_chunk, name, tm=512):
    t, k = x2d.shape
    tm = min(tm, t)
    n = w.shape[1]
    outs = [jax.ShapeDtypeStruct((t, c1 - c0), CDT) for c0, c1 in splits]
    return pl.pallas_call(
        functools.partial(_linear_kernel, splits=tuple(splits), n_chunk=n_chunk),
        grid=(t // tm,),
        in_specs=[pl.BlockSpec((tm, k), lambda i: (i, 0)),
                  pl.BlockSpec((k, n), lambda i: (0, 0))],
        out_specs=[pl.BlockSpec((tm, c1 - c0), lambda i: (i, 0)) for c0, c1 in splits],
        out_shape=outs,
        compiler_params=_cparams(("parallel",), 48),
        name=name,
    )(x2d, w)


def _conv_gate_kernel(x_ref, xh_ref, wc_ref, cw_ref, wg_ref, ob_ref, g_ref, *, tm):
    i = pl.program_id(1)
    w = CONV_WIDTH
    xb = x_ref[...].astype(CDT)
    z = _dot(xb, wc_ref[...])
    u = z[:, w:2 * w] * z[:, 2 * w:3 * w]
    zh = _dot(xh_ref[...].astype(CDT), wc_ref[:, w:3 * w])
    uh = zh[:, :w] * zh[:, w:]
    uh = jnp.where(i == 0, 0.0, uh)
    row = lax.broadcasted_iota(jnp.int32, (tm, w), 0)
    u1 = jnp.where(row == 0, uh[7:8, :], pltpu.roll(u, 1, 0))
    u2 = jnp.where(row == 0, uh[6:7, :], jnp.where(row == 1, uh[7:8, :], pltpu.roll(u, 2, 0)))
    y = cw_ref[0:1, :] * u2 + cw_ref[1:2, :] * u1 + cw_ref[2:3, :] * u
    ob_ref[...] = (z[:, :w] * y).astype(ob_ref.dtype)
    g_ref[...] = jax.nn.sigmoid(_dot(xb, wg_ref[...]))


def _conv_gate(x, wc, conv_w, wg, tm=512):
    b, s, d = x.shape
    tm = min(tm, s)
    hb = tm // 8
    return pl.pallas_call(
        functools.partial(_conv_gate_kernel, tm=tm),
        grid=(b, s // tm),
        in_specs=[pl.BlockSpec((None, tm, d), lambda bi, i: (bi, i, 0)),
                  pl.BlockSpec((None, 8, d), lambda bi, i: (bi, jnp.maximum(i * hb - 1, 0), 0)),
                  pl.BlockSpec(wc.shape, lambda bi, i: (0, 0)),
                  pl.BlockSpec(conv_w.shape, lambda bi, i: (0, 0)),
                  pl.BlockSpec(wg.shape, lambda bi, i: (0, 0))],
        out_specs=[pl.BlockSpec((None, tm, CONV_WIDTH), lambda bi, i: (bi, i, 0)),
                   pl.BlockSpec((None, tm, wg.shape[1]), lambda bi, i: (bi, i, 0))],
        out_shape=[jax.ShapeDtypeStruct((b, s, CONV_WIDTH), CDT),
                   jax.ShapeDtypeStruct((b, s, wg.shape[1]), F32)],
        compiler_params=_cparams(("parallel", "parallel"), 48),
        name="conv_gate",
    )(x, x, wc, conv_w, wg)


def _banded_kernel(*refs, window, pr, tq, kw, want_lse, has_sink):
    q_ref, kp_ref, kc_ref, vp_ref, vc_ref = refs[:5]
    n = 5
    sink_ref = None
    if has_sink:
        sink_ref = refs[n]
        n += 1
    o_ref = refs[n]
    n += 1
    lse_ref = None
    if want_lse:
        lse_ref = refs[n]
        n += 1
    kbuf, vbuf = refs[n], refs[n + 1]

    i = pl.program_id(2)
    kbuf[0:pr, :] = kp_ref[...]
    kbuf[pr:pr + tq, :] = kc_ref[...]
    vbuf[0:pr, :] = vp_ref[...]
    vbuf[pr:pr + tq, :] = vc_ref[...]

    span = SUB_Q + pr
    qi = lax.broadcasted_iota(jnp.int32, (SUB_Q, span), 0)
    kj = lax.broadcasted_iota(jnp.int32, (SUB_Q, span), 1)
    dist = pr + qi - kj
    band = (dist >= 0) & (dist <= window)
    lo = _half_masks()
    for sb in range(tq // SUB_Q):
        r0 = sb * SUB_Q
        mask = band & (i * tq + r0 - pr + kj >= 0)
        for p in range(3):
            c0 = p * LANES
            kc0 = c0 if kw == 3 * LANES else 0
            qp = q_ref[r0:r0 + SUB_Q, c0:c0 + LANES]
            kk = kbuf[r0:r0 + span, kc0:kc0 + LANES]
            vv = vbuf[r0:r0 + span, kc0:kc0 + LANES]
            o_half, lse_half = [], []
            for hm in (lo, jnp.logical_not(lo)):
                qm = (jnp.where(hm, qp, 0) * QK_SCALE).astype(CDT)
                s = jnp.where(mask, _nt_dot(qm, kk), NEG_INF)
                m = jnp.max(s, axis=-1, keepdims=True)
                e = jnp.exp(s - m)
                l = jnp.sum(e, axis=-1, keepdims=True)
                o_half.append(_dot(e.astype(CDT), vv) / l)
                lse_half.append(m + jnp.log(l))
            o_pair = jnp.where(lo, o_half[0], o_half[1])
            lse_pair = jnp.where(lo, lse_half[0], lse_half[1])
            if has_sink:
                o_pair = o_pair * jax.nn.sigmoid(lse_pair - sink_ref[:, c0:c0 + LANES])
            o_ref[r0:r0 + SUB_Q, c0:c0 + LANES] = o_pair.astype(o_ref.dtype)
            if want_lse:
                lse_ref[r0:r0 + SUB_Q, c0:c0 + LANES] = lse_pair


def _banded(qa, ka, va, *, nrep, qcol, kcol, vcol, kw, window, want_lse, sink_row=None, tq=256):
    b, l, _ = qa.shape
    pr = -(-window // SUB_Q) * SUB_Q
    tq = min(max(tq, pr), l)
    assert tq % pr == 0 and l % tq == 0, (tq, pr, l)
    ratio = tq // pr
    qw = 3 * LANES
    in_specs = [
        pl.BlockSpec((None, tq, qw), lambda bi, r, i: (bi, i, qcol(r))),
        pl.BlockSpec((None, pr, kw), lambda bi, r, i: (bi, jnp.maximum(i * ratio - 1, 0), kcol(r))),
        pl.BlockSpec((None, tq, kw), lambda bi, r, i: (bi, i, kcol(r))),
        pl.BlockSpec((None, pr, kw), lambda bi, r, i: (bi, jnp.maximum(i * ratio - 1, 0), vcol(r))),
        pl.BlockSpec((None, tq, kw), lambda bi, r, i: (bi, i, vcol(r))),
    ]
    args = [qa, ka, ka, va, va]
    if sink_row is not None:
        in_specs.append(pl.BlockSpec(sink_row.shape, lambda bi, r, i: (0, 0)))
        args.append(sink_row)
    out_specs = [pl.BlockSpec((None, tq, qw), lambda bi, r, i: (bi, i, r))]
    out_shape = [jax.ShapeDtypeStruct((b, l, nrep * qw), CDT)]
    if want_lse:
        out_specs.append(pl.BlockSpec((None, tq, qw), lambda bi, r, i: (bi, i, r)))
        out_shape.append(jax.ShapeDtypeStruct((b, l, nrep * qw), F32))
    res = pl.pallas_call(
        functools.partial(_banded_kernel, window=window, pr=pr, tq=tq, kw=kw, want_lse=want_lse,
                          has_sink=sink_row is not None),
        grid=(b, nrep, l // tq),
        in_specs=in_specs,
        out_specs=out_specs,
        out_shape=out_shape,
        scratch_shapes=[pltpu.VMEM((pr + tq, kw), ka.dtype), pltpu.VMEM((pr + tq, kw), va.dtype)],
        compiler_params=_cparams(("parallel", "parallel", "parallel"), 32),
        name=f"banded_w{window}_k{kw}_r{nrep}",
    )(*args)
    return res


def _gelu_tanh(x):
    return 0.5 * x * (1.0 + jnp.tanh(0.7978845608028654 * (x + 0.044715 * (x * x * x))))


def _compress_kernel(x_ref, pa_ref, pb_ref, w1a_ref, w1b_ref, b1_ref, w2_ref, b2_ref, o_ref):
    x = x_ref[...].astype(F32)
    n = x.shape[0]
    a = _dot((x + pa_ref[...]).astype(CDT), w1a_ref[...])
    bm = _dot((x + pb_ref[...]).astype(CDT), w1b_ref[...])
    h = a + pltpu.roll(bm, n - 1, 0) + b1_ref[...]
    o_ref[...] = (_dot(_gelu_tanh(h).astype(CDT), w2_ref[...]) + b2_ref[...]).astype(o_ref.dtype)


def _compress(t, pos, w1, b1, w2, b2):
    b, s, _ = t.shape
    nch = s // CMP_STRIDE
    xw = CMP_STRIDE * LANES
    x = t.reshape(b, nch, xw)
    eye = jnp.eye(NSA_KV_HEADS, dtype=F32)
    w1r = w1.reshape(CMP_BLOCK, HEAD_DIM, CMP_HIDDEN)

    def expand_w1(part):
        return jnp.einsum('tdj,kl->tkdlj', part, eye).reshape(xw, NSA_KV_HEADS * CMP_HIDDEN).astype(CDT)

    def expand_pos(part):
        return jnp.broadcast_to(part[:, None, :], (CMP_STRIDE, NSA_KV_HEADS, HEAD_DIM)).reshape(1, xw)

    w1a, w1b = expand_w1(w1r[:CMP_STRIDE]), expand_w1(w1r[CMP_STRIDE:])
    pa, pb = expand_pos(pos[:CMP_STRIDE]), expand_pos(pos[CMP_STRIDE:])
    b1e = jnp.tile(b1, NSA_KV_HEADS).reshape(1, -1)
    w2e = jnp.einsum('jd,kl->kjld', w2, eye).reshape(NSA_KV_HEADS * CMP_HIDDEN, LANES).astype(CDT)
    b2e = jnp.tile(b2, NSA_KV_HEADS).reshape(1, -1)
    consts = [pa, pb, w1a, w1b, b1e, w2e, b2e]
    return pl.pallas_call(
        _compress_kernel,
        grid=(b,),
        in_specs=[pl.BlockSpec((None, nch, xw), lambda bi: (bi, 0, 0))]
        + [pl.BlockSpec(c.shape, lambda bi: (0, 0)) for c in consts],
        out_specs=pl.BlockSpec((None, nch, LANES), lambda bi: (bi, 0, 0)),
        out_shape=jax.ShapeDtypeStruct((b, nch, LANES), CDT),
        compiler_params=_cparams(("parallel",), 48),
        name="nsa_compress",
    )(x, *consts)


def _cmp_select_kernel(q_ref, kc_ref, vc_ref, ov_ref, o_ref, sel_ref, *, tq, n_sel):
    i = pl.program_id(1)
    ncp = kc_ref.shape[0]
    ns = ov_ref.shape[1]
    lo = _half_masks()
    t_col = i * tq + lax.broadcasted_iota(jnp.int32, (tq, 1), 0)
    c_end = lax.broadcasted_iota(jnp.int32, (tq, ncp), 1) * CMP_STRIDE + (CMP_BLOCK - 1)
    vis = c_end <= t_col
    blk = lax.broadcasted_iota(jnp.int32, (tq, ns), 1)
    blk_t = lax.broadcasted_iota(jnp.int32, (ns, tq), 0)
    cur = t_col // SEL_BLOCK
    causal = blk <= cur
    forced = (blk == 0) | (blk == cur) | (blk == cur - 1)
    kc = kc_ref[...]
    vc = vc_ref[...]
    o_kv = []
    for kv, hm in enumerate((lo, jnp.logical_not(lo))):
        psum = jnp.zeros((tq, ncp), F32)
        o_p = []
        for p in range(3):
            qm = (jnp.where(hm, q_ref[:, p * LANES:(p + 1) * LANES], 0) * QK_SCALE).astype(CDT)
            s = jnp.where(vis, _nt_dot(qm, kc), NEG_INF)
            m = jnp.max(s, axis=-1, keepdims=True)
            e = jnp.where(vis, jnp.exp(s - m), 0.0)
            pn = e / jnp.maximum(jnp.sum(e, axis=-1, keepdims=True), 1e-30)
            o_p.append(_dot(pn.astype(CDT), vc))
            psum = psum + pn
        o_kv.append(o_p)
        p_hi = psum.astype(CDT)
        p_lo = (psum - p_hi.astype(F32)).astype(CDT)
        imp = _dot(p_hi, ov_ref[...]) + _dot(p_lo, ov_ref[...])
        work_t = jnp.where(causal & jnp.logical_not(forced), imp, -BIG).T

        def pick(_, carry):
            work, sel = carry
            m = jnp.max(work, axis=0, keepdims=True)
            idx = jnp.min(jnp.where(work == m, blk_t, ns), axis=0, keepdims=True)
            hit = blk_t == idx
            sel = jnp.where(hit & (m > -BIG), 1.0, sel)
            return jnp.where(hit, -2.0 * BIG, work), sel

        _, sel_t = lax.fori_loop(0, n_sel - 3, pick, (work_t, jnp.zeros((ns, tq), F32)))
        sel = jnp.where(forced, 1.0, sel_t.T)
        sel_ref[:, kv * ns:(kv + 1) * ns] = sel.astype(sel_ref.dtype)
    for p in range(3):
        o_ref[:, p * LANES:(p + 1) * LANES] = jnp.where(lo, o_kv[0][p], o_kv[1][p]).astype(o_ref.dtype)


def _cmp_select(zr, qcol, kc, vc, tq=128):
    b, s, _ = zr.shape
    ncp = kc.shape[1]
    ns = s // SEL_BLOCK
    n_sel = min(N_SEL, ns)
    assert n_sel >= 3, "selection needs room for the three forced blocks"
    tq = min(tq, s)
    c = np.arange(ncp)[:, None] * CMP_STRIDE
    j = np.arange(ns)[None, :] * SEL_BLOCK
    overlap = ((c < j + SEL_BLOCK) & (c + CMP_BLOCK - 1 >= j)).astype(np.float32)
    overlap[ncp - 1:, :] = 0.0
    ov = jnp.asarray(overlap, CDT)
    qw = 3 * LANES
    return pl.pallas_call(
        functools.partial(_cmp_select_kernel, tq=tq, n_sel=n_sel),
        grid=(b, s // tq),
        in_specs=[pl.BlockSpec((None, tq, qw), lambda bi, i: (bi, i, qcol)),
                  pl.BlockSpec((None, ncp, LANES), lambda bi, i: (bi, 0, 0)),
                  pl.BlockSpec((None, ncp, LANES), lambda bi, i: (bi, 0, 0)),
                  pl.BlockSpec(ov.shape, lambda bi, i: (0, 0))],
        out_specs=[pl.BlockSpec((None, tq, qw), lambda bi, i: (bi, i, 0)),
                   pl.BlockSpec((None, tq, 2 * ns), lambda bi, i: (bi, i, 0))],
        out_shape=[jax.ShapeDtypeStruct((b, s, qw), CDT),
                   jax.ShapeDtypeStruct((b, s, 2 * ns), CDT)],
        compiler_params=_cparams(("parallel", "parallel"), 48),
        name="nsa_cmp_select",
    )(zr, kc, vc, ov)


def _slc_kernel(q_ref, k_ref, v_ref, sel_ref, o_ref, *, tq, tk):
    i = pl.program_id(1)
    ns = sel_ref.shape[1] // 2
    bpt = tk // SEL_BLOCK
    lo = _half_masks()
    t_col = i * tq + lax.broadcasted_iota(jnp.int32, (tq, 1), 0)
    kcol = lax.broadcasted_iota(jnp.int32, (tq, tk), 1)
    e_row = lax.broadcasted_iota(jnp.int32, (ns, tk), 0)
    e_col = lax.broadcasted_iota(jnp.int32, (ns, tk), 1) // SEL_BLOCK
    n_tiles = ((i + 1) * tq + tk - 1) // tk
    o_kv = []
    for kv, hm in enumerate((lo, jnp.logical_not(lo))):
        qs = jnp.concatenate(
            [(jnp.where(hm, q_ref[:, p * LANES:(p + 1) * LANES], 0) * QK_SCALE).astype(CDT) for p in range(3)],
            axis=0)
        sel = sel_ref[:, kv * ns:(kv + 1) * ns]

        def step(j, carry):
            m, l, acc = carry
            k0 = pl.multiple_of(j * tk, tk)
            kt = k_ref[pl.ds(k0, tk), :]
            vt = v_ref[pl.ds(k0, tk), :]
            expand = (e_row == e_col + j * bpt).astype(CDT)
            allowed = (_dot(sel, expand) > 0.5) & (kcol + j * tk <= t_col)
            bias = jnp.where(allowed, 0.0, NEG_INF)
            s = _nt_dot(qs, kt) + jnp.concatenate([bias, bias, bias], axis=0)
            m_new = jnp.maximum(m, jnp.max(s, axis=-1, keepdims=True))
            alpha = jnp.exp(m - m_new)
            e = jnp.exp(s - m_new)
            l = alpha * l + jnp.sum(e, axis=-1, keepdims=True)
            acc = alpha * acc + _dot(e.astype(CDT), vt)
            return m_new, l, acc

        init = (jnp.full((3 * tq, 1), NEG_INF, F32), jnp.zeros((3 * tq, 1), F32), jnp.zeros((3 * tq, LANES), F32))
        m, l, acc = lax.fori_loop(0, n_tiles, step, init)
        o_kv.append(acc / l)
    for p in range(3):
        o_ref[:, p * LANES:(p + 1) * LANES] = jnp.where(
            lo, o_kv[0][p * tq:(p + 1) * tq], o_kv[1][p * tq:(p + 1) * tq]).astype(o_ref.dtype)


def _slc(zr, qcol, kcol, vcol, sel, tq=256, tk=512):
    b, s, _ = zr.shape
    tq = min(tq, s)
    tk = min(tk, s)
    qw = 3 * LANES
    ns2 = sel.shape[2]
    return pl.pallas_call(
        functools.partial(_slc_kernel, tq=tq, tk=tk),
        grid=(b, s // tq),
        in_specs=[pl.BlockSpec((None, tq, qw), lambda bi, i: (bi, i, qcol)),
                  pl.BlockSpec((None, s, LANES), lambda bi, i: (bi, 0, kcol)),
                  pl.BlockSpec((None, s, LANES), lambda bi, i: (bi, 0, vcol)),
                  pl.BlockSpec((None, tq, ns2), lambda bi, i: (bi, i, 0))],
        out_specs=pl.BlockSpec((None, tq, qw), lambda bi, i: (bi, i, 0)),
        out_shape=jax.ShapeDtypeStruct((b, s, qw), CDT),
        compiler_params=_cparams(("parallel", "arbitrary"), 48),
        name="nsa_slc",
    )(zr, zr, zr, sel)


def _merge_kernel(x_ref, oa0, oa1, oa2, la0, la1, la2, ob, ocmp, oslc, owin, gates, od,
                  wg_ref, bg_ref, wb_ref, wo_ref, g_ref, b_ref, o_ref, *, alpha):
    x = x_ref[...]
    xb = x.astype(CDT)
    bw = BRANCH_WIDTH
    l0, l1, l2 = la0[...], la1[...], la2[...]
    mx = jnp.maximum(jnp.maximum(l0, l1), l2)
    w0, w1, w2 = jnp.exp(l0 - mx), jnp.exp(l1 - mx), jnp.exp(l2 - mx)
    o_a = (w0 * oa0[...].astype(F32) + w1 * oa1[...].astype(F32) + w2 * oa2[...].astype(F32)) / (w0 + w1 + w2)
    o_c = (gates[:, 0:bw] * ocmp[...].astype(F32) + gates[:, bw:2 * bw] * oslc[...].astype(F32)
           + gates[:, 2 * bw:3 * bw] * owin[...].astype(F32))
    branches = (o_a.astype(CDT), ob[...], o_c.astype(CDT), od[...])
    d = x.shape[1]
    merged = jnp.zeros(x.shape, F32)
    for m in range(N_BRANCH):
        gate = jax.nn.sigmoid(_dot(xb, wg_ref[:, m * d:(m + 1) * d]) + bg_ref[:, m * d:(m + 1) * d])
        merged = merged + gate * _dot(branches[m], wb_ref[m])
    r = alpha * x + _dot(merged.astype(CDT), wo_ref[...])
    o_ref[...] = _layer_norm(r, g_ref[...], b_ref[...])


def _merge(x2d, branch_inputs, wg, bg, wb, wo, g, b, alpha, tm=256):
    t, d = x2d.shape
    tm = min(tm, t)
    row = lambda i: (i, 0)
    const2 = lambda i: (0, 0)
    in_specs = [pl.BlockSpec((tm, d), row)]
    in_specs += [pl.BlockSpec((tm, a.shape[1]), row) for a in branch_inputs]
    in_specs += [pl.BlockSpec(wg.shape, const2), pl.BlockSpec(bg.shape, const2),
                 pl.BlockSpec(wb.shape, lambda i: (0, 0, 0)), pl.BlockSpec(wo.shape, const2),
                 pl.BlockSpec(g.shape, const2), pl.BlockSpec(b.shape, const2)]
    return pl.pallas_call(
        functools.partial(_merge_kernel, alpha=alpha),
        grid=(t // tm,),
        in_specs=in_specs,
        out_specs=pl.BlockSpec((tm, d), row),
        out_shape=jax.ShapeDtypeStruct((t, d), F32),
        compiler_params=_cparams(("parallel",), 56),
        name="merge_ln",
    )(x2d, *branch_inputs, wg, bg, wb, wo, g, b)


def _ple_ln(x, xb, f, p_ref, plw_ref, pgw_ref, pgb_ref, g_ref, b_ref, alpha):
    ple = jax.nn.sigmoid(_dot(xb, pgw_ref[...]) + pgb_ref[...]) * _dot(p_ref[...].astype(CDT), plw_ref[...])
    return _layer_norm(alpha * x + f + ple, g_ref[...], b_ref[...])


def _ffn_kernel(x_ref, p_ref, wg_ref, wu_ref, wd_ref, plw_ref, pgw_ref, pgb_ref, g_ref, b_ref, o_ref, *, alpha):
    x = x_ref[...]
    xb = x.astype(CDT)

    def chunk(c, acc):
        h = jax.nn.silu(_dot(xb, wg_ref[c])) * _dot(xb, wu_ref[c])
        return acc + _dot(h.astype(CDT), wd_ref[c])

    f = lax.fori_loop(0, wg_ref.shape[0], chunk, jnp.zeros(x.shape, F32))
    o_ref[...] = _ple_ln(x, xb, f, p_ref, plw_ref, pgw_ref, pgb_ref, g_ref, b_ref, alpha)


FFN_CHUNK = 256


def _ffn(x2d, p2d, wg, wu, wd, plw, pgw, pgb, g, b, alpha, tm=512):
    t, d = x2d.shape
    tm = min(tm, t)
    dff = wg.shape[1]
    nck = dff // FFN_CHUNK
    wg3 = wg.reshape(d, nck, FFN_CHUNK).transpose(1, 0, 2).astype(CDT)
    wu3 = wu.reshape(d, nck, FFN_CHUNK).transpose(1, 0, 2).astype(CDT)
    wd3 = wd.reshape(nck, FFN_CHUNK, d).astype(CDT)
    row = lambda i: (i, 0)
    c2 = lambda i: (0, 0)
    c3 = lambda i: (0, 0, 0)
    return pl.pallas_call(
        functools.partial(_ffn_kernel, alpha=alpha),
        grid=(t // tm,),
        in_specs=[pl.BlockSpec((tm, d), row), pl.BlockSpec((tm, p2d.shape[1]), row),
                  pl.BlockSpec(wg3.shape, c3), pl.BlockSpec(wu3.shape, c3), pl.BlockSpec(wd3.shape, c3),
                  pl.BlockSpec(plw.shape, c2), pl.BlockSpec(pgw.shape, c2), pl.BlockSpec(pgb.shape, c2),
                  pl.BlockSpec(g.shape, c2), pl.BlockSpec(b.shape, c2)],
        out_specs=pl.BlockSpec((tm, d), row),
        out_shape=jax.ShapeDtypeStruct((t, d), F32),
        compiler_params=_cparams(("parallel",), 60),
        name="ffn_ple_ln",
    )(x2d, p2d, wg3, wu3, wd3, plw, pgw, pgb, g, b)


def _router_kernel(x_ref, wh_ref, wl_ref, b_ref, comb_ref, rank_ref, rank_t_ref, cnt_ref):
    x = x_ref[...]
    xh = x.astype(CDT)
    xl = (x - xh.astype(F32)).astype(CDT)
    logits = _dot(xh, wh_ref[...]) + _dot(xh, wl_ref[...]) + _dot(xl, wh_ref[...]) + b_ref[...]
    lane = lax.broadcasted_iota(jnp.int32, logits.shape, 1)
    v1 = jnp.max(logits, axis=-1, keepdims=True)
    i1 = jnp.min(jnp.where(logits == v1, lane, LANES), axis=-1, keepdims=True)
    rest = jnp.where(lane == i1, -jnp.inf, logits)
    v2 = jnp.max(rest, axis=-1, keepdims=True)
    i2 = jnp.min(jnp.where(rest == v2, lane, LANES), axis=-1, keepdims=True)
    e2 = jnp.exp(v2 - v1)
    comb_ref[...] = jnp.where(lane == i1, 1.0 / (1.0 + e2), 0.0) + jnp.where(lane == i2, e2 / (1.0 + e2), 0.0)
    routed = (lane == i1) | (lane == i2)
    mask = routed.astype(CDT)
    tm = x.shape[0]
    before = (lax.broadcasted_iota(jnp.int32, (tm, tm), 1) < lax.broadcasted_iota(jnp.int32, (tm, tm), 0)).astype(CDT)
    rank = jnp.where(routed, _dot(before, mask), -1.0)
    rank_ref[...] = rank
    rank_t_ref[...] = rank.T[0:rank_t_ref.shape[0], :]
    cnt_ref[...] = jnp.sum(routed.astype(F32), axis=0, keepdims=True).astype(jnp.int32)


def _router(x2d, w_router, b_router, tm):
    t, d = x2d.shape
    ne = w_router.shape[1]
    wp = jnp.zeros((d, LANES), F32).at[:, :ne].set(w_router)
    wh = wp.astype(CDT)
    wl = (wp - wh.astype(F32)).astype(CDT)
    bp = jnp.full((1, LANES), -BIG, F32).at[0, :ne].set(b_router)
    nt = t // tm
    row = lambda i: (i, 0)
    return pl.pallas_call(
        _router_kernel,
        grid=(nt,),
        in_specs=[pl.BlockSpec((tm, d), row), pl.BlockSpec(wh.shape, lambda i: (0, 0)),
                  pl.BlockSpec(wl.shape, lambda i: (0, 0)), pl.BlockSpec(bp.shape, lambda i: (0, 0))],
        out_specs=[pl.BlockSpec((tm, LANES), row), pl.BlockSpec((tm, LANES), row),
                   pl.BlockSpec((None, 8, tm), lambda i: (i, 0, 0)),
                   pl.BlockSpec((None, 1, LANES), lambda i: (i, 0, 0))],
        out_shape=[jax.ShapeDtypeStruct((t, LANES), F32), jax.ShapeDtypeStruct((t, LANES), F32),
                   jax.ShapeDtypeStruct((nt, 8, tm), F32), jax.ShapeDtypeStruct((nt, 1, LANES), jnp.int32)],
        compiler_params=_cparams(("parallel",), 40),
        name="moe_router",
    )(x2d, wh, wl, bp)


def _moe_kernel(cnt_ref, x_ref, comb_ref, rank_ref, rank_t_ref, p_ref, wg_ref, wu_ref, wd_ref, plw_ref, pgw_ref,
                pgb_ref, g_ref, b_ref, o_ref, xe_ref, ye_ref, *, alpha, rs):
    i = pl.program_id(0)
    e = pl.program_id(1)
    c = pl.program_id(2)
    tm = x_ref.shape[0]
    n_groups = (cnt_ref[i * LANES + e] + rs - 1) // rs

    def rows(sc):
        return pl.ds(pl.multiple_of(sc * rs, 8), rs)

    @pl.when((e == 0) & (c == 0))
    def _():
        o_ref[...] = jnp.zeros_like(o_ref)

    @pl.when(c == 0)
    def _():
        xb = x_ref[...].astype(CDT)
        rank_row = rank_t_ref[pl.ds(e, 1), :]
        row_id = lax.broadcasted_iota(jnp.int32, (rs, tm), 0).astype(F32)

        def gather(sc, _):
            onehot = (rank_row - (sc * rs).astype(F32) == row_id).astype(CDT)
            xe_ref[rows(sc), :] = _dot(onehot, xb).astype(CDT)
            ye_ref[rows(sc), :] = jnp.zeros((rs, ye_ref.shape[1]), F32)
            return 0

        lax.fori_loop(0, n_groups, gather, 0)

    def expert(sc, _):
        xs = xe_ref[rows(sc), :]
        h = jax.nn.silu(_dot(xs, wg_ref[0])) * _dot(xs, wu_ref[0])
        ye_ref[rows(sc), :] += _dot(h.astype(CDT), wd_ref[0])
        return 0

    lax.fori_loop(0, n_groups, expert, 0)

    @pl.when(c == pl.num_programs(2) - 1)
    def _():
        lane = lax.broadcasted_iota(jnp.int32, comb_ref.shape, 1)
        mine = lane == e
        cw = jnp.sum(jnp.where(mine, comb_ref[...], 0.0), axis=-1, keepdims=True)
        rank_col = jnp.sum(jnp.where(mine, rank_ref[...], 0.0), axis=-1, keepdims=True)
        col_id = lax.broadcasted_iota(jnp.int32, (tm, rs), 1).astype(F32)

        def scatter(sc, _):
            onehot = (rank_col - (sc * rs).astype(F32) == col_id).astype(CDT)
            o_ref[...] += cw * _dot(onehot, ye_ref[rows(sc), :].astype(CDT))
            return 0

        lax.fori_loop(0, n_groups, scatter, 0)

    @pl.when((e == pl.num_programs(1) - 1) & (c == pl.num_programs(2) - 1))
    def _():
        x = x_ref[...]
        o_ref[...] = _ple_ln(x, x.astype(CDT), o_ref[...], p_ref, plw_ref, pgw_ref, pgb_ref, g_ref, b_ref, alpha)


MOE_CHUNK = 512
MOE_TILE = 1024
MOE_ROW_GROUP = 320


def _moe(x2d, routing, p2d, wg, wu, wd, plw, pgw, pgb, g, b, alpha, tm):
    comb, rank, rank_t, cnt = routing
    t, d = x2d.shape
    ne, _, dff = wg.shape
    ck = min(MOE_CHUNK, dff)
    rs = min(MOE_ROW_GROUP, tm)
    max_rows = -(-tm // rs) * rs
    row = lambda i, e, c, cnt: (i, 0)
    c2 = lambda i, e, c, cnt: (0, 0)
    grid_spec = pltpu.PrefetchScalarGridSpec(
        num_scalar_prefetch=1,
        grid=(t // tm, ne, dff // ck),
        in_specs=[pl.BlockSpec((tm, d), row), pl.BlockSpec((tm, LANES), row), pl.BlockSpec((tm, LANES), row),
                  pl.BlockSpec((None, 8, tm), lambda i, e, c, cnt: (i, 0, 0)),
                  pl.BlockSpec((tm, p2d.shape[1]), row),
                  pl.BlockSpec((1, d, ck), lambda i, e, c, cnt: (e, 0, c)),
                  pl.BlockSpec((1, d, ck), lambda i, e, c, cnt: (e, 0, c)),
                  pl.BlockSpec((1, ck, d), lambda i, e, c, cnt: (e, c, 0)),
                  pl.BlockSpec(plw.shape, c2), pl.BlockSpec(pgw.shape, c2), pl.BlockSpec(pgb.shape, c2),
                  pl.BlockSpec(g.shape, c2), pl.BlockSpec(b.shape, c2)],
        out_specs=pl.BlockSpec((tm, d), row),
        scratch_shapes=[pltpu.VMEM((max_rows, d), CDT), pltpu.VMEM((max_rows, d), F32)],
    )
    return pl.pallas_call(
        functools.partial(_moe_kernel, alpha=alpha, rs=rs),
        grid_spec=grid_spec,
        out_shape=jax.ShapeDtypeStruct((t, d), F32),
        compiler_params=_cparams(("parallel", "arbitrary", "arbitrary"), 56),
        name="moe_ple_ln",
    )(cnt.reshape(-1), x2d, comb, rank, rank_t, p2d, wg, wu, wd, plw, pgw, pgb, g, b)


def _prep_in_weights(w_in):
    o = COL_OFF
    bw = BRANCH_WIDTH
    cols = lambda n: w_in[:, o[n]:o[n + 1]]
    qa, ka, va = cols(0), cols(1), cols(2)
    w_dil = jnp.concatenate(
        [t[:, g * bw:(g + 1) * bw] for g in range(N_DIL) for t in (qa, ka, va)], axis=1).astype(CDT)
    w_conv = jnp.concatenate([cols(3), cols(4), cols(5)], axis=1).astype(CDT)
    gn = cols(13)
    w_gate = jnp.concatenate([gn[:, br * NSA_Q_HEADS + GQA_COL_HEAD] for br in range(3)], axis=1).astype(CDT)
    w_rest = jnp.concatenate([cols(6)[:, GQA_COL_PERM], cols(14)[:, GQA_COL_PERM], cols(9), cols(10), cols(11),
                              cols(12), cols(15), cols(16), cols(7), cols(8)], axis=1).astype(CDT)
    return w_dil, w_conv, w_gate, w_rest


ZR_Q_NSA, ZR_Q_SWA = 0, 1
ZR_KSC, ZR_VSC, ZR_KWC, ZR_VWC, ZR_KD, ZR_VD = 6, 7, 8, 9, 10, 11
ZR_WIDTH = 2 * BRANCH_WIDTH + 6 * LANES


def _token_mixers(x, w_in, conv_w, cmp_pos, cmp_w1, cmp_b1, cmp_w2, cmp_b2, sinks):
    b, s, d = x.shape
    x2d = x.reshape(b * s, d)
    w_dil, w_conv, w_gate, w_rest = _prep_in_weights(w_in)
    gw = 3 * BRANCH_WIDTH

    z_dil = _linear(x2d, w_dil, [(g * gw, (g + 1) * gw) for g in range(N_DIL)], gw, "in_proj_dil")
    zr, kcc, vcc = _linear(x2d, w_rest, [(0, ZR_WIDTH), (ZR_WIDTH, ZR_WIDTH + LANES),
                                         (ZR_WIDTH + LANES, ZR_WIDTH + 2 * LANES)], 256, "in_proj_rest")
    zr = zr.reshape(b, s, ZR_WIDTH)
    o_b, gates = _conv_gate(x, w_conv, conv_w, w_gate)

    dil_o, dil_lse = [], []
    for g, (window, dil) in enumerate(DIL_PATTERNS):
        view = z_dil[g].reshape(b, s // dil, dil * gw)
        og, lg = _banded(view, view, view, nrep=dil,
                         qcol=lambda r: 3 * r, kcol=lambda r: 3 * r + 1, vcol=lambda r: 3 * r + 2,
                         kw=3 * LANES, window=window // dil, want_lse=True)
        dil_o.append(og.reshape(b * s, BRANCH_WIDTH))
        dil_lse.append(lg.reshape(b * s, BRANCH_WIDTH))

    kc = _compress(kcc.reshape(b, s, LANES), cmp_pos[0], cmp_w1[0], cmp_b1[0], cmp_w2[0], cmp_b2[0])
    vc = _compress(vcc.reshape(b, s, LANES), cmp_pos[1], cmp_w1[1], cmp_b1[1], cmp_w2[1], cmp_b2[1])
    o_cmp, sel = _cmp_select(zr, ZR_Q_NSA, kc, vc)
    o_slc = _slc(zr, ZR_Q_NSA, ZR_KSC, ZR_VSC, sel)
    (o_win,) = _banded(zr, zr, zr, nrep=1, qcol=lambda r: ZR_Q_NSA, kcol=lambda r: ZR_KWC, vcol=lambda r: ZR_VWC,
                       kw=LANES, window=NSA_WINDOW - 1, want_lse=False, tq=512)

    sink_row = sinks.astype(F32)[GQA_COL_HEAD].reshape(1, BRANCH_WIDTH)
    (o_d,) = _banded(zr, zr, zr, nrep=1, qcol=lambda r: ZR_Q_SWA, kcol=lambda r: ZR_KD, vcol=lambda r: ZR_VD,
                     kw=LANES, window=SWA_WINDOW - 1, want_lse=False, sink_row=sink_row)

    t = b * s
    flat = lambda a: a.reshape(t, a.shape[-1])
    return [dil_o[0], dil_o[1], dil_o[2], dil_lse[0], dil_lse[1], dil_lse[2], flat(o_b), flat(o_cmp), flat(o_slc),
            flat(o_win), flat(gates), flat(o_d)]


def kernel(x, p, w_in, conv_w, cmp_pos, cmp_w1, cmp_b1, cmp_w2, cmp_b2, sinks, w_branch, w_merge_gate, b_merge_gate, w_out, ln_mix_g, ln_mix_b, ffn_w_gate, ffn_w_up, ffn_w_down, w_router, b_router, moe_w_gate, moe_w_up, moe_w_down, ple_w, ple_gate_w, ple_gate_b, ln_ffn_g, ln_ffn_b):
    depth, b, s, _ = p.shape
    d = x.shape[-1]
    t = b * s
    alpha = (2 * depth) ** 0.25
    row = lambda v: v.reshape(1, -1).astype(F32)
    for i in range(depth):
        branch_inputs = _token_mixers(x, w_in[i], conv_w[i], cmp_pos[i], cmp_w1[i], cmp_b1[i], cmp_w2[i],
                                      cmp_b2[i], sinks[i])
        wg = jnp.concatenate([w_merge_gate[i, m] for m in range(N_BRANCH)], axis=1).astype(CDT)
        bg = b_merge_gate[i].reshape(1, N_BRANCH * d).astype(F32)
        wb = jnp.stack([w_branch[i, 0], w_branch[i, 1], w_branch[i, 2][GQA_COL_PERM],
                        w_branch[i, 3][GQA_COL_PERM]]).astype(CDT)
        x1 = _merge(x.reshape(t, d), branch_inputs, wg, bg, wb, w_out[i].astype(CDT),
                    row(ln_mix_g[i]), row(ln_mix_b[i]), alpha)
        p2d = p[i].reshape(t, -1)
        ple_args = (ple_w[i].astype(CDT), ple_gate_w[i].astype(CDT), row(ple_gate_b[i]),
                    row(ln_ffn_g[i]), row(ln_ffn_b[i]))
        j = i // 2
        if i % 2 == 0:
            x2 = _ffn(x1, p2d, ffn_w_gate[j], ffn_w_up[j], ffn_w_down[j], *ple_args, alpha)
        else:
            tm_moe = min(MOE_TILE, t)
            routing = _router(x1, w_router[j], b_router[j], tm_moe)
            x2 = _moe(x1, routing, p2d, moe_w_gate[j].astype(CDT), moe_w_up[j].astype(CDT),
                      moe_w_down[j].astype(CDT), *ple_args, alpha, tm_moe)
        x = x2.reshape(b, s, d)
    return x
```

```python
import functools

import numpy as np
import jax
import jax.numpy as jnp
from jax import lax
from jax.experimental import pallas as pl
from jax.experimental.pallas import tpu as pltpu

D_MODEL = 1024
HEAD_DIM = 64
DIL_PATTERNS = ((128, 1), (512, 4), (2048, 16))
N_DIL = 3
DIL_HEADS = 6
CONV_WIDTH = 384
CONV_K = 3
NSA_Q_HEADS = 6
NSA_KV_HEADS = 2
CMP_BLOCK = 32
CMP_STRIDE = 16
CMP_HIDDEN = 128
SEL_BLOCK = 64
N_SEL = 16
NSA_WINDOW = 512
SWA_Q_HEADS = 6
SWA_WINDOW = 128
BRANCH_WIDTH = 384
N_BRANCH = 4
N_EXPERTS = 8
LN_EPS = 1e-5
NEG_INF = -1e30
DIL_WIDTH = N_DIL * DIL_HEADS * HEAD_DIM
COLUMN_SIZES = (DIL_WIDTH, DIL_WIDTH, DIL_WIDTH, CONV_WIDTH, CONV_WIDTH, CONV_WIDTH,
                NSA_Q_HEADS * HEAD_DIM, 128, 128, 128, 128, 128, 128, 3 * NSA_Q_HEADS,
                SWA_Q_HEADS * HEAD_DIM, 128, 128)
COL_OFF = np.concatenate([[0], np.cumsum(COLUMN_SIZES)]).tolist()

LANES = 128
V7X_VMEM_BYTES = 64 * 1024 * 1024
MIB = 1024 * 1024

CDT = jnp.bfloat16
F32 = jnp.float32
QK_SCALE = HEAD_DIM ** -0.5
SUB_Q = 128
BIG = 1e30

_GQA_HEAD_ORDER = (0, 3, 1, 4, 2, 5)
GQA_COL_PERM = np.concatenate([np.arange(h * HEAD_DIM, (h + 1) * HEAD_DIM) for h in _GQA_HEAD_ORDER])
GQA_COL_HEAD = GQA_COL_PERM // HEAD_DIM


def _cparams(sem, vmem_mib):
    return pltpu.CompilerParams(dimension_semantics=sem, vmem_limit_bytes=int(vmem_mib * MIB))


def _nt_dot(a, b):
    return lax.dot_general(a, b, (((1,), (1,)), ((), ())), preferred_element_type=F32)


def _dot(a, b):
    return jnp.dot(a, b, preferred_element_type=F32)


def _layer_norm(r, g, b):
    mu = jnp.mean(r, axis=-1, keepdims=True)
    d = r - mu
    var = jnp.mean(d * d, axis=-1, keepdims=True)
    return d * lax.rsqrt(var + LN_EPS) * g + b


def _half_masks():
    lane = lax.broadcasted_iota(jnp.int32, (1, LANES), 1)
    return lane < HEAD_DIM


def _linear_kernel(x_ref, w_ref, *o_refs, splits, n_chunk):
    xb = x_ref[...].astype(CDT)
    for o_ref, (c0, c1) in zip(o_refs, splits):
        for a in range(c0, c1, n_chunk):
            b = min(a + n_chunk, c1)
            o_ref[:, a - c0:b - c0] = _dot(xb, w_ref[:, a:b]).astype(o_ref.dtype)


def _linear(x2d, w, splits, n_chunk, name, tm=512):
    t, k = x2d.shape
    tm = min(tm, t)
    n = w.shape[1]
    outs = [jax.ShapeDtypeStruct((t, c1 - c0), CDT) for c0, c1 in splits]
    return pl.pallas_call(
        functools.partial(_linear_kernel, splits=tuple(splits), n_chunk=n_chunk),
        grid=(t // tm,),
        in_specs=[pl.BlockSpec((tm, k), lambda i: (i, 0)),
                  pl.BlockSpec((k, n), lambda i: (0, 0))],
        out_specs=[pl.BlockSpec((tm, c1 - c0), lambda i: (i, 0)) for c0, c1 in splits],
        out_shape=outs,
        compiler_params=_cparams(("parallel",), 48),
        name=name,
    )(x2d, w)


def _conv_gate_kernel(x_ref, xh_ref, wc_ref, cw_ref, wg_ref, ob_ref, g_ref, *, tm):
    i = pl.program_id(1)
    w = CONV_WIDTH
    xb = x_ref[...].astype(CDT)
    z = _dot(xb, wc_ref[...])
    u = z[:, w:2 * w] * z[:, 2 * w:3 * w]
    zh = _dot(xh_ref[...].astype(CDT), wc_ref[:, w:3 * w])
    uh = zh[:, :w] * zh[:, w:]
    uh = jnp.where(i == 0, 0.0, uh)
    row = lax.broadcasted_iota(jnp.int32, (tm, w), 0)
    u1 = jnp.where(row == 0, uh[7:8, :], pltpu.roll(u, 1, 0))
    u2 = jnp.where(row == 0, uh[6:7, :], jnp.where(row == 1, uh[7:8, :], pltpu.roll(u, 2, 0)))
    y = cw_ref[0:1, :] * u2 + cw_ref[1:2, :] * u1 + cw_ref[2:3, :] * u
    ob_ref[...] = (z[:, :w] * y).astype(ob_ref.dtype)
    g_ref[...] = jax.nn.sigmoid(_dot(xb, wg_ref[...]))


def _conv_gate(x, wc, conv_w, wg, tm=512):
    b, s, d = x.shape
    tm = min(tm, s)
    hb = tm // 8
    return pl.pallas_call(
        functools.partial(_conv_gate_kernel, tm=tm),
        grid=(b, s // tm),
        in_specs=[pl.BlockSpec((None, tm, d), lambda bi, i: (bi, i, 0)),
                  pl.BlockSpec((None, 8, d), lambda bi, i: (bi, jnp.maximum(i * hb - 1, 0), 0)),
                  pl.BlockSpec(wc.shape, lambda bi, i: (0, 0)),
                  pl.BlockSpec(conv_w.shape, lambda bi, i: (0, 0)),
                  pl.BlockSpec(wg.shape, lambda bi, i: (0, 0))],
        out_specs=[pl.BlockSpec((None, tm, CONV_WIDTH), lambda bi, i: (bi, i, 0)),
                   pl.BlockSpec((None, tm, wg.shape[1]), lambda bi, i: (bi, i, 0))],
        out_shape=[jax.ShapeDtypeStruct((b, s, CONV_WIDTH), CDT),
                   jax.ShapeDtypeStruct((b, s, wg.shape[1]), F32)],
        compiler_params=_cparams(("parallel", "parallel"), 48),
        name="conv_gate",
    )(x, x, wc, conv_w, wg)


def _banded_kernel(*refs, window, pr, tq, kw, want_lse, has_sink):
    q_ref, kp_ref, kc_ref, vp_ref, vc_ref = refs[:5]
    n = 5
    sink_ref = None
    if has_sink:
        sink_ref = refs[n]
        n += 1
    o_ref = refs[n]
    n += 1
    lse_ref = None
    if want_lse:
        lse_ref = refs[n]
        n += 1
    kbuf, vbuf = refs[n], refs[n + 1]

    i = pl.program_id(2)
    kbuf[0:pr, :] = kp_ref[...]
    kbuf[pr:pr + tq, :] = kc_ref[...]
    vbuf[0:pr, :] = vp_ref[...]
    vbuf[pr:pr + tq, :] = vc_ref[...]

    span = SUB_Q + pr
    qi = lax.broadcasted_iota(jnp.int32, (SUB_Q, span), 0)
    kj = lax.broadcasted_iota(jnp.int32, (SUB_Q, span), 1)
    dist = pr + qi - kj
    band = (dist >= 0) & (dist <= window)
    lo = _half_masks()
    for sb in range(tq // SUB_Q):
        r0 = sb * SUB_Q
        mask = band & (i * tq + r0 - pr + kj >= 0)
        for p in range(3):
            c0 = p * LANES
            kc0 = c0 if kw == 3 * LANES else 0
            qp = q_ref[r0:r0 + SUB_Q, c0:c0 + LANES]
            kk = kbuf[r0:r0 + span, kc0:kc0 + LANES]
            vv = vbuf[r0:r0 + span, kc0:kc0 + LANES]
            o_half, lse_half = [], []
            for hm in (lo, jnp.logical_not(lo)):
                qm = (jnp.where(hm, qp, 0) * QK_SCALE).astype(CDT)
                s = jnp.where(mask, _nt_dot(qm, kk), NEG_INF)
                m = jnp.max(s, axis=-1, keepdims=True)
                e = jnp.exp(s - m)
                l = jnp.sum(e, axis=-1, keepdims=True)
                o_half.append(_dot(e.astype(CDT), vv) / l)
                lse_half.append(m + jnp.log(l))
            o_pair = jnp.where(lo, o_half[0], o_half[1])
            lse_pair = jnp.where(lo, lse_half[0], lse_half[1])
            if has_sink:
                o_pair = o_pair * jax.nn.sigmoid(lse_pair - sink_ref[:, c0:c0 + LANES])
            o_ref[r0:r0 + SUB_Q, c0:c0 + LANES] = o_pair.astype(o_ref.dtype)
            if want_lse:
                lse_ref[r0:r0 + SUB_Q, c0:c0 + LANES] = lse_pair


def _banded(qa, ka, va, *, nrep, qcol, kcol, vcol, kw, window, want_lse, sink_row=None, tq=256):
    b, l, _ = qa.shape
    pr = -(-window // SUB_Q) * SUB_Q
    tq = min(max(tq, pr), l)
    assert tq % pr == 0 and l % tq == 0, (tq, pr, l)
    ratio = tq // pr
    qw = 3 * LANES
    in_specs = [
        pl.BlockSpec((None, tq, qw), lambda bi, r, i: (bi, i, qcol(r))),
        pl.BlockSpec((None, pr, kw), lambda bi, r, i: (bi, jnp.maximum(i * ratio - 1, 0), kcol(r))),
        pl.BlockSpec((None, tq, kw), lambda bi, r, i: (bi, i, kcol(r))),
        pl.BlockSpec((None, pr, kw), lambda bi, r, i: (bi, jnp.maximum(i * ratio - 1, 0), vcol(r))),
        pl.BlockSpec((None, tq, kw), lambda bi, r, i: (bi, i, vcol(r))),
    ]
    args = [qa, ka, ka, va, va]
    if sink_row is not None:
        in_specs.append(pl.BlockSpec(sink_row.shape, lambda bi, r, i: (0, 0)))
        args.append(sink_row)
    out_specs = [pl.BlockSpec((None, tq, qw), lambda bi, r, i: (bi, i, r))]
    out_shape = [jax.ShapeDtypeStruct((b, l, nrep * qw), CDT)]
    if want_lse:
        out_specs.append(pl.BlockSpec((None, tq, qw), lambda bi, r, i: (bi, i, r)))
        out_shape.append(jax.ShapeDtypeStruct((b, l, nrep * qw), F32))
    res = pl.pallas_call(
        functools.partial(_banded_kernel, window=window, pr=pr, tq=tq, kw=kw, want_lse=want_lse,
                          has_sink=sink_row is not None),
        grid=(b, nrep, l // tq),
        in_specs=in_specs,
        out_specs=out_specs,
        out_shape=out_shape,
        scratch_shapes=[pltpu.VMEM((pr + tq, kw), ka.dtype), pltpu.VMEM((pr + tq, kw), va.dtype)],
        compiler_params=_cparams(("parallel", "parallel", "parallel"), 32),
        name=f"banded_w{window}_k{kw}_r{nrep}",
    )(*args)
    return res


def _gelu_tanh(x):
    return 0.5 * x * (1.0 + jnp.tanh(0.7978845608028654 * (x + 0.044715 * (x * x * x))))


def _compress_kernel(x_ref, pa_ref, pb_ref, w1a_ref, w1b_ref, b1_ref, w2_ref, b2_ref, o_ref):
    x = x_ref[...].astype(F32)
    n = x.shape[0]
    a = _dot((x + pa_ref[...]).astype(CDT), w1a_ref[...])
    bm = _dot((x + pb_ref[...]).astype(CDT), w1b_ref[...])
    h = a + pltpu.roll(bm, n - 1, 0) + b1_ref[...]
    o_ref[...] = (_dot(_gelu_tanh(h).astype(CDT), w2_ref[...]) + b2_ref[...]).astype(o_ref.dtype)


def _compress(t, pos, w1, b1, w2, b2):
    b, s, _ = t.shape
    nch = s // CMP_STRIDE
    xw = CMP_STRIDE * LANES
    x = t.reshape(b, nch, xw)
    eye = jnp.eye(NSA_KV_HEADS, dtype=F32)
    w1r = w1.reshape(CMP_BLOCK, HEAD_DIM, CMP_HIDDEN)

    def expand_w1(part):
        return jnp.einsum('tdj,kl->tkdlj', part, eye).reshape(xw, NSA_KV_HEADS * CMP_HIDDEN).astype(CDT)

    def expand_pos(part):
        return jnp.broadcast_to(part[:, None, :], (CMP_STRIDE, NSA_KV_HEADS, HEAD_DIM)).reshape(1, xw)

    w1a, w1b = expand_w1(w1r[:CMP_STRIDE]), expand_w1(w1r[CMP_STRIDE:])
    pa, pb = expand_pos(pos[:CMP_STRIDE]), expand_pos(pos[CMP_STRIDE:])
    b1e = jnp.tile(b1, NSA_KV_HEADS).reshape(1, -1)
    w2e = jnp.einsum('jd,kl->kjld', w2, eye).reshape(NSA_KV_HEADS * CMP_HIDDEN, LANES).astype(CDT)
    b2e = jnp.tile(b2, NSA_KV_HEADS).reshape(1, -1)
    consts = [pa, pb, w1a, w1b, b1e, w2e, b2e]
    return pl.pallas_call(
        _compress_kernel,
        grid=(b,),
        in_specs=[pl.BlockSpec((None, nch, xw), lambda bi: (bi, 0, 0))]
        + [pl.BlockSpec(c.shape, lambda bi: (0, 0)) for c in consts],
        out_specs=pl.BlockSpec((None, nch, LANES), lambda bi: (bi, 0, 0)),
        out_shape=jax.ShapeDtypeStruct((b, nch, LANES), CDT),
        compiler_params=_cparams(("parallel",), 48),
        name="nsa_compress",
    )(x, *consts)


def _cmp_select_kernel(q_ref, kc_ref, vc_ref, ov_ref, o_ref, sel_ref, *, tq, n_sel):
    i = pl.program_id(1)
    ncp = kc_ref.shape[0]
    ns = ov_ref.shape[1]
    lo = _half_masks()
    t_col = i * tq + lax.broadcasted_iota(jnp.int32, (tq, 1), 0)
    c_end = lax.broadcasted_iota(jnp.int32, (tq, ncp), 1) * CMP_STRIDE + (CMP_BLOCK - 1)
    vis = c_end <= t_col
    blk = lax.broadcasted_iota(jnp.int32, (tq, ns), 1)
    blk_t = lax.broadcasted_iota(jnp.int32, (ns, tq), 0)
    cur = t_col // SEL_BLOCK
    causal = blk <= cur
    forced = (blk == 0) | (blk == cur) | (blk == cur - 1)
    kc = kc_ref[...]
    vc = vc_ref[...]
    o_kv = []
    for kv, hm in enumerate((lo, jnp.logical_not(lo))):
        psum = jnp.zeros((tq, ncp), F32)
        o_p = []
        for p in range(3):
            qm = (jnp.where(hm, q_ref[:, p * LANES:(p + 1) * LANES], 0) * QK_SCALE).astype(CDT)
            s = jnp.where(vis, _nt_dot(qm, kc), NEG_INF)
            m = jnp.max(s, axis=-1, keepdims=True)
            e = jnp.where(vis, jnp.exp(s - m), 0.0)
            pn = e / jnp.maximum(jnp.sum(e, axis=-1, keepdims=True), 1e-30)
            o_p.append(_dot(pn.astype(CDT), vc))
            psum = psum + pn
        o_kv.append(o_p)
        p_hi = psum.astype(CDT)
        p_lo = (psum - p_hi.astype(F32)).astype(CDT)
        imp = _dot(p_hi, ov_ref[...]) + _dot(p_lo, ov_ref[...])
        work_t = jnp.where(causal & jnp.logical_not(forced), imp, -BIG).T

        def pick(_, carry):
            work, sel = carry
            m = jnp.max(work, axis=0, keepdims=True)
            idx = jnp.min(jnp.where(work == m, blk_t, ns), axis=0, keepdims=True)
            hit = blk_t == idx
            sel = jnp.where(hit & (m > -BIG), 1.0, sel)
            return jnp.where(hit, -2.0 * BIG, work), sel

        _, sel_t = lax.fori_loop(0, n_sel - 3, pick, (work_t, jnp.zeros((ns, tq), F32)))
        selb = jnp.where(forced | (sel_t.T > 0.5), 0.0, NEG_INF)
        sel_ref[:, kv * ns:(kv + 1) * ns] = selb.astype(sel_ref.dtype)
    for p in range(3):
        o_ref[:, p * LANES:(p + 1) * LANES] = jnp.where(lo, o_kv[0][p], o_kv[1][p]).astype(o_ref.dtype)


def _cmp_select(zr, qcol, kc, vc, tq=128):
    b, s, _ = zr.shape
    ncp = kc.shape[1]
    ns = s // SEL_BLOCK
    n_sel = min(N_SEL, ns)
    assert n_sel >= 3, "selection needs room for the three forced blocks"
    tq = min(tq, s)
    c = np.arange(ncp)[:, None] * CMP_STRIDE
    j = np.arange(ns)[None, :] * SEL_BLOCK
    overlap = ((c < j + SEL_BLOCK) & (c + CMP_BLOCK - 1 >= j)).astype(np.float32)
    overlap[ncp - 1:, :] = 0.0
    ov = jnp.asarray(overlap, CDT)
    qw = 3 * LANES
    return pl.pallas_call(
        functools.partial(_cmp_select_kernel, tq=tq, n_sel=n_sel),
        grid=(b, s // tq),
        in_specs=[pl.BlockSpec((None, tq, qw), lambda bi, i: (bi, i, qcol)),
                  pl.BlockSpec((None, ncp, LANES), lambda bi, i: (bi, 0, 0)),
                  pl.BlockSpec((None, ncp, LANES), lambda bi, i: (bi, 0, 0)),
                  pl.BlockSpec(ov.shape, lambda bi, i: (0, 0))],
        out_specs=[pl.BlockSpec((None, tq, qw), lambda bi, i: (bi, i, 0)),
                   pl.BlockSpec((None, tq, 2 * ns), lambda bi, i: (bi, i, 0))],
        out_shape=[jax.ShapeDtypeStruct((b, s, qw), CDT),
                   jax.ShapeDtypeStruct((b, s, 2 * ns), CDT)],
        compiler_params=_cparams(("parallel", "parallel"), 48),
        name="nsa_cmp_select",
    )(zr, kc, vc, ov)


SLC_ROWS = 64


def _slc_kernel(q_ref, k0_ref, k1_ref, v_ref, selb_ref, o_ref, s_ref, e_ref, m_ref, l_ref, a_ref, acc_ref,
                *, tq, tk):
    i = pl.program_id(1)
    ns = selb_ref.shape[1] // 2
    bpt = tk // SEL_BLOCK
    rb = SLC_ROWS
    lo = _half_masks()
    n_tiles = ((i + 1) * tq + tk - 1) // tk
    p_row = lax.broadcasted_iota(jnp.int32, (ns, LANES), 0)
    p_col = lax.broadcasted_iota(jnp.int32, (ns, LANES), 1)
    groups = []
    for kv, (hm, k_ref) in enumerate(((lo, k0_ref), (jnp.logical_not(lo), k1_ref))):
        q3 = [(jnp.where(hm, q_ref[:, p * LANES:(p + 1) * LANES], 0) * QK_SCALE).astype(CDT) for p in range(3)]
        selb = selb_ref[:, kv * ns:(kv + 1) * ns]
        lane0 = HEAD_DIM if kv == 0 else 0
        groups.append((kv, hm, k_ref, q3, selb, lane0))
    m_ref[...] = jnp.full(m_ref.shape, NEG_INF, F32)
    l_ref[...] = jnp.zeros(l_ref.shape, F32)
    acc_ref[...] = jnp.zeros(acc_ref.shape, F32)

    def scores(j):
        k0 = pl.multiple_of(j * tk, tk)
        for kv, hm, k_ref, q3, selb, lane0 in groups:
            place = ((p_col >= lane0) & (p_col < lane0 + bpt) & (p_row == p_col - lane0 + j * bpt)).astype(CDT)
            sb = _dot(selb, place).astype(CDT)
            qp = jnp.concatenate([jnp.where(hm, q, sb) for q in q3], axis=0)
            s_ref[kv] = _nt_dot(qp, k_ref[pl.ds(k0, tk), :])

    def softmax_pv(j, diagonal):
        k0 = pl.multiple_of(j * tk, tk)
        for kv in range(2):
            for r0 in range(0, 3 * tq, rb):
                rows = slice(r0, r0 + rb)
                s = s_ref[kv, rows, :]
                if diagonal:
                    t_row = i * tq + (r0 % tq) + lax.broadcasted_iota(jnp.int32, (rb, 1), 0)
                    kpos = k0 + lax.broadcasted_iota(jnp.int32, (rb, tk), 1)
                    s = jnp.where(kpos <= t_row, s, NEG_INF)
                m_old = m_ref[kv, rows, :]
                m_new = jnp.maximum(m_old, jnp.max(s, axis=-1, keepdims=True))
                alpha = jnp.exp(m_old - m_new)
                e = jnp.exp(s - jnp.tile(m_new, (1, tk // LANES)))
                l_ref[kv, rows, :] = alpha * l_ref[kv, rows, :] + jnp.sum(e, axis=-1, keepdims=True)
                m_ref[kv, rows, :] = m_new
                a_ref[kv, rows, :] = alpha
                e_ref[kv, rows, :] = e.astype(CDT)
        for kv in range(2):
            acc_ref[kv] = a_ref[kv] * acc_ref[kv] + _dot(e_ref[kv], v_ref[pl.ds(k0, tk), :])

    def tile(j, diagonal):
        scores(j)
        softmax_pv(j, diagonal)

    lax.fori_loop(0, n_tiles - 1, lambda j, c: (tile(j, False), c)[1], 0)
    tile(n_tiles - 1, True)
    o0 = acc_ref[0] / l_ref[0]
    o1 = acc_ref[1] / l_ref[1]
    for p in range(3):
        o_ref[:, p * LANES:(p + 1) * LANES] = jnp.where(
            lo, o0[p * tq:(p + 1) * tq], o1[p * tq:(p + 1) * tq]).astype(o_ref.dtype)


def _slc(zr, qcol, ksc, vcol, selb, tq=256, tk=512):
    b, s, _ = zr.shape
    tq = min(tq, s)
    tk = min(tk, s)
    assert tq % SLC_ROWS == 0 and tk // SEL_BLOCK <= HEAD_DIM
    qw = 3 * LANES
    ns2 = selb.shape[2]
    pat = jax.nn.one_hot((jnp.arange(s) // SEL_BLOCK) % (tk // SEL_BLOCK), HEAD_DIM, dtype=ksc.dtype)
    pat = jnp.broadcast_to(pat[None], (b, s, HEAD_DIM))
    k0 = jnp.concatenate([ksc[..., :HEAD_DIM], pat], axis=-1)
    k1 = jnp.concatenate([pat, ksc[..., HEAD_DIM:]], axis=-1)
    full = lambda bi, i: (bi, 0, 0)
    return pl.pallas_call(
        functools.partial(_slc_kernel, tq=tq, tk=tk),
        grid=(b, s // tq),
        in_specs=[pl.BlockSpec((None, tq, qw), lambda bi, i: (bi, i, qcol)),
                  pl.BlockSpec((None, s, LANES), full),
                  pl.BlockSpec((None, s, LANES), full),
                  pl.BlockSpec((None, s, LANES), lambda bi, i: (bi, 0, vcol)),
                  pl.BlockSpec((None, tq, ns2), lambda bi, i: (bi, i, 0))],
        out_specs=pl.BlockSpec((None, tq, qw), lambda bi, i: (bi, i, 0)),
        out_shape=jax.ShapeDtypeStruct((b, s, qw), CDT),
        scratch_shapes=[pltpu.VMEM((2, 3 * tq, tk), F32), pltpu.VMEM((2, 3 * tq, tk), CDT),
                        pltpu.VMEM((2, 3 * tq, LANES), F32), pltpu.VMEM((2, 3 * tq, LANES), F32),
                        pltpu.VMEM((2, 3 * tq, LANES), F32), pltpu.VMEM((2, 3 * tq, LANES), F32)],
        compiler_params=_cparams(("parallel", "arbitrary"), 56),
        name="nsa_slc",
    )(zr, k0, k1, zr, selb)


def _merge_kernel(x_ref, oa0, oa1, oa2, la0, la1, la2, ob, ocmp, oslc, owin, gates, od,
                  wg_ref, bg_ref, wb_ref, wo_ref, g_ref, b_ref, o_ref, *, alpha):
    x = x_ref[...]
    xb = x.astype(CDT)
    bw = BRANCH_WIDTH
    l0, l1, l2 = la0[...], la1[...], la2[...]
    mx = jnp.maximum(jnp.maximum(l0, l1), l2)
    w0, w1, w2 = jnp.exp(l0 - mx), jnp.exp(l1 - mx), jnp.exp(l2 - mx)
    o_a = (w0 * oa0[...].astype(F32) + w1 * oa1[...].astype(F32) + w2 * oa2[...].astype(F32)) / (w0 + w1 + w2)
    o_c = (gates[:, 0:bw] * ocmp[...].astype(F32) + gates[:, bw:2 * bw] * oslc[...].astype(F32)
           + gates[:, 2 * bw:3 * bw] * owin[...].astype(F32))
    branches = (o_a.astype(CDT), ob[...], o_c.astype(CDT), od[...])
    d = x.shape[1]
    merged = jnp.zeros(x.shape, F32)
    for m in range(N_BRANCH):
        gate = jax.nn.sigmoid(_dot(xb, wg_ref[:, m * d:(m + 1) * d]) + bg_ref[:, m * d:(m + 1) * d])
        merged = merged + gate * _dot(branches[m], wb_ref[m])
    r = alpha * x + _dot(merged.astype(CDT), wo_ref[...])
    o_ref[...] = _layer_norm(r, g_ref[...], b_ref[...])


def _merge(x2d, branch_inputs, wg, bg, wb, wo, g, b, alpha, tm=256):
    t, d = x2d.shape
    tm = min(tm, t)
    row = lambda i: (i, 0)
    const2 = lambda i: (0, 0)
    in_specs = [pl.BlockSpec((tm, d), row)]
    in_specs += [pl.BlockSpec((tm, a.shape[1]), row) for a in branch_inputs]
    in_specs += [pl.BlockSpec(wg.shape, const2), pl.BlockSpec(bg.shape, const2),
                 pl.BlockSpec(wb.shape, lambda i: (0, 0, 0)), pl.BlockSpec(wo.shape, const2),
                 pl.BlockSpec(g.shape, const2), pl.BlockSpec(b.shape, const2)]
    return pl.pallas_call(
        functools.partial(_merge_kernel, alpha=alpha),
        grid=(t // tm,),
        in_specs=in_specs,
        out_specs=pl.BlockSpec((tm, d), row),
        out_shape=jax.ShapeDtypeStruct((t, d), F32),
        compiler_params=_cparams(("parallel",), 56),
        name="merge_ln",
    )(x2d, *branch_inputs, wg, bg, wb, wo, g, b)


def _ple_ln(x, xb, f, p_ref, plw_ref, pgw_ref, pgb_ref, g_ref, b_ref, alpha):
    ple = jax.nn.sigmoid(_dot(xb, pgw_ref[...]) + pgb_ref[...]) * _dot(p_ref[...].astype(CDT), plw_ref[...])
    return _layer_norm(alpha * x + f + ple, g_ref[...], b_ref[...])


def _ffn_kernel(x_ref, p_ref, wg_ref, wu_ref, wd_ref, plw_ref, pgw_ref, pgb_ref, g_ref, b_ref, o_ref, *, alpha):
    x = x_ref[...]
    xb = x.astype(CDT)

    def chunk(c, acc):
        h = jax.nn.silu(_dot(xb, wg_ref[c])) * _dot(xb, wu_ref[c])
        return acc + _dot(h.astype(CDT), wd_ref[c])

    f = lax.fori_loop(0, wg_ref.shape[0], chunk, jnp.zeros(x.shape, F32))
    o_ref[...] = _ple_ln(x, xb, f, p_ref, plw_ref, pgw_ref, pgb_ref, g_ref, b_ref, alpha)


FFN_CHUNK = 256


def _ffn(x2d, p2d, wg, wu, wd, plw, pgw, pgb, g, b, alpha, tm=512):
    t, d = x2d.shape
    tm = min(tm, t)
    dff = wg.shape[1]
    nck = dff // FFN_CHUNK
    wg3 = wg.reshape(d, nck, FFN_CHUNK).transpose(1, 0, 2).astype(CDT)
    wu3 = wu.reshape(d, nck, FFN_CHUNK).transpose(1, 0, 2).astype(CDT)
    wd3 = wd.reshape(nck, FFN_CHUNK, d).astype(CDT)
    row = lambda i: (i, 0)
    c2 = lambda i: (0, 0)
    c3 = lambda i: (0, 0, 0)
    return pl.pallas_call(
        functools.partial(_ffn_kernel, alpha=alpha),
        grid=(t // tm,),
        in_specs=[pl.BlockSpec((tm, d), row), pl.BlockSpec((tm, p2d.shape[1]), row),
                  pl.BlockSpec(wg3.shape, c3), pl.BlockSpec(wu3.shape, c3), pl.BlockSpec(wd3.shape, c3),
                  pl.BlockSpec(plw.shape, c2), pl.BlockSpec(pgw.shape, c2), pl.BlockSpec(pgb.shape, c2),
                  pl.BlockSpec(g.shape, c2), pl.BlockSpec(b.shape, c2)],
        out_specs=pl.BlockSpec((tm, d), row),
        out_shape=jax.ShapeDtypeStruct((t, d), F32),
        compiler_params=_cparams(("parallel",), 60),
        name="ffn_ple_ln",
    )(x2d, p2d, wg3, wu3, wd3, plw, pgw, pgb, g, b)


def _router_kernel(x_ref, wh_ref, wl_ref, b_ref, comb_ref, rank_ref, rank_t_ref, cnt_ref):
    x = x_ref[...]
    xh = x.astype(CDT)
    xl = (x - xh.astype(F32)).astype(CDT)
    logits = _dot(xh, wh_ref[...]) + _dot(xh, wl_ref[...]) + _dot(xl, wh_ref[...]) + b_ref[...]
    lane = lax.broadcasted_iota(jnp.int32, logits.shape, 1)
    v1 = jnp.max(logits, axis=-1, keepdims=True)
    i1 = jnp.min(jnp.where(logits == v1, lane, LANES), axis=-1, keepdims=True)
    rest = jnp.where(lane == i1, -jnp.inf, logits)
    v2 = jnp.max(rest, axis=-1, keepdims=True)
    i2 = jnp.min(jnp.where(rest == v2, lane, LANES), axis=-1, keepdims=True)
    e2 = jnp.exp(v2 - v1)
    comb_ref[...] = jnp.where(lane == i1, 1.0 / (1.0 + e2), 0.0) + jnp.where(lane == i2, e2 / (1.0 + e2), 0.0)
    routed = (lane == i1) | (lane == i2)
    mask = routed.astype(CDT)
    tm = x.shape[0]
    before = (lax.broadcasted_iota(jnp.int32, (tm, tm), 1) < lax.broadcasted_iota(jnp.int32, (tm, tm), 0)).astype(CDT)
    rank = jnp.where(routed, _dot(before, mask), -1.0)
    rank_ref[...] = rank
    rank_t_ref[...] = rank.T[0:rank_t_ref.shape[0], :]
    cnt_ref[...] = jnp.sum(routed.astype(F32), axis=0, keepdims=True).astype(jnp.int32)


def _router(x2d, w_router, b_router, tm):
    t, d = x2d.shape
    ne = w_router.shape[1]
    wp = jnp.zeros((d, LANES), F32).at[:, :ne].set(w_router)
    wh = wp.astype(CDT)
    wl = (wp - wh.astype(F32)).astype(CDT)
    bp = jnp.full((1, LANES), -BIG, F32).at[0, :ne].set(b_router)
    nt = t // tm
    row = lambda i: (i, 0)
    return pl.pallas_call(
        _router_kernel,
        grid=(nt,),
        in_specs=[pl.BlockSpec((tm, d), row), pl.BlockSpec(wh.shape, lambda i: (0, 0)),
                  pl.BlockSpec(wl.shape, lambda i: (0, 0)), pl.BlockSpec(bp.shape, lambda i: (0, 0))],
        out_specs=[pl.BlockSpec((tm, LANES), row), pl.BlockSpec((tm, LANES), row),
                   pl.BlockSpec((None, 8, tm), lambda i: (i, 0, 0)),
                   pl.BlockSpec((None, 1, LANES), lambda i: (i, 0, 0))],
        out_shape=[jax.ShapeDtypeStruct((t, LANES), F32), jax.ShapeDtypeStruct((t, LANES), F32),
                   jax.ShapeDtypeStruct((nt, 8, tm), F32), jax.ShapeDtypeStruct((nt, 1, LANES), jnp.int32)],
        compiler_params=_cparams(("parallel",), 40),
        name="moe_router",
    )(x2d, wh, wl, bp)


def _moe_kernel(cnt_ref, x_ref, comb_ref, rank_ref, rank_t_ref, p_ref, wg_ref, wu_ref, wd_ref, plw_ref, pgw_ref,
                pgb_ref, g_ref, b_ref, o_ref, xe_ref, ye_ref, *, alpha, rs):
    i = pl.program_id(0)
    e = pl.program_id(1)
    c = pl.program_id(2)
    tm = x_ref.shape[0]
    n_groups = (cnt_ref[i * LANES + e] + rs - 1) // rs

    def rows(sc):
        return pl.ds(pl.multiple_of(sc * rs, 8), rs)

    @pl.when((e == 0) & (c == 0))
    def _():
        o_ref[...] = jnp.zeros_like(o_ref)

    @pl.when(c == 0)
    def _():
        xb = x_ref[...].astype(CDT)
        rank_row = rank_t_ref[pl.ds(e, 1), :]
        row_id = lax.broadcasted_iota(jnp.int32, (rs, tm), 0).astype(F32)

        def gather(sc, _):
            onehot = (rank_row - (sc * rs).astype(F32) == row_id).astype(CDT)
            xe_ref[rows(sc), :] = _dot(onehot, xb).astype(CDT)
            ye_ref[rows(sc), :] = jnp.zeros((rs, ye_ref.shape[1]), F32)
            return 0

        lax.fori_loop(0, n_groups, gather, 0)

    def expert(sc, _):
        xs = xe_ref[rows(sc), :]
        h = jax.nn.silu(_dot(xs, wg_ref[0])) * _dot(xs, wu_ref[0])
        ye_ref[rows(sc), :] += _dot(h.astype(CDT), wd_ref[0])
        return 0

    lax.fori_loop(0, n_groups, expert, 0)

    @pl.when(c == pl.num_programs(2) - 1)
    def _():
        lane = lax.broadcasted_iota(jnp.int32, comb_ref.shape, 1)
        mine = lane == e
        cw = jnp.sum(jnp.where(mine, comb_ref[...], 0.0), axis=-1, keepdims=True)
        rank_col = jnp.sum(jnp.where(mine, rank_ref[...], 0.0), axis=-1, keepdims=True)
        col_id = lax.broadcasted_iota(jnp.int32, (tm, rs), 1).astype(F32)

        def scatter(sc, _):
            onehot = (rank_col - (sc * rs).astype(F32) == col_id).astype(CDT)
            o_ref[...] += cw * _dot(onehot, ye_ref[rows(sc), :].astype(CDT))
            return 0

        lax.fori_loop(0, n_groups, scatter, 0)

    @pl.when((e == pl.num_programs(1) - 1) & (c == pl.num_programs(2) - 1))
    def _():
        x = x_ref[...]
        o_ref[...] = _ple_ln(x, x.astype(CDT), o_ref[...], p_ref, plw_ref, pgw_ref, pgb_ref, g_ref, b_ref, alpha)


MOE_CHUNK = 512
MOE_TILE = 1024
MOE_ROW_GROUP = 320


def _moe(x2d, routing, p2d, wg, wu, wd, plw, pgw, pgb, g, b, alpha, tm):
    comb, rank, rank_t, cnt = routing
    t, d = x2d.shape
    ne, _, dff = wg.shape
    ck = min(MOE_CHUNK, dff)
    rs = min(MOE_ROW_GROUP, tm)
    max_rows = -(-tm // rs) * rs
    row = lambda i, e, c, cnt: (i, 0)
    c2 = lambda i, e, c, cnt: (0, 0)
    grid_spec = pltpu.PrefetchScalarGridSpec(
        num_scalar_prefetch=1,
        grid=(t // tm, ne, dff // ck),
        in_specs=[pl.BlockSpec((tm, d), row), pl.BlockSpec((tm, LANES), row), pl.BlockSpec((tm, LANES), row),
                  pl.BlockSpec((None, 8, tm), lambda i, e, c, cnt: (i, 0, 0)),
                  pl.BlockSpec((tm, p2d.shape[1]), row),
                  pl.BlockSpec((1, d, ck), lambda i, e, c, cnt: (e, 0, c)),
                  pl.BlockSpec((1, d, ck), lambda i, e, c, cnt: (e, 0, c)),
                  pl.BlockSpec((1, ck, d), lambda i, e, c, cnt: (e, c, 0)),
                  pl.BlockSpec(plw.shape, c2), pl.BlockSpec(pgw.shape, c2), pl.BlockSpec(pgb.shape, c2),
                  pl.BlockSpec(g.shape, c2), pl.BlockSpec(b.shape, c2)],
        out_specs=pl.BlockSpec((tm, d), row),
        scratch_shapes=[pltpu.VMEM((max_rows, d), CDT), pltpu.VMEM((max_rows, d), F32)],
    )
    return pl.pallas_call(
        functools.partial(_moe_kernel, alpha=alpha, rs=rs),
        grid_spec=grid_spec,
        out_shape=jax.ShapeDtypeStruct((t, d), F32),
        compiler_params=_cparams(("parallel", "arbitrary", "arbitrary"), 56),
        name="moe_ple_ln",
    )(cnt.reshape(-1), x2d, comb, rank, rank_t, p2d, wg, wu, wd, plw, pgw, pgb, g, b)


def _prep_in_weights(w_in):
    o = COL_OFF
    bw = BRANCH_WIDTH
    cols = lambda n: w_in[:, o[n]:o[n + 1]]
    qa, ka, va = cols(0), cols(1), cols(2)
    w_dil = jnp.concatenate(
        [t[:, g * bw:(g + 1) * bw] for g in range(N_DIL) for t in (qa, ka, va)], axis=1).astype(CDT)
    w_conv = jnp.concatenate([cols(3), cols(4), cols(5)], axis=1).astype(CDT)
    gn = cols(13)
    w_gate = jnp.concatenate([gn[:, br * NSA_Q_HEADS + GQA_COL_HEAD] for br in range(3)], axis=1).astype(CDT)
    w_rest = jnp.concatenate([cols(6)[:, GQA_COL_PERM], cols(14)[:, GQA_COL_PERM], cols(10), cols(11),
                              cols(12), cols(15), cols(16), cols(9), cols(7), cols(8)], axis=1).astype(CDT)
    return w_dil, w_conv, w_gate, w_rest


ZR_Q_NSA, ZR_Q_SWA = 0, 1
ZR_VSC, ZR_KWC, ZR_VWC, ZR_KD, ZR_VD = 6, 7, 8, 9, 10
ZR_WIDTH = 2 * BRANCH_WIDTH + 5 * LANES


def _token_mixers(x, w_in, conv_w, cmp_pos, cmp_w1, cmp_b1, cmp_w2, cmp_b2, sinks):
    b, s, d = x.shape
    x2d = x.reshape(b * s, d)
    w_dil, w_conv, w_gate, w_rest = _prep_in_weights(w_in)
    gw = 3 * BRANCH_WIDTH

    z_dil = _linear(x2d, w_dil, [(g * gw, (g + 1) * gw) for g in range(N_DIL)], gw, "in_proj_dil")
    zr, ksc, kcc, vcc = _linear(x2d, w_rest, [(0, ZR_WIDTH)] + [(ZR_WIDTH + n * LANES, ZR_WIDTH + (n + 1) * LANES)
                                                           for n in range(3)], 256, "in_proj_rest")
    zr = zr.reshape(b, s, ZR_WIDTH)
    o_b, gates = _conv_gate(x, w_conv, conv_w, w_gate)

    dil_o, dil_lse = [], []
    for g, (window, dil) in enumerate(DIL_PATTERNS):
        view = z_dil[g].reshape(b, s // dil, dil * gw)
        og, lg = _banded(view, view, view, nrep=dil,
                         qcol=lambda r: 3 * r, kcol=lambda r: 3 * r + 1, vcol=lambda r: 3 * r + 2,
                         kw=3 * LANES, window=window // dil, want_lse=True)
        dil_o.append(og.reshape(b * s, BRANCH_WIDTH))
        dil_lse.append(lg.reshape(b * s, BRANCH_WIDTH))

    kc = _compress(kcc.reshape(b, s, LANES), cmp_pos[0], cmp_w1[0], cmp_b1[0], cmp_w2[0], cmp_b2[0])
    vc = _compress(vcc.reshape(b, s, LANES), cmp_pos[1], cmp_w1[1], cmp_b1[1], cmp_w2[1], cmp_b2[1])
    o_cmp, selb = _cmp_select(zr, ZR_Q_NSA, kc, vc)
    o_slc = _slc(zr, ZR_Q_NSA, ksc.reshape(b, s, LANES), ZR_VSC, selb)
    (o_win,) = _banded(zr, zr, zr, nrep=1, qcol=lambda r: ZR_Q_NSA, kcol=lambda r: ZR_KWC, vcol=lambda r: ZR_VWC,
                       kw=LANES, window=NSA_WINDOW - 1, want_lse=False, tq=512)

    sink_row = sinks.astype(F32)[GQA_COL_HEAD].reshape(1, BRANCH_WIDTH)
    (o_d,) = _banded(zr, zr, zr, nrep=1, qcol=lambda r: ZR_Q_SWA, kcol=lambda r: ZR_KD, vcol=lambda r: ZR_VD,
                     kw=LANES, window=SWA_WINDOW - 1, want_lse=False, sink_row=sink_row)

    t = b * s
    flat = lambda a: a.reshape(t, a.shape[-1])
    return [dil_o[0], dil_o[1], dil_o[2], dil_lse[0], dil_lse[1], dil_lse[2], flat(o_b), flat(o_cmp), flat(o_slc),
            flat(o_win), flat(gates), flat(o_d)]


def kernel(x, p, w_in, conv_w, cmp_pos, cmp_w1, cmp_b1, cmp_w2, cmp_b2, sinks, w_branch, w_merge_gate, b_merge_gate, w_out, ln_mix_g, ln_mix_b, ffn_w_gate, ffn_w_up, ffn_w_down, w_router, b_router, moe_w_gate, moe_w_up, moe_w_down, ple_w, ple_gate_w, ple_gate_b, ln_ffn_g, ln_ffn_b):
    depth, b, s, _ = p.shape
    d = x.shape[-1]
    t = b * s
    alpha = (2 * depth) ** 0.25
    row = lambda v: v.reshape(1, -1).astype(F32)
    for i in range(depth):
        branch_inputs = _token_mixers(x, w_in[i], conv_w[i], cmp_pos[i], cmp_w1[i], cmp_b1[i], cmp_w2[i],
                                      cmp_b2[i], sinks[i])
        wg = jnp.concatenate([w_merge_gate[i, m] for m in range(N_BRANCH)], axis=1).astype(CDT)
        bg = b_merge_gate[i].reshape(1, N_BRANCH * d).astype(F32)
        wb = jnp.stack([w_branch[i, 0], w_branch[i, 1], w_branch[i, 2][GQA_COL_PERM],
                        w_branch[i, 3][GQA_COL_PERM]]).astype(CDT)
        x1 = _merge(x.reshape(t, d), branch_inputs, wg, bg, wb, w_out[i].astype(CDT),
                    row(ln_mix_g[i]), row(ln_mix_b[i]), alpha)
        p2d = p[i].reshape(t, -1)
        ple_args = (ple_w[i].astype(CDT), ple_gate_w[i].astype(CDT), row(ple_gate_b[i]),
                    row(ln_ffn_g[i]), row(ln_ffn_b[i]))
        j = i // 2
        if i % 2 == 0:
            x2 = _ffn(x1, p2d, ffn_w_gate[j], ffn_w_up[j], ffn_w_down[j], *ple_args, alpha)
        else:
            tm_moe = min(MOE_TILE, t)
            routing = _router(x1, w_router[j], b_router[j], tm_moe)
            x2 = _moe(x1, routing, p2d, moe_w_gate[j].astype(CDT), moe_w_up[j].astype(CDT),
                      moe_w_down[j].astype(CDT), *ple_args, alpha, tm_moe)
        x = x2.reshape(b, s, d)
    return x
```

```python
import functools

import numpy as np
import jax
import jax.numpy as jnp
from jax import lax
from jax.experimental import pallas as pl
from jax.experimental.pallas import tpu as pltpu

D_MODEL = 1024
HEAD_DIM = 64
DIL_PATTERNS = ((128, 1), (512, 4), (2048, 16))
N_DIL = 3
DIL_HEADS = 6
CONV_WIDTH = 384
CONV_K = 3
NSA_Q_HEADS = 6
NSA_KV_HEADS = 2
CMP_BLOCK = 32
CMP_STRIDE = 16
CMP_HIDDEN = 128
SEL_BLOCK = 64
N_SEL = 16
NSA_WINDOW = 512
SWA_Q_HEADS = 6
SWA_WINDOW = 128
BRANCH_WIDTH = 384
N_BRANCH = 4
N_EXPERTS = 8
LN_EPS = 1e-5
NEG_INF = -1e30
DIL_WIDTH = N_DIL * DIL_HEADS * HEAD_DIM
COLUMN_SIZES = (DIL_WIDTH, DIL_WIDTH, DIL_WIDTH, CONV_WIDTH, CONV_WIDTH, CONV_WIDTH,
                NSA_Q_HEADS * HEAD_DIM, 128, 128, 128, 128, 128, 128, 3 * NSA_Q_HEADS,
                SWA_Q_HEADS * HEAD_DIM, 128, 128)
COL_OFF = np.concatenate([[0], np.cumsum(COLUMN_SIZES)]).tolist()

LANES = 128
V7X_VMEM_BYTES = 64 * 1024 * 1024
MIB = 1024 * 1024

CDT = jnp.bfloat16
F32 = jnp.float32
QK_SCALE = HEAD_DIM ** -0.5
SUB_Q = 128
BIG = 1e30

_GQA_HEAD_ORDER = (0, 3, 1, 4, 2, 5)
GQA_COL_PERM = np.concatenate([np.arange(h * HEAD_DIM, (h + 1) * HEAD_DIM) for h in _GQA_HEAD_ORDER])
GQA_COL_HEAD = GQA_COL_PERM // HEAD_DIM


def _cparams(sem, vmem_mib):
    return pltpu.CompilerParams(dimension_semantics=sem, vmem_limit_bytes=int(vmem_mib * MIB))


def _nt_dot(a, b):
    return lax.dot_general(a, b, (((1,), (1,)), ((), ())), preferred_element_type=F32)


def _dot(a, b):
    return jnp.dot(a, b, preferred_element_type=F32)


def _layer_norm(r, g, b):
    mu = jnp.mean(r, axis=-1, keepdims=True)
    d = r - mu
    var = jnp.mean(d * d, axis=-1, keepdims=True)
    return d * lax.rsqrt(var + LN_EPS) * g + b


def _half_masks():
    lane = lax.broadcasted_iota(jnp.int32, (1, LANES), 1)
    return lane < HEAD_DIM


def _linear_kernel(x_ref, w_ref, *o_refs, splits, n_chunk):
    xb = x_ref[...].astype(CDT)
    for o_ref, (c0, c1) in zip(o_refs, splits):
        for a in range(c0, c1, n_chunk):
            b = min(a + n_chunk, c1)
            o_ref[:, a - c0:b - c0] = _dot(xb, w_ref[:, a:b]).astype(o_ref.dtype)


def _linear(x2d, w, splits, n_chunk, name, tm=512):
    t, k = x2d.shape
    tm = min(tm, t)
    n = w.shape[1]
    outs = [jax.ShapeDtypeStruct((t, c1 - c0), CDT) for c0, c1 in splits]
    return pl.pallas_call(
        functools.partial(_linear_kernel, splits=tuple(splits), n_chunk=n_chunk),
        grid=(t // tm,),
        in_specs=[pl.BlockSpec((tm, k), lambda i: (i, 0)),
                  pl.BlockSpec((k, n), lambda i: (0, 0))],
        out_specs=[pl.BlockSpec((tm, c1 - c0), lambda i: (i, 0)) for c0, c1 in splits],
        out_shape=outs,
        compiler_params=_cparams(("parallel",), 48),
        name=name,
    )(x2d, w)


def _conv_gate_kernel(x_ref, xh_ref, wc_ref, cw_ref, wg_ref, ob_ref, g_ref, *, tm):
    i = pl.program_id(1)
    w = CONV_WIDTH
    xb = x_ref[...].astype(CDT)
    z = _dot(xb, wc_ref[...])
    u = z[:, w:2 * w] * z[:, 2 * w:3 * w]
    zh = _dot(xh_ref[...].astype(CDT), wc_ref[:, w:3 * w])
    uh = zh[:, :w] * zh[:, w:]
    uh = jnp.where(i == 0, 0.0, uh)
    row = lax.broadcasted_iota(jnp.int32, (tm, w), 0)
    u1 = jnp.where(row == 0, uh[7:8, :], pltpu.roll(u, 1, 0))
    u2 = jnp.where(row == 0, uh[6:7, :], jnp.where(row == 1, uh[7:8, :], pltpu.roll(u, 2, 0)))
    y = cw_ref[0:1, :] * u2 + cw_ref[1:2, :] * u1 + cw_ref[2:3, :] * u
    ob_ref[...] = (z[:, :w] * y).astype(ob_ref.dtype)
    g_ref[...] = jax.nn.sigmoid(_dot(xb, wg_ref[...]))


def _conv_gate(x, wc, conv_w, wg, tm=512):
    b, s, d = x.shape
    tm = min(tm, s)
    hb = tm // 8
    return pl.pallas_call(
        functools.partial(_conv_gate_kernel, tm=tm),
        grid=(b, s // tm),
        in_specs=[pl.BlockSpec((None, tm, d), lambda bi, i: (bi, i, 0)),
                  pl.BlockSpec((None, 8, d), lambda bi, i: (bi, jnp.maximum(i * hb - 1, 0), 0)),
                  pl.BlockSpec(wc.shape, lambda bi, i: (0, 0)),
                  pl.BlockSpec(conv_w.shape, lambda bi, i: (0, 0)),
                  pl.BlockSpec(wg.shape, lambda bi, i: (0, 0))],
        out_specs=[pl.BlockSpec((None, tm, CONV_WIDTH), lambda bi, i: (bi, i, 0)),
                   pl.BlockSpec((None, tm, wg.shape[1]), lambda bi, i: (bi, i, 0))],
        out_shape=[jax.ShapeDtypeStruct((b, s, CONV_WIDTH), CDT),
                   jax.ShapeDtypeStruct((b, s, wg.shape[1]), F32)],
        compiler_params=_cparams(("parallel", "parallel"), 48),
        name="conv_gate",
    )(x, x, wc, conv_w, wg)


def _banded_kernel(*refs, window, pr, tq, kw, want_lse, has_sink):
    q_ref, kp_ref, kc_ref, vp_ref, vc_ref = refs[:5]
    n = 5
    sink_ref = None
    if has_sink:
        sink_ref = refs[n]
        n += 1
    o_ref = refs[n]
    n += 1
    lse_ref = None
    if want_lse:
        lse_ref = refs[n]
        n += 1
    kbuf, vbuf = refs[n], refs[n + 1]

    i = pl.program_id(2)
    kbuf[0:pr, :] = kp_ref[...]
    kbuf[pr:pr + tq, :] = kc_ref[...]
    vbuf[0:pr, :] = vp_ref[...]
    vbuf[pr:pr + tq, :] = vc_ref[...]

    span = SUB_Q + pr
    qi = lax.broadcasted_iota(jnp.int32, (SUB_Q, span), 0)
    kj = lax.broadcasted_iota(jnp.int32, (SUB_Q, span), 1)
    dist = pr + qi - kj
    band = (dist >= 0) & (dist <= window)
    lo = _half_masks()
    for sb in range(tq // SUB_Q):
        r0 = sb * SUB_Q
        mask = band & (i * tq + r0 - pr + kj >= 0)
        for p in range(3):
            c0 = p * LANES
            kc0 = c0 if kw == 3 * LANES else 0
            qp = q_ref[r0:r0 + SUB_Q, c0:c0 + LANES]
            kk = kbuf[r0:r0 + span, kc0:kc0 + LANES]
            vv = vbuf[r0:r0 + span, kc0:kc0 + LANES]
            o_half, lse_half = [], []
            for hm in (lo, jnp.logical_not(lo)):
                qm = (jnp.where(hm, qp, 0) * QK_SCALE).astype(CDT)
                s = jnp.where(mask, _nt_dot(qm, kk), NEG_INF)
                m = jnp.max(s, axis=-1, keepdims=True)
                e = jnp.exp(s - m)
                l = jnp.sum(e, axis=-1, keepdims=True)
                o_half.append(_dot(e.astype(CDT), vv) / l)
                lse_half.append(m + jnp.log(l))
            o_pair = jnp.where(lo, o_half[0], o_half[1])
            lse_pair = jnp.where(lo, lse_half[0], lse_half[1])
            if has_sink:
                o_pair = o_pair * jax.nn.sigmoid(lse_pair - sink_ref[:, c0:c0 + LANES])
            o_ref[r0:r0 + SUB_Q, c0:c0 + LANES] = o_pair.astype(o_ref.dtype)
            if want_lse:
                lse_ref[r0:r0 + SUB_Q, c0:c0 + LANES] = lse_pair


def _banded(qa, ka, va, *, nrep, qcol, kcol, vcol, kw, window, want_lse, sink_row=None, tq=256):
    b, l, _ = qa.shape
    pr = -(-window // SUB_Q) * SUB_Q
    tq = min(max(tq, pr), l)
    assert tq % pr == 0 and l % tq == 0, (tq, pr, l)
    ratio = tq // pr
    qw = 3 * LANES
    in_specs = [
        pl.BlockSpec((None, tq, qw), lambda bi, r, i: (bi, i, qcol(r))),
        pl.BlockSpec((None, pr, kw), lambda bi, r, i: (bi, jnp.maximum(i * ratio - 1, 0), kcol(r))),
        pl.BlockSpec((None, tq, kw), lambda bi, r, i: (bi, i, kcol(r))),
        pl.BlockSpec((None, pr, kw), lambda bi, r, i: (bi, jnp.maximum(i * ratio - 1, 0), vcol(r))),
        pl.BlockSpec((None, tq, kw), lambda bi, r, i: (bi, i, vcol(r))),
    ]
    args = [qa, ka, ka, va, va]
    if sink_row is not None:
        in_specs.append(pl.BlockSpec(sink_row.shape, lambda bi, r, i: (0, 0)))
        args.append(sink_row)
    out_specs = [pl.BlockSpec((None, tq, qw), lambda bi, r, i: (bi, i, r))]
    out_shape = [jax.ShapeDtypeStruct((b, l, nrep * qw), CDT)]
    if want_lse:
        out_specs.append(pl.BlockSpec((None, tq, qw), lambda bi, r, i: (bi, i, r)))
        out_shape.append(jax.ShapeDtypeStruct((b, l, nrep * qw), F32))
    res = pl.pallas_call(
        functools.partial(_banded_kernel, window=window, pr=pr, tq=tq, kw=kw, want_lse=want_lse,
                          has_sink=sink_row is not None),
        grid=(b, nrep, l // tq),
        in_specs=in_specs,
        out_specs=out_specs,
        out_shape=out_shape,
        scratch_shapes=[pltpu.VMEM((pr + tq, kw), ka.dtype), pltpu.VMEM((pr + tq, kw), va.dtype)],
        compiler_params=_cparams(("parallel", "parallel", "parallel"), 32),
        name=f"banded_w{window}_k{kw}_r{nrep}",
    )(*args)
    return res


def _gelu_tanh(x):
    return 0.5 * x * (1.0 + jnp.tanh(0.7978845608028654 * (x + 0.044715 * (x * x * x))))


def _compress_kernel(x_ref, pa_ref, pb_ref, w1a_ref, w1b_ref, b1_ref, w2_ref, b2_ref, o_ref):
    x = x_ref[...].astype(F32)
    n = x.shape[0]
    a = _dot((x + pa_ref[...]).astype(CDT), w1a_ref[...])
    bm = _dot((x + pb_ref[...]).astype(CDT), w1b_ref[...])
    h = a + pltpu.roll(bm, n - 1, 0) + b1_ref[...]
    o_ref[...] = (_dot(_gelu_tanh(h).astype(CDT), w2_ref[...]) + b2_ref[...]).astype(o_ref.dtype)


def _compress(t, pos, w1, b1, w2, b2):
    b, s, _ = t.shape
    nch = s // CMP_STRIDE
    xw = CMP_STRIDE * LANES
    x = t.reshape(b, nch, xw)
    eye = jnp.eye(NSA_KV_HEADS, dtype=F32)
    w1r = w1.reshape(CMP_BLOCK, HEAD_DIM, CMP_HIDDEN)

    def expand_w1(part):
        return jnp.einsum('tdj,kl->tkdlj', part, eye).reshape(xw, NSA_KV_HEADS * CMP_HIDDEN).astype(CDT)

    def expand_pos(part):
        return jnp.broadcast_to(part[:, None, :], (CMP_STRIDE, NSA_KV_HEADS, HEAD_DIM)).reshape(1, xw)

    w1a, w1b = expand_w1(w1r[:CMP_STRIDE]), expand_w1(w1r[CMP_STRIDE:])
    pa, pb = expand_pos(pos[:CMP_STRIDE]), expand_pos(pos[CMP_STRIDE:])
    b1e = jnp.tile(b1, NSA_KV_HEADS).reshape(1, -1)
    w2e = jnp.einsum('jd,kl->kjld', w2, eye).reshape(NSA_KV_HEADS * CMP_HIDDEN, LANES).astype(CDT)
    b2e = jnp.tile(b2, NSA_KV_HEADS).reshape(1, -1)
    consts = [pa, pb, w1a, w1b, b1e, w2e, b2e]
    return pl.pallas_call(
        _compress_kernel,
        grid=(b,),
        in_specs=[pl.BlockSpec((None, nch, xw), lambda bi: (bi, 0, 0))]
        + [pl.BlockSpec(c.shape, lambda bi: (0, 0)) for c in consts],
        out_specs=pl.BlockSpec((None, nch, LANES), lambda bi: (bi, 0, 0)),
        out_shape=jax.ShapeDtypeStruct((b, nch, LANES), CDT),
        compiler_params=_cparams(("parallel",), 48),
        name="nsa_compress",
    )(x, *consts)


def _cmp_select_kernel(q_ref, kc_ref, vc_ref, ov_ref, o_ref, sel_ref, *, tq, n_sel):
    i = pl.program_id(1)
    ncp = kc_ref.shape[0]
    ns = ov_ref.shape[1]
    lo = _half_masks()
    t_col = i * tq + lax.broadcasted_iota(jnp.int32, (tq, 1), 0)
    c_end = lax.broadcasted_iota(jnp.int32, (tq, ncp), 1) * CMP_STRIDE + (CMP_BLOCK - 1)
    vis_bias = jnp.where(c_end <= t_col, 0.0, NEG_INF)
    has_visible = t_col >= CMP_BLOCK - 1
    blk = lax.broadcasted_iota(jnp.int32, (tq, ns), 1)
    blk_t = lax.broadcasted_iota(jnp.int32, (ns, tq), 0)
    cur = t_col // SEL_BLOCK
    causal = blk <= cur
    forced = (blk == 0) | (blk == cur) | (blk == cur - 1)
    kc = kc_ref[...]
    vc = vc_ref[...]
    o_kv = []
    for kv, hm in enumerate((lo, jnp.logical_not(lo))):
        psum = jnp.zeros((tq, ncp), F32)
        o_p = []
        for p in range(3):
            qm = (jnp.where(hm, q_ref[:, p * LANES:(p + 1) * LANES], 0) * QK_SCALE).astype(CDT)
            s = _nt_dot(qm, kc) + vis_bias
            m = jnp.max(s, axis=-1, keepdims=True)
            e = jnp.exp(s - m)
            inv = jnp.where(has_visible, 1.0 / jnp.maximum(jnp.sum(e, axis=-1, keepdims=True), 1e-30), 0.0)
            pn = e * inv
            o_p.append(_dot(pn.astype(CDT), vc))
            psum = psum + pn
        o_kv.append(o_p)
        p_hi = psum.astype(CDT)
        p_lo = (psum - p_hi.astype(F32)).astype(CDT)
        imp = _dot(p_hi, ov_ref[...]) + _dot(p_lo, ov_ref[...])
        work_t = jnp.where(causal & jnp.logical_not(forced), imp, -BIG).T

        def pick(_, carry):
            work, sel = carry
            m = jnp.max(work, axis=0, keepdims=True)
            idx = jnp.min(jnp.where(work == m, blk_t, ns), axis=0, keepdims=True)
            hit = blk_t == idx
            sel = jnp.where(hit & (m > -BIG), 1.0, sel)
            return jnp.where(hit, -2.0 * BIG, work), sel

        _, sel_t = lax.fori_loop(0, n_sel - 3, pick, (work_t, jnp.zeros((ns, tq), F32)))
        selb = jnp.where(forced | (sel_t.T > 0.5), 0.0, NEG_INF)
        sel_ref[:, kv * ns:(kv + 1) * ns] = selb.astype(sel_ref.dtype)
    for p in range(3):
        o_ref[:, p * LANES:(p + 1) * LANES] = jnp.where(lo, o_kv[0][p], o_kv[1][p]).astype(o_ref.dtype)


def _cmp_select(zr, qcol, kc, vc, tq=128):
    b, s, _ = zr.shape
    ncp = kc.shape[1]
    ns = s // SEL_BLOCK
    n_sel = min(N_SEL, ns)
    assert n_sel >= 3, "selection needs room for the three forced blocks"
    tq = min(tq, s)
    c = np.arange(ncp)[:, None] * CMP_STRIDE
    j = np.arange(ns)[None, :] * SEL_BLOCK
    overlap = ((c < j + SEL_BLOCK) & (c + CMP_BLOCK - 1 >= j)).astype(np.float32)
    overlap[ncp - 1:, :] = 0.0
    ov = jnp.asarray(overlap, CDT)
    qw = 3 * LANES
    return pl.pallas_call(
        functools.partial(_cmp_select_kernel, tq=tq, n_sel=n_sel),
        grid=(b, s // tq),
        in_specs=[pl.BlockSpec((None, tq, qw), lambda bi, i: (bi, i, qcol)),
                  pl.BlockSpec((None, ncp, LANES), lambda bi, i: (bi, 0, 0)),
                  pl.BlockSpec((None, ncp, LANES), lambda bi, i: (bi, 0, 0)),
                  pl.BlockSpec(ov.shape, lambda bi, i: (0, 0))],
        out_specs=[pl.BlockSpec((None, tq, qw), lambda bi, i: (bi, i, 0)),
                   pl.BlockSpec((None, tq, 2 * ns), lambda bi, i: (bi, i, 0))],
        out_shape=[jax.ShapeDtypeStruct((b, s, qw), CDT),
                   jax.ShapeDtypeStruct((b, s, 2 * ns), CDT)],
        compiler_params=_cparams(("parallel", "parallel"), 48),
        name="nsa_cmp_select",
    )(zr, kc, vc, ov)


SLC_ROWS = 64


def _slc_kernel(q_ref, k0_ref, k1_ref, v_ref, selb_ref, o_ref, s_ref, e_ref, m_ref, l_ref, a_ref, acc_ref,
                *, tq, tk):
    i = pl.program_id(1)
    ns = selb_ref.shape[1] // 2
    bpt = tk // SEL_BLOCK
    rb = SLC_ROWS
    lo = _half_masks()
    n_tiles = ((i + 1) * tq + tk - 1) // tk
    p_row = lax.broadcasted_iota(jnp.int32, (ns, LANES), 0)
    p_col = lax.broadcasted_iota(jnp.int32, (ns, LANES), 1)
    groups = []
    for kv, (hm, k_ref) in enumerate(((lo, k0_ref), (jnp.logical_not(lo), k1_ref))):
        q3 = [(jnp.where(hm, q_ref[:, p * LANES:(p + 1) * LANES], 0) * QK_SCALE).astype(CDT) for p in range(3)]
        selb = selb_ref[:, kv * ns:(kv + 1) * ns]
        lane0 = HEAD_DIM if kv == 0 else 0
        groups.append((kv, hm, k_ref, q3, selb, lane0))
    m_ref[...] = jnp.full(m_ref.shape, NEG_INF, F32)
    l_ref[...] = jnp.zeros(l_ref.shape, F32)
    acc_ref[...] = jnp.zeros(acc_ref.shape, F32)

    def scores(j):
        k0 = pl.multiple_of(j * tk, tk)
        for kv, hm, k_ref, q3, selb, lane0 in groups:
            place = ((p_col >= lane0) & (p_col < lane0 + bpt) & (p_row == p_col - lane0 + j * bpt)).astype(CDT)
            sb = _dot(selb, place).astype(CDT)
            qp = jnp.concatenate([jnp.where(hm, q, sb) for q in q3], axis=0)
            s_ref[kv] = _nt_dot(qp, k_ref[pl.ds(k0, tk), :])

    def softmax_pv(j, diagonal):
        k0 = pl.multiple_of(j * tk, tk)
        for kv in range(2):
            for r0 in range(0, 3 * tq, rb):
                rows = slice(r0, r0 + rb)
                s = s_ref[kv, rows, :]
                if diagonal:
                    t_row = i * tq + (r0 % tq) + lax.broadcasted_iota(jnp.int32, (rb, 1), 0)
                    kpos = k0 + lax.broadcasted_iota(jnp.int32, (rb, tk), 1)
                    s = jnp.where(kpos <= t_row, s, NEG_INF)
                m_old = m_ref[kv, rows, :]
                m_new = jnp.maximum(m_old, jnp.max(s, axis=-1, keepdims=True))
                alpha = jnp.exp(m_old - m_new)
                e = jnp.exp(s - jnp.tile(m_new, (1, tk // LANES)))
                l_ref[kv, rows, :] = alpha * l_ref[kv, rows, :] + jnp.sum(e, axis=-1, keepdims=True)
                m_ref[kv, rows, :] = m_new
                a_ref[kv, rows, :] = alpha
                e_ref[kv, rows, :] = e.astype(CDT)
        for kv in range(2):
            acc_ref[kv] = a_ref[kv] * acc_ref[kv] + _dot(e_ref[kv], v_ref[pl.ds(k0, tk), :])

    def tile(j, diagonal):
        scores(j)
        softmax_pv(j, diagonal)

    lax.fori_loop(0, n_tiles - 1, lambda j, c: (tile(j, False), c)[1], 0)
    tile(n_tiles - 1, True)
    o0 = acc_ref[0] / l_ref[0]
    o1 = acc_ref[1] / l_ref[1]
    for p in range(3):
        o_ref[:, p * LANES:(p + 1) * LANES] = jnp.where(
            lo, o0[p * tq:(p + 1) * tq], o1[p * tq:(p + 1) * tq]).astype(o_ref.dtype)


def _slc(zr, qcol, ksc, vcol, selb, tq=256, tk=512):
    b, s, _ = zr.shape
    tq = min(tq, s)
    tk = min(tk, s)
    assert tq % SLC_ROWS == 0 and tk // SEL_BLOCK <= HEAD_DIM
    qw = 3 * LANES
    ns2 = selb.shape[2]
    pat = jax.nn.one_hot((jnp.arange(s) // SEL_BLOCK) % (tk // SEL_BLOCK), HEAD_DIM, dtype=ksc.dtype)
    pat = jnp.broadcast_to(pat[None], (b, s, HEAD_DIM))
    k0 = jnp.concatenate([ksc[..., :HEAD_DIM], pat], axis=-1)
    k1 = jnp.concatenate([pat, ksc[..., HEAD_DIM:]], axis=-1)
    full = lambda bi, i: (bi, 0, 0)
    return pl.pallas_call(
        functools.partial(_slc_kernel, tq=tq, tk=tk),
        grid=(b, s // tq),
        in_specs=[pl.BlockSpec((None, tq, qw), lambda bi, i: (bi, i, qcol)),
                  pl.BlockSpec((None, s, LANES), full),
                  pl.BlockSpec((None, s, LANES), full),
                  pl.BlockSpec((None, s, LANES), lambda bi, i: (bi, 0, vcol)),
                  pl.BlockSpec((None, tq, ns2), lambda bi, i: (bi, i, 0))],
        out_specs=pl.BlockSpec((None, tq, qw), lambda bi, i: (bi, i, 0)),
        out_shape=jax.ShapeDtypeStruct((b, s, qw), CDT),
        scratch_shapes=[pltpu.VMEM((2, 3 * tq, tk), F32), pltpu.VMEM((2, 3 * tq, tk), CDT),
                        pltpu.VMEM((2, 3 * tq, LANES), F32), pltpu.VMEM((2, 3 * tq, LANES), F32),
                        pltpu.VMEM((2, 3 * tq, LANES), F32), pltpu.VMEM((2, 3 * tq, LANES), F32)],
        compiler_params=_cparams(("parallel", "arbitrary"), 56),
        name="nsa_slc",
    )(zr, k0, k1, zr, selb)


def _merge_kernel(x_ref, oa0, oa1, oa2, la0, la1, la2, ob, ocmp, oslc, owin, gates, od,
                  wg_ref, bg_ref, wb_ref, wo_ref, g_ref, b_ref, o_ref, *, alpha):
    x = x_ref[...]
    xb = x.astype(CDT)
    bw = BRANCH_WIDTH
    l0, l1, l2 = la0[...], la1[...], la2[...]
    mx = jnp.maximum(jnp.maximum(l0, l1), l2)
    w0, w1, w2 = jnp.exp(l0 - mx), jnp.exp(l1 - mx), jnp.exp(l2 - mx)
    o_a = (w0 * oa0[...].astype(F32) + w1 * oa1[...].astype(F32) + w2 * oa2[...].astype(F32)) / (w0 + w1 + w2)
    o_c = (gates[:, 0:bw] * ocmp[...].astype(F32) + gates[:, bw:2 * bw] * oslc[...].astype(F32)
           + gates[:, 2 * bw:3 * bw] * owin[...].astype(F32))
    branches = (o_a.astype(CDT), ob[...], o_c.astype(CDT), od[...])
    d = x.shape[1]
    merged = jnp.zeros(x.shape, F32)
    for m in range(N_BRANCH):
        gate = jax.nn.sigmoid(_dot(xb, wg_ref[:, m * d:(m + 1) * d]) + bg_ref[:, m * d:(m + 1) * d])
        merged = merged + gate * _dot(branches[m], wb_ref[m])
    r = alpha * x + _dot(merged.astype(CDT), wo_ref[...])
    o_ref[...] = _layer_norm(r, g_ref[...], b_ref[...])


def _merge(x2d, branch_inputs, wg, bg, wb, wo, g, b, alpha, tm=256):
    t, d = x2d.shape
    tm = min(tm, t)
    row = lambda i: (i, 0)
    const2 = lambda i: (0, 0)
    in_specs = [pl.BlockSpec((tm, d), row)]
    in_specs += [pl.BlockSpec((tm, a.shape[1]), row) for a in branch_inputs]
    in_specs += [pl.BlockSpec(wg.shape, const2), pl.BlockSpec(bg.shape, const2),
                 pl.BlockSpec(wb.shape, lambda i: (0, 0, 0)), pl.BlockSpec(wo.shape, const2),
                 pl.BlockSpec(g.shape, const2), pl.BlockSpec(b.shape, const2)]
    return pl.pallas_call(
        functools.partial(_merge_kernel, alpha=alpha),
        grid=(t // tm,),
        in_specs=in_specs,
        out_specs=pl.BlockSpec((tm, d), row),
        out_shape=jax.ShapeDtypeStruct((t, d), F32),
        compiler_params=_cparams(("parallel",), 56),
        name="merge_ln",
    )(x2d, *branch_inputs, wg, bg, wb, wo, g, b)


def _ple_ln(x, xb, f, p, plw_ref, pgw_ref, pgb_ref, g_ref, b_ref, alpha):
    ple = jax.nn.sigmoid(_dot(xb, pgw_ref[...]) + pgb_ref[...]) * _dot(p.astype(CDT), plw_ref[...])
    return _layer_norm(alpha * x + f + ple, g_ref[...], b_ref[...])


def _ffn_kernel(x_ref, p_ref, wg_ref, wu_ref, wd_ref, plw_ref, pgw_ref, pgb_ref, g_ref, b_ref, o_ref, *, alpha):
    x = x_ref[...]
    xb = x.astype(CDT)

    def chunk(c, acc):
        h = jax.nn.silu(_dot(xb, wg_ref[c])) * _dot(xb, wu_ref[c])
        return acc + _dot(h.astype(CDT), wd_ref[c])

    f = lax.fori_loop(0, wg_ref.shape[0], chunk, jnp.zeros(x.shape, F32))
    o_ref[...] = _ple_ln(x, xb, f, p_ref[...], plw_ref, pgw_ref, pgb_ref, g_ref, b_ref, alpha)


FFN_CHUNK = 256


def _ffn(x2d, p2d, wg, wu, wd, plw, pgw, pgb, g, b, alpha, tm=512):
    t, d = x2d.shape
    tm = min(tm, t)
    dff = wg.shape[1]
    nck = dff // FFN_CHUNK
    wg3 = wg.reshape(d, nck, FFN_CHUNK).transpose(1, 0, 2).astype(CDT)
    wu3 = wu.reshape(d, nck, FFN_CHUNK).transpose(1, 0, 2).astype(CDT)
    wd3 = wd.reshape(nck, FFN_CHUNK, d).astype(CDT)
    row = lambda i: (i, 0)
    c2 = lambda i: (0, 0)
    c3 = lambda i: (0, 0, 0)
    return pl.pallas_call(
        functools.partial(_ffn_kernel, alpha=alpha),
        grid=(t // tm,),
        in_specs=[pl.BlockSpec((tm, d), row), pl.BlockSpec((tm, p2d.shape[1]), row),
                  pl.BlockSpec(wg3.shape, c3), pl.BlockSpec(wu3.shape, c3), pl.BlockSpec(wd3.shape, c3),
                  pl.BlockSpec(plw.shape, c2), pl.BlockSpec(pgw.shape, c2), pl.BlockSpec(pgb.shape, c2),
                  pl.BlockSpec(g.shape, c2), pl.BlockSpec(b.shape, c2)],
        out_specs=pl.BlockSpec((tm, d), row),
        out_shape=jax.ShapeDtypeStruct((t, d), F32),
        compiler_params=_cparams(("parallel",), 60),
        name="ffn_ple_ln",
    )(x2d, p2d, wg3, wu3, wd3, plw, pgw, pgb, g, b)


def _router_kernel(x_ref, wh_ref, wl_ref, b_ref, comb_ref, rank_ref, rank_t_ref, cnt_ref):
    x = x_ref[...]
    xh = x.astype(CDT)
    xl = (x - xh.astype(F32)).astype(CDT)
    logits = _dot(xh, wh_ref[...]) + _dot(xh, wl_ref[...]) + _dot(xl, wh_ref[...]) + b_ref[...]
    lane = lax.broadcasted_iota(jnp.int32, logits.shape, 1)
    v1 = jnp.max(logits, axis=-1, keepdims=True)
    i1 = jnp.min(jnp.where(logits == v1, lane, LANES), axis=-1, keepdims=True)
    rest = jnp.where(lane == i1, -jnp.inf, logits)
    v2 = jnp.max(rest, axis=-1, keepdims=True)
    i2 = jnp.min(jnp.where(rest == v2, lane, LANES), axis=-1, keepdims=True)
    e2 = jnp.exp(v2 - v1)
    comb_ref[...] = jnp.where(lane == i1, 1.0 / (1.0 + e2), 0.0) + jnp.where(lane == i2, e2 / (1.0 + e2), 0.0)
    routed = (lane == i1) | (lane == i2)
    mask = routed.astype(CDT)
    tm = x.shape[0]
    before = (lax.broadcasted_iota(jnp.int32, (tm, tm), 1) < lax.broadcasted_iota(jnp.int32, (tm, tm), 0)).astype(CDT)
    rank = jnp.where(routed, _dot(before, mask), -1.0)
    rank_ref[...] = rank
    rank_t_ref[...] = rank.T[0:rank_t_ref.shape[0], :]
    cnt_ref[...] = jnp.sum(routed.astype(F32), axis=0, keepdims=True).astype(jnp.int32)


def _router(x2d, w_router, b_router, tm):
    t, d = x2d.shape
    ne = w_router.shape[1]
    wp = jnp.zeros((d, LANES), F32).at[:, :ne].set(w_router)
    wh = wp.astype(CDT)
    wl = (wp - wh.astype(F32)).astype(CDT)
    bp = jnp.full((1, LANES), -BIG, F32).at[0, :ne].set(b_router)
    nt = t // tm
    row = lambda i: (i, 0)
    return pl.pallas_call(
        _router_kernel,
        grid=(nt,),
        in_specs=[pl.BlockSpec((tm, d), row), pl.BlockSpec(wh.shape, lambda i: (0, 0)),
                  pl.BlockSpec(wl.shape, lambda i: (0, 0)), pl.BlockSpec(bp.shape, lambda i: (0, 0))],
        out_specs=[pl.BlockSpec((tm, LANES), row), pl.BlockSpec((tm, LANES), row),
                   pl.BlockSpec((None, 8, tm), lambda i: (i, 0, 0)),
                   pl.BlockSpec((None, 1, LANES), lambda i: (i, 0, 0))],
        out_shape=[jax.ShapeDtypeStruct((t, LANES), F32), jax.ShapeDtypeStruct((t, LANES), F32),
                   jax.ShapeDtypeStruct((nt, 8, tm), F32), jax.ShapeDtypeStruct((nt, 1, LANES), jnp.int32)],
        compiler_params=_cparams(("parallel",), 40),
        name="moe_router",
    )(x2d, wh, wl, bp)


def _moe_kernel(cnt_ref, x_ref, comb_ref, rank_ref, rank_t_ref, p_ref, wg_ref, wu_ref, wd_ref, plw_ref, pgw_ref,
                pgb_ref, g_ref, b_ref, o_ref, xe_ref, ye_ref, *, alpha, rs, seg):
    i = pl.program_id(0)
    e = pl.program_id(1)
    c = pl.program_id(2)
    n_seg = x_ref.shape[0] // seg
    last_chunk = c == pl.num_programs(2) - 1

    def rows(sc):
        return pl.ds(pl.multiple_of(sc * rs, 8), rs)

    @pl.when((e == 0) & (c == 0))
    def _():
        o_ref[...] = jnp.zeros_like(o_ref)

    for sg in range(n_seg):
        tok = slice(sg * seg, (sg + 1) * seg)
        n_groups = (cnt_ref[(i * n_seg + sg) * LANES + e] + rs - 1) // rs

        @pl.when(c == 0)
        def _():
            xb = x_ref[tok, :].astype(CDT)
            rank_row = rank_t_ref[sg, pl.ds(e, 1), :]
            row_id = lax.broadcasted_iota(jnp.int32, (rs, seg), 0).astype(F32)

            def gather(sc, _):
                onehot = (rank_row - (sc * rs).astype(F32) == row_id).astype(CDT)
                xe_ref[sg, rows(sc), :] = _dot(onehot, xb).astype(CDT)
                ye_ref[sg, rows(sc), :] = jnp.zeros((rs, ye_ref.shape[2]), F32)
                return 0

            lax.fori_loop(0, n_groups, gather, 0)

        def expert(sc, _):
            xs = xe_ref[sg, rows(sc), :]
            h = jax.nn.silu(_dot(xs, wg_ref[0])) * _dot(xs, wu_ref[0])
            ye_ref[sg, rows(sc), :] += _dot(h.astype(CDT), wd_ref[0])
            return 0

        lax.fori_loop(0, n_groups, expert, 0)

        @pl.when(last_chunk)
        def _():
            lane = lax.broadcasted_iota(jnp.int32, (seg, LANES), 1)
            mine = lane == e
            cw = jnp.sum(jnp.where(mine, comb_ref[tok, :], 0.0), axis=-1, keepdims=True)
            rank_col = jnp.sum(jnp.where(mine, rank_ref[tok, :], 0.0), axis=-1, keepdims=True)
            col_id = lax.broadcasted_iota(jnp.int32, (seg, rs), 1).astype(F32)

            def scatter(sc, _):
                onehot = (rank_col - (sc * rs).astype(F32) == col_id).astype(CDT)
                o_ref[tok, :] += cw * _dot(onehot, ye_ref[sg, rows(sc), :].astype(CDT))
                return 0

            lax.fori_loop(0, n_groups, scatter, 0)

    @pl.when((e == pl.num_programs(1) - 1) & last_chunk)
    def _():
        for sg in range(n_seg):
            tok = slice(sg * seg, (sg + 1) * seg)
            x = x_ref[tok, :]
            o_ref[tok, :] = _ple_ln(x, x.astype(CDT), o_ref[tok, :], p_ref[tok, :], plw_ref, pgw_ref, pgb_ref,
                                    g_ref, b_ref, alpha)


MOE_CHUNK = 512
MOE_SEGMENT = 1024
MOE_SEGMENTS_PER_TILE = 2
MOE_ROW_GROUP = 288


def _single_buffered(shape, index_map):
    return pl.BlockSpec(shape, index_map, pipeline_mode=pl.Buffered(1))


def _moe(x2d, routing, p2d, wg, wu, wd, plw, pgw, pgb, g, b, alpha, seg):
    comb, rank, rank_t, cnt = routing
    t, d = x2d.shape
    ne, _, dff = wg.shape
    ck = min(MOE_CHUNK, dff)
    rs = min(MOE_ROW_GROUP, seg)
    n_seg = min(MOE_SEGMENTS_PER_TILE, t // seg)
    tm = n_seg * seg
    max_rows = -(-seg // rs) * rs
    row = lambda i, e, c, cnt: (i, 0)
    c2 = lambda i, e, c, cnt: (0, 0)
    grid_spec = pltpu.PrefetchScalarGridSpec(
        num_scalar_prefetch=1,
        grid=(t // tm, ne, dff // ck),
        in_specs=[_single_buffered((tm, d), row), _single_buffered((tm, LANES), row),
                  _single_buffered((tm, LANES), row),
                  _single_buffered((n_seg, 8, seg), lambda i, e, c, cnt: (i, 0, 0)),
                  _single_buffered((tm, p2d.shape[1]), row),
                  pl.BlockSpec((1, d, ck), lambda i, e, c, cnt: (e, 0, c)),
                  pl.BlockSpec((1, d, ck), lambda i, e, c, cnt: (e, 0, c)),
                  pl.BlockSpec((1, ck, d), lambda i, e, c, cnt: (e, c, 0)),
                  _single_buffered(plw.shape, c2), _single_buffered(pgw.shape, c2), _single_buffered(pgb.shape, c2),
                  _single_buffered(g.shape, c2), _single_buffered(b.shape, c2)],
        out_specs=pl.BlockSpec((tm, d), row),
        scratch_shapes=[pltpu.VMEM((n_seg, max_rows, d), CDT), pltpu.VMEM((n_seg, max_rows, d), F32)],
    )
    return pl.pallas_call(
        functools.partial(_moe_kernel, alpha=alpha, rs=rs, seg=seg),
        grid_spec=grid_spec,
        out_shape=jax.ShapeDtypeStruct((t, d), F32),
        compiler_params=_cparams(("parallel", "arbitrary", "arbitrary"), 60),
        name="moe_ple_ln",
    )(cnt.reshape(-1), x2d, comb, rank, rank_t, p2d, wg, wu, wd, plw, pgw, pgb, g, b)


def _prep_in_weights(w_in):
    o = COL_OFF
    bw = BRANCH_WIDTH
    cols = lambda n: w_in[:, o[n]:o[n + 1]]
    qa, ka, va = cols(0), cols(1), cols(2)
    w_dil = jnp.concatenate(
        [t[:, g * bw:(g + 1) * bw] for g in range(N_DIL) for t in (qa, ka, va)], axis=1).astype(CDT)
    w_conv = jnp.concatenate([cols(3), cols(4), cols(5)], axis=1).astype(CDT)
    gn = cols(13)
    w_gate = jnp.concatenate([gn[:, br * NSA_Q_HEADS + GQA_COL_HEAD] for br in range(3)], axis=1).astype(CDT)
    w_rest = jnp.concatenate([cols(6)[:, GQA_COL_PERM], cols(14)[:, GQA_COL_PERM], cols(10), cols(11),
                              cols(12), cols(15), cols(16), cols(9), cols(7), cols(8)], axis=1).astype(CDT)
    return w_dil, w_conv, w_gate, w_rest


ZR_Q_NSA, ZR_Q_SWA = 0, 1
ZR_VSC, ZR_KWC, ZR_VWC, ZR_KD, ZR_VD = 6, 7, 8, 9, 10
ZR_WIDTH = 2 * BRANCH_WIDTH + 5 * LANES


def _token_mixers(x, w_in, conv_w, cmp_pos, cmp_w1, cmp_b1, cmp_w2, cmp_b2, sinks):
    b, s, d = x.shape
    x2d = x.reshape(b * s, d)
    w_dil, w_conv, w_gate, w_rest = _prep_in_weights(w_in)
    gw = 3 * BRANCH_WIDTH

    z_dil = _linear(x2d, w_dil, [(g * gw, (g + 1) * gw) for g in range(N_DIL)], gw, "in_proj_dil")
    zr, ksc, kcc, vcc = _linear(x2d, w_rest, [(0, ZR_WIDTH)] + [(ZR_WIDTH + n * LANES, ZR_WIDTH + (n + 1) * LANES)
                                                           for n in range(3)], 256, "in_proj_rest")
    zr = zr.reshape(b, s, ZR_WIDTH)
    o_b, gates = _conv_gate(x, w_conv, conv_w, w_gate)

    dil_o, dil_lse = [], []
    for g, (window, dil) in enumerate(DIL_PATTERNS):
        view = z_dil[g].reshape(b, s // dil, dil * gw)
        og, lg = _banded(view, view, view, nrep=dil,
                         qcol=lambda r: 3 * r, kcol=lambda r: 3 * r + 1, vcol=lambda r: 3 * r + 2,
                         kw=3 * LANES, window=window // dil, want_lse=True)
        dil_o.append(og.reshape(b * s, BRANCH_WIDTH))
        dil_lse.append(lg.reshape(b * s, BRANCH_WIDTH))

    kc = _compress(kcc.reshape(b, s, LANES), cmp_pos[0], cmp_w1[0], cmp_b1[0], cmp_w2[0], cmp_b2[0])
    vc = _compress(vcc.reshape(b, s, LANES), cmp_pos[1], cmp_w1[1], cmp_b1[1], cmp_w2[1], cmp_b2[1])
    o_cmp, selb = _cmp_select(zr, ZR_Q_NSA, kc, vc)
    o_slc = _slc(zr, ZR_Q_NSA, ksc.reshape(b, s, LANES), ZR_VSC, selb)
    (o_win,) = _banded(zr, zr, zr, nrep=1, qcol=lambda r: ZR_Q_NSA, kcol=lambda r: ZR_KWC, vcol=lambda r: ZR_VWC,
                       kw=LANES, window=NSA_WINDOW - 1, want_lse=False, tq=512)

    sink_row = sinks.astype(F32)[GQA_COL_HEAD].reshape(1, BRANCH_WIDTH)
    (o_d,) = _banded(zr, zr, zr, nrep=1, qcol=lambda r: ZR_Q_SWA, kcol=lambda r: ZR_KD, vcol=lambda r: ZR_VD,
                     kw=LANES, window=SWA_WINDOW - 1, want_lse=False, sink_row=sink_row)

    t = b * s
    flat = lambda a: a.reshape(t, a.shape[-1])
    return [dil_o[0], dil_o[1], dil_o[2], dil_lse[0], dil_lse[1], dil_lse[2], flat(o_b), flat(o_cmp), flat(o_slc),
            flat(o_win), flat(gates), flat(o_d)]


def kernel(x, p, w_in, conv_w, cmp_pos, cmp_w1, cmp_b1, cmp_w2, cmp_b2, sinks, w_branch, w_merge_gate, b_merge_gate, w_out, ln_mix_g, ln_mix_b, ffn_w_gate, ffn_w_up, ffn_w_down, w_router, b_router, moe_w_gate, moe_w_up, moe_w_down, ple_w, ple_gate_w, ple_gate_b, ln_ffn_g, ln_ffn_b):
    depth, b, s, _ = p.shape
    d = x.shape[-1]
    t = b * s
    alpha = (2 * depth) ** 0.25
    row = lambda v: v.reshape(1, -1).astype(F32)
    for i in range(depth):
        branch_inputs = _token_mixers(x, w_in[i], conv_w[i], cmp_pos[i], cmp_w1[i], cmp_b1[i], cmp_w2[i],
                                      cmp_b2[i], sinks[i])
        wg = jnp.concatenate([w_merge_gate[i, m] for m in range(N_BRANCH)], axis=1).astype(CDT)
        bg = b_merge_gate[i].reshape(1, N_BRANCH * d).astype(F32)
        wb = jnp.stack([w_branch[i, 0], w_branch[i, 1], w_branch[i, 2][GQA_COL_PERM],
                        w_branch[i, 3][GQA_COL_PERM]]).astype(CDT)
        x1 = _merge(x.reshape(t, d), branch_inputs, wg, bg, wb, w_out[i].astype(CDT),
                    row(ln_mix_g[i]), row(ln_mix_b[i]), alpha)
        p2d = p[i].reshape(t, -1)
        ple_args = (ple_w[i].astype(CDT), ple_gate_w[i].astype(CDT), row(ple_gate_b[i]),
                    row(ln_ffn_g[i]), row(ln_ffn_b[i]))
        j = i // 2
        if i % 2 == 0:
            x2 = _ffn(x1, p2d, ffn_w_gate[j], ffn_w_up[j], ffn_w_down[j], *ple_args, alpha)
        else:
            seg = min(MOE_SEGMENT, t)
            routing = _router(x1, w_router[j], b_router[j], seg)
            x2 = _moe(x1, routing, p2d, moe_w_gate[j].astype(CDT), moe_w_up[j].astype(CDT),
                      moe_w_down[j].astype(CDT), *ple_args, alpha, seg)
        x = x2.reshape(b, s, d)
    return x
```

```python
import functools

import numpy as np
import jax
import jax.numpy as jnp
from jax import lax
from jax.experimental import pallas as pl
from jax.experimental.pallas import tpu as pltpu

D_MODEL = 1024
HEAD_DIM = 64
DIL_PATTERNS = ((128, 1), (512, 4), (2048, 16))
N_DIL = 3
DIL_HEADS = 6
CONV_WIDTH = 384
CONV_K = 3
NSA_Q_HEADS = 6
NSA_KV_HEADS = 2
CMP_BLOCK = 32
CMP_STRIDE = 16
CMP_HIDDEN = 128
SEL_BLOCK = 64
N_SEL = 16
NSA_WINDOW = 512
SWA_Q_HEADS = 6
SWA_WINDOW = 128
BRANCH_WIDTH = 384
N_BRANCH = 4
N_EXPERTS = 8
LN_EPS = 1e-5
NEG_INF = -1e30
DIL_WIDTH = N_DIL * DIL_HEADS * HEAD_DIM
COLUMN_SIZES = (DIL_WIDTH, DIL_WIDTH, DIL_WIDTH, CONV_WIDTH, CONV_WIDTH, CONV_WIDTH,
                NSA_Q_HEADS * HEAD_DIM, 128, 128, 128, 128, 128, 128, 3 * NSA_Q_HEADS,
                SWA_Q_HEADS * HEAD_DIM, 128, 128)
COL_OFF = np.concatenate([[0], np.cumsum(COLUMN_SIZES)]).tolist()

LANES = 128
V7X_VMEM_BYTES = 64 * 1024 * 1024
MIB = 1024 * 1024

CDT = jnp.bfloat16
F32 = jnp.float32
QK_SCALE = HEAD_DIM ** -0.5
SUB_Q = 128
BIG = 1e30

_GQA_HEAD_ORDER = (0, 3, 1, 4, 2, 5)
GQA_COL_PERM = np.concatenate([np.arange(h * HEAD_DIM, (h + 1) * HEAD_DIM) for h in _GQA_HEAD_ORDER])
GQA_COL_HEAD = GQA_COL_PERM // HEAD_DIM


def _cparams(sem, vmem_mib):
    return pltpu.CompilerParams(dimension_semantics=sem, vmem_limit_bytes=int(vmem_mib * MIB))


def _nt_dot(a, b):
    return lax.dot_general(a, b, (((1,), (1,)), ((), ())), preferred_element_type=F32)


def _dot(a, b):
    return jnp.dot(a, b, preferred_element_type=F32)


def _layer_norm(r, g, b):
    mu = jnp.mean(r, axis=-1, keepdims=True)
    d = r - mu
    var = jnp.mean(d * d, axis=-1, keepdims=True)
    return d * lax.rsqrt(var + LN_EPS) * g + b


def _half_masks():
    lane = lax.broadcasted_iota(jnp.int32, (1, LANES), 1)
    return lane < HEAD_DIM


def _linear_kernel(x_ref, w_ref, *o_refs, splits, n_chunk):
    xb = x_ref[...].astype(CDT)
    for o_ref, (c0, c1) in zip(o_refs, splits):
        for a in range(c0, c1, n_chunk):
            b = min(a + n_chunk, c1)
            o_ref[:, a - c0:b - c0] = _dot(xb, w_ref[:, a:b]).astype(o_ref.dtype)


def _linear(x2d, w, splits, n_chunk, name, tm=512):
    t, k = x2d.shape
    tm = min(tm, t)
    n = w.shape[1]
    outs = [jax.ShapeDtypeStruct((t, c1 - c0), CDT) for c0, c1 in splits]
    return pl.pallas_call(
        functools.partial(_linear_kernel, splits=tuple(splits), n_chunk=n_chunk),
        grid=(t // tm,),
        in_specs=[pl.BlockSpec((tm, k), lambda i: (i, 0)),
                  pl.BlockSpec((k, n), lambda i: (0, 0))],
        out_specs=[pl.BlockSpec((tm, c1 - c0), lambda i: (i, 0)) for c0, c1 in splits],
        out_shape=outs,
        compiler_params=_cparams(("parallel",), 48),
        name=name,
    )(x2d, w)


def _conv_gate_kernel(x_ref, xh_ref, wc_ref, cw_ref, wg_ref, ob_ref, g_ref, *, tm):
    i = pl.program_id(1)
    w = CONV_WIDTH
    xb = x_ref[...].astype(CDT)
    z = _dot(xb, wc_ref[...])
    u = z[:, w:2 * w] * z[:, 2 * w:3 * w]
    zh = _dot(xh_ref[...].astype(CDT), wc_ref[:, w:3 * w])
    uh = zh[:, :w] * zh[:, w:]
    uh = jnp.where(i == 0, 0.0, uh)
    row = lax.broadcasted_iota(jnp.int32, (tm, w), 0)
    u1 = jnp.where(row == 0, uh[7:8, :], pltpu.roll(u, 1, 0))
    u2 = jnp.where(row == 0, uh[6:7, :], jnp.where(row == 1, uh[7:8, :], pltpu.roll(u, 2, 0)))
    y = cw_ref[0:1, :] * u2 + cw_ref[1:2, :] * u1 + cw_ref[2:3, :] * u
    ob_ref[...] = (z[:, :w] * y).astype(ob_ref.dtype)
    g_ref[...] = jax.nn.sigmoid(_dot(xb, wg_ref[...]))


def _conv_gate(x, wc, conv_w, wg, tm=512):
    b, s, d = x.shape
    tm = min(tm, s)
    hb = tm // 8
    return pl.pallas_call(
        functools.partial(_conv_gate_kernel, tm=tm),
        grid=(b, s // tm),
        in_specs=[pl.BlockSpec((None, tm, d), lambda bi, i: (bi, i, 0)),
                  pl.BlockSpec((None, 8, d), lambda bi, i: (bi, jnp.maximum(i * hb - 1, 0), 0)),
                  pl.BlockSpec(wc.shape, lambda bi, i: (0, 0)),
                  pl.BlockSpec(conv_w.shape, lambda bi, i: (0, 0)),
                  pl.BlockSpec(wg.shape, lambda bi, i: (0, 0))],
        out_specs=[pl.BlockSpec((None, tm, CONV_WIDTH), lambda bi, i: (bi, i, 0)),
                   pl.BlockSpec((None, tm, wg.shape[1]), lambda bi, i: (bi, i, 0))],
        out_shape=[jax.ShapeDtypeStruct((b, s, CONV_WIDTH), CDT),
                   jax.ShapeDtypeStruct((b, s, wg.shape[1]), F32)],
        compiler_params=_cparams(("parallel", "parallel"), 48),
        name="conv_gate",
    )(x, x, wc, conv_w, wg)


def _banded_kernel(*refs, window, pr, tq, kw, want_lse, has_sink):
    q_ref, kp_ref, kc_ref, vp_ref, vc_ref = refs[:5]
    n = 5
    sink_ref = None
    if has_sink:
        sink_ref = refs[n]
        n += 1
    o_ref = refs[n]
    n += 1
    lse_ref = None
    if want_lse:
        lse_ref = refs[n]
        n += 1
    kbuf, vbuf = refs[n], refs[n + 1]

    i = pl.program_id(2)
    kbuf[0:pr, :] = kp_ref[...]
    kbuf[pr:pr + tq, :] = kc_ref[...]
    vbuf[0:pr, :] = vp_ref[...]
    vbuf[pr:pr + tq, :] = vc_ref[...]

    span = SUB_Q + pr
    qi = lax.broadcasted_iota(jnp.int32, (SUB_Q, span), 0)
    kj = lax.broadcasted_iota(jnp.int32, (SUB_Q, span), 1)
    dist = pr + qi - kj
    band = (dist >= 0) & (dist <= window)
    lo = _half_masks()
    for sb in range(tq // SUB_Q):
        r0 = sb * SUB_Q
        mask = band & (i * tq + r0 - pr + kj >= 0)
        for p in range(3):
            c0 = p * LANES
            kc0 = c0 if kw == 3 * LANES else 0
            qp = q_ref[r0:r0 + SUB_Q, c0:c0 + LANES]
            kk = kbuf[r0:r0 + span, kc0:kc0 + LANES]
            vv = vbuf[r0:r0 + span, kc0:kc0 + LANES]
            o_half, lse_half = [], []
            for hm in (lo, jnp.logical_not(lo)):
                qm = (jnp.where(hm, qp, 0) * QK_SCALE).astype(CDT)
                s = jnp.where(mask, _nt_dot(qm, kk), NEG_INF)
                m = jnp.max(s, axis=-1, keepdims=True)
                e = jnp.exp(s - m)
                l = jnp.sum(e, axis=-1, keepdims=True)
                o_half.append(_dot(e.astype(CDT), vv) / l)
                lse_half.append(m + jnp.log(l))
            o_pair = jnp.where(lo, o_half[0], o_half[1])
            lse_pair = jnp.where(lo, lse_half[0], lse_half[1])
            if has_sink:
                o_pair = o_pair * jax.nn.sigmoid(lse_pair - sink_ref[:, c0:c0 + LANES])
            o_ref[r0:r0 + SUB_Q, c0:c0 + LANES] = o_pair.astype(o_ref.dtype)
            if want_lse:
                lse_ref[r0:r0 + SUB_Q, c0:c0 + LANES] = lse_pair


def _banded(qa, ka, va, *, nrep, qcol, kcol, vcol, kw, window, want_lse, sink_row=None, tq=256):
    b, l, _ = qa.shape
    pr = -(-window // SUB_Q) * SUB_Q
    tq = min(max(tq, pr), l)
    assert tq % pr == 0 and l % tq == 0, (tq, pr, l)
    ratio = tq // pr
    qw = 3 * LANES
    in_specs = [
        pl.BlockSpec((None, tq, qw), lambda bi, r, i: (bi, i, qcol(r))),
        pl.BlockSpec((None, pr, kw), lambda bi, r, i: (bi, jnp.maximum(i * ratio - 1, 0), kcol(r))),
        pl.BlockSpec((None, tq, kw), lambda bi, r, i: (bi, i, kcol(r))),
        pl.BlockSpec((None, pr, kw), lambda bi, r, i: (bi, jnp.maximum(i * ratio - 1, 0), vcol(r))),
        pl.BlockSpec((None, tq, kw), lambda bi, r, i: (bi, i, vcol(r))),
    ]
    args = [qa, ka, ka, va, va]
    if sink_row is not None:
        in_specs.append(pl.BlockSpec(sink_row.shape, lambda bi, r, i: (0, 0)))
        args.append(sink_row)
    out_specs = [pl.BlockSpec((None, tq, qw), lambda bi, r, i: (bi, i, r))]
    out_shape = [jax.ShapeDtypeStruct((b, l, nrep * qw), CDT)]
    if want_lse:
        out_specs.append(pl.BlockSpec((None, tq, qw), lambda bi, r, i: (bi, i, r)))
        out_shape.append(jax.ShapeDtypeStruct((b, l, nrep * qw), F32))
    res = pl.pallas_call(
        functools.partial(_banded_kernel, window=window, pr=pr, tq=tq, kw=kw, want_lse=want_lse,
                          has_sink=sink_row is not None),
        grid=(b, nrep, l // tq),
        in_specs=in_specs,
        out_specs=out_specs,
        out_shape=out_shape,
        scratch_shapes=[pltpu.VMEM((pr + tq, kw), ka.dtype), pltpu.VMEM((pr + tq, kw), va.dtype)],
        compiler_params=_cparams(("parallel", "parallel", "parallel"), 32),
        name=f"banded_w{window}_k{kw}_r{nrep}",
    )(*args)
    return res


def _gelu_tanh(x):
    return 0.5 * x * (1.0 + jnp.tanh(0.7978845608028654 * (x + 0.044715 * (x * x * x))))


def _compress_kernel(x_ref, pa_ref, pb_ref, w1a_ref, w1b_ref, b1_ref, w2_ref, b2_ref, o_ref):
    x = x_ref[...].astype(F32)
    n = x.shape[0]
    a = _dot((x + pa_ref[...]).astype(CDT), w1a_ref[...])
    bm = _dot((x + pb_ref[...]).astype(CDT), w1b_ref[...])
    h = a + pltpu.roll(bm, n - 1, 0) + b1_ref[...]
    o_ref[...] = (_dot(_gelu_tanh(h).astype(CDT), w2_ref[...]) + b2_ref[...]).astype(o_ref.dtype)


def _compress(t, pos, w1, b1, w2, b2):
    b, s, _ = t.shape
    nch = s // CMP_STRIDE
    xw = CMP_STRIDE * LANES
    x = t.reshape(b, nch, xw)
    eye = jnp.eye(NSA_KV_HEADS, dtype=F32)
    w1r = w1.reshape(CMP_BLOCK, HEAD_DIM, CMP_HIDDEN)

    def expand_w1(part):
        return jnp.einsum('tdj,kl->tkdlj', part, eye).reshape(xw, NSA_KV_HEADS * CMP_HIDDEN).astype(CDT)

    def expand_pos(part):
        return jnp.broadcast_to(part[:, None, :], (CMP_STRIDE, NSA_KV_HEADS, HEAD_DIM)).reshape(1, xw)

    w1a, w1b = expand_w1(w1r[:CMP_STRIDE]), expand_w1(w1r[CMP_STRIDE:])
    pa, pb = expand_pos(pos[:CMP_STRIDE]), expand_pos(pos[CMP_STRIDE:])
    b1e = jnp.tile(b1, NSA_KV_HEADS).reshape(1, -1)
    w2e = jnp.einsum('jd,kl->kjld', w2, eye).reshape(NSA_KV_HEADS * CMP_HIDDEN, LANES).astype(CDT)
    b2e = jnp.tile(b2, NSA_KV_HEADS).reshape(1, -1)
    consts = [pa, pb, w1a, w1b, b1e, w2e, b2e]
    return pl.pallas_call(
        _compress_kernel,
        grid=(b,),
        in_specs=[pl.BlockSpec((None, nch, xw), lambda bi: (bi, 0, 0))]
        + [pl.BlockSpec(c.shape, lambda bi: (0, 0)) for c in consts],
        out_specs=pl.BlockSpec((None, nch, LANES), lambda bi: (bi, 0, 0)),
        out_shape=jax.ShapeDtypeStruct((b, nch, LANES), CDT),
        compiler_params=_cparams(("parallel",), 48),
        name="nsa_compress",
    )(x, *consts)


def _cmp_select_kernel(q_ref, kc_ref, vc_ref, ov_ref, o_ref, sel_ref, *, tq, n_sel):
    i = pl.program_id(1)
    ncp = kc_ref.shape[0]
    ns = ov_ref.shape[1]
    lo = _half_masks()
    t_col = i * tq + lax.broadcasted_iota(jnp.int32, (tq, 1), 0)
    c_end = lax.broadcasted_iota(jnp.int32, (tq, ncp), 1) * CMP_STRIDE + (CMP_BLOCK - 1)
    vis_bias = jnp.where(c_end <= t_col, 0.0, NEG_INF)
    has_visible = t_col >= CMP_BLOCK - 1
    blk = lax.broadcasted_iota(jnp.int32, (tq, ns), 1)
    blk_t = lax.broadcasted_iota(jnp.int32, (ns, tq), 0)
    cur = t_col // SEL_BLOCK
    causal = blk <= cur
    forced = (blk == 0) | (blk == cur) | (blk == cur - 1)
    kc = kc_ref[...]
    vc = vc_ref[...]
    o_kv = []
    for kv, hm in enumerate((lo, jnp.logical_not(lo))):
        psum = jnp.zeros((tq, ncp), F32)
        o_p = []
        for p in range(3):
            qm = (jnp.where(hm, q_ref[:, p * LANES:(p + 1) * LANES], 0) * QK_SCALE).astype(CDT)
            s = _nt_dot(qm, kc) + vis_bias
            m = jnp.max(s, axis=-1, keepdims=True)
            e = jnp.exp(s - m)
            inv = jnp.where(has_visible, 1.0 / jnp.maximum(jnp.sum(e, axis=-1, keepdims=True), 1e-30), 0.0)
            pn = e * inv
            o_p.append(_dot(pn.astype(CDT), vc))
            psum = psum + pn
        o_kv.append(o_p)
        p_hi = psum.astype(CDT)
        p_lo = (psum - p_hi.astype(F32)).astype(CDT)
        imp = _dot(p_hi, ov_ref[...]) + _dot(p_lo, ov_ref[...])
        work_t = jnp.where(causal & jnp.logical_not(forced), imp, -BIG).T

        def pick(_, carry):
            work, sel = carry
            m = jnp.max(work, axis=0, keepdims=True)
            idx = jnp.min(jnp.where(work == m, blk_t, ns), axis=0, keepdims=True)
            hit = blk_t == idx
            sel = jnp.where(hit & (m > -BIG), 1.0, sel)
            return jnp.where(hit, -2.0 * BIG, work), sel

        _, sel_t = lax.fori_loop(0, n_sel - 3, pick, (work_t, jnp.zeros((ns, tq), F32)))
        selb = jnp.where(forced | (sel_t.T > 0.5), 0.0, NEG_INF)
        sel_ref[:, kv * ns:(kv + 1) * ns] = selb.astype(sel_ref.dtype)
    for p in range(3):
        o_ref[:, p * LANES:(p + 1) * LANES] = jnp.where(lo, o_kv[0][p], o_kv[1][p]).astype(o_ref.dtype)


def _cmp_select(zr, qcol, kc, vc, tq=128):
    b, s, _ = zr.shape
    ncp = kc.shape[1]
    ns = s // SEL_BLOCK
    n_sel = min(N_SEL, ns)
    assert n_sel >= 3, "selection needs room for the three forced blocks"
    tq = min(tq, s)
    c = np.arange(ncp)[:, None] * CMP_STRIDE
    j = np.arange(ns)[None, :] * SEL_BLOCK
    overlap = ((c < j + SEL_BLOCK) & (c + CMP_BLOCK - 1 >= j)).astype(np.float32)
    overlap[ncp - 1:, :] = 0.0
    ov = jnp.asarray(overlap, CDT)
    qw = 3 * LANES
    return pl.pallas_call(
        functools.partial(_cmp_select_kernel, tq=tq, n_sel=n_sel),
        grid=(b, s // tq),
        in_specs=[pl.BlockSpec((None, tq, qw), lambda bi, i: (bi, i, qcol)),
                  pl.BlockSpec((None, ncp, LANES), lambda bi, i: (bi, 0, 0)),
                  pl.BlockSpec((None, ncp, LANES), lambda bi, i: (bi, 0, 0)),
                  pl.BlockSpec(ov.shape, lambda bi, i: (0, 0))],
        out_specs=[pl.BlockSpec((None, tq, qw), lambda bi, i: (bi, i, 0)),
                   pl.BlockSpec((None, tq, 2 * ns), lambda bi, i: (bi, i, 0))],
        out_shape=[jax.ShapeDtypeStruct((b, s, qw), CDT),
                   jax.ShapeDtypeStruct((b, s, 2 * ns), CDT)],
        compiler_params=_cparams(("parallel", "parallel"), 48),
        name="nsa_cmp_select",
    )(zr, kc, vc, ov)


SLC_ROWS = 64


def _slc_kernel(q_ref, k0_ref, k1_ref, v_ref, selb_ref, o_ref, s_ref, e_ref, m_ref, l_ref, a_ref, acc_ref,
                *, tq, tk):
    i = pl.program_id(1)
    ns = selb_ref.shape[1] // 2
    bpt = tk // SEL_BLOCK
    rb = SLC_ROWS
    lo = _half_masks()
    n_tiles = ((i + 1) * tq + tk - 1) // tk
    p_row = lax.broadcasted_iota(jnp.int32, (ns, LANES), 0)
    p_col = lax.broadcasted_iota(jnp.int32, (ns, LANES), 1)
    groups = []
    for kv, (hm, k_ref) in enumerate(((lo, k0_ref), (jnp.logical_not(lo), k1_ref))):
        q3 = [(jnp.where(hm, q_ref[:, p * LANES:(p + 1) * LANES], 0) * QK_SCALE).astype(CDT) for p in range(3)]
        selb = selb_ref[:, kv * ns:(kv + 1) * ns]
        lane0 = HEAD_DIM if kv == 0 else 0
        groups.append((kv, hm, k_ref, q3, selb, lane0))
    m_ref[...] = jnp.full(m_ref.shape, NEG_INF, F32)
    l_ref[...] = jnp.zeros(l_ref.shape, F32)
    acc_ref[...] = jnp.zeros(acc_ref.shape, F32)

    def scores(j):
        k0 = pl.multiple_of(j * tk, tk)
        for kv, hm, k_ref, q3, selb, lane0 in groups:
            place = ((p_col >= lane0) & (p_col < lane0 + bpt) & (p_row == p_col - lane0 + j * bpt)).astype(CDT)
            sb = _dot(selb, place).astype(CDT)
            qp = jnp.concatenate([jnp.where(hm, q, sb) for q in q3], axis=0)
            s_ref[kv] = _nt_dot(qp, k_ref[pl.ds(k0, tk), :])

    def softmax_pv(j, diagonal):
        k0 = pl.multiple_of(j * tk, tk)
        for kv in range(2):
            for r0 in range(0, 3 * tq, rb):
                rows = slice(r0, r0 + rb)
                s = s_ref[kv, rows, :]
                if diagonal:
                    t_row = i * tq + (r0 % tq) + lax.broadcasted_iota(jnp.int32, (rb, 1), 0)
                    kpos = k0 + lax.broadcasted_iota(jnp.int32, (rb, tk), 1)
                    s = jnp.where(kpos <= t_row, s, NEG_INF)
                m_old = m_ref[kv, rows, :]
                m_new = jnp.maximum(m_old, jnp.max(s, axis=-1, keepdims=True))
                alpha = jnp.exp(m_old - m_new)
                e = jnp.exp(s - jnp.tile(m_new, (1, tk // LANES)))
                l_ref[kv, rows, :] = alpha * l_ref[kv, rows, :] + jnp.sum(e, axis=-1, keepdims=True)
                m_ref[kv, rows, :] = m_new
                a_ref[kv, rows, :] = alpha
                e_ref[kv, rows, :] = e.astype(CDT)
        for kv in range(2):
            acc_ref[kv] = a_ref[kv] * acc_ref[kv] + _dot(e_ref[kv], v_ref[pl.ds(k0, tk), :])

    def tile(j, diagonal):
        scores(j)
        softmax_pv(j, diagonal)

    lax.fori_loop(0, n_tiles - 1, lambda j, c: (tile(j, False), c)[1], 0)
    tile(n_tiles - 1, True)
    o0 = acc_ref[0] / l_ref[0]
    o1 = acc_ref[1] / l_ref[1]
    for p in range(3):
        o_ref[:, p * LANES:(p + 1) * LANES] = jnp.where(
            lo, o0[p * tq:(p + 1) * tq], o1[p * tq:(p + 1) * tq]).astype(o_ref.dtype)


def _slc(zr, qcol, ksc, vcol, selb, tq=256, tk=1024):
    b, s, _ = zr.shape
    tq = min(tq, s)
    tk = min(tk, s)
    assert tq % SLC_ROWS == 0 and tk // SEL_BLOCK <= HEAD_DIM
    qw = 3 * LANES
    ns2 = selb.shape[2]
    pat = jax.nn.one_hot((jnp.arange(s) // SEL_BLOCK) % (tk // SEL_BLOCK), HEAD_DIM, dtype=ksc.dtype)
    pat = jnp.broadcast_to(pat[None], (b, s, HEAD_DIM))
    k0 = jnp.concatenate([ksc[..., :HEAD_DIM], pat], axis=-1)
    k1 = jnp.concatenate([pat, ksc[..., HEAD_DIM:]], axis=-1)
    full = lambda bi, i: (bi, 0, 0)
    return pl.pallas_call(
        functools.partial(_slc_kernel, tq=tq, tk=tk),
        grid=(b, s // tq),
        in_specs=[pl.BlockSpec((None, tq, qw), lambda bi, i: (bi, i, qcol)),
                  pl.BlockSpec((None, s, LANES), full),
                  pl.BlockSpec((None, s, LANES), full),
                  pl.BlockSpec((None, s, LANES), lambda bi, i: (bi, 0, vcol)),
                  pl.BlockSpec((None, tq, ns2), lambda bi, i: (bi, i, 0))],
        out_specs=pl.BlockSpec((None, tq, qw), lambda bi, i: (bi, i, 0)),
        out_shape=jax.ShapeDtypeStruct((b, s, qw), CDT),
        scratch_shapes=[pltpu.VMEM((2, 3 * tq, tk), F32), pltpu.VMEM((2, 3 * tq, tk), CDT),
                        pltpu.VMEM((2, 3 * tq, LANES), F32), pltpu.VMEM((2, 3 * tq, LANES), F32),
                        pltpu.VMEM((2, 3 * tq, LANES), F32), pltpu.VMEM((2, 3 * tq, LANES), F32)],
        compiler_params=_cparams(("parallel", "arbitrary"), 56),
        name="nsa_slc",
    )(zr, k0, k1, zr, selb)


def _merge_kernel(x_ref, oa0, oa1, oa2, la0, la1, la2, ob, ocmp, oslc, owin, gates, od,
                  wg_ref, bg_ref, wb_ref, wo_ref, g_ref, b_ref, o_ref, *, alpha):
    x = x_ref[...]
    xb = x.astype(CDT)
    bw = BRANCH_WIDTH
    l0, l1, l2 = la0[...], la1[...], la2[...]
    mx = jnp.maximum(jnp.maximum(l0, l1), l2)
    w0, w1, w2 = jnp.exp(l0 - mx), jnp.exp(l1 - mx), jnp.exp(l2 - mx)
    o_a = (w0 * oa0[...].astype(F32) + w1 * oa1[...].astype(F32) + w2 * oa2[...].astype(F32)) / (w0 + w1 + w2)
    o_c = (gates[:, 0:bw] * ocmp[...].astype(F32) + gates[:, bw:2 * bw] * oslc[...].astype(F32)
           + gates[:, 2 * bw:3 * bw] * owin[...].astype(F32))
    branches = (o_a.astype(CDT), ob[...], o_c.astype(CDT), od[...])
    d = x.shape[1]
    merged = jnp.zeros(x.shape, F32)
    for m in range(N_BRANCH):
        gate = jax.nn.sigmoid(_dot(xb, wg_ref[:, m * d:(m + 1) * d]) + bg_ref[:, m * d:(m + 1) * d])
        merged = merged + gate * _dot(branches[m], wb_ref[m])
    r = alpha * x + _dot(merged.astype(CDT), wo_ref[...])
    o_ref[...] = _layer_norm(r, g_ref[...], b_ref[...])


def _merge(x2d, branch_inputs, wg, bg, wb, wo, g, b, alpha, tm=256):
    t, d = x2d.shape
    tm = min(tm, t)
    row = lambda i: (i, 0)
    const2 = lambda i: (0, 0)
    in_specs = [pl.BlockSpec((tm, d), row)]
    in_specs += [pl.BlockSpec((tm, a.shape[1]), row) for a in branch_inputs]
    in_specs += [pl.BlockSpec(wg.shape, const2), pl.BlockSpec(bg.shape, const2),
                 pl.BlockSpec(wb.shape, lambda i: (0, 0, 0)), pl.BlockSpec(wo.shape, const2),
                 pl.BlockSpec(g.shape, const2), pl.BlockSpec(b.shape, const2)]
    return pl.pallas_call(
        functools.partial(_merge_kernel, alpha=alpha),
        grid=(t // tm,),
        in_specs=in_specs,
        out_specs=pl.BlockSpec((tm, d), row),
        out_shape=jax.ShapeDtypeStruct((t, d), F32),
        compiler_params=_cparams(("parallel",), 56),
        name="merge_ln",
    )(x2d, *branch_inputs, wg, bg, wb, wo, g, b)


def _ple_ln(x, xb, f, p, plw_ref, pgw_ref, pgb_ref, g_ref, b_ref, alpha):
    ple = jax.nn.sigmoid(_dot(xb, pgw_ref[...]) + pgb_ref[...]) * _dot(p.astype(CDT), plw_ref[...])
    return _layer_norm(alpha * x + f + ple, g_ref[...], b_ref[...])


FFN_CHUNK = 512


def _ffn_kernel(x_ref, p_ref, wg_ref, wu_ref, wd_ref, plw_ref, pgw_ref, pgb_ref, g_ref, b_ref, o_ref, h_ref, *, alpha):
    x = x_ref[...]
    xb = x.astype(CDT)
    dff = wg_ref.shape[1]
    for c0 in range(0, dff, FFN_CHUNK):
        cols = slice(c0, min(c0 + FFN_CHUNK, dff))
        h_ref[:, cols] = (jax.nn.silu(_dot(xb, wg_ref[:, cols])) * _dot(xb, wu_ref[:, cols])).astype(CDT)
    f = _dot(h_ref[...], wd_ref[...])
    o_ref[...] = _ple_ln(x, xb, f, p_ref[...], plw_ref, pgw_ref, pgb_ref, g_ref, b_ref, alpha)


def _ffn(x2d, p2d, wg, wu, wd, plw, pgw, pgb, g, b, alpha, tm=512):
    t, d = x2d.shape
    tm = min(tm, t)
    dff = wg.shape[1]
    wg, wu, wd = wg.astype(CDT), wu.astype(CDT), wd.astype(CDT)
    row = lambda i: (i, 0)
    const = lambda shape: pl.BlockSpec(shape, lambda i: (0, 0), pipeline_mode=pl.Buffered(1))
    return pl.pallas_call(
        functools.partial(_ffn_kernel, alpha=alpha),
        grid=(t // tm,),
        in_specs=[pl.BlockSpec((tm, d), row), pl.BlockSpec((tm, p2d.shape[1]), row),
                  const(wg.shape), const(wu.shape), const(wd.shape),
                  const(plw.shape), const(pgw.shape), const(pgb.shape), const(g.shape), const(b.shape)],
        out_specs=pl.BlockSpec((tm, d), row),
        out_shape=jax.ShapeDtypeStruct((t, d), F32),
        scratch_shapes=[pltpu.VMEM((tm, dff), CDT)],
        compiler_params=_cparams(("parallel",), 48),
        name="ffn_ple_ln",
    )(x2d, p2d, wg, wu, wd, plw, pgw, pgb, g, b)


def _router_kernel(x_ref, wh_ref, wl_ref, b_ref, comb_ref, rank_ref, rank_t_ref, cnt_ref):
    x = x_ref[...]
    xh = x.astype(CDT)
    xl = (x - xh.astype(F32)).astype(CDT)
    logits = _dot(xh, wh_ref[...]) + _dot(xh, wl_ref[...]) + _dot(xl, wh_ref[...]) + b_ref[...]
    lane = lax.broadcasted_iota(jnp.int32, logits.shape, 1)
    v1 = jnp.max(logits, axis=-1, keepdims=True)
    i1 = jnp.min(jnp.where(logits == v1, lane, LANES), axis=-1, keepdims=True)
    rest = jnp.where(lane == i1, -jnp.inf, logits)
    v2 = jnp.max(rest, axis=-1, keepdims=True)
    i2 = jnp.min(jnp.where(rest == v2, lane, LANES), axis=-1, keepdims=True)
    e2 = jnp.exp(v2 - v1)
    comb_ref[...] = jnp.where(lane == i1, 1.0 / (1.0 + e2), 0.0) + jnp.where(lane == i2, e2 / (1.0 + e2), 0.0)
    routed = (lane == i1) | (lane == i2)
    mask = routed.astype(CDT)
    tm = x.shape[0]
    before = (lax.broadcasted_iota(jnp.int32, (tm, tm), 1) < lax.broadcasted_iota(jnp.int32, (tm, tm), 0)).astype(CDT)
    rank = jnp.where(routed, _dot(before, mask), -1.0)
    rank_ref[...] = rank
    rank_t_ref[...] = rank.T[0:rank_t_ref.shape[0], :]
    cnt_ref[...] = jnp.sum(routed.astype(F32), axis=0, keepdims=True).astype(jnp.int32)


def _router(x2d, w_router, b_router, tm):
    t, d = x2d.shape
    ne = w_router.shape[1]
    wp = jnp.zeros((d, LANES), F32).at[:, :ne].set(w_router)
    wh = wp.astype(CDT)
    wl = (wp - wh.astype(F32)).astype(CDT)
    bp = jnp.full((1, LANES), -BIG, F32).at[0, :ne].set(b_router)
    nt = t // tm
    row = lambda i: (i, 0)
    return pl.pallas_call(
        _router_kernel,
        grid=(nt,),
        in_specs=[pl.BlockSpec((tm, d), row), pl.BlockSpec(wh.shape, lambda i: (0, 0)),
                  pl.BlockSpec(wl.shape, lambda i: (0, 0)), pl.BlockSpec(bp.shape, lambda i: (0, 0))],
        out_specs=[pl.BlockSpec((tm, LANES), row), pl.BlockSpec((tm, LANES), row),
                   pl.BlockSpec((None, 8, tm), lambda i: (i, 0, 0)),
                   pl.BlockSpec((None, 1, LANES), lambda i: (i, 0, 0))],
        out_shape=[jax.ShapeDtypeStruct((t, LANES), F32), jax.ShapeDtypeStruct((t, LANES), F32),
                   jax.ShapeDtypeStruct((nt, 8, tm), F32), jax.ShapeDtypeStruct((nt, 1, LANES), jnp.int32)],
        compiler_params=_cparams(("parallel",), 40),
        name="moe_router",
    )(x2d, wh, wl, bp)


def _moe_kernel(cnt_ref, x_ref, comb_ref, rank_ref, rank_t_ref, p_ref, wg_ref, wu_ref, wd_ref, plw_ref, pgw_ref,
                pgb_ref, g_ref, b_ref, o_ref, xe_ref, ye_ref, *, alpha, rs, seg):
    i = pl.program_id(0)
    e = pl.program_id(1)
    c = pl.program_id(2)
    n_seg = x_ref.shape[0] // seg
    last_chunk = c == pl.num_programs(2) - 1

    def rows(sc):
        return pl.ds(pl.multiple_of(sc * rs, 8), rs)

    @pl.when((e == 0) & (c == 0))
    def _():
        o_ref[...] = jnp.zeros_like(o_ref)

    for sg in range(n_seg):
        tok = slice(sg * seg, (sg + 1) * seg)
        n_groups = (cnt_ref[(i * n_seg + sg) * LANES + e] + rs - 1) // rs

        @pl.when(c == 0)
        def _():
            xb = x_ref[tok, :].astype(CDT)
            rank_row = rank_t_ref[sg, pl.ds(e, 1), :]
            row_id = lax.broadcasted_iota(jnp.int32, (rs, seg), 0).astype(F32)

            def gather(sc, _):
                onehot = (rank_row - (sc * rs).astype(F32) == row_id).astype(CDT)
                xe_ref[sg, rows(sc), :] = _dot(onehot, xb).astype(CDT)
                ye_ref[sg, rows(sc), :] = jnp.zeros((rs, ye_ref.shape[2]), F32)
                return 0

            lax.fori_loop(0, n_groups, gather, 0)

        def expert(sc, _):
            xs = xe_ref[sg, rows(sc), :]
            h = jax.nn.silu(_dot(xs, wg_ref[0])) * _dot(xs, wu_ref[0])
            ye_ref[sg, rows(sc), :] += _dot(h.astype(CDT), wd_ref[0])
            return 0

        lax.fori_loop(0, n_groups, expert, 0)

        @pl.when(last_chunk)
        def _():
            lane = lax.broadcasted_iota(jnp.int32, (seg, LANES), 1)
            mine = lane == e
            cw = jnp.sum(jnp.where(mine, comb_ref[tok, :], 0.0), axis=-1, keepdims=True)
            rank_col = jnp.sum(jnp.where(mine, rank_ref[tok, :], 0.0), axis=-1, keepdims=True)
            col_id = lax.broadcasted_iota(jnp.int32, (seg, rs), 1).astype(F32)

            def scatter(sc, _):
                onehot = (rank_col - (sc * rs).astype(F32) == col_id).astype(CDT)
                o_ref[tok, :] += cw * _dot(onehot, ye_ref[sg, rows(sc), :].astype(CDT))
                return 0

            lax.fori_loop(0, n_groups, scatter, 0)

    @pl.when((e == pl.num_programs(1) - 1) & last_chunk)
    def _():
        for sg in range(n_seg):
            tok = slice(sg * seg, (sg + 1) * seg)
            x = x_ref[tok, :]
            o_ref[tok, :] = _ple_ln(x, x.astype(CDT), o_ref[tok, :], p_ref[tok, :], plw_ref, pgw_ref, pgb_ref,
                                    g_ref, b_ref, alpha)


MOE_CHUNK = 512
MOE_SEGMENT = 1024
MOE_SEGMENTS_PER_TILE = 2
MOE_ROW_GROUP = 288


def _single_buffered(shape, index_map):
    return pl.BlockSpec(shape, index_map, pipeline_mode=pl.Buffered(1))


def _moe(x2d, routing, p2d, wg, wu, wd, plw, pgw, pgb, g, b, alpha, seg):
    comb, rank, rank_t, cnt = routing
    t, d = x2d.shape
    ne, _, dff = wg.shape
    ck = min(MOE_CHUNK, dff)
    rs = min(MOE_ROW_GROUP, seg)
    n_seg = min(MOE_SEGMENTS_PER_TILE, t // seg)
    tm = n_seg * seg
    max_rows = -(-seg // rs) * rs
    nck = dff // ck
    wg = wg.reshape(ne, d, nck, ck).transpose(0, 2, 1, 3).astype(CDT)
    wu = wu.reshape(ne, d, nck, ck).transpose(0, 2, 1, 3).astype(CDT)
    wd = wd.reshape(ne, nck, ck, d).astype(CDT)
    row = lambda i, e, c, cnt: (i, 0)
    c2 = lambda i, e, c, cnt: (0, 0)
    grid_spec = pltpu.PrefetchScalarGridSpec(
        num_scalar_prefetch=1,
        grid=(t // tm, ne, dff // ck),
        in_specs=[_single_buffered((tm, d), row), _single_buffered((tm, LANES), row),
                  _single_buffered((tm, LANES), row),
                  _single_buffered((n_seg, 8, seg), lambda i, e, c, cnt: (i, 0, 0)),
                  _single_buffered((tm, p2d.shape[1]), row),
                  pl.BlockSpec((None, 1, d, ck), lambda i, e, c, cnt: (e, c, 0, 0)),
                  pl.BlockSpec((None, 1, d, ck), lambda i, e, c, cnt: (e, c, 0, 0)),
                  pl.BlockSpec((None, 1, ck, d), lambda i, e, c, cnt: (e, c, 0, 0)),
                  _single_buffered(plw.shape, c2), _single_buffered(pgw.shape, c2), _single_buffered(pgb.shape, c2),
                  _single_buffered(g.shape, c2), _single_buffered(b.shape, c2)],
        out_specs=pl.BlockSpec((tm, d), row),
        scratch_shapes=[pltpu.VMEM((n_seg, max_rows, d), CDT), pltpu.VMEM((n_seg, max_rows, d), F32)],
    )
    return pl.pallas_call(
        functools.partial(_moe_kernel, alpha=alpha, rs=rs, seg=seg),
        grid_spec=grid_spec,
        out_shape=jax.ShapeDtypeStruct((t, d), F32),
        compiler_params=_cparams(("parallel", "arbitrary", "arbitrary"), 60),
        name="moe_ple_ln",
    )(cnt.reshape(-1), x2d, comb, rank, rank_t, p2d, wg, wu, wd, plw, pgw, pgb, g, b)


def _prep_in_weights(w_in):
    o = COL_OFF
    bw = BRANCH_WIDTH
    cols = lambda n: w_in[:, o[n]:o[n + 1]]
    qa, ka, va = cols(0), cols(1), cols(2)
    w_dil = jnp.concatenate(
        [t[:, g * bw:(g + 1) * bw] for g in range(N_DIL) for t in (qa, ka, va)], axis=1).astype(CDT)
    w_conv = jnp.concatenate([cols(3), cols(4), cols(5)], axis=1).astype(CDT)
    gn = cols(13)
    w_gate = jnp.concatenate([gn[:, br * NSA_Q_HEADS + GQA_COL_HEAD] for br in range(3)], axis=1).astype(CDT)
    w_rest = jnp.concatenate([cols(6)[:, GQA_COL_PERM], cols(14)[:, GQA_COL_PERM], cols(10), cols(11),
                              cols(12), cols(15), cols(16), cols(9), cols(7), cols(8)], axis=1).astype(CDT)
    return w_dil, w_conv, w_gate, w_rest


ZR_Q_NSA, ZR_Q_SWA = 0, 1
ZR_VSC, ZR_KWC, ZR_VWC, ZR_KD, ZR_VD = 6, 7, 8, 9, 10
ZR_WIDTH = 2 * BRANCH_WIDTH + 5 * LANES


def _token_mixers(x, w_in, conv_w, cmp_pos, cmp_w1, cmp_b1, cmp_w2, cmp_b2, sinks):
    b, s, d = x.shape
    x2d = x.reshape(b * s, d)
    w_dil, w_conv, w_gate, w_rest = _prep_in_weights(w_in)
    gw = 3 * BRANCH_WIDTH

    z_dil = _linear(x2d, w_dil, [(g * gw, (g + 1) * gw) for g in range(N_DIL)], gw, "in_proj_dil")
    zr, ksc, kcc, vcc = _linear(x2d, w_rest, [(0, ZR_WIDTH)] + [(ZR_WIDTH + n * LANES, ZR_WIDTH + (n + 1) * LANES)
                                                           for n in range(3)], 256, "in_proj_rest")
    zr = zr.reshape(b, s, ZR_WIDTH)
    o_b, gates = _conv_gate(x, w_conv, conv_w, w_gate)

    dil_o, dil_lse = [], []
    for g, (window, dil) in enumerate(DIL_PATTERNS):
        view = z_dil[g].reshape(b, s // dil, dil * gw)
        og, lg = _banded(view, view, view, nrep=dil,
                         qcol=lambda r: 3 * r, kcol=lambda r: 3 * r + 1, vcol=lambda r: 3 * r + 2,
                         kw=3 * LANES, window=window // dil, want_lse=True)
        dil_o.append(og.reshape(b * s, BRANCH_WIDTH))
        dil_lse.append(lg.reshape(b * s, BRANCH_WIDTH))

    kc = _compress(kcc.reshape(b, s, LANES), cmp_pos[0], cmp_w1[0], cmp_b1[0], cmp_w2[0], cmp_b2[0])
    vc = _compress(vcc.reshape(b, s, LANES), cmp_pos[1], cmp_w1[1], cmp_b1[1], cmp_w2[1], cmp_b2[1])
    o_cmp, selb = _cmp_select(zr, ZR_Q_NSA, kc, vc)
    o_slc = _slc(zr, ZR_Q_NSA, ksc.reshape(b, s, LANES), ZR_VSC, selb)
    (o_win,) = _banded(zr, zr, zr, nrep=1, qcol=lambda r: ZR_Q_NSA, kcol=lambda r: ZR_KWC, vcol=lambda r: ZR_VWC,
                       kw=LANES, window=NSA_WINDOW - 1, want_lse=False, tq=512)

    sink_row = sinks.astype(F32)[GQA_COL_HEAD].reshape(1, BRANCH_WIDTH)
    (o_d,) = _banded(zr, zr, zr, nrep=1, qcol=lambda r: ZR_Q_SWA, kcol=lambda r: ZR_KD, vcol=lambda r: ZR_VD,
                     kw=LANES, window=SWA_WINDOW - 1, want_lse=False, sink_row=sink_row)

    t = b * s
    flat = lambda a: a.reshape(t, a.shape[-1])
    return [dil_o[0], dil_o[1], dil_o[2], dil_lse[0], dil_lse[1], dil_lse[2], flat(o_b), flat(o_cmp), flat(o_slc),
            flat(o_win), flat(gates), flat(o_d)]


def kernel(x, p, w_in, conv_w, cmp_pos, cmp_w1, cmp_b1, cmp_w2, cmp_b2, sinks, w_branch, w_merge_gate, b_merge_gate, w_out, ln_mix_g, ln_mix_b, ffn_w_gate, ffn_w_up, ffn_w_down, w_router, b_router, moe_w_gate, moe_w_up, moe_w_down, ple_w, ple_gate_w, ple_gate_b, ln_ffn_g, ln_ffn_b):
    depth, b, s, _ = p.shape
    d = x.shape[-1]
    t = b * s
    alpha = (2 * depth) ** 0.25
    row = lambda v: v.reshape(1, -1).astype(F32)
    for i in range(depth):
        branch_inputs = _token_mixers(x, w_in[i], conv_w[i], cmp_pos[i], cmp_w1[i], cmp_b1[i], cmp_w2[i],
                                      cmp_b2[i], sinks[i])
        wg = jnp.concatenate([w_merge_gate[i, m] for m in range(N_BRANCH)], axis=1).astype(CDT)
        bg = b_merge_gate[i].reshape(1, N_BRANCH * d).astype(F32)
        wb = jnp.stack([w_branch[i, 0], w_branch[i, 1], w_branch[i, 2][GQA_COL_PERM],
                        w_branch[i, 3][GQA_COL_PERM]]).astype(CDT)
        x1 = _merge(x.reshape(t, d), branch_inputs, wg, bg, wb, w_out[i].astype(CDT),
                    row(ln_mix_g[i]), row(ln_mix_b[i]), alpha)
        p2d = p[i].reshape(t, -1)
        ple_args = (ple_w[i].astype(CDT), ple_gate_w[i].astype(CDT), row(ple_gate_b[i]),
                    row(ln_ffn_g[i]), row(ln_ffn_b[i]))
        j = i // 2
        if i % 2 == 0:
            x2 = _ffn(x1, p2d, ffn_w_gate[j], ffn_w_up[j], ffn_w_down[j], *ple_args, alpha)
        else:
            seg = min(MOE_SEGMENT, t)
            routing = _router(x1, w_router[j], b_router[j], seg)
            x2 = _moe(x1, routing, p2d, moe_w_gate[j], moe_w_up[j], moe_w_down[j], *ple_args, alpha, seg)
        x = x2.reshape(b, s, d)
    return x
```

```python
import functools

import numpy as np
import jax
import jax.numpy as jnp
from jax import lax
from jax.experimental import pallas as pl
from jax.experimental.pallas import tpu as pltpu

D_MODEL = 1024
HEAD_DIM = 64
DIL_PATTERNS = ((128, 1), (512, 4), (2048, 16))
N_DIL = 3
DIL_HEADS = 6
CONV_WIDTH = 384
CONV_K = 3
NSA_Q_HEADS = 6
NSA_KV_HEADS = 2
CMP_BLOCK = 32
CMP_STRIDE = 16
CMP_HIDDEN = 128
SEL_BLOCK = 64
N_SEL = 16
NSA_WINDOW = 512
SWA_Q_HEADS = 6
SWA_WINDOW = 128
BRANCH_WIDTH = 384
N_BRANCH = 4
N_EXPERTS = 8
LN_EPS = 1e-5
NEG_INF = -1e30
DIL_WIDTH = N_DIL * DIL_HEADS * HEAD_DIM
COLUMN_SIZES = (DIL_WIDTH, DIL_WIDTH, DIL_WIDTH, CONV_WIDTH, CONV_WIDTH, CONV_WIDTH,
                NSA_Q_HEADS * HEAD_DIM, 128, 128, 128, 128, 128, 128, 3 * NSA_Q_HEADS,
                SWA_Q_HEADS * HEAD_DIM, 128, 128)
COL_OFF = np.concatenate([[0], np.cumsum(COLUMN_SIZES)]).tolist()

LANES = 128
V7X_VMEM_BYTES = 64 * 1024 * 1024
MIB = 1024 * 1024

CDT = jnp.bfloat16
F32 = jnp.float32
QK_SCALE = HEAD_DIM ** -0.5
SUB_Q = 128
BIG = 1e30

_GQA_HEAD_ORDER = (0, 3, 1, 4, 2, 5)
GQA_COL_PERM = np.concatenate([np.arange(h * HEAD_DIM, (h + 1) * HEAD_DIM) for h in _GQA_HEAD_ORDER])
GQA_COL_HEAD = GQA_COL_PERM // HEAD_DIM


def _cparams(sem, vmem_mib):
    return pltpu.CompilerParams(dimension_semantics=sem, vmem_limit_bytes=int(vmem_mib * MIB))


def _nt_dot(a, b):
    return lax.dot_general(a, b, (((1,), (1,)), ((), ())), preferred_element_type=F32)


def _dot(a, b):
    return jnp.dot(a, b, preferred_element_type=F32)


def _layer_norm(r, g, b):
    mu = jnp.mean(r, axis=-1, keepdims=True)
    d = r - mu
    var = jnp.mean(d * d, axis=-1, keepdims=True)
    return d * lax.rsqrt(var + LN_EPS) * g + b


def _half_masks():
    lane = lax.broadcasted_iota(jnp.int32, (1, LANES), 1)
    return lane < HEAD_DIM


def _linear_kernel(x_ref, w_ref, *refs, splits, n_chunk, dils):
    o_refs, z_ref = refs[:len(splits)], refs[len(splits)]
    xb = x_ref[...].astype(CDT)
    tm = xb.shape[0]
    for o_ref, (c0, c1), dil in zip(o_refs, splits, dils):
        width = c1 - c0
        if dil == 1:
            for a in range(c0, c1, n_chunk):
                b = min(a + n_chunk, c1)
                o_ref[:, a - c0:b - c0] = _dot(xb, w_ref[:, a:b]).astype(o_ref.dtype)
        else:
            z = _dot(xb, w_ref[:, c0:c1])
            for cb in range(width // LANES):
                z_ref[cb] = z[:, cb * LANES:(cb + 1) * LANES]
            for r in range(dil):
                for cb in range(width // LANES):
                    o_ref[:, r * width + cb * LANES:r * width + (cb + 1) * LANES] = (
                        z_ref[cb, pl.ds(r, tm // dil, stride=dil), :].astype(o_ref.dtype))


def _linear(x2d, w, splits, n_chunk, name, dils=None, tm=512):
    t, k = x2d.shape
    tm = min(tm, t)
    n = w.shape[1]
    dils = tuple(dils) if dils is not None else (1,) * len(splits)
    widths = [c1 - c0 for c0, c1 in splits]
    assert all(tm % (16 * dl) == 0 for dl in dils)
    return pl.pallas_call(
        functools.partial(_linear_kernel, splits=tuple(splits), n_chunk=n_chunk, dils=dils),
        grid=(t // tm,),
        in_specs=[pl.BlockSpec((tm, k), lambda i: (i, 0)),
                  pl.BlockSpec((k, n), lambda i: (0, 0))],
        out_specs=[pl.BlockSpec((tm // dl, dl * wd), lambda i: (i, 0)) for wd, dl in zip(widths, dils)],
        out_shape=[jax.ShapeDtypeStruct((t // dl, dl * wd), CDT) for wd, dl in zip(widths, dils)],
        scratch_shapes=[pltpu.VMEM((max(widths) // LANES, tm, LANES), F32)],
        compiler_params=_cparams(("parallel",), 48),
        name=name,
    )(x2d, w)


def _conv_gate_kernel(x_ref, xh_ref, wc_ref, cw_ref, wg_ref, ob_ref, g_ref, *, tm):
    i = pl.program_id(1)
    w = CONV_WIDTH
    xb = x_ref[...].astype(CDT)
    z = _dot(xb, wc_ref[...])
    u = z[:, w:2 * w] * z[:, 2 * w:3 * w]
    zh = _dot(xh_ref[...].astype(CDT), wc_ref[:, w:3 * w])
    uh = zh[:, :w] * zh[:, w:]
    uh = jnp.where(i == 0, 0.0, uh)
    row = lax.broadcasted_iota(jnp.int32, (tm, w), 0)
    u1 = jnp.where(row == 0, uh[7:8, :], pltpu.roll(u, 1, 0))
    u2 = jnp.where(row == 0, uh[6:7, :], jnp.where(row == 1, uh[7:8, :], pltpu.roll(u, 2, 0)))
    y = cw_ref[0:1, :] * u2 + cw_ref[1:2, :] * u1 + cw_ref[2:3, :] * u
    ob_ref[...] = (z[:, :w] * y).astype(ob_ref.dtype)
    g_ref[...] = jax.nn.sigmoid(_dot(xb, wg_ref[...]))


def _conv_gate(x, wc, conv_w, wg, tm=512):
    b, s, d = x.shape
    tm = min(tm, s)
    hb = tm // 8
    return pl.pallas_call(
        functools.partial(_conv_gate_kernel, tm=tm),
        grid=(b, s // tm),
        in_specs=[pl.BlockSpec((None, tm, d), lambda bi, i: (bi, i, 0)),
                  pl.BlockSpec((None, 8, d), lambda bi, i: (bi, jnp.maximum(i * hb - 1, 0), 0)),
                  pl.BlockSpec(wc.shape, lambda bi, i: (0, 0)),
                  pl.BlockSpec(conv_w.shape, lambda bi, i: (0, 0)),
                  pl.BlockSpec(wg.shape, lambda bi, i: (0, 0))],
        out_specs=[pl.BlockSpec((None, tm, CONV_WIDTH), lambda bi, i: (bi, i, 0)),
                   pl.BlockSpec((None, tm, wg.shape[1]), lambda bi, i: (bi, i, 0))],
        out_shape=[jax.ShapeDtypeStruct((b, s, CONV_WIDTH), CDT),
                   jax.ShapeDtypeStruct((b, s, wg.shape[1]), F32)],
        compiler_params=_cparams(("parallel", "parallel"), 48),
        name="conv_gate",
    )(x, x, wc, conv_w, wg)


def _banded_kernel(*refs, window, pr, tq, kw, want_lse, has_sink):
    q_ref, kp_ref, kc_ref, vp_ref, vc_ref = refs[:5]
    n = 5
    sink_ref = None
    if has_sink:
        sink_ref = refs[n]
        n += 1
    o_ref = refs[n]
    n += 1
    lse_ref = None
    if want_lse:
        lse_ref = refs[n]
        n += 1
    kbuf, vbuf = refs[n], refs[n + 1]

    i = pl.program_id(2)
    kbuf[0:pr, :] = kp_ref[...]
    kbuf[pr:pr + tq, :] = kc_ref[...]
    vbuf[0:pr, :] = vp_ref[...]
    vbuf[pr:pr + tq, :] = vc_ref[...]

    span = SUB_Q + pr
    qi = lax.broadcasted_iota(jnp.int32, (SUB_Q, span), 0)
    kj = lax.broadcasted_iota(jnp.int32, (SUB_Q, span), 1)
    dist = pr + qi - kj
    band = (dist >= 0) & (dist <= window)
    lo = _half_masks()
    for sb in range(tq // SUB_Q):
        r0 = sb * SUB_Q
        mask = band & (i * tq + r0 - pr + kj >= 0)
        for p in range(3):
            c0 = p * LANES
            kc0 = c0 if kw == 3 * LANES else 0
            qp = q_ref[r0:r0 + SUB_Q, c0:c0 + LANES]
            kk = kbuf[r0:r0 + span, kc0:kc0 + LANES]
            vv = vbuf[r0:r0 + span, kc0:kc0 + LANES]
            o_half, lse_half = [], []
            for hm in (lo, jnp.logical_not(lo)):
                qm = (jnp.where(hm, qp, 0) * QK_SCALE).astype(CDT)
                s = jnp.where(mask, _nt_dot(qm, kk), NEG_INF)
                m = jnp.max(s, axis=-1, keepdims=True)
                e = jnp.exp(s - m)
                l = jnp.sum(e, axis=-1, keepdims=True)
                o_half.append(_dot(e.astype(CDT), vv) / l)
                lse_half.append(m + jnp.log(l))
            o_pair = jnp.where(lo, o_half[0], o_half[1])
            lse_pair = jnp.where(lo, lse_half[0], lse_half[1])
            if has_sink:
                o_pair = o_pair * jax.nn.sigmoid(lse_pair - sink_ref[:, c0:c0 + LANES])
            o_ref[r0:r0 + SUB_Q, c0:c0 + LANES] = o_pair.astype(o_ref.dtype)
            if want_lse:
                lse_ref[r0:r0 + SUB_Q, c0:c0 + LANES] = lse_pair


def _banded(qa, ka, va, *, nrep, qcol, kcol, vcol, kw, window, want_lse, sink_row=None, tq=256):
    b, l, _ = qa.shape
    pr = -(-window // SUB_Q) * SUB_Q
    tq = min(max(tq, pr), l)
    assert tq % pr == 0 and l % tq == 0, (tq, pr, l)
    ratio = tq // pr
    qw = 3 * LANES
    in_specs = [
        pl.BlockSpec((None, tq, qw), lambda bi, r, i: (bi, i, qcol(r))),
        pl.BlockSpec((None, pr, kw), lambda bi, r, i: (bi, jnp.maximum(i * ratio - 1, 0), kcol(r))),
        pl.BlockSpec((None, tq, kw), lambda bi, r, i: (bi, i, kcol(r))),
        pl.BlockSpec((None, pr, kw), lambda bi, r, i: (bi, jnp.maximum(i * ratio - 1, 0), vcol(r))),
        pl.BlockSpec((None, tq, kw), lambda bi, r, i: (bi, i, vcol(r))),
    ]
    args = [qa, ka, ka, va, va]
    if sink_row is not None:
        in_specs.append(pl.BlockSpec(sink_row.shape, lambda bi, r, i: (0, 0)))
        args.append(sink_row)
    out_specs = [pl.BlockSpec((None, tq, qw), lambda bi, r, i: (bi, i, r))]
    out_shape = [jax.ShapeDtypeStruct((b, l, nrep * qw), CDT)]
    if want_lse:
        out_specs.append(pl.BlockSpec((None, tq, qw), lambda bi, r, i: (bi, i, r)))
        out_shape.append(jax.ShapeDtypeStruct((b, l, nrep * qw), F32))
    res = pl.pallas_call(
        functools.partial(_banded_kernel, window=window, pr=pr, tq=tq, kw=kw, want_lse=want_lse,
                          has_sink=sink_row is not None),
        grid=(b, nrep, l // tq),
        in_specs=in_specs,
        out_specs=out_specs,
        out_shape=out_shape,
        scratch_shapes=[pltpu.VMEM((pr + tq, kw), ka.dtype), pltpu.VMEM((pr + tq, kw), va.dtype)],
        compiler_params=_cparams(("parallel", "parallel", "parallel"), 32),
        name=f"banded_w{window}_k{kw}_r{nrep}",
    )(*args)
    return res


def _gelu_tanh(x):
    return 0.5 * x * (1.0 + jnp.tanh(0.7978845608028654 * (x + 0.044715 * (x * x * x))))


def _compress_kernel(x_ref, pa_ref, pb_ref, w1a_ref, w1b_ref, b1_ref, w2_ref, b2_ref, o_ref):
    x = x_ref[...].astype(F32)
    n = x.shape[0]
    a = _dot((x + pa_ref[...]).astype(CDT), w1a_ref[...])
    bm = _dot((x + pb_ref[...]).astype(CDT), w1b_ref[...])
    h = a + pltpu.roll(bm, n - 1, 0) + b1_ref[...]
    o_ref[...] = (_dot(_gelu_tanh(h).astype(CDT), w2_ref[...]) + b2_ref[...]).astype(o_ref.dtype)


def _compress(t, pos, w1, b1, w2, b2):
    b, s, _ = t.shape
    nch = s // CMP_STRIDE
    xw = CMP_STRIDE * LANES
    x = t.reshape(b, nch, xw)
    eye = jnp.eye(NSA_KV_HEADS, dtype=F32)
    w1r = w1.reshape(CMP_BLOCK, HEAD_DIM, CMP_HIDDEN)

    def expand_w1(part):
        return jnp.einsum('tdj,kl->tkdlj', part, eye).reshape(xw, NSA_KV_HEADS * CMP_HIDDEN).astype(CDT)

    def expand_pos(part):
        return jnp.broadcast_to(part[:, None, :], (CMP_STRIDE, NSA_KV_HEADS, HEAD_DIM)).reshape(1, xw)

    w1a, w1b = expand_w1(w1r[:CMP_STRIDE]), expand_w1(w1r[CMP_STRIDE:])
    pa, pb = expand_pos(pos[:CMP_STRIDE]), expand_pos(pos[CMP_STRIDE:])
    b1e = jnp.tile(b1, NSA_KV_HEADS).reshape(1, -1)
    w2e = jnp.einsum('jd,kl->kjld', w2, eye).reshape(NSA_KV_HEADS * CMP_HIDDEN, LANES).astype(CDT)
    b2e = jnp.tile(b2, NSA_KV_HEADS).reshape(1, -1)
    consts = [pa, pb, w1a, w1b, b1e, w2e, b2e]
    return pl.pallas_call(
        _compress_kernel,
        grid=(b,),
        in_specs=[pl.BlockSpec((None, nch, xw), lambda bi: (bi, 0, 0))]
        + [pl.BlockSpec(c.shape, lambda bi: (0, 0)) for c in consts],
        out_specs=pl.BlockSpec((None, nch, LANES), lambda bi: (bi, 0, 0)),
        out_shape=jax.ShapeDtypeStruct((b, nch, LANES), CDT),
        compiler_params=_cparams(("parallel",), 48),
        name="nsa_compress",
    )(x, *consts)


def _cmp_select_kernel(q_ref, kc_ref, vc_ref, ov_ref, o_ref, sel_ref, *, tq, n_sel, tile0):
    i = pl.program_id(1) + tile0
    ncp = kc_ref.shape[0]
    ns = ov_ref.shape[1]
    lo = _half_masks()
    t_col = i * tq + lax.broadcasted_iota(jnp.int32, (tq, 1), 0)
    c_end = lax.broadcasted_iota(jnp.int32, (tq, ncp), 1) * CMP_STRIDE + (CMP_BLOCK - 1)
    vis_bias = jnp.where(c_end <= t_col, 0.0, NEG_INF)
    has_visible = t_col >= CMP_BLOCK - 1
    blk = lax.broadcasted_iota(jnp.int32, (tq, ns), 1)
    blk_t = lax.broadcasted_iota(jnp.int32, (ns, tq), 0)
    cur = t_col // SEL_BLOCK
    causal = blk <= cur
    forced = (blk == 0) | (blk == cur) | (blk == cur - 1)
    kc = kc_ref[...]
    vc = vc_ref[...]
    o_kv = []
    for kv, hm in enumerate((lo, jnp.logical_not(lo))):
        psum = jnp.zeros((tq, ncp), F32)
        o_p = []
        for p in range(3):
            qm = (jnp.where(hm, q_ref[:, p * LANES:(p + 1) * LANES], 0) * QK_SCALE).astype(CDT)
            s = _nt_dot(qm, kc) + vis_bias
            m = jnp.max(s, axis=-1, keepdims=True)
            e = jnp.exp(s - m)
            inv = jnp.where(has_visible, 1.0 / jnp.maximum(jnp.sum(e, axis=-1, keepdims=True), 1e-30), 0.0)
            pn = e * inv
            o_p.append(_dot(pn.astype(CDT), vc))
            psum = psum + pn
        o_kv.append(o_p)
        p_hi = psum.astype(CDT)
        p_lo = (psum - p_hi.astype(F32)).astype(CDT)
        imp = _dot(p_hi, ov_ref[...]) + _dot(p_lo, ov_ref[...])
        work_t = jnp.where(causal & jnp.logical_not(forced), imp, -BIG).T

        def pick(_, carry):
            work, sel = carry
            m = jnp.max(work, axis=0, keepdims=True)
            idx = jnp.min(jnp.where(work == m, blk_t, ns), axis=0, keepdims=True)
            hit = blk_t == idx
            sel = jnp.where(hit & (m > -BIG), 1.0, sel)
            return jnp.where(hit, -2.0 * BIG, work), sel

        _, sel_t = lax.fori_loop(0, n_sel - 3, pick, (work_t, jnp.zeros((ns, tq), F32)))
        selb = jnp.where(forced | (sel_t.T > 0.5), 0.0, NEG_INF)
        sel_ref[:, kv * ns:(kv + 1) * ns] = selb.astype(sel_ref.dtype)
    for p in range(3):
        o_ref[:, p * LANES:(p + 1) * LANES] = jnp.where(lo, o_kv[0][p], o_kv[1][p]).astype(o_ref.dtype)


CMP_CAUSAL_SPLITS = 4


def _cmp_select(zr, qcol, kc, vc, tq=128):
    b, s, _ = zr.shape
    ncp = kc.shape[1]
    ns = s // SEL_BLOCK
    n_sel = min(N_SEL, ns)
    assert n_sel >= 3, "selection needs room for the three forced blocks"
    tq = min(tq, s)
    c = np.arange(ncp)[:, None] * CMP_STRIDE
    j = np.arange(ns)[None, :] * SEL_BLOCK
    overlap = ((c < j + SEL_BLOCK) & (c + CMP_BLOCK - 1 >= j)).astype(np.float32)
    overlap[ncp - 1:, :] = 0.0
    ov = jnp.asarray(overlap, CDT)
    qw = 3 * LANES
    n_split = CMP_CAUSAL_SPLITS if (s // tq) % CMP_CAUSAL_SPLITS == 0 and ncp % (16 * CMP_CAUSAL_SPLITS) == 0 else 1
    tiles = s // tq // n_split
    outs, sels = [], []
    for part in range(n_split):
        tile0 = part * tiles
        ncp_part = ncp * (part + 1) // n_split
        o_part, sel_part = pl.pallas_call(
            functools.partial(_cmp_select_kernel, tq=tq, n_sel=n_sel, tile0=tile0),
            grid=(b, tiles),
            in_specs=[pl.BlockSpec((None, tq, qw), lambda bi, i, tile0=tile0: (bi, i + tile0, qcol)),
                      pl.BlockSpec((None, ncp_part, LANES), lambda bi, i: (bi, 0, 0)),
                      pl.BlockSpec((None, ncp_part, LANES), lambda bi, i: (bi, 0, 0)),
                      pl.BlockSpec((ncp_part, ns), lambda bi, i: (0, 0))],
            out_specs=[pl.BlockSpec((None, tq, qw), lambda bi, i: (bi, i, 0)),
                       pl.BlockSpec((None, tq, 2 * ns), lambda bi, i: (bi, i, 0))],
            out_shape=[jax.ShapeDtypeStruct((b, tiles * tq, qw), CDT),
                       jax.ShapeDtypeStruct((b, tiles * tq, 2 * ns), CDT)],
            compiler_params=_cparams(("parallel", "parallel"), 48),
            name=f"nsa_cmp_select_p{part}",
        )(zr, kc, vc, ov)
        outs.append(o_part)
        sels.append(sel_part)
    return jnp.concatenate(outs, axis=1), jnp.concatenate(sels, axis=1)


SLC_ROWS = 64


def _slc_kernel(q_ref, k0_ref, k1_ref, v_ref, selb_ref, o_ref, s_ref, e_ref, m_ref, l_ref, a_ref, acc_ref,
                *, tq, tk):
    i = pl.program_id(1)
    ns = selb_ref.shape[1] // 2
    bpt = tk // SEL_BLOCK
    rb = SLC_ROWS
    lo = _half_masks()
    n_tiles = ((i + 1) * tq + tk - 1) // tk
    p_row = lax.broadcasted_iota(jnp.int32, (ns, LANES), 0)
    p_col = lax.broadcasted_iota(jnp.int32, (ns, LANES), 1)
    groups = []
    for kv, (hm, k_ref) in enumerate(((lo, k0_ref), (jnp.logical_not(lo), k1_ref))):
        q3 = [(jnp.where(hm, q_ref[:, p * LANES:(p + 1) * LANES], 0) * QK_SCALE).astype(CDT) for p in range(3)]
        selb = selb_ref[:, kv * ns:(kv + 1) * ns]
        lane0 = HEAD_DIM if kv == 0 else 0
        groups.append((kv, hm, k_ref, q3, selb, lane0))
    m_ref[...] = jnp.full(m_ref.shape, NEG_INF, F32)
    l_ref[...] = jnp.zeros(l_ref.shape, F32)
    acc_ref[...] = jnp.zeros(acc_ref.shape, F32)

    def scores(j):
        k0 = pl.multiple_of(j * tk, tk)
        for kv, hm, k_ref, q3, selb, lane0 in groups:
            place = ((p_col >= lane0) & (p_col < lane0 + bpt) & (p_row == p_col - lane0 + j * bpt)).astype(CDT)
            sb = _dot(selb, place).astype(CDT)
            qp = jnp.concatenate([jnp.where(hm, q, sb) for q in q3], axis=0)
            s_ref[kv] = _nt_dot(qp, k_ref[pl.ds(k0, tk), :])

    def softmax_pv(j, diagonal):
        k0 = pl.multiple_of(j * tk, tk)
        for kv in range(2):
            for r0 in range(0, 3 * tq, rb):
                rows = slice(r0, r0 + rb)
                s = s_ref[kv, rows, :]
                if diagonal:
                    t_row = i * tq + (r0 % tq) + lax.broadcasted_iota(jnp.int32, (rb, 1), 0)
                    kpos = k0 + lax.broadcasted_iota(jnp.int32, (rb, tk), 1)
                    s = jnp.where(kpos <= t_row, s, NEG_INF)
                m_old = m_ref[kv, rows, :]
                m_new = jnp.maximum(m_old, jnp.max(s, axis=-1, keepdims=True))
                alpha = jnp.exp(m_old - m_new)
                e = jnp.exp(s - jnp.tile(m_new, (1, tk // LANES)))
                l_ref[kv, rows, :] = alpha * l_ref[kv, rows, :] + jnp.sum(e, axis=-1, keepdims=True)
                m_ref[kv, rows, :] = m_new
                a_ref[kv, rows, :] = alpha
                e_ref[kv, rows, :] = e.astype(CDT)
        for kv in range(2):
            acc_ref[kv] = a_ref[kv] * acc_ref[kv] + _dot(e_ref[kv], v_ref[pl.ds(k0, tk), :])

    def tile(j, diagonal):
        scores(j)
        softmax_pv(j, diagonal)

    lax.fori_loop(0, n_tiles - 1, lambda j, c: (tile(j, False), c)[1], 0)
    tile(n_tiles - 1, True)
    o0 = acc_ref[0] / l_ref[0]
    o1 = acc_ref[1] / l_ref[1]
    for p in range(3):
        o_ref[:, p * LANES:(p + 1) * LANES] = jnp.where(
            lo, o0[p * tq:(p + 1) * tq], o1[p * tq:(p + 1) * tq]).astype(o_ref.dtype)


def _slc(zr, qcol, ksc, vcol, selb, tq=256, tk=1024):
    b, s, _ = zr.shape
    tq = min(tq, s)
    tk = min(tk, s)
    assert tq % SLC_ROWS == 0 and tk // SEL_BLOCK <= HEAD_DIM
    qw = 3 * LANES
    ns2 = selb.shape[2]
    pat = jax.nn.one_hot((jnp.arange(s) // SEL_BLOCK) % (tk // SEL_BLOCK), HEAD_DIM, dtype=ksc.dtype)
    pat = jnp.broadcast_to(pat[None], (b, s, HEAD_DIM))
    k0 = jnp.concatenate([ksc[..., :HEAD_DIM], pat], axis=-1)
    k1 = jnp.concatenate([pat, ksc[..., HEAD_DIM:]], axis=-1)
    full = lambda bi, i: (bi, 0, 0)
    return pl.pallas_call(
        functools.partial(_slc_kernel, tq=tq, tk=tk),
        grid=(b, s // tq),
        in_specs=[pl.BlockSpec((None, tq, qw), lambda bi, i: (bi, i, qcol)),
                  pl.BlockSpec((None, s, LANES), full),
                  pl.BlockSpec((None, s, LANES), full),
                  pl.BlockSpec((None, s, LANES), lambda bi, i: (bi, 0, vcol)),
                  pl.BlockSpec((None, tq, ns2), lambda bi, i: (bi, i, 0))],
        out_specs=pl.BlockSpec((None, tq, qw), lambda bi, i: (bi, i, 0)),
        out_shape=jax.ShapeDtypeStruct((b, s, qw), CDT),
        scratch_shapes=[pltpu.VMEM((2, 3 * tq, tk), F32), pltpu.VMEM((2, 3 * tq, tk), CDT),
                        pltpu.VMEM((2, 3 * tq, LANES), F32), pltpu.VMEM((2, 3 * tq, LANES), F32),
                        pltpu.VMEM((2, 3 * tq, LANES), F32), pltpu.VMEM((2, 3 * tq, LANES), F32)],
        compiler_params=_cparams(("parallel", "arbitrary"), 56),
        name="nsa_slc",
    )(zr, k0, k1, zr, selb)


def _merge_kernel(x_ref, oa0, oa1, oa2, la0, la1, la2, ob, ocmp, oslc, owin, gates, od,
                  wg_ref, bg_ref, wb_ref, wo_ref, g_ref, b_ref, o_ref, *scratch, alpha):
    x = x_ref[...]
    xb = x.astype(CDT)
    bw = BRANCH_WIDTH

    def token_rows(src_ref, dst_ref):
        dil = src_ref.shape[1] // bw
        if dil == 1:
            return src_ref[...].astype(F32)
        n_cb = bw // LANES
        for r in range(dil):
            for cb in range(n_cb):
                c0 = r * bw + cb * LANES
                dst_ref[cb, pl.ds(r, src_ref.shape[0], stride=dil), :] = src_ref[:, c0:c0 + LANES].astype(F32)
        return jnp.concatenate([dst_ref[cb] for cb in range(n_cb)], axis=1)

    o0, o1, o2 = (token_rows(s, d) for s, d in zip((oa0, oa1, oa2), scratch[0:3]))
    l0, l1, l2 = (token_rows(s, d) for s, d in zip((la0, la1, la2), scratch[3:6]))
    mx = jnp.maximum(jnp.maximum(l0, l1), l2)
    w0, w1, w2 = jnp.exp(l0 - mx), jnp.exp(l1 - mx), jnp.exp(l2 - mx)
    o_a = (w0 * o0 + w1 * o1 + w2 * o2) / (w0 + w1 + w2)
    o_c = (gates[:, 0:bw] * ocmp[...].astype(F32) + gates[:, bw:2 * bw] * oslc[...].astype(F32)
           + gates[:, 2 * bw:3 * bw] * owin[...].astype(F32))
    branches = (o_a.astype(CDT), ob[...], o_c.astype(CDT), od[...])
    d = x.shape[1]
    merged = jnp.zeros(x.shape, F32)
    for m in range(N_BRANCH):
        gate = jax.nn.sigmoid(_dot(xb, wg_ref[:, m * d:(m + 1) * d]) + bg_ref[:, m * d:(m + 1) * d])
        merged = merged + gate * _dot(branches[m], wb_ref[m])
    r = alpha * x + _dot(merged.astype(CDT), wo_ref[...])
    o_ref[...] = _layer_norm(r, g_ref[...], b_ref[...])


def _merge(x2d, branch_inputs, wg, bg, wb, wo, g, b, alpha, tm=256):
    t, d = x2d.shape
    tm = min(tm, t)
    row = lambda i: (i, 0)
    const2 = lambda i: (0, 0)
    in_specs = [pl.BlockSpec((tm, d), row)]
    in_specs += [pl.BlockSpec((tm * a.shape[0] // t, a.shape[1]), row) for a in branch_inputs]
    in_specs += [pl.BlockSpec(wg.shape, const2), pl.BlockSpec(bg.shape, const2),
                 pl.BlockSpec(wb.shape, lambda i: (0, 0, 0)), pl.BlockSpec(wo.shape, const2),
                 pl.BlockSpec(g.shape, const2), pl.BlockSpec(b.shape, const2)]
    return pl.pallas_call(
        functools.partial(_merge_kernel, alpha=alpha),
        grid=(t // tm,),
        in_specs=in_specs,
        out_specs=pl.BlockSpec((tm, d), row),
        out_shape=jax.ShapeDtypeStruct((t, d), F32),
        scratch_shapes=[pltpu.VMEM((BRANCH_WIDTH // LANES, tm, LANES), F32) for _ in range(2 * N_DIL)],
        compiler_params=_cparams(("parallel",), 56),
        name="merge_ln",
    )(x2d, *branch_inputs, wg, bg, wb, wo, g, b)


def _ple_ln(x, xb, f, p, plw_ref, pgw_ref, pgb_ref, g_ref, b_ref, alpha):
    ple = jax.nn.sigmoid(_dot(xb, pgw_ref[...]) + pgb_ref[...]) * _dot(p.astype(CDT), plw_ref[...])
    return _layer_norm(alpha * x + f + ple, g_ref[...], b_ref[...])


FFN_CHUNK = 512


def _ffn_kernel(x_ref, p_ref, wg_ref, wu_ref, wd_ref, plw_ref, pgw_ref, pgb_ref, g_ref, b_ref, o_ref, h_ref, *, alpha):
    x = x_ref[...]
    xb = x.astype(CDT)
    dff = wg_ref.shape[1]
    for c0 in range(0, dff, FFN_CHUNK):
        cols = slice(c0, min(c0 + FFN_CHUNK, dff))
        h_ref[:, cols] = (jax.nn.silu(_dot(xb, wg_ref[:, cols])) * _dot(xb, wu_ref[:, cols])).astype(CDT)
    f = _dot(h_ref[...], wd_ref[...])
    o_ref[...] = _ple_ln(x, xb, f, p_ref[...], plw_ref, pgw_ref, pgb_ref, g_ref, b_ref, alpha)


def _ffn(x2d, p2d, wg, wu, wd, plw, pgw, pgb, g, b, alpha, tm=512):
    t, d = x2d.shape
    tm = min(tm, t)
    dff = wg.shape[1]
    wg, wu, wd = wg.astype(CDT), wu.astype(CDT), wd.astype(CDT)
    row = lambda i: (i, 0)
    const = lambda shape: pl.BlockSpec(shape, lambda i: (0, 0), pipeline_mode=pl.Buffered(1))
    return pl.pallas_call(
        functools.partial(_ffn_kernel, alpha=alpha),
        grid=(t // tm,),
        in_specs=[pl.BlockSpec((tm, d), row), pl.BlockSpec((tm, p2d.shape[1]), row),
                  const(wg.shape), const(wu.shape), const(wd.shape),
                  const(plw.shape), const(pgw.shape), const(pgb.shape), const(g.shape), const(b.shape)],
        out_specs=pl.BlockSpec((tm, d), row),
        out_shape=jax.ShapeDtypeStruct((t, d), F32),
        scratch_shapes=[pltpu.VMEM((tm, dff), CDT)],
        compiler_params=_cparams(("parallel",), 48),
        name="ffn_ple_ln",
    )(x2d, p2d, wg, wu, wd, plw, pgw, pgb, g, b)


def _router_kernel(x_ref, wh_ref, wl_ref, b_ref, comb_ref, rank_ref, rank_t_ref, cnt_ref):
    x = x_ref[...]
    xh = x.astype(CDT)
    xl = (x - xh.astype(F32)).astype(CDT)
    logits = _dot(xh, wh_ref[...]) + _dot(xh, wl_ref[...]) + _dot(xl, wh_ref[...]) + b_ref[...]
    lane = lax.broadcasted_iota(jnp.int32, logits.shape, 1)
    v1 = jnp.max(logits, axis=-1, keepdims=True)
    i1 = jnp.min(jnp.where(logits == v1, lane, LANES), axis=-1, keepdims=True)
    rest = jnp.where(lane == i1, -jnp.inf, logits)
    v2 = jnp.max(rest, axis=-1, keepdims=True)
    i2 = jnp.min(jnp.where(rest == v2, lane, LANES), axis=-1, keepdims=True)
    e2 = jnp.exp(v2 - v1)
    comb_ref[...] = jnp.where(lane == i1, 1.0 / (1.0 + e2), 0.0) + jnp.where(lane == i2, e2 / (1.0 + e2), 0.0)
    routed = (lane == i1) | (lane == i2)
    mask = routed.astype(CDT)
    tm = x.shape[0]
    before = (lax.broadcasted_iota(jnp.int32, (tm, tm), 1) < lax.broadcasted_iota(jnp.int32, (tm, tm), 0)).astype(CDT)
    rank = jnp.where(routed, _dot(before, mask), -1.0)
    rank_ref[...] = rank
    rank_t_ref[...] = rank.T[0:rank_t_ref.shape[0], :]
    cnt_ref[...] = jnp.sum(routed.astype(F32), axis=0, keepdims=True).astype(jnp.int32)


def _router(x2d, w_router, b_router, tm):
    t, d = x2d.shape
    ne = w_router.shape[1]
    wp = jnp.zeros((d, LANES), F32).at[:, :ne].set(w_router)
    wh = wp.astype(CDT)
    wl = (wp - wh.astype(F32)).astype(CDT)
    bp = jnp.full((1, LANES), -BIG, F32).at[0, :ne].set(b_router)
    nt = t // tm
    row = lambda i: (i, 0)
    return pl.pallas_call(
        _router_kernel,
        grid=(nt,),
        in_specs=[pl.BlockSpec((tm, d), row), pl.BlockSpec(wh.shape, lambda i: (0, 0)),
                  pl.BlockSpec(wl.shape, lambda i: (0, 0)), pl.BlockSpec(bp.shape, lambda i: (0, 0))],
        out_specs=[pl.BlockSpec((tm, LANES), row), pl.BlockSpec((tm, LANES), row),
                   pl.BlockSpec((None, 8, tm), lambda i: (i, 0, 0)),
                   pl.BlockSpec((None, 1, LANES), lambda i: (i, 0, 0))],
        out_shape=[jax.ShapeDtypeStruct((t, LANES), F32), jax.ShapeDtypeStruct((t, LANES), F32),
                   jax.ShapeDtypeStruct((nt, 8, tm), F32), jax.ShapeDtypeStruct((nt, 1, LANES), jnp.int32)],
        compiler_params=_cparams(("parallel",), 40),
        name="moe_router",
    )(x2d, wh, wl, bp)


def _moe_kernel(cnt_ref, x_ref, comb_ref, rank_ref, rank_t_ref, p_ref, wg_ref, wu_ref, wd_ref, plw_ref, pgw_ref,
                pgb_ref, g_ref, b_ref, o_ref, xe_ref, ye_ref, *, alpha, rs, seg):
    i = pl.program_id(0)
    e = pl.program_id(1)
    c = pl.program_id(2)
    n_seg = x_ref.shape[0] // seg
    last_chunk = c == pl.num_programs(2) - 1

    def rows(sc):
        return pl.ds(pl.multiple_of(sc * rs, 8), rs)

    @pl.when((e == 0) & (c == 0))
    def _():
        o_ref[...] = jnp.zeros_like(o_ref)

    for sg in range(n_seg):
        tok = slice(sg * seg, (sg + 1) * seg)
        n_groups = (cnt_ref[(i * n_seg + sg) * LANES + e] + rs - 1) // rs

        @pl.when(c == 0)
        def _():
            xb = x_ref[tok, :].astype(CDT)
            rank_row = rank_t_ref[sg, pl.ds(e, 1), :]
            row_id = lax.broadcasted_iota(jnp.int32, (rs, seg), 0).astype(F32)

            def gather(sc, _):
                onehot = (rank_row - (sc * rs).astype(F32) == row_id).astype(CDT)
                xe_ref[sg, rows(sc), :] = _dot(onehot, xb).astype(CDT)
                ye_ref[sg, rows(sc), :] = jnp.zeros((rs, ye_ref.shape[2]), F32)
                return 0

            lax.fori_loop(0, n_groups, gather, 0)

        def expert(sc, _):
            xs = xe_ref[sg, rows(sc), :]
            h = jax.nn.silu(_dot(xs, wg_ref[0])) * _dot(xs, wu_ref[0])
            ye_ref[sg, rows(sc), :] += _dot(h.astype(CDT), wd_ref[0])
            return 0

        lax.fori_loop(0, n_groups, expert, 0)

        @pl.when(last_chunk)
        def _():
            lane = lax.broadcasted_iota(jnp.int32, (seg, LANES), 1)
            mine = lane == e
            cw = jnp.sum(jnp.where(mine, comb_ref[tok, :], 0.0), axis=-1, keepdims=True)
            rank_col = jnp.sum(jnp.where(mine, rank_ref[tok, :], 0.0), axis=-1, keepdims=True)
            col_id = lax.broadcasted_iota(jnp.int32, (seg, rs), 1).astype(F32)

            def scatter(sc, _):
                onehot = (rank_col - (sc * rs).astype(F32) == col_id).astype(CDT)
                o_ref[tok, :] += cw * _dot(onehot, ye_ref[sg, rows(sc), :].astype(CDT))
                return 0

            lax.fori_loop(0, n_groups, scatter, 0)

    @pl.when((e == pl.num_programs(1) - 1) & last_chunk)
    def _():
        for sg in range(n_seg):
            tok = slice(sg * seg, (sg + 1) * seg)
            x = x_ref[tok, :]
            o_ref[tok, :] = _ple_ln(x, x.astype(CDT), o_ref[tok, :], p_ref[tok, :], plw_ref, pgw_ref, pgb_ref,
                                    g_ref, b_ref, alpha)


MOE_CHUNK = 512
MOE_SEGMENT = 1024
MOE_SEGMENTS_PER_TILE = 2
MOE_ROW_GROUP = 288


def _single_buffered(shape, index_map):
    return pl.BlockSpec(shape, index_map, pipeline_mode=pl.Buffered(1))


def _moe(x2d, routing, p2d, wg, wu, wd, plw, pgw, pgb, g, b, alpha, seg):
    comb, rank, rank_t, cnt = routing
    t, d = x2d.shape
    ne, _, dff = wg.shape
    ck = min(MOE_CHUNK, dff)
    rs = min(MOE_ROW_GROUP, seg)
    n_seg = min(MOE_SEGMENTS_PER_TILE, t // seg)
    tm = n_seg * seg
    max_rows = -(-seg // rs) * rs
    nck = dff // ck
    wg = wg.reshape(ne, d, nck, ck).transpose(0, 2, 1, 3).astype(CDT)
    wu = wu.reshape(ne, d, nck, ck).transpose(0, 2, 1, 3).astype(CDT)
    wd = wd.reshape(ne, nck, ck, d).astype(CDT)
    row = lambda i, e, c, cnt: (i, 0)
    c2 = lambda i, e, c, cnt: (0, 0)
    grid_spec = pltpu.PrefetchScalarGridSpec(
        num_scalar_prefetch=1,
        grid=(t // tm, ne, dff // ck),
        in_specs=[_single_buffered((tm, d), row), _single_buffered((tm, LANES), row),
                  _single_buffered((tm, LANES), row),
                  _single_buffered((n_seg, 8, seg), lambda i, e, c, cnt: (i, 0, 0)),
                  _single_buffered((tm, p2d.shape[1]), row),
                  pl.BlockSpec((None, 1, d, ck), lambda i, e, c, cnt: (e, c, 0, 0)),
                  pl.BlockSpec((None, 1, d, ck), lambda i, e, c, cnt: (e, c, 0, 0)),
                  pl.BlockSpec((None, 1, ck, d), lambda i, e, c, cnt: (e, c, 0, 0)),
                  _single_buffered(plw.shape, c2), _single_buffered(pgw.shape, c2), _single_buffered(pgb.shape, c2),
                  _single_buffered(g.shape, c2), _single_buffered(b.shape, c2)],
        out_specs=pl.BlockSpec((tm, d), row),
        scratch_shapes=[pltpu.VMEM((n_seg, max_rows, d), CDT), pltpu.VMEM((n_seg, max_rows, d), F32)],
    )
    return pl.pallas_call(
        functools.partial(_moe_kernel, alpha=alpha, rs=rs, seg=seg),
        grid_spec=grid_spec,
        out_shape=jax.ShapeDtypeStruct((t, d), F32),
        compiler_params=_cparams(("parallel", "arbitrary", "arbitrary"), 60),
        name="moe_ple_ln",
    )(cnt.reshape(-1), x2d, comb, rank, rank_t, p2d, wg, wu, wd, plw, pgw, pgb, g, b)


def _prep_in_weights(w_in):
    o = COL_OFF
    bw = BRANCH_WIDTH
    cols = lambda n: w_in[:, o[n]:o[n + 1]]
    qa, ka, va = cols(0), cols(1), cols(2)
    w_dil = jnp.concatenate(
        [t[:, g * bw:(g + 1) * bw] for g in range(N_DIL) for t in (qa, ka, va)], axis=1).astype(CDT)
    w_conv = jnp.concatenate([cols(3), cols(4), cols(5)], axis=1).astype(CDT)
    gn = cols(13)
    w_gate = jnp.concatenate([gn[:, br * NSA_Q_HEADS + GQA_COL_HEAD] for br in range(3)], axis=1).astype(CDT)
    w_rest = jnp.concatenate([cols(6)[:, GQA_COL_PERM], cols(14)[:, GQA_COL_PERM], cols(10), cols(11),
                              cols(12), cols(15), cols(16), cols(9), cols(7), cols(8)], axis=1).astype(CDT)
    return w_dil, w_conv, w_gate, w_rest


ZR_Q_NSA, ZR_Q_SWA = 0, 1
ZR_VSC, ZR_KWC, ZR_VWC, ZR_KD, ZR_VD = 6, 7, 8, 9, 10
ZR_WIDTH = 2 * BRANCH_WIDTH + 5 * LANES


def _token_mixers(x, w_in, conv_w, cmp_pos, cmp_w1, cmp_b1, cmp_w2, cmp_b2, sinks):
    b, s, d = x.shape
    x2d = x.reshape(b * s, d)
    w_dil, w_conv, w_gate, w_rest = _prep_in_weights(w_in)
    gw = 3 * BRANCH_WIDTH

    z_dil = _linear(x2d, w_dil, [(g * gw, (g + 1) * gw) for g in range(N_DIL)], gw, "in_proj_dil",
                    dils=[dil for _, dil in DIL_PATTERNS])
    zr, ksc, kcc, vcc = _linear(x2d, w_rest, [(0, ZR_WIDTH)] + [(ZR_WIDTH + n * LANES, ZR_WIDTH + (n + 1) * LANES)
                                                           for n in range(3)], 256, "in_proj_rest")
    zr = zr.reshape(b, s, ZR_WIDTH)
    o_b, gates = _conv_gate(x, w_conv, conv_w, w_gate)

    dil_o, dil_lse = [], []
    for g, (window, dil) in enumerate(DIL_PATTERNS):
        view = z_dil[g].reshape(b, s // dil, dil * gw)
        og, lg = _banded(view, view, view, nrep=dil,
                         qcol=lambda r: 3 * r, kcol=lambda r: 3 * r + 1, vcol=lambda r: 3 * r + 2,
                         kw=3 * LANES, window=window // dil, want_lse=True)
        dil_o.append(og.reshape(b * s // dil, dil * BRANCH_WIDTH))
        dil_lse.append(lg.reshape(b * s // dil, dil * BRANCH_WIDTH))

    kc = _compress(kcc.reshape(b, s, LANES), cmp_pos[0], cmp_w1[0], cmp_b1[0], cmp_w2[0], cmp_b2[0])
    vc = _compress(vcc.reshape(b, s, LANES), cmp_pos[1], cmp_w1[1], cmp_b1[1], cmp_w2[1], cmp_b2[1])
    o_cmp, selb = _cmp_select(zr, ZR_Q_NSA, kc, vc)
    o_slc = _slc(zr, ZR_Q_NSA, ksc.reshape(b, s, LANES), ZR_VSC, selb)
    (o_win,) = _banded(zr, zr, zr, nrep=1, qcol=lambda r: ZR_Q_NSA, kcol=lambda r: ZR_KWC, vcol=lambda r: ZR_VWC,
                       kw=LANES, window=NSA_WINDOW - 1, want_lse=False, tq=512)

    sink_row = sinks.astype(F32)[GQA_COL_HEAD].reshape(1, BRANCH_WIDTH)
    (o_d,) = _banded(zr, zr, zr, nrep=1, qcol=lambda r: ZR_Q_SWA, kcol=lambda r: ZR_KD, vcol=lambda r: ZR_VD,
                     kw=LANES, window=SWA_WINDOW - 1, want_lse=False, sink_row=sink_row)

    t = b * s
    flat = lambda a: a.reshape(t, a.shape[-1])
    return [dil_o[0], dil_o[1], dil_o[2], dil_lse[0], dil_lse[1], dil_lse[2], flat(o_b), flat(o_cmp), flat(o_slc),
            flat(o_win), flat(gates), flat(o_d)]


def kernel(x, p, w_in, conv_w, cmp_pos, cmp_w1, cmp_b1, cmp_w2, cmp_b2, sinks, w_branch, w_merge_gate, b_merge_gate, w_out, ln_mix_g, ln_mix_b, ffn_w_gate, ffn_w_up, ffn_w_down, w_router, b_router, moe_w_gate, moe_w_up, moe_w_down, ple_w, ple_gate_w, ple_gate_b, ln_ffn_g, ln_ffn_b):
    depth, b, s, _ = p.shape
    d = x.shape[-1]
    t = b * s
    alpha = (2 * depth) ** 0.25
    row = lambda v: v.reshape(1, -1).astype(F32)
    for i in range(depth):
        branch_inputs = _token_mixers(x, w_in[i], conv_w[i], cmp_pos[i], cmp_w1[i], cmp_b1[i], cmp_w2[i],
                                      cmp_b2[i], sinks[i])
        wg = jnp.concatenate([w_merge_gate[i, m] for m in range(N_BRANCH)], axis=1).astype(CDT)
        bg = b_merge_gate[i].reshape(1, N_BRANCH * d).astype(F32)
        wb = jnp.stack([w_branch[i, 0], w_branch[i, 1], w_branch[i, 2][GQA_COL_PERM],
                        w_branch[i, 3][GQA_COL_PERM]]).astype(CDT)
        x1 = _merge(x.reshape(t, d), branch_inputs, wg, bg, wb, w_out[i].astype(CDT),
                    row(ln_mix_g[i]), row(ln_mix_b[i]), alpha)
        p2d = p[i].reshape(t, -1)
        ple_args = (ple_w[i].astype(CDT), ple_gate_w[i].astype(CDT), row(ple_gate_b[i]),
                    row(ln_ffn_g[i]), row(ln_ffn_b[i]))
        j = i // 2
        if i % 2 == 0:
            x2 = _ffn(x1, p2d, ffn_w_gate[j], ffn_w_up[j], ffn_w_down[j], *ple_args, alpha)
        else:
            seg = min(MOE_SEGMENT, t)
            routing = _router(x1, w_router[j], b_router[j], seg)
            x2 = _moe(x1, routing, p2d, moe_w_gate[j], moe_w_up[j], moe_w_down[j], *ple_args, alpha, seg)
        x = x2.reshape(b, s, d)
    return x
```

```python
import functools

import numpy as np
import jax
import jax.numpy as jnp
from jax import lax
from jax.experimental import pallas as pl
from jax.experimental.pallas import tpu as pltpu

D_MODEL = 1024
HEAD_DIM = 64
DIL_PATTERNS = ((128, 1), (512, 4), (2048, 16))
N_DIL = 3
DIL_HEADS = 6
CONV_WIDTH = 384
CONV_K = 3
NSA_Q_HEADS = 6
NSA_KV_HEADS = 2
CMP_BLOCK = 32
CMP_STRIDE = 16
CMP_HIDDEN = 128
SEL_BLOCK = 64
N_SEL = 16
NSA_WINDOW = 512
SWA_Q_HEADS = 6
SWA_WINDOW = 128
BRANCH_WIDTH = 384
N_BRANCH = 4
N_EXPERTS = 8
LN_EPS = 1e-5
NEG_INF = -1e30
DIL_WIDTH = N_DIL * DIL_HEADS * HEAD_DIM
COLUMN_SIZES = (DIL_WIDTH, DIL_WIDTH, DIL_WIDTH, CONV_WIDTH, CONV_WIDTH, CONV_WIDTH,
                NSA_Q_HEADS * HEAD_DIM, 128, 128, 128, 128, 128, 128, 3 * NSA_Q_HEADS,
                SWA_Q_HEADS * HEAD_DIM, 128, 128)
COL_OFF = np.concatenate([[0], np.cumsum(COLUMN_SIZES)]).tolist()

LANES = 128
V7X_VMEM_BYTES = 64 * 1024 * 1024
MIB = 1024 * 1024

CDT = jnp.bfloat16
F32 = jnp.float32
QK_SCALE = HEAD_DIM ** -0.5
SUB_Q = 128
BAND_ROWS = 64
BIG = 1e30

_GQA_HEAD_ORDER = (0, 3, 1, 4, 2, 5)
GQA_COL_PERM = np.concatenate([np.arange(h * HEAD_DIM, (h + 1) * HEAD_DIM) for h in _GQA_HEAD_ORDER])
GQA_COL_HEAD = GQA_COL_PERM // HEAD_DIM


def _cparams(sem, vmem_mib):
    return pltpu.CompilerParams(dimension_semantics=sem, vmem_limit_bytes=int(vmem_mib * MIB))


def _nt_dot(a, b):
    return lax.dot_general(a, b, (((1,), (1,)), ((), ())), preferred_element_type=F32)


def _dot(a, b):
    return jnp.dot(a, b, preferred_element_type=F32)


def _layer_norm(r, g, b):
    mu = jnp.mean(r, axis=-1, keepdims=True)
    d = r - mu
    var = jnp.mean(d * d, axis=-1, keepdims=True)
    return d * lax.rsqrt(var + LN_EPS) * g + b


def _half_masks():
    lane = lax.broadcasted_iota(jnp.int32, (1, LANES), 1)
    return lane < HEAD_DIM


def _linear_kernel(x_ref, w_ref, *refs, splits, n_chunk, dils):
    o_refs, z_ref = refs[:len(splits)], refs[len(splits)]
    xb = x_ref[...].astype(CDT)
    tm = xb.shape[0]
    for o_ref, (c0, c1), dil in zip(o_refs, splits, dils):
        width = c1 - c0
        if dil == 1:
            for a in range(c0, c1, n_chunk):
                b = min(a + n_chunk, c1)
                o_ref[:, a - c0:b - c0] = _dot(xb, w_ref[:, a:b]).astype(o_ref.dtype)
        else:
            z = _dot(xb, w_ref[:, c0:c1])
            for cb in range(width // LANES):
                z_ref[cb] = z[:, cb * LANES:(cb + 1) * LANES]
            for r in range(dil):
                for cb in range(width // LANES):
                    o_ref[:, r * width + cb * LANES:r * width + (cb + 1) * LANES] = (
                        z_ref[cb, pl.ds(r, tm // dil, stride=dil), :].astype(o_ref.dtype))


def _linear(x2d, w, splits, n_chunk, name, dils=None, tm=512):
    t, k = x2d.shape
    tm = min(tm, t)
    n = w.shape[1]
    dils = tuple(dils) if dils is not None else (1,) * len(splits)
    widths = [c1 - c0 for c0, c1 in splits]
    assert all(tm % (16 * dl) == 0 for dl in dils)
    return pl.pallas_call(
        functools.partial(_linear_kernel, splits=tuple(splits), n_chunk=n_chunk, dils=dils),
        grid=(t // tm,),
        in_specs=[pl.BlockSpec((tm, k), lambda i: (i, 0)),
                  pl.BlockSpec((k, n), lambda i: (0, 0))],
        out_specs=[pl.BlockSpec((tm // dl, dl * wd), lambda i: (i, 0)) for wd, dl in zip(widths, dils)],
        out_shape=[jax.ShapeDtypeStruct((t // dl, dl * wd), CDT) for wd, dl in zip(widths, dils)],
        scratch_shapes=[pltpu.VMEM((max(widths) // LANES, tm, LANES), F32)],
        compiler_params=_cparams(("parallel",), 48),
        name=name,
    )(x2d, w)


def _conv_gate_kernel(x_ref, xh_ref, wc_ref, cw_ref, wg_ref, ob_ref, g_ref, *, tm):
    i = pl.program_id(1)
    w = CONV_WIDTH
    xb = x_ref[...].astype(CDT)
    z = _dot(xb, wc_ref[...])
    u = z[:, w:2 * w] * z[:, 2 * w:3 * w]
    zh = _dot(xh_ref[...].astype(CDT), wc_ref[:, w:3 * w])
    uh = zh[:, :w] * zh[:, w:]
    uh = jnp.where(i == 0, 0.0, uh)
    row = lax.broadcasted_iota(jnp.int32, (tm, w), 0)
    u1 = jnp.where(row == 0, uh[7:8, :], pltpu.roll(u, 1, 0))
    u2 = jnp.where(row == 0, uh[6:7, :], jnp.where(row == 1, uh[7:8, :], pltpu.roll(u, 2, 0)))
    y = cw_ref[0:1, :] * u2 + cw_ref[1:2, :] * u1 + cw_ref[2:3, :] * u
    ob_ref[...] = (z[:, :w] * y).astype(ob_ref.dtype)
    g_ref[...] = jax.nn.sigmoid(_dot(xb, wg_ref[...]))


def _conv_gate(x, wc, conv_w, wg, tm=512):
    b, s, d = x.shape
    tm = min(tm, s)
    hb = tm // 8
    return pl.pallas_call(
        functools.partial(_conv_gate_kernel, tm=tm),
        grid=(b, s // tm),
        in_specs=[pl.BlockSpec((None, tm, d), lambda bi, i: (bi, i, 0)),
                  pl.BlockSpec((None, 8, d), lambda bi, i: (bi, jnp.maximum(i * hb - 1, 0), 0)),
                  pl.BlockSpec(wc.shape, lambda bi, i: (0, 0)),
                  pl.BlockSpec(conv_w.shape, lambda bi, i: (0, 0)),
                  pl.BlockSpec(wg.shape, lambda bi, i: (0, 0))],
        out_specs=[pl.BlockSpec((None, tm, CONV_WIDTH), lambda bi, i: (bi, i, 0)),
                   pl.BlockSpec((None, tm, wg.shape[1]), lambda bi, i: (bi, i, 0))],
        out_shape=[jax.ShapeDtypeStruct((b, s, CONV_WIDTH), CDT),
                   jax.ShapeDtypeStruct((b, s, wg.shape[1]), F32)],
        compiler_params=_cparams(("parallel", "parallel"), 48),
        name="conv_gate",
    )(x, x, wc, conv_w, wg)


def _banded_kernel(*refs, window, pr, tq, kw, want_lse, has_sink):
    q_ref, kp_ref, kc_ref, vp_ref, vc_ref = refs[:5]
    n = 5
    sink_ref = None
    if has_sink:
        sink_ref = refs[n]
        n += 1
    o_ref = refs[n]
    n += 1
    lse_ref = None
    if want_lse:
        lse_ref = refs[n]
        n += 1
    kbuf, vbuf, s_ref, e_ref, m_ref, l_ref = refs[n:n + 6]

    i = pl.program_id(2)
    kbuf[0:pr, :] = kp_ref[...]
    kbuf[pr:pr + tq, :] = kc_ref[...]
    vbuf[0:pr, :] = vp_ref[...]
    vbuf[pr:pr + tq, :] = vc_ref[...]

    span = SUB_Q + pr
    qi = lax.broadcasted_iota(jnp.int32, (SUB_Q, span), 0)
    kj = lax.broadcasted_iota(jnp.int32, (SUB_Q, span), 1)
    dist = pr + qi - kj
    band = (dist >= 0) & (dist <= window)
    lo = _half_masks()
    halves = (lo, jnp.logical_not(lo))
    groups = ((0,), (1,), (2,)) if kw == 3 * LANES else ((0, 1, 2),)
    rb = BAND_ROWS
    for sb in range(tq // SUB_Q):
        r0 = sb * SUB_Q
        bias = jnp.where(band & (i * tq + r0 - pr + kj >= 0), 0.0, NEG_INF)
        for grp in groups:
            kc0 = grp[0] * LANES if kw == 3 * LANES else 0
            qs = jnp.concatenate(
                [(jnp.where(hm, q_ref[r0:r0 + SUB_Q, p * LANES:(p + 1) * LANES], 0) * QK_SCALE).astype(CDT)
                 for p in grp for hm in halves], axis=0)
            g0 = 2 * grp[0] * SUB_Q
            s_ref[g0:g0 + qs.shape[0], :] = _nt_dot(qs, kbuf[r0:r0 + span, kc0:kc0 + LANES])
        for c0 in range(0, 6 * SUB_Q, rb):
            rows = slice(c0, c0 + rb)
            s = s_ref[rows, :] + bias[c0 % SUB_Q:c0 % SUB_Q + rb, :]
            m = jnp.max(s, axis=-1, keepdims=True)
            e = jnp.exp(s - m)
            e_ref[rows, :] = e.astype(CDT)
            m_ref[rows, :] = jnp.broadcast_to(m, (rb, LANES))
            l_ref[rows, :] = jnp.broadcast_to(jnp.sum(e, axis=-1, keepdims=True), (rb, LANES))
        for grp in groups:
            kc0 = grp[0] * LANES if kw == 3 * LANES else 0
            g0 = 2 * grp[0] * SUB_Q
            g1 = g0 + 2 * len(grp) * SUB_Q
            l = l_ref[g0:g1, :]
            o = _dot(e_ref[g0:g1, :], vbuf[r0:r0 + span, kc0:kc0 + LANES]) / l
            lse = m_ref[g0:g1, :] + jnp.log(l)
            for n_p, p in enumerate(grp):
                a = 2 * n_p * SUB_Q
                o_pair = jnp.where(lo, o[a:a + SUB_Q], o[a + SUB_Q:a + 2 * SUB_Q])
                lse_pair = jnp.where(lo, lse[a:a + SUB_Q], lse[a + SUB_Q:a + 2 * SUB_Q])
                if has_sink:
                    o_pair = o_pair * jax.nn.sigmoid(lse_pair - sink_ref[:, p * LANES:(p + 1) * LANES])
                o_ref[r0:r0 + SUB_Q, p * LANES:(p + 1) * LANES] = o_pair.astype(o_ref.dtype)
                if want_lse:
                    lse_ref[r0:r0 + SUB_Q, p * LANES:(p + 1) * LANES] = lse_pair


def _banded(qa, ka, va, *, nrep, qcol, kcol, vcol, kw, window, want_lse, sink_row=None, tq=256):
    b, l, _ = qa.shape
    pr = -(-window // SUB_Q) * SUB_Q
    tq = min(max(tq, pr), l)
    assert tq % pr == 0 and l % tq == 0, (tq, pr, l)
    ratio = tq // pr
    qw = 3 * LANES
    in_specs = [
        pl.BlockSpec((None, tq, qw), lambda bi, r, i: (bi, i, qcol(r))),
        pl.BlockSpec((None, pr, kw), lambda bi, r, i: (bi, jnp.maximum(i * ratio - 1, 0), kcol(r))),
        pl.BlockSpec((None, tq, kw), lambda bi, r, i: (bi, i, kcol(r))),
        pl.BlockSpec((None, pr, kw), lambda bi, r, i: (bi, jnp.maximum(i * ratio - 1, 0), vcol(r))),
        pl.BlockSpec((None, tq, kw), lambda bi, r, i: (bi, i, vcol(r))),
    ]
    args = [qa, ka, ka, va, va]
    if sink_row is not None:
        in_specs.append(pl.BlockSpec(sink_row.shape, lambda bi, r, i: (0, 0)))
        args.append(sink_row)
    out_specs = [pl.BlockSpec((None, tq, qw), lambda bi, r, i: (bi, i, r))]
    out_shape = [jax.ShapeDtypeStruct((b, l, nrep * qw), CDT)]
    if want_lse:
        out_specs.append(pl.BlockSpec((None, tq, qw), lambda bi, r, i: (bi, i, r)))
        out_shape.append(jax.ShapeDtypeStruct((b, l, nrep * qw), F32))
    res = pl.pallas_call(
        functools.partial(_banded_kernel, window=window, pr=pr, tq=tq, kw=kw, want_lse=want_lse,
                          has_sink=sink_row is not None),
        grid=(b, nrep, l // tq),
        in_specs=in_specs,
        out_specs=out_specs,
        out_shape=out_shape,
        scratch_shapes=[pltpu.VMEM((pr + tq, kw), ka.dtype), pltpu.VMEM((pr + tq, kw), va.dtype),
                        pltpu.VMEM((6 * SUB_Q, SUB_Q + pr), F32), pltpu.VMEM((6 * SUB_Q, SUB_Q + pr), CDT),
                        pltpu.VMEM((6 * SUB_Q, LANES), F32), pltpu.VMEM((6 * SUB_Q, LANES), F32)],
        compiler_params=_cparams(("parallel", "parallel", "parallel"), 32),
        name=f"banded_w{window}_k{kw}_r{nrep}",
    )(*args)
    return res


def _gelu_tanh(x):
    return 0.5 * x * (1.0 + jnp.tanh(0.7978845608028654 * (x + 0.044715 * (x * x * x))))


def _compress_kernel(x_ref, pa_ref, pb_ref, w1a_ref, w1b_ref, b1_ref, w2_ref, b2_ref, o_ref):
    x = x_ref[...].astype(F32)
    n = x.shape[0]
    a = _dot((x + pa_ref[...]).astype(CDT), w1a_ref[...])
    bm = _dot((x + pb_ref[...]).astype(CDT), w1b_ref[...])
    h = a + pltpu.roll(bm, n - 1, 0) + b1_ref[...]
    o_ref[...] = (_dot(_gelu_tanh(h).astype(CDT), w2_ref[...]) + b2_ref[...]).astype(o_ref.dtype)


def _compress(t, pos, w1, b1, w2, b2):
    b, s, _ = t.shape
    nch = s // CMP_STRIDE
    xw = CMP_STRIDE * LANES
    x = t.reshape(b, nch, xw)
    eye = jnp.eye(NSA_KV_HEADS, dtype=F32)
    w1r = w1.reshape(CMP_BLOCK, HEAD_DIM, CMP_HIDDEN)

    def expand_w1(part):
        return jnp.einsum('tdj,kl->tkdlj', part, eye).reshape(xw, NSA_KV_HEADS * CMP_HIDDEN).astype(CDT)

    def expand_pos(part):
        return jnp.broadcast_to(part[:, None, :], (CMP_STRIDE, NSA_KV_HEADS, HEAD_DIM)).reshape(1, xw)

    w1a, w1b = expand_w1(w1r[:CMP_STRIDE]), expand_w1(w1r[CMP_STRIDE:])
    pa, pb = expand_pos(pos[:CMP_STRIDE]), expand_pos(pos[CMP_STRIDE:])
    b1e = jnp.tile(b1, NSA_KV_HEADS).reshape(1, -1)
    w2e = jnp.einsum('jd,kl->kjld', w2, eye).reshape(NSA_KV_HEADS * CMP_HIDDEN, LANES).astype(CDT)
    b2e = jnp.tile(b2, NSA_KV_HEADS).reshape(1, -1)
    consts = [pa, pb, w1a, w1b, b1e, w2e, b2e]
    return pl.pallas_call(
        _compress_kernel,
        grid=(b,),
        in_specs=[pl.BlockSpec((None, nch, xw), lambda bi: (bi, 0, 0))]
        + [pl.BlockSpec(c.shape, lambda bi: (0, 0)) for c in consts],
        out_specs=pl.BlockSpec((None, nch, LANES), lambda bi: (bi, 0, 0)),
        out_shape=jax.ShapeDtypeStruct((b, nch, LANES), CDT),
        compiler_params=_cparams(("parallel",), 48),
        name="nsa_compress",
    )(x, *consts)


CMP_ROWS = 16


def _cmp_select_kernel(q_ref, kc_ref, vc_ref, ov_ref, o_ref, sel_ref, s_ref, p_ref, hi_ref, lo_ref,
                       *, tq, n_sel, tile0):
    i = pl.program_id(1) + tile0
    ncp = kc_ref.shape[0]
    ns = ov_ref.shape[1]
    lo = _half_masks()
    t_col = i * tq + lax.broadcasted_iota(jnp.int32, (tq, 1), 0)
    blk = lax.broadcasted_iota(jnp.int32, (tq, ns), 1)
    blk_t = lax.broadcasted_iota(jnp.int32, (ns, tq), 0)
    cur = t_col // SEL_BLOCK
    causal = blk <= cur
    forced = (blk == 0) | (blk == cur) | (blk == cur - 1)
    rb = CMP_ROWS
    c_end = lax.broadcasted_iota(jnp.int32, (rb, ncp), 1) * CMP_STRIDE + (CMP_BLOCK - 1)
    o_kv, work_t = [], []
    for kv, hm in enumerate((lo, jnp.logical_not(lo))):
        qs = jnp.concatenate(
            [(jnp.where(hm, q_ref[:, p * LANES:(p + 1) * LANES], 0) * QK_SCALE).astype(CDT) for p in range(3)], axis=0)
        s_ref[...] = _nt_dot(qs, kc_ref[...])
        for r0 in range(0, tq, rb):
            t_rows = i * tq + r0 + lax.broadcasted_iota(jnp.int32, (rb, 1), 0)
            vis_bias = jnp.where(c_end <= t_rows, 0.0, NEG_INF)
            has_visible = t_rows >= CMP_BLOCK - 1
            psum = jnp.zeros((rb, ncp), F32)
            for h in range(3):
                rows = slice(h * tq + r0, h * tq + r0 + rb)
                s = s_ref[rows, :] + vis_bias
                e = jnp.exp(s - jnp.max(s, axis=-1, keepdims=True))
                inv = jnp.where(has_visible, 1.0 / jnp.maximum(jnp.sum(e, axis=-1, keepdims=True), 1e-30), 0.0)
                pn = e * inv
                p_ref[rows, :] = pn.astype(CDT)
                psum = psum + pn
            p_hi = psum.astype(CDT)
            hi_ref[r0:r0 + rb, :] = p_hi
            lo_ref[r0:r0 + rb, :] = (psum - p_hi.astype(F32)).astype(CDT)
        o = _dot(p_ref[...], vc_ref[...])
        o_kv.append([o[p * tq:(p + 1) * tq] for p in range(3)])
        imp = _dot(hi_ref[...], ov_ref[...]) + _dot(lo_ref[...], ov_ref[...])
        work_t.append(jnp.where(causal & jnp.logical_not(forced), imp, -BIG).T)

    def pick(_, work):
        m = jnp.max(work, axis=0, keepdims=True)
        idx = jnp.min(jnp.where(work == m, blk_t, ns), axis=0, keepdims=True)
        return jnp.where(blk_t == idx, -2.0 * BIG, work)

    for kv, start in enumerate(work_t):
        done = lax.fori_loop(0, n_sel - 3, pick, start)
        taken = jnp.where((done < -BIG) & (start > -BIG), 1.0, 0.0).T
        selb = jnp.where(forced | (taken > 0.5), 0.0, NEG_INF)
        sel_ref[:, kv * ns:(kv + 1) * ns] = selb.astype(sel_ref.dtype)
    for p in range(3):
        o_ref[:, p * LANES:(p + 1) * LANES] = jnp.where(lo, o_kv[0][p], o_kv[1][p]).astype(o_ref.dtype)


CMP_CAUSAL_SPLITS = 4


def _cmp_select(zr, qcol, kc, vc, tq=128):
    b, s, _ = zr.shape
    ncp = kc.shape[1]
    ns = s // SEL_BLOCK
    n_sel = min(N_SEL, ns)
    assert n_sel >= 3, "selection needs room for the three forced blocks"
    tq = min(tq, s)
    c = np.arange(ncp)[:, None] * CMP_STRIDE
    j = np.arange(ns)[None, :] * SEL_BLOCK
    overlap = ((c < j + SEL_BLOCK) & (c + CMP_BLOCK - 1 >= j)).astype(np.float32)
    overlap[ncp - 1:, :] = 0.0
    ov = jnp.asarray(overlap, CDT)
    qw = 3 * LANES
    n_split = CMP_CAUSAL_SPLITS if (s // tq) % CMP_CAUSAL_SPLITS == 0 and ncp % (16 * CMP_CAUSAL_SPLITS) == 0 else 1
    tiles = s // tq // n_split
    outs, sels = [], []
    for part in range(n_split):
        tile0 = part * tiles
        ncp_part = ncp * (part + 1) // n_split
        o_part, sel_part = pl.pallas_call(
            functools.partial(_cmp_select_kernel, tq=tq, n_sel=n_sel, tile0=tile0),
            grid=(b, tiles),
            in_specs=[pl.BlockSpec((None, tq, qw), lambda bi, i, tile0=tile0: (bi, i + tile0, qcol)),
                      pl.BlockSpec((None, ncp_part, LANES), lambda bi, i: (bi, 0, 0)),
                      pl.BlockSpec((None, ncp_part, LANES), lambda bi, i: (bi, 0, 0)),
                      pl.BlockSpec((ncp_part, ns), lambda bi, i: (0, 0))],
            out_specs=[pl.BlockSpec((None, tq, qw), lambda bi, i: (bi, i, 0)),
                       pl.BlockSpec((None, tq, 2 * ns), lambda bi, i: (bi, i, 0))],
            out_shape=[jax.ShapeDtypeStruct((b, tiles * tq, qw), CDT),
                       jax.ShapeDtypeStruct((b, tiles * tq, 2 * ns), CDT)],
            scratch_shapes=[pltpu.VMEM((3 * tq, ncp_part), F32), pltpu.VMEM((3 * tq, ncp_part), CDT),
                            pltpu.VMEM((tq, ncp_part), CDT), pltpu.VMEM((tq, ncp_part), CDT)],
            compiler_params=_cparams(("parallel", "parallel"), 48),
            name=f"nsa_cmp_select_p{part}",
        )(zr, kc, vc, ov)
        outs.append(o_part)
        sels.append(sel_part)
    return jnp.concatenate(outs, axis=1), jnp.concatenate(sels, axis=1)


SLC_ROWS = 64


def _slc_kernel(q_ref, k0_ref, k1_ref, v_ref, selb_ref, o_ref, s_ref, e_ref, m_ref, l_ref, a_ref, acc_ref,
                *, tq, tk):
    i = pl.program_id(1)
    ns = selb_ref.shape[1] // 2
    bpt = tk // SEL_BLOCK
    rb = SLC_ROWS
    lo = _half_masks()
    n_tiles = ((i + 1) * tq + tk - 1) // tk
    p_row = lax.broadcasted_iota(jnp.int32, (ns, LANES), 0)
    p_col = lax.broadcasted_iota(jnp.int32, (ns, LANES), 1)
    groups = []
    for kv, (hm, k_ref) in enumerate(((lo, k0_ref), (jnp.logical_not(lo), k1_ref))):
        q3 = [(jnp.where(hm, q_ref[:, p * LANES:(p + 1) * LANES], 0) * QK_SCALE).astype(CDT) for p in range(3)]
        selb = selb_ref[:, kv * ns:(kv + 1) * ns]
        lane0 = HEAD_DIM if kv == 0 else 0
        groups.append((kv, hm, k_ref, q3, selb, lane0))
    m_ref[...] = jnp.full(m_ref.shape, NEG_INF, F32)
    l_ref[...] = jnp.zeros(l_ref.shape, F32)
    acc_ref[...] = jnp.zeros(acc_ref.shape, F32)

    def scores(j):
        k0 = pl.multiple_of(j * tk, tk)
        for kv, hm, k_ref, q3, selb, lane0 in groups:
            place = ((p_col >= lane0) & (p_col < lane0 + bpt) & (p_row == p_col - lane0 + j * bpt)).astype(CDT)
            sb = _dot(selb, place).astype(CDT)
            qp = jnp.concatenate([jnp.where(hm, q, sb) for q in q3], axis=0)
            s_ref[kv] = _nt_dot(qp, k_ref[pl.ds(k0, tk), :])

    def softmax_pv(j, diagonal):
        k0 = pl.multiple_of(j * tk, tk)
        for kv in range(2):
            for r0 in range(0, 3 * tq, rb):
                rows = slice(r0, r0 + rb)
                s = s_ref[kv, rows, :]
                if diagonal:
                    t_row = i * tq + (r0 % tq) + lax.broadcasted_iota(jnp.int32, (rb, 1), 0)
                    kpos = k0 + lax.broadcasted_iota(jnp.int32, (rb, tk), 1)
                    s = jnp.where(kpos <= t_row, s, NEG_INF)
                m_old = m_ref[kv, rows, :]
                m_new = jnp.maximum(m_old, jnp.max(s, axis=-1, keepdims=True))
                alpha = jnp.exp(m_old - m_new)
                e = jnp.exp(s - jnp.tile(m_new, (1, tk // LANES)))
                l_ref[kv, rows, :] = alpha * l_ref[kv, rows, :] + jnp.sum(e, axis=-1, keepdims=True)
                m_ref[kv, rows, :] = m_new
                a_ref[kv, rows, :] = alpha
                e_ref[kv, rows, :] = e.astype(CDT)
        for kv in range(2):
            acc_ref[kv] = a_ref[kv] * acc_ref[kv] + _dot(e_ref[kv], v_ref[pl.ds(k0, tk), :])

    def tile(j, diagonal):
        scores(j)
        softmax_pv(j, diagonal)

    lax.fori_loop(0, n_tiles - 1, lambda j, c: (tile(j, False), c)[1], 0)
    tile(n_tiles - 1, True)
    o0 = acc_ref[0] / l_ref[0]
    o1 = acc_ref[1] / l_ref[1]
    for p in range(3):
        o_ref[:, p * LANES:(p + 1) * LANES] = jnp.where(
            lo, o0[p * tq:(p + 1) * tq], o1[p * tq:(p + 1) * tq]).astype(o_ref.dtype)


def _slc(zr, qcol, ksc, vcol, selb, tq=256, tk=1024):
    b, s, _ = zr.shape
    tq = min(tq, s)
    tk = min(tk, s)
    assert tq % SLC_ROWS == 0 and tk // SEL_BLOCK <= HEAD_DIM
    qw = 3 * LANES
    ns2 = selb.shape[2]
    pat = jax.nn.one_hot((jnp.arange(s) // SEL_BLOCK) % (tk // SEL_BLOCK), HEAD_DIM, dtype=ksc.dtype)
    pat = jnp.broadcast_to(pat[None], (b, s, HEAD_DIM))
    k0 = jnp.concatenate([ksc[..., :HEAD_DIM], pat], axis=-1)
    k1 = jnp.concatenate([pat, ksc[..., HEAD_DIM:]], axis=-1)
    full = lambda bi, i: (bi, 0, 0)
    return pl.pallas_call(
        functools.partial(_slc_kernel, tq=tq, tk=tk),
        grid=(b, s // tq),
        in_specs=[pl.BlockSpec((None, tq, qw), lambda bi, i: (bi, i, qcol)),
                  pl.BlockSpec((None, s, LANES), full),
                  pl.BlockSpec((None, s, LANES), full),
                  pl.BlockSpec((None, s, LANES), lambda bi, i: (bi, 0, vcol)),
                  pl.BlockSpec((None, tq, ns2), lambda bi, i: (bi, i, 0))],
        out_specs=pl.BlockSpec((None, tq, qw), lambda bi, i: (bi, i, 0)),
        out_shape=jax.ShapeDtypeStruct((b, s, qw), CDT),
        scratch_shapes=[pltpu.VMEM((2, 3 * tq, tk), F32), pltpu.VMEM((2, 3 * tq, tk), CDT),
                        pltpu.VMEM((2, 3 * tq, LANES), F32), pltpu.VMEM((2, 3 * tq, LANES), F32),
                        pltpu.VMEM((2, 3 * tq, LANES), F32), pltpu.VMEM((2, 3 * tq, LANES), F32)],
        compiler_params=_cparams(("parallel", "arbitrary"), 56),
        name="nsa_slc",
    )(zr, k0, k1, zr, selb)


def _merge_kernel(x_ref, oa0, oa1, oa2, la0, la1, la2, ob, ocmp, oslc, owin, gates, od,
                  wg_ref, bg_ref, wb_ref, wo_ref, g_ref, b_ref, o_ref, *scratch, alpha):
    x = x_ref[...]
    xb = x.astype(CDT)
    bw = BRANCH_WIDTH

    def token_rows(src_ref, dst_ref):
        dil = src_ref.shape[1] // bw
        if dil == 1:
            return src_ref[...].astype(F32)
        n_cb = bw // LANES
        for r in range(dil):
            for cb in range(n_cb):
                c0 = r * bw + cb * LANES
                dst_ref[cb, pl.ds(r, src_ref.shape[0], stride=dil), :] = src_ref[:, c0:c0 + LANES].astype(F32)
        return jnp.concatenate([dst_ref[cb] for cb in range(n_cb)], axis=1)

    o0, o1, o2 = (token_rows(s, d) for s, d in zip((oa0, oa1, oa2), scratch[0:3]))
    l0, l1, l2 = (token_rows(s, d) for s, d in zip((la0, la1, la2), scratch[3:6]))
    mx = jnp.maximum(jnp.maximum(l0, l1), l2)
    w0, w1, w2 = jnp.exp(l0 - mx), jnp.exp(l1 - mx), jnp.exp(l2 - mx)
    o_a = (w0 * o0 + w1 * o1 + w2 * o2) / (w0 + w1 + w2)
    o_c = (gates[:, 0:bw] * ocmp[...].astype(F32) + gates[:, bw:2 * bw] * oslc[...].astype(F32)
           + gates[:, 2 * bw:3 * bw] * owin[...].astype(F32))
    branches = (o_a.astype(CDT), ob[...], o_c.astype(CDT), od[...])
    d = x.shape[1]
    merged = jnp.zeros(x.shape, F32)
    for m in range(N_BRANCH):
        gate = jax.nn.sigmoid(_dot(xb, wg_ref[:, m * d:(m + 1) * d]) + bg_ref[:, m * d:(m + 1) * d])
        merged = merged + gate * _dot(branches[m], wb_ref[m])
    r = alpha * x + _dot(merged.astype(CDT), wo_ref[...])
    o_ref[...] = _layer_norm(r, g_ref[...], b_ref[...])


def _merge(x2d, branch_inputs, wg, bg, wb, wo, g, b, alpha, tm=256):
    t, d = x2d.shape
    tm = min(tm, t)
    row = lambda i: (i, 0)
    const2 = lambda i: (0, 0)
    in_specs = [pl.BlockSpec((tm, d), row)]
    in_specs += [pl.BlockSpec((tm * a.shape[0] // t, a.shape[1]), row) for a in branch_inputs]
    in_specs += [pl.BlockSpec(wg.shape, const2), pl.BlockSpec(bg.shape, const2),
                 pl.BlockSpec(wb.shape, lambda i: (0, 0, 0)), pl.BlockSpec(wo.shape, const2),
                 pl.BlockSpec(g.shape, const2), pl.BlockSpec(b.shape, const2)]
    return pl.pallas_call(
        functools.partial(_merge_kernel, alpha=alpha),
        grid=(t // tm,),
        in_specs=in_specs,
        out_specs=pl.BlockSpec((tm, d), row),
        out_shape=jax.ShapeDtypeStruct((t, d), F32),
        scratch_shapes=[pltpu.VMEM((BRANCH_WIDTH // LANES, tm, LANES), F32) for _ in range(2 * N_DIL)],
        compiler_params=_cparams(("parallel",), 56),
        name="merge_ln",
    )(x2d, *branch_inputs, wg, bg, wb, wo, g, b)


def _ple_ln(x, xb, f, p, plw_ref, pgw_ref, pgb_ref, g_ref, b_ref, alpha):
    ple = jax.nn.sigmoid(_dot(xb, pgw_ref[...]) + pgb_ref[...]) * _dot(p.astype(CDT), plw_ref[...])
    return _layer_norm(alpha * x + f + ple, g_ref[...], b_ref[...])


FFN_CHUNK = 512


def _ffn_kernel(x_ref, p_ref, wg_ref, wu_ref, wd_ref, plw_ref, pgw_ref, pgb_ref, g_ref, b_ref, o_ref, h_ref, *, alpha):
    x = x_ref[...]
    xb = x.astype(CDT)
    dff = wg_ref.shape[1]
    for c0 in range(0, dff, FFN_CHUNK):
        cols = slice(c0, min(c0 + FFN_CHUNK, dff))
        h_ref[:, cols] = (jax.nn.silu(_dot(xb, wg_ref[:, cols])) * _dot(xb, wu_ref[:, cols])).astype(CDT)
    f = _dot(h_ref[...], wd_ref[...])
    o_ref[...] = _ple_ln(x, xb, f, p_ref[...], plw_ref, pgw_ref, pgb_ref, g_ref, b_ref, alpha)


def _ffn(x2d, p2d, wg, wu, wd, plw, pgw, pgb, g, b, alpha, tm=512):
    t, d = x2d.shape
    tm = min(tm, t)
    dff = wg.shape[1]
    wg, wu, wd = wg.astype(CDT), wu.astype(CDT), wd.astype(CDT)
    row = lambda i: (i, 0)
    const = lambda shape: pl.BlockSpec(shape, lambda i: (0, 0), pipeline_mode=pl.Buffered(1))
    return pl.pallas_call(
        functools.partial(_ffn_kernel, alpha=alpha),
        grid=(t // tm,),
        in_specs=[pl.BlockSpec((tm, d), row), pl.BlockSpec((tm, p2d.shape[1]), row),
                  const(wg.shape), const(wu.shape), const(wd.shape),
                  const(plw.shape), const(pgw.shape), const(pgb.shape), const(g.shape), const(b.shape)],
        out_specs=pl.BlockSpec((tm, d), row),
        out_shape=jax.ShapeDtypeStruct((t, d), F32),
        scratch_shapes=[pltpu.VMEM((tm, dff), CDT)],
        compiler_params=_cparams(("parallel",), 48),
        name="ffn_ple_ln",
    )(x2d, p2d, wg, wu, wd, plw, pgw, pgb, g, b)


def _router_kernel(x_ref, wh_ref, wl_ref, b_ref, comb_ref, rank_ref, rank_t_ref, cnt_ref):
    x = x_ref[...]
    xh = x.astype(CDT)
    xl = (x - xh.astype(F32)).astype(CDT)
    logits = _dot(xh, wh_ref[...]) + _dot(xh, wl_ref[...]) + _dot(xl, wh_ref[...]) + b_ref[...]
    lane = lax.broadcasted_iota(jnp.int32, logits.shape, 1)
    v1 = jnp.max(logits, axis=-1, keepdims=True)
    i1 = jnp.min(jnp.where(logits == v1, lane, LANES), axis=-1, keepdims=True)
    rest = jnp.where(lane == i1, -jnp.inf, logits)
    v2 = jnp.max(rest, axis=-1, keepdims=True)
    i2 = jnp.min(jnp.where(rest == v2, lane, LANES), axis=-1, keepdims=True)
    e2 = jnp.exp(v2 - v1)
    comb_ref[...] = jnp.where(lane == i1, 1.0 / (1.0 + e2), 0.0) + jnp.where(lane == i2, e2 / (1.0 + e2), 0.0)
    routed = (lane == i1) | (lane == i2)
    mask = routed.astype(CDT)
    tm = x.shape[0]
    before = (lax.broadcasted_iota(jnp.int32, (tm, tm), 1) < lax.broadcasted_iota(jnp.int32, (tm, tm), 0)).astype(CDT)
    rank = jnp.where(routed, _dot(before, mask), -1.0)
    rank_ref[...] = rank
    rank_t_ref[...] = rank.T[0:rank_t_ref.shape[0], :]
    cnt_ref[...] = jnp.sum(routed.astype(F32), axis=0, keepdims=True).astype(jnp.int32)


def _router(x2d, w_router, b_router, tm):
    t, d = x2d.shape
    ne = w_router.shape[1]
    wp = jnp.zeros((d, LANES), F32).at[:, :ne].set(w_router)
    wh = wp.astype(CDT)
    wl = (wp - wh.astype(F32)).astype(CDT)
    bp = jnp.full((1, LANES), -BIG, F32).at[0, :ne].set(b_router)
    nt = t // tm
    row = lambda i: (i, 0)
    return pl.pallas_call(
        _router_kernel,
        grid=(nt,),
        in_specs=[pl.BlockSpec((tm, d), row), pl.BlockSpec(wh.shape, lambda i: (0, 0)),
                  pl.BlockSpec(wl.shape, lambda i: (0, 0)), pl.BlockSpec(bp.shape, lambda i: (0, 0))],
        out_specs=[pl.BlockSpec((tm, LANES), row), pl.BlockSpec((tm, LANES), row),
                   pl.BlockSpec((None, 8, tm), lambda i: (i, 0, 0)),
                   pl.BlockSpec((None, 1, LANES), lambda i: (i, 0, 0))],
        out_shape=[jax.ShapeDtypeStruct((t, LANES), F32), jax.ShapeDtypeStruct((t, LANES), F32),
                   jax.ShapeDtypeStruct((nt, 8, tm), F32), jax.ShapeDtypeStruct((nt, 1, LANES), jnp.int32)],
        compiler_params=_cparams(("parallel",), 40),
        name="moe_router",
    )(x2d, wh, wl, bp)


def _moe_kernel(cnt_ref, x_ref, comb_ref, rank_ref, rank_t_ref, p_ref, wg_ref, wu_ref, wd_ref, plw_ref, pgw_ref,
                pgb_ref, g_ref, b_ref, o_ref, xe_ref, ye_ref, *, alpha, rs, seg):
    i = pl.program_id(0)
    e = pl.program_id(1)
    c = pl.program_id(2)
    n_seg = x_ref.shape[0] // seg
    last_chunk = c == pl.num_programs(2) - 1

    def rows(sc):
        return pl.ds(pl.multiple_of(sc * rs, 8), rs)

    @pl.when((e == 0) & (c == 0))
    def _():
        o_ref[...] = jnp.zeros_like(o_ref)

    for sg in range(n_seg):
        tok = slice(sg * seg, (sg + 1) * seg)
        n_groups = (cnt_ref[(i * n_seg + sg) * LANES + e] + rs - 1) // rs

        @pl.when(c == 0)
        def _():
            xb = x_ref[tok, :].astype(CDT)
            rank_row = rank_t_ref[sg, pl.ds(e, 1), :]
            row_id = lax.broadcasted_iota(jnp.int32, (rs, seg), 0).astype(F32)

            def gather(sc, _):
                onehot = (rank_row - (sc * rs).astype(F32) == row_id).astype(CDT)
                xe_ref[sg, rows(sc), :] = _dot(onehot, xb).astype(CDT)
                ye_ref[sg, rows(sc), :] = jnp.zeros((rs, ye_ref.shape[2]), F32)
                return 0

            lax.fori_loop(0, n_groups, gather, 0)

        def expert(sc, _):
            xs = xe_ref[sg, rows(sc), :]
            h = jax.nn.silu(_dot(xs, wg_ref[0])) * _dot(xs, wu_ref[0])
            ye_ref[sg, rows(sc), :] += _dot(h.astype(CDT), wd_ref[0])
            return 0

        lax.fori_loop(0, n_groups, expert, 0)

        @pl.when(last_chunk)
        def _():
            lane = lax.broadcasted_iota(jnp.int32, (seg, LANES), 1)
            mine = lane == e
            cw = jnp.sum(jnp.where(mine, comb_ref[tok, :], 0.0), axis=-1, keepdims=True)
            rank_col = jnp.sum(jnp.where(mine, rank_ref[tok, :], 0.0), axis=-1, keepdims=True)
            col_id = lax.broadcasted_iota(jnp.int32, (seg, rs), 1).astype(F32)

            def scatter(sc, _):
                onehot = (rank_col - (sc * rs).astype(F32) == col_id).astype(CDT)
                o_ref[tok, :] += cw * _dot(onehot, ye_ref[sg, rows(sc), :].astype(CDT))
                return 0

            lax.fori_loop(0, n_groups, scatter, 0)

    @pl.when((e == pl.num_programs(1) - 1) & last_chunk)
    def _():
        for sg in range(n_seg):
            tok = slice(sg * seg, (sg + 1) * seg)
            x = x_ref[tok, :]
            o_ref[tok, :] = _ple_ln(x, x.astype(CDT), o_ref[tok, :], p_ref[tok, :], plw_ref, pgw_ref, pgb_ref,
                                    g_ref, b_ref, alpha)


MOE_CHUNK = 512
MOE_SEGMENT = 1024
MOE_SEGMENTS_PER_TILE = 2
MOE_ROW_GROUP = 288


def _single_buffered(shape, index_map):
    return pl.BlockSpec(shape, index_map, pipeline_mode=pl.Buffered(1))


def _moe(x2d, routing, p2d, wg, wu, wd, plw, pgw, pgb, g, b, alpha, seg):
    comb, rank, rank_t, cnt = routing
    t, d = x2d.shape
    ne, _, dff = wg.shape
    ck = min(MOE_CHUNK, dff)
    rs = min(MOE_ROW_GROUP, seg)
    n_seg = min(MOE_SEGMENTS_PER_TILE, t // seg)
    tm = n_seg * seg
    max_rows = -(-seg // rs) * rs
    nck = dff // ck
    wg = wg.reshape(ne, d, nck, ck).transpose(0, 2, 1, 3).astype(CDT)
    wu = wu.reshape(ne, d, nck, ck).transpose(0, 2, 1, 3).astype(CDT)
    wd = wd.reshape(ne, nck, ck, d).astype(CDT)
    row = lambda i, e, c, cnt: (i, 0)
    c2 = lambda i, e, c, cnt: (0, 0)
    grid_spec = pltpu.PrefetchScalarGridSpec(
        num_scalar_prefetch=1,
        grid=(t // tm, ne, dff // ck),
        in_specs=[_single_buffered((tm, d), row), _single_buffered((tm, LANES), row),
                  _single_buffered((tm, LANES), row),
                  _single_buffered((n_seg, 8, seg), lambda i, e, c, cnt: (i, 0, 0)),
                  _single_buffered((tm, p2d.shape[1]), row),
                  pl.BlockSpec((None, 1, d, ck), lambda i, e, c, cnt: (e, c, 0, 0)),
                  pl.BlockSpec((None, 1, d, ck), lambda i, e, c, cnt: (e, c, 0, 0)),
                  pl.BlockSpec((None, 1, ck, d), lambda i, e, c, cnt: (e, c, 0, 0)),
                  _single_buffered(plw.shape, c2), _single_buffered(pgw.shape, c2), _single_buffered(pgb.shape, c2),
                  _single_buffered(g.shape, c2), _single_buffered(b.shape, c2)],
        out_specs=pl.BlockSpec((tm, d), row),
        scratch_shapes=[pltpu.VMEM((n_seg, max_rows, d), CDT), pltpu.VMEM((n_seg, max_rows, d), F32)],
    )
    return pl.pallas_call(
        functools.partial(_moe_kernel, alpha=alpha, rs=rs, seg=seg),
        grid_spec=grid_spec,
        out_shape=jax.ShapeDtypeStruct((t, d), F32),
        compiler_params=_cparams(("parallel", "arbitrary", "arbitrary"), 60),
        name="moe_ple_ln",
    )(cnt.reshape(-1), x2d, comb, rank, rank_t, p2d, wg, wu, wd, plw, pgw, pgb, g, b)


def _prep_in_weights(w_in):
    o = COL_OFF
    bw = BRANCH_WIDTH
    cols = lambda n: w_in[:, o[n]:o[n + 1]]
    qa, ka, va = cols(0), cols(1), cols(2)
    w_dil = jnp.concatenate(
        [t[:, g * bw:(g + 1) * bw] for g in range(N_DIL) for t in (qa, ka, va)], axis=1).astype(CDT)
    w_conv = jnp.concatenate([cols(3), cols(4), cols(5)], axis=1).astype(CDT)
    gn = cols(13)
    w_gate = jnp.concatenate([gn[:, br * NSA_Q_HEADS + GQA_COL_HEAD] for br in range(3)], axis=1).astype(CDT)
    w_rest = jnp.concatenate([cols(6)[:, GQA_COL_PERM], cols(14)[:, GQA_COL_PERM], cols(10), cols(11),
                              cols(12), cols(15), cols(16), cols(9), cols(7), cols(8)], axis=1).astype(CDT)
    return w_dil, w_conv, w_gate, w_rest


ZR_Q_NSA, ZR_Q_SWA = 0, 1
ZR_VSC, ZR_KWC, ZR_VWC, ZR_KD, ZR_VD = 6, 7, 8, 9, 10
ZR_WIDTH = 2 * BRANCH_WIDTH + 5 * LANES


def _token_mixers(x, w_in, conv_w, cmp_pos, cmp_w1, cmp_b1, cmp_w2, cmp_b2, sinks):
    b, s, d = x.shape
    x2d = x.reshape(b * s, d)
    w_dil, w_conv, w_gate, w_rest = _prep_in_weights(w_in)
    gw = 3 * BRANCH_WIDTH

    z_dil = _linear(x2d, w_dil, [(g * gw, (g + 1) * gw) for g in range(N_DIL)], gw, "in_proj_dil",
                    dils=[dil for _, dil in DIL_PATTERNS])
    zr, ksc, kcc, vcc = _linear(x2d, w_rest, [(0, ZR_WIDTH)] + [(ZR_WIDTH + n * LANES, ZR_WIDTH + (n + 1) * LANES)
                                                           for n in range(3)], 256, "in_proj_rest")
    zr = zr.reshape(b, s, ZR_WIDTH)
    o_b, gates = _conv_gate(x, w_conv, conv_w, w_gate)

    dil_o, dil_lse = [], []
    for g, (window, dil) in enumerate(DIL_PATTERNS):
        view = z_dil[g].reshape(b, s // dil, dil * gw)
        og, lg = _banded(view, view, view, nrep=dil,
                         qcol=lambda r: 3 * r, kcol=lambda r: 3 * r + 1, vcol=lambda r: 3 * r + 2,
                         kw=3 * LANES, window=window // dil, want_lse=True)
        dil_o.append(og.reshape(b * s // dil, dil * BRANCH_WIDTH))
        dil_lse.append(lg.reshape(b * s // dil, dil * BRANCH_WIDTH))

    kc = _compress(kcc.reshape(b, s, LANES), cmp_pos[0], cmp_w1[0], cmp_b1[0], cmp_w2[0], cmp_b2[0])
    vc = _compress(vcc.reshape(b, s, LANES), cmp_pos[1], cmp_w1[1], cmp_b1[1], cmp_w2[1], cmp_b2[1])
    o_cmp, selb = _cmp_select(zr, ZR_Q_NSA, kc, vc)
    o_slc = _slc(zr, ZR_Q_NSA, ksc.reshape(b, s, LANES), ZR_VSC, selb)
    (o_win,) = _banded(zr, zr, zr, nrep=1, qcol=lambda r: ZR_Q_NSA, kcol=lambda r: ZR_KWC, vcol=lambda r: ZR_VWC,
                       kw=LANES, window=NSA_WINDOW - 1, want_lse=False, tq=512)

    sink_row = sinks.astype(F32)[GQA_COL_HEAD].reshape(1, BRANCH_WIDTH)
    (o_d,) = _banded(zr, zr, zr, nrep=1, qcol=lambda r: ZR_Q_SWA, kcol=lambda r: ZR_KD, vcol=lambda r: ZR_VD,
                     kw=LANES, window=SWA_WINDOW - 1, want_lse=False, sink_row=sink_row)

    t = b * s
    flat = lambda a: a.reshape(t, a.shape[-1])
    return [dil_o[0], dil_o[1], dil_o[2], dil_lse[0], dil_lse[1], dil_lse[2], flat(o_b), flat(o_cmp), flat(o_slc),
            flat(o_win), flat(gates), flat(o_d)]


def kernel(x, p, w_in, conv_w, cmp_pos, cmp_w1, cmp_b1, cmp_w2, cmp_b2, sinks, w_branch, w_merge_gate, b_merge_gate, w_out, ln_mix_g, ln_mix_b, ffn_w_gate, ffn_w_up, ffn_w_down, w_router, b_router, moe_w_gate, moe_w_up, moe_w_down, ple_w, ple_gate_w, ple_gate_b, ln_ffn_g, ln_ffn_b):
    depth, b, s, _ = p.shape
    d = x.shape[-1]
    t = b * s
    alpha = (2 * depth) ** 0.25
    row = lambda v: v.reshape(1, -1).astype(F32)
    for i in range(depth):
        branch_inputs = _token_mixers(x, w_in[i], conv_w[i], cmp_pos[i], cmp_w1[i], cmp_b1[i], cmp_w2[i],
                                      cmp_b2[i], sinks[i])
        wg = jnp.concatenate([w_merge_gate[i, m] for m in range(N_BRANCH)], axis=1).astype(CDT)
        bg = b_merge_gate[i].reshape(1, N_BRANCH * d).astype(F32)
        wb = jnp.stack([w_branch[i, 0], w_branch[i, 1], w_branch[i, 2][GQA_COL_PERM],
                        w_branch[i, 3][GQA_COL_PERM]]).astype(CDT)
        x1 = _merge(x.reshape(t, d), branch_inputs, wg, bg, wb, w_out[i].astype(CDT),
                    row(ln_mix_g[i]), row(ln_mix_b[i]), alpha)
        p2d = p[i].reshape(t, -1)
        ple_args = (ple_w[i].astype(CDT), ple_gate_w[i].astype(CDT), row(ple_gate_b[i]),
                    row(ln_ffn_g[i]), row(ln_ffn_b[i]))
        j = i // 2
        if i % 2 == 0:
            x2 = _ffn(x1, p2d, ffn_w_gate[j], ffn_w_up[j], ffn_w_down[j], *ple_args, alpha)
        else:
            seg = min(MOE_SEGMENT, t)
            routing = _router(x1, w_router[j], b_router[j], seg)
            x2 = _moe(x1, routing, p2d, moe_w_gate[j], moe_w_up[j], moe_w_down[j], *ple_args, alpha, seg)
        x = x2.reshape(b, s, d)
    return x
```

```python
import functools

import numpy as np
import jax
import jax.numpy as jnp
from jax import lax
from jax.experimental import pallas as pl
from jax.experimental.pallas import tpu as pltpu

D_MODEL = 1024
HEAD_DIM = 64
DIL_PATTERNS = ((128, 1), (512, 4), (2048, 16))
N_DIL = 3
DIL_HEADS = 6
CONV_WIDTH = 384
CONV_K = 3
NSA_Q_HEADS = 6
NSA_KV_HEADS = 2
CMP_BLOCK = 32
CMP_STRIDE = 16
CMP_HIDDEN = 128
SEL_BLOCK = 64
N_SEL = 16
NSA_WINDOW = 512
SWA_Q_HEADS = 6
SWA_WINDOW = 128
BRANCH_WIDTH = 384
N_BRANCH = 4
N_EXPERTS = 8
LN_EPS = 1e-5
NEG_INF = -1e30
DIL_WIDTH = N_DIL * DIL_HEADS * HEAD_DIM
COLUMN_SIZES = (DIL_WIDTH, DIL_WIDTH, DIL_WIDTH, CONV_WIDTH, CONV_WIDTH, CONV_WIDTH,
                NSA_Q_HEADS * HEAD_DIM, 128, 128, 128, 128, 128, 128, 3 * NSA_Q_HEADS,
                SWA_Q_HEADS * HEAD_DIM, 128, 128)
COL_OFF = np.concatenate([[0], np.cumsum(COLUMN_SIZES)]).tolist()

LANES = 128
V7X_VMEM_BYTES = 64 * 1024 * 1024
MIB = 1024 * 1024

CDT = jnp.bfloat16
F32 = jnp.float32
QK_SCALE = HEAD_DIM ** -0.5
SUB_Q = 128
BAND_ROWS = 64
BIG = 1e30

_GQA_HEAD_ORDER = (0, 3, 1, 4, 2, 5)
GQA_COL_PERM = np.concatenate([np.arange(h * HEAD_DIM, (h + 1) * HEAD_DIM) for h in _GQA_HEAD_ORDER])
GQA_COL_HEAD = GQA_COL_PERM // HEAD_DIM


def _cparams(sem, vmem_mib):
    return pltpu.CompilerParams(dimension_semantics=sem, vmem_limit_bytes=int(vmem_mib * MIB))


def _nt_dot(a, b):
    return lax.dot_general(a, b, (((1,), (1,)), ((), ())), preferred_element_type=F32)


def _dot(a, b):
    return jnp.dot(a, b, preferred_element_type=F32)


def _layer_norm(r, g, b):
    mu = jnp.mean(r, axis=-1, keepdims=True)
    d = r - mu
    var = jnp.mean(d * d, axis=-1, keepdims=True)
    return d * lax.rsqrt(var + LN_EPS) * g + b


def _half_masks():
    lane = lax.broadcasted_iota(jnp.int32, (1, LANES), 1)
    return lane < HEAD_DIM


def _linear_kernel(x_ref, w_ref, *refs, splits, n_chunk, dils):
    o_refs, z_ref = refs[:len(splits)], refs[len(splits)]
    xb = x_ref[...].astype(CDT)
    tm = xb.shape[0]
    for o_ref, (c0, c1), dil in zip(o_refs, splits, dils):
        width = c1 - c0
        if dil == 1:
            for a in range(c0, c1, n_chunk):
                b = min(a + n_chunk, c1)
                o_ref[:, a - c0:b - c0] = _dot(xb, w_ref[:, a:b]).astype(o_ref.dtype)
        else:
            z = _dot(xb, w_ref[:, c0:c1])
            for cb in range(width // LANES):
                z_ref[cb] = z[:, cb * LANES:(cb + 1) * LANES]
            for r in range(dil):
                for cb in range(width // LANES):
                    o_ref[:, r * width + cb * LANES:r * width + (cb + 1) * LANES] = (
                        z_ref[cb, pl.ds(r, tm // dil, stride=dil), :].astype(o_ref.dtype))


def _linear(x2d, w, splits, n_chunk, name, dils=None, tm=512):
    t, k = x2d.shape
    tm = min(tm, t)
    n = w.shape[1]
    dils = tuple(dils) if dils is not None else (1,) * len(splits)
    widths = [c1 - c0 for c0, c1 in splits]
    assert all(tm % (16 * dl) == 0 for dl in dils)
    return pl.pallas_call(
        functools.partial(_linear_kernel, splits=tuple(splits), n_chunk=n_chunk, dils=dils),
        grid=(t // tm,),
        in_specs=[pl.BlockSpec((tm, k), lambda i: (i, 0)),
                  pl.BlockSpec((k, n), lambda i: (0, 0))],
        out_specs=[pl.BlockSpec((tm // dl, dl * wd), lambda i: (i, 0)) for wd, dl in zip(widths, dils)],
        out_shape=[jax.ShapeDtypeStruct((t // dl, dl * wd), CDT) for wd, dl in zip(widths, dils)],
        scratch_shapes=[pltpu.VMEM((max(widths) // LANES, tm, LANES), F32)],
        compiler_params=_cparams(("parallel",), 48),
        name=name,
    )(x2d, w)


def _linear_t_kernel(x_ref, wt_ref, o_ref):
    o_ref[...] = _nt_dot(wt_ref[...], x_ref[...].astype(CDT)).astype(o_ref.dtype)


def _linear_t(x2d, wt, tm, name):
    t, k = x2d.shape
    n = wt.shape[0]
    return pl.pallas_call(
        _linear_t_kernel,
        grid=(t // tm,),
        in_specs=[pl.BlockSpec((tm, k), lambda i: (i, 0)), pl.BlockSpec((n, k), lambda i: (0, 0))],
        out_specs=pl.BlockSpec((None, n, tm), lambda i: (i, 0, 0)),
        out_shape=jax.ShapeDtypeStruct((t // tm, n, tm), CDT),
        compiler_params=_cparams(("parallel",), 32),
        name=name,
    )(x2d, wt)


def _conv_gate_kernel(x_ref, xh_ref, wc_ref, cw_ref, wg_ref, ob_ref, g_ref, *, tm):
    i = pl.program_id(1)
    w = CONV_WIDTH
    xb = x_ref[...].astype(CDT)
    z = _dot(xb, wc_ref[...])
    u = z[:, w:2 * w] * z[:, 2 * w:3 * w]
    zh = _dot(xh_ref[...].astype(CDT), wc_ref[:, w:3 * w])
    uh = zh[:, :w] * zh[:, w:]
    uh = jnp.where(i == 0, 0.0, uh)
    row = lax.broadcasted_iota(jnp.int32, (tm, w), 0)
    u1 = jnp.where(row == 0, uh[7:8, :], pltpu.roll(u, 1, 0))
    u2 = jnp.where(row == 0, uh[6:7, :], jnp.where(row == 1, uh[7:8, :], pltpu.roll(u, 2, 0)))
    y = cw_ref[0:1, :] * u2 + cw_ref[1:2, :] * u1 + cw_ref[2:3, :] * u
    ob_ref[...] = (z[:, :w] * y).astype(ob_ref.dtype)
    g_ref[...] = jax.nn.sigmoid(_dot(xb, wg_ref[...]))


def _conv_gate(x, wc, conv_w, wg, tm=512):
    b, s, d = x.shape
    tm = min(tm, s)
    hb = tm // 8
    return pl.pallas_call(
        functools.partial(_conv_gate_kernel, tm=tm),
        grid=(b, s // tm),
        in_specs=[pl.BlockSpec((None, tm, d), lambda bi, i: (bi, i, 0)),
                  pl.BlockSpec((None, 8, d), lambda bi, i: (bi, jnp.maximum(i * hb - 1, 0), 0)),
                  pl.BlockSpec(wc.shape, lambda bi, i: (0, 0)),
                  pl.BlockSpec(conv_w.shape, lambda bi, i: (0, 0)),
                  pl.BlockSpec(wg.shape, lambda bi, i: (0, 0))],
        out_specs=[pl.BlockSpec((None, tm, CONV_WIDTH), lambda bi, i: (bi, i, 0)),
                   pl.BlockSpec((None, tm, wg.shape[1]), lambda bi, i: (bi, i, 0))],
        out_shape=[jax.ShapeDtypeStruct((b, s, CONV_WIDTH), CDT),
                   jax.ShapeDtypeStruct((b, s, wg.shape[1]), F32)],
        compiler_params=_cparams(("parallel", "parallel"), 48),
        name="conv_gate",
    )(x, x, wc, conv_w, wg)


def _banded_kernel(*refs, window, pr, tq, kw, want_lse, has_sink):
    q_ref, kp_ref, kc_ref, vp_ref, vc_ref = refs[:5]
    n = 5
    sink_ref = None
    if has_sink:
        sink_ref = refs[n]
        n += 1
    o_ref = refs[n]
    n += 1
    lse_ref = None
    if want_lse:
        lse_ref = refs[n]
        n += 1
    kbuf, vbuf, s_ref, e_ref, m_ref, l_ref = refs[n:n + 6]

    i = pl.program_id(2)
    kbuf[0:pr, :] = kp_ref[...]
    kbuf[pr:pr + tq, :] = kc_ref[...]
    vbuf[0:pr, :] = vp_ref[...]
    vbuf[pr:pr + tq, :] = vc_ref[...]

    span = SUB_Q + pr
    qi = lax.broadcasted_iota(jnp.int32, (SUB_Q, span), 0)
    kj = lax.broadcasted_iota(jnp.int32, (SUB_Q, span), 1)
    dist = pr + qi - kj
    band = (dist >= 0) & (dist <= window)
    lo = _half_masks()
    halves = (lo, jnp.logical_not(lo))
    groups = ((0,), (1,), (2,)) if kw == 3 * LANES else ((0, 1, 2),)
    rb = BAND_ROWS
    for sb in range(tq // SUB_Q):
        r0 = sb * SUB_Q
        bias = jnp.where(band & (i * tq + r0 - pr + kj >= 0), 0.0, NEG_INF)
        for grp in groups:
            kc0 = grp[0] * LANES if kw == 3 * LANES else 0
            qs = jnp.concatenate(
                [(jnp.where(hm, q_ref[r0:r0 + SUB_Q, p * LANES:(p + 1) * LANES], 0) * QK_SCALE).astype(CDT)
                 for p in grp for hm in halves], axis=0)
            g0 = 2 * grp[0] * SUB_Q
            s_ref[g0:g0 + qs.shape[0], :] = _nt_dot(qs, kbuf[r0:r0 + span, kc0:kc0 + LANES])
        for c0 in range(0, 6 * SUB_Q, rb):
            rows = slice(c0, c0 + rb)
            s = s_ref[rows, :] + bias[c0 % SUB_Q:c0 % SUB_Q + rb, :]
            m = jnp.max(s, axis=-1, keepdims=True)
            e = jnp.exp(s - m)
            e_ref[rows, :] = e.astype(CDT)
            m_ref[rows, :] = jnp.broadcast_to(m, (rb, LANES))
            l_ref[rows, :] = jnp.broadcast_to(jnp.sum(e, axis=-1, keepdims=True), (rb, LANES))
        for grp in groups:
            kc0 = grp[0] * LANES if kw == 3 * LANES else 0
            g0 = 2 * grp[0] * SUB_Q
            g1 = g0 + 2 * len(grp) * SUB_Q
            l = l_ref[g0:g1, :]
            o = _dot(e_ref[g0:g1, :], vbuf[r0:r0 + span, kc0:kc0 + LANES]) / l
            lse = m_ref[g0:g1, :] + jnp.log(l)
            for n_p, p in enumerate(grp):
                a = 2 * n_p * SUB_Q
                o_pair = jnp.where(lo, o[a:a + SUB_Q], o[a + SUB_Q:a + 2 * SUB_Q])
                lse_pair = jnp.where(lo, lse[a:a + SUB_Q], lse[a + SUB_Q:a + 2 * SUB_Q])
                if has_sink:
                    o_pair = o_pair * jax.nn.sigmoid(lse_pair - sink_ref[:, p * LANES:(p + 1) * LANES])
                o_ref[r0:r0 + SUB_Q, p * LANES:(p + 1) * LANES] = o_pair.astype(o_ref.dtype)
                if want_lse:
                    lse_ref[r0:r0 + SUB_Q, p * LANES:(p + 1) * LANES] = lse_pair


def _banded(qa, ka, va, *, nrep, qcol, kcol, vcol, kw, window, want_lse, sink_row=None, tq=256):
    b, l, _ = qa.shape
    pr = -(-window // SUB_Q) * SUB_Q
    tq = min(max(tq, pr), l)
    assert tq % pr == 0 and l % tq == 0, (tq, pr, l)
    ratio = tq // pr
    qw = 3 * LANES
    in_specs = [
        pl.BlockSpec((None, tq, qw), lambda bi, r, i: (bi, i, qcol(r))),
        pl.BlockSpec((None, pr, kw), lambda bi, r, i: (bi, jnp.maximum(i * ratio - 1, 0), kcol(r))),
        pl.BlockSpec((None, tq, kw), lambda bi, r, i: (bi, i, kcol(r))),
        pl.BlockSpec((None, pr, kw), lambda bi, r, i: (bi, jnp.maximum(i * ratio - 1, 0), vcol(r))),
        pl.BlockSpec((None, tq, kw), lambda bi, r, i: (bi, i, vcol(r))),
    ]
    args = [qa, ka, ka, va, va]
    if sink_row is not None:
        in_specs.append(pl.BlockSpec(sink_row.shape, lambda bi, r, i: (0, 0)))
        args.append(sink_row)
    out_specs = [pl.BlockSpec((None, tq, qw), lambda bi, r, i: (bi, i, r))]
    out_shape = [jax.ShapeDtypeStruct((b, l, nrep * qw), CDT)]
    if want_lse:
        out_specs.append(pl.BlockSpec((None, tq, qw), lambda bi, r, i: (bi, i, r)))
        out_shape.append(jax.ShapeDtypeStruct((b, l, nrep * qw), F32))
    res = pl.pallas_call(
        functools.partial(_banded_kernel, window=window, pr=pr, tq=tq, kw=kw, want_lse=want_lse,
                          has_sink=sink_row is not None),
        grid=(b, nrep, l // tq),
        in_specs=in_specs,
        out_specs=out_specs,
        out_shape=out_shape,
        scratch_shapes=[pltpu.VMEM((pr + tq, kw), ka.dtype), pltpu.VMEM((pr + tq, kw), va.dtype),
                        pltpu.VMEM((6 * SUB_Q, SUB_Q + pr), F32), pltpu.VMEM((6 * SUB_Q, SUB_Q + pr), CDT),
                        pltpu.VMEM((6 * SUB_Q, LANES), F32), pltpu.VMEM((6 * SUB_Q, LANES), F32)],
        compiler_params=_cparams(("parallel", "parallel", "parallel"), 32),
        name=f"banded_w{window}_k{kw}_r{nrep}",
    )(*args)
    return res


def _gelu_tanh(x):
    return 0.5 * x * (1.0 + jnp.tanh(0.7978845608028654 * (x + 0.044715 * (x * x * x))))


def _compress_kernel(x_ref, pa_ref, pb_ref, w1a_ref, w1b_ref, b1_ref, w2_ref, b2_ref, o_ref):
    x = x_ref[...].astype(F32)
    n = x.shape[0]
    a = _dot((x + pa_ref[...]).astype(CDT), w1a_ref[...])
    bm = _dot((x + pb_ref[...]).astype(CDT), w1b_ref[...])
    h = a + pltpu.roll(bm, n - 1, 0) + b1_ref[...]
    o_ref[...] = (_dot(_gelu_tanh(h).astype(CDT), w2_ref[...]) + b2_ref[...]).astype(o_ref.dtype)


def _compress(t, pos, w1, b1, w2, b2):
    b, s, _ = t.shape
    nch = s // CMP_STRIDE
    xw = CMP_STRIDE * LANES
    x = t.reshape(b, nch, xw)
    eye = jnp.eye(NSA_KV_HEADS, dtype=F32)
    w1r = w1.reshape(CMP_BLOCK, HEAD_DIM, CMP_HIDDEN)

    def expand_w1(part):
        return jnp.einsum('tdj,kl->tkdlj', part, eye).reshape(xw, NSA_KV_HEADS * CMP_HIDDEN).astype(CDT)

    def expand_pos(part):
        return jnp.broadcast_to(part[:, None, :], (CMP_STRIDE, NSA_KV_HEADS, HEAD_DIM)).reshape(1, xw)

    w1a, w1b = expand_w1(w1r[:CMP_STRIDE]), expand_w1(w1r[CMP_STRIDE:])
    pa, pb = expand_pos(pos[:CMP_STRIDE]), expand_pos(pos[CMP_STRIDE:])
    b1e = jnp.tile(b1, NSA_KV_HEADS).reshape(1, -1)
    w2e = jnp.einsum('jd,kl->kjld', w2, eye).reshape(NSA_KV_HEADS * CMP_HIDDEN, LANES).astype(CDT)
    b2e = jnp.tile(b2, NSA_KV_HEADS).reshape(1, -1)
    consts = [pa, pb, w1a, w1b, b1e, w2e, b2e]
    return pl.pallas_call(
        _compress_kernel,
        grid=(b,),
        in_specs=[pl.BlockSpec((None, nch, xw), lambda bi: (bi, 0, 0))]
        + [pl.BlockSpec(c.shape, lambda bi: (0, 0)) for c in consts],
        out_specs=pl.BlockSpec((None, nch, LANES), lambda bi: (bi, 0, 0)),
        out_shape=jax.ShapeDtypeStruct((b, nch, LANES), CDT),
        compiler_params=_cparams(("parallel",), 48),
        name="nsa_compress",
    )(x, *consts)


CMP_ROWS = 16


def _cmp_select_kernel(q_ref, kc_ref, vc_ref, ov_ref, o_ref, sel_ref, s_ref, p_ref, hi_ref, lo_ref,
                       *, tq, n_sel, tile0):
    i = pl.program_id(1) + tile0
    ncp = kc_ref.shape[0]
    ns = ov_ref.shape[1]
    lo = _half_masks()
    t_col = i * tq + lax.broadcasted_iota(jnp.int32, (tq, 1), 0)
    blk = lax.broadcasted_iota(jnp.int32, (tq, ns), 1)
    blk_t = lax.broadcasted_iota(jnp.int32, (ns, tq), 0)
    cur = t_col // SEL_BLOCK
    causal = blk <= cur
    forced = (blk == 0) | (blk == cur) | (blk == cur - 1)
    rb = CMP_ROWS
    c_end = lax.broadcasted_iota(jnp.int32, (rb, ncp), 1) * CMP_STRIDE + (CMP_BLOCK - 1)
    o_kv, work_t = [], []
    for kv, hm in enumerate((lo, jnp.logical_not(lo))):
        qs = jnp.concatenate(
            [(jnp.where(hm, q_ref[:, p * LANES:(p + 1) * LANES], 0) * QK_SCALE).astype(CDT) for p in range(3)], axis=0)
        s_ref[...] = _nt_dot(qs, kc_ref[...])
        for r0 in range(0, tq, rb):
            t_rows = i * tq + r0 + lax.broadcasted_iota(jnp.int32, (rb, 1), 0)
            vis_bias = jnp.where(c_end <= t_rows, 0.0, NEG_INF)
            has_visible = t_rows >= CMP_BLOCK - 1
            psum = jnp.zeros((rb, ncp), F32)
            for h in range(3):
                rows = slice(h * tq + r0, h * tq + r0 + rb)
                s = s_ref[rows, :] + vis_bias
                e = jnp.exp(s - jnp.max(s, axis=-1, keepdims=True))
                inv = jnp.where(has_visible, 1.0 / jnp.maximum(jnp.sum(e, axis=-1, keepdims=True), 1e-30), 0.0)
                pn = e * inv
                p_ref[rows, :] = pn.astype(CDT)
                psum = psum + pn
            p_hi = psum.astype(CDT)
            hi_ref[r0:r0 + rb, :] = p_hi
            lo_ref[r0:r0 + rb, :] = (psum - p_hi.astype(F32)).astype(CDT)
        o = _dot(p_ref[...], vc_ref[...])
        o_kv.append([o[p * tq:(p + 1) * tq] for p in range(3)])
        imp = _dot(hi_ref[...], ov_ref[...]) + _dot(lo_ref[...], ov_ref[...])
        work_t.append(jnp.where(causal & jnp.logical_not(forced), imp, -BIG).T)

    def pick(_, work):
        m = jnp.max(work, axis=0, keepdims=True)
        idx = jnp.min(jnp.where(work == m, blk_t, ns), axis=0, keepdims=True)
        return jnp.where(blk_t == idx, -2.0 * BIG, work)

    for kv, start in enumerate(work_t):
        done = lax.fori_loop(0, n_sel - 3, pick, start)
        taken = jnp.where((done < -BIG) & (start > -BIG), 1.0, 0.0).T
        selb = jnp.where(forced | (taken > 0.5), 0.0, NEG_INF)
        sel_ref[:, kv * ns:(kv + 1) * ns] = selb.astype(sel_ref.dtype)
    for p in range(3):
        o_ref[:, p * LANES:(p + 1) * LANES] = jnp.where(lo, o_kv[0][p], o_kv[1][p]).astype(o_ref.dtype)


CMP_CAUSAL_SPLITS = 4


def _cmp_select(zr, qcol, kc, vc, tq=128):
    b, s, _ = zr.shape
    ncp = kc.shape[1]
    ns = s // SEL_BLOCK
    n_sel = min(N_SEL, ns)
    assert n_sel >= 3, "selection needs room for the three forced blocks"
    tq = min(tq, s)
    c = np.arange(ncp)[:, None] * CMP_STRIDE
    j = np.arange(ns)[None, :] * SEL_BLOCK
    overlap = ((c < j + SEL_BLOCK) & (c + CMP_BLOCK - 1 >= j)).astype(np.float32)
    overlap[ncp - 1:, :] = 0.0
    ov = jnp.asarray(overlap, CDT)
    qw = 3 * LANES
    n_split = CMP_CAUSAL_SPLITS if (s // tq) % CMP_CAUSAL_SPLITS == 0 and ncp % (16 * CMP_CAUSAL_SPLITS) == 0 else 1
    tiles = s // tq // n_split
    outs, sels = [], []
    for part in range(n_split):
        tile0 = part * tiles
        ncp_part = ncp * (part + 1) // n_split
        o_part, sel_part = pl.pallas_call(
            functools.partial(_cmp_select_kernel, tq=tq, n_sel=n_sel, tile0=tile0),
            grid=(b, tiles),
            in_specs=[pl.BlockSpec((None, tq, qw), lambda bi, i, tile0=tile0: (bi, i + tile0, qcol)),
                      pl.BlockSpec((None, ncp_part, LANES), lambda bi, i: (bi, 0, 0)),
                      pl.BlockSpec((None, ncp_part, LANES), lambda bi, i: (bi, 0, 0)),
                      pl.BlockSpec((ncp_part, ns), lambda bi, i: (0, 0))],
            out_specs=[pl.BlockSpec((None, tq, qw), lambda bi, i: (bi, i, 0)),
                       pl.BlockSpec((None, tq, 2 * ns), lambda bi, i: (bi, i, 0))],
            out_shape=[jax.ShapeDtypeStruct((b, tiles * tq, qw), CDT),
                       jax.ShapeDtypeStruct((b, tiles * tq, 2 * ns), CDT)],
            scratch_shapes=[pltpu.VMEM((3 * tq, ncp_part), F32), pltpu.VMEM((3 * tq, ncp_part), CDT),
                            pltpu.VMEM((tq, ncp_part), CDT), pltpu.VMEM((tq, ncp_part), CDT)],
            compiler_params=_cparams(("parallel", "parallel"), 48),
            name=f"nsa_cmp_select_p{part}",
        )(zr, kc, vc, ov)
        outs.append(o_part)
        sels.append(sel_part)
    return jnp.concatenate(outs, axis=1), jnp.concatenate(sels, axis=1)


SLC_KEY_CHUNK = 256


def _slc_kernel(q_ref, k0_ref, k1_ref, vt_ref, selb_ref, o_ref, s_ref, e_ref, m_ref, l_ref, a_ref, acc_ref,
                *, tq, tk):
    i = pl.program_id(1)
    ns = selb_ref.shape[1] // 2
    bpt = tk // SEL_BLOCK
    n_q = 3 * tq
    rc = min(SLC_KEY_CHUNK, tk)
    lo = _half_masks()
    n_tiles = ((i + 1) * tq + tk - 1) // tk
    p_row = lax.broadcasted_iota(jnp.int32, (ns, LANES), 0)
    p_col = lax.broadcasted_iota(jnp.int32, (ns, LANES), 1)
    groups = []
    for kv, (hm, k_ref) in enumerate(((lo, k0_ref), (jnp.logical_not(lo), k1_ref))):
        q3 = [(jnp.where(hm, q_ref[:, p * LANES:(p + 1) * LANES], 0) * QK_SCALE).astype(CDT) for p in range(3)]
        selb = selb_ref[:, kv * ns:(kv + 1) * ns]
        lane0 = HEAD_DIM if kv == 0 else 0
        groups.append((kv, hm, k_ref, q3, selb, lane0))
    m_ref[...] = jnp.full(m_ref.shape, NEG_INF, F32)
    l_ref[...] = jnp.zeros(l_ref.shape, F32)
    acc_ref[...] = jnp.zeros(acc_ref.shape, F32)

    def scores(j):
        k0 = pl.multiple_of(j * tk, tk)
        for kv, hm, k_ref, q3, selb, lane0 in groups:
            place = ((p_col >= lane0) & (p_col < lane0 + bpt) & (p_row == p_col - lane0 + j * bpt)).astype(CDT)
            sb = _dot(selb, place).astype(CDT)
            qp = jnp.concatenate([jnp.where(hm, q, sb) for q in q3], axis=0)
            s_ref[kv] = _nt_dot(k_ref[pl.ds(k0, tk), :], qp)

    def softmax_pv(j, diagonal):
        k0 = pl.multiple_of(j * tk, tk)
        for kv in range(2):
            for c0 in range(0, n_q, LANES):
                cols = slice(c0, c0 + LANES)
                t_lane = i * tq + (c0 % tq) + lax.broadcasted_iota(jnp.int32, (1, LANES), 1)

                def chunk(r0):
                    s = s_ref[kv, r0:r0 + rc, cols]
                    if diagonal:
                        kpos = k0 + r0 + lax.broadcasted_iota(jnp.int32, (rc, LANES), 0)
                        s = jnp.where(kpos <= t_lane, s, NEG_INF)
                    return s

                m8 = m_ref[kv, :, cols]
                for r0 in range(0, tk, rc):
                    m8 = jnp.maximum(m8, jnp.max(chunk(r0).reshape(rc // 8, 8, LANES), axis=0))
                m_new = jnp.max(m8, axis=0, keepdims=True)
                alpha = jnp.exp(m_ref[kv, :, cols] - m_new)
                l8 = jnp.zeros((8, LANES), F32)
                for r0 in range(0, tk, rc):
                    e = jnp.exp(chunk(r0) - m_new)
                    l8 = l8 + jnp.sum(e.reshape(rc // 8, 8, LANES), axis=0)
                    e_ref[kv, r0:r0 + rc, cols] = e.astype(CDT)
                l_ref[kv, :, cols] = alpha * l_ref[kv, :, cols] + jnp.sum(l8, axis=0, keepdims=True)
                m_ref[kv, :, cols] = jnp.broadcast_to(m_new, (8, LANES))
                a_ref[kv, :, cols] = alpha
        for kv in range(2):
            acc_ref[kv] = a_ref[kv, 0:1, :] * acc_ref[kv] + _dot(vt_ref[j], e_ref[kv])

    def tile(j, diagonal):
        scores(j)
        softmax_pv(j, diagonal)

    lax.fori_loop(0, n_tiles - 1, lambda j, c: (tile(j, False), c)[1], 0)
    tile(n_tiles - 1, True)
    o0 = (acc_ref[0] / l_ref[0, 0:1, :]).T
    o1 = (acc_ref[1] / l_ref[1, 0:1, :]).T
    for p in range(3):
        o_ref[:, p * LANES:(p + 1) * LANES] = jnp.where(
            lo, o0[p * tq:(p + 1) * tq], o1[p * tq:(p + 1) * tq]).astype(o_ref.dtype)


def _slc(zr, qcol, ksc, vt, selb, tq=512):
    b, s, _ = zr.shape
    tq = min(tq, s)
    tk = vt.shape[2]
    assert tq % LANES == 0 and tk // SEL_BLOCK <= HEAD_DIM and s % tk == 0
    qw = 3 * LANES
    ns2 = selb.shape[2]
    pat = jax.nn.one_hot((jnp.arange(s) // SEL_BLOCK) % (tk // SEL_BLOCK), HEAD_DIM, dtype=ksc.dtype)
    pat = jnp.broadcast_to(pat[None], (b, s, HEAD_DIM))
    k0 = jnp.concatenate([ksc[..., :HEAD_DIM], pat], axis=-1)
    k1 = jnp.concatenate([pat, ksc[..., HEAD_DIM:]], axis=-1)
    full = lambda bi, i: (bi, 0, 0)
    return pl.pallas_call(
        functools.partial(_slc_kernel, tq=tq, tk=tk),
        grid=(b, s // tq),
        in_specs=[pl.BlockSpec((None, tq, qw), lambda bi, i: (bi, i, qcol)),
                  _single_buffered((None, s, LANES), full),
                  _single_buffered((None, s, LANES), full),
                  _single_buffered((s // tk, LANES, tk), lambda bi, i: (bi, 0, 0)),
                  pl.BlockSpec((None, tq, ns2), lambda bi, i: (bi, i, 0))],
        out_specs=pl.BlockSpec((None, tq, qw), lambda bi, i: (bi, i, 0)),
        out_shape=jax.ShapeDtypeStruct((b, s, qw), CDT),
        scratch_shapes=[pltpu.VMEM((2, tk, 3 * tq), F32), pltpu.VMEM((2, tk, 3 * tq), CDT),
                        pltpu.VMEM((2, 8, 3 * tq), F32), pltpu.VMEM((2, 8, 3 * tq), F32),
                        pltpu.VMEM((2, 8, 3 * tq), F32), pltpu.VMEM((2, LANES, 3 * tq), F32)],
        compiler_params=_cparams(("parallel", "arbitrary"), 56),
        name="nsa_slc",
    )(zr, k0, k1, vt, selb)


def _merge_kernel(x_ref, oa0, oa1, oa2, la0, la1, la2, ob, ocmp, oslc, owin, gates, od,
                  wg_ref, bg_ref, wb_ref, wo_ref, g_ref, b_ref, o_ref, *scratch, alpha):
    x = x_ref[...]
    xb = x.astype(CDT)
    bw = BRANCH_WIDTH

    def token_rows(src_ref, dst_ref):
        dil = src_ref.shape[1] // bw
        if dil == 1:
            return src_ref[...].astype(F32)
        n_cb = bw // LANES
        for r in range(dil):
            for cb in range(n_cb):
                c0 = r * bw + cb * LANES
                dst_ref[cb, pl.ds(r, src_ref.shape[0], stride=dil), :] = src_ref[:, c0:c0 + LANES].astype(F32)
        return jnp.concatenate([dst_ref[cb] for cb in range(n_cb)], axis=1)

    o0, o1, o2 = (token_rows(s, d) for s, d in zip((oa0, oa1, oa2), scratch[0:3]))
    l0, l1, l2 = (token_rows(s, d) for s, d in zip((la0, la1, la2), scratch[3:6]))
    mx = jnp.maximum(jnp.maximum(l0, l1), l2)
    w0, w1, w2 = jnp.exp(l0 - mx), jnp.exp(l1 - mx), jnp.exp(l2 - mx)
    o_a = (w0 * o0 + w1 * o1 + w2 * o2) / (w0 + w1 + w2)
    o_c = (gates[:, 0:bw] * ocmp[...].astype(F32) + gates[:, bw:2 * bw] * oslc[...].astype(F32)
           + gates[:, 2 * bw:3 * bw] * owin[...].astype(F32))
    branches = (o_a.astype(CDT), ob[...], o_c.astype(CDT), od[...])
    d = x.shape[1]
    merged = jnp.zeros(x.shape, F32)
    for m in range(N_BRANCH):
        gate = jax.nn.sigmoid(_dot(xb, wg_ref[:, m * d:(m + 1) * d]) + bg_ref[:, m * d:(m + 1) * d])
        merged = merged + gate * _dot(branches[m], wb_ref[m])
    r = alpha * x + _dot(merged.astype(CDT), wo_ref[...])
    o_ref[...] = _layer_norm(r, g_ref[...], b_ref[...])


def _merge(x2d, branch_inputs, wg, bg, wb, wo, g, b, alpha, tm=256):
    t, d = x2d.shape
    tm = min(tm, t)
    row = lambda i: (i, 0)
    const2 = lambda i: (0, 0)
    in_specs = [pl.BlockSpec((tm, d), row)]
    in_specs += [pl.BlockSpec((tm * a.shape[0] // t, a.shape[1]), row) for a in branch_inputs]
    in_specs += [pl.BlockSpec(wg.shape, const2), pl.BlockSpec(bg.shape, const2),
                 pl.BlockSpec(wb.shape, lambda i: (0, 0, 0)), pl.BlockSpec(wo.shape, const2),
                 pl.BlockSpec(g.shape, const2), pl.BlockSpec(b.shape, const2)]
    return pl.pallas_call(
        functools.partial(_merge_kernel, alpha=alpha),
        grid=(t // tm,),
        in_specs=in_specs,
        out_specs=pl.BlockSpec((tm, d), row),
        out_shape=jax.ShapeDtypeStruct((t, d), F32),
        scratch_shapes=[pltpu.VMEM((BRANCH_WIDTH // LANES, tm, LANES), F32) for _ in range(2 * N_DIL)],
        compiler_params=_cparams(("parallel",), 56),
        name="merge_ln",
    )(x2d, *branch_inputs, wg, bg, wb, wo, g, b)


def _ple_ln(x, xb, f, p, plw_ref, pgw_ref, pgb_ref, g_ref, b_ref, alpha):
    ple = jax.nn.sigmoid(_dot(xb, pgw_ref[...]) + pgb_ref[...]) * _dot(p.astype(CDT), plw_ref[...])
    return _layer_norm(alpha * x + f + ple, g_ref[...], b_ref[...])


FFN_CHUNK = 512


def _ffn_kernel(x_ref, p_ref, wg_ref, wu_ref, wd_ref, plw_ref, pgw_ref, pgb_ref, g_ref, b_ref, o_ref, h_ref, *, alpha):
    x = x_ref[...]
    xb = x.astype(CDT)
    dff = wg_ref.shape[1]
    for c0 in range(0, dff, FFN_CHUNK):
        cols = slice(c0, min(c0 + FFN_CHUNK, dff))
        h_ref[:, cols] = (jax.nn.silu(_dot(xb, wg_ref[:, cols])) * _dot(xb, wu_ref[:, cols])).astype(CDT)
    f = _dot(h_ref[...], wd_ref[...])
    o_ref[...] = _ple_ln(x, xb, f, p_ref[...], plw_ref, pgw_ref, pgb_ref, g_ref, b_ref, alpha)


def _ffn(x2d, p2d, wg, wu, wd, plw, pgw, pgb, g, b, alpha, tm=512):
    t, d = x2d.shape
    tm = min(tm, t)
    dff = wg.shape[1]
    wg, wu, wd = wg.astype(CDT), wu.astype(CDT), wd.astype(CDT)
    row = lambda i: (i, 0)
    const = lambda shape: pl.BlockSpec(shape, lambda i: (0, 0), pipeline_mode=pl.Buffered(1))
    return pl.pallas_call(
        functools.partial(_ffn_kernel, alpha=alpha),
        grid=(t // tm,),
        in_specs=[pl.BlockSpec((tm, d), row), pl.BlockSpec((tm, p2d.shape[1]), row),
                  const(wg.shape), const(wu.shape), const(wd.shape),
                  const(plw.shape), const(pgw.shape), const(pgb.shape), const(g.shape), const(b.shape)],
        out_specs=pl.BlockSpec((tm, d), row),
        out_shape=jax.ShapeDtypeStruct((t, d), F32),
        scratch_shapes=[pltpu.VMEM((tm, dff), CDT)],
        compiler_params=_cparams(("parallel",), 48),
        name="ffn_ple_ln",
    )(x2d, p2d, wg, wu, wd, plw, pgw, pgb, g, b)


def _router_kernel(x_ref, wh_ref, wl_ref, b_ref, comb_ref, rank_ref, rank_t_ref, cnt_ref):
    x = x_ref[...]
    xh = x.astype(CDT)
    xl = (x - xh.astype(F32)).astype(CDT)
    logits = _dot(xh, wh_ref[...]) + _dot(xh, wl_ref[...]) + _dot(xl, wh_ref[...]) + b_ref[...]
    lane = lax.broadcasted_iota(jnp.int32, logits.shape, 1)
    v1 = jnp.max(logits, axis=-1, keepdims=True)
    i1 = jnp.min(jnp.where(logits == v1, lane, LANES), axis=-1, keepdims=True)
    rest = jnp.where(lane == i1, -jnp.inf, logits)
    v2 = jnp.max(rest, axis=-1, keepdims=True)
    i2 = jnp.min(jnp.where(rest == v2, lane, LANES), axis=-1, keepdims=True)
    e2 = jnp.exp(v2 - v1)
    comb_ref[...] = jnp.where(lane == i1, 1.0 / (1.0 + e2), 0.0) + jnp.where(lane == i2, e2 / (1.0 + e2), 0.0)
    routed = (lane == i1) | (lane == i2)
    mask = routed.astype(CDT)
    tm = x.shape[0]
    before = (lax.broadcasted_iota(jnp.int32, (tm, tm), 1) < lax.broadcasted_iota(jnp.int32, (tm, tm), 0)).astype(CDT)
    rank = jnp.where(routed, _dot(before, mask), -1.0)
    rank_ref[...] = rank
    rank_t_ref[...] = rank.T[0:rank_t_ref.shape[0], :]
    cnt_ref[...] = jnp.sum(routed.astype(F32), axis=0, keepdims=True).astype(jnp.int32)


def _router(x2d, w_router, b_router, tm):
    t, d = x2d.shape
    ne = w_router.shape[1]
    wp = jnp.zeros((d, LANES), F32).at[:, :ne].set(w_router)
    wh = wp.astype(CDT)
    wl = (wp - wh.astype(F32)).astype(CDT)
    bp = jnp.full((1, LANES), -BIG, F32).at[0, :ne].set(b_router)
    nt = t // tm
    row = lambda i: (i, 0)
    return pl.pallas_call(
        _router_kernel,
        grid=(nt,),
        in_specs=[pl.BlockSpec((tm, d), row), pl.BlockSpec(wh.shape, lambda i: (0, 0)),
                  pl.BlockSpec(wl.shape, lambda i: (0, 0)), pl.BlockSpec(bp.shape, lambda i: (0, 0))],
        out_specs=[pl.BlockSpec((tm, LANES), row), pl.BlockSpec((tm, LANES), row),
                   pl.BlockSpec((None, 8, tm), lambda i: (i, 0, 0)),
                   pl.BlockSpec((None, 1, LANES), lambda i: (i, 0, 0))],
        out_shape=[jax.ShapeDtypeStruct((t, LANES), F32), jax.ShapeDtypeStruct((t, LANES), F32),
                   jax.ShapeDtypeStruct((nt, 8, tm), F32), jax.ShapeDtypeStruct((nt, 1, LANES), jnp.int32)],
        compiler_params=_cparams(("parallel",), 40),
        name="moe_router",
    )(x2d, wh, wl, bp)


def _moe_kernel(cnt_ref, x_ref, comb_ref, rank_ref, rank_t_ref, p_ref, wg_ref, wu_ref, wd_ref, plw_ref, pgw_ref,
                pgb_ref, g_ref, b_ref, o_ref, xe_ref, ye_ref, *, alpha, rs, seg):
    i = pl.program_id(0)
    e = pl.program_id(1)
    c = pl.program_id(2)
    n_seg = x_ref.shape[0] // seg
    last_chunk = c == pl.num_programs(2) - 1

    def rows(sc):
        return pl.ds(pl.multiple_of(sc * rs, 8), rs)

    @pl.when((e == 0) & (c == 0))
    def _():
        o_ref[...] = jnp.zeros_like(o_ref)

    for sg in range(n_seg):
        tok = slice(sg * seg, (sg + 1) * seg)
        n_groups = (cnt_ref[(i * n_seg + sg) * LANES + e] + rs - 1) // rs

        @pl.when(c == 0)
        def _():
            xb = x_ref[tok, :].astype(CDT)
            rank_row = rank_t_ref[sg, pl.ds(e, 1), :]
            row_id = lax.broadcasted_iota(jnp.int32, (rs, seg), 0).astype(F32)

            def gather(sc, _):
                onehot = (rank_row - (sc * rs).astype(F32) == row_id).astype(CDT)
                xe_ref[sg, rows(sc), :] = _dot(onehot, xb).astype(CDT)
                ye_ref[sg, rows(sc), :] = jnp.zeros((rs, ye_ref.shape[2]), F32)
                return 0

            lax.fori_loop(0, n_groups, gather, 0)

        def expert(sc, _):
            xs = xe_ref[sg, rows(sc), :]
            h = jax.nn.silu(_dot(xs, wg_ref[0])) * _dot(xs, wu_ref[0])
            ye_ref[sg, rows(sc), :] += _dot(h.astype(CDT), wd_ref[0])
            return 0

        lax.fori_loop(0, n_groups, expert, 0)

        @pl.when(last_chunk)
        def _():
            lane = lax.broadcasted_iota(jnp.int32, (seg, LANES), 1)
            mine = lane == e
            cw = jnp.sum(jnp.where(mine, comb_ref[tok, :], 0.0), axis=-1, keepdims=True)
            rank_col = jnp.sum(jnp.where(mine, rank_ref[tok, :], 0.0), axis=-1, keepdims=True)
            col_id = lax.broadcasted_iota(jnp.int32, (seg, rs), 1).astype(F32)

            def scatter(sc, _):
                onehot = (rank_col - (sc * rs).astype(F32) == col_id).astype(CDT)
                o_ref[tok, :] += cw * _dot(onehot, ye_ref[sg, rows(sc), :].astype(CDT))
                return 0

            lax.fori_loop(0, n_groups, scatter, 0)

    @pl.when((e == pl.num_programs(1) - 1) & last_chunk)
    def _():
        for sg in range(n_seg):
            tok = slice(sg * seg, (sg + 1) * seg)
            x = x_ref[tok, :]
            o_ref[tok, :] = _ple_ln(x, x.astype(CDT), o_ref[tok, :], p_ref[tok, :], plw_ref, pgw_ref, pgb_ref,
                                    g_ref, b_ref, alpha)


MOE_CHUNK = 512
MOE_SEGMENT = 1024
MOE_SEGMENTS_PER_TILE = 2
MOE_ROW_GROUP = 288


def _single_buffered(shape, index_map):
    return pl.BlockSpec(shape, index_map, pipeline_mode=pl.Buffered(1))


def _moe(x2d, routing, p2d, wg, wu, wd, plw, pgw, pgb, g, b, alpha, seg):
    comb, rank, rank_t, cnt = routing
    t, d = x2d.shape
    ne, _, dff = wg.shape
    ck = min(MOE_CHUNK, dff)
    rs = min(MOE_ROW_GROUP, seg)
    n_seg = min(MOE_SEGMENTS_PER_TILE, t // seg)
    tm = n_seg * seg
    max_rows = -(-seg // rs) * rs
    nck = dff // ck
    wg = wg.reshape(ne, d, nck, ck).transpose(0, 2, 1, 3).astype(CDT)
    wu = wu.reshape(ne, d, nck, ck).transpose(0, 2, 1, 3).astype(CDT)
    wd = wd.reshape(ne, nck, ck, d).astype(CDT)
    row = lambda i, e, c, cnt: (i, 0)
    c2 = lambda i, e, c, cnt: (0, 0)
    grid_spec = pltpu.PrefetchScalarGridSpec(
        num_scalar_prefetch=1,
        grid=(t // tm, ne, dff // ck),
        in_specs=[_single_buffered((tm, d), row), _single_buffered((tm, LANES), row),
                  _single_buffered((tm, LANES), row),
                  _single_buffered((n_seg, 8, seg), lambda i, e, c, cnt: (i, 0, 0)),
                  _single_buffered((tm, p2d.shape[1]), row),
                  pl.BlockSpec((None, 1, d, ck), lambda i, e, c, cnt: (e, c, 0, 0)),
                  pl.BlockSpec((None, 1, d, ck), lambda i, e, c, cnt: (e, c, 0, 0)),
                  pl.BlockSpec((None, 1, ck, d), lambda i, e, c, cnt: (e, c, 0, 0)),
                  _single_buffered(plw.shape, c2), _single_buffered(pgw.shape, c2), _single_buffered(pgb.shape, c2),
                  _single_buffered(g.shape, c2), _single_buffered(b.shape, c2)],
        out_specs=pl.BlockSpec((tm, d), row),
        scratch_shapes=[pltpu.VMEM((n_seg, max_rows, d), CDT), pltpu.VMEM((n_seg, max_rows, d), F32)],
    )
    return pl.pallas_call(
        functools.partial(_moe_kernel, alpha=alpha, rs=rs, seg=seg),
        grid_spec=grid_spec,
        out_shape=jax.ShapeDtypeStruct((t, d), F32),
        compiler_params=_cparams(("parallel", "arbitrary", "arbitrary"), 60),
        name="moe_ple_ln",
    )(cnt.reshape(-1), x2d, comb, rank, rank_t, p2d, wg, wu, wd, plw, pgw, pgb, g, b)


def _prep_in_weights(w_in):
    o = COL_OFF
    bw = BRANCH_WIDTH
    cols = lambda n: w_in[:, o[n]:o[n + 1]]
    qa, ka, va = cols(0), cols(1), cols(2)
    w_dil = jnp.concatenate(
        [t[:, g * bw:(g + 1) * bw] for g in range(N_DIL) for t in (qa, ka, va)], axis=1).astype(CDT)
    w_conv = jnp.concatenate([cols(3), cols(4), cols(5)], axis=1).astype(CDT)
    gn = cols(13)
    w_gate = jnp.concatenate([gn[:, br * NSA_Q_HEADS + GQA_COL_HEAD] for br in range(3)], axis=1).astype(CDT)
    w_rest = jnp.concatenate([cols(6)[:, GQA_COL_PERM], cols(14)[:, GQA_COL_PERM], cols(11),
                              cols(12), cols(15), cols(16), cols(9), cols(7), cols(8)], axis=1).astype(CDT)
    wt_vsc = cols(10).T.astype(CDT)
    return w_dil, w_conv, w_gate, w_rest, wt_vsc


ZR_Q_NSA, ZR_Q_SWA = 0, 1
ZR_KWC, ZR_VWC, ZR_KD, ZR_VD = 6, 7, 8, 9
ZR_WIDTH = 2 * BRANCH_WIDTH + 4 * LANES
SLC_KEY_TILE = 1024


def _token_mixers(x, w_in, conv_w, cmp_pos, cmp_w1, cmp_b1, cmp_w2, cmp_b2, sinks):
    b, s, d = x.shape
    x2d = x.reshape(b * s, d)
    w_dil, w_conv, w_gate, w_rest, wt_vsc = _prep_in_weights(w_in)
    gw = 3 * BRANCH_WIDTH

    z_dil = _linear(x2d, w_dil, [(g * gw, (g + 1) * gw) for g in range(N_DIL)], gw, "in_proj_dil",
                    dils=[dil for _, dil in DIL_PATTERNS])
    zr, ksc, kcc, vcc = _linear(x2d, w_rest, [(0, ZR_WIDTH)] + [(ZR_WIDTH + n * LANES, ZR_WIDTH + (n + 1) * LANES)
                                                           for n in range(3)], 256, "in_proj_rest")
    zr = zr.reshape(b, s, ZR_WIDTH)
    o_b, gates = _conv_gate(x, w_conv, conv_w, w_gate)

    dil_o, dil_lse = [], []
    for g, (window, dil) in enumerate(DIL_PATTERNS):
        view = z_dil[g].reshape(b, s // dil, dil * gw)
        og, lg = _banded(view, view, view, nrep=dil,
                         qcol=lambda r: 3 * r, kcol=lambda r: 3 * r + 1, vcol=lambda r: 3 * r + 2,
                         kw=3 * LANES, window=window // dil, want_lse=True)
        dil_o.append(og.reshape(b * s // dil, dil * BRANCH_WIDTH))
        dil_lse.append(lg.reshape(b * s // dil, dil * BRANCH_WIDTH))

    kc = _compress(kcc.reshape(b, s, LANES), cmp_pos[0], cmp_w1[0], cmp_b1[0], cmp_w2[0], cmp_b2[0])
    vc = _compress(vcc.reshape(b, s, LANES), cmp_pos[1], cmp_w1[1], cmp_b1[1], cmp_w2[1], cmp_b2[1])
    o_cmp, selb = _cmp_select(zr, ZR_Q_NSA, kc, vc)
    vsc_t = _linear_t(x2d, wt_vsc, min(SLC_KEY_TILE, s), "in_proj_vsc_t")
    o_slc = _slc(zr, ZR_Q_NSA, ksc.reshape(b, s, LANES), vsc_t, selb)
    (o_win,) = _banded(zr, zr, zr, nrep=1, qcol=lambda r: ZR_Q_NSA, kcol=lambda r: ZR_KWC, vcol=lambda r: ZR_VWC,
                       kw=LANES, window=NSA_WINDOW - 1, want_lse=False, tq=512)

    sink_row = sinks.astype(F32)[GQA_COL_HEAD].reshape(1, BRANCH_WIDTH)
    (o_d,) = _banded(zr, zr, zr, nrep=1, qcol=lambda r: ZR_Q_SWA, kcol=lambda r: ZR_KD, vcol=lambda r: ZR_VD,
                     kw=LANES, window=SWA_WINDOW - 1, want_lse=False, sink_row=sink_row)

    t = b * s
    flat = lambda a: a.reshape(t, a.shape[-1])
    return [dil_o[0], dil_o[1], dil_o[2], dil_lse[0], dil_lse[1], dil_lse[2], flat(o_b), flat(o_cmp), flat(o_slc),
            flat(o_win), flat(gates), flat(o_d)]


def kernel(x, p, w_in, conv_w, cmp_pos, cmp_w1, cmp_b1, cmp_w2, cmp_b2, sinks, w_branch, w_merge_gate, b_merge_gate, w_out, ln_mix_g, ln_mix_b, ffn_w_gate, ffn_w_up, ffn_w_down, w_router, b_router, moe_w_gate, moe_w_up, moe_w_down, ple_w, ple_gate_w, ple_gate_b, ln_ffn_g, ln_ffn_b):
    depth, b, s, _ = p.shape
    d = x.shape[-1]
    t = b * s
    alpha = (2 * depth) ** 0.25
    row = lambda v: v.reshape(1, -1).astype(F32)
    for i in range(depth):
        branch_inputs = _token_mixers(x, w_in[i], conv_w[i], cmp_pos[i], cmp_w1[i], cmp_b1[i], cmp_w2[i],
                                      cmp_b2[i], sinks[i])
        wg = jnp.concatenate([w_merge_gate[i, m] for m in range(N_BRANCH)], axis=1).astype(CDT)
        bg = b_merge_gate[i].reshape(1, N_BRANCH * d).astype(F32)
        wb = jnp.stack([w_branch[i, 0], w_branch[i, 1], w_branch[i, 2][GQA_COL_PERM],
                        w_branch[i, 3][GQA_COL_PERM]]).astype(CDT)
        x1 = _merge(x.reshape(t, d), branch_inputs, wg, bg, wb, w_out[i].astype(CDT),
                    row(ln_mix_g[i]), row(ln_mix_b[i]), alpha)
        p2d = p[i].reshape(t, -1)
        ple_args = (ple_w[i].astype(CDT), ple_gate_w[i].astype(CDT), row(ple_gate_b[i]),
                    row(ln_ffn_g[i]), row(ln_ffn_b[i]))
        j = i // 2
        if i % 2 == 0:
            x2 = _ffn(x1, p2d, ffn_w_gate[j], ffn_w_up[j], ffn_w_down[j], *ple_args, alpha)
        else:
            seg = min(MOE_SEGMENT, t)
            routing = _router(x1, w_router[j], b_router[j], seg)
            x2 = _moe(x1, routing, p2d, moe_w_gate[j], moe_w_up[j], moe_w_down[j], *ple_args, alpha, seg)
        x = x2.reshape(b, s, d)
    return x
```

```python
import functools

import numpy as np
import jax
import jax.numpy as jnp
from jax import lax
from jax.experimental import pallas as pl
from jax.experimental.pallas import tpu as pltpu

D_MODEL = 1024
HEAD_DIM = 64
DIL_PATTERNS = ((128, 1), (512, 4), (2048, 16))
N_DIL = 3
DIL_HEADS = 6
CONV_WIDTH = 384
CONV_K = 3
NSA_Q_HEADS = 6
NSA_KV_HEADS = 2
CMP_BLOCK = 32
CMP_STRIDE = 16
CMP_HIDDEN = 128
SEL_BLOCK = 64
N_SEL = 16
NSA_WINDOW = 512
SWA_Q_HEADS = 6
SWA_WINDOW = 128
BRANCH_WIDTH = 384
N_BRANCH = 4
N_EXPERTS = 8
LN_EPS = 1e-5
NEG_INF = -1e30
DIL_WIDTH = N_DIL * DIL_HEADS * HEAD_DIM
COLUMN_SIZES = (DIL_WIDTH, DIL_WIDTH, DIL_WIDTH, CONV_WIDTH, CONV_WIDTH, CONV_WIDTH,
                NSA_Q_HEADS * HEAD_DIM, 128, 128, 128, 128, 128, 128, 3 * NSA_Q_HEADS,
                SWA_Q_HEADS * HEAD_DIM, 128, 128)
COL_OFF = np.concatenate([[0], np.cumsum(COLUMN_SIZES)]).tolist()

LANES = 128
V7X_VMEM_BYTES = 64 * 1024 * 1024
MIB = 1024 * 1024

CDT = jnp.bfloat16
F32 = jnp.float32
QK_SCALE = HEAD_DIM ** -0.5
SUB_Q = 128
BAND_ROWS = 64
BIG = 1e30

_GQA_HEAD_ORDER = (0, 3, 1, 4, 2, 5)
GQA_COL_PERM = np.concatenate([np.arange(h * HEAD_DIM, (h + 1) * HEAD_DIM) for h in _GQA_HEAD_ORDER])
GQA_COL_HEAD = GQA_COL_PERM // HEAD_DIM


def _cparams(sem, vmem_mib):
    return pltpu.CompilerParams(dimension_semantics=sem, vmem_limit_bytes=int(vmem_mib * MIB))


def _nt_dot(a, b):
    return lax.dot_general(a, b, (((1,), (1,)), ((), ())), preferred_element_type=F32)


def _dot(a, b):
    return jnp.dot(a, b, preferred_element_type=F32)


def _layer_norm(r, g, b):
    mu = jnp.mean(r, axis=-1, keepdims=True)
    d = r - mu
    var = jnp.mean(d * d, axis=-1, keepdims=True)
    return d * lax.rsqrt(var + LN_EPS) * g + b


def _half_masks():
    lane = lax.broadcasted_iota(jnp.int32, (1, LANES), 1)
    return lane < HEAD_DIM


def _linear_kernel(x_ref, w_ref, *refs, splits, n_chunk, dils):
    o_refs, z_ref = refs[:len(splits)], refs[len(splits)]
    xb = x_ref[...].astype(CDT)
    tm = xb.shape[0]
    for o_ref, (c0, c1), dil in zip(o_refs, splits, dils):
        width = c1 - c0
        if dil == 1:
            for a in range(c0, c1, n_chunk):
                b = min(a + n_chunk, c1)
                o_ref[:, a - c0:b - c0] = _dot(xb, w_ref[:, a:b]).astype(o_ref.dtype)
        else:
            z = _dot(xb, w_ref[:, c0:c1])
            for cb in range(width // LANES):
                z_ref[cb] = z[:, cb * LANES:(cb + 1) * LANES]
            for r in range(dil):
                for cb in range(width // LANES):
                    o_ref[:, r * width + cb * LANES:r * width + (cb + 1) * LANES] = (
                        z_ref[cb, pl.ds(r, tm // dil, stride=dil), :].astype(o_ref.dtype))


def _linear(x2d, w, splits, n_chunk, name, dils=None, tm=512):
    t, k = x2d.shape
    tm = min(tm, t)
    n = w.shape[1]
    dils = tuple(dils) if dils is not None else (1,) * len(splits)
    widths = [c1 - c0 for c0, c1 in splits]
    assert all(tm % (16 * dl) == 0 for dl in dils)
    return pl.pallas_call(
        functools.partial(_linear_kernel, splits=tuple(splits), n_chunk=n_chunk, dils=dils),
        grid=(t // tm,),
        in_specs=[pl.BlockSpec((tm, k), lambda i: (i, 0)),
                  pl.BlockSpec((k, n), lambda i: (0, 0))],
        out_specs=[pl.BlockSpec((tm // dl, dl * wd), lambda i: (i, 0)) for wd, dl in zip(widths, dils)],
        out_shape=[jax.ShapeDtypeStruct((t // dl, dl * wd), CDT) for wd, dl in zip(widths, dils)],
        scratch_shapes=[pltpu.VMEM((max(widths) // LANES, tm, LANES), F32)],
        compiler_params=_cparams(("parallel",), 48),
        name=name,
    )(x2d, w)


def _linear_t_kernel(x_ref, wt_ref, o_ref):
    o_ref[...] = _nt_dot(wt_ref[...], x_ref[...].astype(CDT)).astype(o_ref.dtype)


def _linear_t(x2d, wt, tm, name):
    t, k = x2d.shape
    n = wt.shape[0]
    return pl.pallas_call(
        _linear_t_kernel,
        grid=(t // tm,),
        in_specs=[pl.BlockSpec((tm, k), lambda i: (i, 0)), pl.BlockSpec((n, k), lambda i: (0, 0))],
        out_specs=pl.BlockSpec((None, n, tm), lambda i: (i, 0, 0)),
        out_shape=jax.ShapeDtypeStruct((t // tm, n, tm), CDT),
        compiler_params=_cparams(("parallel",), 32),
        name=name,
    )(x2d, wt)


def _conv_gate_kernel(x_ref, xh_ref, wc_ref, cw_ref, wg_ref, ob_ref, g_ref, *, tm):
    i = pl.program_id(1)
    w = CONV_WIDTH
    xb = x_ref[...].astype(CDT)
    z = _dot(xb, wc_ref[...])
    u = z[:, w:2 * w] * z[:, 2 * w:3 * w]
    zh = _dot(xh_ref[...].astype(CDT), wc_ref[:, w:3 * w])
    uh = zh[:, :w] * zh[:, w:]
    uh = jnp.where(i == 0, 0.0, uh)
    row = lax.broadcasted_iota(jnp.int32, (tm, w), 0)
    u1 = jnp.where(row == 0, uh[7:8, :], pltpu.roll(u, 1, 0))
    u2 = jnp.where(row == 0, uh[6:7, :], jnp.where(row == 1, uh[7:8, :], pltpu.roll(u, 2, 0)))
    y = cw_ref[0:1, :] * u2 + cw_ref[1:2, :] * u1 + cw_ref[2:3, :] * u
    ob_ref[...] = (z[:, :w] * y).astype(ob_ref.dtype)
    g_ref[...] = jax.nn.sigmoid(_dot(xb, wg_ref[...]))


def _conv_gate(x, wc, conv_w, wg, tm=512):
    b, s, d = x.shape
    tm = min(tm, s)
    hb = tm // 8
    return pl.pallas_call(
        functools.partial(_conv_gate_kernel, tm=tm),
        grid=(b, s // tm),
        in_specs=[pl.BlockSpec((None, tm, d), lambda bi, i: (bi, i, 0)),
                  pl.BlockSpec((None, 8, d), lambda bi, i: (bi, jnp.maximum(i * hb - 1, 0), 0)),
                  pl.BlockSpec(wc.shape, lambda bi, i: (0, 0)),
                  pl.BlockSpec(conv_w.shape, lambda bi, i: (0, 0)),
                  pl.BlockSpec(wg.shape, lambda bi, i: (0, 0))],
        out_specs=[pl.BlockSpec((None, tm, CONV_WIDTH), lambda bi, i: (bi, i, 0)),
                   pl.BlockSpec((None, tm, wg.shape[1]), lambda bi, i: (bi, i, 0))],
        out_shape=[jax.ShapeDtypeStruct((b, s, CONV_WIDTH), CDT),
                   jax.ShapeDtypeStruct((b, s, wg.shape[1]), F32)],
        compiler_params=_cparams(("parallel", "parallel"), 48),
        name="conv_gate",
    )(x, x, wc, conv_w, wg)


def _banded_kernel(*refs, window, pr, tq, kw, want_lse, has_sink):
    q_ref, kp_ref, kc_ref, vp_ref, vc_ref = refs[:5]
    n = 5
    sink_ref = None
    if has_sink:
        sink_ref = refs[n]
        n += 1
    o_ref = refs[n]
    n += 1
    lse_ref = None
    if want_lse:
        lse_ref = refs[n]
        n += 1
    kbuf, vbuf, s_ref, e_ref, m_ref, l_ref = refs[n:n + 6]

    i = pl.program_id(2)
    kbuf[0:pr, :] = kp_ref[...]
    kbuf[pr:pr + tq, :] = kc_ref[...]
    vbuf[0:pr, :] = vp_ref[...]
    vbuf[pr:pr + tq, :] = vc_ref[...]

    span = SUB_Q + pr
    qi = lax.broadcasted_iota(jnp.int32, (SUB_Q, span), 0)
    kj = lax.broadcasted_iota(jnp.int32, (SUB_Q, span), 1)
    dist = pr + qi - kj
    band = (dist >= 0) & (dist <= window)
    lo = _half_masks()
    halves = (lo, jnp.logical_not(lo))
    groups = ((0,), (1,), (2,)) if kw == 3 * LANES else ((0, 1, 2),)
    rb = BAND_ROWS
    for sb in range(tq // SUB_Q):
        r0 = sb * SUB_Q
        bias = jnp.where(band & (i * tq + r0 - pr + kj >= 0), 0.0, NEG_INF)
        for grp in groups:
            kc0 = grp[0] * LANES if kw == 3 * LANES else 0
            qs = jnp.concatenate(
                [(jnp.where(hm, q_ref[r0:r0 + SUB_Q, p * LANES:(p + 1) * LANES], 0) * QK_SCALE).astype(CDT)
                 for p in grp for hm in halves], axis=0)
            g0 = 2 * grp[0] * SUB_Q
            s_ref[g0:g0 + qs.shape[0], :] = _nt_dot(qs, kbuf[r0:r0 + span, kc0:kc0 + LANES])
        for c0 in range(0, 6 * SUB_Q, rb):
            rows = slice(c0, c0 + rb)
            s = s_ref[rows, :] + bias[c0 % SUB_Q:c0 % SUB_Q + rb, :]
            m = jnp.max(s, axis=-1, keepdims=True)
            e = jnp.exp(s - m)
            e_ref[rows, :] = e.astype(CDT)
            m_ref[rows, :] = jnp.broadcast_to(m, (rb, LANES))
            l_ref[rows, :] = jnp.broadcast_to(jnp.sum(e, axis=-1, keepdims=True), (rb, LANES))
        for grp in groups:
            kc0 = grp[0] * LANES if kw == 3 * LANES else 0
            g0 = 2 * grp[0] * SUB_Q
            g1 = g0 + 2 * len(grp) * SUB_Q
            l = l_ref[g0:g1, :]
            o = _dot(e_ref[g0:g1, :], vbuf[r0:r0 + span, kc0:kc0 + LANES]) / l
            lse = m_ref[g0:g1, :] + jnp.log(l)
            for n_p, p in enumerate(grp):
                a = 2 * n_p * SUB_Q
                o_pair = jnp.where(lo, o[a:a + SUB_Q], o[a + SUB_Q:a + 2 * SUB_Q])
                lse_pair = jnp.where(lo, lse[a:a + SUB_Q], lse[a + SUB_Q:a + 2 * SUB_Q])
                if has_sink:
                    o_pair = o_pair * jax.nn.sigmoid(lse_pair - sink_ref[:, p * LANES:(p + 1) * LANES])
                o_ref[r0:r0 + SUB_Q, p * LANES:(p + 1) * LANES] = o_pair.astype(o_ref.dtype)
                if want_lse:
                    lse_ref[r0:r0 + SUB_Q, p * LANES:(p + 1) * LANES] = lse_pair


def _banded(qa, ka, va, *, nrep, qcol, kcol, vcol, kw, window, want_lse, sink_row=None, tq=512):
    b, l, _ = qa.shape
    pr = -(-window // SUB_Q) * SUB_Q
    tq = min(max(tq, pr), l)
    assert tq % pr == 0 and l % tq == 0, (tq, pr, l)
    ratio = tq // pr
    qw = 3 * LANES
    in_specs = [
        pl.BlockSpec((None, tq, qw), lambda bi, r, i: (bi, i, qcol(r))),
        pl.BlockSpec((None, pr, kw), lambda bi, r, i: (bi, jnp.maximum(i * ratio - 1, 0), kcol(r))),
        pl.BlockSpec((None, tq, kw), lambda bi, r, i: (bi, i, kcol(r))),
        pl.BlockSpec((None, pr, kw), lambda bi, r, i: (bi, jnp.maximum(i * ratio - 1, 0), vcol(r))),
        pl.BlockSpec((None, tq, kw), lambda bi, r, i: (bi, i, vcol(r))),
    ]
    args = [qa, ka, ka, va, va]
    if sink_row is not None:
        in_specs.append(pl.BlockSpec(sink_row.shape, lambda bi, r, i: (0, 0)))
        args.append(sink_row)
    out_specs = [pl.BlockSpec((None, tq, qw), lambda bi, r, i: (bi, i, r))]
    out_shape = [jax.ShapeDtypeStruct((b, l, nrep * qw), CDT)]
    if want_lse:
        out_specs.append(pl.BlockSpec((None, tq, qw), lambda bi, r, i: (bi, i, r)))
        out_shape.append(jax.ShapeDtypeStruct((b, l, nrep * qw), F32))
    res = pl.pallas_call(
        functools.partial(_banded_kernel, window=window, pr=pr, tq=tq, kw=kw, want_lse=want_lse,
                          has_sink=sink_row is not None),
        grid=(b, nrep, l // tq),
        in_specs=in_specs,
        out_specs=out_specs,
        out_shape=out_shape,
        scratch_shapes=[pltpu.VMEM((pr + tq, kw), ka.dtype), pltpu.VMEM((pr + tq, kw), va.dtype),
                        pltpu.VMEM((6 * SUB_Q, SUB_Q + pr), F32), pltpu.VMEM((6 * SUB_Q, SUB_Q + pr), CDT),
                        pltpu.VMEM((6 * SUB_Q, LANES), F32), pltpu.VMEM((6 * SUB_Q, LANES), F32)],
        compiler_params=_cparams(("parallel", "parallel", "parallel"), 32),
        name=f"banded_w{window}_k{kw}_r{nrep}",
    )(*args)
    return res


def _gelu_tanh(x):
    return 0.5 * x * (1.0 + jnp.tanh(0.7978845608028654 * (x + 0.044715 * (x * x * x))))


def _compress_kernel(x_ref, pa_ref, pb_ref, w1a_ref, w1b_ref, b1_ref, w2_ref, b2_ref, o_ref):
    x = x_ref[...].astype(F32)
    n = x.shape[0]
    a = _dot((x + pa_ref[...]).astype(CDT), w1a_ref[...])
    bm = _dot((x + pb_ref[...]).astype(CDT), w1b_ref[...])
    h = a + pltpu.roll(bm, n - 1, 0) + b1_ref[...]
    o_ref[...] = (_dot(_gelu_tanh(h).astype(CDT), w2_ref[...]) + b2_ref[...]).astype(o_ref.dtype)


def _compress(t, pos, w1, b1, w2, b2):
    b, s, _ = t.shape
    nch = s // CMP_STRIDE
    xw = CMP_STRIDE * LANES
    x = t.reshape(b, nch, xw)
    eye = jnp.eye(NSA_KV_HEADS, dtype=F32)
    w1r = w1.reshape(CMP_BLOCK, HEAD_DIM, CMP_HIDDEN)

    def expand_w1(part):
        return jnp.einsum('tdj,kl->tkdlj', part, eye).reshape(xw, NSA_KV_HEADS * CMP_HIDDEN).astype(CDT)

    def expand_pos(part):
        return jnp.broadcast_to(part[:, None, :], (CMP_STRIDE, NSA_KV_HEADS, HEAD_DIM)).reshape(1, xw)

    w1a, w1b = expand_w1(w1r[:CMP_STRIDE]), expand_w1(w1r[CMP_STRIDE:])
    pa, pb = expand_pos(pos[:CMP_STRIDE]), expand_pos(pos[CMP_STRIDE:])
    b1e = jnp.tile(b1, NSA_KV_HEADS).reshape(1, -1)
    w2e = jnp.einsum('jd,kl->kjld', w2, eye).reshape(NSA_KV_HEADS * CMP_HIDDEN, LANES).astype(CDT)
    b2e = jnp.tile(b2, NSA_KV_HEADS).reshape(1, -1)
    consts = [pa, pb, w1a, w1b, b1e, w2e, b2e]
    return pl.pallas_call(
        _compress_kernel,
        grid=(b,),
        in_specs=[pl.BlockSpec((None, nch, xw), lambda bi: (bi, 0, 0))]
        + [pl.BlockSpec(c.shape, lambda bi: (0, 0)) for c in consts],
        out_specs=pl.BlockSpec((None, nch, LANES), lambda bi: (bi, 0, 0)),
        out_shape=jax.ShapeDtypeStruct((b, nch, LANES), CDT),
        compiler_params=_cparams(("parallel",), 48),
        name="nsa_compress",
    )(x, *consts)


CMP_ROWS = 16


def _cmp_select_kernel(q_ref, kc_ref, vc_ref, ov_ref, o_ref, sel_ref, s_ref, p_ref, hi_ref, lo_ref,
                       *, tq, n_sel, tile0):
    i = pl.program_id(1) + tile0
    ncp = kc_ref.shape[0]
    ns = ov_ref.shape[1]
    lo = _half_masks()
    t_col = i * tq + lax.broadcasted_iota(jnp.int32, (tq, 1), 0)
    blk = lax.broadcasted_iota(jnp.int32, (tq, ns), 1)
    blk_t = lax.broadcasted_iota(jnp.int32, (ns, tq), 0)
    cur = t_col // SEL_BLOCK
    causal = blk <= cur
    forced = (blk == 0) | (blk == cur) | (blk == cur - 1)
    rb = CMP_ROWS
    c_end = lax.broadcasted_iota(jnp.int32, (rb, ncp), 1) * CMP_STRIDE + (CMP_BLOCK - 1)
    o_kv, work_t = [], []
    for kv, hm in enumerate((lo, jnp.logical_not(lo))):
        qs = jnp.concatenate(
            [(jnp.where(hm, q_ref[:, p * LANES:(p + 1) * LANES], 0) * QK_SCALE).astype(CDT) for p in range(3)], axis=0)
        s_ref[...] = _nt_dot(qs, kc_ref[...])
        for r0 in range(0, tq, rb):
            t_rows = i * tq + r0 + lax.broadcasted_iota(jnp.int32, (rb, 1), 0)
            vis_bias = jnp.where(c_end <= t_rows, 0.0, NEG_INF)
            has_visible = t_rows >= CMP_BLOCK - 1
            psum = jnp.zeros((rb, ncp), F32)
            for h in range(3):
                rows = slice(h * tq + r0, h * tq + r0 + rb)
                s = s_ref[rows, :] + vis_bias
                e = jnp.exp(s - jnp.max(s, axis=-1, keepdims=True))
                inv = jnp.where(has_visible, 1.0 / jnp.maximum(jnp.sum(e, axis=-1, keepdims=True), 1e-30), 0.0)
                pn = e * inv
                p_ref[rows, :] = pn.astype(CDT)
                psum = psum + pn
            p_hi = psum.astype(CDT)
            hi_ref[r0:r0 + rb, :] = p_hi
            lo_ref[r0:r0 + rb, :] = (psum - p_hi.astype(F32)).astype(CDT)
        o = _dot(p_ref[...], vc_ref[...])
        o_kv.append([o[p * tq:(p + 1) * tq] for p in range(3)])
        imp = _dot(hi_ref[...], ov_ref[...]) + _dot(lo_ref[...], ov_ref[...])
        work_t.append(jnp.where(causal & jnp.logical_not(forced), imp, -BIG).T)

    def pick(_, work):
        m = jnp.max(work, axis=0, keepdims=True)
        idx = jnp.min(jnp.where(work == m, blk_t, ns), axis=0, keepdims=True)
        return jnp.where(blk_t == idx, -2.0 * BIG, work)

    for kv, start in enumerate(work_t):
        done = lax.fori_loop(0, n_sel - 3, pick, start)
        taken = jnp.where((done < -BIG) & (start > -BIG), 1.0, 0.0).T
        selb = jnp.where(forced | (taken > 0.5), 0.0, NEG_INF)
        sel_ref[:, kv * ns:(kv + 1) * ns] = selb.astype(sel_ref.dtype)
    for p in range(3):
        o_ref[:, p * LANES:(p + 1) * LANES] = jnp.where(lo, o_kv[0][p], o_kv[1][p]).astype(o_ref.dtype)


CMP_CAUSAL_SPLITS = 4


def _cmp_select(zr, qcol, kc, vc, tq=128):
    b, s, _ = zr.shape
    ncp = kc.shape[1]
    ns = s // SEL_BLOCK
    n_sel = min(N_SEL, ns)
    assert n_sel >= 3, "selection needs room for the three forced blocks"
    tq = min(tq, s)
    c = np.arange(ncp)[:, None] * CMP_STRIDE
    j = np.arange(ns)[None, :] * SEL_BLOCK
    overlap = ((c < j + SEL_BLOCK) & (c + CMP_BLOCK - 1 >= j)).astype(np.float32)
    overlap[ncp - 1:, :] = 0.0
    ov = jnp.asarray(overlap, CDT)
    qw = 3 * LANES
    n_split = CMP_CAUSAL_SPLITS if (s // tq) % CMP_CAUSAL_SPLITS == 0 and ncp % (16 * CMP_CAUSAL_SPLITS) == 0 else 1
    tiles = s // tq // n_split
    outs, sels = [], []
    for part in range(n_split):
        tile0 = part * tiles
        ncp_part = ncp * (part + 1) // n_split
        o_part, sel_part = pl.pallas_call(
            functools.partial(_cmp_select_kernel, tq=tq, n_sel=n_sel, tile0=tile0),
            grid=(b, tiles),
            in_specs=[pl.BlockSpec((None, tq, qw), lambda bi, i, tile0=tile0: (bi, i + tile0, qcol)),
                      pl.BlockSpec((None, ncp_part, LANES), lambda bi, i: (bi, 0, 0)),
                      pl.BlockSpec((None, ncp_part, LANES), lambda bi, i: (bi, 0, 0)),
                      pl.BlockSpec((ncp_part, ns), lambda bi, i: (0, 0))],
            out_specs=[pl.BlockSpec((None, tq, qw), lambda bi, i: (bi, i, 0)),
                       pl.BlockSpec((None, tq, 2 * ns), lambda bi, i: (bi, i, 0))],
            out_shape=[jax.ShapeDtypeStruct((b, tiles * tq, qw), CDT),
                       jax.ShapeDtypeStruct((b, tiles * tq, 2 * ns), CDT)],
            scratch_shapes=[pltpu.VMEM((3 * tq, ncp_part), F32), pltpu.VMEM((3 * tq, ncp_part), CDT),
                            pltpu.VMEM((tq, ncp_part), CDT), pltpu.VMEM((tq, ncp_part), CDT)],
            compiler_params=_cparams(("parallel", "parallel"), 48),
            name=f"nsa_cmp_select_p{part}",
        )(zr, kc, vc, ov)
        outs.append(o_part)
        sels.append(sel_part)
    return jnp.concatenate(outs, axis=1), jnp.concatenate(sels, axis=1)


SLC_KEY_CHUNK = 256


def _slc_kernel(q_ref, k0_ref, k1_ref, vt_ref, selb_ref, o_ref, s_ref, e_ref, m_ref, l_ref, a_ref, acc_ref,
                *, tq, tk):
    i = pl.program_id(1)
    ns = selb_ref.shape[1] // 2
    bpt = tk // SEL_BLOCK
    n_q = 3 * tq
    rc = min(SLC_KEY_CHUNK, tk)
    lo = _half_masks()
    n_tiles = ((i + 1) * tq + tk - 1) // tk
    p_row = lax.broadcasted_iota(jnp.int32, (ns, LANES), 0)
    p_col = lax.broadcasted_iota(jnp.int32, (ns, LANES), 1)
    groups = []
    for kv, (hm, k_ref) in enumerate(((lo, k0_ref), (jnp.logical_not(lo), k1_ref))):
        q3 = [(jnp.where(hm, q_ref[:, p * LANES:(p + 1) * LANES], 0) * QK_SCALE).astype(CDT) for p in range(3)]
        selb = selb_ref[:, kv * ns:(kv + 1) * ns]
        lane0 = HEAD_DIM if kv == 0 else 0
        groups.append((kv, hm, k_ref, q3, selb, lane0))
    m_ref[...] = jnp.full(m_ref.shape, NEG_INF, F32)
    l_ref[...] = jnp.zeros(l_ref.shape, F32)
    acc_ref[...] = jnp.zeros(acc_ref.shape, F32)

    def scores(j):
        k0 = pl.multiple_of(j * tk, tk)
        for kv, hm, k_ref, q3, selb, lane0 in groups:
            place = ((p_col >= lane0) & (p_col < lane0 + bpt) & (p_row == p_col - lane0 + j * bpt)).astype(CDT)
            sb = _dot(selb, place).astype(CDT)
            qp = jnp.concatenate([jnp.where(hm, q, sb) for q in q3], axis=0)
            s_ref[kv] = _nt_dot(k_ref[pl.ds(k0, tk), :], qp)

    def softmax_pv(j, diagonal):
        k0 = pl.multiple_of(j * tk, tk)
        for kv in range(2):
            for c0 in range(0, n_q, LANES):
                cols = slice(c0, c0 + LANES)
                t_lane = i * tq + (c0 % tq) + lax.broadcasted_iota(jnp.int32, (1, LANES), 1)

                def chunk(r0):
                    s = s_ref[kv, r0:r0 + rc, cols]
                    if diagonal:
                        kpos = k0 + r0 + lax.broadcasted_iota(jnp.int32, (rc, LANES), 0)
                        s = jnp.where(kpos <= t_lane, s, NEG_INF)
                    return s

                m8 = m_ref[kv, :, cols]
                for r0 in range(0, tk, rc):
                    m8 = jnp.maximum(m8, jnp.max(chunk(r0).reshape(rc // 8, 8, LANES), axis=0))
                m_new = jnp.max(m8, axis=0, keepdims=True)
                alpha = jnp.exp(m_ref[kv, :, cols] - m_new)
                l8 = jnp.zeros((8, LANES), F32)
                for r0 in range(0, tk, rc):
                    e = jnp.exp(chunk(r0) - m_new)
                    l8 = l8 + jnp.sum(e.reshape(rc // 8, 8, LANES), axis=0)
                    e_ref[kv, r0:r0 + rc, cols] = e.astype(CDT)
                l_ref[kv, :, cols] = alpha * l_ref[kv, :, cols] + jnp.sum(l8, axis=0, keepdims=True)
                m_ref[kv, :, cols] = jnp.broadcast_to(m_new, (8, LANES))
                a_ref[kv, :, cols] = alpha
        for kv in range(2):
            acc_ref[kv] = a_ref[kv, 0:1, :] * acc_ref[kv] + _dot(vt_ref[j], e_ref[kv])

    def tile(j, diagonal):
        scores(j)
        softmax_pv(j, diagonal)

    lax.fori_loop(0, n_tiles - 1, lambda j, c: (tile(j, False), c)[1], 0)
    tile(n_tiles - 1, True)
    o0 = (acc_ref[0] / l_ref[0, 0:1, :]).T
    o1 = (acc_ref[1] / l_ref[1, 0:1, :]).T
    for p in range(3):
        o_ref[:, p * LANES:(p + 1) * LANES] = jnp.where(
            lo, o0[p * tq:(p + 1) * tq], o1[p * tq:(p + 1) * tq]).astype(o_ref.dtype)


def _slc(zr, qcol, ksc, vt, selb, tq=512):
    b, s, _ = zr.shape
    tq = min(tq, s)
    tk = vt.shape[2]
    assert tq % LANES == 0 and tk // SEL_BLOCK <= HEAD_DIM and s % tk == 0
    qw = 3 * LANES
    ns2 = selb.shape[2]
    pat = jax.nn.one_hot((jnp.arange(s) // SEL_BLOCK) % (tk // SEL_BLOCK), HEAD_DIM, dtype=ksc.dtype)
    pat = jnp.broadcast_to(pat[None], (b, s, HEAD_DIM))
    k0 = jnp.concatenate([ksc[..., :HEAD_DIM], pat], axis=-1)
    k1 = jnp.concatenate([pat, ksc[..., HEAD_DIM:]], axis=-1)
    full = lambda bi, i: (bi, 0, 0)
    return pl.pallas_call(
        functools.partial(_slc_kernel, tq=tq, tk=tk),
        grid=(b, s // tq),
        in_specs=[pl.BlockSpec((None, tq, qw), lambda bi, i: (bi, i, qcol)),
                  _single_buffered((None, s, LANES), full),
                  _single_buffered((None, s, LANES), full),
                  _single_buffered((s // tk, LANES, tk), lambda bi, i: (bi, 0, 0)),
                  pl.BlockSpec((None, tq, ns2), lambda bi, i: (bi, i, 0))],
        out_specs=pl.BlockSpec((None, tq, qw), lambda bi, i: (bi, i, 0)),
        out_shape=jax.ShapeDtypeStruct((b, s, qw), CDT),
        scratch_shapes=[pltpu.VMEM((2, tk, 3 * tq), F32), pltpu.VMEM((2, tk, 3 * tq), CDT),
                        pltpu.VMEM((2, 8, 3 * tq), F32), pltpu.VMEM((2, 8, 3 * tq), F32),
                        pltpu.VMEM((2, 8, 3 * tq), F32), pltpu.VMEM((2, LANES, 3 * tq), F32)],
        compiler_params=_cparams(("parallel", "arbitrary"), 56),
        name="nsa_slc",
    )(zr, k0, k1, vt, selb)


def _merge_kernel(x_ref, oa0, oa1, oa2, la0, la1, la2, ob, ocmp, oslc, owin, gates, od,
                  wg_ref, bg_ref, wb_ref, wo_ref, g_ref, b_ref, o_ref, *scratch, alpha):
    x = x_ref[...]
    xb = x.astype(CDT)
    bw = BRANCH_WIDTH

    def token_rows(src_ref, dst_ref):
        dil = src_ref.shape[1] // bw
        if dil == 1:
            return src_ref[...].astype(F32)
        n_cb = bw // LANES
        for r in range(dil):
            for cb in range(n_cb):
                c0 = r * bw + cb * LANES
                dst_ref[cb, pl.ds(r, src_ref.shape[0], stride=dil), :] = src_ref[:, c0:c0 + LANES].astype(F32)
        return jnp.concatenate([dst_ref[cb] for cb in range(n_cb)], axis=1)

    o0, o1, o2 = (token_rows(s, d) for s, d in zip((oa0, oa1, oa2), scratch[0:3]))
    l0, l1, l2 = (token_rows(s, d) for s, d in zip((la0, la1, la2), scratch[3:6]))
    mx = jnp.maximum(jnp.maximum(l0, l1), l2)
    w0, w1, w2 = jnp.exp(l0 - mx), jnp.exp(l1 - mx), jnp.exp(l2 - mx)
    o_a = (w0 * o0 + w1 * o1 + w2 * o2) / (w0 + w1 + w2)
    o_c = (gates[:, 0:bw] * ocmp[...].astype(F32) + gates[:, bw:2 * bw] * oslc[...].astype(F32)
           + gates[:, 2 * bw:3 * bw] * owin[...].astype(F32))
    branches = (o_a.astype(CDT), ob[...], o_c.astype(CDT), od[...])
    d = x.shape[1]
    merged = jnp.zeros(x.shape, F32)
    for m in range(N_BRANCH):
        gate = jax.nn.sigmoid(_dot(xb, wg_ref[:, m * d:(m + 1) * d]) + bg_ref[:, m * d:(m + 1) * d])
        merged = merged + gate * _dot(branches[m], wb_ref[m])
    r = alpha * x + _dot(merged.astype(CDT), wo_ref[...])
    o_ref[...] = _layer_norm(r, g_ref[...], b_ref[...])


def _merge(x2d, branch_inputs, wg, bg, wb, wo, g, b, alpha, tm=256):
    t, d = x2d.shape
    tm = min(tm, t)
    row = lambda i: (i, 0)
    const2 = lambda i: (0, 0)
    in_specs = [pl.BlockSpec((tm, d), row)]
    in_specs += [pl.BlockSpec((tm * a.shape[0] // t, a.shape[1]), row) for a in branch_inputs]
    in_specs += [pl.BlockSpec(wg.shape, const2), pl.BlockSpec(bg.shape, const2),
                 pl.BlockSpec(wb.shape, lambda i: (0, 0, 0)), pl.BlockSpec(wo.shape, const2),
                 pl.BlockSpec(g.shape, const2), pl.BlockSpec(b.shape, const2)]
    return pl.pallas_call(
        functools.partial(_merge_kernel, alpha=alpha),
        grid=(t // tm,),
        in_specs=in_specs,
        out_specs=pl.BlockSpec((tm, d), row),
        out_shape=jax.ShapeDtypeStruct((t, d), F32),
        scratch_shapes=[pltpu.VMEM((BRANCH_WIDTH // LANES, tm, LANES), F32) for _ in range(2 * N_DIL)],
        compiler_params=_cparams(("parallel",), 56),
        name="merge_ln",
    )(x2d, *branch_inputs, wg, bg, wb, wo, g, b)


def _ple_ln(x, xb, f, p, plw_ref, pgw_ref, pgb_ref, g_ref, b_ref, alpha):
    ple = jax.nn.sigmoid(_dot(xb, pgw_ref[...]) + pgb_ref[...]) * _dot(p.astype(CDT), plw_ref[...])
    return _layer_norm(alpha * x + f + ple, g_ref[...], b_ref[...])


FFN_CHUNK = 512


def _ffn_kernel(x_ref, p_ref, wg_ref, wu_ref, wd_ref, plw_ref, pgw_ref, pgb_ref, g_ref, b_ref, o_ref, h_ref, *, alpha):
    x = x_ref[...]
    xb = x.astype(CDT)
    dff = wg_ref.shape[1]
    for c0 in range(0, dff, FFN_CHUNK):
        cols = slice(c0, min(c0 + FFN_CHUNK, dff))
        h_ref[:, cols] = (jax.nn.silu(_dot(xb, wg_ref[:, cols])) * _dot(xb, wu_ref[:, cols])).astype(CDT)
    f = _dot(h_ref[...], wd_ref[...])
    o_ref[...] = _ple_ln(x, xb, f, p_ref[...], plw_ref, pgw_ref, pgb_ref, g_ref, b_ref, alpha)


def _ffn(x2d, p2d, wg, wu, wd, plw, pgw, pgb, g, b, alpha, tm=512):
    t, d = x2d.shape
    tm = min(tm, t)
    dff = wg.shape[1]
    wg, wu, wd = wg.astype(CDT), wu.astype(CDT), wd.astype(CDT)
    row = lambda i: (i, 0)
    const = lambda shape: pl.BlockSpec(shape, lambda i: (0, 0), pipeline_mode=pl.Buffered(1))
    return pl.pallas_call(
        functools.partial(_ffn_kernel, alpha=alpha),
        grid=(t // tm,),
        in_specs=[pl.BlockSpec((tm, d), row), pl.BlockSpec((tm, p2d.shape[1]), row),
                  const(wg.shape), const(wu.shape), const(wd.shape),
                  const(plw.shape), const(pgw.shape), const(pgb.shape), const(g.shape), const(b.shape)],
        out_specs=pl.BlockSpec((tm, d), row),
        out_shape=jax.ShapeDtypeStruct((t, d), F32),
        scratch_shapes=[pltpu.VMEM((tm, dff), CDT)],
        compiler_params=_cparams(("parallel",), 48),
        name="ffn_ple_ln",
    )(x2d, p2d, wg, wu, wd, plw, pgw, pgb, g, b)


def _router_kernel(x_ref, wh_ref, wl_ref, b_ref, comb_ref, rank_ref, rank_t_ref, cnt_ref):
    x = x_ref[...]
    xh = x.astype(CDT)
    xl = (x - xh.astype(F32)).astype(CDT)
    logits = _dot(xh, wh_ref[...]) + _dot(xh, wl_ref[...]) + _dot(xl, wh_ref[...]) + b_ref[...]
    lane = lax.broadcasted_iota(jnp.int32, logits.shape, 1)
    v1 = jnp.max(logits, axis=-1, keepdims=True)
    i1 = jnp.min(jnp.where(logits == v1, lane, LANES), axis=-1, keepdims=True)
    rest = jnp.where(lane == i1, -jnp.inf, logits)
    v2 = jnp.max(rest, axis=-1, keepdims=True)
    i2 = jnp.min(jnp.where(rest == v2, lane, LANES), axis=-1, keepdims=True)
    e2 = jnp.exp(v2 - v1)
    comb_ref[...] = jnp.where(lane == i1, 1.0 / (1.0 + e2), 0.0) + jnp.where(lane == i2, e2 / (1.0 + e2), 0.0)
    routed = (lane == i1) | (lane == i2)
    mask = routed.astype(CDT)
    tm = x.shape[0]
    before = (lax.broadcasted_iota(jnp.int32, (tm, tm), 1) < lax.broadcasted_iota(jnp.int32, (tm, tm), 0)).astype(CDT)
    rank = jnp.where(routed, _dot(before, mask), -1.0)
    rank_ref[...] = rank
    rank_t_ref[...] = rank.T[0:rank_t_ref.shape[0], :]
    cnt_ref[...] = jnp.sum(routed.astype(F32), axis=0, keepdims=True).astype(jnp.int32)


def _router(x2d, w_router, b_router, tm):
    t, d = x2d.shape
    ne = w_router.shape[1]
    wp = jnp.zeros((d, LANES), F32).at[:, :ne].set(w_router)
    wh = wp.astype(CDT)
    wl = (wp - wh.astype(F32)).astype(CDT)
    bp = jnp.full((1, LANES), -BIG, F32).at[0, :ne].set(b_router)
    nt = t // tm
    row = lambda i: (i, 0)
    return pl.pallas_call(
        _router_kernel,
        grid=(nt,),
        in_specs=[pl.BlockSpec((tm, d), row), pl.BlockSpec(wh.shape, lambda i: (0, 0)),
                  pl.BlockSpec(wl.shape, lambda i: (0, 0)), pl.BlockSpec(bp.shape, lambda i: (0, 0))],
        out_specs=[pl.BlockSpec((tm, LANES), row), pl.BlockSpec((tm, LANES), row),
                   pl.BlockSpec((None, 8, tm), lambda i: (i, 0, 0)),
                   pl.BlockSpec((None, 1, LANES), lambda i: (i, 0, 0))],
        out_shape=[jax.ShapeDtypeStruct((t, LANES), F32), jax.ShapeDtypeStruct((t, LANES), F32),
                   jax.ShapeDtypeStruct((nt, 8, tm), F32), jax.ShapeDtypeStruct((nt, 1, LANES), jnp.int32)],
        compiler_params=_cparams(("parallel",), 40),
        name="moe_router",
    )(x2d, wh, wl, bp)


def _moe_kernel(cnt_ref, x_ref, comb_ref, rank_ref, rank_t_ref, p_ref, wg_ref, wu_ref, wd_ref, plw_ref, pgw_ref,
                pgb_ref, g_ref, b_ref, o_ref, xe_ref, ye_ref, *, alpha, rs, seg):
    i = pl.program_id(0)
    e = pl.program_id(1)
    c = pl.program_id(2)
    n_seg = x_ref.shape[0] // seg
    last_chunk = c == pl.num_programs(2) - 1

    def rows(sc):
        return pl.ds(pl.multiple_of(sc * rs, 8), rs)

    @pl.when((e == 0) & (c == 0))
    def _():
        o_ref[...] = jnp.zeros_like(o_ref)

    for sg in range(n_seg):
        tok = slice(sg * seg, (sg + 1) * seg)
        n_groups = (cnt_ref[(i * n_seg + sg) * LANES + e] + rs - 1) // rs

        @pl.when(c == 0)
        def _():
            xb = x_ref[tok, :].astype(CDT)
            rank_row = rank_t_ref[sg, pl.ds(e, 1), :]
            row_id = lax.broadcasted_iota(jnp.int32, (rs, seg), 0).astype(F32)

            def gather(sc, _):
                onehot = (rank_row - (sc * rs).astype(F32) == row_id).astype(CDT)
                xe_ref[sg, rows(sc), :] = _dot(onehot, xb).astype(CDT)
                ye_ref[sg, rows(sc), :] = jnp.zeros((rs, ye_ref.shape[2]), F32)
                return 0

            lax.fori_loop(0, n_groups, gather, 0)

        def expert(sc, _):
            xs = xe_ref[sg, rows(sc), :]
            h = jax.nn.silu(_dot(xs, wg_ref[0])) * _dot(xs, wu_ref[0])
            ye_ref[sg, rows(sc), :] += _dot(h.astype(CDT), wd_ref[0])
            return 0

        lax.fori_loop(0, n_groups, expert, 0)

        @pl.when(last_chunk)
        def _():
            lane = lax.broadcasted_iota(jnp.int32, (seg, LANES), 1)
            mine = lane == e
            cw = jnp.sum(jnp.where(mine, comb_ref[tok, :], 0.0), axis=-1, keepdims=True)
            rank_col = jnp.sum(jnp.where(mine, rank_ref[tok, :], 0.0), axis=-1, keepdims=True)
            col_id = lax.broadcasted_iota(jnp.int32, (seg, rs), 1).astype(F32)

            def scatter(sc, _):
                onehot = (rank_col - (sc * rs).astype(F32) == col_id).astype(CDT)
                o_ref[tok, :] += cw * _dot(onehot, ye_ref[sg, rows(sc), :].astype(CDT))
                return 0

            lax.fori_loop(0, n_groups, scatter, 0)

    @pl.when((e == pl.num_programs(1) - 1) & last_chunk)
    def _():
        for sg in range(n_seg):
            tok = slice(sg * seg, (sg + 1) * seg)
            x = x_ref[tok, :]
            o_ref[tok, :] = _ple_ln(x, x.astype(CDT), o_ref[tok, :], p_ref[tok, :], plw_ref, pgw_ref, pgb_ref,
                                    g_ref, b_ref, alpha)


MOE_CHUNK = 512
MOE_SEGMENT = 1024
MOE_SEGMENTS_PER_TILE = 2
MOE_ROW_GROUP = 288


def _single_buffered(shape, index_map):
    return pl.BlockSpec(shape, index_map, pipeline_mode=pl.Buffered(1))


def _moe(x2d, routing, p2d, wg, wu, wd, plw, pgw, pgb, g, b, alpha, seg):
    comb, rank, rank_t, cnt = routing
    t, d = x2d.shape
    ne, _, dff = wg.shape
    ck = min(MOE_CHUNK, dff)
    rs = min(MOE_ROW_GROUP, seg)
    n_seg = min(MOE_SEGMENTS_PER_TILE, t // seg)
    tm = n_seg * seg
    max_rows = -(-seg // rs) * rs
    wg, wu, wd = wg.astype(CDT), wu.astype(CDT), wd.astype(CDT)
    row = lambda i, e, c, cnt: (i, 0)
    c2 = lambda i, e, c, cnt: (0, 0)
    grid_spec = pltpu.PrefetchScalarGridSpec(
        num_scalar_prefetch=1,
        grid=(t // tm, ne, dff // ck),
        in_specs=[_single_buffered((tm, d), row), _single_buffered((tm, LANES), row),
                  _single_buffered((tm, LANES), row),
                  _single_buffered((n_seg, 8, seg), lambda i, e, c, cnt: (i, 0, 0)),
                  _single_buffered((tm, p2d.shape[1]), row),
                  pl.BlockSpec((1, d, ck), lambda i, e, c, cnt: (e, 0, c)),
                  pl.BlockSpec((1, d, ck), lambda i, e, c, cnt: (e, 0, c)),
                  pl.BlockSpec((1, ck, d), lambda i, e, c, cnt: (e, c, 0)),
                  _single_buffered(plw.shape, c2), _single_buffered(pgw.shape, c2), _single_buffered(pgb.shape, c2),
                  _single_buffered(g.shape, c2), _single_buffered(b.shape, c2)],
        out_specs=pl.BlockSpec((tm, d), row),
        scratch_shapes=[pltpu.VMEM((n_seg, max_rows, d), CDT), pltpu.VMEM((n_seg, max_rows, d), F32)],
    )
    return pl.pallas_call(
        functools.partial(_moe_kernel, alpha=alpha, rs=rs, seg=seg),
        grid_spec=grid_spec,
        out_shape=jax.ShapeDtypeStruct((t, d), F32),
        compiler_params=_cparams(("parallel", "arbitrary", "arbitrary"), 60),
        name="moe_ple_ln",
    )(cnt.reshape(-1), x2d, comb, rank, rank_t, p2d, wg, wu, wd, plw, pgw, pgb, g, b)


def _prep_in_weights(w_in):
    o = COL_OFF
    bw = BRANCH_WIDTH
    cols = lambda n: w_in[:, o[n]:o[n + 1]]
    qa, ka, va = cols(0), cols(1), cols(2)
    w_dil = jnp.concatenate(
        [t[:, g * bw:(g + 1) * bw] for g in range(N_DIL) for t in (qa, ka, va)], axis=1).astype(CDT)
    w_conv = jnp.concatenate([cols(3), cols(4), cols(5)], axis=1).astype(CDT)
    gn = cols(13)
    w_gate = jnp.concatenate([gn[:, br * NSA_Q_HEADS + GQA_COL_HEAD] for br in range(3)], axis=1).astype(CDT)
    w_rest = jnp.concatenate([cols(6)[:, GQA_COL_PERM], cols(14)[:, GQA_COL_PERM], cols(11),
                              cols(12), cols(15), cols(16), cols(9), cols(7), cols(8)], axis=1).astype(CDT)
    wt_vsc = cols(10).T.astype(CDT)
    return w_dil, w_conv, w_gate, w_rest, wt_vsc


ZR_Q_NSA, ZR_Q_SWA = 0, 1
ZR_KWC, ZR_VWC, ZR_KD, ZR_VD = 6, 7, 8, 9
ZR_WIDTH = 2 * BRANCH_WIDTH + 4 * LANES
SLC_KEY_TILE = 1024


def _token_mixers(x, w_in, conv_w, cmp_pos, cmp_w1, cmp_b1, cmp_w2, cmp_b2, sinks):
    b, s, d = x.shape
    x2d = x.reshape(b * s, d)
    w_dil, w_conv, w_gate, w_rest, wt_vsc = _prep_in_weights(w_in)
    gw = 3 * BRANCH_WIDTH

    z_dil = _linear(x2d, w_dil, [(g * gw, (g + 1) * gw) for g in range(N_DIL)], gw, "in_proj_dil",
                    dils=[dil for _, dil in DIL_PATTERNS])
    zr, ksc, kcc, vcc = _linear(x2d, w_rest, [(0, ZR_WIDTH)] + [(ZR_WIDTH + n * LANES, ZR_WIDTH + (n + 1) * LANES)
                                                           for n in range(3)], 256, "in_proj_rest")
    zr = zr.reshape(b, s, ZR_WIDTH)
    o_b, gates = _conv_gate(x, w_conv, conv_w, w_gate)

    dil_o, dil_lse = [], []
    for g, (window, dil) in enumerate(DIL_PATTERNS):
        view = z_dil[g].reshape(b, s // dil, dil * gw)
        og, lg = _banded(view, view, view, nrep=dil,
                         qcol=lambda r: 3 * r, kcol=lambda r: 3 * r + 1, vcol=lambda r: 3 * r + 2,
                         kw=3 * LANES, window=window // dil, want_lse=True)
        dil_o.append(og.reshape(b * s // dil, dil * BRANCH_WIDTH))
        dil_lse.append(lg.reshape(b * s // dil, dil * BRANCH_WIDTH))

    kc = _compress(kcc.reshape(b, s, LANES), cmp_pos[0], cmp_w1[0], cmp_b1[0], cmp_w2[0], cmp_b2[0])
    vc = _compress(vcc.reshape(b, s, LANES), cmp_pos[1], cmp_w1[1], cmp_b1[1], cmp_w2[1], cmp_b2[1])
    o_cmp, selb = _cmp_select(zr, ZR_Q_NSA, kc, vc)
    vsc_t = _linear_t(x2d, wt_vsc, min(SLC_KEY_TILE, s), "in_proj_vsc_t")
    o_slc = _slc(zr, ZR_Q_NSA, ksc.reshape(b, s, LANES), vsc_t, selb)
    (o_win,) = _banded(zr, zr, zr, nrep=1, qcol=lambda r: ZR_Q_NSA, kcol=lambda r: ZR_KWC, vcol=lambda r: ZR_VWC,
                       kw=LANES, window=NSA_WINDOW - 1, want_lse=False, tq=512)

    sink_row = sinks.astype(F32)[GQA_COL_HEAD].reshape(1, BRANCH_WIDTH)
    (o_d,) = _banded(zr, zr, zr, nrep=1, qcol=lambda r: ZR_Q_SWA, kcol=lambda r: ZR_KD, vcol=lambda r: ZR_VD,
                     kw=LANES, window=SWA_WINDOW - 1, want_lse=False, sink_row=sink_row)

    t = b * s
    flat = lambda a: a.reshape(t, a.shape[-1])
    return [dil_o[0], dil_o[1], dil_o[2], dil_lse[0], dil_lse[1], dil_lse[2], flat(o_b), flat(o_cmp), flat(o_slc),
            flat(o_win), flat(gates), flat(o_d)]


def kernel(x, p, w_in, conv_w, cmp_pos, cmp_w1, cmp_b1, cmp_w2, cmp_b2, sinks, w_branch, w_merge_gate, b_merge_gate, w_out, ln_mix_g, ln_mix_b, ffn_w_gate, ffn_w_up, ffn_w_down, w_router, b_router, moe_w_gate, moe_w_up, moe_w_down, ple_w, ple_gate_w, ple_gate_b, ln_ffn_g, ln_ffn_b):
    depth, b, s, _ = p.shape
    d = x.shape[-1]
    t = b * s
    alpha = (2 * depth) ** 0.25
    row = lambda v: v.reshape(1, -1).astype(F32)
    for i in range(depth):
        branch_inputs = _token_mixers(x, w_in[i], conv_w[i], cmp_pos[i], cmp_w1[i], cmp_b1[i], cmp_w2[i],
                                      cmp_b2[i], sinks[i])
        wg = jnp.concatenate([w_merge_gate[i, m] for m in range(N_BRANCH)], axis=1).astype(CDT)
        bg = b_merge_gate[i].reshape(1, N_BRANCH * d).astype(F32)
        wb = jnp.stack([w_branch[i, 0], w_branch[i, 1], w_branch[i, 2][GQA_COL_PERM],
                        w_branch[i, 3][GQA_COL_PERM]]).astype(CDT)
        x1 = _merge(x.reshape(t, d), branch_inputs, wg, bg, wb, w_out[i].astype(CDT),
                    row(ln_mix_g[i]), row(ln_mix_b[i]), alpha)
        p2d = p[i].reshape(t, -1)
        ple_args = (ple_w[i].astype(CDT), ple_gate_w[i].astype(CDT), row(ple_gate_b[i]),
                    row(ln_ffn_g[i]), row(ln_ffn_b[i]))
        j = i // 2
        if i % 2 == 0:
            x2 = _ffn(x1, p2d, ffn_w_gate[j], ffn_w_up[j], ffn_w_down[j], *ple_args, alpha)
        else:
            seg = min(MOE_SEGMENT, t)
            routing = _router(x1, w_router[j], b_router[j], seg)
            x2 = _moe(x1, routing, p2d, moe_w_gate[j], moe_w_up[j], moe_w_down[j], *ple_args, alpha, seg)
        x = x2.reshape(b, s, d)
    return x
```

```python
import functools

import numpy as np
import jax
import jax.numpy as jnp
from jax import lax
from jax.experimental import pallas as pl
from jax.experimental.pallas import tpu as pltpu

D_MODEL = 1024
HEAD_DIM = 64
DIL_PATTERNS = ((128, 1), (512, 4), (2048, 16))
N_DIL = 3
DIL_HEADS = 6
CONV_WIDTH = 384
CONV_K = 3
NSA_Q_HEADS = 6
NSA_KV_HEADS = 2
CMP_BLOCK = 32
CMP_STRIDE = 16
CMP_HIDDEN = 128
SEL_BLOCK = 64
N_SEL = 16
NSA_WINDOW = 512
SWA_Q_HEADS = 6
SWA_WINDOW = 128
BRANCH_WIDTH = 384
N_BRANCH = 4
N_EXPERTS = 8
LN_EPS = 1e-5
NEG_INF = -1e30
DIL_WIDTH = N_DIL * DIL_HEADS * HEAD_DIM
COLUMN_SIZES = (DIL_WIDTH, DIL_WIDTH, DIL_WIDTH, CONV_WIDTH, CONV_WIDTH, CONV_WIDTH,
                NSA_Q_HEADS * HEAD_DIM, 128, 128, 128, 128, 128, 128, 3 * NSA_Q_HEADS,
                SWA_Q_HEADS * HEAD_DIM, 128, 128)
COL_OFF = np.concatenate([[0], np.cumsum(COLUMN_SIZES)]).tolist()

LANES = 128
V7X_VMEM_BYTES = 64 * 1024 * 1024
MIB = 1024 * 1024

CDT = jnp.bfloat16
F32 = jnp.float32
QK_SCALE = HEAD_DIM ** -0.5
SUB_Q = 128
BAND_ROWS = 64
BIG = 1e30

_GQA_HEAD_ORDER = (0, 3, 1, 4, 2, 5)
GQA_COL_PERM = np.concatenate([np.arange(h * HEAD_DIM, (h + 1) * HEAD_DIM) for h in _GQA_HEAD_ORDER])
GQA_COL_HEAD = GQA_COL_PERM // HEAD_DIM


def _cparams(sem, vmem_mib):
    return pltpu.CompilerParams(dimension_semantics=sem, vmem_limit_bytes=int(vmem_mib * MIB))


def _nt_dot(a, b):
    return lax.dot_general(a, b, (((1,), (1,)), ((), ())), preferred_element_type=F32)


def _dot(a, b):
    return jnp.dot(a, b, preferred_element_type=F32)


def _layer_norm(r, g, b):
    mu = jnp.mean(r, axis=-1, keepdims=True)
    d = r - mu
    var = jnp.mean(d * d, axis=-1, keepdims=True)
    return d * lax.rsqrt(var + LN_EPS) * g + b


def _half_masks():
    lane = lax.broadcasted_iota(jnp.int32, (1, LANES), 1)
    return lane < HEAD_DIM


def _linear_kernel(x_ref, w_ref, *refs, splits, n_chunk, dils):
    o_refs, z_ref = refs[:len(splits)], refs[len(splits)]
    xb = x_ref[...].astype(CDT)
    tm = xb.shape[0]
    for o_ref, (c0, c1), dil in zip(o_refs, splits, dils):
        width = c1 - c0
        if dil == 1:
            for a in range(c0, c1, n_chunk):
                b = min(a + n_chunk, c1)
                o_ref[:, a - c0:b - c0] = _dot(xb, w_ref[:, a:b]).astype(o_ref.dtype)
        else:
            z = _dot(xb, w_ref[:, c0:c1])
            for cb in range(width // LANES):
                z_ref[cb] = z[:, cb * LANES:(cb + 1) * LANES]
            for r in range(dil):
                for cb in range(width // LANES):
                    o_ref[:, r * width + cb * LANES:r * width + (cb + 1) * LANES] = (
                        z_ref[cb, pl.ds(r, tm // dil, stride=dil), :].astype(o_ref.dtype))


def _linear(x2d, w, splits, n_chunk, name, dils=None, tm=512):
    t, k = x2d.shape
    tm = min(tm, t)
    n = w.shape[1]
    dils = tuple(dils) if dils is not None else (1,) * len(splits)
    widths = [c1 - c0 for c0, c1 in splits]
    assert all(tm % (16 * dl) == 0 for dl in dils)
    return pl.pallas_call(
        functools.partial(_linear_kernel, splits=tuple(splits), n_chunk=n_chunk, dils=dils),
        grid=(t // tm,),
        in_specs=[pl.BlockSpec((tm, k), lambda i: (i, 0)),
                  pl.BlockSpec((k, n), lambda i: (0, 0))],
        out_specs=[pl.BlockSpec((tm // dl, dl * wd), lambda i: (i, 0)) for wd, dl in zip(widths, dils)],
        out_shape=[jax.ShapeDtypeStruct((t // dl, dl * wd), CDT) for wd, dl in zip(widths, dils)],
        scratch_shapes=[pltpu.VMEM((max(widths) // LANES, tm, LANES), F32)],
        compiler_params=_cparams(("parallel",), 48),
        name=name,
    )(x2d, w)


def _linear_t_kernel(x_ref, wt_ref, o_ref):
    o_ref[...] = _nt_dot(wt_ref[...], x_ref[...].astype(CDT)).astype(o_ref.dtype)


def _linear_t(x2d, wt, tm, name):
    t, k = x2d.shape
    n = wt.shape[0]
    return pl.pallas_call(
        _linear_t_kernel,
        grid=(t // tm,),
        in_specs=[pl.BlockSpec((tm, k), lambda i: (i, 0)), pl.BlockSpec((n, k), lambda i: (0, 0))],
        out_specs=pl.BlockSpec((None, n, tm), lambda i: (i, 0, 0)),
        out_shape=jax.ShapeDtypeStruct((t // tm, n, tm), CDT),
        compiler_params=_cparams(("parallel",), 32),
        name=name,
    )(x2d, wt)


def _conv_kernel(x_ref, xh_ref, wc_ref, cw_ref, ob_ref, *, tm):
    i = pl.program_id(1)
    w = CONV_WIDTH
    xb = x_ref[...].astype(CDT)
    z = _dot(xb, wc_ref[...])
    u = z[:, w:2 * w] * z[:, 2 * w:3 * w]
    zh = _dot(xh_ref[...].astype(CDT), wc_ref[:, w:3 * w])
    uh = zh[:, :w] * zh[:, w:]
    uh = jnp.where(i == 0, 0.0, uh)
    row = lax.broadcasted_iota(jnp.int32, (tm, w), 0)
    u1 = jnp.where(row == 0, uh[7:8, :], pltpu.roll(u, 1, 0))
    u2 = jnp.where(row == 0, uh[6:7, :], jnp.where(row == 1, uh[7:8, :], pltpu.roll(u, 2, 0)))
    y = cw_ref[0:1, :] * u2 + cw_ref[1:2, :] * u1 + cw_ref[2:3, :] * u
    ob_ref[...] = (z[:, :w] * y).astype(ob_ref.dtype)


def _conv(x, wc, conv_w, tm=512):
    b, s, d = x.shape
    tm = min(tm, s)
    hb = tm // 8
    return pl.pallas_call(
        functools.partial(_conv_kernel, tm=tm),
        grid=(b, s // tm),
        in_specs=[pl.BlockSpec((None, tm, d), lambda bi, i: (bi, i, 0)),
                  pl.BlockSpec((None, 8, d), lambda bi, i: (bi, jnp.maximum(i * hb - 1, 0), 0)),
                  pl.BlockSpec(wc.shape, lambda bi, i: (0, 0)),
                  pl.BlockSpec(conv_w.shape, lambda bi, i: (0, 0))],
        out_specs=pl.BlockSpec((None, tm, CONV_WIDTH), lambda bi, i: (bi, i, 0)),
        out_shape=jax.ShapeDtypeStruct((b, s, CONV_WIDTH), CDT),
        compiler_params=_cparams(("parallel", "parallel"), 48),
        name="short_conv",
    )(x, x, wc, conv_w)


def _banded_kernel(*refs, window, pr, tq, kw, want_lse, has_sink):
    q_ref, kp_ref, kc_ref, vp_ref, vc_ref = refs[:5]
    n = 5
    sink_ref = None
    if has_sink:
        sink_ref = refs[n]
        n += 1
    o_ref = refs[n]
    n += 1
    lse_ref = None
    if want_lse:
        lse_ref = refs[n]
        n += 1
    kbuf, vbuf, s_ref, e_ref, m_ref, l_ref = refs[n:n + 6]

    i = pl.program_id(2)
    kbuf[0:pr, :] = kp_ref[...]
    kbuf[pr:pr + tq, :] = kc_ref[...]
    vbuf[0:pr, :] = vp_ref[...]
    vbuf[pr:pr + tq, :] = vc_ref[...]

    span = SUB_Q + pr
    qi = lax.broadcasted_iota(jnp.int32, (SUB_Q, span), 0)
    kj = lax.broadcasted_iota(jnp.int32, (SUB_Q, span), 1)
    dist = pr + qi - kj
    band = (dist >= 0) & (dist <= window)
    lo = _half_masks()
    halves = (lo, jnp.logical_not(lo))
    groups = ((0,), (1,), (2,)) if kw == 3 * LANES else ((0, 1, 2),)
    rb = BAND_ROWS
    for sb in range(tq // SUB_Q):
        r0 = sb * SUB_Q
        bias = jnp.where(band & (i * tq + r0 - pr + kj >= 0), 0.0, NEG_INF)
        for grp in groups:
            kc0 = grp[0] * LANES if kw == 3 * LANES else 0
            qs = jnp.concatenate(
                [(jnp.where(hm, q_ref[r0:r0 + SUB_Q, p * LANES:(p + 1) * LANES], 0) * QK_SCALE).astype(CDT)
                 for p in grp for hm in halves], axis=0)
            g0 = 2 * grp[0] * SUB_Q
            s_ref[g0:g0 + qs.shape[0], :] = _nt_dot(qs, kbuf[r0:r0 + span, kc0:kc0 + LANES])
        for c0 in range(0, 6 * SUB_Q, rb):
            rows = slice(c0, c0 + rb)
            s = s_ref[rows, :] + bias[c0 % SUB_Q:c0 % SUB_Q + rb, :]
            m = jnp.max(s, axis=-1, keepdims=True)
            e = jnp.exp(s - m)
            e_ref[rows, :] = e.astype(CDT)
            m_ref[rows, :] = jnp.broadcast_to(m, (rb, LANES))
            l_ref[rows, :] = jnp.broadcast_to(jnp.sum(e, axis=-1, keepdims=True), (rb, LANES))
        for grp in groups:
            kc0 = grp[0] * LANES if kw == 3 * LANES else 0
            g0 = 2 * grp[0] * SUB_Q
            g1 = g0 + 2 * len(grp) * SUB_Q
            l = l_ref[g0:g1, :]
            o = _dot(e_ref[g0:g1, :], vbuf[r0:r0 + span, kc0:kc0 + LANES]) / l
            lse = m_ref[g0:g1, :] + jnp.log(l)
            for n_p, p in enumerate(grp):
                a = 2 * n_p * SUB_Q
                o_pair = jnp.where(lo, o[a:a + SUB_Q], o[a + SUB_Q:a + 2 * SUB_Q])
                lse_pair = jnp.where(lo, lse[a:a + SUB_Q], lse[a + SUB_Q:a + 2 * SUB_Q])
                if has_sink:
                    o_pair = o_pair * jax.nn.sigmoid(lse_pair - sink_ref[:, p * LANES:(p + 1) * LANES])
                o_ref[r0:r0 + SUB_Q, p * LANES:(p + 1) * LANES] = o_pair.astype(o_ref.dtype)
                if want_lse:
                    lse_ref[r0:r0 + SUB_Q, p * LANES:(p + 1) * LANES] = lse_pair


def _banded(qa, ka, va, *, nrep, qcol, kcol, vcol, kw, window, want_lse, sink_row=None, tq=512):
    b, l, _ = qa.shape
    pr = -(-window // SUB_Q) * SUB_Q
    tq = min(max(tq, pr), l)
    assert tq % pr == 0 and l % tq == 0, (tq, pr, l)
    ratio = tq // pr
    qw = 3 * LANES
    in_specs = [
        pl.BlockSpec((None, tq, qw), lambda bi, r, i: (bi, i, qcol(r))),
        pl.BlockSpec((None, pr, kw), lambda bi, r, i: (bi, jnp.maximum(i * ratio - 1, 0), kcol(r))),
        pl.BlockSpec((None, tq, kw), lambda bi, r, i: (bi, i, kcol(r))),
        pl.BlockSpec((None, pr, kw), lambda bi, r, i: (bi, jnp.maximum(i * ratio - 1, 0), vcol(r))),
        pl.BlockSpec((None, tq, kw), lambda bi, r, i: (bi, i, vcol(r))),
    ]
    args = [qa, ka, ka, va, va]
    if sink_row is not None:
        in_specs.append(pl.BlockSpec(sink_row.shape, lambda bi, r, i: (0, 0)))
        args.append(sink_row)
    out_specs = [pl.BlockSpec((None, tq, qw), lambda bi, r, i: (bi, i, r))]
    out_shape = [jax.ShapeDtypeStruct((b, l, nrep * qw), CDT)]
    if want_lse:
        out_specs.append(pl.BlockSpec((None, tq, qw), lambda bi, r, i: (bi, i, r)))
        out_shape.append(jax.ShapeDtypeStruct((b, l, nrep * qw), F32))
    res = pl.pallas_call(
        functools.partial(_banded_kernel, window=window, pr=pr, tq=tq, kw=kw, want_lse=want_lse,
                          has_sink=sink_row is not None),
        grid=(b, nrep, l // tq),
        in_specs=in_specs,
        out_specs=out_specs,
        out_shape=out_shape,
        scratch_shapes=[pltpu.VMEM((pr + tq, kw), ka.dtype), pltpu.VMEM((pr + tq, kw), va.dtype),
                        pltpu.VMEM((6 * SUB_Q, SUB_Q + pr), F32), pltpu.VMEM((6 * SUB_Q, SUB_Q + pr), CDT),
                        pltpu.VMEM((6 * SUB_Q, LANES), F32), pltpu.VMEM((6 * SUB_Q, LANES), F32)],
        compiler_params=_cparams(("parallel", "parallel", "parallel"), 32),
        name=f"banded_w{window}_k{kw}_r{nrep}",
    )(*args)
    return res


def _gelu_tanh(x):
    return 0.5 * x * (1.0 + jnp.tanh(0.7978845608028654 * (x + 0.044715 * (x * x * x))))


def _compress_kernel(x_ref, pa_ref, pb_ref, w1a_ref, w1b_ref, b1_ref, w2_ref, b2_ref, o_ref):
    x = x_ref[...].astype(F32)
    n = x.shape[0]
    a = _dot((x + pa_ref[...]).astype(CDT), w1a_ref[...])
    bm = _dot((x + pb_ref[...]).astype(CDT), w1b_ref[...])
    h = a + pltpu.roll(bm, n - 1, 0) + b1_ref[...]
    o_ref[...] = (_dot(_gelu_tanh(h).astype(CDT), w2_ref[...]) + b2_ref[...]).astype(o_ref.dtype)


def _compress(t, pos, w1, b1, w2, b2):
    b, s, _ = t.shape
    nch = s // CMP_STRIDE
    xw = CMP_STRIDE * LANES
    x = t.reshape(b, nch, xw)
    eye = jnp.eye(NSA_KV_HEADS, dtype=F32)
    w1r = w1.reshape(CMP_BLOCK, HEAD_DIM, CMP_HIDDEN)

    def expand_w1(part):
        return jnp.einsum('tdj,kl->tkdlj', part, eye).reshape(xw, NSA_KV_HEADS * CMP_HIDDEN).astype(CDT)

    def expand_pos(part):
        return jnp.broadcast_to(part[:, None, :], (CMP_STRIDE, NSA_KV_HEADS, HEAD_DIM)).reshape(1, xw)

    w1a, w1b = expand_w1(w1r[:CMP_STRIDE]), expand_w1(w1r[CMP_STRIDE:])
    pa, pb = expand_pos(pos[:CMP_STRIDE]), expand_pos(pos[CMP_STRIDE:])
    b1e = jnp.tile(b1, NSA_KV_HEADS).reshape(1, -1)
    w2e = jnp.einsum('jd,kl->kjld', w2, eye).reshape(NSA_KV_HEADS * CMP_HIDDEN, LANES).astype(CDT)
    b2e = jnp.tile(b2, NSA_KV_HEADS).reshape(1, -1)
    consts = [pa, pb, w1a, w1b, b1e, w2e, b2e]
    return pl.pallas_call(
        _compress_kernel,
        grid=(b,),
        in_specs=[pl.BlockSpec((None, nch, xw), lambda bi: (bi, 0, 0))]
        + [pl.BlockSpec(c.shape, lambda bi: (0, 0)) for c in consts],
        out_specs=pl.BlockSpec((None, nch, LANES), lambda bi: (bi, 0, 0)),
        out_shape=jax.ShapeDtypeStruct((b, nch, LANES), CDT),
        compiler_params=_cparams(("parallel",), 48),
        name="nsa_compress",
    )(x, *consts)


CMP_ROWS = 16


def _cmp_select_kernel(q_ref, kc_ref, vc_ref, ov_ref, o_ref, sel_ref, s_ref, p_ref, hi_ref, lo_ref,
                       *, tq, n_sel, tile0):
    i = pl.program_id(1) + tile0
    ncp = kc_ref.shape[0]
    ns = ov_ref.shape[1]
    lo = _half_masks()
    t_col = i * tq + lax.broadcasted_iota(jnp.int32, (tq, 1), 0)
    blk = lax.broadcasted_iota(jnp.int32, (tq, ns), 1)
    blk_t = lax.broadcasted_iota(jnp.int32, (ns, tq), 0)
    cur = t_col // SEL_BLOCK
    causal = blk <= cur
    forced = (blk == 0) | (blk == cur) | (blk == cur - 1)
    rb = CMP_ROWS
    c_end = lax.broadcasted_iota(jnp.int32, (rb, ncp), 1) * CMP_STRIDE + (CMP_BLOCK - 1)
    o_kv, work_t = [], []
    for kv, hm in enumerate((lo, jnp.logical_not(lo))):
        qs = jnp.concatenate(
            [(jnp.where(hm, q_ref[:, p * LANES:(p + 1) * LANES], 0) * QK_SCALE).astype(CDT) for p in range(3)], axis=0)
        s_ref[...] = _nt_dot(qs, kc_ref[...])
        for r0 in range(0, tq, rb):
            t_rows = i * tq + r0 + lax.broadcasted_iota(jnp.int32, (rb, 1), 0)
            vis_bias = jnp.where(c_end <= t_rows, 0.0, NEG_INF)
            has_visible = t_rows >= CMP_BLOCK - 1
            psum = jnp.zeros((rb, ncp), F32)
            for h in range(3):
                rows = slice(h * tq + r0, h * tq + r0 + rb)
                s = s_ref[rows, :] + vis_bias
                e = jnp.exp(s - jnp.max(s, axis=-1, keepdims=True))
                inv = jnp.where(has_visible, 1.0 / jnp.maximum(jnp.sum(e, axis=-1, keepdims=True), 1e-30), 0.0)
                pn = e * inv
                p_ref[rows, :] = pn.astype(CDT)
                psum = psum + pn
            p_hi = psum.astype(CDT)
            hi_ref[r0:r0 + rb, :] = p_hi
            lo_ref[r0:r0 + rb, :] = (psum - p_hi.astype(F32)).astype(CDT)
        o = _dot(p_ref[...], vc_ref[...])
        o_kv.append([o[p * tq:(p + 1) * tq] for p in range(3)])
        imp = _dot(hi_ref[...], ov_ref[...]) + _dot(lo_ref[...], ov_ref[...])
        work_t.append(jnp.where(causal & jnp.logical_not(forced), imp, -BIG).T)

    blk_lanes = blk_t[:, 0:LANES]

    def pick(_, work):
        m = jnp.max(work, axis=0, keepdims=True)
        idx = jnp.min(jnp.where(work == m, blk_lanes, ns), axis=0, keepdims=True)
        return jnp.where(blk_lanes == idx, -2.0 * BIG, work)

    for kv, start in enumerate(work_t):
        done = jnp.concatenate([lax.fori_loop(0, n_sel - 3, pick, start[:, c0:c0 + LANES])
                                for c0 in range(0, tq, LANES)], axis=1)
        taken = jnp.where((done < -BIG) & (start > -BIG), 1.0, 0.0).T
        selb = jnp.where(forced | (taken > 0.5), 0.0, NEG_INF)
        sel_ref[:, kv * ns:(kv + 1) * ns] = selb.astype(sel_ref.dtype)
    for p in range(3):
        o_ref[:, p * LANES:(p + 1) * LANES] = jnp.where(lo, o_kv[0][p], o_kv[1][p]).astype(o_ref.dtype)


CMP_CAUSAL_SPLITS = 4


def _cmp_select(zr, qcol, kc, vc, tq=256):
    b, s, _ = zr.shape
    ncp = kc.shape[1]
    ns = s // SEL_BLOCK
    n_sel = min(N_SEL, ns)
    assert n_sel >= 3, "selection needs room for the three forced blocks"
    tq = min(tq, s)
    c = np.arange(ncp)[:, None] * CMP_STRIDE
    j = np.arange(ns)[None, :] * SEL_BLOCK
    overlap = ((c < j + SEL_BLOCK) & (c + CMP_BLOCK - 1 >= j)).astype(np.float32)
    overlap[ncp - 1:, :] = 0.0
    ov = jnp.asarray(overlap, CDT)
    qw = 3 * LANES
    n_split = CMP_CAUSAL_SPLITS if (s // tq) % CMP_CAUSAL_SPLITS == 0 and ncp % (16 * CMP_CAUSAL_SPLITS) == 0 else 1
    tiles = s // tq // n_split
    outs, sels = [], []
    for part in range(n_split):
        tile0 = part * tiles
        ncp_part = ncp * (part + 1) // n_split
        o_part, sel_part = pl.pallas_call(
            functools.partial(_cmp_select_kernel, tq=tq, n_sel=n_sel, tile0=tile0),
            grid=(b, tiles),
            in_specs=[pl.BlockSpec((None, tq, qw), lambda bi, i, tile0=tile0: (bi, i + tile0, qcol)),
                      pl.BlockSpec((None, ncp_part, LANES), lambda bi, i: (bi, 0, 0)),
                      pl.BlockSpec((None, ncp_part, LANES), lambda bi, i: (bi, 0, 0)),
                      pl.BlockSpec((ncp_part, ns), lambda bi, i: (0, 0))],
            out_specs=[pl.BlockSpec((None, tq, qw), lambda bi, i: (bi, i, 0)),
                       pl.BlockSpec((None, tq, 2 * ns), lambda bi, i: (bi, i, 0))],
            out_shape=[jax.ShapeDtypeStruct((b, tiles * tq, qw), CDT),
                       jax.ShapeDtypeStruct((b, tiles * tq, 2 * ns), CDT)],
            scratch_shapes=[pltpu.VMEM((3 * tq, ncp_part), F32), pltpu.VMEM((3 * tq, ncp_part), CDT),
                            pltpu.VMEM((tq, ncp_part), CDT), pltpu.VMEM((tq, ncp_part), CDT)],
            compiler_params=_cparams(("parallel", "parallel"), 48),
            name=f"nsa_cmp_select_p{part}",
        )(zr, kc, vc, ov)
        outs.append(o_part)
        sels.append(sel_part)
    return jnp.concatenate(outs, axis=1), jnp.concatenate(sels, axis=1)


SLC_KEY_CHUNK = 256


def _slc_kernel(q_ref, k0_ref, k1_ref, vt_ref, selb_ref, o_ref, s_ref, e_ref, m_ref, l_ref, a_ref, acc_ref,
                *, tq, tk):
    i = pl.program_id(1)
    ns = selb_ref.shape[1] // 2
    bpt = tk // SEL_BLOCK
    n_q = 3 * tq
    rc = min(SLC_KEY_CHUNK, tk)
    lo = _half_masks()
    n_tiles = ((i + 1) * tq + tk - 1) // tk
    p_row = lax.broadcasted_iota(jnp.int32, (ns, LANES), 0)
    p_col = lax.broadcasted_iota(jnp.int32, (ns, LANES), 1)
    groups = []
    for kv, (hm, k_ref) in enumerate(((lo, k0_ref), (jnp.logical_not(lo), k1_ref))):
        q3 = [(jnp.where(hm, q_ref[:, p * LANES:(p + 1) * LANES], 0) * QK_SCALE).astype(CDT) for p in range(3)]
        selb = selb_ref[:, kv * ns:(kv + 1) * ns]
        lane0 = HEAD_DIM if kv == 0 else 0
        groups.append((kv, hm, k_ref, q3, selb, lane0))
    m_ref[...] = jnp.full(m_ref.shape, NEG_INF, F32)
    l_ref[...] = jnp.zeros(l_ref.shape, F32)
    acc_ref[...] = jnp.zeros(acc_ref.shape, F32)

    def scores(j):
        k0 = pl.multiple_of(j * tk, tk)
        for kv, hm, k_ref, q3, selb, lane0 in groups:
            place = ((p_col >= lane0) & (p_col < lane0 + bpt) & (p_row == p_col - lane0 + j * bpt)).astype(CDT)
            sb = _dot(selb, place).astype(CDT)
            qp = jnp.concatenate([jnp.where(hm, q, sb) for q in q3], axis=0)
            s_ref[kv] = _nt_dot(k_ref[pl.ds(k0, tk), :], qp)

    def softmax_pv(j, diagonal):
        k0 = pl.multiple_of(j * tk, tk)
        for kv in range(2):
            for c0 in range(0, n_q, LANES):
                cols = slice(c0, c0 + LANES)
                t_lane = i * tq + (c0 % tq) + lax.broadcasted_iota(jnp.int32, (1, LANES), 1)

                def chunk(r0):
                    s = s_ref[kv, r0:r0 + rc, cols]
                    if diagonal:
                        kpos = k0 + r0 + lax.broadcasted_iota(jnp.int32, (rc, LANES), 0)
                        s = jnp.where(kpos <= t_lane, s, NEG_INF)
                    return s

                m8 = m_ref[kv, :, cols]
                for r0 in range(0, tk, rc):
                    m8 = jnp.maximum(m8, jnp.max(chunk(r0).reshape(rc // 8, 8, LANES), axis=0))
                m_new = jnp.max(m8, axis=0, keepdims=True)
                alpha = jnp.exp(m_ref[kv, :, cols] - m_new)
                l8 = jnp.zeros((8, LANES), F32)
                for r0 in range(0, tk, rc):
                    e = jnp.exp(chunk(r0) - m_new)
                    l8 = l8 + jnp.sum(e.reshape(rc // 8, 8, LANES), axis=0)
                    e_ref[kv, r0:r0 + rc, cols] = e.astype(CDT)
                l_ref[kv, :, cols] = alpha * l_ref[kv, :, cols] + jnp.sum(l8, axis=0, keepdims=True)
                m_ref[kv, :, cols] = jnp.broadcast_to(m_new, (8, LANES))
                a_ref[kv, :, cols] = alpha
        for kv in range(2):
            acc_ref[kv] = a_ref[kv, 0:1, :] * acc_ref[kv] + _dot(vt_ref[j], e_ref[kv])

    def tile(j, diagonal):
        scores(j)
        softmax_pv(j, diagonal)

    lax.fori_loop(0, n_tiles - 1, lambda j, c: (tile(j, False), c)[1], 0)
    tile(n_tiles - 1, True)
    o0 = (acc_ref[0] / l_ref[0, 0:1, :]).T
    o1 = (acc_ref[1] / l_ref[1, 0:1, :]).T
    for p in range(3):
        o_ref[:, p * LANES:(p + 1) * LANES] = jnp.where(
            lo, o0[p * tq:(p + 1) * tq], o1[p * tq:(p + 1) * tq]).astype(o_ref.dtype)


def _slc(zr, qcol, ksc, vt, selb, tq=512):
    b, s, _ = zr.shape
    tq = min(tq, s)
    tk = vt.shape[2]
    assert tq % LANES == 0 and tk // SEL_BLOCK <= HEAD_DIM and s % tk == 0
    qw = 3 * LANES
    ns2 = selb.shape[2]
    pat = jax.nn.one_hot((jnp.arange(s) // SEL_BLOCK) % (tk // SEL_BLOCK), HEAD_DIM, dtype=ksc.dtype)
    pat = jnp.broadcast_to(pat[None], (b, s, HEAD_DIM))
    k0 = jnp.concatenate([ksc[..., :HEAD_DIM], pat], axis=-1)
    k1 = jnp.concatenate([pat, ksc[..., HEAD_DIM:]], axis=-1)
    full = lambda bi, i: (bi, 0, 0)
    return pl.pallas_call(
        functools.partial(_slc_kernel, tq=tq, tk=tk),
        grid=(b, s // tq),
        in_specs=[pl.BlockSpec((None, tq, qw), lambda bi, i: (bi, i, qcol)),
                  _single_buffered((None, s, LANES), full),
                  _single_buffered((None, s, LANES), full),
                  _single_buffered((s // tk, LANES, tk), lambda bi, i: (bi, 0, 0)),
                  pl.BlockSpec((None, tq, ns2), lambda bi, i: (bi, i, 0))],
        out_specs=pl.BlockSpec((None, tq, qw), lambda bi, i: (bi, i, 0)),
        out_shape=jax.ShapeDtypeStruct((b, s, qw), CDT),
        scratch_shapes=[pltpu.VMEM((2, tk, 3 * tq), F32), pltpu.VMEM((2, tk, 3 * tq), CDT),
                        pltpu.VMEM((2, 8, 3 * tq), F32), pltpu.VMEM((2, 8, 3 * tq), F32),
                        pltpu.VMEM((2, 8, 3 * tq), F32), pltpu.VMEM((2, LANES, 3 * tq), F32)],
        compiler_params=_cparams(("parallel", "arbitrary"), 56),
        name="nsa_slc",
    )(zr, k0, k1, vt, selb)


def _merge_kernel(x_ref, oa0, oa1, oa2, la0, la1, la2, ob, ocmp, oslc, owin, od,
                  wn_ref, wg_ref, bg_ref, wb_ref, wo_ref, g_ref, b_ref, o_ref, *scratch, alpha):
    x = x_ref[...]
    xb = x.astype(CDT)
    bw = BRANCH_WIDTH

    def token_rows(src_ref, dst_ref):
        dil = src_ref.shape[1] // bw
        if dil == 1:
            return src_ref[...].astype(F32)
        n_cb = bw // LANES
        for r in range(dil):
            for cb in range(n_cb):
                c0 = r * bw + cb * LANES
                dst_ref[cb, pl.ds(r, src_ref.shape[0], stride=dil), :] = src_ref[:, c0:c0 + LANES].astype(F32)
        return jnp.concatenate([dst_ref[cb] for cb in range(n_cb)], axis=1)

    o0, o1, o2 = (token_rows(s, d) for s, d in zip((oa0, oa1, oa2), scratch[0:3]))
    l0, l1, l2 = (token_rows(s, d) for s, d in zip((la0, la1, la2), scratch[3:6]))
    mx = jnp.maximum(jnp.maximum(l0, l1), l2)
    w0, w1, w2 = jnp.exp(l0 - mx), jnp.exp(l1 - mx), jnp.exp(l2 - mx)
    o_a = (w0 * o0 + w1 * o1 + w2 * o2) / (w0 + w1 + w2)
    gates = jax.nn.sigmoid(_dot(xb, wn_ref[...]))
    o_c = (gates[:, 0:bw] * ocmp[...].astype(F32) + gates[:, bw:2 * bw] * oslc[...].astype(F32)
           + gates[:, 2 * bw:3 * bw] * owin[...].astype(F32))
    branches = (o_a.astype(CDT), ob[...], o_c.astype(CDT), od[...])
    d = x.shape[1]
    merged = jnp.zeros(x.shape, F32)
    for m in range(N_BRANCH):
        gate = jax.nn.sigmoid(_dot(xb, wg_ref[:, m * d:(m + 1) * d]) + bg_ref[:, m * d:(m + 1) * d])
        merged = merged + gate * _dot(branches[m], wb_ref[m])
    r = alpha * x + _dot(merged.astype(CDT), wo_ref[...])
    o_ref[...] = _layer_norm(r, g_ref[...], b_ref[...])


def _merge(x2d, branch_inputs, wn, wg, bg, wb, wo, g, b, alpha, tm=256):
    t, d = x2d.shape
    tm = min(tm, t)
    row = lambda i: (i, 0)
    const2 = lambda i: (0, 0)
    in_specs = [pl.BlockSpec((tm, d), row)]
    in_specs += [pl.BlockSpec((tm * a.shape[0] // t, a.shape[1]), row) for a in branch_inputs]
    in_specs += [pl.BlockSpec(wn.shape, const2), pl.BlockSpec(wg.shape, const2), pl.BlockSpec(bg.shape, const2),
                 pl.BlockSpec(wb.shape, lambda i: (0, 0, 0)), pl.BlockSpec(wo.shape, const2),
                 pl.BlockSpec(g.shape, const2), pl.BlockSpec(b.shape, const2)]
    return pl.pallas_call(
        functools.partial(_merge_kernel, alpha=alpha),
        grid=(t // tm,),
        in_specs=in_specs,
        out_specs=pl.BlockSpec((tm, d), row),
        out_shape=jax.ShapeDtypeStruct((t, d), F32),
        scratch_shapes=[pltpu.VMEM((BRANCH_WIDTH // LANES, tm, LANES), F32) for _ in range(2 * N_DIL)],
        compiler_params=_cparams(("parallel",), 56),
        name="merge_ln",
    )(x2d, *branch_inputs, wn, wg, bg, wb, wo, g, b)


def _ple_ln(x, xb, f, p, plw_ref, pgw_ref, pgb_ref, g_ref, b_ref, alpha):
    ple = jax.nn.sigmoid(_dot(xb, pgw_ref[...]) + pgb_ref[...]) * _dot(p.astype(CDT), plw_ref[...])
    return _layer_norm(alpha * x + f + ple, g_ref[...], b_ref[...])


FFN_CHUNK = 512


def _ffn_kernel(x_ref, p_ref, wg_ref, wu_ref, wd_ref, plw_ref, pgw_ref, pgb_ref, g_ref, b_ref, o_ref, h_ref, *, alpha):
    x = x_ref[...]
    xb = x.astype(CDT)
    dff = wg_ref.shape[1]
    for c0 in range(0, dff, FFN_CHUNK):
        cols = slice(c0, min(c0 + FFN_CHUNK, dff))
        h_ref[:, cols] = (jax.nn.silu(_dot(xb, wg_ref[:, cols])) * _dot(xb, wu_ref[:, cols])).astype(CDT)
    f = _dot(h_ref[...], wd_ref[...])
    o_ref[...] = _ple_ln(x, xb, f, p_ref[...], plw_ref, pgw_ref, pgb_ref, g_ref, b_ref, alpha)


def _ffn(x2d, p2d, wg, wu, wd, plw, pgw, pgb, g, b, alpha, tm=512):
    t, d = x2d.shape
    tm = min(tm, t)
    dff = wg.shape[1]
    wg, wu, wd = wg.astype(CDT), wu.astype(CDT), wd.astype(CDT)
    row = lambda i: (i, 0)
    const = lambda shape: pl.BlockSpec(shape, lambda i: (0, 0), pipeline_mode=pl.Buffered(1))
    return pl.pallas_call(
        functools.partial(_ffn_kernel, alpha=alpha),
        grid=(t // tm,),
        in_specs=[pl.BlockSpec((tm, d), row), pl.BlockSpec((tm, p2d.shape[1]), row),
                  const(wg.shape), const(wu.shape), const(wd.shape),
                  const(plw.shape), const(pgw.shape), const(pgb.shape), const(g.shape), const(b.shape)],
        out_specs=pl.BlockSpec((tm, d), row),
        out_shape=jax.ShapeDtypeStruct((t, d), F32),
        scratch_shapes=[pltpu.VMEM((tm, dff), CDT)],
        compiler_params=_cparams(("parallel",), 48),
        name="ffn_ple_ln",
    )(x2d, p2d, wg, wu, wd, plw, pgw, pgb, g, b)


def _router_kernel(x_ref, wh_ref, wl_ref, b_ref, comb_ref, rank_ref, rank_t_ref, cnt_ref):
    x = x_ref[...]
    xh = x.astype(CDT)
    xl = (x - xh.astype(F32)).astype(CDT)
    logits = _dot(xh, wh_ref[...]) + _dot(xh, wl_ref[...]) + _dot(xl, wh_ref[...]) + b_ref[...]
    lane = lax.broadcasted_iota(jnp.int32, logits.shape, 1)
    v1 = jnp.max(logits, axis=-1, keepdims=True)
    i1 = jnp.min(jnp.where(logits == v1, lane, LANES), axis=-1, keepdims=True)
    rest = jnp.where(lane == i1, -jnp.inf, logits)
    v2 = jnp.max(rest, axis=-1, keepdims=True)
    i2 = jnp.min(jnp.where(rest == v2, lane, LANES), axis=-1, keepdims=True)
    e2 = jnp.exp(v2 - v1)
    comb_ref[...] = jnp.where(lane == i1, 1.0 / (1.0 + e2), 0.0) + jnp.where(lane == i2, e2 / (1.0 + e2), 0.0)
    routed = (lane == i1) | (lane == i2)
    mask = routed.astype(CDT)
    tm = x.shape[0]
    before = (lax.broadcasted_iota(jnp.int32, (tm, tm), 1) < lax.broadcasted_iota(jnp.int32, (tm, tm), 0)).astype(CDT)
    rank = jnp.where(routed, _dot(before, mask), -1.0)
    rank_ref[...] = rank
    rank_t_ref[...] = rank.T[0:rank_t_ref.shape[0], :]
    cnt_ref[...] = jnp.sum(routed.astype(F32), axis=0, keepdims=True).astype(jnp.int32)


def _router(x2d, w_router, b_router, tm):
    t, d = x2d.shape
    ne = w_router.shape[1]
    wp = jnp.zeros((d, LANES), F32).at[:, :ne].set(w_router)
    wh = wp.astype(CDT)
    wl = (wp - wh.astype(F32)).astype(CDT)
    bp = jnp.full((1, LANES), -BIG, F32).at[0, :ne].set(b_router)
    nt = t // tm
    row = lambda i: (i, 0)
    return pl.pallas_call(
        _router_kernel,
        grid=(nt,),
        in_specs=[pl.BlockSpec((tm, d), row), pl.BlockSpec(wh.shape, lambda i: (0, 0)),
                  pl.BlockSpec(wl.shape, lambda i: (0, 0)), pl.BlockSpec(bp.shape, lambda i: (0, 0))],
        out_specs=[pl.BlockSpec((tm, LANES), row), pl.BlockSpec((tm, LANES), row),
                   pl.BlockSpec((None, 8, tm), lambda i: (i, 0, 0)),
                   pl.BlockSpec((None, 1, LANES), lambda i: (i, 0, 0))],
        out_shape=[jax.ShapeDtypeStruct((t, LANES), F32), jax.ShapeDtypeStruct((t, LANES), F32),
                   jax.ShapeDtypeStruct((nt, 8, tm), F32), jax.ShapeDtypeStruct((nt, 1, LANES), jnp.int32)],
        compiler_params=_cparams(("parallel",), 40),
        name="moe_router",
    )(x2d, wh, wl, bp)


def _moe_kernel(cnt_ref, x_ref, comb_ref, rank_ref, rank_t_ref, p_ref, wg_ref, wu_ref, wd_ref, plw_ref, pgw_ref,
                pgb_ref, g_ref, b_ref, o_ref, xe_ref, ye_ref, *, alpha, rs, seg):
    i = pl.program_id(0)
    e = pl.program_id(1)
    c = pl.program_id(2)
    n_seg = x_ref.shape[0] // seg
    last_chunk = c == pl.num_programs(2) - 1

    def rows(sc):
        return pl.ds(pl.multiple_of(sc * rs, 8), rs)

    @pl.when((e == 0) & (c == 0))
    def _():
        o_ref[...] = jnp.zeros_like(o_ref)

    for sg in range(n_seg):
        tok = slice(sg * seg, (sg + 1) * seg)
        n_groups = (cnt_ref[(i * n_seg + sg) * LANES + e] + rs - 1) // rs

        @pl.when(c == 0)
        def _():
            xb = x_ref[tok, :].astype(CDT)
            rank_row = rank_t_ref[sg, pl.ds(e, 1), :]
            row_id = lax.broadcasted_iota(jnp.int32, (rs, seg), 0).astype(F32)

            def gather(sc, _):
                onehot = (rank_row - (sc * rs).astype(F32) == row_id).astype(CDT)
                xe_ref[sg, rows(sc), :] = _dot(onehot, xb).astype(CDT)
                ye_ref[sg, rows(sc), :] = jnp.zeros((rs, ye_ref.shape[2]), F32)
                return 0

            lax.fori_loop(0, n_groups, gather, 0)

        def expert(sc, _):
            xs = xe_ref[sg, rows(sc), :]
            h = jax.nn.silu(_dot(xs, wg_ref[0])) * _dot(xs, wu_ref[0])
            ye_ref[sg, rows(sc), :] += _dot(h.astype(CDT), wd_ref[0])
            return 0

        lax.fori_loop(0, n_groups, expert, 0)

        @pl.when(last_chunk)
        def _():
            lane = lax.broadcasted_iota(jnp.int32, (seg, LANES), 1)
            mine = lane == e
            cw = jnp.sum(jnp.where(mine, comb_ref[tok, :], 0.0), axis=-1, keepdims=True)
            rank_col = jnp.sum(jnp.where(mine, rank_ref[tok, :], 0.0), axis=-1, keepdims=True)
            col_id = lax.broadcasted_iota(jnp.int32, (seg, rs), 1).astype(F32)

            def scatter(sc, _):
                onehot = (rank_col - (sc * rs).astype(F32) == col_id).astype(CDT)
                o_ref[tok, :] += cw * _dot(onehot, ye_ref[sg, rows(sc), :].astype(CDT))
                return 0

            lax.fori_loop(0, n_groups, scatter, 0)

    @pl.when((e == pl.num_programs(1) - 1) & last_chunk)
    def _():
        for sg in range(n_seg):
            tok = slice(sg * seg, (sg + 1) * seg)
            x = x_ref[tok, :]
            o_ref[tok, :] = _ple_ln(x, x.astype(CDT), o_ref[tok, :], p_ref[tok, :], plw_ref, pgw_ref, pgb_ref,
                                    g_ref, b_ref, alpha)


MOE_CHUNK = 512
MOE_SEGMENT = 1024
MOE_SEGMENTS_PER_TILE = 2
MOE_ROW_GROUP = 288


def _single_buffered(shape, index_map):
    return pl.BlockSpec(shape, index_map, pipeline_mode=pl.Buffered(1))


def _moe(x2d, routing, p2d, wg, wu, wd, plw, pgw, pgb, g, b, alpha, seg):
    comb, rank, rank_t, cnt = routing
    t, d = x2d.shape
    ne, _, dff = wg.shape
    ck = min(MOE_CHUNK, dff)
    rs = min(MOE_ROW_GROUP, seg)
    n_seg = min(MOE_SEGMENTS_PER_TILE, t // seg)
    tm = n_seg * seg
    max_rows = -(-seg // rs) * rs
    wg, wu, wd = wg.astype(CDT), wu.astype(CDT), wd.astype(CDT)
    row = lambda i, e, c, cnt: (i, 0)
    c2 = lambda i, e, c, cnt: (0, 0)
    grid_spec = pltpu.PrefetchScalarGridSpec(
        num_scalar_prefetch=1,
        grid=(t // tm, ne, dff // ck),
        in_specs=[_single_buffered((tm, d), row), _single_buffered((tm, LANES), row),
                  _single_buffered((tm, LANES), row),
                  _single_buffered((n_seg, 8, seg), lambda i, e, c, cnt: (i, 0, 0)),
                  _single_buffered((tm, p2d.shape[1]), row),
                  pl.BlockSpec((1, d, ck), lambda i, e, c, cnt: (e, 0, c)),
                  pl.BlockSpec((1, d, ck), lambda i, e, c, cnt: (e, 0, c)),
                  pl.BlockSpec((1, ck, d), lambda i, e, c, cnt: (e, c, 0)),
                  _single_buffered(plw.shape, c2), _single_buffered(pgw.shape, c2), _single_buffered(pgb.shape, c2),
                  _single_buffered(g.shape, c2), _single_buffered(b.shape, c2)],
        out_specs=pl.BlockSpec((tm, d), row),
        scratch_shapes=[pltpu.VMEM((n_seg, max_rows, d), CDT), pltpu.VMEM((n_seg, max_rows, d), F32)],
    )
    return pl.pallas_call(
        functools.partial(_moe_kernel, alpha=alpha, rs=rs, seg=seg),
        grid_spec=grid_spec,
        out_shape=jax.ShapeDtypeStruct((t, d), F32),
        compiler_params=_cparams(("parallel", "arbitrary", "arbitrary"), 60),
        name="moe_ple_ln",
    )(cnt.reshape(-1), x2d, comb, rank, rank_t, p2d, wg, wu, wd, plw, pgw, pgb, g, b)


def _prep_in_weights(w_in):
    o = COL_OFF
    bw = BRANCH_WIDTH
    cols = lambda n: w_in[:, o[n]:o[n + 1]]
    qa, ka, va = cols(0), cols(1), cols(2)
    w_dil = jnp.concatenate(
        [t[:, g * bw:(g + 1) * bw] for g in range(N_DIL) for t in (qa, ka, va)], axis=1).astype(CDT)
    w_conv = jnp.concatenate([cols(3), cols(4), cols(5)], axis=1).astype(CDT)
    gn = cols(13)
    w_gate = jnp.concatenate([gn[:, br * NSA_Q_HEADS + GQA_COL_HEAD] for br in range(3)], axis=1).astype(CDT)
    w_rest = jnp.concatenate([cols(6)[:, GQA_COL_PERM], cols(14)[:, GQA_COL_PERM], cols(11),
                              cols(12), cols(15), cols(16), cols(9), cols(7), cols(8)], axis=1).astype(CDT)
    wt_vsc = cols(10).T.astype(CDT)
    return w_dil, w_conv, w_gate, w_rest, wt_vsc


ZR_Q_NSA, ZR_Q_SWA = 0, 1
ZR_KWC, ZR_VWC, ZR_KD, ZR_VD = 6, 7, 8, 9
ZR_WIDTH = 2 * BRANCH_WIDTH + 4 * LANES
SLC_KEY_TILE = 1024


def _token_mixers(x, w_in, conv_w, cmp_pos, cmp_w1, cmp_b1, cmp_w2, cmp_b2, sinks):
    b, s, d = x.shape
    x2d = x.reshape(b * s, d)
    w_dil, w_conv, w_gate, w_rest, wt_vsc = _prep_in_weights(w_in)
    gw = 3 * BRANCH_WIDTH

    z_dil = _linear(x2d, w_dil, [(g * gw, (g + 1) * gw) for g in range(N_DIL)], gw, "in_proj_dil",
                    dils=[dil for _, dil in DIL_PATTERNS])
    zr, ksc, kcc, vcc = _linear(x2d, w_rest, [(0, ZR_WIDTH)] + [(ZR_WIDTH + n * LANES, ZR_WIDTH + (n + 1) * LANES)
                                                           for n in range(3)], 256, "in_proj_rest")
    zr = zr.reshape(b, s, ZR_WIDTH)
    o_b = _conv(x, w_conv, conv_w)

    dil_o, dil_lse = [], []
    for g, (window, dil) in enumerate(DIL_PATTERNS):
        view = z_dil[g].reshape(b, s // dil, dil * gw)
        og, lg = _banded(view, view, view, nrep=dil,
                         qcol=lambda r: 3 * r, kcol=lambda r: 3 * r + 1, vcol=lambda r: 3 * r + 2,
                         kw=3 * LANES, window=window // dil, want_lse=True)
        dil_o.append(og.reshape(b * s // dil, dil * BRANCH_WIDTH))
        dil_lse.append(lg.reshape(b * s // dil, dil * BRANCH_WIDTH))

    kc = _compress(kcc.reshape(b, s, LANES), cmp_pos[0], cmp_w1[0], cmp_b1[0], cmp_w2[0], cmp_b2[0])
    vc = _compress(vcc.reshape(b, s, LANES), cmp_pos[1], cmp_w1[1], cmp_b1[1], cmp_w2[1], cmp_b2[1])
    o_cmp, selb = _cmp_select(zr, ZR_Q_NSA, kc, vc)
    vsc_t = _linear_t(x2d, wt_vsc, min(SLC_KEY_TILE, s), "in_proj_vsc_t")
    o_slc = _slc(zr, ZR_Q_NSA, ksc.reshape(b, s, LANES), vsc_t, selb)
    (o_win,) = _banded(zr, zr, zr, nrep=1, qcol=lambda r: ZR_Q_NSA, kcol=lambda r: ZR_KWC, vcol=lambda r: ZR_VWC,
                       kw=LANES, window=NSA_WINDOW - 1, want_lse=False, tq=512)

    sink_row = sinks.astype(F32)[GQA_COL_HEAD].reshape(1, BRANCH_WIDTH)
    (o_d,) = _banded(zr, zr, zr, nrep=1, qcol=lambda r: ZR_Q_SWA, kcol=lambda r: ZR_KD, vcol=lambda r: ZR_VD,
                     kw=LANES, window=SWA_WINDOW - 1, want_lse=False, sink_row=sink_row)

    t = b * s
    flat = lambda a: a.reshape(t, a.shape[-1])
    return [dil_o[0], dil_o[1], dil_o[2], dil_lse[0], dil_lse[1], dil_lse[2], flat(o_b), flat(o_cmp), flat(o_slc),
            flat(o_win), flat(o_d)], w_gate


def kernel(x, p, w_in, conv_w, cmp_pos, cmp_w1, cmp_b1, cmp_w2, cmp_b2, sinks, w_branch, w_merge_gate, b_merge_gate, w_out, ln_mix_g, ln_mix_b, ffn_w_gate, ffn_w_up, ffn_w_down, w_router, b_router, moe_w_gate, moe_w_up, moe_w_down, ple_w, ple_gate_w, ple_gate_b, ln_ffn_g, ln_ffn_b):
    depth, b, s, _ = p.shape
    d = x.shape[-1]
    t = b * s
    alpha = (2 * depth) ** 0.25
    row = lambda v: v.reshape(1, -1).astype(F32)
    for i in range(depth):
        branch_inputs, w_nsa_gate = _token_mixers(x, w_in[i], conv_w[i], cmp_pos[i], cmp_w1[i], cmp_b1[i], cmp_w2[i],
                                                  cmp_b2[i], sinks[i])
        wg = jnp.concatenate([w_merge_gate[i, m] for m in range(N_BRANCH)], axis=1).astype(CDT)
        bg = b_merge_gate[i].reshape(1, N_BRANCH * d).astype(F32)
        wb = jnp.stack([w_branch[i, 0], w_branch[i, 1], w_branch[i, 2][GQA_COL_PERM],
                        w_branch[i, 3][GQA_COL_PERM]]).astype(CDT)
        x1 = _merge(x.reshape(t, d), branch_inputs, w_nsa_gate, wg, bg, wb, w_out[i].astype(CDT),
                    row(ln_mix_g[i]), row(ln_mix_b[i]), alpha)
        p2d = p[i].reshape(t, -1)
        ple_args = (ple_w[i].astype(CDT), ple_gate_w[i].astype(CDT), row(ple_gate_b[i]),
                    row(ln_ffn_g[i]), row(ln_ffn_b[i]))
        j = i // 2
        if i % 2 == 0:
            x2 = _ffn(x1, p2d, ffn_w_gate[j], ffn_w_up[j], ffn_w_down[j], *ple_args, alpha)
        else:
            seg = min(MOE_SEGMENT, t)
            routing = _router(x1, w_router[j], b_router[j], seg)
            x2 = _moe(x1, routing, p2d, moe_w_gate[j], moe_w_up[j], moe_w_down[j], *ple_args, alpha, seg)
        x = x2.reshape(b, s, d)
    return x
```

```python
import functools

import numpy as np
import jax
import jax.numpy as jnp
from jax import lax
from jax.experimental import pallas as pl
from jax.experimental.pallas import tpu as pltpu

D_MODEL = 1024
HEAD_DIM = 64
DIL_PATTERNS = ((128, 1), (512, 4), (2048, 16))
N_DIL = 3
DIL_HEADS = 6
CONV_WIDTH = 384
CONV_K = 3
NSA_Q_HEADS = 6
NSA_KV_HEADS = 2
CMP_BLOCK = 32
CMP_STRIDE = 16
CMP_HIDDEN = 128
SEL_BLOCK = 64
N_SEL = 16
NSA_WINDOW = 512
SWA_Q_HEADS = 6
SWA_WINDOW = 128
BRANCH_WIDTH = 384
N_BRANCH = 4
N_EXPERTS = 8
LN_EPS = 1e-5
NEG_INF = -1e30
DIL_WIDTH = N_DIL * DIL_HEADS * HEAD_DIM
COLUMN_SIZES = (DIL_WIDTH, DIL_WIDTH, DIL_WIDTH, CONV_WIDTH, CONV_WIDTH, CONV_WIDTH,
                NSA_Q_HEADS * HEAD_DIM, 128, 128, 128, 128, 128, 128, 3 * NSA_Q_HEADS,
                SWA_Q_HEADS * HEAD_DIM, 128, 128)
COL_OFF = np.concatenate([[0], np.cumsum(COLUMN_SIZES)]).tolist()

LANES = 128
V7X_VMEM_BYTES = 64 * 1024 * 1024
MIB = 1024 * 1024

CDT = jnp.bfloat16
F32 = jnp.float32
QK_SCALE = HEAD_DIM ** -0.5
SUB_Q = 128
BAND_ROWS = 64
BIG = 1e30

_GQA_HEAD_ORDER = (0, 3, 1, 4, 2, 5)
GQA_COL_PERM = np.concatenate([np.arange(h * HEAD_DIM, (h + 1) * HEAD_DIM) for h in _GQA_HEAD_ORDER])
GQA_COL_HEAD = GQA_COL_PERM // HEAD_DIM


def _cparams(sem, vmem_mib):
    return pltpu.CompilerParams(dimension_semantics=sem, vmem_limit_bytes=int(vmem_mib * MIB))


def _nt_dot(a, b):
    return lax.dot_general(a, b, (((1,), (1,)), ((), ())), preferred_element_type=F32)


def _dot(a, b):
    return jnp.dot(a, b, preferred_element_type=F32)


def _layer_norm(r, g, b):
    mu = jnp.mean(r, axis=-1, keepdims=True)
    d = r - mu
    var = jnp.mean(d * d, axis=-1, keepdims=True)
    return d * lax.rsqrt(var + LN_EPS) * g + b


def _half_masks():
    lane = lax.broadcasted_iota(jnp.int32, (1, LANES), 1)
    return lane < HEAD_DIM


def _linear_kernel(x_ref, w_ref, *refs, splits, n_chunk, dils):
    o_refs, z_ref = refs[:len(splits)], refs[len(splits)]
    xb = x_ref[...].astype(CDT)
    tm = xb.shape[0]
    for o_ref, (c0, c1), dil in zip(o_refs, splits, dils):
        width = c1 - c0
        if dil == 1:
            for a in range(c0, c1, n_chunk):
                b = min(a + n_chunk, c1)
                o_ref[:, a - c0:b - c0] = _dot(xb, w_ref[:, a:b]).astype(o_ref.dtype)
        else:
            z = _dot(xb, w_ref[:, c0:c1])
            for cb in range(width // LANES):
                z_ref[cb] = z[:, cb * LANES:(cb + 1) * LANES]
            for r in range(dil):
                for cb in range(width // LANES):
                    o_ref[:, r * width + cb * LANES:r * width + (cb + 1) * LANES] = (
                        z_ref[cb, pl.ds(r, tm // dil, stride=dil), :].astype(o_ref.dtype))


def _linear(x2d, w, splits, n_chunk, name, dils=None, tm=512):
    t, k = x2d.shape
    tm = min(tm, t)
    n = w.shape[1]
    dils = tuple(dils) if dils is not None else (1,) * len(splits)
    widths = [c1 - c0 for c0, c1 in splits]
    assert all(tm % (16 * dl) == 0 for dl in dils)
    return pl.pallas_call(
        functools.partial(_linear_kernel, splits=tuple(splits), n_chunk=n_chunk, dils=dils),
        grid=(t // tm,),
        in_specs=[pl.BlockSpec((tm, k), lambda i: (i, 0)),
                  pl.BlockSpec((k, n), lambda i: (0, 0))],
        out_specs=[pl.BlockSpec((tm // dl, dl * wd), lambda i: (i, 0)) for wd, dl in zip(widths, dils)],
        out_shape=[jax.ShapeDtypeStruct((t // dl, dl * wd), CDT) for wd, dl in zip(widths, dils)],
        scratch_shapes=[pltpu.VMEM((max(widths) // LANES, tm, LANES), F32)],
        compiler_params=_cparams(("parallel",), 48),
        name=name,
    )(x2d, w)


def _value_t_kernel(x_ref, wt_ref, o0_ref, o1_ref):
    z = _nt_dot(wt_ref[...], x_ref[...].astype(CDT))
    row = lax.broadcasted_iota(jnp.int32, z.shape, 0)
    o0_ref[...] = jnp.where(row < HEAD_DIM, z, jnp.where(row == HEAD_DIM, 1.0, 0.0)).astype(o0_ref.dtype)
    o1_ref[...] = jnp.where(row >= HEAD_DIM, z, jnp.where(row == 0, 1.0, 0.0)).astype(o1_ref.dtype)


def _value_t(x2d, wt, tm, name):
    t, k = x2d.shape
    n = wt.shape[0]
    slab = pl.BlockSpec((None, n, tm), lambda i: (i, 0, 0))
    return pl.pallas_call(
        _value_t_kernel,
        grid=(t // tm,),
        in_specs=[pl.BlockSpec((tm, k), lambda i: (i, 0)), pl.BlockSpec((n, k), lambda i: (0, 0))],
        out_specs=[slab, slab],
        out_shape=[jax.ShapeDtypeStruct((t // tm, n, tm), CDT)] * 2,
        compiler_params=_cparams(("parallel",), 32),
        name=name,
    )(x2d, wt)


def _conv_kernel(x_ref, xh_ref, wc_ref, cw_ref, ob_ref, *, tm):
    i = pl.program_id(1)
    w = CONV_WIDTH
    xb = x_ref[...].astype(CDT)
    z = _dot(xb, wc_ref[...])
    u = z[:, w:2 * w] * z[:, 2 * w:3 * w]
    zh = _dot(xh_ref[...].astype(CDT), wc_ref[:, w:3 * w])
    uh = zh[:, :w] * zh[:, w:]
    uh = jnp.where(i == 0, 0.0, uh)
    row = lax.broadcasted_iota(jnp.int32, (tm, w), 0)
    u1 = jnp.where(row == 0, uh[7:8, :], pltpu.roll(u, 1, 0))
    u2 = jnp.where(row == 0, uh[6:7, :], jnp.where(row == 1, uh[7:8, :], pltpu.roll(u, 2, 0)))
    y = cw_ref[0:1, :] * u2 + cw_ref[1:2, :] * u1 + cw_ref[2:3, :] * u
    ob_ref[...] = (z[:, :w] * y).astype(ob_ref.dtype)


def _conv(x, wc, conv_w, tm=512):
    b, s, d = x.shape
    tm = min(tm, s)
    hb = tm // 8
    return pl.pallas_call(
        functools.partial(_conv_kernel, tm=tm),
        grid=(b, s // tm),
        in_specs=[pl.BlockSpec((None, tm, d), lambda bi, i: (bi, i, 0)),
                  pl.BlockSpec((None, 8, d), lambda bi, i: (bi, jnp.maximum(i * hb - 1, 0), 0)),
                  pl.BlockSpec(wc.shape, lambda bi, i: (0, 0)),
                  pl.BlockSpec(conv_w.shape, lambda bi, i: (0, 0))],
        out_specs=pl.BlockSpec((None, tm, CONV_WIDTH), lambda bi, i: (bi, i, 0)),
        out_shape=jax.ShapeDtypeStruct((b, s, CONV_WIDTH), CDT),
        compiler_params=_cparams(("parallel", "parallel"), 48),
        name="short_conv",
    )(x, x, wc, conv_w)


def _banded_kernel(*refs, window, pr, tq, kw, want_lse, has_sink):
    q_ref, kp_ref, kc_ref, vp_ref, vc_ref = refs[:5]
    n = 5
    sink_ref = None
    if has_sink:
        sink_ref = refs[n]
        n += 1
    o_ref = refs[n]
    n += 1
    lse_ref = None
    if want_lse:
        lse_ref = refs[n]
        n += 1
    kbuf, vbuf, s_ref, e_ref, m_ref, l_ref = refs[n:n + 6]

    i = pl.program_id(2)
    kbuf[0:pr, :] = kp_ref[...]
    kbuf[pr:pr + tq, :] = kc_ref[...]
    vbuf[0:pr, :] = vp_ref[...]
    vbuf[pr:pr + tq, :] = vc_ref[...]

    span = SUB_Q + pr
    qi = lax.broadcasted_iota(jnp.int32, (SUB_Q, span), 0)
    kj = lax.broadcasted_iota(jnp.int32, (SUB_Q, span), 1)
    dist = pr + qi - kj
    band = (dist >= 0) & (dist <= window)
    lo = _half_masks()
    halves = (lo, jnp.logical_not(lo))
    groups = ((0,), (1,), (2,)) if kw == 3 * LANES else ((0, 1, 2),)
    rb = BAND_ROWS
    for sb in range(tq // SUB_Q):
        r0 = sb * SUB_Q
        bias = jnp.where(band & (i * tq + r0 - pr + kj >= 0), 0.0, NEG_INF)
        for grp in groups:
            kc0 = grp[0] * LANES if kw == 3 * LANES else 0
            qs = jnp.concatenate(
                [(jnp.where(hm, q_ref[r0:r0 + SUB_Q, p * LANES:(p + 1) * LANES], 0) * QK_SCALE).astype(CDT)
                 for p in grp for hm in halves], axis=0)
            g0 = 2 * grp[0] * SUB_Q
            s_ref[g0:g0 + qs.shape[0], :] = _nt_dot(qs, kbuf[r0:r0 + span, kc0:kc0 + LANES])
        for c0 in range(0, 6 * SUB_Q, rb):
            rows = slice(c0, c0 + rb)
            s = s_ref[rows, :] + bias[c0 % SUB_Q:c0 % SUB_Q + rb, :]
            m = jnp.max(s, axis=-1, keepdims=True)
            e = jnp.exp(s - m)
            e_ref[rows, :] = e.astype(CDT)
            m_ref[rows, :] = jnp.broadcast_to(m, (rb, LANES))
            l_ref[rows, :] = jnp.broadcast_to(jnp.sum(e, axis=-1, keepdims=True), (rb, LANES))
        for grp in groups:
            kc0 = grp[0] * LANES if kw == 3 * LANES else 0
            g0 = 2 * grp[0] * SUB_Q
            g1 = g0 + 2 * len(grp) * SUB_Q
            l = l_ref[g0:g1, :]
            o = _dot(e_ref[g0:g1, :], vbuf[r0:r0 + span, kc0:kc0 + LANES]) / l
            lse = m_ref[g0:g1, :] + jnp.log(l)
            for n_p, p in enumerate(grp):
                a = 2 * n_p * SUB_Q
                o_pair = jnp.where(lo, o[a:a + SUB_Q], o[a + SUB_Q:a + 2 * SUB_Q])
                lse_pair = jnp.where(lo, lse[a:a + SUB_Q], lse[a + SUB_Q:a + 2 * SUB_Q])
                if has_sink:
                    o_pair = o_pair * jax.nn.sigmoid(lse_pair - sink_ref[:, p * LANES:(p + 1) * LANES])
                o_ref[r0:r0 + SUB_Q, p * LANES:(p + 1) * LANES] = o_pair.astype(o_ref.dtype)
                if want_lse:
                    lse_ref[r0:r0 + SUB_Q, p * LANES:(p + 1) * LANES] = lse_pair


def _banded(qa, ka, va, *, nrep, qcol, kcol, vcol, kw, window, want_lse, sink_row=None, tq=512):
    b, l, _ = qa.shape
    pr = -(-window // SUB_Q) * SUB_Q
    tq = min(max(tq, pr), l)
    assert tq % pr == 0 and l % tq == 0, (tq, pr, l)
    ratio = tq // pr
    qw = 3 * LANES
    in_specs = [
        pl.BlockSpec((None, tq, qw), lambda bi, r, i: (bi, i, qcol(r))),
        pl.BlockSpec((None, pr, kw), lambda bi, r, i: (bi, jnp.maximum(i * ratio - 1, 0), kcol(r))),
        pl.BlockSpec((None, tq, kw), lambda bi, r, i: (bi, i, kcol(r))),
        pl.BlockSpec((None, pr, kw), lambda bi, r, i: (bi, jnp.maximum(i * ratio - 1, 0), vcol(r))),
        pl.BlockSpec((None, tq, kw), lambda bi, r, i: (bi, i, vcol(r))),
    ]
    args = [qa, ka, ka, va, va]
    if sink_row is not None:
        in_specs.append(pl.BlockSpec(sink_row.shape, lambda bi, r, i: (0, 0)))
        args.append(sink_row)
    out_specs = [pl.BlockSpec((None, tq, qw), lambda bi, r, i: (bi, i, r))]
    out_shape = [jax.ShapeDtypeStruct((b, l, nrep * qw), CDT)]
    if want_lse:
        out_specs.append(pl.BlockSpec((None, tq, qw), lambda bi, r, i: (bi, i, r)))
        out_shape.append(jax.ShapeDtypeStruct((b, l, nrep * qw), F32))
    res = pl.pallas_call(
        functools.partial(_banded_kernel, window=window, pr=pr, tq=tq, kw=kw, want_lse=want_lse,
                          has_sink=sink_row is not None),
        grid=(b, nrep, l // tq),
        in_specs=in_specs,
        out_specs=out_specs,
        out_shape=out_shape,
        scratch_shapes=[pltpu.VMEM((pr + tq, kw), ka.dtype), pltpu.VMEM((pr + tq, kw), va.dtype),
                        pltpu.VMEM((6 * SUB_Q, SUB_Q + pr), F32), pltpu.VMEM((6 * SUB_Q, SUB_Q + pr), CDT),
                        pltpu.VMEM((6 * SUB_Q, LANES), F32), pltpu.VMEM((6 * SUB_Q, LANES), F32)],
        compiler_params=_cparams(("parallel", "parallel", "parallel"), 32),
        name=f"banded_w{window}_k{kw}_r{nrep}",
    )(*args)
    return res


def _gelu_tanh(x):
    return 0.5 * x * (1.0 + jnp.tanh(0.7978845608028654 * (x + 0.044715 * (x * x * x))))


def _compress_kernel(x_ref, pa_ref, pb_ref, w1a_ref, w1b_ref, b1_ref, w2_ref, b2_ref, o_ref):
    x = x_ref[...].astype(F32)
    n = x.shape[0]
    a = _dot((x + pa_ref[...]).astype(CDT), w1a_ref[...])
    bm = _dot((x + pb_ref[...]).astype(CDT), w1b_ref[...])
    h = a + pltpu.roll(bm, n - 1, 0) + b1_ref[...]
    o_ref[...] = (_dot(_gelu_tanh(h).astype(CDT), w2_ref[...]) + b2_ref[...]).astype(o_ref.dtype)


def _compress(t, pos, w1, b1, w2, b2):
    b, s, _ = t.shape
    nch = s // CMP_STRIDE
    xw = CMP_STRIDE * LANES
    x = t.reshape(b, nch, xw)
    eye = jnp.eye(NSA_KV_HEADS, dtype=F32)
    w1r = w1.reshape(CMP_BLOCK, HEAD_DIM, CMP_HIDDEN)

    def expand_w1(part):
        return jnp.einsum('tdj,kl->tkdlj', part, eye).reshape(xw, NSA_KV_HEADS * CMP_HIDDEN).astype(CDT)

    def expand_pos(part):
        return jnp.broadcast_to(part[:, None, :], (CMP_STRIDE, NSA_KV_HEADS, HEAD_DIM)).reshape(1, xw)

    w1a, w1b = expand_w1(w1r[:CMP_STRIDE]), expand_w1(w1r[CMP_STRIDE:])
    pa, pb = expand_pos(pos[:CMP_STRIDE]), expand_pos(pos[CMP_STRIDE:])
    b1e = jnp.tile(b1, NSA_KV_HEADS).reshape(1, -1)
    w2e = jnp.einsum('jd,kl->kjld', w2, eye).reshape(NSA_KV_HEADS * CMP_HIDDEN, LANES).astype(CDT)
    b2e = jnp.tile(b2, NSA_KV_HEADS).reshape(1, -1)
    consts = [pa, pb, w1a, w1b, b1e, w2e, b2e]
    return pl.pallas_call(
        _compress_kernel,
        grid=(b,),
        in_specs=[pl.BlockSpec((None, nch, xw), lambda bi: (bi, 0, 0))]
        + [pl.BlockSpec(c.shape, lambda bi: (0, 0)) for c in consts],
        out_specs=pl.BlockSpec((None, nch, LANES), lambda bi: (bi, 0, 0)),
        out_shape=jax.ShapeDtypeStruct((b, nch, LANES), CDT),
        compiler_params=_cparams(("parallel",), 48),
        name="nsa_compress",
    )(x, *consts)


CMP_ROWS = 16


def _cmp_select_kernel(q_ref, kc_ref, vc_ref, ov_ref, o_ref, sel_ref, s_ref, p_ref, hi_ref, lo_ref,
                       *, tq, n_sel, tile0):
    i = pl.program_id(1) + tile0
    ncp = kc_ref.shape[0]
    ns = ov_ref.shape[1]
    lo = _half_masks()
    t_col = i * tq + lax.broadcasted_iota(jnp.int32, (tq, 1), 0)
    blk = lax.broadcasted_iota(jnp.int32, (tq, ns), 1)
    blk_t = lax.broadcasted_iota(jnp.int32, (ns, tq), 0)
    cur = t_col // SEL_BLOCK
    causal = blk <= cur
    forced = (blk == 0) | (blk == cur) | (blk == cur - 1)
    rb = CMP_ROWS
    c_end = lax.broadcasted_iota(jnp.int32, (rb, ncp), 1) * CMP_STRIDE + (CMP_BLOCK - 1)
    o_kv, work_t = [], []
    for kv, hm in enumerate((lo, jnp.logical_not(lo))):
        qs = jnp.concatenate(
            [(jnp.where(hm, q_ref[:, p * LANES:(p + 1) * LANES], 0) * QK_SCALE).astype(CDT) for p in range(3)], axis=0)
        s_ref[...] = _nt_dot(qs, kc_ref[...])
        for r0 in range(0, tq, rb):
            t_rows = i * tq + r0 + lax.broadcasted_iota(jnp.int32, (rb, 1), 0)
            vis_bias = jnp.where(c_end <= t_rows, 0.0, NEG_INF)
            has_visible = t_rows >= CMP_BLOCK - 1
            psum = jnp.zeros((rb, ncp), F32)
            for h in range(3):
                rows = slice(h * tq + r0, h * tq + r0 + rb)
                s = s_ref[rows, :] + vis_bias
                e = jnp.exp(s - jnp.max(s, axis=-1, keepdims=True))
                inv = jnp.where(has_visible, 1.0 / jnp.maximum(jnp.sum(e, axis=-1, keepdims=True), 1e-30), 0.0)
                pn = e * inv
                p_ref[rows, :] = pn.astype(CDT)
                psum = psum + pn
            p_hi = psum.astype(CDT)
            hi_ref[r0:r0 + rb, :] = p_hi
            lo_ref[r0:r0 + rb, :] = (psum - p_hi.astype(F32)).astype(CDT)
        o = _dot(p_ref[...], vc_ref[...])
        o_kv.append([o[p * tq:(p + 1) * tq] for p in range(3)])
        imp = _dot(hi_ref[...], ov_ref[...]) + _dot(lo_ref[...], ov_ref[...])
        work_t.append(jnp.where(causal & jnp.logical_not(forced), imp, -BIG).T)

    blk_lanes = blk_t[:, 0:LANES]

    def pick(_, work):
        m = jnp.max(work, axis=0, keepdims=True)
        idx = jnp.min(jnp.where(work == m, blk_lanes, ns), axis=0, keepdims=True)
        return jnp.where(blk_lanes == idx, -2.0 * BIG, work)

    for kv, start in enumerate(work_t):
        done = jnp.concatenate([lax.fori_loop(0, n_sel - 3, pick, start[:, c0:c0 + LANES])
                                for c0 in range(0, tq, LANES)], axis=1)
        taken = jnp.where((done < -BIG) & (start > -BIG), 1.0, 0.0).T
        selb = jnp.where(forced | (taken > 0.5), 0.0, NEG_INF)
        sel_ref[:, kv * ns:(kv + 1) * ns] = selb.astype(sel_ref.dtype)
    for p in range(3):
        o_ref[:, p * LANES:(p + 1) * LANES] = jnp.where(lo, o_kv[0][p], o_kv[1][p]).astype(o_ref.dtype)


CMP_CAUSAL_SPLITS = 4


def _cmp_select(zr, qcol, kc, vc, tq=256):
    b, s, _ = zr.shape
    ncp = kc.shape[1]
    ns = s // SEL_BLOCK
    n_sel = min(N_SEL, ns)
    assert n_sel >= 3, "selection needs room for the three forced blocks"
    tq = min(tq, s)
    c = np.arange(ncp)[:, None] * CMP_STRIDE
    j = np.arange(ns)[None, :] * SEL_BLOCK
    overlap = ((c < j + SEL_BLOCK) & (c + CMP_BLOCK - 1 >= j)).astype(np.float32)
    overlap[ncp - 1:, :] = 0.0
    ov = jnp.asarray(overlap, CDT)
    qw = 3 * LANES
    n_split = CMP_CAUSAL_SPLITS if (s // tq) % CMP_CAUSAL_SPLITS == 0 and ncp % (16 * CMP_CAUSAL_SPLITS) == 0 else 1
    tiles = s // tq // n_split
    outs, sels = [], []
    for part in range(n_split):
        tile0 = part * tiles
        ncp_part = ncp * (part + 1) // n_split
        o_part, sel_part = pl.pallas_call(
            functools.partial(_cmp_select_kernel, tq=tq, n_sel=n_sel, tile0=tile0),
            grid=(b, tiles),
            in_specs=[pl.BlockSpec((None, tq, qw), lambda bi, i, tile0=tile0: (bi, i + tile0, qcol)),
                      pl.BlockSpec((None, ncp_part, LANES), lambda bi, i: (bi, 0, 0)),
                      pl.BlockSpec((None, ncp_part, LANES), lambda bi, i: (bi, 0, 0)),
                      pl.BlockSpec((ncp_part, ns), lambda bi, i: (0, 0))],
            out_specs=[pl.BlockSpec((None, tq, qw), lambda bi, i: (bi, i, 0)),
                       pl.BlockSpec((None, tq, 2 * ns), lambda bi, i: (bi, i, 0))],
            out_shape=[jax.ShapeDtypeStruct((b, tiles * tq, qw), CDT),
                       jax.ShapeDtypeStruct((b, tiles * tq, 2 * ns), CDT)],
            scratch_shapes=[pltpu.VMEM((3 * tq, ncp_part), F32), pltpu.VMEM((3 * tq, ncp_part), CDT),
                            pltpu.VMEM((tq, ncp_part), CDT), pltpu.VMEM((tq, ncp_part), CDT)],
            compiler_params=_cparams(("parallel", "parallel"), 48),
            name=f"nsa_cmp_select_p{part}",
        )(zr, kc, vc, ov)
        outs.append(o_part)
        sels.append(sel_part)
    return jnp.concatenate(outs, axis=1), jnp.concatenate(sels, axis=1)


SLC_KEY_CHUNK = 256


def _slc_kernel(q_ref, k0_ref, k1_ref, vt0_ref, vt1_ref, selb_ref, o_ref, s_ref, e_ref, m_ref, a_ref, acc_ref,
                *, tq, tk):
    i = pl.program_id(1)
    ns = selb_ref.shape[1] // 2
    vt_refs = (vt0_ref, vt1_ref)
    den_row = (HEAD_DIM, 0)
    bpt = tk // SEL_BLOCK
    n_q = 3 * tq
    rc = min(SLC_KEY_CHUNK, tk)
    lo = _half_masks()
    n_tiles = ((i + 1) * tq + tk - 1) // tk
    p_row = lax.broadcasted_iota(jnp.int32, (ns, LANES), 0)
    p_col = lax.broadcasted_iota(jnp.int32, (ns, LANES), 1)
    groups = []
    for kv, (hm, k_ref) in enumerate(((lo, k0_ref), (jnp.logical_not(lo), k1_ref))):
        q3 = [(jnp.where(hm, q_ref[:, p * LANES:(p + 1) * LANES], 0) * QK_SCALE).astype(CDT) for p in range(3)]
        selb = selb_ref[:, kv * ns:(kv + 1) * ns]
        lane0 = HEAD_DIM if kv == 0 else 0
        groups.append((kv, hm, k_ref, q3, selb, lane0))
    m_ref[...] = jnp.full(m_ref.shape, NEG_INF, F32)
    acc_ref[...] = jnp.zeros(acc_ref.shape, F32)

    def scores(j):
        k0 = pl.multiple_of(j * tk, tk)
        for kv, hm, k_ref, q3, selb, lane0 in groups:
            place = ((p_col >= lane0) & (p_col < lane0 + bpt) & (p_row == p_col - lane0 + j * bpt)).astype(CDT)
            sb = _dot(selb, place).astype(CDT)
            qp = jnp.concatenate([jnp.where(hm, q, sb) for q in q3], axis=0)
            s_ref[kv] = _nt_dot(k_ref[pl.ds(k0, tk), :], qp)

    def softmax_pv(j, diagonal):
        k0 = pl.multiple_of(j * tk, tk)
        for kv in range(2):
            for c0 in range(0, n_q, LANES):
                cols = slice(c0, c0 + LANES)
                t_lane = i * tq + (c0 % tq) + lax.broadcasted_iota(jnp.int32, (1, LANES), 1)

                def chunk(r0):
                    s = s_ref[kv, r0:r0 + rc, cols]
                    if diagonal:
                        kpos = k0 + r0 + lax.broadcasted_iota(jnp.int32, (rc, LANES), 0)
                        s = jnp.where(kpos <= t_lane, s, NEG_INF)
                    return s

                m8 = m_ref[kv, :, cols]
                for r0 in range(0, tk, rc):
                    m8 = jnp.maximum(m8, jnp.max(chunk(r0).reshape(rc // 8, 8, LANES), axis=0))
                m_new = jnp.max(m8, axis=0, keepdims=True)
                a_ref[kv, :, cols] = jnp.exp(m_ref[kv, :, cols] - m_new)
                m_ref[kv, :, cols] = jnp.broadcast_to(m_new, (8, LANES))
                for r0 in range(0, tk, rc):
                    e_ref[kv, r0:r0 + rc, cols] = jnp.exp((chunk(r0) - m_new).astype(CDT))
        for kv in range(2):
            acc_ref[kv] = a_ref[kv, 0:1, :] * acc_ref[kv] + _dot(vt_refs[kv][j], e_ref[kv])

    def tile(j, diagonal):
        scores(j)
        softmax_pv(j, diagonal)

    lax.fori_loop(0, n_tiles - 1, lambda j, c: (tile(j, False), c)[1], 0)
    tile(n_tiles - 1, True)
    o0, o1 = ((acc_ref[kv] / acc_ref[kv, den_row[kv]:den_row[kv] + 1, :]).T for kv in range(2))
    for p in range(3):
        o_ref[:, p * LANES:(p + 1) * LANES] = jnp.where(
            lo, o0[p * tq:(p + 1) * tq], o1[p * tq:(p + 1) * tq]).astype(o_ref.dtype)


def _slc(zr, qcol, ksc, vts, selb, tq=512):
    b, s, _ = zr.shape
    tq = min(tq, s)
    tk = vts[0].shape[2]
    assert tq % LANES == 0 and tk // SEL_BLOCK <= HEAD_DIM and s % tk == 0
    qw = 3 * LANES
    ns2 = selb.shape[2]
    pat = jax.nn.one_hot((jnp.arange(s) // SEL_BLOCK) % (tk // SEL_BLOCK), HEAD_DIM, dtype=ksc.dtype)
    pat = jnp.broadcast_to(pat[None], (b, s, HEAD_DIM))
    k0 = jnp.concatenate([ksc[..., :HEAD_DIM], pat], axis=-1)
    k1 = jnp.concatenate([pat, ksc[..., HEAD_DIM:]], axis=-1)
    full = lambda bi, i: (bi, 0, 0)
    return pl.pallas_call(
        functools.partial(_slc_kernel, tq=tq, tk=tk),
        grid=(b, s // tq),
        in_specs=[pl.BlockSpec((None, tq, qw), lambda bi, i: (bi, i, qcol)),
                  _single_buffered((None, s, LANES), full),
                  _single_buffered((None, s, LANES), full),
                  _single_buffered((s // tk, LANES, tk), lambda bi, i: (bi, 0, 0)),
                  _single_buffered((s // tk, LANES, tk), lambda bi, i: (bi, 0, 0)),
                  pl.BlockSpec((None, tq, ns2), lambda bi, i: (bi, i, 0))],
        out_specs=pl.BlockSpec((None, tq, qw), lambda bi, i: (bi, i, 0)),
        out_shape=jax.ShapeDtypeStruct((b, s, qw), CDT),
        scratch_shapes=[pltpu.VMEM((2, tk, 3 * tq), F32), pltpu.VMEM((2, tk, 3 * tq), CDT),
                        pltpu.VMEM((2, 8, 3 * tq), F32), pltpu.VMEM((2, 8, 3 * tq), F32),
                        pltpu.VMEM((2, LANES, 3 * tq), F32)],
        compiler_params=_cparams(("parallel", "arbitrary"), 60),
        name="nsa_slc",
    )(zr, k0, k1, vts[0], vts[1], selb)


def _merge_kernel(x_ref, oa0, oa1, oa2, la0, la1, la2, ob, ocmp, oslc, owin, od,
                  wn_ref, wg_ref, bg_ref, wb_ref, wo_ref, g_ref, b_ref, o_ref, *scratch, alpha):
    x = x_ref[...]
    xb = x.astype(CDT)
    bw = BRANCH_WIDTH

    def token_rows(src_ref, dst_ref):
        dil = src_ref.shape[1] // bw
        if dil == 1:
            return src_ref[...].astype(F32)
        n_cb = bw // LANES
        for r in range(dil):
            for cb in range(n_cb):
                c0 = r * bw + cb * LANES
                dst_ref[cb, pl.ds(r, src_ref.shape[0], stride=dil), :] = src_ref[:, c0:c0 + LANES].astype(F32)
        return jnp.concatenate([dst_ref[cb] for cb in range(n_cb)], axis=1)

    o0, o1, o2 = (token_rows(s, d) for s, d in zip((oa0, oa1, oa2), scratch[0:3]))
    l0, l1, l2 = (token_rows(s, d) for s, d in zip((la0, la1, la2), scratch[3:6]))
    mx = jnp.maximum(jnp.maximum(l0, l1), l2)
    w0, w1, w2 = jnp.exp(l0 - mx), jnp.exp(l1 - mx), jnp.exp(l2 - mx)
    o_a = (w0 * o0 + w1 * o1 + w2 * o2) / (w0 + w1 + w2)
    gates = jax.nn.sigmoid(_dot(xb, wn_ref[...]))
    o_c = (gates[:, 0:bw] * ocmp[...].astype(F32) + gates[:, bw:2 * bw] * oslc[...].astype(F32)
           + gates[:, 2 * bw:3 * bw] * owin[...].astype(F32))
    branches = (o_a.astype(CDT), ob[...], o_c.astype(CDT), od[...])
    d = x.shape[1]
    merged = jnp.zeros(x.shape, F32)
    for m in range(N_BRANCH):
        gate = jax.nn.sigmoid(_dot(xb, wg_ref[:, m * d:(m + 1) * d]) + bg_ref[:, m * d:(m + 1) * d])
        merged = merged + gate * _dot(branches[m], wb_ref[m])
    r = alpha * x + _dot(merged.astype(CDT), wo_ref[...])
    o_ref[...] = _layer_norm(r, g_ref[...], b_ref[...])


def _merge(x2d, branch_inputs, wn, wg, bg, wb, wo, g, b, alpha, tm=256):
    t, d = x2d.shape
    tm = min(tm, t)
    row = lambda i: (i, 0)
    const2 = lambda i: (0, 0)
    in_specs = [pl.BlockSpec((tm, d), row)]
    in_specs += [pl.BlockSpec((tm * a.shape[0] // t, a.shape[1]), row) for a in branch_inputs]
    in_specs += [pl.BlockSpec(wn.shape, const2), pl.BlockSpec(wg.shape, const2), pl.BlockSpec(bg.shape, const2),
                 pl.BlockSpec(wb.shape, lambda i: (0, 0, 0)), pl.BlockSpec(wo.shape, const2),
                 pl.BlockSpec(g.shape, const2), pl.BlockSpec(b.shape, const2)]
    return pl.pallas_call(
        functools.partial(_merge_kernel, alpha=alpha),
        grid=(t // tm,),
        in_specs=in_specs,
        out_specs=pl.BlockSpec((tm, d), row),
        out_shape=jax.ShapeDtypeStruct((t, d), F32),
        scratch_shapes=[pltpu.VMEM((BRANCH_WIDTH // LANES, tm, LANES), F32) for _ in range(2 * N_DIL)],
        compiler_params=_cparams(("parallel",), 56),
        name="merge_ln",
    )(x2d, *branch_inputs, wn, wg, bg, wb, wo, g, b)


def _ple_ln(x, xb, f, p, plw_ref, pgw_ref, pgb_ref, g_ref, b_ref, alpha):
    ple = jax.nn.sigmoid(_dot(xb, pgw_ref[...]) + pgb_ref[...]) * _dot(p.astype(CDT), plw_ref[...])
    return _layer_norm(alpha * x + f + ple, g_ref[...], b_ref[...])


FFN_CHUNK = 512


def _ffn_kernel(x_ref, p_ref, wg_ref, wu_ref, wd_ref, plw_ref, pgw_ref, pgb_ref, g_ref, b_ref, o_ref, h_ref, *, alpha):
    x = x_ref[...]
    xb = x.astype(CDT)
    dff = wg_ref.shape[1]
    for c0 in range(0, dff, FFN_CHUNK):
        cols = slice(c0, min(c0 + FFN_CHUNK, dff))
        h_ref[:, cols] = (jax.nn.silu(_dot(xb, wg_ref[:, cols])) * _dot(xb, wu_ref[:, cols])).astype(CDT)
    f = _dot(h_ref[...], wd_ref[...])
    o_ref[...] = _ple_ln(x, xb, f, p_ref[...], plw_ref, pgw_ref, pgb_ref, g_ref, b_ref, alpha)


def _ffn(x2d, p2d, wg, wu, wd, plw, pgw, pgb, g, b, alpha, tm=512):
    t, d = x2d.shape
    tm = min(tm, t)
    dff = wg.shape[1]
    wg, wu, wd = wg.astype(CDT), wu.astype(CDT), wd.astype(CDT)
    row = lambda i: (i, 0)
    const = lambda shape: pl.BlockSpec(shape, lambda i: (0, 0), pipeline_mode=pl.Buffered(1))
    return pl.pallas_call(
        functools.partial(_ffn_kernel, alpha=alpha),
        grid=(t // tm,),
        in_specs=[pl.BlockSpec((tm, d), row), pl.BlockSpec((tm, p2d.shape[1]), row),
                  const(wg.shape), const(wu.shape), const(wd.shape),
                  const(plw.shape), const(pgw.shape), const(pgb.shape), const(g.shape), const(b.shape)],
        out_specs=pl.BlockSpec((tm, d), row),
        out_shape=jax.ShapeDtypeStruct((t, d), F32),
        scratch_shapes=[pltpu.VMEM((tm, dff), CDT)],
        compiler_params=_cparams(("parallel",), 48),
        name="ffn_ple_ln",
    )(x2d, p2d, wg, wu, wd, plw, pgw, pgb, g, b)


def _router_kernel(x_ref, wh_ref, wl_ref, b_ref, comb_ref, rank_ref, rank_t_ref, cnt_ref):
    x = x_ref[...]
    xh = x.astype(CDT)
    xl = (x - xh.astype(F32)).astype(CDT)
    logits = _dot(xh, wh_ref[...]) + _dot(xh, wl_ref[...]) + _dot(xl, wh_ref[...]) + b_ref[...]
    lane = lax.broadcasted_iota(jnp.int32, logits.shape, 1)
    v1 = jnp.max(logits, axis=-1, keepdims=True)
    i1 = jnp.min(jnp.where(logits == v1, lane, LANES), axis=-1, keepdims=True)
    rest = jnp.where(lane == i1, -jnp.inf, logits)
    v2 = jnp.max(rest, axis=-1, keepdims=True)
    i2 = jnp.min(jnp.where(rest == v2, lane, LANES), axis=-1, keepdims=True)
    e2 = jnp.exp(v2 - v1)
    comb_ref[...] = jnp.where(lane == i1, 1.0 / (1.0 + e2), 0.0) + jnp.where(lane == i2, e2 / (1.0 + e2), 0.0)
    routed = (lane == i1) | (lane == i2)
    mask = routed.astype(CDT)
    tm = x.shape[0]
    before = (lax.broadcasted_iota(jnp.int32, (tm, tm), 1) < lax.broadcasted_iota(jnp.int32, (tm, tm), 0)).astype(CDT)
    rank = jnp.where(routed, _dot(before, mask), -1.0)
    rank_ref[...] = rank
    rank_t_ref[...] = rank.T[0:rank_t_ref.shape[0], :]
    cnt_ref[...] = jnp.sum(routed.astype(F32), axis=0, keepdims=True).astype(jnp.int32)


def _router(x2d, w_router, b_router, tm):
    t, d = x2d.shape
    ne = w_router.shape[1]
    wp = jnp.zeros((d, LANES), F32).at[:, :ne].set(w_router)
    wh = wp.astype(CDT)
    wl = (wp - wh.astype(F32)).astype(CDT)
    bp = jnp.full((1, LANES), -BIG, F32).at[0, :ne].set(b_router)
    nt = t // tm
    row = lambda i: (i, 0)
    return pl.pallas_call(
        _router_kernel,
        grid=(nt,),
        in_specs=[pl.BlockSpec((tm, d), row), pl.BlockSpec(wh.shape, lambda i: (0, 0)),
                  pl.BlockSpec(wl.shape, lambda i: (0, 0)), pl.BlockSpec(bp.shape, lambda i: (0, 0))],
        out_specs=[pl.BlockSpec((tm, LANES), row), pl.BlockSpec((tm, LANES), row),
                   pl.BlockSpec((None, 8, tm), lambda i: (i, 0, 0)),
                   pl.BlockSpec((None, 1, LANES), lambda i: (i, 0, 0))],
        out_shape=[jax.ShapeDtypeStruct((t, LANES), F32), jax.ShapeDtypeStruct((t, LANES), F32),
                   jax.ShapeDtypeStruct((nt, 8, tm), F32), jax.ShapeDtypeStruct((nt, 1, LANES), jnp.int32)],
        compiler_params=_cparams(("parallel",), 40),
        name="moe_router",
    )(x2d, wh, wl, bp)


def _moe_kernel(cnt_ref, x_ref, comb_ref, rank_ref, rank_t_ref, p_ref, wg_ref, wu_ref, wd_ref, plw_ref, pgw_ref,
                pgb_ref, g_ref, b_ref, o_ref, xe_ref, ye_ref, *, alpha, rs, seg):
    i = pl.program_id(0)
    e = pl.program_id(1)
    c = pl.program_id(2)
    n_seg = x_ref.shape[0] // seg
    last_chunk = c == pl.num_programs(2) - 1

    def rows(sc):
        return pl.ds(pl.multiple_of(sc * rs, 8), rs)

    @pl.when((e == 0) & (c == 0))
    def _():
        o_ref[...] = jnp.zeros_like(o_ref)

    for sg in range(n_seg):
        tok = slice(sg * seg, (sg + 1) * seg)
        n_groups = (cnt_ref[(i * n_seg + sg) * LANES + e] + rs - 1) // rs

        @pl.when(c == 0)
        def _():
            xb = x_ref[tok, :].astype(CDT)
            rank_row = rank_t_ref[sg, pl.ds(e, 1), :]
            row_id = lax.broadcasted_iota(jnp.int32, (rs, seg), 0).astype(F32)

            def gather(sc, _):
                onehot = (rank_row - (sc * rs).astype(F32) == row_id).astype(CDT)
                xe_ref[sg, rows(sc), :] = _dot(onehot, xb).astype(CDT)
                ye_ref[sg, rows(sc), :] = jnp.zeros((rs, ye_ref.shape[2]), F32)
                return 0

            lax.fori_loop(0, n_groups, gather, 0)

        def expert(sc, _):
            xs = xe_ref[sg, rows(sc), :]
            h = jax.nn.silu(_dot(xs, wg_ref[0])) * _dot(xs, wu_ref[0])
            ye_ref[sg, rows(sc), :] += _dot(h.astype(CDT), wd_ref[0])
            return 0

        lax.fori_loop(0, n_groups, expert, 0)

        @pl.when(last_chunk)
        def _():
            lane = lax.broadcasted_iota(jnp.int32, (seg, LANES), 1)
            mine = lane == e
            cw = jnp.sum(jnp.where(mine, comb_ref[tok, :], 0.0), axis=-1, keepdims=True)
            rank_col = jnp.sum(jnp.where(mine, rank_ref[tok, :], 0.0), axis=-1, keepdims=True)
            col_id = lax.broadcasted_iota(jnp.int32, (seg, rs), 1).astype(F32)

            def scatter(sc, _):
                onehot = (rank_col - (sc * rs).astype(F32) == col_id).astype(CDT)
                o_ref[tok, :] += cw * _dot(onehot, ye_ref[sg, rows(sc), :].astype(CDT))
                return 0

            lax.fori_loop(0, n_groups, scatter, 0)

    @pl.when((e == pl.num_programs(1) - 1) & last_chunk)
    def _():
        for sg in range(n_seg):
            tok = slice(sg * seg, (sg + 1) * seg)
            x = x_ref[tok, :]
            o_ref[tok, :] = _ple_ln(x, x.astype(CDT), o_ref[tok, :], p_ref[tok, :], plw_ref, pgw_ref, pgb_ref,
                                    g_ref, b_ref, alpha)


MOE_CHUNK = 512
MOE_SEGMENT = 1024
MOE_SEGMENTS_PER_TILE = 2
MOE_ROW_GROUP = 288


def _single_buffered(shape, index_map):
    return pl.BlockSpec(shape, index_map, pipeline_mode=pl.Buffered(1))


def _moe(x2d, routing, p2d, wg, wu, wd, plw, pgw, pgb, g, b, alpha, seg):
    comb, rank, rank_t, cnt = routing
    t, d = x2d.shape
    ne, _, dff = wg.shape
    ck = min(MOE_CHUNK, dff)
    rs = min(MOE_ROW_GROUP, seg)
    n_seg = min(MOE_SEGMENTS_PER_TILE, t // seg)
    tm = n_seg * seg
    max_rows = -(-seg // rs) * rs
    wg, wu, wd = wg.astype(CDT), wu.astype(CDT), wd.astype(CDT)
    row = lambda i, e, c, cnt: (i, 0)
    c2 = lambda i, e, c, cnt: (0, 0)
    grid_spec = pltpu.PrefetchScalarGridSpec(
        num_scalar_prefetch=1,
        grid=(t // tm, ne, dff // ck),
        in_specs=[_single_buffered((tm, d), row), _single_buffered((tm, LANES), row),
                  _single_buffered((tm, LANES), row),
                  _single_buffered((n_seg, 8, seg), lambda i, e, c, cnt: (i, 0, 0)),
                  _single_buffered((tm, p2d.shape[1]), row),
                  pl.BlockSpec((1, d, ck), lambda i, e, c, cnt: (e, 0, c)),
                  pl.BlockSpec((1, d, ck), lambda i, e, c, cnt: (e, 0, c)),
                  pl.BlockSpec((1, ck, d), lambda i, e, c, cnt: (e, c, 0)),
                  _single_buffered(plw.shape, c2), _single_buffered(pgw.shape, c2), _single_buffered(pgb.shape, c2),
                  _single_buffered(g.shape, c2), _single_buffered(b.shape, c2)],
        out_specs=pl.BlockSpec((tm, d), row),
        scratch_shapes=[pltpu.VMEM((n_seg, max_rows, d), CDT), pltpu.VMEM((n_seg, max_rows, d), F32)],
    )
    return pl.pallas_call(
        functools.partial(_moe_kernel, alpha=alpha, rs=rs, seg=seg),
        grid_spec=grid_spec,
        out_shape=jax.ShapeDtypeStruct((t, d), F32),
        compiler_params=_cparams(("parallel", "arbitrary", "arbitrary"), 60),
        name="moe_ple_ln",
    )(cnt.reshape(-1), x2d, comb, rank, rank_t, p2d, wg, wu, wd, plw, pgw, pgb, g, b)


def _prep_in_weights(w_in):
    o = COL_OFF
    bw = BRANCH_WIDTH
    cols = lambda n: w_in[:, o[n]:o[n + 1]]
    qa, ka, va = cols(0), cols(1), cols(2)
    w_dil = jnp.concatenate(
        [t[:, g * bw:(g + 1) * bw] for g in range(N_DIL) for t in (qa, ka, va)], axis=1).astype(CDT)
    w_conv = jnp.concatenate([cols(3), cols(4), cols(5)], axis=1).astype(CDT)
    gn = cols(13)
    w_gate = jnp.concatenate([gn[:, br * NSA_Q_HEADS + GQA_COL_HEAD] for br in range(3)], axis=1).astype(CDT)
    w_rest = jnp.concatenate([cols(6)[:, GQA_COL_PERM], cols(14)[:, GQA_COL_PERM], cols(11),
                              cols(12), cols(15), cols(16), cols(9), cols(7), cols(8)], axis=1).astype(CDT)
    wt_vsc = cols(10).T.astype(CDT)
    return w_dil, w_conv, w_gate, w_rest, wt_vsc


ZR_Q_NSA, ZR_Q_SWA = 0, 1
ZR_KWC, ZR_VWC, ZR_KD, ZR_VD = 6, 7, 8, 9
ZR_WIDTH = 2 * BRANCH_WIDTH + 4 * LANES
SLC_KEY_TILE = 1024


def _token_mixers(x, w_in, conv_w, cmp_pos, cmp_w1, cmp_b1, cmp_w2, cmp_b2, sinks):
    b, s, d = x.shape
    x2d = x.reshape(b * s, d)
    w_dil, w_conv, w_gate, w_rest, wt_vsc = _prep_in_weights(w_in)
    gw = 3 * BRANCH_WIDTH

    z_dil = _linear(x2d, w_dil, [(g * gw, (g + 1) * gw) for g in range(N_DIL)], gw, "in_proj_dil",
                    dils=[dil for _, dil in DIL_PATTERNS])
    zr, ksc, kcc, vcc = _linear(x2d, w_rest, [(0, ZR_WIDTH)] + [(ZR_WIDTH + n * LANES, ZR_WIDTH + (n + 1) * LANES)
                                                           for n in range(3)], 256, "in_proj_rest")
    zr = zr.reshape(b, s, ZR_WIDTH)
    o_b = _conv(x, w_conv, conv_w)

    dil_o, dil_lse = [], []
    for g, (window, dil) in enumerate(DIL_PATTERNS):
        view = z_dil[g].reshape(b, s // dil, dil * gw)
        og, lg = _banded(view, view, view, nrep=dil,
                         qcol=lambda r: 3 * r, kcol=lambda r: 3 * r + 1, vcol=lambda r: 3 * r + 2,
                         kw=3 * LANES, window=window // dil, want_lse=True)
        dil_o.append(og.reshape(b * s // dil, dil * BRANCH_WIDTH))
        dil_lse.append(lg.reshape(b * s // dil, dil * BRANCH_WIDTH))

    kc = _compress(kcc.reshape(b, s, LANES), cmp_pos[0], cmp_w1[0], cmp_b1[0], cmp_w2[0], cmp_b2[0])
    vc = _compress(vcc.reshape(b, s, LANES), cmp_pos[1], cmp_w1[1], cmp_b1[1], cmp_w2[1], cmp_b2[1])
    o_cmp, selb = _cmp_select(zr, ZR_Q_NSA, kc, vc)
    vsc_t = _value_t(x2d, wt_vsc, min(SLC_KEY_TILE, s), "in_proj_vsc_t")
    o_slc = _slc(zr, ZR_Q_NSA, ksc.reshape(b, s, LANES), vsc_t, selb)
    (o_win,) = _banded(zr, zr, zr, nrep=1, qcol=lambda r: ZR_Q_NSA, kcol=lambda r: ZR_KWC, vcol=lambda r: ZR_VWC,
                       kw=LANES, window=NSA_WINDOW - 1, want_lse=False, tq=512)

    sink_row = sinks.astype(F32)[GQA_COL_HEAD].reshape(1, BRANCH_WIDTH)
    (o_d,) = _banded(zr, zr, zr, nrep=1, qcol=lambda r: ZR_Q_SWA, kcol=lambda r: ZR_KD, vcol=lambda r: ZR_VD,
                     kw=LANES, window=SWA_WINDOW - 1, want_lse=False, sink_row=sink_row)

    t = b * s
    flat = lambda a: a.reshape(t, a.shape[-1])
    return [dil_o[0], dil_o[1], dil_o[2], dil_lse[0], dil_lse[1], dil_lse[2], flat(o_b), flat(o_cmp), flat(o_slc),
            flat(o_win), flat(o_d)], w_gate


def kernel(x, p, w_in, conv_w, cmp_pos, cmp_w1, cmp_b1, cmp_w2, cmp_b2, sinks, w_branch, w_merge_gate, b_merge_gate, w_out, ln_mix_g, ln_mix_b, ffn_w_gate, ffn_w_up, ffn_w_down, w_router, b_router, moe_w_gate, moe_w_up, moe_w_down, ple_w, ple_gate_w, ple_gate_b, ln_ffn_g, ln_ffn_b):
    depth, b, s, _ = p.shape
    d = x.shape[-1]
    t = b * s
    alpha = (2 * depth) ** 0.25
    row = lambda v: v.reshape(1, -1).astype(F32)
    for i in range(depth):
        branch_inputs, w_nsa_gate = _token_mixers(x, w_in[i], conv_w[i], cmp_pos[i], cmp_w1[i], cmp_b1[i], cmp_w2[i],
                                                  cmp_b2[i], sinks[i])
        wg = jnp.concatenate([w_merge_gate[i, m] for m in range(N_BRANCH)], axis=1).astype(CDT)
        bg = b_merge_gate[i].reshape(1, N_BRANCH * d).astype(F32)
        wb = jnp.stack([w_branch[i, 0], w_branch[i, 1], w_branch[i, 2][GQA_COL_PERM],
                        w_branch[i, 3][GQA_COL_PERM]]).astype(CDT)
        x1 = _merge(x.reshape(t, d), branch_inputs, w_nsa_gate, wg, bg, wb, w_out[i].astype(CDT),
                    row(ln_mix_g[i]), row(ln_mix_b[i]), alpha)
        p2d = p[i].reshape(t, -1)
        ple_args = (ple_w[i].astype(CDT), ple_gate_w[i].astype(CDT), row(ple_gate_b[i]),
                    row(ln_ffn_g[i]), row(ln_ffn_b[i]))
        j = i // 2
        if i % 2 == 0:
            x2 = _ffn(x1, p2d, ffn_w_gate[j], ffn_w_up[j], ffn_w_down[j], *ple_args, alpha)
        else:
            seg = min(MOE_SEGMENT, t)
            routing = _router(x1, w_router[j], b_router[j], seg)
            x2 = _moe(x1, routing, p2d, moe_w_gate[j], moe_w_up[j], moe_w_down[j], *ple_args, alpha, seg)
        x = x2.reshape(b, s, d)
    return x
```

```python
import functools

import numpy as np
import jax
import jax.numpy as jnp
from jax import lax
from jax.experimental import pallas as pl
from jax.experimental.pallas import tpu as pltpu

D_MODEL = 1024
HEAD_DIM = 64
DIL_PATTERNS = ((128, 1), (512, 4), (2048, 16))
N_DIL = 3
DIL_HEADS = 6
CONV_WIDTH = 384
CONV_K = 3
NSA_Q_HEADS = 6
NSA_KV_HEADS = 2
CMP_BLOCK = 32
CMP_STRIDE = 16
CMP_HIDDEN = 128
SEL_BLOCK = 64
N_SEL = 16
NSA_WINDOW = 512
SWA_Q_HEADS = 6
SWA_WINDOW = 128
BRANCH_WIDTH = 384
N_BRANCH = 4
N_EXPERTS = 8
LN_EPS = 1e-5
NEG_INF = -1e30
DIL_WIDTH = N_DIL * DIL_HEADS * HEAD_DIM
COLUMN_SIZES = (DIL_WIDTH, DIL_WIDTH, DIL_WIDTH, CONV_WIDTH, CONV_WIDTH, CONV_WIDTH,
                NSA_Q_HEADS * HEAD_DIM, 128, 128, 128, 128, 128, 128, 3 * NSA_Q_HEADS,
                SWA_Q_HEADS * HEAD_DIM, 128, 128)
COL_OFF = np.concatenate([[0], np.cumsum(COLUMN_SIZES)]).tolist()

LANES = 128
V7X_VMEM_BYTES = 64 * 1024 * 1024
MIB = 1024 * 1024

CDT = jnp.bfloat16
F32 = jnp.float32
QK_SCALE = HEAD_DIM ** -0.5
SUB_Q = 128
BAND_ROWS = 64
BIG = 1e30

_GQA_HEAD_ORDER = (0, 3, 1, 4, 2, 5)
GQA_COL_PERM = np.concatenate([np.arange(h * HEAD_DIM, (h + 1) * HEAD_DIM) for h in _GQA_HEAD_ORDER])
GQA_COL_HEAD = GQA_COL_PERM // HEAD_DIM


def _cparams(sem, vmem_mib):
    return pltpu.CompilerParams(dimension_semantics=sem, vmem_limit_bytes=int(vmem_mib * MIB))


def _nt_dot(a, b):
    return lax.dot_general(a, b, (((1,), (1,)), ((), ())), preferred_element_type=F32)


def _dot(a, b):
    return jnp.dot(a, b, preferred_element_type=F32)


def _layer_norm(r, g, b):
    mu = jnp.mean(r, axis=-1, keepdims=True)
    d = r - mu
    var = jnp.mean(d * d, axis=-1, keepdims=True)
    return d * lax.rsqrt(var + LN_EPS) * g + b


def _half_masks():
    lane = lax.broadcasted_iota(jnp.int32, (1, LANES), 1)
    return lane < HEAD_DIM


def _linear_kernel(x_ref, w_ref, *refs, splits, n_chunk, dils):
    o_refs, z_ref = refs[:len(splits)], refs[len(splits)]
    xb = x_ref[...].astype(CDT)
    tm = xb.shape[0]
    for o_ref, (c0, c1), dil in zip(o_refs, splits, dils):
        width = c1 - c0
        if dil == 1:
            for a in range(c0, c1, n_chunk):
                b = min(a + n_chunk, c1)
                o_ref[:, a - c0:b - c0] = _dot(xb, w_ref[:, a:b]).astype(o_ref.dtype)
        else:
            z = _dot(xb, w_ref[:, c0:c1])
            for cb in range(width // LANES):
                z_ref[cb] = z[:, cb * LANES:(cb + 1) * LANES]
            for r in range(dil):
                for cb in range(width // LANES):
                    o_ref[:, r * width + cb * LANES:r * width + (cb + 1) * LANES] = (
                        z_ref[cb, pl.ds(r, tm // dil, stride=dil), :].astype(o_ref.dtype))


def _linear(x2d, w, splits, n_chunk, name, dils=None, tm=512):
    t, k = x2d.shape
    tm = min(tm, t)
    n = w.shape[1]
    dils = tuple(dils) if dils is not None else (1,) * len(splits)
    widths = [c1 - c0 for c0, c1 in splits]
    assert all(tm % (16 * dl) == 0 for dl in dils)
    return pl.pallas_call(
        functools.partial(_linear_kernel, splits=tuple(splits), n_chunk=n_chunk, dils=dils),
        grid=(t // tm,),
        in_specs=[pl.BlockSpec((tm, k), lambda i: (i, 0)),
                  pl.BlockSpec((k, n), lambda i: (0, 0))],
        out_specs=[pl.BlockSpec((tm // dl, dl * wd), lambda i: (i, 0)) for wd, dl in zip(widths, dils)],
        out_shape=[jax.ShapeDtypeStruct((t // dl, dl * wd), CDT) for wd, dl in zip(widths, dils)],
        scratch_shapes=[pltpu.VMEM((max(widths) // LANES, tm, LANES), F32)],
        compiler_params=_cparams(("parallel",), 48),
        name=name,
    )(x2d, w)


def _linear_t_kernel(x_ref, wt_ref, o_ref):
    o_ref[...] = _nt_dot(wt_ref[...], x_ref[...].astype(CDT)).astype(o_ref.dtype)


def _linear_t(x2d, wt, tm, name):
    t, k = x2d.shape
    n = wt.shape[0]
    return pl.pallas_call(
        _linear_t_kernel,
        grid=(t // tm,),
        in_specs=[pl.BlockSpec((tm, k), lambda i: (i, 0)), pl.BlockSpec((n, k), lambda i: (0, 0))],
        out_specs=pl.BlockSpec((None, n, tm), lambda i: (i, 0, 0)),
        out_shape=jax.ShapeDtypeStruct((t // tm, n, tm), CDT),
        compiler_params=_cparams(("parallel",), 32),
        name=name,
    )(x2d, wt)


def _conv_kernel(x_ref, xh_ref, wc_ref, cw_ref, ob_ref, *, tm):
    i = pl.program_id(1)
    w = CONV_WIDTH
    xb = x_ref[...].astype(CDT)
    z = _dot(xb, wc_ref[...])
    u = z[:, w:2 * w] * z[:, 2 * w:3 * w]
    zh = _dot(xh_ref[...].astype(CDT), wc_ref[:, w:3 * w])
    uh = zh[:, :w] * zh[:, w:]
    uh = jnp.where(i == 0, 0.0, uh)
    row = lax.broadcasted_iota(jnp.int32, (tm, w), 0)
    u1 = jnp.where(row == 0, uh[7:8, :], pltpu.roll(u, 1, 0))
    u2 = jnp.where(row == 0, uh[6:7, :], jnp.where(row == 1, uh[7:8, :], pltpu.roll(u, 2, 0)))
    y = cw_ref[0:1, :] * u2 + cw_ref[1:2, :] * u1 + cw_ref[2:3, :] * u
    ob_ref[...] = (z[:, :w] * y).astype(ob_ref.dtype)


def _conv(x, wc, conv_w, tm=512):
    b, s, d = x.shape
    tm = min(tm, s)
    hb = tm // 8
    return pl.pallas_call(
        functools.partial(_conv_kernel, tm=tm),
        grid=(b, s // tm),
        in_specs=[pl.BlockSpec((None, tm, d), lambda bi, i: (bi, i, 0)),
                  pl.BlockSpec((None, 8, d), lambda bi, i: (bi, jnp.maximum(i * hb - 1, 0), 0)),
                  pl.BlockSpec(wc.shape, lambda bi, i: (0, 0)),
                  pl.BlockSpec(conv_w.shape, lambda bi, i: (0, 0))],
        out_specs=pl.BlockSpec((None, tm, CONV_WIDTH), lambda bi, i: (bi, i, 0)),
        out_shape=jax.ShapeDtypeStruct((b, s, CONV_WIDTH), CDT),
        compiler_params=_cparams(("parallel", "parallel"), 48),
        name="short_conv",
    )(x, x, wc, conv_w)


def _banded_kernel(*refs, window, pr, tq, kw, want_lse, has_sink):
    q_ref, kp_ref, kc_ref, vp_ref, vc_ref = refs[:5]
    n = 5
    sink_ref = None
    if has_sink:
        sink_ref = refs[n]
        n += 1
    o_ref = refs[n]
    n += 1
    lse_ref = None
    if want_lse:
        lse_ref = refs[n]
        n += 1
    kbuf, vbuf, s_ref, e_ref, m_ref, l_ref = refs[n:n + 6]

    i = pl.program_id(2)
    kbuf[0:pr, :] = kp_ref[...]
    kbuf[pr:pr + tq, :] = kc_ref[...]
    vbuf[0:pr, :] = vp_ref[...]
    vbuf[pr:pr + tq, :] = vc_ref[...]

    span = SUB_Q + pr
    qi = lax.broadcasted_iota(jnp.int32, (SUB_Q, span), 0)
    kj = lax.broadcasted_iota(jnp.int32, (SUB_Q, span), 1)
    dist = pr + qi - kj
    band = (dist >= 0) & (dist <= window)
    lo = _half_masks()
    halves = (lo, jnp.logical_not(lo))
    groups = ((0,), (1,), (2,)) if kw == 3 * LANES else ((0, 1, 2),)
    rb = BAND_ROWS
    for sb in range(tq // SUB_Q):
        r0 = sb * SUB_Q
        bias = jnp.where(band & (i * tq + r0 - pr + kj >= 0), 0.0, NEG_INF)
        for grp in groups:
            kc0 = grp[0] * LANES if kw == 3 * LANES else 0
            qs = jnp.concatenate(
                [(jnp.where(hm, q_ref[r0:r0 + SUB_Q, p * LANES:(p + 1) * LANES], 0) * QK_SCALE).astype(CDT)
                 for p in grp for hm in halves], axis=0)
            g0 = 2 * grp[0] * SUB_Q
            s_ref[g0:g0 + qs.shape[0], :] = _nt_dot(qs, kbuf[r0:r0 + span, kc0:kc0 + LANES])
        for c0 in range(0, 6 * SUB_Q, rb):
            rows = slice(c0, c0 + rb)
            s = s_ref[rows, :] + bias[c0 % SUB_Q:c0 % SUB_Q + rb, :]
            m = jnp.max(s, axis=-1, keepdims=True)
            e = jnp.exp(s - m)
            e_ref[rows, :] = e.astype(CDT)
            m_ref[rows, :] = jnp.broadcast_to(m, (rb, LANES))
            l_ref[rows, :] = jnp.broadcast_to(jnp.sum(e, axis=-1, keepdims=True), (rb, LANES))
        for grp in groups:
            kc0 = grp[0] * LANES if kw == 3 * LANES else 0
            g0 = 2 * grp[0] * SUB_Q
            g1 = g0 + 2 * len(grp) * SUB_Q
            l = l_ref[g0:g1, :]
            o = _dot(e_ref[g0:g1, :], vbuf[r0:r0 + span, kc0:kc0 + LANES]) / l
            lse = m_ref[g0:g1, :] + jnp.log(l)
            for n_p, p in enumerate(grp):
                a = 2 * n_p * SUB_Q
                o_pair = jnp.where(lo, o[a:a + SUB_Q], o[a + SUB_Q:a + 2 * SUB_Q])
                lse_pair = jnp.where(lo, lse[a:a + SUB_Q], lse[a + SUB_Q:a + 2 * SUB_Q])
                if has_sink:
                    o_pair = o_pair * jax.nn.sigmoid(lse_pair - sink_ref[:, p * LANES:(p + 1) * LANES])
                o_ref[r0:r0 + SUB_Q, p * LANES:(p + 1) * LANES] = o_pair.astype(o_ref.dtype)
                if want_lse:
                    lse_ref[r0:r0 + SUB_Q, p * LANES:(p + 1) * LANES] = lse_pair


def _banded(qa, ka, va, *, nrep, qcol, kcol, vcol, kw, window, want_lse, sink_row=None, tq=512):
    b, l, _ = qa.shape
    pr = -(-window // SUB_Q) * SUB_Q
    tq = min(max(tq, pr), l)
    assert tq % pr == 0 and l % tq == 0, (tq, pr, l)
    ratio = tq // pr
    qw = 3 * LANES
    in_specs = [
        pl.BlockSpec((None, tq, qw), lambda bi, r, i: (bi, i, qcol(r))),
        pl.BlockSpec((None, pr, kw), lambda bi, r, i: (bi, jnp.maximum(i * ratio - 1, 0), kcol(r))),
        pl.BlockSpec((None, tq, kw), lambda bi, r, i: (bi, i, kcol(r))),
        pl.BlockSpec((None, pr, kw), lambda bi, r, i: (bi, jnp.maximum(i * ratio - 1, 0), vcol(r))),
        pl.BlockSpec((None, tq, kw), lambda bi, r, i: (bi, i, vcol(r))),
    ]
    args = [qa, ka, ka, va, va]
    if sink_row is not None:
        in_specs.append(pl.BlockSpec(sink_row.shape, lambda bi, r, i: (0, 0)))
        args.append(sink_row)
    out_specs = [pl.BlockSpec((None, tq, qw), lambda bi, r, i: (bi, i, r))]
    out_shape = [jax.ShapeDtypeStruct((b, l, nrep * qw), CDT)]
    if want_lse:
        out_specs.append(pl.BlockSpec((None, tq, qw), lambda bi, r, i: (bi, i, r)))
        out_shape.append(jax.ShapeDtypeStruct((b, l, nrep * qw), F32))
    res = pl.pallas_call(
        functools.partial(_banded_kernel, window=window, pr=pr, tq=tq, kw=kw, want_lse=want_lse,
                          has_sink=sink_row is not None),
        grid=(b, nrep, l // tq),
        in_specs=in_specs,
        out_specs=out_specs,
        out_shape=out_shape,
        scratch_shapes=[pltpu.VMEM((pr + tq, kw), ka.dtype), pltpu.VMEM((pr + tq, kw), va.dtype),
                        pltpu.VMEM((6 * SUB_Q, SUB_Q + pr), F32), pltpu.VMEM((6 * SUB_Q, SUB_Q + pr), CDT),
                        pltpu.VMEM((6 * SUB_Q, LANES), F32), pltpu.VMEM((6 * SUB_Q, LANES), F32)],
        compiler_params=_cparams(("parallel", "parallel", "parallel"), 32),
        name=f"banded_w{window}_k{kw}_r{nrep}",
    )(*args)
    return res


def _gelu_tanh(x):
    return 0.5 * x * (1.0 + jnp.tanh(0.7978845608028654 * (x + 0.044715 * (x * x * x))))


def _compress_kernel(x_ref, pa_ref, pb_ref, w1a_ref, w1b_ref, b1_ref, w2_ref, b2_ref, o_ref):
    x = x_ref[...].astype(F32)
    n = x.shape[0]
    a = _dot((x + pa_ref[...]).astype(CDT), w1a_ref[...])
    bm = _dot((x + pb_ref[...]).astype(CDT), w1b_ref[...])
    h = a + pltpu.roll(bm, n - 1, 0) + b1_ref[...]
    o_ref[...] = (_dot(_gelu_tanh(h).astype(CDT), w2_ref[...]) + b2_ref[...]).astype(o_ref.dtype)


def _compress(t, pos, w1, b1, w2, b2):
    b, s, _ = t.shape
    nch = s // CMP_STRIDE
    xw = CMP_STRIDE * LANES
    x = t.reshape(b, nch, xw)
    eye = jnp.eye(NSA_KV_HEADS, dtype=F32)
    w1r = w1.reshape(CMP_BLOCK, HEAD_DIM, CMP_HIDDEN)

    def expand_w1(part):
        return jnp.einsum('tdj,kl->tkdlj', part, eye).reshape(xw, NSA_KV_HEADS * CMP_HIDDEN).astype(CDT)

    def expand_pos(part):
        return jnp.broadcast_to(part[:, None, :], (CMP_STRIDE, NSA_KV_HEADS, HEAD_DIM)).reshape(1, xw)

    w1a, w1b = expand_w1(w1r[:CMP_STRIDE]), expand_w1(w1r[CMP_STRIDE:])
    pa, pb = expand_pos(pos[:CMP_STRIDE]), expand_pos(pos[CMP_STRIDE:])
    b1e = jnp.tile(b1, NSA_KV_HEADS).reshape(1, -1)
    w2e = jnp.einsum('jd,kl->kjld', w2, eye).reshape(NSA_KV_HEADS * CMP_HIDDEN, LANES).astype(CDT)
    b2e = jnp.tile(b2, NSA_KV_HEADS).reshape(1, -1)
    consts = [pa, pb, w1a, w1b, b1e, w2e, b2e]
    return pl.pallas_call(
        _compress_kernel,
        grid=(b,),
        in_specs=[pl.BlockSpec((None, nch, xw), lambda bi: (bi, 0, 0))]
        + [pl.BlockSpec(c.shape, lambda bi: (0, 0)) for c in consts],
        out_specs=pl.BlockSpec((None, nch, LANES), lambda bi: (bi, 0, 0)),
        out_shape=jax.ShapeDtypeStruct((b, nch, LANES), CDT),
        compiler_params=_cparams(("parallel",), 48),
        name="nsa_compress",
    )(x, *consts)


CMP_ROWS = 16


def _cmp_select_kernel(q_ref, kc_ref, vc_ref, ov_ref, o_ref, sel_ref, s_ref, p_ref, hi_ref, lo_ref,
                       *, tq, n_sel, tile0):
    i = pl.program_id(1) + tile0
    ncp = kc_ref.shape[0]
    ns = ov_ref.shape[1]
    lo = _half_masks()
    t_col = i * tq + lax.broadcasted_iota(jnp.int32, (tq, 1), 0)
    blk = lax.broadcasted_iota(jnp.int32, (tq, ns), 1)
    blk_t = lax.broadcasted_iota(jnp.int32, (ns, tq), 0)
    cur = t_col // SEL_BLOCK
    causal = blk <= cur
    forced = (blk == 0) | (blk == cur) | (blk == cur - 1)
    rb = CMP_ROWS
    c_end = lax.broadcasted_iota(jnp.int32, (rb, ncp), 1) * CMP_STRIDE + (CMP_BLOCK - 1)
    o_kv, work_t = [], []
    for kv, hm in enumerate((lo, jnp.logical_not(lo))):
        qs = jnp.concatenate(
            [(jnp.where(hm, q_ref[:, p * LANES:(p + 1) * LANES], 0) * QK_SCALE).astype(CDT) for p in range(3)], axis=0)
        s_ref[...] = _nt_dot(qs, kc_ref[...])
        for r0 in range(0, tq, rb):
            t_rows = i * tq + r0 + lax.broadcasted_iota(jnp.int32, (rb, 1), 0)
            vis_bias = jnp.where(c_end <= t_rows, 0.0, NEG_INF)
            has_visible = t_rows >= CMP_BLOCK - 1
            psum = jnp.zeros((rb, ncp), F32)
            for h in range(3):
                rows = slice(h * tq + r0, h * tq + r0 + rb)
                s = s_ref[rows, :] + vis_bias
                e = jnp.exp(s - jnp.max(s, axis=-1, keepdims=True))
                inv = jnp.where(has_visible, 1.0 / jnp.maximum(jnp.sum(e, axis=-1, keepdims=True), 1e-30), 0.0)
                pn = e * inv
                p_ref[rows, :] = pn.astype(CDT)
                psum = psum + pn
            p_hi = psum.astype(CDT)
            hi_ref[r0:r0 + rb, :] = p_hi
            lo_ref[r0:r0 + rb, :] = (psum - p_hi.astype(F32)).astype(CDT)
        o = _dot(p_ref[...], vc_ref[...])
        o_kv.append([o[p * tq:(p + 1) * tq] for p in range(3)])
        imp = _dot(hi_ref[...], ov_ref[...]) + _dot(lo_ref[...], ov_ref[...])
        work_t.append(jnp.where(causal & jnp.logical_not(forced), imp, -BIG).T)

    blk_lanes = blk_t[:, 0:LANES]

    def pick(_, work):
        m = jnp.max(work, axis=0, keepdims=True)
        idx = jnp.min(jnp.where(work == m, blk_lanes, ns), axis=0, keepdims=True)
        return jnp.where(blk_lanes == idx, -2.0 * BIG, work)

    for kv, start in enumerate(work_t):
        done = jnp.concatenate([lax.fori_loop(0, n_sel - 3, pick, start[:, c0:c0 + LANES])
                                for c0 in range(0, tq, LANES)], axis=1)
        taken = jnp.where((done < -BIG) & (start > -BIG), 1.0, 0.0).T
        selb = jnp.where(forced | (taken > 0.5), 0.0, NEG_INF)
        sel_ref[:, kv * ns:(kv + 1) * ns] = selb.astype(sel_ref.dtype)
    for p in range(3):
        o_ref[:, p * LANES:(p + 1) * LANES] = jnp.where(lo, o_kv[0][p], o_kv[1][p]).astype(o_ref.dtype)


CMP_CAUSAL_SPLITS = 4


def _cmp_select(zr, qcol, kc, vc, tq=256):
    b, s, _ = zr.shape
    ncp = kc.shape[1]
    ns = s // SEL_BLOCK
    n_sel = min(N_SEL, ns)
    assert n_sel >= 3, "selection needs room for the three forced blocks"
    tq = min(tq, s)
    c = np.arange(ncp)[:, None] * CMP_STRIDE
    j = np.arange(ns)[None, :] * SEL_BLOCK
    overlap = ((c < j + SEL_BLOCK) & (c + CMP_BLOCK - 1 >= j)).astype(np.float32)
    overlap[ncp - 1:, :] = 0.0
    ov = jnp.asarray(overlap, CDT)
    qw = 3 * LANES
    n_split = CMP_CAUSAL_SPLITS if (s // tq) % CMP_CAUSAL_SPLITS == 0 and ncp % (16 * CMP_CAUSAL_SPLITS) == 0 else 1
    tiles = s // tq // n_split
    outs, sels = [], []
    for part in range(n_split):
        tile0 = part * tiles
        ncp_part = ncp * (part + 1) // n_split
        o_part, sel_part = pl.pallas_call(
            functools.partial(_cmp_select_kernel, tq=tq, n_sel=n_sel, tile0=tile0),
            grid=(b, tiles),
            in_specs=[pl.BlockSpec((None, tq, qw), lambda bi, i, tile0=tile0: (bi, i + tile0, qcol)),
                      pl.BlockSpec((None, ncp_part, LANES), lambda bi, i: (bi, 0, 0)),
                      pl.BlockSpec((None, ncp_part, LANES), lambda bi, i: (bi, 0, 0)),
                      pl.BlockSpec((ncp_part, ns), lambda bi, i: (0, 0))],
            out_specs=[pl.BlockSpec((None, tq, qw), lambda bi, i: (bi, i, 0)),
                       pl.BlockSpec((None, tq, 2 * ns), lambda bi, i: (bi, i, 0))],
            out_shape=[jax.ShapeDtypeStruct((b, tiles * tq, qw), CDT),
                       jax.ShapeDtypeStruct((b, tiles * tq, 2 * ns), CDT)],
            scratch_shapes=[pltpu.VMEM((3 * tq, ncp_part), F32), pltpu.VMEM((3 * tq, ncp_part), CDT),
                            pltpu.VMEM((tq, ncp_part), CDT), pltpu.VMEM((tq, ncp_part), CDT)],
            compiler_params=_cparams(("parallel", "parallel"), 48),
            name=f"nsa_cmp_select_p{part}",
        )(zr, kc, vc, ov)
        outs.append(o_part)
        sels.append(sel_part)
    return jnp.concatenate(outs, axis=1), jnp.concatenate(sels, axis=1)


SLC_KEY_CHUNK = 256


def _slc_kernel(q_ref, k0_ref, k1_ref, vt_ref, selb_ref, o_ref, s_ref, e_ref, m_ref, l_ref, a_ref, acc_ref,
                *, tq, tk):
    i = pl.program_id(1)
    ns = selb_ref.shape[1] // 2
    bpt = tk // SEL_BLOCK
    n_q = 3 * tq
    rc = min(SLC_KEY_CHUNK, tk)
    lo = _half_masks()
    n_tiles = ((i + 1) * tq + tk - 1) // tk
    p_row = lax.broadcasted_iota(jnp.int32, (ns, LANES), 0)
    p_col = lax.broadcasted_iota(jnp.int32, (ns, LANES), 1)
    groups = []
    for kv, (hm, k_ref) in enumerate(((lo, k0_ref), (jnp.logical_not(lo), k1_ref))):
        q3 = [(jnp.where(hm, q_ref[:, p * LANES:(p + 1) * LANES], 0) * QK_SCALE).astype(CDT) for p in range(3)]
        selb = selb_ref[:, kv * ns:(kv + 1) * ns]
        lane0 = HEAD_DIM if kv == 0 else 0
        groups.append((kv, hm, k_ref, q3, selb, lane0))
    m_ref[...] = jnp.full(m_ref.shape, NEG_INF, F32)
    l_ref[...] = jnp.zeros(l_ref.shape, F32)
    acc_ref[...] = jnp.zeros(acc_ref.shape, F32)

    def scores(j, slot):
        k0 = pl.multiple_of(j * tk, tk)
        for kv, hm, k_ref, q3, selb, lane0 in groups:
            place = ((p_col >= lane0) & (p_col < lane0 + bpt) & (p_row == p_col - lane0 + j * bpt)).astype(CDT)
            sb = _dot(selb, place).astype(CDT)
            qp = jnp.concatenate([jnp.where(hm, q, sb) for q in q3], axis=0)
            s_ref[slot, kv] = _nt_dot(k_ref[pl.ds(k0, tk), :], qp)

    def softmax_pv(j, slot, diagonal):
        k0 = pl.multiple_of(j * tk, tk)
        for kv in range(2):
            for c0 in range(0, n_q, LANES):
                cols = slice(c0, c0 + LANES)
                t_lane = i * tq + (c0 % tq) + lax.broadcasted_iota(jnp.int32, (1, LANES), 1)

                def chunk(r0):
                    s = s_ref[slot, kv, r0:r0 + rc, cols]
                    if diagonal:
                        kpos = k0 + r0 + lax.broadcasted_iota(jnp.int32, (rc, LANES), 0)
                        s = jnp.where(kpos <= t_lane, s, NEG_INF)
                    return s

                m8 = m_ref[kv, :, cols]
                for r0 in range(0, tk, rc):
                    m8 = jnp.maximum(m8, jnp.max(chunk(r0).reshape(rc // 8, 8, LANES), axis=0))
                m_new = jnp.max(m8, axis=0, keepdims=True)
                alpha = jnp.exp(m_ref[kv, :, cols] - m_new)
                l8 = jnp.zeros((8, LANES), F32)
                for r0 in range(0, tk, rc):
                    e = jnp.exp(chunk(r0) - m_new)
                    l8 = l8 + jnp.sum(e.reshape(rc // 8, 8, LANES), axis=0)
                    e_ref[kv, r0:r0 + rc, cols] = e.astype(CDT)
                l_ref[kv, :, cols] = alpha * l_ref[kv, :, cols] + jnp.sum(l8, axis=0, keepdims=True)
                m_ref[kv, :, cols] = jnp.broadcast_to(m_new, (8, LANES))
                a_ref[kv, :, cols] = alpha
        for kv in range(2):
            acc_ref[kv] = a_ref[kv, 0:1, :] * acc_ref[kv] + _dot(vt_ref[j], e_ref[kv])

    n_plain = n_tiles - 1
    n_pairs = n_plain // 2
    scores(0, 0)

    def pair(jj, c):
        a = 2 * jj
        scores(a + 1, 1)
        softmax_pv(a, 0, False)
        scores(a + 2, 0)
        softmax_pv(a + 1, 1, False)
        return c

    lax.fori_loop(0, n_pairs, pair, 0)
    tail = 2 * n_pairs

    @pl.when(n_plain % 2 == 0)
    def _():
        softmax_pv(tail, 0, True)

    @pl.when(n_plain % 2 == 1)
    def _():
        scores(tail + 1, 1)
        softmax_pv(tail, 0, False)
        softmax_pv(tail + 1, 1, True)

    o0 = (acc_ref[0] / l_ref[0, 0:1, :]).T
    o1 = (acc_ref[1] / l_ref[1, 0:1, :]).T
    for p in range(3):
        o_ref[:, p * LANES:(p + 1) * LANES] = jnp.where(
            lo, o0[p * tq:(p + 1) * tq], o1[p * tq:(p + 1) * tq]).astype(o_ref.dtype)


def _slc(zr, qcol, ksc, vt, selb, tq=256):
    b, s, _ = zr.shape
    tq = min(tq, s)
    tk = vt.shape[2]
    assert tq % LANES == 0 and tk // SEL_BLOCK <= HEAD_DIM and s % tk == 0
    qw = 3 * LANES
    ns2 = selb.shape[2]
    pat = jax.nn.one_hot((jnp.arange(s) // SEL_BLOCK) % (tk // SEL_BLOCK), HEAD_DIM, dtype=ksc.dtype)
    pat = jnp.broadcast_to(pat[None], (b, s, HEAD_DIM))
    k0 = jnp.concatenate([ksc[..., :HEAD_DIM], pat], axis=-1)
    k1 = jnp.concatenate([pat, ksc[..., HEAD_DIM:]], axis=-1)
    full = lambda bi, i: (bi, 0, 0)
    return pl.pallas_call(
        functools.partial(_slc_kernel, tq=tq, tk=tk),
        grid=(b, s // tq),
        in_specs=[pl.BlockSpec((None, tq, qw), lambda bi, i: (bi, i, qcol)),
                  _single_buffered((None, s, LANES), full),
                  _single_buffered((None, s, LANES), full),
                  _single_buffered((s // tk, LANES, tk), lambda bi, i: (bi, 0, 0)),
                  pl.BlockSpec((None, tq, ns2), lambda bi, i: (bi, i, 0))],
        out_specs=pl.BlockSpec((None, tq, qw), lambda bi, i: (bi, i, 0)),
        out_shape=jax.ShapeDtypeStruct((b, s, qw), CDT),
        scratch_shapes=[pltpu.VMEM((2, 2, tk, 3 * tq), F32), pltpu.VMEM((2, tk, 3 * tq), CDT),
                        pltpu.VMEM((2, 8, 3 * tq), F32), pltpu.VMEM((2, 8, 3 * tq), F32),
                        pltpu.VMEM((2, 8, 3 * tq), F32), pltpu.VMEM((2, LANES, 3 * tq), F32)],
        compiler_params=_cparams(("parallel", "arbitrary"), 56),
        name="nsa_slc",
    )(zr, k0, k1, vt, selb)


def _merge_kernel(x_ref, oa0, oa1, oa2, la0, la1, la2, ob, ocmp, oslc, owin, od,
                  wn_ref, wg_ref, bg_ref, wb_ref, wo_ref, g_ref, b_ref, o_ref, *scratch, alpha):
    x = x_ref[...]
    xb = x.astype(CDT)
    bw = BRANCH_WIDTH

    def token_rows(src_ref, dst_ref):
        dil = src_ref.shape[1] // bw
        if dil == 1:
            return src_ref[...].astype(F32)
        n_cb = bw // LANES
        for r in range(dil):
            for cb in range(n_cb):
                c0 = r * bw + cb * LANES
                dst_ref[cb, pl.ds(r, src_ref.shape[0], stride=dil), :] = src_ref[:, c0:c0 + LANES].astype(F32)
        return jnp.concatenate([dst_ref[cb] for cb in range(n_cb)], axis=1)

    o0, o1, o2 = (token_rows(s, d) for s, d in zip((oa0, oa1, oa2), scratch[0:3]))
    l0, l1, l2 = (token_rows(s, d) for s, d in zip((la0, la1, la2), scratch[3:6]))
    mx = jnp.maximum(jnp.maximum(l0, l1), l2)
    w0, w1, w2 = jnp.exp(l0 - mx), jnp.exp(l1 - mx), jnp.exp(l2 - mx)
    o_a = (w0 * o0 + w1 * o1 + w2 * o2) / (w0 + w1 + w2)
    gates = jax.nn.sigmoid(_dot(xb, wn_ref[...]))
    o_c = (gates[:, 0:bw] * ocmp[...].astype(F32) + gates[:, bw:2 * bw] * oslc[...].astype(F32)
           + gates[:, 2 * bw:3 * bw] * owin[...].astype(F32))
    branches = (o_a.astype(CDT), ob[...], o_c.astype(CDT), od[...])
    d = x.shape[1]
    merged = jnp.zeros(x.shape, F32)
    for m in range(N_BRANCH):
        gate = jax.nn.sigmoid(_dot(xb, wg_ref[:, m * d:(m + 1) * d]) + bg_ref[:, m * d:(m + 1) * d])
        merged = merged + gate * _dot(branches[m], wb_ref[m])
    r = alpha * x + _dot(merged.astype(CDT), wo_ref[...])
    o_ref[...] = _layer_norm(r, g_ref[...], b_ref[...])


def _merge(x2d, branch_inputs, wn, wg, bg, wb, wo, g, b, alpha, tm=256):
    t, d = x2d.shape
    tm = min(tm, t)
    row = lambda i: (i, 0)
    const2 = lambda i: (0, 0)
    in_specs = [pl.BlockSpec((tm, d), row)]
    in_specs += [pl.BlockSpec((tm * a.shape[0] // t, a.shape[1]), row) for a in branch_inputs]
    in_specs += [pl.BlockSpec(wn.shape, const2), pl.BlockSpec(wg.shape, const2), pl.BlockSpec(bg.shape, const2),
                 pl.BlockSpec(wb.shape, lambda i: (0, 0, 0)), pl.BlockSpec(wo.shape, const2),
                 pl.BlockSpec(g.shape, const2), pl.BlockSpec(b.shape, const2)]
    return pl.pallas_call(
        functools.partial(_merge_kernel, alpha=alpha),
        grid=(t // tm,),
        in_specs=in_specs,
        out_specs=pl.BlockSpec((tm, d), row),
        out_shape=jax.ShapeDtypeStruct((t, d), F32),
        scratch_shapes=[pltpu.VMEM((BRANCH_WIDTH // LANES, tm, LANES), F32) for _ in range(2 * N_DIL)],
        compiler_params=_cparams(("parallel",), 56),
        name="merge_ln",
    )(x2d, *branch_inputs, wn, wg, bg, wb, wo, g, b)


def _ple_ln(x, xb, f, p, plw_ref, pgw_ref, pgb_ref, g_ref, b_ref, alpha):
    ple = jax.nn.sigmoid(_dot(xb, pgw_ref[...]) + pgb_ref[...]) * _dot(p.astype(CDT), plw_ref[...])
    return _layer_norm(alpha * x + f + ple, g_ref[...], b_ref[...])


FFN_CHUNK = 512


def _ffn_kernel(x_ref, p_ref, wg_ref, wu_ref, wd_ref, plw_ref, pgw_ref, pgb_ref, g_ref, b_ref, o_ref, h_ref, *, alpha):
    x = x_ref[...]
    xb = x.astype(CDT)
    dff = wg_ref.shape[1]
    for c0 in range(0, dff, FFN_CHUNK):
        cols = slice(c0, min(c0 + FFN_CHUNK, dff))
        h_ref[:, cols] = (jax.nn.silu(_dot(xb, wg_ref[:, cols])) * _dot(xb, wu_ref[:, cols])).astype(CDT)
    f = _dot(h_ref[...], wd_ref[...])
    o_ref[...] = _ple_ln(x, xb, f, p_ref[...], plw_ref, pgw_ref, pgb_ref, g_ref, b_ref, alpha)


def _ffn(x2d, p2d, wg, wu, wd, plw, pgw, pgb, g, b, alpha, tm=512):
    t, d = x2d.shape
    tm = min(tm, t)
    dff = wg.shape[1]
    wg, wu, wd = wg.astype(CDT), wu.astype(CDT), wd.astype(CDT)
    row = lambda i: (i, 0)
    const = lambda shape: pl.BlockSpec(shape, lambda i: (0, 0), pipeline_mode=pl.Buffered(1))
    return pl.pallas_call(
        functools.partial(_ffn_kernel, alpha=alpha),
        grid=(t // tm,),
        in_specs=[pl.BlockSpec((tm, d), row), pl.BlockSpec((tm, p2d.shape[1]), row),
                  const(wg.shape), const(wu.shape), const(wd.shape),
                  const(plw.shape), const(pgw.shape), const(pgb.shape), const(g.shape), const(b.shape)],
        out_specs=pl.BlockSpec((tm, d), row),
        out_shape=jax.ShapeDtypeStruct((t, d), F32),
        scratch_shapes=[pltpu.VMEM((tm, dff), CDT)],
        compiler_params=_cparams(("parallel",), 48),
        name="ffn_ple_ln",
    )(x2d, p2d, wg, wu, wd, plw, pgw, pgb, g, b)


def _router_kernel(x_ref, wh_ref, wl_ref, b_ref, comb_ref, rank_ref, rank_t_ref, cnt_ref):
    x = x_ref[...]
    xh = x.astype(CDT)
    xl = (x - xh.astype(F32)).astype(CDT)
    logits = _dot(xh, wh_ref[...]) + _dot(xh, wl_ref[...]) + _dot(xl, wh_ref[...]) + b_ref[...]
    lane = lax.broadcasted_iota(jnp.int32, logits.shape, 1)
    v1 = jnp.max(logits, axis=-1, keepdims=True)
    i1 = jnp.min(jnp.where(logits == v1, lane, LANES), axis=-1, keepdims=True)
    rest = jnp.where(lane == i1, -jnp.inf, logits)
    v2 = jnp.max(rest, axis=-1, keepdims=True)
    i2 = jnp.min(jnp.where(rest == v2, lane, LANES), axis=-1, keepdims=True)
    e2 = jnp.exp(v2 - v1)
    comb_ref[...] = jnp.where(lane == i1, 1.0 / (1.0 + e2), 0.0) + jnp.where(lane == i2, e2 / (1.0 + e2), 0.0)
    routed = (lane == i1) | (lane == i2)
    mask = routed.astype(CDT)
    tm = x.shape[0]
    before = (lax.broadcasted_iota(jnp.int32, (tm, tm), 1) < lax.broadcasted_iota(jnp.int32, (tm, tm), 0)).astype(CDT)
    rank = jnp.where(routed, _dot(before, mask), -1.0)
    rank_ref[...] = rank
    rank_t_ref[...] = rank.T[0:rank_t_ref.shape[0], :]
    cnt_ref[...] = jnp.sum(routed.astype(F32), axis=0, keepdims=True).astype(jnp.int32)


def _router(x2d, w_router, b_router, tm):
    t, d = x2d.shape
    ne = w_router.shape[1]
    wp = jnp.zeros((d, LANES), F32).at[:, :ne].set(w_router)
    wh = wp.astype(CDT)
    wl = (wp - wh.astype(F32)).astype(CDT)
    bp = jnp.full((1, LANES), -BIG, F32).at[0, :ne].set(b_router)
    nt = t // tm
    row = lambda i: (i, 0)
    return pl.pallas_call(
        _router_kernel,
        grid=(nt,),
        in_specs=[pl.BlockSpec((tm, d), row), pl.BlockSpec(wh.shape, lambda i: (0, 0)),
                  pl.BlockSpec(wl.shape, lambda i: (0, 0)), pl.BlockSpec(bp.shape, lambda i: (0, 0))],
        out_specs=[pl.BlockSpec((tm, LANES), row), pl.BlockSpec((tm, LANES), row),
                   pl.BlockSpec((None, 8, tm), lambda i: (i, 0, 0)),
                   pl.BlockSpec((None, 1, LANES), lambda i: (i, 0, 0))],
        out_shape=[jax.ShapeDtypeStruct((t, LANES), F32), jax.ShapeDtypeStruct((t, LANES), F32),
                   jax.ShapeDtypeStruct((nt, 8, tm), F32), jax.ShapeDtypeStruct((nt, 1, LANES), jnp.int32)],
        compiler_params=_cparams(("parallel",), 40),
        name="moe_router",
    )(x2d, wh, wl, bp)


def _moe_kernel(cnt_ref, x_ref, comb_ref, rank_ref, rank_t_ref, p_ref, wg_ref, wu_ref, wd_ref, plw_ref, pgw_ref,
                pgb_ref, g_ref, b_ref, o_ref, xe_ref, ye_ref, *, alpha, rs, seg):
    i = pl.program_id(0)
    e = pl.program_id(1)
    c = pl.program_id(2)
    n_seg = x_ref.shape[0] // seg
    last_chunk = c == pl.num_programs(2) - 1

    def rows(sc):
        return pl.ds(pl.multiple_of(sc * rs, 8), rs)

    @pl.when((e == 0) & (c == 0))
    def _():
        o_ref[...] = jnp.zeros_like(o_ref)

    for sg in range(n_seg):
        tok = slice(sg * seg, (sg + 1) * seg)
        n_groups = (cnt_ref[(i * n_seg + sg) * LANES + e] + rs - 1) // rs

        @pl.when(c == 0)
        def _():
            xb = x_ref[tok, :].astype(CDT)
            rank_row = rank_t_ref[sg, pl.ds(e, 1), :]
            row_id = lax.broadcasted_iota(jnp.int32, (rs, seg), 0).astype(F32)

            def gather(sc, _):
                onehot = (rank_row - (sc * rs).astype(F32) == row_id).astype(CDT)
                xe_ref[sg, rows(sc), :] = _dot(onehot, xb).astype(CDT)
                ye_ref[sg, rows(sc), :] = jnp.zeros((rs, ye_ref.shape[2]), F32)
                return 0

            lax.fori_loop(0, n_groups, gather, 0)

        def expert(sc, _):
            xs = xe_ref[sg, rows(sc), :]
            h = jax.nn.silu(_dot(xs, wg_ref[0])) * _dot(xs, wu_ref[0])
            ye_ref[sg, rows(sc), :] += _dot(h.astype(CDT), wd_ref[0])
            return 0

        lax.fori_loop(0, n_groups, expert, 0)

        @pl.when(last_chunk)
        def _():
            lane = lax.broadcasted_iota(jnp.int32, (seg, LANES), 1)
            mine = lane == e
            cw = jnp.sum(jnp.where(mine, comb_ref[tok, :], 0.0), axis=-1, keepdims=True)
            rank_col = jnp.sum(jnp.where(mine, rank_ref[tok, :], 0.0), axis=-1, keepdims=True)
            col_id = lax.broadcasted_iota(jnp.int32, (seg, rs), 1).astype(F32)

            def scatter(sc, _):
                onehot = (rank_col - (sc * rs).astype(F32) == col_id).astype(CDT)
                o_ref[tok, :] += cw * _dot(onehot, ye_ref[sg, rows(sc), :].astype(CDT))
                return 0

            lax.fori_loop(0, n_groups, scatter, 0)

    @pl.when((e == pl.num_programs(1) - 1) & last_chunk)
    def _():
        for sg in range(n_seg):
            tok = slice(sg * seg, (sg + 1) * seg)
            x = x_ref[tok, :]
            o_ref[tok, :] = _ple_ln(x, x.astype(CDT), o_ref[tok, :], p_ref[tok, :], plw_ref, pgw_ref, pgb_ref,
                                    g_ref, b_ref, alpha)


MOE_CHUNK = 512
MOE_SEGMENT = 1024
MOE_SEGMENTS_PER_TILE = 2
MOE_ROW_GROUP = 288


def _single_buffered(shape, index_map):
    return pl.BlockSpec(shape, index_map, pipeline_mode=pl.Buffered(1))


def _moe(x2d, routing, p2d, wg, wu, wd, plw, pgw, pgb, g, b, alpha, seg):
    comb, rank, rank_t, cnt = routing
    t, d = x2d.shape
    ne, _, dff = wg.shape
    ck = min(MOE_CHUNK, dff)
    rs = min(MOE_ROW_GROUP, seg)
    n_seg = min(MOE_SEGMENTS_PER_TILE, t // seg)
    tm = n_seg * seg
    max_rows = -(-seg // rs) * rs
    wg, wu, wd = wg.astype(CDT), wu.astype(CDT), wd.astype(CDT)
    row = lambda i, e, c, cnt: (i, 0)
    c2 = lambda i, e, c, cnt: (0, 0)
    grid_spec = pltpu.PrefetchScalarGridSpec(
        num_scalar_prefetch=1,
        grid=(t // tm, ne, dff // ck),
        in_specs=[_single_buffered((tm, d), row), _single_buffered((tm, LANES), row),
                  _single_buffered((tm, LANES), row),
                  _single_buffered((n_seg, 8, seg), lambda i, e, c, cnt: (i, 0, 0)),
                  _single_buffered((tm, p2d.shape[1]), row),
                  pl.BlockSpec((1, d, ck), lambda i, e, c, cnt: (e, 0, c)),
                  pl.BlockSpec((1, d, ck), lambda i, e, c, cnt: (e, 0, c)),
                  pl.BlockSpec((1, ck, d), lambda i, e, c, cnt: (e, c, 0)),
                  _single_buffered(plw.shape, c2), _single_buffered(pgw.shape, c2), _single_buffered(pgb.shape, c2),
                  _single_buffered(g.shape, c2), _single_buffered(b.shape, c2)],
        out_specs=pl.BlockSpec((tm, d), row),
        scratch_shapes=[pltpu.VMEM((n_seg, max_rows, d), CDT), pltpu.VMEM((n_seg, max_rows, d), F32)],
    )
    return pl.pallas_call(
        functools.partial(_moe_kernel, alpha=alpha, rs=rs, seg=seg),
        grid_spec=grid_spec,
        out_shape=jax.ShapeDtypeStruct((t, d), F32),
        compiler_params=_cparams(("parallel", "arbitrary", "arbitrary"), 60),
        name="moe_ple_ln",
    )(cnt.reshape(-1), x2d, comb, rank, rank_t, p2d, wg, wu, wd, plw, pgw, pgb, g, b)


def _prep_in_weights(w_in):
    o = COL_OFF
    bw = BRANCH_WIDTH
    cols = lambda n: w_in[:, o[n]:o[n + 1]]
    qa, ka, va = cols(0), cols(1), cols(2)
    w_dil = jnp.concatenate(
        [t[:, g * bw:(g + 1) * bw] for g in range(N_DIL) for t in (qa, ka, va)], axis=1).astype(CDT)
    w_conv = jnp.concatenate([cols(3), cols(4), cols(5)], axis=1).astype(CDT)
    gn = cols(13)
    w_gate = jnp.concatenate([gn[:, br * NSA_Q_HEADS + GQA_COL_HEAD] for br in range(3)], axis=1).astype(CDT)
    w_rest = jnp.concatenate([cols(6)[:, GQA_COL_PERM], cols(14)[:, GQA_COL_PERM], cols(11),
                              cols(12), cols(15), cols(16), cols(9), cols(7), cols(8)], axis=1).astype(CDT)
    wt_vsc = cols(10).T.astype(CDT)
    return w_dil, w_conv, w_gate, w_rest, wt_vsc


ZR_Q_NSA, ZR_Q_SWA = 0, 1
ZR_KWC, ZR_VWC, ZR_KD, ZR_VD = 6, 7, 8, 9
ZR_WIDTH = 2 * BRANCH_WIDTH + 4 * LANES
SLC_KEY_TILE = 1024


def _token_mixers(x, w_in, conv_w, cmp_pos, cmp_w1, cmp_b1, cmp_w2, cmp_b2, sinks):
    b, s, d = x.shape
    x2d = x.reshape(b * s, d)
    w_dil, w_conv, w_gate, w_rest, wt_vsc = _prep_in_weights(w_in)
    gw = 3 * BRANCH_WIDTH

    z_dil = _linear(x2d, w_dil, [(g * gw, (g + 1) * gw) for g in range(N_DIL)], gw, "in_proj_dil",
                    dils=[dil for _, dil in DIL_PATTERNS])
    zr, ksc, kcc, vcc = _linear(x2d, w_rest, [(0, ZR_WIDTH)] + [(ZR_WIDTH + n * LANES, ZR_WIDTH + (n + 1) * LANES)
                                                           for n in range(3)], 256, "in_proj_rest")
    zr = zr.reshape(b, s, ZR_WIDTH)
    o_b = _conv(x, w_conv, conv_w)

    dil_o, dil_lse = [], []
    for g, (window, dil) in enumerate(DIL_PATTERNS):
        view = z_dil[g].reshape(b, s // dil, dil * gw)
        og, lg = _banded(view, view, view, nrep=dil,
                         qcol=lambda r: 3 * r, kcol=lambda r: 3 * r + 1, vcol=lambda r: 3 * r + 2,
                         kw=3 * LANES, window=window // dil, want_lse=True)
        dil_o.append(og.reshape(b * s // dil, dil * BRANCH_WIDTH))
        dil_lse.append(lg.reshape(b * s // dil, dil * BRANCH_WIDTH))

    kc = _compress(kcc.reshape(b, s, LANES), cmp_pos[0], cmp_w1[0], cmp_b1[0], cmp_w2[0], cmp_b2[0])
    vc = _compress(vcc.reshape(b, s, LANES), cmp_pos[1], cmp_w1[1], cmp_b1[1], cmp_w2[1], cmp_b2[1])
    o_cmp, selb = _cmp_select(zr, ZR_Q_NSA, kc, vc)
    vsc_t = _linear_t(x2d, wt_vsc, min(SLC_KEY_TILE, s), "in_proj_vsc_t")
    o_slc = _slc(zr, ZR_Q_NSA, ksc.reshape(b, s, LANES), vsc_t, selb)
    (o_win,) = _banded(zr, zr, zr, nrep=1, qcol=lambda r: ZR_Q_NSA, kcol=lambda r: ZR_KWC, vcol=lambda r: ZR_VWC,
                       kw=LANES, window=NSA_WINDOW - 1, want_lse=False, tq=512)

    sink_row = sinks.astype(F32)[GQA_COL_HEAD].reshape(1, BRANCH_WIDTH)
    (o_d,) = _banded(zr, zr, zr, nrep=1, qcol=lambda r: ZR_Q_SWA, kcol=lambda r: ZR_KD, vcol=lambda r: ZR_VD,
                     kw=LANES, window=SWA_WINDOW - 1, want_lse=False, sink_row=sink_row)

    t = b * s
    flat = lambda a: a.reshape(t, a.shape[-1])
    return [dil_o[0], dil_o[1], dil_o[2], dil_lse[0], dil_lse[1], dil_lse[2], flat(o_b), flat(o_cmp), flat(o_slc),
            flat(o_win), flat(o_d)], w_gate


def kernel(x, p, w_in, conv_w, cmp_pos, cmp_w1, cmp_b1, cmp_w2, cmp_b2, sinks, w_branch, w_merge_gate, b_merge_gate, w_out, ln_mix_g, ln_mix_b, ffn_w_gate, ffn_w_up, ffn_w_down, w_router, b_router, moe_w_gate, moe_w_up, moe_w_down, ple_w, ple_gate_w, ple_gate_b, ln_ffn_g, ln_ffn_b):
    depth, b, s, _ = p.shape
    d = x.shape[-1]
    t = b * s
    alpha = (2 * depth) ** 0.25
    row = lambda v: v.reshape(1, -1).astype(F32)
    for i in range(depth):
        branch_inputs, w_nsa_gate = _token_mixers(x, w_in[i], conv_w[i], cmp_pos[i], cmp_w1[i], cmp_b1[i], cmp_w2[i],
                                                  cmp_b2[i], sinks[i])
        wg = jnp.concatenate([w_merge_gate[i, m] for m in range(N_BRANCH)], axis=1).astype(CDT)
        bg = b_merge_gate[i].reshape(1, N_BRANCH * d).astype(F32)
        wb = jnp.stack([w_branch[i, 0], w_branch[i, 1], w_branch[i, 2][GQA_COL_PERM],
                        w_branch[i, 3][GQA_COL_PERM]]).astype(CDT)
        x1 = _merge(x.reshape(t, d), branch_inputs, w_nsa_gate, wg, bg, wb, w_out[i].astype(CDT),
                    row(ln_mix_g[i]), row(ln_mix_b[i]), alpha)
        p2d = p[i].reshape(t, -1)
        ple_args = (ple_w[i].astype(CDT), ple_gate_w[i].astype(CDT), row(ple_gate_b[i]),
                    row(ln_ffn_g[i]), row(ln_ffn_b[i]))
        j = i // 2
        if i % 2 == 0:
            x2 = _ffn(x1, p2d, ffn_w_gate[j], ffn_w_up[j], ffn_w_down[j], *ple_args, alpha)
        else:
            seg = min(MOE_SEGMENT, t)
            routing = _router(x1, w_router[j], b_router[j], seg)
            x2 = _moe(x1, routing, p2d, moe_w_gate[j], moe_w_up[j], moe_w_down[j], *ple_args, alpha, seg)
        x = x2.reshape(b, s, d)
    return x
```

```python
import functools

import numpy as np
import jax
import jax.numpy as jnp
from jax import lax
from jax.experimental import pallas as pl
from jax.experimental.pallas import tpu as pltpu

D_MODEL = 1024
HEAD_DIM = 64
DIL_PATTERNS = ((128, 1), (512, 4), (2048, 16))
N_DIL = 3
DIL_HEADS = 6
CONV_WIDTH = 384
CONV_K = 3
NSA_Q_HEADS = 6
NSA_KV_HEADS = 2
CMP_BLOCK = 32
CMP_STRIDE = 16
CMP_HIDDEN = 128
SEL_BLOCK = 64
N_SEL = 16
NSA_WINDOW = 512
SWA_Q_HEADS = 6
SWA_WINDOW = 128
BRANCH_WIDTH = 384
N_BRANCH = 4
N_EXPERTS = 8
LN_EPS = 1e-5
NEG_INF = -1e30
DIL_WIDTH = N_DIL * DIL_HEADS * HEAD_DIM
COLUMN_SIZES = (DIL_WIDTH, DIL_WIDTH, DIL_WIDTH, CONV_WIDTH, CONV_WIDTH, CONV_WIDTH,
                NSA_Q_HEADS * HEAD_DIM, 128, 128, 128, 128, 128, 128, 3 * NSA_Q_HEADS,
                SWA_Q_HEADS * HEAD_DIM, 128, 128)
COL_OFF = np.concatenate([[0], np.cumsum(COLUMN_SIZES)]).tolist()

LANES = 128
V7X_VMEM_BYTES = 64 * 1024 * 1024
MIB = 1024 * 1024

CDT = jnp.bfloat16
F32 = jnp.float32
QK_SCALE = HEAD_DIM ** -0.5
SUB_Q = 128
BAND_ROWS = 64
BIG = 1e30

_GQA_HEAD_ORDER = (0, 3, 1, 4, 2, 5)
GQA_COL_PERM = np.concatenate([np.arange(h * HEAD_DIM, (h + 1) * HEAD_DIM) for h in _GQA_HEAD_ORDER])
GQA_COL_HEAD = GQA_COL_PERM // HEAD_DIM


def _cparams(sem, vmem_mib):
    return pltpu.CompilerParams(dimension_semantics=sem, vmem_limit_bytes=int(vmem_mib * MIB))


def _nt_dot(a, b):
    return lax.dot_general(a, b, (((1,), (1,)), ((), ())), preferred_element_type=F32)


def _dot(a, b):
    return jnp.dot(a, b, preferred_element_type=F32)


def _layer_norm(r, g, b):
    mu = jnp.mean(r, axis=-1, keepdims=True)
    d = r - mu
    var = jnp.mean(d * d, axis=-1, keepdims=True)
    return d * lax.rsqrt(var + LN_EPS) * g + b


def _half_masks():
    lane = lax.broadcasted_iota(jnp.int32, (1, LANES), 1)
    return lane < HEAD_DIM


def _linear_kernel(x_ref, w_ref, *refs, splits, n_chunk, dils):
    o_refs, z_ref = refs[:len(splits)], refs[len(splits)]
    xb = x_ref[...].astype(CDT)
    tm = xb.shape[0]
    for o_ref, (c0, c1), dil in zip(o_refs, splits, dils):
        width = c1 - c0
        if dil == 1:
            for a in range(c0, c1, n_chunk):
                b = min(a + n_chunk, c1)
                o_ref[:, a - c0:b - c0] = _dot(xb, w_ref[:, a:b]).astype(o_ref.dtype)
        else:
            z = _dot(xb, w_ref[:, c0:c1])
            for cb in range(width // LANES):
                z_ref[cb] = z[:, cb * LANES:(cb + 1) * LANES]
            for r in range(dil):
                for cb in range(width // LANES):
                    o_ref[:, r * width + cb * LANES:r * width + (cb + 1) * LANES] = (
                        z_ref[cb, pl.ds(r, tm // dil, stride=dil), :].astype(o_ref.dtype))


def _linear(x2d, w, splits, n_chunk, name, dils=None, tm=512):
    t, k = x2d.shape
    tm = min(tm, t)
    n = w.shape[1]
    dils = tuple(dils) if dils is not None else (1,) * len(splits)
    widths = [c1 - c0 for c0, c1 in splits]
    assert all(tm % (16 * dl) == 0 for dl in dils)
    return pl.pallas_call(
        functools.partial(_linear_kernel, splits=tuple(splits), n_chunk=n_chunk, dils=dils),
        grid=(t // tm,),
        in_specs=[pl.BlockSpec((tm, k), lambda i: (i, 0)),
                  pl.BlockSpec((k, n), lambda i: (0, 0))],
        out_specs=[pl.BlockSpec((tm // dl, dl * wd), lambda i: (i, 0)) for wd, dl in zip(widths, dils)],
        out_shape=[jax.ShapeDtypeStruct((t // dl, dl * wd), CDT) for wd, dl in zip(widths, dils)],
        scratch_shapes=[pltpu.VMEM((max(widths) // LANES, tm, LANES), F32)],
        compiler_params=_cparams(("parallel",), 48),
        name=name,
    )(x2d, w)


def _linear_t_kernel(x_ref, wt_ref, o_ref):
    o_ref[...] = _nt_dot(wt_ref[...], x_ref[...].astype(CDT)).astype(o_ref.dtype)


def _linear_t(x2d, wt, tm, name):
    t, k = x2d.shape
    n = wt.shape[0]
    return pl.pallas_call(
        _linear_t_kernel,
        grid=(t // tm,),
        in_specs=[pl.BlockSpec((tm, k), lambda i: (i, 0)), pl.BlockSpec((n, k), lambda i: (0, 0))],
        out_specs=pl.BlockSpec((None, n, tm), lambda i: (i, 0, 0)),
        out_shape=jax.ShapeDtypeStruct((t // tm, n, tm), CDT),
        compiler_params=_cparams(("parallel",), 32),
        name=name,
    )(x2d, wt)


def _conv_kernel(x_ref, xh_ref, wc_ref, cw_ref, ob_ref, *, tm):
    i = pl.program_id(1)
    w = CONV_WIDTH
    xb = x_ref[...].astype(CDT)
    z = _dot(xb, wc_ref[...])
    u = z[:, w:2 * w] * z[:, 2 * w:3 * w]
    zh = _dot(xh_ref[...].astype(CDT), wc_ref[:, w:3 * w])
    uh = zh[:, :w] * zh[:, w:]
    uh = jnp.where(i == 0, 0.0, uh)
    row = lax.broadcasted_iota(jnp.int32, (tm, w), 0)
    u1 = jnp.where(row == 0, uh[7:8, :], pltpu.roll(u, 1, 0))
    u2 = jnp.where(row == 0, uh[6:7, :], jnp.where(row == 1, uh[7:8, :], pltpu.roll(u, 2, 0)))
    y = cw_ref[0:1, :] * u2 + cw_ref[1:2, :] * u1 + cw_ref[2:3, :] * u
    ob_ref[...] = (z[:, :w] * y).astype(ob_ref.dtype)


def _conv(x, wc, conv_w, tm=512):
    b, s, d = x.shape
    tm = min(tm, s)
    hb = tm // 8
    return pl.pallas_call(
        functools.partial(_conv_kernel, tm=tm),
        grid=(b, s // tm),
        in_specs=[pl.BlockSpec((None, tm, d), lambda bi, i: (bi, i, 0)),
                  pl.BlockSpec((None, 8, d), lambda bi, i: (bi, jnp.maximum(i * hb - 1, 0), 0)),
                  pl.BlockSpec(wc.shape, lambda bi, i: (0, 0)),
                  pl.BlockSpec(conv_w.shape, lambda bi, i: (0, 0))],
        out_specs=pl.BlockSpec((None, tm, CONV_WIDTH), lambda bi, i: (bi, i, 0)),
        out_shape=jax.ShapeDtypeStruct((b, s, CONV_WIDTH), CDT),
        compiler_params=_cparams(("parallel", "parallel"), 48),
        name="short_conv",
    )(x, x, wc, conv_w)


def _banded_kernel(*refs, window, pr, tq, kw, want_lse, has_sink):
    q_ref, kp_ref, kc_ref, vp_ref, vc_ref = refs[:5]
    n = 5
    sink_ref = None
    if has_sink:
        sink_ref = refs[n]
        n += 1
    o_ref = refs[n]
    n += 1
    lse_ref = None
    if want_lse:
        lse_ref = refs[n]
        n += 1
    kbuf, vbuf, s_ref, e_ref, m_ref, l_ref = refs[n:n + 6]

    i = pl.program_id(2)
    kbuf[0:pr, :] = kp_ref[...]
    kbuf[pr:pr + tq, :] = kc_ref[...]
    vbuf[0:pr, :] = vp_ref[...]
    vbuf[pr:pr + tq, :] = vc_ref[...]

    span = SUB_Q + pr
    qi = lax.broadcasted_iota(jnp.int32, (SUB_Q, span), 0)
    kj = lax.broadcasted_iota(jnp.int32, (SUB_Q, span), 1)
    dist = pr + qi - kj
    band = (dist >= 0) & (dist <= window)
    lo = _half_masks()
    halves = (lo, jnp.logical_not(lo))
    groups = ((0,), (1,), (2,)) if kw == 3 * LANES else ((0, 1, 2),)
    rb = BAND_ROWS
    for sb in range(tq // SUB_Q):
        r0 = sb * SUB_Q
        bias = jnp.where(band & (i * tq + r0 - pr + kj >= 0), 0.0, NEG_INF)
        for grp in groups:
            kc0 = grp[0] * LANES if kw == 3 * LANES else 0
            qs = jnp.concatenate(
                [(jnp.where(hm, q_ref[r0:r0 + SUB_Q, p * LANES:(p + 1) * LANES], 0) * QK_SCALE).astype(CDT)
                 for p in grp for hm in halves], axis=0)
            g0 = 2 * grp[0] * SUB_Q
            s_ref[g0:g0 + qs.shape[0], :] = _nt_dot(qs, kbuf[r0:r0 + span, kc0:kc0 + LANES])
        for c0 in range(0, 6 * SUB_Q, rb):
            rows = slice(c0, c0 + rb)
            s = s_ref[rows, :] + bias[c0 % SUB_Q:c0 % SUB_Q + rb, :]
            m = jnp.max(s, axis=-1, keepdims=True)
            e = jnp.exp(s - m)
            e_ref[rows, :] = e.astype(CDT)
            m_ref[rows, :] = jnp.broadcast_to(m, (rb, LANES))
            l_ref[rows, :] = jnp.broadcast_to(jnp.sum(e, axis=-1, keepdims=True), (rb, LANES))
        for grp in groups:
            kc0 = grp[0] * LANES if kw == 3 * LANES else 0
            g0 = 2 * grp[0] * SUB_Q
            g1 = g0 + 2 * len(grp) * SUB_Q
            l = l_ref[g0:g1, :]
            o = _dot(e_ref[g0:g1, :], vbuf[r0:r0 + span, kc0:kc0 + LANES]) / l
            lse = m_ref[g0:g1, :] + jnp.log(l)
            for n_p, p in enumerate(grp):
                a = 2 * n_p * SUB_Q
                o_pair = jnp.where(lo, o[a:a + SUB_Q], o[a + SUB_Q:a + 2 * SUB_Q])
                lse_pair = jnp.where(lo, lse[a:a + SUB_Q], lse[a + SUB_Q:a + 2 * SUB_Q])
                if has_sink:
                    o_pair = o_pair * jax.nn.sigmoid(lse_pair - sink_ref[:, p * LANES:(p + 1) * LANES])
                o_ref[r0:r0 + SUB_Q, p * LANES:(p + 1) * LANES] = o_pair.astype(o_ref.dtype)
                if want_lse:
                    lse_ref[r0:r0 + SUB_Q, p * LANES:(p + 1) * LANES] = lse_pair


def _banded(qa, ka, va, *, nrep, qcol, kcol, vcol, kw, window, want_lse, sink_row=None, tq=512):
    b, l, _ = qa.shape
    pr = -(-window // SUB_Q) * SUB_Q
    tq = min(max(tq, pr), l)
    assert tq % pr == 0 and l % tq == 0, (tq, pr, l)
    ratio = tq // pr
    qw = 3 * LANES
    in_specs = [
        pl.BlockSpec((None, tq, qw), lambda bi, r, i: (bi, i, qcol(r))),
        pl.BlockSpec((None, pr, kw), lambda bi, r, i: (bi, jnp.maximum(i * ratio - 1, 0), kcol(r))),
        pl.BlockSpec((None, tq, kw), lambda bi, r, i: (bi, i, kcol(r))),
        pl.BlockSpec((None, pr, kw), lambda bi, r, i: (bi, jnp.maximum(i * ratio - 1, 0), vcol(r))),
        pl.BlockSpec((None, tq, kw), lambda bi, r, i: (bi, i, vcol(r))),
    ]
    args = [qa, ka, ka, va, va]
    if sink_row is not None:
        in_specs.append(pl.BlockSpec(sink_row.shape, lambda bi, r, i: (0, 0)))
        args.append(sink_row)
    out_specs = [pl.BlockSpec((None, tq, qw), lambda bi, r, i: (bi, i, r))]
    out_shape = [jax.ShapeDtypeStruct((b, l, nrep * qw), CDT)]
    if want_lse:
        out_specs.append(pl.BlockSpec((None, tq, qw), lambda bi, r, i: (bi, i, r)))
        out_shape.append(jax.ShapeDtypeStruct((b, l, nrep * qw), F32))
    res = pl.pallas_call(
        functools.partial(_banded_kernel, window=window, pr=pr, tq=tq, kw=kw, want_lse=want_lse,
                          has_sink=sink_row is not None),
        grid=(b, nrep, l // tq),
        in_specs=in_specs,
        out_specs=out_specs,
        out_shape=out_shape,
        scratch_shapes=[pltpu.VMEM((pr + tq, kw), ka.dtype), pltpu.VMEM((pr + tq, kw), va.dtype),
                        pltpu.VMEM((6 * SUB_Q, SUB_Q + pr), F32), pltpu.VMEM((6 * SUB_Q, SUB_Q + pr), CDT),
                        pltpu.VMEM((6 * SUB_Q, LANES), F32), pltpu.VMEM((6 * SUB_Q, LANES), F32)],
        compiler_params=_cparams(("parallel", "parallel", "parallel"), 32),
        name=f"banded_w{window}_k{kw}_r{nrep}",
    )(*args)
    return res


def _gelu_tanh(x):
    return 0.5 * x * (1.0 + jnp.tanh(0.7978845608028654 * (x + 0.044715 * (x * x * x))))


def _compress_kernel(x_ref, pa_ref, pb_ref, w1a_ref, w1b_ref, b1_ref, w2_ref, b2_ref, o_ref):
    x = x_ref[...].astype(F32)
    n = x.shape[0]
    a = _dot((x + pa_ref[...]).astype(CDT), w1a_ref[...])
    bm = _dot((x + pb_ref[...]).astype(CDT), w1b_ref[...])
    h = a + pltpu.roll(bm, n - 1, 0) + b1_ref[...]
    o_ref[...] = (_dot(_gelu_tanh(h).astype(CDT), w2_ref[...]) + b2_ref[...]).astype(o_ref.dtype)


def _compress(t, pos, w1, b1, w2, b2):
    b, s, _ = t.shape
    nch = s // CMP_STRIDE
    xw = CMP_STRIDE * LANES
    x = t.reshape(b, nch, xw)
    eye = jnp.eye(NSA_KV_HEADS, dtype=F32)
    w1r = w1.reshape(CMP_BLOCK, HEAD_DIM, CMP_HIDDEN)

    def expand_w1(part):
        return jnp.einsum('tdj,kl->tkdlj', part, eye).reshape(xw, NSA_KV_HEADS * CMP_HIDDEN).astype(CDT)

    def expand_pos(part):
        return jnp.broadcast_to(part[:, None, :], (CMP_STRIDE, NSA_KV_HEADS, HEAD_DIM)).reshape(1, xw)

    w1a, w1b = expand_w1(w1r[:CMP_STRIDE]), expand_w1(w1r[CMP_STRIDE:])
    pa, pb = expand_pos(pos[:CMP_STRIDE]), expand_pos(pos[CMP_STRIDE:])
    b1e = jnp.tile(b1, NSA_KV_HEADS).reshape(1, -1)
    w2e = jnp.einsum('jd,kl->kjld', w2, eye).reshape(NSA_KV_HEADS * CMP_HIDDEN, LANES).astype(CDT)
    b2e = jnp.tile(b2, NSA_KV_HEADS).reshape(1, -1)
    consts = [pa, pb, w1a, w1b, b1e, w2e, b2e]
    return pl.pallas_call(
        _compress_kernel,
        grid=(b,),
        in_specs=[pl.BlockSpec((None, nch, xw), lambda bi: (bi, 0, 0))]
        + [pl.BlockSpec(c.shape, lambda bi: (0, 0)) for c in consts],
        out_specs=pl.BlockSpec((None, nch, LANES), lambda bi: (bi, 0, 0)),
        out_shape=jax.ShapeDtypeStruct((b, nch, LANES), CDT),
        compiler_params=_cparams(("parallel",), 48),
        name="nsa_compress",
    )(x, *consts)


CMP_ROWS = 16


def _cmp_select_kernel(q_ref, kc_ref, vc_ref, ov_ref, o_ref, sel_ref, s_ref, p_ref, hi_ref, lo_ref,
                       *, tq, n_sel, tile0):
    i = pl.program_id(1) + tile0
    ncp = kc_ref.shape[0]
    ns = ov_ref.shape[1]
    lo = _half_masks()
    t_col = i * tq + lax.broadcasted_iota(jnp.int32, (tq, 1), 0)
    blk = lax.broadcasted_iota(jnp.int32, (tq, ns), 1)
    blk_t = lax.broadcasted_iota(jnp.int32, (ns, tq), 0)
    cur = t_col // SEL_BLOCK
    causal = blk <= cur
    forced = (blk == 0) | (blk == cur) | (blk == cur - 1)
    rb = CMP_ROWS
    c_end = lax.broadcasted_iota(jnp.int32, (rb, ncp), 1) * CMP_STRIDE + (CMP_BLOCK - 1)
    o_kv, work_t = [], []
    for kv, hm in enumerate((lo, jnp.logical_not(lo))):
        qs = jnp.concatenate(
            [(jnp.where(hm, q_ref[:, p * LANES:(p + 1) * LANES], 0) * QK_SCALE).astype(CDT) for p in range(3)], axis=0)
        s_ref[...] = _nt_dot(qs, kc_ref[...])
        for r0 in range(0, tq, rb):
            t_rows = i * tq + r0 + lax.broadcasted_iota(jnp.int32, (rb, 1), 0)
            vis_bias = jnp.where(c_end <= t_rows, 0.0, NEG_INF)
            has_visible = t_rows >= CMP_BLOCK - 1
            psum = jnp.zeros((rb, ncp), F32)
            for h in range(3):
                rows = slice(h * tq + r0, h * tq + r0 + rb)
                s = s_ref[rows, :] + vis_bias
                e = jnp.exp(s - jnp.max(s, axis=-1, keepdims=True))
                inv = jnp.where(has_visible, 1.0 / jnp.maximum(jnp.sum(e, axis=-1, keepdims=True), 1e-30), 0.0)
                pn = e * inv
                p_ref[rows, :] = pn.astype(CDT)
                psum = psum + pn
            p_hi = psum.astype(CDT)
            hi_ref[r0:r0 + rb, :] = p_hi
            lo_ref[r0:r0 + rb, :] = (psum - p_hi.astype(F32)).astype(CDT)
        o = _dot(p_ref[...], vc_ref[...])
        o_kv.append([o[p * tq:(p + 1) * tq] for p in range(3)])
        imp = _dot(hi_ref[...], ov_ref[...]) + _dot(lo_ref[...], ov_ref[...])
        work_t.append(jnp.where(causal & jnp.logical_not(forced), imp, -BIG).T)

    blk_lanes = blk_t[:, 0:LANES]

    def pick(_, work):
        m = jnp.max(work, axis=0, keepdims=True)
        idx = jnp.min(jnp.where(work == m, blk_lanes, ns), axis=0, keepdims=True)
        return jnp.where(blk_lanes == idx, -2.0 * BIG, work)

    for kv, start in enumerate(work_t):
        done = jnp.concatenate([lax.fori_loop(0, n_sel - 3, pick, start[:, c0:c0 + LANES])
                                for c0 in range(0, tq, LANES)], axis=1)
        taken = jnp.where((done < -BIG) & (start > -BIG), 1.0, 0.0).T
        selb = jnp.where(forced | (taken > 0.5), 0.0, NEG_INF)
        sel_ref[:, kv * ns:(kv + 1) * ns] = selb.astype(sel_ref.dtype)
    for p in range(3):
        o_ref[:, p * LANES:(p + 1) * LANES] = jnp.where(lo, o_kv[0][p], o_kv[1][p]).astype(o_ref.dtype)


CMP_CAUSAL_SPLITS = 4


def _cmp_select(zr, qcol, kc, vc, tq=256):
    b, s, _ = zr.shape
    ncp = kc.shape[1]
    ns = s // SEL_BLOCK
    n_sel = min(N_SEL, ns)
    assert n_sel >= 3, "selection needs room for the three forced blocks"
    tq = min(tq, s)
    c = np.arange(ncp)[:, None] * CMP_STRIDE
    j = np.arange(ns)[None, :] * SEL_BLOCK
    overlap = ((c < j + SEL_BLOCK) & (c + CMP_BLOCK - 1 >= j)).astype(np.float32)
    overlap[ncp - 1:, :] = 0.0
    ov = jnp.asarray(overlap, CDT)
    qw = 3 * LANES
    n_split = CMP_CAUSAL_SPLITS if (s // tq) % CMP_CAUSAL_SPLITS == 0 and ncp % (16 * CMP_CAUSAL_SPLITS) == 0 else 1
    tiles = s // tq // n_split
    outs, sels = [], []
    for part in range(n_split):
        tile0 = part * tiles
        ncp_part = ncp * (part + 1) // n_split
        o_part, sel_part = pl.pallas_call(
            functools.partial(_cmp_select_kernel, tq=tq, n_sel=n_sel, tile0=tile0),
            grid=(b, tiles),
            in_specs=[pl.BlockSpec((None, tq, qw), lambda bi, i, tile0=tile0: (bi, i + tile0, qcol)),
                      pl.BlockSpec((None, ncp_part, LANES), lambda bi, i: (bi, 0, 0)),
                      pl.BlockSpec((None, ncp_part, LANES), lambda bi, i: (bi, 0, 0)),
                      pl.BlockSpec((ncp_part, ns), lambda bi, i: (0, 0))],
            out_specs=[pl.BlockSpec((None, tq, qw), lambda bi, i: (bi, i, 0)),
                       pl.BlockSpec((None, tq, 2 * ns), lambda bi, i: (bi, i, 0))],
            out_shape=[jax.ShapeDtypeStruct((b, tiles * tq, qw), CDT),
                       jax.ShapeDtypeStruct((b, tiles * tq, 2 * ns), CDT)],
            scratch_shapes=[pltpu.VMEM((3 * tq, ncp_part), F32), pltpu.VMEM((3 * tq, ncp_part), CDT),
                            pltpu.VMEM((tq, ncp_part), CDT), pltpu.VMEM((tq, ncp_part), CDT)],
            compiler_params=_cparams(("parallel", "parallel"), 48),
            name=f"nsa_cmp_select_p{part}",
        )(zr, kc, vc, ov)
        outs.append(o_part)
        sels.append(sel_part)
    return jnp.concatenate(outs, axis=1), jnp.concatenate(sels, axis=1)


SLC_KEY_CHUNK = 256


def _slc_kernel(q_ref, k0_ref, k1_ref, vt_ref, selb_ref, o_ref, s_ref, e_ref, m_ref, l_ref, a_ref, acc_ref,
                *, tq, tk):
    i = pl.program_id(1)
    ns = selb_ref.shape[1] // 2
    bpt = tk // SEL_BLOCK
    n_q = 3 * tq
    rc = min(SLC_KEY_CHUNK, tk)
    lo = _half_masks()
    n_tiles = ((i + 1) * tq + tk - 1) // tk
    p_row = lax.broadcasted_iota(jnp.int32, (ns, LANES), 0)
    p_col = lax.broadcasted_iota(jnp.int32, (ns, LANES), 1)
    groups = []
    for kv, (hm, k_ref) in enumerate(((lo, k0_ref), (jnp.logical_not(lo), k1_ref))):
        q3 = [(jnp.where(hm, q_ref[:, p * LANES:(p + 1) * LANES], 0) * QK_SCALE).astype(CDT) for p in range(3)]
        selb = selb_ref[:, kv * ns:(kv + 1) * ns]
        lane0 = HEAD_DIM if kv == 0 else 0
        groups.append((kv, hm, k_ref, q3, selb, lane0))
    m_ref[...] = jnp.full(m_ref.shape, NEG_INF, F32)
    l_ref[...] = jnp.zeros(l_ref.shape, F32)
    acc_ref[...] = jnp.zeros(acc_ref.shape, F32)

    def scores(j):
        k0 = pl.multiple_of(j * tk, tk)
        for kv, hm, k_ref, q3, selb, lane0 in groups:
            place = ((p_col >= lane0) & (p_col < lane0 + bpt) & (p_row == p_col - lane0 + j * bpt)).astype(CDT)
            sb = _dot(selb, place).astype(CDT)
            qp = jnp.concatenate([jnp.where(hm, q, sb) for q in q3], axis=0)
            s_ref[kv] = _nt_dot(k_ref[pl.ds(k0, tk), :], qp)

    def softmax_pv(j, diagonal):
        k0 = pl.multiple_of(j * tk, tk)
        for kv in range(2):
            for c0 in range(0, n_q, LANES):
                cols = slice(c0, c0 + LANES)
                t_lane = i * tq + (c0 % tq) + lax.broadcasted_iota(jnp.int32, (1, LANES), 1)

                def chunk(r0):
                    s = s_ref[kv, r0:r0 + rc, cols]
                    if diagonal:
                        kpos = k0 + r0 + lax.broadcasted_iota(jnp.int32, (rc, LANES), 0)
                        s = jnp.where(kpos <= t_lane, s, NEG_INF)
                    return s

                m8 = m_ref[kv, :, cols]
                for r0 in range(0, tk, rc):
                    m8 = jnp.maximum(m8, jnp.max(chunk(r0).reshape(rc // 8, 8, LANES), axis=0))
                m_new = jnp.max(m8, axis=0, keepdims=True)
                alpha = jnp.exp(m_ref[kv, :, cols] - m_new)
                l8 = jnp.zeros((8, LANES), F32)
                for r0 in range(0, tk, rc):
                    e = jnp.exp(chunk(r0) - m_new)
                    l8 = l8 + jnp.sum(e.reshape(rc // 8, 8, LANES), axis=0)
                    e_ref[kv, r0:r0 + rc, cols] = e.astype(CDT)
                l_ref[kv, :, cols] = alpha * l_ref[kv, :, cols] + jnp.sum(l8, axis=0, keepdims=True)
                m_ref[kv, :, cols] = jnp.broadcast_to(m_new, (8, LANES))
                a_ref[kv, :, cols] = alpha
        for kv in range(2):
            acc_ref[kv] = a_ref[kv, 0:1, :] * acc_ref[kv] + _dot(vt_ref[j], e_ref[kv])

    def tile(j, diagonal):
        scores(j)
        softmax_pv(j, diagonal)

    lax.fori_loop(0, n_tiles - 1, lambda j, c: (tile(j, False), c)[1], 0)
    tile(n_tiles - 1, True)
    o0 = (acc_ref[0] / l_ref[0, 0:1, :]).T
    o1 = (acc_ref[1] / l_ref[1, 0:1, :]).T
    for p in range(3):
        o_ref[:, p * LANES:(p + 1) * LANES] = jnp.where(
            lo, o0[p * tq:(p + 1) * tq], o1[p * tq:(p + 1) * tq]).astype(o_ref.dtype)


def _slc(zr, qcol, ksc, vt, selb, tq=512):
    b, s, _ = zr.shape
    tq = min(tq, s)
    tk = vt.shape[2]
    assert tq % LANES == 0 and tk // SEL_BLOCK <= HEAD_DIM and s % tk == 0
    qw = 3 * LANES
    ns2 = selb.shape[2]
    pat = jax.nn.one_hot((jnp.arange(s) // SEL_BLOCK) % (tk // SEL_BLOCK), HEAD_DIM, dtype=ksc.dtype)
    pat = jnp.broadcast_to(pat[None], (b, s, HEAD_DIM))
    k0 = jnp.concatenate([ksc[..., :HEAD_DIM], pat], axis=-1)
    k1 = jnp.concatenate([pat, ksc[..., HEAD_DIM:]], axis=-1)
    full = lambda bi, i: (bi, 0, 0)
    return pl.pallas_call(
        functools.partial(_slc_kernel, tq=tq, tk=tk),
        grid=(b, s // tq),
        in_specs=[pl.BlockSpec((None, tq, qw), lambda bi, i: (bi, i, qcol)),
                  _single_buffered((None, s, LANES), full),
                  _single_buffered((None, s, LANES), full),
                  _single_buffered((s // tk, LANES, tk), lambda bi, i: (bi, 0, 0)),
                  pl.BlockSpec((None, tq, ns2), lambda bi, i: (bi, i, 0))],
        out_specs=pl.BlockSpec((None, tq, qw), lambda bi, i: (bi, i, 0)),
        out_shape=jax.ShapeDtypeStruct((b, s, qw), CDT),
        scratch_shapes=[pltpu.VMEM((2, tk, 3 * tq), F32), pltpu.VMEM((2, tk, 3 * tq), CDT),
                        pltpu.VMEM((2, 8, 3 * tq), F32), pltpu.VMEM((2, 8, 3 * tq), F32),
                        pltpu.VMEM((2, 8, 3 * tq), F32), pltpu.VMEM((2, LANES, 3 * tq), F32)],
        compiler_params=_cparams(("parallel", "arbitrary"), 56),
        name="nsa_slc",
    )(zr, k0, k1, vt, selb)


def _merge_kernel(x_ref, oa0, oa1, oa2, la0, la1, la2, ob, ocmp, oslc, owin, od,
                  wn_ref, wg_ref, bg_ref, wb_ref, wo_ref, g_ref, b_ref, o_ref, *scratch, alpha):
    x = x_ref[...]
    xb = x.astype(CDT)
    bw = BRANCH_WIDTH

    def token_rows(src_ref, dst_ref):
        dil = src_ref.shape[1] // bw
        if dil == 1:
            return src_ref[...].astype(F32)
        n_cb = bw // LANES
        for r in range(dil):
            for cb in range(n_cb):
                c0 = r * bw + cb * LANES
                dst_ref[cb, pl.ds(r, src_ref.shape[0], stride=dil), :] = src_ref[:, c0:c0 + LANES].astype(F32)
        return jnp.concatenate([dst_ref[cb] for cb in range(n_cb)], axis=1)

    o0, o1, o2 = (token_rows(s, d) for s, d in zip((oa0, oa1, oa2), scratch[0:3]))
    l0, l1, l2 = (token_rows(s, d) for s, d in zip((la0, la1, la2), scratch[3:6]))
    mx = jnp.maximum(jnp.maximum(l0, l1), l2)
    w0, w1, w2 = jnp.exp(l0 - mx), jnp.exp(l1 - mx), jnp.exp(l2 - mx)
    o_a = (w0 * o0 + w1 * o1 + w2 * o2) / (w0 + w1 + w2)
    gates = jax.nn.sigmoid(_dot(xb, wn_ref[...]))
    o_c = (gates[:, 0:bw] * ocmp[...].astype(F32) + gates[:, bw:2 * bw] * oslc[...].astype(F32)
           + gates[:, 2 * bw:3 * bw] * owin[...].astype(F32))
    branches = (o_a.astype(CDT), ob[...], o_c.astype(CDT), od[...])
    d = x.shape[1]
    merged = jnp.zeros(x.shape, F32)
    for m in range(N_BRANCH):
        gate = jax.nn.sigmoid(_dot(xb, wg_ref[:, m * d:(m + 1) * d]) + bg_ref[:, m * d:(m + 1) * d])
        merged = merged + gate * _dot(branches[m], wb_ref[m])
    r = alpha * x + _dot(merged.astype(CDT), wo_ref[...])
    o_ref[...] = _layer_norm(r, g_ref[...], b_ref[...])


def _merge(x2d, branch_inputs, wn, wg, bg, wb, wo, g, b, alpha, tm=256):
    t, d = x2d.shape
    tm = min(tm, t)
    row = lambda i: (i, 0)
    const2 = lambda i: (0, 0)
    in_specs = [pl.BlockSpec((tm, d), row)]
    in_specs += [pl.BlockSpec((tm * a.shape[0] // t, a.shape[1]), row) for a in branch_inputs]
    in_specs += [pl.BlockSpec(wn.shape, const2), pl.BlockSpec(wg.shape, const2), pl.BlockSpec(bg.shape, const2),
                 pl.BlockSpec(wb.shape, lambda i: (0, 0, 0)), pl.BlockSpec(wo.shape, const2),
                 pl.BlockSpec(g.shape, const2), pl.BlockSpec(b.shape, const2)]
    return pl.pallas_call(
        functools.partial(_merge_kernel, alpha=alpha),
        grid=(t // tm,),
        in_specs=in_specs,
        out_specs=pl.BlockSpec((tm, d), row),
        out_shape=jax.ShapeDtypeStruct((t, d), F32),
        scratch_shapes=[pltpu.VMEM((BRANCH_WIDTH // LANES, tm, LANES), F32) for _ in range(2 * N_DIL)],
        compiler_params=_cparams(("parallel",), 56),
        name="merge_ln",
    )(x2d, *branch_inputs, wn, wg, bg, wb, wo, g, b)


def _ple_ln(x, xb, f, p, plw_ref, pgw_ref, pgb_ref, g_ref, b_ref, alpha):
    ple = jax.nn.sigmoid(_dot(xb, pgw_ref[...]) + pgb_ref[...]) * _dot(p.astype(CDT), plw_ref[...])
    return _layer_norm(alpha * x + f + ple, g_ref[...], b_ref[...])


FFN_CHUNK = 512


def _ffn_kernel(x_ref, p_ref, wg_ref, wu_ref, wd_ref, plw_ref, pgw_ref, pgb_ref, g_ref, b_ref, o_ref, h_ref, *, alpha):
    x = x_ref[...]
    xb = x.astype(CDT)
    dff = wg_ref.shape[1]
    for c0 in range(0, dff, FFN_CHUNK):
        cols = slice(c0, min(c0 + FFN_CHUNK, dff))
        h_ref[:, cols] = (jax.nn.silu(_dot(xb, wg_ref[:, cols])) * _dot(xb, wu_ref[:, cols])).astype(CDT)
    f = _dot(h_ref[...], wd_ref[...])
    o_ref[...] = _ple_ln(x, xb, f, p_ref[...], plw_ref, pgw_ref, pgb_ref, g_ref, b_ref, alpha)


def _ffn(x2d, p2d, wg, wu, wd, plw, pgw, pgb, g, b, alpha, tm=512):
    t, d = x2d.shape
    tm = min(tm, t)
    dff = wg.shape[1]
    wg, wu, wd = wg.astype(CDT), wu.astype(CDT), wd.astype(CDT)
    row = lambda i: (i, 0)
    const = lambda shape: pl.BlockSpec(shape, lambda i: (0, 0), pipeline_mode=pl.Buffered(1))
    return pl.pallas_call(
        functools.partial(_ffn_kernel, alpha=alpha),
        grid=(t // tm,),
        in_specs=[pl.BlockSpec((tm, d), row), pl.BlockSpec((tm, p2d.shape[1]), row),
                  const(wg.shape), const(wu.shape), const(wd.shape),
                  const(plw.shape), const(pgw.shape), const(pgb.shape), const(g.shape), const(b.shape)],
        out_specs=pl.BlockSpec((tm, d), row),
        out_shape=jax.ShapeDtypeStruct((t, d), F32),
        scratch_shapes=[pltpu.VMEM((tm, dff), CDT)],
        compiler_params=_cparams(("parallel",), 48),
        name="ffn_ple_ln",
    )(x2d, p2d, wg, wu, wd, plw, pgw, pgb, g, b)


def _router_kernel(x_ref, wh_ref, wl_ref, b_ref, comb_ref, rank_ref, rank_t_ref, cnt_ref):
    x = x_ref[...]
    xh = x.astype(CDT)
    xl = (x - xh.astype(F32)).astype(CDT)
    logits = _dot(xh, wh_ref[...]) + _dot(xh, wl_ref[...]) + _dot(xl, wh_ref[...]) + b_ref[...]
    lane = lax.broadcasted_iota(jnp.int32, logits.shape, 1)
    v1 = jnp.max(logits, axis=-1, keepdims=True)
    i1 = jnp.min(jnp.where(logits == v1, lane, LANES), axis=-1, keepdims=True)
    rest = jnp.where(lane == i1, -jnp.inf, logits)
    v2 = jnp.max(rest, axis=-1, keepdims=True)
    i2 = jnp.min(jnp.where(rest == v2, lane, LANES), axis=-1, keepdims=True)
    e2 = jnp.exp(v2 - v1)
    comb_ref[...] = jnp.where(lane == i1, 1.0 / (1.0 + e2), 0.0) + jnp.where(lane == i2, e2 / (1.0 + e2), 0.0)
    routed = (lane == i1) | (lane == i2)
    mask = routed.astype(CDT)
    tm = x.shape[0]
    before = (lax.broadcasted_iota(jnp.int32, (tm, tm), 1) < lax.broadcasted_iota(jnp.int32, (tm, tm), 0)).astype(CDT)
    rank = jnp.where(routed, _dot(before, mask), -1.0)
    rank_ref[...] = rank
    rank_t_ref[...] = rank.T[0:rank_t_ref.shape[0], :]
    cnt_ref[...] = jnp.sum(routed.astype(F32), axis=0, keepdims=True).astype(jnp.int32)


def _router(x2d, w_router, b_router, tm):
    t, d = x2d.shape
    ne = w_router.shape[1]
    wp = jnp.zeros((d, LANES), F32).at[:, :ne].set(w_router)
    wh = wp.astype(CDT)
    wl = (wp - wh.astype(F32)).astype(CDT)
    bp = jnp.full((1, LANES), -BIG, F32).at[0, :ne].set(b_router)
    nt = t // tm
    row = lambda i: (i, 0)
    return pl.pallas_call(
        _router_kernel,
        grid=(nt,),
        in_specs=[pl.BlockSpec((tm, d), row), pl.BlockSpec(wh.shape, lambda i: (0, 0)),
                  pl.BlockSpec(wl.shape, lambda i: (0, 0)), pl.BlockSpec(bp.shape, lambda i: (0, 0))],
        out_specs=[pl.BlockSpec((tm, LANES), row), pl.BlockSpec((tm, LANES), row),
                   pl.BlockSpec((None, 8, tm), lambda i: (i, 0, 0)),
                   pl.BlockSpec((None, 1, LANES), lambda i: (i, 0, 0))],
        out_shape=[jax.ShapeDtypeStruct((t, LANES), F32), jax.ShapeDtypeStruct((t, LANES), F32),
                   jax.ShapeDtypeStruct((nt, 8, tm), F32), jax.ShapeDtypeStruct((nt, 1, LANES), jnp.int32)],
        compiler_params=_cparams(("parallel",), 40),
        name="moe_router",
    )(x2d, wh, wl, bp)


def _moe_kernel(cnt_ref, x_ref, comb_ref, rank_ref, rank_t_ref, p_ref, wg_ref, wu_ref, wd_ref, plw_ref, pgw_ref,
                pgb_ref, g_ref, b_ref, o_ref, xe_ref, ye_ref, *, alpha, rs, seg):
    i = pl.program_id(0)
    e = pl.program_id(1)
    c = pl.program_id(2)
    n_seg = x_ref.shape[0] // seg
    last_chunk = c == pl.num_programs(2) - 1

    first = slice(0, rs)
    toks = [slice(sg * seg, (sg + 1) * seg) for sg in range(n_seg)]
    n_groups = [(cnt_ref[(i * n_seg + sg) * LANES + e] + rs - 1) // rs for sg in range(n_seg)]

    def later(sc):
        return pl.ds(pl.multiple_of(sc * rs, 8), rs), (sc * rs).astype(F32)

    @pl.when((e == 0) & (c == 0))
    def _():
        o_ref[...] = jnp.zeros_like(o_ref)

    @pl.when(c == 0)
    def _():
        row_id = lax.broadcasted_iota(jnp.int32, (rs, seg), 0).astype(F32)
        xbs = [x_ref[tok, :].astype(CDT) for tok in toks]
        rank_rows = [rank_t_ref[sg, pl.ds(e, 1), :] for sg in range(n_seg)]

        def gather(sg, rws, base):
            onehot = (rank_rows[sg] - base == row_id).astype(CDT)
            xe_ref[sg, rws, :] = _dot(onehot, xbs[sg]).astype(CDT)
            ye_ref[sg, rws, :] = jnp.zeros((rs, ye_ref.shape[2]), F32)

        for sg in range(n_seg):
            gather(sg, first, 0.0)
        for sg in range(n_seg):
            lax.fori_loop(1, n_groups[sg], lambda sc, _, sg=sg: (gather(sg, *later(sc)), 0)[1], 0)

    def expert(sg, rws):
        xs = xe_ref[sg, rws, :]
        h = jax.nn.silu(_dot(xs, wg_ref[0])) * _dot(xs, wu_ref[0])
        ye_ref[sg, rws, :] += _dot(h.astype(CDT), wd_ref[0])

    for sg in range(n_seg):
        expert(sg, first)
    for sg in range(n_seg):
        lax.fori_loop(1, n_groups[sg], lambda sc, _, sg=sg: (expert(sg, later(sc)[0]), 0)[1], 0)

    @pl.when(last_chunk)
    def _():
        lane = lax.broadcasted_iota(jnp.int32, (seg, LANES), 1)
        mine = lane == e
        col_id = lax.broadcasted_iota(jnp.int32, (seg, rs), 1).astype(F32)
        cws = [jnp.sum(jnp.where(mine, comb_ref[tok, :], 0.0), axis=-1, keepdims=True) for tok in toks]
        rank_cols = [jnp.sum(jnp.where(mine, rank_ref[tok, :], 0.0), axis=-1, keepdims=True) for tok in toks]

        def scatter(sg, rws, base):
            onehot = (rank_cols[sg] - base == col_id).astype(CDT)
            o_ref[toks[sg], :] += cws[sg] * _dot(onehot, ye_ref[sg, rws, :].astype(CDT))

        for sg in range(n_seg):
            scatter(sg, first, 0.0)
        for sg in range(n_seg):
            lax.fori_loop(1, n_groups[sg], lambda sc, _, sg=sg: (scatter(sg, *later(sc)), 0)[1], 0)

    @pl.when((e == pl.num_programs(1) - 1) & last_chunk)
    def _():
        for sg in range(n_seg):
            tok = slice(sg * seg, (sg + 1) * seg)
            x = x_ref[tok, :]
            o_ref[tok, :] = _ple_ln(x, x.astype(CDT), o_ref[tok, :], p_ref[tok, :], plw_ref, pgw_ref, pgb_ref,
                                    g_ref, b_ref, alpha)


MOE_CHUNK = 512
MOE_SEGMENT = 1024
MOE_SEGMENTS_PER_TILE = 2
MOE_ROW_GROUP = 288


def _single_buffered(shape, index_map):
    return pl.BlockSpec(shape, index_map, pipeline_mode=pl.Buffered(1))


def _moe(x2d, routing, p2d, wg, wu, wd, plw, pgw, pgb, g, b, alpha, seg):
    comb, rank, rank_t, cnt = routing
    t, d = x2d.shape
    ne, _, dff = wg.shape
    ck = min(MOE_CHUNK, dff)
    rs = min(MOE_ROW_GROUP, seg)
    n_seg = min(MOE_SEGMENTS_PER_TILE, t // seg)
    tm = n_seg * seg
    max_rows = -(-seg // rs) * rs
    wg, wu, wd = wg.astype(CDT), wu.astype(CDT), wd.astype(CDT)
    row = lambda i, e, c, cnt: (i, 0)
    c2 = lambda i, e, c, cnt: (0, 0)
    grid_spec = pltpu.PrefetchScalarGridSpec(
        num_scalar_prefetch=1,
        grid=(t // tm, ne, dff // ck),
        in_specs=[_single_buffered((tm, d), row), _single_buffered((tm, LANES), row),
                  _single_buffered((tm, LANES), row),
                  _single_buffered((n_seg, 8, seg), lambda i, e, c, cnt: (i, 0, 0)),
                  _single_buffered((tm, p2d.shape[1]), row),
                  pl.BlockSpec((1, d, ck), lambda i, e, c, cnt: (e, 0, c)),
                  pl.BlockSpec((1, d, ck), lambda i, e, c, cnt: (e, 0, c)),
                  pl.BlockSpec((1, ck, d), lambda i, e, c, cnt: (e, c, 0)),
                  _single_buffered(plw.shape, c2), _single_buffered(pgw.shape, c2), _single_buffered(pgb.shape, c2),
                  _single_buffered(g.shape, c2), _single_buffered(b.shape, c2)],
        out_specs=pl.BlockSpec((tm, d), row),
        scratch_shapes=[pltpu.VMEM((n_seg, max_rows, d), CDT), pltpu.VMEM((n_seg, max_rows, d), F32)],
    )
    return pl.pallas_call(
        functools.partial(_moe_kernel, alpha=alpha, rs=rs, seg=seg),
        grid_spec=grid_spec,
        out_shape=jax.ShapeDtypeStruct((t, d), F32),
        compiler_params=_cparams(("parallel", "arbitrary", "arbitrary"), 60),
        name="moe_ple_ln",
    )(cnt.reshape(-1), x2d, comb, rank, rank_t, p2d, wg, wu, wd, plw, pgw, pgb, g, b)


def _prep_in_weights(w_in):
    o = COL_OFF
    bw = BRANCH_WIDTH
    cols = lambda n: w_in[:, o[n]:o[n + 1]]
    qa, ka, va = cols(0), cols(1), cols(2)
    w_dil = jnp.concatenate(
        [t[:, g * bw:(g + 1) * bw] for g in range(N_DIL) for t in (qa, ka, va)], axis=1).astype(CDT)
    w_conv = jnp.concatenate([cols(3), cols(4), cols(5)], axis=1).astype(CDT)
    gn = cols(13)
    w_gate = jnp.concatenate([gn[:, br * NSA_Q_HEADS + GQA_COL_HEAD] for br in range(3)], axis=1).astype(CDT)
    w_rest = jnp.concatenate([cols(6)[:, GQA_COL_PERM], cols(14)[:, GQA_COL_PERM], cols(11),
                              cols(12), cols(15), cols(16), cols(9), cols(7), cols(8)], axis=1).astype(CDT)
    wt_vsc = cols(10).T.astype(CDT)
    return w_dil, w_conv, w_gate, w_rest, wt_vsc


ZR_Q_NSA, ZR_Q_SWA = 0, 1
ZR_KWC, ZR_VWC, ZR_KD, ZR_VD = 6, 7, 8, 9
ZR_WIDTH = 2 * BRANCH_WIDTH + 4 * LANES
SLC_KEY_TILE = 1024


def _token_mixers(x, w_in, conv_w, cmp_pos, cmp_w1, cmp_b1, cmp_w2, cmp_b2, sinks):
    b, s, d = x.shape
    x2d = x.reshape(b * s, d)
    w_dil, w_conv, w_gate, w_rest, wt_vsc = _prep_in_weights(w_in)
    gw = 3 * BRANCH_WIDTH

    z_dil = _linear(x2d, w_dil, [(g * gw, (g + 1) * gw) for g in range(N_DIL)], gw, "in_proj_dil",
                    dils=[dil for _, dil in DIL_PATTERNS])
    zr, ksc, kcc, vcc = _linear(x2d, w_rest, [(0, ZR_WIDTH)] + [(ZR_WIDTH + n * LANES, ZR_WIDTH + (n + 1) * LANES)
                                                           for n in range(3)], 256, "in_proj_rest")
    zr = zr.reshape(b, s, ZR_WIDTH)
    o_b = _conv(x, w_conv, conv_w)

    dil_o, dil_lse = [], []
    for g, (window, dil) in enumerate(DIL_PATTERNS):
        view = z_dil[g].reshape(b, s // dil, dil * gw)
        og, lg = _banded(view, view, view, nrep=dil,
                         qcol=lambda r: 3 * r, kcol=lambda r: 3 * r + 1, vcol=lambda r: 3 * r + 2,
                         kw=3 * LANES, window=window // dil, want_lse=True)
        dil_o.append(og.reshape(b * s // dil, dil * BRANCH_WIDTH))
        dil_lse.append(lg.reshape(b * s // dil, dil * BRANCH_WIDTH))

    kc = _compress(kcc.reshape(b, s, LANES), cmp_pos[0], cmp_w1[0], cmp_b1[0], cmp_w2[0], cmp_b2[0])
    vc = _compress(vcc.reshape(b, s, LANES), cmp_pos[1], cmp_w1[1], cmp_b1[1], cmp_w2[1], cmp_b2[1])
    o_cmp, selb = _cmp_select(zr, ZR_Q_NSA, kc, vc)
    vsc_t = _linear_t(x2d, wt_vsc, min(SLC_KEY_TILE, s), "in_proj_vsc_t")
    o_slc = _slc(zr, ZR_Q_NSA, ksc.reshape(b, s, LANES), vsc_t, selb)
    (o_win,) = _banded(zr, zr, zr, nrep=1, qcol=lambda r: ZR_Q_NSA, kcol=lambda r: ZR_KWC, vcol=lambda r: ZR_VWC,
                       kw=LANES, window=NSA_WINDOW - 1, want_lse=False, tq=512)

    sink_row = sinks.astype(F32)[GQA_COL_HEAD].reshape(1, BRANCH_WIDTH)
    (o_d,) = _banded(zr, zr, zr, nrep=1, qcol=lambda r: ZR_Q_SWA, kcol=lambda r: ZR_KD, vcol=lambda r: ZR_VD,
                     kw=LANES, window=SWA_WINDOW - 1, want_lse=False, sink_row=sink_row)

    t = b * s
    flat = lambda a: a.reshape(t, a.shape[-1])
    return [dil_o[0], dil_o[1], dil_o[2], dil_lse[0], dil_lse[1], dil_lse[2], flat(o_b), flat(o_cmp), flat(o_slc),
            flat(o_win), flat(o_d)], w_gate


def kernel(x, p, w_in, conv_w, cmp_pos, cmp_w1, cmp_b1, cmp_w2, cmp_b2, sinks, w_branch, w_merge_gate, b_merge_gate, w_out, ln_mix_g, ln_mix_b, ffn_w_gate, ffn_w_up, ffn_w_down, w_router, b_router, moe_w_gate, moe_w_up, moe_w_down, ple_w, ple_gate_w, ple_gate_b, ln_ffn_g, ln_ffn_b):
    depth, b, s, _ = p.shape
    d = x.shape[-1]
    t = b * s
    alpha = (2 * depth) ** 0.25
    row = lambda v: v.reshape(1, -1).astype(F32)
    for i in range(depth):
        branch_inputs, w_nsa_gate = _token_mixers(x, w_in[i], conv_w[i], cmp_pos[i], cmp_w1[i], cmp_b1[i], cmp_w2[i],
                                                  cmp_b2[i], sinks[i])
        wg = jnp.concatenate([w_merge_gate[i, m] for m in range(N_BRANCH)], axis=1).astype(CDT)
        bg = b_merge_gate[i].reshape(1, N_BRANCH * d).astype(F32)
        wb = jnp.stack([w_branch[i, 0], w_branch[i, 1], w_branch[i, 2][GQA_COL_PERM],
                        w_branch[i, 3][GQA_COL_PERM]]).astype(CDT)
        x1 = _merge(x.reshape(t, d), branch_inputs, w_nsa_gate, wg, bg, wb, w_out[i].astype(CDT),
                    row(ln_mix_g[i]), row(ln_mix_b[i]), alpha)
        p2d = p[i].reshape(t, -1)
        ple_args = (ple_w[i].astype(CDT), ple_gate_w[i].astype(CDT), row(ple_gate_b[i]),
                    row(ln_ffn_g[i]), row(ln_ffn_b[i]))
        j = i // 2
        if i % 2 == 0:
            x2 = _ffn(x1, p2d, ffn_w_gate[j], ffn_w_up[j], ffn_w_down[j], *ple_args, alpha)
        else:
            seg = min(MOE_SEGMENT, t)
            routing = _router(x1, w_router[j], b_router[j], seg)
            x2 = _moe(x1, routing, p2d, moe_w_gate[j], moe_w_up[j], moe_w_down[j], *ple_args, alpha, seg)
        x = x2.reshape(b, s, d)
    return x
```

```python
import functools

import numpy as np
import jax
import jax.numpy as jnp
from jax import lax
from jax.experimental import pallas as pl
from jax.experimental.pallas import tpu as pltpu

D_MODEL = 1024
HEAD_DIM = 64
DIL_PATTERNS = ((128, 1), (512, 4), (2048, 16))
N_DIL = 3
DIL_HEADS = 6
CONV_WIDTH = 384
CONV_K = 3
NSA_Q_HEADS = 6
NSA_KV_HEADS = 2
CMP_BLOCK = 32
CMP_STRIDE = 16
CMP_HIDDEN = 128
SEL_BLOCK = 64
N_SEL = 16
NSA_WINDOW = 512
SWA_Q_HEADS = 6
SWA_WINDOW = 128
BRANCH_WIDTH = 384
N_BRANCH = 4
N_EXPERTS = 8
LN_EPS = 1e-5
NEG_INF = -1e30
DIL_WIDTH = N_DIL * DIL_HEADS * HEAD_DIM
COLUMN_SIZES = (DIL_WIDTH, DIL_WIDTH, DIL_WIDTH, CONV_WIDTH, CONV_WIDTH, CONV_WIDTH,
                NSA_Q_HEADS * HEAD_DIM, 128, 128, 128, 128, 128, 128, 3 * NSA_Q_HEADS,
                SWA_Q_HEADS * HEAD_DIM, 128, 128)
COL_OFF = np.concatenate([[0], np.cumsum(COLUMN_SIZES)]).tolist()

LANES = 128
V7X_VMEM_BYTES = 64 * 1024 * 1024
MIB = 1024 * 1024

CDT = jnp.bfloat16
F32 = jnp.float32
QK_SCALE = HEAD_DIM ** -0.5
SUB_Q = 128
BAND_ROWS = 64
BIG = 1e30

_GQA_HEAD_ORDER = (0, 3, 1, 4, 2, 5)
GQA_COL_PERM = np.concatenate([np.arange(h * HEAD_DIM, (h + 1) * HEAD_DIM) for h in _GQA_HEAD_ORDER])
GQA_COL_HEAD = GQA_COL_PERM // HEAD_DIM


def _cparams(sem, vmem_mib):
    return pltpu.CompilerParams(dimension_semantics=sem, vmem_limit_bytes=int(vmem_mib * MIB))


def _nt_dot(a, b):
    return lax.dot_general(a, b, (((1,), (1,)), ((), ())), preferred_element_type=F32)


def _dot(a, b):
    return jnp.dot(a, b, preferred_element_type=F32)


def _layer_norm(r, g, b):
    mu = jnp.mean(r, axis=-1, keepdims=True)
    d = r - mu
    var = jnp.mean(d * d, axis=-1, keepdims=True)
    return d * lax.rsqrt(var + LN_EPS) * g + b


def _half_masks():
    lane = lax.broadcasted_iota(jnp.int32, (1, LANES), 1)
    return lane < HEAD_DIM


def _linear_kernel(x_ref, w_ref, *refs, splits, n_chunk, dils):
    o_refs, z_ref = refs[:len(splits)], refs[len(splits)]
    xb = x_ref[...].astype(CDT)
    tm = xb.shape[0]
    for o_ref, (c0, c1), dil in zip(o_refs, splits, dils):
        width = c1 - c0
        if dil == 1:
            for a in range(c0, c1, n_chunk):
                b = min(a + n_chunk, c1)
                o_ref[:, a - c0:b - c0] = _dot(xb, w_ref[:, a:b]).astype(o_ref.dtype)
        else:
            z = _dot(xb, w_ref[:, c0:c1])
            for cb in range(width // LANES):
                z_ref[cb] = z[:, cb * LANES:(cb + 1) * LANES]
            for r in range(dil):
                for cb in range(width // LANES):
                    o_ref[:, r * width + cb * LANES:r * width + (cb + 1) * LANES] = (
                        z_ref[cb, pl.ds(r, tm // dil, stride=dil), :].astype(o_ref.dtype))


def _linear(x2d, w, splits, n_chunk, name, dils=None, tm=512):
    t, k = x2d.shape
    tm = min(tm, t)
    n = w.shape[1]
    dils = tuple(dils) if dils is not None else (1,) * len(splits)
    widths = [c1 - c0 for c0, c1 in splits]
    assert all(tm % (16 * dl) == 0 for dl in dils)
    return pl.pallas_call(
        functools.partial(_linear_kernel, splits=tuple(splits), n_chunk=n_chunk, dils=dils),
        grid=(t // tm,),
        in_specs=[pl.BlockSpec((tm, k), lambda i: (i, 0)),
                  pl.BlockSpec((k, n), lambda i: (0, 0))],
        out_specs=[pl.BlockSpec((tm // dl, dl * wd), lambda i: (i, 0)) for wd, dl in zip(widths, dils)],
        out_shape=[jax.ShapeDtypeStruct((t // dl, dl * wd), CDT) for wd, dl in zip(widths, dils)],
        scratch_shapes=[pltpu.VMEM((max(widths) // LANES, tm, LANES), F32)],
        compiler_params=_cparams(("parallel",), 48),
        name=name,
    )(x2d, w)


def _linear_t_kernel(x_ref, wt_ref, o_ref):
    o_ref[...] = _nt_dot(wt_ref[...], x_ref[...].astype(CDT)).astype(o_ref.dtype)


def _linear_t(x2d, wt, tm, name):
    t, k = x2d.shape
    n = wt.shape[0]
    return pl.pallas_call(
        _linear_t_kernel,
        grid=(t // tm,),
        in_specs=[pl.BlockSpec((tm, k), lambda i: (i, 0)), pl.BlockSpec((n, k), lambda i: (0, 0))],
        out_specs=pl.BlockSpec((None, n, tm), lambda i: (i, 0, 0)),
        out_shape=jax.ShapeDtypeStruct((t // tm, n, tm), CDT),
        compiler_params=_cparams(("parallel",), 32),
        name=name,
    )(x2d, wt)


def _conv_kernel(x_ref, xh_ref, wc_ref, cw_ref, ob_ref, *, tm):
    i = pl.program_id(1)
    w = CONV_WIDTH
    xb = x_ref[...].astype(CDT)
    z = _dot(xb, wc_ref[...])
    u = z[:, w:2 * w] * z[:, 2 * w:3 * w]
    zh = _dot(xh_ref[...].astype(CDT), wc_ref[:, w:3 * w])
    uh = zh[:, :w] * zh[:, w:]
    uh = jnp.where(i == 0, 0.0, uh)
    row = lax.broadcasted_iota(jnp.int32, (tm, w), 0)
    u1 = jnp.where(row == 0, uh[7:8, :], pltpu.roll(u, 1, 0))
    u2 = jnp.where(row == 0, uh[6:7, :], jnp.where(row == 1, uh[7:8, :], pltpu.roll(u, 2, 0)))
    y = cw_ref[0:1, :] * u2 + cw_ref[1:2, :] * u1 + cw_ref[2:3, :] * u
    ob_ref[...] = (z[:, :w] * y).astype(ob_ref.dtype)


def _conv(x, wc, conv_w, tm=512):
    b, s, d = x.shape
    tm = min(tm, s)
    hb = tm // 8
    return pl.pallas_call(
        functools.partial(_conv_kernel, tm=tm),
        grid=(b, s // tm),
        in_specs=[pl.BlockSpec((None, tm, d), lambda bi, i: (bi, i, 0)),
                  pl.BlockSpec((None, 8, d), lambda bi, i: (bi, jnp.maximum(i * hb - 1, 0), 0)),
                  pl.BlockSpec(wc.shape, lambda bi, i: (0, 0)),
                  pl.BlockSpec(conv_w.shape, lambda bi, i: (0, 0))],
        out_specs=pl.BlockSpec((None, tm, CONV_WIDTH), lambda bi, i: (bi, i, 0)),
        out_shape=jax.ShapeDtypeStruct((b, s, CONV_WIDTH), CDT),
        compiler_params=_cparams(("parallel", "parallel"), 48),
        name="short_conv",
    )(x, x, wc, conv_w)


def _banded_kernel(*refs, window, pr, tq, kw, want_lse, has_sink):
    q_ref, kp_ref, kc_ref, vp_ref, vc_ref = refs[:5]
    n = 5
    sink_ref = None
    if has_sink:
        sink_ref = refs[n]
        n += 1
    o_ref = refs[n]
    n += 1
    lse_ref = None
    if want_lse:
        lse_ref = refs[n]
        n += 1
    kbuf, vbuf, s_ref, e_ref, m_ref, l_ref = refs[n:n + 6]

    i = pl.program_id(2)
    kbuf[0:pr, :] = kp_ref[...]
    kbuf[pr:pr + tq, :] = kc_ref[...]
    vbuf[0:pr, :] = vp_ref[...]
    vbuf[pr:pr + tq, :] = vc_ref[...]

    span = SUB_Q + pr
    qi = lax.broadcasted_iota(jnp.int32, (SUB_Q, span), 0)
    kj = lax.broadcasted_iota(jnp.int32, (SUB_Q, span), 1)
    dist = pr + qi - kj
    band = (dist >= 0) & (dist <= window)
    lo = _half_masks()
    halves = (lo, jnp.logical_not(lo))
    groups = ((0,), (1,), (2,)) if kw == 3 * LANES else ((0, 1, 2),)
    rb = BAND_ROWS
    for sb in range(tq // SUB_Q):
        r0 = sb * SUB_Q
        bias = jnp.where(band & (i * tq + r0 - pr + kj >= 0), 0.0, NEG_INF)
        for grp in groups:
            kc0 = grp[0] * LANES if kw == 3 * LANES else 0
            qs = jnp.concatenate(
                [(jnp.where(hm, q_ref[r0:r0 + SUB_Q, p * LANES:(p + 1) * LANES], 0) * QK_SCALE).astype(CDT)
                 for p in grp for hm in halves], axis=0)
            g0 = 2 * grp[0] * SUB_Q
            s_ref[g0:g0 + qs.shape[0], :] = _nt_dot(qs, kbuf[r0:r0 + span, kc0:kc0 + LANES])
        for c0 in range(0, 6 * SUB_Q, rb):
            rows = slice(c0, c0 + rb)
            s = s_ref[rows, :] + bias[c0 % SUB_Q:c0 % SUB_Q + rb, :]
            m = jnp.max(s, axis=-1, keepdims=True)
            e = jnp.exp(s - m)
            e_ref[rows, :] = e.astype(CDT)
            m_ref[rows, :] = jnp.broadcast_to(m, (rb, LANES))
            l_ref[rows, :] = jnp.broadcast_to(jnp.sum(e, axis=-1, keepdims=True), (rb, LANES))
        for grp in groups:
            kc0 = grp[0] * LANES if kw == 3 * LANES else 0
            g0 = 2 * grp[0] * SUB_Q
            g1 = g0 + 2 * len(grp) * SUB_Q
            l = l_ref[g0:g1, :]
            o = _dot(e_ref[g0:g1, :], vbuf[r0:r0 + span, kc0:kc0 + LANES]) / l
            lse = m_ref[g0:g1, :] + jnp.log(l)
            for n_p, p in enumerate(grp):
                a = 2 * n_p * SUB_Q
                o_pair = jnp.where(lo, o[a:a + SUB_Q], o[a + SUB_Q:a + 2 * SUB_Q])
                lse_pair = jnp.where(lo, lse[a:a + SUB_Q], lse[a + SUB_Q:a + 2 * SUB_Q])
                if has_sink:
                    o_pair = o_pair * jax.nn.sigmoid(lse_pair - sink_ref[:, p * LANES:(p + 1) * LANES])
                o_ref[r0:r0 + SUB_Q, p * LANES:(p + 1) * LANES] = o_pair.astype(o_ref.dtype)
                if want_lse:
                    lse_ref[r0:r0 + SUB_Q, p * LANES:(p + 1) * LANES] = lse_pair


def _banded(qa, ka, va, *, nrep, qcol, kcol, vcol, kw, window, want_lse, sink_row=None, tq=512):
    b, l, _ = qa.shape
    pr = -(-window // SUB_Q) * SUB_Q
    tq = min(max(tq, pr), l)
    assert tq % pr == 0 and l % tq == 0, (tq, pr, l)
    ratio = tq // pr
    qw = 3 * LANES
    in_specs = [
        pl.BlockSpec((None, tq, qw), lambda bi, r, i: (bi, i, qcol(r))),
        pl.BlockSpec((None, pr, kw), lambda bi, r, i: (bi, jnp.maximum(i * ratio - 1, 0), kcol(r))),
        pl.BlockSpec((None, tq, kw), lambda bi, r, i: (bi, i, kcol(r))),
        pl.BlockSpec((None, pr, kw), lambda bi, r, i: (bi, jnp.maximum(i * ratio - 1, 0), vcol(r))),
        pl.BlockSpec((None, tq, kw), lambda bi, r, i: (bi, i, vcol(r))),
    ]
    args = [qa, ka, ka, va, va]
    if sink_row is not None:
        in_specs.append(pl.BlockSpec(sink_row.shape, lambda bi, r, i: (0, 0)))
        args.append(sink_row)
    out_specs = [pl.BlockSpec((None, tq, qw), lambda bi, r, i: (bi, i, r))]
    out_shape = [jax.ShapeDtypeStruct((b, l, nrep * qw), CDT)]
    if want_lse:
        out_specs.append(pl.BlockSpec((None, tq, qw), lambda bi, r, i: (bi, i, r)))
        out_shape.append(jax.ShapeDtypeStruct((b, l, nrep * qw), F32))
    res = pl.pallas_call(
        functools.partial(_banded_kernel, window=window, pr=pr, tq=tq, kw=kw, want_lse=want_lse,
                          has_sink=sink_row is not None),
        grid=(b, nrep, l // tq),
        in_specs=in_specs,
        out_specs=out_specs,
        out_shape=out_shape,
        scratch_shapes=[pltpu.VMEM((pr + tq, kw), ka.dtype), pltpu.VMEM((pr + tq, kw), va.dtype),
                        pltpu.VMEM((6 * SUB_Q, SUB_Q + pr), F32), pltpu.VMEM((6 * SUB_Q, SUB_Q + pr), CDT),
                        pltpu.VMEM((6 * SUB_Q, LANES), F32), pltpu.VMEM((6 * SUB_Q, LANES), F32)],
        compiler_params=_cparams(("parallel", "parallel", "parallel"), 32),
        name=f"banded_w{window}_k{kw}_r{nrep}",
    )(*args)
    return res


def _gelu_tanh(x):
    return 0.5 * x * (1.0 + jnp.tanh(0.7978845608028654 * (x + 0.044715 * (x * x * x))))


def _compress_kernel(x_ref, pa_ref, pb_ref, w1a_ref, w1b_ref, b1_ref, w2_ref, b2_ref, o_ref):
    x = x_ref[...].astype(F32)
    n = x.shape[0]
    a = _dot((x + pa_ref[...]).astype(CDT), w1a_ref[...])
    bm = _dot((x + pb_ref[...]).astype(CDT), w1b_ref[...])
    h = a + pltpu.roll(bm, n - 1, 0) + b1_ref[...]
    o_ref[...] = (_dot(_gelu_tanh(h).astype(CDT), w2_ref[...]) + b2_ref[...]).astype(o_ref.dtype)


def _compress(t, pos, w1, b1, w2, b2):
    b, s, _ = t.shape
    nch = s // CMP_STRIDE
    xw = CMP_STRIDE * LANES
    x = t.reshape(b, nch, xw)
    eye = jnp.eye(NSA_KV_HEADS, dtype=F32)
    w1r = w1.reshape(CMP_BLOCK, HEAD_DIM, CMP_HIDDEN)

    def expand_w1(part):
        return jnp.einsum('tdj,kl->tkdlj', part, eye).reshape(xw, NSA_KV_HEADS * CMP_HIDDEN).astype(CDT)

    def expand_pos(part):
        return jnp.broadcast_to(part[:, None, :], (CMP_STRIDE, NSA_KV_HEADS, HEAD_DIM)).reshape(1, xw)

    w1a, w1b = expand_w1(w1r[:CMP_STRIDE]), expand_w1(w1r[CMP_STRIDE:])
    pa, pb = expand_pos(pos[:CMP_STRIDE]), expand_pos(pos[CMP_STRIDE:])
    b1e = jnp.tile(b1, NSA_KV_HEADS).reshape(1, -1)
    w2e = jnp.einsum('jd,kl->kjld', w2, eye).reshape(NSA_KV_HEADS * CMP_HIDDEN, LANES).astype(CDT)
    b2e = jnp.tile(b2, NSA_KV_HEADS).reshape(1, -1)
    consts = [pa, pb, w1a, w1b, b1e, w2e, b2e]
    return pl.pallas_call(
        _compress_kernel,
        grid=(b,),
        in_specs=[pl.BlockSpec((None, nch, xw), lambda bi: (bi, 0, 0))]
        + [pl.BlockSpec(c.shape, lambda bi: (0, 0)) for c in consts],
        out_specs=pl.BlockSpec((None, nch, LANES), lambda bi: (bi, 0, 0)),
        out_shape=jax.ShapeDtypeStruct((b, nch, LANES), CDT),
        compiler_params=_cparams(("parallel",), 48),
        name="nsa_compress",
    )(x, *consts)


CMP_ROWS = 16


def _cmp_select_kernel(q_ref, kc_ref, vc_ref, ov_ref, o_ref, sel_ref, s_ref, p_ref, hi_ref, lo_ref,
                       *, tq, n_sel, tile0):
    i = pl.program_id(1) + tile0
    ncp = kc_ref.shape[0]
    ns = ov_ref.shape[1]
    lo = _half_masks()
    t_col = i * tq + lax.broadcasted_iota(jnp.int32, (tq, 1), 0)
    blk = lax.broadcasted_iota(jnp.int32, (tq, ns), 1)
    blk_t = lax.broadcasted_iota(jnp.int32, (ns, tq), 0)
    cur = t_col // SEL_BLOCK
    causal = blk <= cur
    forced = (blk == 0) | (blk == cur) | (blk == cur - 1)
    rb = CMP_ROWS
    c_end = lax.broadcasted_iota(jnp.int32, (rb, ncp), 1) * CMP_STRIDE + (CMP_BLOCK - 1)
    o_kv, work_t = [], []
    for kv, hm in enumerate((lo, jnp.logical_not(lo))):
        qs = jnp.concatenate(
            [(jnp.where(hm, q_ref[:, p * LANES:(p + 1) * LANES], 0) * QK_SCALE).astype(CDT) for p in range(3)], axis=0)
        s_ref[...] = _nt_dot(qs, kc_ref[...])
        for r0 in range(0, tq, rb):
            t_rows = i * tq + r0 + lax.broadcasted_iota(jnp.int32, (rb, 1), 0)
            vis_bias = jnp.where(c_end <= t_rows, 0.0, NEG_INF)
            has_visible = t_rows >= CMP_BLOCK - 1
            psum = jnp.zeros((rb, ncp), F32)
            for h in range(3):
                rows = slice(h * tq + r0, h * tq + r0 + rb)
                s = s_ref[rows, :] + vis_bias
                e = jnp.exp(s - jnp.max(s, axis=-1, keepdims=True))
                inv = jnp.where(has_visible, 1.0 / jnp.maximum(jnp.sum(e, axis=-1, keepdims=True), 1e-30), 0.0)
                pn = e * inv
                p_ref[rows, :] = pn.astype(CDT)
                psum = psum + pn
            p_hi = psum.astype(CDT)
            hi_ref[r0:r0 + rb, :] = p_hi
            lo_ref[r0:r0 + rb, :] = (psum - p_hi.astype(F32)).astype(CDT)
        o = _dot(p_ref[...], vc_ref[...])
        o_kv.append([o[p * tq:(p + 1) * tq] for p in range(3)])
        imp = _dot(hi_ref[...], ov_ref[...]) + _dot(lo_ref[...], ov_ref[...])
        work_t.append(jnp.where(causal & jnp.logical_not(forced), imp, -BIG).T)

    blk_lanes = blk_t[:, 0:LANES]

    def pick(_, work):
        m = jnp.max(work, axis=0, keepdims=True)
        idx = jnp.min(jnp.where(work == m, blk_lanes, ns), axis=0, keepdims=True)
        return jnp.where(blk_lanes == idx, -2.0 * BIG, work)

    for kv, start in enumerate(work_t):
        done = jnp.concatenate([lax.fori_loop(0, n_sel - 3, pick, start[:, c0:c0 + LANES], unroll=True)
                                for c0 in range(0, tq, LANES)], axis=1)
        taken = jnp.where((done < -BIG) & (start > -BIG), 1.0, 0.0).T
        selb = jnp.where(forced | (taken > 0.5), 0.0, NEG_INF)
        sel_ref[:, kv * ns:(kv + 1) * ns] = selb.astype(sel_ref.dtype)
    for p in range(3):
        o_ref[:, p * LANES:(p + 1) * LANES] = jnp.where(lo, o_kv[0][p], o_kv[1][p]).astype(o_ref.dtype)


CMP_CAUSAL_SPLITS = 4


def _cmp_select(zr, qcol, kc, vc, tq=256):
    b, s, _ = zr.shape
    ncp = kc.shape[1]
    ns = s // SEL_BLOCK
    n_sel = min(N_SEL, ns)
    assert n_sel >= 3, "selection needs room for the three forced blocks"
    tq = min(tq, s)
    c = np.arange(ncp)[:, None] * CMP_STRIDE
    j = np.arange(ns)[None, :] * SEL_BLOCK
    overlap = ((c < j + SEL_BLOCK) & (c + CMP_BLOCK - 1 >= j)).astype(np.float32)
    overlap[ncp - 1:, :] = 0.0
    ov = jnp.asarray(overlap, CDT)
    qw = 3 * LANES
    n_split = CMP_CAUSAL_SPLITS if (s // tq) % CMP_CAUSAL_SPLITS == 0 and ncp % (16 * CMP_CAUSAL_SPLITS) == 0 else 1
    tiles = s // tq // n_split
    outs, sels = [], []
    for part in range(n_split):
        tile0 = part * tiles
        ncp_part = ncp * (part + 1) // n_split
        o_part, sel_part = pl.pallas_call(
            functools.partial(_cmp_select_kernel, tq=tq, n_sel=n_sel, tile0=tile0),
            grid=(b, tiles),
            in_specs=[pl.BlockSpec((None, tq, qw), lambda bi, i, tile0=tile0: (bi, i + tile0, qcol)),
                      pl.BlockSpec((None, ncp_part, LANES), lambda bi, i: (bi, 0, 0)),
                      pl.BlockSpec((None, ncp_part, LANES), lambda bi, i: (bi, 0, 0)),
                      pl.BlockSpec((ncp_part, ns), lambda bi, i: (0, 0))],
            out_specs=[pl.BlockSpec((None, tq, qw), lambda bi, i: (bi, i, 0)),
                       pl.BlockSpec((None, tq, 2 * ns), lambda bi, i: (bi, i, 0))],
            out_shape=[jax.ShapeDtypeStruct((b, tiles * tq, qw), CDT),
                       jax.ShapeDtypeStruct((b, tiles * tq, 2 * ns), CDT)],
            scratch_shapes=[pltpu.VMEM((3 * tq, ncp_part), F32), pltpu.VMEM((3 * tq, ncp_part), CDT),
                            pltpu.VMEM((tq, ncp_part), CDT), pltpu.VMEM((tq, ncp_part), CDT)],
            compiler_params=_cparams(("parallel", "parallel"), 48),
            name=f"nsa_cmp_select_p{part}",
        )(zr, kc, vc, ov)
        outs.append(o_part)
        sels.append(sel_part)
    return jnp.concatenate(outs, axis=1), jnp.concatenate(sels, axis=1)


SLC_KEY_CHUNK = 256


def _slc_kernel(q_ref, k0_ref, k1_ref, vt_ref, selb_ref, o_ref, s_ref, e_ref, m_ref, l_ref, a_ref, acc_ref,
                *, tq, tk):
    i = pl.program_id(1)
    ns = selb_ref.shape[1] // 2
    bpt = tk // SEL_BLOCK
    n_q = 3 * tq
    rc = min(SLC_KEY_CHUNK, tk)
    lo = _half_masks()
    n_tiles = ((i + 1) * tq + tk - 1) // tk
    p_row = lax.broadcasted_iota(jnp.int32, (ns, LANES), 0)
    p_col = lax.broadcasted_iota(jnp.int32, (ns, LANES), 1)
    groups = []
    for kv, (hm, k_ref) in enumerate(((lo, k0_ref), (jnp.logical_not(lo), k1_ref))):
        q3 = [(jnp.where(hm, q_ref[:, p * LANES:(p + 1) * LANES], 0) * QK_SCALE).astype(CDT) for p in range(3)]
        selb = selb_ref[:, kv * ns:(kv + 1) * ns]
        lane0 = HEAD_DIM if kv == 0 else 0
        groups.append((kv, hm, k_ref, q3, selb, lane0))
    m_ref[...] = jnp.full(m_ref.shape, NEG_INF, F32)
    l_ref[...] = jnp.zeros(l_ref.shape, F32)
    acc_ref[...] = jnp.zeros(acc_ref.shape, F32)

    def scores(j):
        k0 = pl.multiple_of(j * tk, tk)
        for kv, hm, k_ref, q3, selb, lane0 in groups:
            place = ((p_col >= lane0) & (p_col < lane0 + bpt) & (p_row == p_col - lane0 + j * bpt)).astype(CDT)
            sb = _dot(selb, place).astype(CDT)
            qp = jnp.concatenate([jnp.where(hm, q, sb) for q in q3], axis=0)
            s_ref[kv] = _nt_dot(k_ref[pl.ds(k0, tk), :], qp)

    def softmax_pv(j, diagonal):
        k0 = pl.multiple_of(j * tk, tk)
        for kv in range(2):
            for c0 in range(0, n_q, LANES):
                cols = slice(c0, c0 + LANES)
                t_lane = i * tq + (c0 % tq) + lax.broadcasted_iota(jnp.int32, (1, LANES), 1)

                def chunk(r0):
                    s = s_ref[kv, r0:r0 + rc, cols]
                    if diagonal:
                        kpos = k0 + r0 + lax.broadcasted_iota(jnp.int32, (rc, LANES), 0)
                        s = jnp.where(kpos <= t_lane, s, NEG_INF)
                    return s

                m8 = m_ref[kv, :, cols]
                for r0 in range(0, tk, rc):
                    m8 = jnp.maximum(m8, jnp.max(chunk(r0).reshape(rc // 8, 8, LANES), axis=0))
                m_new = jnp.max(m8, axis=0, keepdims=True)
                alpha = jnp.exp(m_ref[kv, :, cols] - m_new)
                l8 = jnp.zeros((8, LANES), F32)
                for r0 in range(0, tk, rc):
                    e = jnp.exp(chunk(r0) - m_new)
                    l8 = l8 + jnp.sum(e.reshape(rc // 8, 8, LANES), axis=0)
                    e_ref[kv, r0:r0 + rc, cols] = e.astype(CDT)
                l_ref[kv, :, cols] = alpha * l_ref[kv, :, cols] + jnp.sum(l8, axis=0, keepdims=True)
                m_ref[kv, :, cols] = jnp.broadcast_to(m_new, (8, LANES))
                a_ref[kv, :, cols] = alpha
        for kv in range(2):
            acc_ref[kv] = a_ref[kv, 0:1, :] * acc_ref[kv] + _dot(vt_ref[j], e_ref[kv])

    def tile(j, diagonal):
        scores(j)
        softmax_pv(j, diagonal)

    lax.fori_loop(0, n_tiles - 1, lambda j, c: (tile(j, False), c)[1], 0)
    tile(n_tiles - 1, True)
    o0 = (acc_ref[0] / l_ref[0, 0:1, :]).T
    o1 = (acc_ref[1] / l_ref[1, 0:1, :]).T
    for p in range(3):
        o_ref[:, p * LANES:(p + 1) * LANES] = jnp.where(
            lo, o0[p * tq:(p + 1) * tq], o1[p * tq:(p + 1) * tq]).astype(o_ref.dtype)


def _slc(zr, qcol, ksc, vt, selb, tq=512):
    b, s, _ = zr.shape
    tq = min(tq, s)
    tk = vt.shape[2]
    assert tq % LANES == 0 and tk // SEL_BLOCK <= HEAD_DIM and s % tk == 0
    qw = 3 * LANES
    ns2 = selb.shape[2]
    pat = jax.nn.one_hot((jnp.arange(s) // SEL_BLOCK) % (tk // SEL_BLOCK), HEAD_DIM, dtype=ksc.dtype)
    pat = jnp.broadcast_to(pat[None], (b, s, HEAD_DIM))
    k0 = jnp.concatenate([ksc[..., :HEAD_DIM], pat], axis=-1)
    k1 = jnp.concatenate([pat, ksc[..., HEAD_DIM:]], axis=-1)
    full = lambda bi, i: (bi, 0, 0)
    return pl.pallas_call(
        functools.partial(_slc_kernel, tq=tq, tk=tk),
        grid=(b, s // tq),
        in_specs=[pl.BlockSpec((None, tq, qw), lambda bi, i: (bi, i, qcol)),
                  _single_buffered((None, s, LANES), full),
                  _single_buffered((None, s, LANES), full),
                  _single_buffered((s // tk, LANES, tk), lambda bi, i: (bi, 0, 0)),
                  pl.BlockSpec((None, tq, ns2), lambda bi, i: (bi, i, 0))],
        out_specs=pl.BlockSpec((None, tq, qw), lambda bi, i: (bi, i, 0)),
        out_shape=jax.ShapeDtypeStruct((b, s, qw), CDT),
        scratch_shapes=[pltpu.VMEM((2, tk, 3 * tq), F32), pltpu.VMEM((2, tk, 3 * tq), CDT),
                        pltpu.VMEM((2, 8, 3 * tq), F32), pltpu.VMEM((2, 8, 3 * tq), F32),
                        pltpu.VMEM((2, 8, 3 * tq), F32), pltpu.VMEM((2, LANES, 3 * tq), F32)],
        compiler_params=_cparams(("parallel", "arbitrary"), 56),
        name="nsa_slc",
    )(zr, k0, k1, vt, selb)


def _merge_kernel(x_ref, oa0, oa1, oa2, la0, la1, la2, ob, ocmp, oslc, owin, od,
                  wn_ref, wg_ref, bg_ref, wb_ref, wo_ref, g_ref, b_ref, o_ref, *scratch, alpha):
    x = x_ref[...]
    xb = x.astype(CDT)
    bw = BRANCH_WIDTH

    def token_rows(src_ref, dst_ref):
        dil = src_ref.shape[1] // bw
        if dil == 1:
            return src_ref[...].astype(F32)
        n_cb = bw // LANES
        for r in range(dil):
            for cb in range(n_cb):
                c0 = r * bw + cb * LANES
                dst_ref[cb, pl.ds(r, src_ref.shape[0], stride=dil), :] = src_ref[:, c0:c0 + LANES].astype(F32)
        return jnp.concatenate([dst_ref[cb] for cb in range(n_cb)], axis=1)

    o0, o1, o2 = (token_rows(s, d) for s, d in zip((oa0, oa1, oa2), scratch[0:3]))
    l0, l1, l2 = (token_rows(s, d) for s, d in zip((la0, la1, la2), scratch[3:6]))
    mx = jnp.maximum(jnp.maximum(l0, l1), l2)
    w0, w1, w2 = jnp.exp(l0 - mx), jnp.exp(l1 - mx), jnp.exp(l2 - mx)
    o_a = (w0 * o0 + w1 * o1 + w2 * o2) / (w0 + w1 + w2)
    gates = jax.nn.sigmoid(_dot(xb, wn_ref[...]))
    o_c = (gates[:, 0:bw] * ocmp[...].astype(F32) + gates[:, bw:2 * bw] * oslc[...].astype(F32)
           + gates[:, 2 * bw:3 * bw] * owin[...].astype(F32))
    branches = (o_a.astype(CDT), ob[...], o_c.astype(CDT), od[...])
    d = x.shape[1]
    merged = jnp.zeros(x.shape, F32)
    for m in range(N_BRANCH):
        gate = jax.nn.sigmoid(_dot(xb, wg_ref[:, m * d:(m + 1) * d]) + bg_ref[:, m * d:(m + 1) * d])
        merged = merged + gate * _dot(branches[m], wb_ref[m])
    r = alpha * x + _dot(merged.astype(CDT), wo_ref[...])
    o_ref[...] = _layer_norm(r, g_ref[...], b_ref[...])


def _merge(x2d, branch_inputs, wn, wg, bg, wb, wo, g, b, alpha, tm=256):
    t, d = x2d.shape
    tm = min(tm, t)
    row = lambda i: (i, 0)
    const2 = lambda i: (0, 0)
    in_specs = [pl.BlockSpec((tm, d), row)]
    in_specs += [pl.BlockSpec((tm * a.shape[0] // t, a.shape[1]), row) for a in branch_inputs]
    in_specs += [pl.BlockSpec(wn.shape, const2), pl.BlockSpec(wg.shape, const2), pl.BlockSpec(bg.shape, const2),
                 pl.BlockSpec(wb.shape, lambda i: (0, 0, 0)), pl.BlockSpec(wo.shape, const2),
                 pl.BlockSpec(g.shape, const2), pl.BlockSpec(b.shape, const2)]
    return pl.pallas_call(
        functools.partial(_merge_kernel, alpha=alpha),
        grid=(t // tm,),
        in_specs=in_specs,
        out_specs=pl.BlockSpec((tm, d), row),
        out_shape=jax.ShapeDtypeStruct((t, d), F32),
        scratch_shapes=[pltpu.VMEM((BRANCH_WIDTH // LANES, tm, LANES), F32) for _ in range(2 * N_DIL)],
        compiler_params=_cparams(("parallel",), 56),
        name="merge_ln",
    )(x2d, *branch_inputs, wn, wg, bg, wb, wo, g, b)


def _ple_ln(x, xb, f, p, plw_ref, pgw_ref, pgb_ref, g_ref, b_ref, alpha):
    ple = jax.nn.sigmoid(_dot(xb, pgw_ref[...]) + pgb_ref[...]) * _dot(p.astype(CDT), plw_ref[...])
    return _layer_norm(alpha * x + f + ple, g_ref[...], b_ref[...])


FFN_CHUNK = 512


def _ffn_kernel(x_ref, p_ref, wg_ref, wu_ref, wd_ref, plw_ref, pgw_ref, pgb_ref, g_ref, b_ref, o_ref, h_ref, *, alpha):
    x = x_ref[...]
    xb = x.astype(CDT)
    dff = wg_ref.shape[1]
    for c0 in range(0, dff, FFN_CHUNK):
        cols = slice(c0, min(c0 + FFN_CHUNK, dff))
        h_ref[:, cols] = (jax.nn.silu(_dot(xb, wg_ref[:, cols])) * _dot(xb, wu_ref[:, cols])).astype(CDT)
    f = _dot(h_ref[...], wd_ref[...])
    o_ref[...] = _ple_ln(x, xb, f, p_ref[...], plw_ref, pgw_ref, pgb_ref, g_ref, b_ref, alpha)


def _ffn(x2d, p2d, wg, wu, wd, plw, pgw, pgb, g, b, alpha, tm=512):
    t, d = x2d.shape
    tm = min(tm, t)
    dff = wg.shape[1]
    wg, wu, wd = wg.astype(CDT), wu.astype(CDT), wd.astype(CDT)
    row = lambda i: (i, 0)
    const = lambda shape: pl.BlockSpec(shape, lambda i: (0, 0), pipeline_mode=pl.Buffered(1))
    return pl.pallas_call(
        functools.partial(_ffn_kernel, alpha=alpha),
        grid=(t // tm,),
        in_specs=[pl.BlockSpec((tm, d), row), pl.BlockSpec((tm, p2d.shape[1]), row),
                  const(wg.shape), const(wu.shape), const(wd.shape),
                  const(plw.shape), const(pgw.shape), const(pgb.shape), const(g.shape), const(b.shape)],
        out_specs=pl.BlockSpec((tm, d), row),
        out_shape=jax.ShapeDtypeStruct((t, d), F32),
        scratch_shapes=[pltpu.VMEM((tm, dff), CDT)],
        compiler_params=_cparams(("parallel",), 48),
        name="ffn_ple_ln",
    )(x2d, p2d, wg, wu, wd, plw, pgw, pgb, g, b)


def _router_kernel(x_ref, wh_ref, wl_ref, b_ref, comb_ref, rank_ref, rank_t_ref, cnt_ref):
    x = x_ref[...]
    xh = x.astype(CDT)
    xl = (x - xh.astype(F32)).astype(CDT)
    logits = _dot(xh, wh_ref[...]) + _dot(xh, wl_ref[...]) + _dot(xl, wh_ref[...]) + b_ref[...]
    lane = lax.broadcasted_iota(jnp.int32, logits.shape, 1)
    v1 = jnp.max(logits, axis=-1, keepdims=True)
    i1 = jnp.min(jnp.where(logits == v1, lane, LANES), axis=-1, keepdims=True)
    rest = jnp.where(lane == i1, -jnp.inf, logits)
    v2 = jnp.max(rest, axis=-1, keepdims=True)
    i2 = jnp.min(jnp.where(rest == v2, lane, LANES), axis=-1, keepdims=True)
    e2 = jnp.exp(v2 - v1)
    comb_ref[...] = jnp.where(lane == i1, 1.0 / (1.0 + e2), 0.0) + jnp.where(lane == i2, e2 / (1.0 + e2), 0.0)
    routed = (lane == i1) | (lane == i2)
    mask = routed.astype(CDT)
    tm = x.shape[0]
    before = (lax.broadcasted_iota(jnp.int32, (tm, tm), 1) < lax.broadcasted_iota(jnp.int32, (tm, tm), 0)).astype(CDT)
    rank = jnp.where(routed, _dot(before, mask), -1.0)
    rank_ref[...] = rank
    rank_t_ref[...] = rank.T[0:rank_t_ref.shape[0], :]
    cnt_ref[...] = jnp.sum(routed.astype(F32), axis=0, keepdims=True).astype(jnp.int32)


def _router(x2d, w_router, b_router, tm):
    t, d = x2d.shape
    ne = w_router.shape[1]
    wp = jnp.zeros((d, LANES), F32).at[:, :ne].set(w_router)
    wh = wp.astype(CDT)
    wl = (wp - wh.astype(F32)).astype(CDT)
    bp = jnp.full((1, LANES), -BIG, F32).at[0, :ne].set(b_router)
    nt = t // tm
    row = lambda i: (i, 0)
    return pl.pallas_call(
        _router_kernel,
        grid=(nt,),
        in_specs=[pl.BlockSpec((tm, d), row), pl.BlockSpec(wh.shape, lambda i: (0, 0)),
                  pl.BlockSpec(wl.shape, lambda i: (0, 0)), pl.BlockSpec(bp.shape, lambda i: (0, 0))],
        out_specs=[pl.BlockSpec((tm, LANES), row), pl.BlockSpec((tm, LANES), row),
                   pl.BlockSpec((None, 8, tm), lambda i: (i, 0, 0)),
                   pl.BlockSpec((None, 1, LANES), lambda i: (i, 0, 0))],
        out_shape=[jax.ShapeDtypeStruct((t, LANES), F32), jax.ShapeDtypeStruct((t, LANES), F32),
                   jax.ShapeDtypeStruct((nt, 8, tm), F32), jax.ShapeDtypeStruct((nt, 1, LANES), jnp.int32)],
        compiler_params=_cparams(("parallel",), 40),
        name="moe_router",
    )(x2d, wh, wl, bp)


def _moe_kernel(cnt_ref, x_ref, comb_ref, rank_ref, rank_t_ref, p_ref, wg_ref, wu_ref, wd_ref, plw_ref, pgw_ref,
                pgb_ref, g_ref, b_ref, o_ref, xe_ref, ye_ref, *, alpha, rs, seg):
    i = pl.program_id(0)
    e = pl.program_id(1)
    c = pl.program_id(2)
    n_seg = x_ref.shape[0] // seg
    last_chunk = c == pl.num_programs(2) - 1

    first = slice(0, rs)
    toks = [slice(sg * seg, (sg + 1) * seg) for sg in range(n_seg)]
    n_groups = [(cnt_ref[(i * n_seg + sg) * LANES + e] + rs - 1) // rs for sg in range(n_seg)]

    def later(sc):
        return pl.ds(pl.multiple_of(sc * rs, 8), rs), (sc * rs).astype(F32)

    @pl.when((e == 0) & (c == 0))
    def _():
        o_ref[...] = jnp.zeros_like(o_ref)

    @pl.when(c == 0)
    def _():
        row_id = lax.broadcasted_iota(jnp.int32, (rs, seg), 0).astype(F32)
        xbs = [x_ref[tok, :].astype(CDT) for tok in toks]
        rank_rows = [rank_t_ref[sg, pl.ds(e, 1), :] for sg in range(n_seg)]

        def gather(sg, rws, base):
            onehot = (rank_rows[sg] - base == row_id).astype(CDT)
            xe_ref[sg, rws, :] = _dot(onehot, xbs[sg]).astype(CDT)
            ye_ref[sg, rws, :] = jnp.zeros((rs, ye_ref.shape[2]), F32)

        for sg in range(n_seg):
            gather(sg, first, 0.0)
        for sg in range(n_seg):
            lax.fori_loop(1, n_groups[sg], lambda sc, _, sg=sg: (gather(sg, *later(sc)), 0)[1], 0)

    def expert(sg, rws):
        xs = xe_ref[sg, rws, :]
        h = jax.nn.silu(_dot(xs, wg_ref[0])) * _dot(xs, wu_ref[0])
        ye_ref[sg, rws, :] += _dot(h.astype(CDT), wd_ref[0])

    for sg in range(n_seg):
        expert(sg, first)
    for sg in range(n_seg):
        lax.fori_loop(1, n_groups[sg], lambda sc, _, sg=sg: (expert(sg, later(sc)[0]), 0)[1], 0)

    @pl.when(last_chunk)
    def _():
        lane = lax.broadcasted_iota(jnp.int32, (seg, LANES), 1)
        mine = lane == e
        col_id = lax.broadcasted_iota(jnp.int32, (seg, rs), 1).astype(F32)
        cws = [jnp.sum(jnp.where(mine, comb_ref[tok, :], 0.0), axis=-1, keepdims=True) for tok in toks]
        rank_cols = [jnp.sum(jnp.where(mine, rank_ref[tok, :], 0.0), axis=-1, keepdims=True) for tok in toks]

        def scatter(sg, rws, base):
            onehot = (rank_cols[sg] - base == col_id).astype(CDT)
            o_ref[toks[sg], :] += cws[sg] * _dot(onehot, ye_ref[sg, rws, :].astype(CDT))

        for sg in range(n_seg):
            scatter(sg, first, 0.0)
        for sg in range(n_seg):
            lax.fori_loop(1, n_groups[sg], lambda sc, _, sg=sg: (scatter(sg, *later(sc)), 0)[1], 0)

    @pl.when((e == pl.num_programs(1) - 1) & last_chunk)
    def _():
        for sg in range(n_seg):
            tok = slice(sg * seg, (sg + 1) * seg)
            x = x_ref[tok, :]
            o_ref[tok, :] = _ple_ln(x, x.astype(CDT), o_ref[tok, :], p_ref[tok, :], plw_ref, pgw_ref, pgb_ref,
                                    g_ref, b_ref, alpha)


MOE_CHUNK = 512
MOE_SEGMENT = 1024
MOE_SEGMENTS_PER_TILE = 2
MOE_ROW_GROUP = 288


def _single_buffered(shape, index_map):
    return pl.BlockSpec(shape, index_map, pipeline_mode=pl.Buffered(1))


def _moe(x2d, routing, p2d, wg, wu, wd, plw, pgw, pgb, g, b, alpha, seg):
    comb, rank, rank_t, cnt = routing
    t, d = x2d.shape
    ne, _, dff = wg.shape
    ck = min(MOE_CHUNK, dff)
    rs = min(MOE_ROW_GROUP, seg)
    n_seg = min(MOE_SEGMENTS_PER_TILE, t // seg)
    tm = n_seg * seg
    max_rows = -(-seg // rs) * rs
    wg, wu, wd = wg.astype(CDT), wu.astype(CDT), wd.astype(CDT)
    row = lambda i, e, c, cnt: (i, 0)
    c2 = lambda i, e, c, cnt: (0, 0)
    grid_spec = pltpu.PrefetchScalarGridSpec(
        num_scalar_prefetch=1,
        grid=(t // tm, ne, dff // ck),
        in_specs=[_single_buffered((tm, d), row), _single_buffered((tm, LANES), row),
                  _single_buffered((tm, LANES), row),
                  _single_buffered((n_seg, 8, seg), lambda i, e, c, cnt: (i, 0, 0)),
                  _single_buffered((tm, p2d.shape[1]), row),
                  pl.BlockSpec((1, d, ck), lambda i, e, c, cnt: (e, 0, c)),
                  pl.BlockSpec((1, d, ck), lambda i, e, c, cnt: (e, 0, c)),
                  pl.BlockSpec((1, ck, d), lambda i, e, c, cnt: (e, c, 0)),
                  _single_buffered(plw.shape, c2), _single_buffered(pgw.shape, c2), _single_buffered(pgb.shape, c2),
                  _single_buffered(g.shape, c2), _single_buffered(b.shape, c2)],
        out_specs=pl.BlockSpec((tm, d), row),
        scratch_shapes=[pltpu.VMEM((n_seg, max_rows, d), CDT), pltpu.VMEM((n_seg, max_rows, d), F32)],
    )
    return pl.pallas_call(
        functools.partial(_moe_kernel, alpha=alpha, rs=rs, seg=seg),
        grid_spec=grid_spec,
        out_shape=jax.ShapeDtypeStruct((t, d), F32),
        compiler_params=_cparams(("parallel", "arbitrary", "arbitrary"), 60),
        name="moe_ple_ln",
    )(cnt.reshape(-1), x2d, comb, rank, rank_t, p2d, wg, wu, wd, plw, pgw, pgb, g, b)


def _prep_in_weights(w_in):
    o = COL_OFF
    bw = BRANCH_WIDTH
    cols = lambda n: w_in[:, o[n]:o[n + 1]]
    qa, ka, va = cols(0), cols(1), cols(2)
    w_dil = jnp.concatenate(
        [t[:, g * bw:(g + 1) * bw] for g in range(N_DIL) for t in (qa, ka, va)], axis=1).astype(CDT)
    w_conv = jnp.concatenate([cols(3), cols(4), cols(5)], axis=1).astype(CDT)
    gn = cols(13)
    w_gate = jnp.concatenate([gn[:, br * NSA_Q_HEADS + GQA_COL_HEAD] for br in range(3)], axis=1).astype(CDT)
    w_rest = jnp.concatenate([cols(6)[:, GQA_COL_PERM], cols(14)[:, GQA_COL_PERM], cols(11),
                              cols(12), cols(15), cols(16), cols(9), cols(7), cols(8)], axis=1).astype(CDT)
    wt_vsc = cols(10).T.astype(CDT)
    return w_dil, w_conv, w_gate, w_rest, wt_vsc


ZR_Q_NSA, ZR_Q_SWA = 0, 1
ZR_KWC, ZR_VWC, ZR_KD, ZR_VD = 6, 7, 8, 9
ZR_WIDTH = 2 * BRANCH_WIDTH + 4 * LANES
SLC_KEY_TILE = 1024


def _token_mixers(x, w_in, conv_w, cmp_pos, cmp_w1, cmp_b1, cmp_w2, cmp_b2, sinks):
    b, s, d = x.shape
    x2d = x.reshape(b * s, d)
    w_dil, w_conv, w_gate, w_rest, wt_vsc = _prep_in_weights(w_in)
    gw = 3 * BRANCH_WIDTH

    z_dil = _linear(x2d, w_dil, [(g * gw, (g + 1) * gw) for g in range(N_DIL)], gw, "in_proj_dil",
                    dils=[dil for _, dil in DIL_PATTERNS])
    zr, ksc, kcc, vcc = _linear(x2d, w_rest, [(0, ZR_WIDTH)] + [(ZR_WIDTH + n * LANES, ZR_WIDTH + (n + 1) * LANES)
                                                           for n in range(3)], 256, "in_proj_rest")
    zr = zr.reshape(b, s, ZR_WIDTH)
    o_b = _conv(x, w_conv, conv_w)

    dil_o, dil_lse = [], []
    for g, (window, dil) in enumerate(DIL_PATTERNS):
        view = z_dil[g].reshape(b, s // dil, dil * gw)
        og, lg = _banded(view, view, view, nrep=dil,
                         qcol=lambda r: 3 * r, kcol=lambda r: 3 * r + 1, vcol=lambda r: 3 * r + 2,
                         kw=3 * LANES, window=window // dil, want_lse=True)
        dil_o.append(og.reshape(b * s // dil, dil * BRANCH_WIDTH))
        dil_lse.append(lg.reshape(b * s // dil, dil * BRANCH_WIDTH))

    kc = _compress(kcc.reshape(b, s, LANES), cmp_pos[0], cmp_w1[0], cmp_b1[0], cmp_w2[0], cmp_b2[0])
    vc = _compress(vcc.reshape(b, s, LANES), cmp_pos[1], cmp_w1[1], cmp_b1[1], cmp_w2[1], cmp_b2[1])
    o_cmp, selb = _cmp_select(zr, ZR_Q_NSA, kc, vc)
    vsc_t = _linear_t(x2d, wt_vsc, min(SLC_KEY_TILE, s), "in_proj_vsc_t")
    o_slc = _slc(zr, ZR_Q_NSA, ksc.reshape(b, s, LANES), vsc_t, selb)
    (o_win,) = _banded(zr, zr, zr, nrep=1, qcol=lambda r: ZR_Q_NSA, kcol=lambda r: ZR_KWC, vcol=lambda r: ZR_VWC,
                       kw=LANES, window=NSA_WINDOW - 1, want_lse=False, tq=512)

    sink_row = sinks.astype(F32)[GQA_COL_HEAD].reshape(1, BRANCH_WIDTH)
    (o_d,) = _banded(zr, zr, zr, nrep=1, qcol=lambda r: ZR_Q_SWA, kcol=lambda r: ZR_KD, vcol=lambda r: ZR_VD,
                     kw=LANES, window=SWA_WINDOW - 1, want_lse=False, sink_row=sink_row)

    t = b * s
    flat = lambda a: a.reshape(t, a.shape[-1])
    return [dil_o[0], dil_o[1], dil_o[2], dil_lse[0], dil_lse[1], dil_lse[2], flat(o_b), flat(o_cmp), flat(o_slc),
            flat(o_win), flat(o_d)], w_gate


def kernel(x, p, w_in, conv_w, cmp_pos, cmp_w1, cmp_b1, cmp_w2, cmp_b2, sinks, w_branch, w_merge_gate, b_merge_gate, w_out, ln_mix_g, ln_mix_b, ffn_w_gate, ffn_w_up, ffn_w_down, w_router, b_router, moe_w_gate, moe_w_up, moe_w_down, ple_w, ple_gate_w, ple_gate_b, ln_ffn_g, ln_ffn_b):
    depth, b, s, _ = p.shape
    d = x.shape[-1]
    t = b * s
    alpha = (2 * depth) ** 0.25
    row = lambda v: v.reshape(1, -1).astype(F32)
    for i in range(depth):
        branch_inputs, w_nsa_gate = _token_mixers(x, w_in[i], conv_w[i], cmp_pos[i], cmp_w1[i], cmp_b1[i], cmp_w2[i],
                                                  cmp_b2[i], sinks[i])
        wg = jnp.concatenate([w_merge_gate[i, m] for m in range(N_BRANCH)], axis=1).astype(CDT)
        bg = b_merge_gate[i].reshape(1, N_BRANCH * d).astype(F32)
        wb = jnp.stack([w_branch[i, 0], w_branch[i, 1], w_branch[i, 2][GQA_COL_PERM],
                        w_branch[i, 3][GQA_COL_PERM]]).astype(CDT)
        x1 = _merge(x.reshape(t, d), branch_inputs, w_nsa_gate, wg, bg, wb, w_out[i].astype(CDT),
                    row(ln_mix_g[i]), row(ln_mix_b[i]), alpha)
        p2d = p[i].reshape(t, -1)
        ple_args = (ple_w[i].astype(CDT), ple_gate_w[i].astype(CDT), row(ple_gate_b[i]),
                    row(ln_ffn_g[i]), row(ln_ffn_b[i]))
        j = i // 2
        if i % 2 == 0:
            x2 = _ffn(x1, p2d, ffn_w_gate[j], ffn_w_up[j], ffn_w_down[j], *ple_args, alpha)
        else:
            seg = min(MOE_SEGMENT, t)
            routing = _router(x1, w_router[j], b_router[j], seg)
            x2 = _moe(x1, routing, p2d, moe_w_gate[j], moe_w_up[j], moe_w_down[j], *ple_args, alpha, seg)
        x = x2.reshape(b, s, d)
    return x
```

```python
import functools

import numpy as np
import jax
import jax.numpy as jnp
from jax import lax
from jax.experimental import pallas as pl
from jax.experimental.pallas import tpu as pltpu

HEAD_DIM = 64
DIL_PATTERNS = ((128, 1), (512, 4), (2048, 16))
N_DIL = 3
DIL_HEADS = 6
CONV_WIDTH = 384
NSA_Q_HEADS = 6
NSA_KV_HEADS = 2
CMP_BLOCK = 32
CMP_STRIDE = 16
CMP_HIDDEN = 128
SEL_BLOCK = 64
N_SEL = 16
NSA_WINDOW = 512
SWA_Q_HEADS = 6
SWA_WINDOW = 128
BRANCH_WIDTH = 384
N_BRANCH = 4
LN_EPS = 1e-5
NEG_INF = -1e30
DIL_WIDTH = N_DIL * DIL_HEADS * HEAD_DIM
COLUMN_SIZES = (DIL_WIDTH, DIL_WIDTH, DIL_WIDTH, CONV_WIDTH, CONV_WIDTH, CONV_WIDTH,
                NSA_Q_HEADS * HEAD_DIM, 128, 128, 128, 128, 128, 128, 3 * NSA_Q_HEADS,
                SWA_Q_HEADS * HEAD_DIM, 128, 128)
COL_OFF = np.concatenate([[0], np.cumsum(COLUMN_SIZES)]).tolist()

LANES = 128
V7X_VMEM_BYTES = 64 * 1024 * 1024

CDT = jnp.bfloat16
F32 = jnp.float32
QK_SCALE = HEAD_DIM ** -0.5
SUB_Q = 128
BAND_ROWS = 64
BIG = 1e30

_GQA_HEAD_ORDER = (0, 3, 1, 4, 2, 5)
GQA_COL_PERM = np.concatenate([np.arange(h * HEAD_DIM, (h + 1) * HEAD_DIM) for h in _GQA_HEAD_ORDER])
GQA_COL_HEAD = GQA_COL_PERM // HEAD_DIM


def _cparams(sem, vmem_sixteenths):
    return pltpu.CompilerParams(dimension_semantics=sem, vmem_limit_bytes=V7X_VMEM_BYTES * vmem_sixteenths // 16)


def _nt_dot(a, b):
    return lax.dot_general(a, b, (((1,), (1,)), ((), ())), preferred_element_type=F32)


def _dot(a, b):
    return jnp.dot(a, b, preferred_element_type=F32)


def _layer_norm(r, g, b):
    mu = jnp.mean(r, axis=-1, keepdims=True)
    d = r - mu
    var = jnp.mean(d * d, axis=-1, keepdims=True)
    return d * lax.rsqrt(var + LN_EPS) * g + b


def _half_masks():
    lane = lax.broadcasted_iota(jnp.int32, (1, LANES), 1)
    return lane < HEAD_DIM


def _linear_kernel(x_ref, w_ref, *refs, splits, n_chunk, dils):
    o_refs, z_ref = refs[:len(splits)], refs[len(splits)]
    xb = x_ref[...].astype(CDT)
    tm = xb.shape[0]
    for o_ref, (c0, c1), dil in zip(o_refs, splits, dils):
        width = c1 - c0
        if dil == 1:
            for a in range(c0, c1, n_chunk):
                b = min(a + n_chunk, c1)
                o_ref[:, a - c0:b - c0] = _dot(xb, w_ref[:, a:b]).astype(o_ref.dtype)
        else:
            z = _dot(xb, w_ref[:, c0:c1])
            for cb in range(width // LANES):
                z_ref[cb] = z[:, cb * LANES:(cb + 1) * LANES]
            for r in range(dil):
                for cb in range(width // LANES):
                    o_ref[:, r * width + cb * LANES:r * width + (cb + 1) * LANES] = (
                        z_ref[cb, pl.ds(r, tm // dil, stride=dil), :].astype(o_ref.dtype))


def _linear(x2d, w, splits, n_chunk, name, dils=None, tm=512):
    t, k = x2d.shape
    tm = min(tm, t)
    n = w.shape[1]
    dils = tuple(dils) if dils is not None else (1,) * len(splits)
    widths = [c1 - c0 for c0, c1 in splits]
    assert all(tm % (16 * dl) == 0 for dl in dils)
    return pl.pallas_call(
        functools.partial(_linear_kernel, splits=tuple(splits), n_chunk=n_chunk, dils=dils),
        grid=(t // tm,),
        in_specs=[pl.BlockSpec((tm, k), lambda i: (i, 0)),
                  pl.BlockSpec((k, n), lambda i: (0, 0))],
        out_specs=[pl.BlockSpec((tm // dl, dl * wd), lambda i: (i, 0)) for wd, dl in zip(widths, dils)],
        out_shape=[jax.ShapeDtypeStruct((t // dl, dl * wd), CDT) for wd, dl in zip(widths, dils)],
        scratch_shapes=[pltpu.VMEM((max(widths) // LANES, tm, LANES), F32)],
        compiler_params=_cparams(("parallel",), 12),
        name=name,
    )(x2d, w)


def _linear_t_kernel(x_ref, wt_ref, o_ref):
    o_ref[...] = _nt_dot(wt_ref[...], x_ref[...].astype(CDT)).astype(o_ref.dtype)


def _linear_t(x2d, wt, tm, name):
    t, k = x2d.shape
    n = wt.shape[0]
    return pl.pallas_call(
        _linear_t_kernel,
        grid=(t // tm,),
        in_specs=[pl.BlockSpec((tm, k), lambda i: (i, 0)), pl.BlockSpec((n, k), lambda i: (0, 0))],
        out_specs=pl.BlockSpec((None, n, tm), lambda i: (i, 0, 0)),
        out_shape=jax.ShapeDtypeStruct((t // tm, n, tm), CDT),
        compiler_params=_cparams(("parallel",), 8),
        name=name,
    )(x2d, wt)


def _conv_kernel(x_ref, xh_ref, wc_ref, cw_ref, ob_ref, *, tm):
    i = pl.program_id(1)
    w = CONV_WIDTH
    xb = x_ref[...].astype(CDT)
    z = _dot(xb, wc_ref[...])
    u = z[:, w:2 * w] * z[:, 2 * w:3 * w]
    zh = _dot(xh_ref[...].astype(CDT), wc_ref[:, w:3 * w])
    uh = zh[:, :w] * zh[:, w:]
    uh = jnp.where(i == 0, 0.0, uh)
    row = lax.broadcasted_iota(jnp.int32, (tm, w), 0)
    u1 = jnp.where(row == 0, uh[7:8, :], pltpu.roll(u, 1, 0))
    u2 = jnp.where(row == 0, uh[6:7, :], jnp.where(row == 1, uh[7:8, :], pltpu.roll(u, 2, 0)))
    y = cw_ref[0:1, :] * u2 + cw_ref[1:2, :] * u1 + cw_ref[2:3, :] * u
    ob_ref[...] = (z[:, :w] * y).astype(ob_ref.dtype)


def _conv(x, wc, conv_w, tm=512):
    b, s, d = x.shape
    tm = min(tm, s)
    hb = tm // 8
    return pl.pallas_call(
        functools.partial(_conv_kernel, tm=tm),
        grid=(b, s // tm),
        in_specs=[pl.BlockSpec((None, tm, d), lambda bi, i: (bi, i, 0)),
                  pl.BlockSpec((None, 8, d), lambda bi, i: (bi, jnp.maximum(i * hb - 1, 0), 0)),
                  pl.BlockSpec(wc.shape, lambda bi, i: (0, 0)),
                  pl.BlockSpec(conv_w.shape, lambda bi, i: (0, 0))],
        out_specs=pl.BlockSpec((None, tm, CONV_WIDTH), lambda bi, i: (bi, i, 0)),
        out_shape=jax.ShapeDtypeStruct((b, s, CONV_WIDTH), CDT),
        compiler_params=_cparams(("parallel", "parallel"), 12),
        name="short_conv",
    )(x, x, wc, conv_w)


def _banded_kernel(*refs, window, pr, tq, kw, want_lse, has_sink):
    q_ref, kp_ref, kc_ref, vp_ref, vc_ref = refs[:5]
    n = 5
    sink_ref = None
    if has_sink:
        sink_ref = refs[n]
        n += 1
    o_ref = refs[n]
    n += 1
    lse_ref = None
    if want_lse:
        lse_ref = refs[n]
        n += 1
    kbuf, vbuf, s_ref, e_ref, m_ref, l_ref = refs[n:n + 6]

    i = pl.program_id(2)
    kbuf[0:pr, :] = kp_ref[...]
    kbuf[pr:pr + tq, :] = kc_ref[...]
    vbuf[0:pr, :] = vp_ref[...]
    vbuf[pr:pr + tq, :] = vc_ref[...]

    span = SUB_Q + pr
    qi = lax.broadcasted_iota(jnp.int32, (SUB_Q, span), 0)
    kj = lax.broadcasted_iota(jnp.int32, (SUB_Q, span), 1)
    dist = pr + qi - kj
    band = (dist >= 0) & (dist <= window)
    lo = _half_masks()
    halves = (lo, jnp.logical_not(lo))
    groups = ((0,), (1,), (2,)) if kw == 3 * LANES else ((0, 1, 2),)
    rb = BAND_ROWS
    for sb in range(tq // SUB_Q):
        r0 = sb * SUB_Q
        bias = jnp.where(band & (i * tq + r0 - pr + kj >= 0), 0.0, NEG_INF)
        for grp in groups:
            kc0 = grp[0] * LANES if kw == 3 * LANES else 0
            qs = jnp.concatenate(
                [(jnp.where(hm, q_ref[r0:r0 + SUB_Q, p * LANES:(p + 1) * LANES], 0) * QK_SCALE).astype(CDT)
                 for p in grp for hm in halves], axis=0)
            g0 = 2 * grp[0] * SUB_Q
            s_ref[g0:g0 + qs.shape[0], :] = _nt_dot(qs, kbuf[r0:r0 + span, kc0:kc0 + LANES])
        for c0 in range(0, 6 * SUB_Q, rb):
            rows = slice(c0, c0 + rb)
            s = s_ref[rows, :] + bias[c0 % SUB_Q:c0 % SUB_Q + rb, :]
            m = jnp.max(s, axis=-1, keepdims=True)
            e = jnp.exp(s - m)
            e_ref[rows, :] = e.astype(CDT)
            m_ref[rows, :] = jnp.broadcast_to(m, (rb, LANES))
            l_ref[rows, :] = jnp.broadcast_to(jnp.sum(e, axis=-1, keepdims=True), (rb, LANES))
        for grp in groups:
            kc0 = grp[0] * LANES if kw == 3 * LANES else 0
            g0 = 2 * grp[0] * SUB_Q
            g1 = g0 + 2 * len(grp) * SUB_Q
            l = l_ref[g0:g1, :]
            o = _dot(e_ref[g0:g1, :], vbuf[r0:r0 + span, kc0:kc0 + LANES]) / l
            lse = m_ref[g0:g1, :] + jnp.log(l)
            for n_p, p in enumerate(grp):
                a = 2 * n_p * SUB_Q
                o_pair = jnp.where(lo, o[a:a + SUB_Q], o[a + SUB_Q:a + 2 * SUB_Q])
                lse_pair = jnp.where(lo, lse[a:a + SUB_Q], lse[a + SUB_Q:a + 2 * SUB_Q])
                if has_sink:
                    o_pair = o_pair * jax.nn.sigmoid(lse_pair - sink_ref[:, p * LANES:(p + 1) * LANES])
                o_ref[r0:r0 + SUB_Q, p * LANES:(p + 1) * LANES] = o_pair.astype(o_ref.dtype)
                if want_lse:
                    lse_ref[r0:r0 + SUB_Q, p * LANES:(p + 1) * LANES] = lse_pair


def _banded(qa, ka, va, *, nrep, qcol, kcol, vcol, kw, window, want_lse, sink_row=None, tq=512):
    b, l, _ = qa.shape
    pr = -(-window // SUB_Q) * SUB_Q
    tq = min(max(tq, pr), l)
    assert tq % pr == 0 and l % tq == 0, (tq, pr, l)
    ratio = tq // pr
    qw = 3 * LANES
    in_specs = [
        pl.BlockSpec((None, tq, qw), lambda bi, r, i: (bi, i, qcol(r))),
        pl.BlockSpec((None, pr, kw), lambda bi, r, i: (bi, jnp.maximum(i * ratio - 1, 0), kcol(r))),
        pl.BlockSpec((None, tq, kw), lambda bi, r, i: (bi, i, kcol(r))),
        pl.BlockSpec((None, pr, kw), lambda bi, r, i: (bi, jnp.maximum(i * ratio - 1, 0), vcol(r))),
        pl.BlockSpec((None, tq, kw), lambda bi, r, i: (bi, i, vcol(r))),
    ]
    args = [qa, ka, ka, va, va]
    if sink_row is not None:
        in_specs.append(pl.BlockSpec(sink_row.shape, lambda bi, r, i: (0, 0)))
        args.append(sink_row)
    out_specs = [pl.BlockSpec((None, tq, qw), lambda bi, r, i: (bi, i, r))]
    out_shape = [jax.ShapeDtypeStruct((b, l, nrep * qw), CDT)]
    if want_lse:
        out_specs.append(pl.BlockSpec((None, tq, qw), lambda bi, r, i: (bi, i, r)))
        out_shape.append(jax.ShapeDtypeStruct((b, l, nrep * qw), F32))
    res = pl.pallas_call(
        functools.partial(_banded_kernel, window=window, pr=pr, tq=tq, kw=kw, want_lse=want_lse,
                          has_sink=sink_row is not None),
        grid=(b, nrep, l // tq),
        in_specs=in_specs,
        out_specs=out_specs,
        out_shape=out_shape,
        scratch_shapes=[pltpu.VMEM((pr + tq, kw), ka.dtype), pltpu.VMEM((pr + tq, kw), va.dtype),
                        pltpu.VMEM((6 * SUB_Q, SUB_Q + pr), F32), pltpu.VMEM((6 * SUB_Q, SUB_Q + pr), CDT),
                        pltpu.VMEM((6 * SUB_Q, LANES), F32), pltpu.VMEM((6 * SUB_Q, LANES), F32)],
        compiler_params=_cparams(("parallel", "parallel", "parallel"), 8),
        name=f"banded_w{window}_k{kw}_r{nrep}",
    )(*args)
    return res


def _gelu_tanh(x):
    return 0.5 * x * (1.0 + jnp.tanh(0.7978845608028654 * (x + 0.044715 * (x * x * x))))


def _compress_kernel(x_ref, pa_ref, pb_ref, w1a_ref, w1b_ref, b1_ref, w2_ref, b2_ref, o_ref):
    x = x_ref[...].astype(F32)
    n = x.shape[0]
    a = _dot((x + pa_ref[...]).astype(CDT), w1a_ref[...])
    bm = _dot((x + pb_ref[...]).astype(CDT), w1b_ref[...])
    h = a + pltpu.roll(bm, n - 1, 0) + b1_ref[...]
    o_ref[...] = (_dot(_gelu_tanh(h).astype(CDT), w2_ref[...]) + b2_ref[...]).astype(o_ref.dtype)


def _compress(t, pos, w1, b1, w2, b2):
    b, s, _ = t.shape
    nch = s // CMP_STRIDE
    xw = CMP_STRIDE * LANES
    x = t.reshape(b, nch, xw)
    eye = jnp.eye(NSA_KV_HEADS, dtype=F32)
    w1r = w1.reshape(CMP_BLOCK, HEAD_DIM, CMP_HIDDEN)

    def expand_w1(part):
        return jnp.einsum('tdj,kl->tkdlj', part, eye).reshape(xw, NSA_KV_HEADS * CMP_HIDDEN).astype(CDT)

    def expand_pos(part):
        return jnp.broadcast_to(part[:, None, :], (CMP_STRIDE, NSA_KV_HEADS, HEAD_DIM)).reshape(1, xw)

    w1a, w1b = expand_w1(w1r[:CMP_STRIDE]), expand_w1(w1r[CMP_STRIDE:])
    pa, pb = expand_pos(pos[:CMP_STRIDE]), expand_pos(pos[CMP_STRIDE:])
    b1e = jnp.tile(b1, NSA_KV_HEADS).reshape(1, -1)
    w2e = jnp.einsum('jd,kl->kjld', w2, eye).reshape(NSA_KV_HEADS * CMP_HIDDEN, LANES).astype(CDT)
    b2e = jnp.tile(b2, NSA_KV_HEADS).reshape(1, -1)
    consts = [pa, pb, w1a, w1b, b1e, w2e, b2e]
    return pl.pallas_call(
        _compress_kernel,
        grid=(b,),
        in_specs=[pl.BlockSpec((None, nch, xw), lambda bi: (bi, 0, 0))]
        + [pl.BlockSpec(c.shape, lambda bi: (0, 0)) for c in consts],
        out_specs=pl.BlockSpec((None, nch, LANES), lambda bi: (bi, 0, 0)),
        out_shape=jax.ShapeDtypeStruct((b, nch, LANES), CDT),
        compiler_params=_cparams(("parallel",), 12),
        name="nsa_compress",
    )(x, *consts)


CMP_ROWS = 16


def _cmp_select_kernel(q_ref, kc_ref, vc_ref, ov_ref, o_ref, sel_ref, s_ref, p_ref, hi_ref, lo_ref,
                       *, tq, n_sel, tile0):
    i = pl.program_id(1) + tile0
    ncp = kc_ref.shape[0]
    ns = ov_ref.shape[1]
    lo = _half_masks()
    t_col = i * tq + lax.broadcasted_iota(jnp.int32, (tq, 1), 0)
    blk = lax.broadcasted_iota(jnp.int32, (tq, ns), 1)
    blk_t = lax.broadcasted_iota(jnp.int32, (ns, tq), 0)
    cur = t_col // SEL_BLOCK
    causal = blk <= cur
    forced = (blk == 0) | (blk == cur) | (blk == cur - 1)
    rb = CMP_ROWS
    c_end = lax.broadcasted_iota(jnp.int32, (rb, ncp), 1) * CMP_STRIDE + (CMP_BLOCK - 1)
    o_kv, work_t = [], []
    for kv, hm in enumerate((lo, jnp.logical_not(lo))):
        qs = jnp.concatenate(
            [(jnp.where(hm, q_ref[:, p * LANES:(p + 1) * LANES], 0) * QK_SCALE).astype(CDT) for p in range(3)], axis=0)
        s_ref[...] = _nt_dot(qs, kc_ref[...])
        for r0 in range(0, tq, rb):
            t_rows = i * tq + r0 + lax.broadcasted_iota(jnp.int32, (rb, 1), 0)
            vis_bias = jnp.where(c_end <= t_rows, 0.0, NEG_INF)
            has_visible = t_rows >= CMP_BLOCK - 1
            psum = jnp.zeros((rb, ncp), F32)
            for h in range(3):
                rows = slice(h * tq + r0, h * tq + r0 + rb)
                s = s_ref[rows, :] + vis_bias
                e = jnp.exp(s - jnp.max(s, axis=-1, keepdims=True))
                inv = jnp.where(has_visible, 1.0 / jnp.maximum(jnp.sum(e, axis=-1, keepdims=True), 1e-30), 0.0)
                pn = e * inv
                p_ref[rows, :] = pn.astype(CDT)
                psum = psum + pn
            p_hi = psum.astype(CDT)
            hi_ref[r0:r0 + rb, :] = p_hi
            lo_ref[r0:r0 + rb, :] = (psum - p_hi.astype(F32)).astype(CDT)
        o = _dot(p_ref[...], vc_ref[...])
        o_kv.append([o[p * tq:(p + 1) * tq] for p in range(3)])
        imp = _dot(hi_ref[...], ov_ref[...]) + _dot(lo_ref[...], ov_ref[...])
        work_t.append(jnp.where(causal & jnp.logical_not(forced), imp, -BIG).T)

    blk_lanes = blk_t[:, 0:LANES]

    def pick(_, work):
        m = jnp.max(work, axis=0, keepdims=True)
        idx = jnp.min(jnp.where(work == m, blk_lanes, ns), axis=0, keepdims=True)
        return jnp.where(blk_lanes == idx, -2.0 * BIG, work)

    for kv, start in enumerate(work_t):
        done = jnp.concatenate([lax.fori_loop(0, n_sel - 3, pick, start[:, c0:c0 + LANES], unroll=True)
                                for c0 in range(0, tq, LANES)], axis=1)
        taken = jnp.where((done < -BIG) & (start > -BIG), 1.0, 0.0).T
        selb = jnp.where(forced | (taken > 0.5), 0.0, NEG_INF)
        sel_ref[:, kv * ns:(kv + 1) * ns] = selb.astype(sel_ref.dtype)
    for p in range(3):
        o_ref[:, p * LANES:(p + 1) * LANES] = jnp.where(lo, o_kv[0][p], o_kv[1][p]).astype(o_ref.dtype)


CMP_CAUSAL_SPLITS = 4


def _cmp_select(zr, qcol, kc, vc, tq=256):
    b, s, _ = zr.shape
    ncp = kc.shape[1]
    ns = s // SEL_BLOCK
    n_sel = min(N_SEL, ns)
    assert n_sel >= 3, "selection needs room for the three forced blocks"
    tq = min(tq, s)
    c = np.arange(ncp)[:, None] * CMP_STRIDE
    j = np.arange(ns)[None, :] * SEL_BLOCK
    overlap = ((c < j + SEL_BLOCK) & (c + CMP_BLOCK - 1 >= j)).astype(np.float32)
    overlap[ncp - 1:, :] = 0.0
    ov = jnp.asarray(overlap, CDT)
    qw = 3 * LANES
    n_split = CMP_CAUSAL_SPLITS if (s // tq) % CMP_CAUSAL_SPLITS == 0 and ncp % (16 * CMP_CAUSAL_SPLITS) == 0 else 1
    tiles = s // tq // n_split
    outs, sels = [], []
    for part in range(n_split):
        tile0 = part * tiles
        ncp_part = ncp * (part + 1) // n_split
        o_part, sel_part = pl.pallas_call(
            functools.partial(_cmp_select_kernel, tq=tq, n_sel=n_sel, tile0=tile0),
            grid=(b, tiles),
            in_specs=[pl.BlockSpec((None, tq, qw), lambda bi, i, tile0=tile0: (bi, i + tile0, qcol)),
                      pl.BlockSpec((None, ncp_part, LANES), lambda bi, i: (bi, 0, 0)),
                      pl.BlockSpec((None, ncp_part, LANES), lambda bi, i: (bi, 0, 0)),
                      pl.BlockSpec((ncp_part, ns), lambda bi, i: (0, 0))],
            out_specs=[pl.BlockSpec((None, tq, qw), lambda bi, i: (bi, i, 0)),
                       pl.BlockSpec((None, tq, 2 * ns), lambda bi, i: (bi, i, 0))],
            out_shape=[jax.ShapeDtypeStruct((b, tiles * tq, qw), CDT),
                       jax.ShapeDtypeStruct((b, tiles * tq, 2 * ns), CDT)],
            scratch_shapes=[pltpu.VMEM((3 * tq, ncp_part), F32), pltpu.VMEM((3 * tq, ncp_part), CDT),
                            pltpu.VMEM((tq, ncp_part), CDT), pltpu.VMEM((tq, ncp_part), CDT)],
            compiler_params=_cparams(("parallel", "parallel"), 12),
            name=f"nsa_cmp_select_p{part}",
        )(zr, kc, vc, ov)
        outs.append(o_part)
        sels.append(sel_part)
    return jnp.concatenate(outs, axis=1), jnp.concatenate(sels, axis=1)


SLC_KEY_CHUNK = 256


def _slc_kernel(q_ref, k0_ref, k1_ref, vt_ref, selb_ref, o_ref, s_ref, e_ref, m_ref, l_ref, a_ref, acc_ref,
                *, tq, tk):
    i = pl.program_id(1)
    ns = selb_ref.shape[1] // 2
    bpt = tk // SEL_BLOCK
    n_q = 3 * tq
    rc = min(SLC_KEY_CHUNK, tk)
    lo = _half_masks()
    n_tiles = ((i + 1) * tq + tk - 1) // tk
    p_row = lax.broadcasted_iota(jnp.int32, (ns, LANES), 0)
    p_col = lax.broadcasted_iota(jnp.int32, (ns, LANES), 1)
    groups = []
    for kv, (hm, k_ref) in enumerate(((lo, k0_ref), (jnp.logical_not(lo), k1_ref))):
        q3 = [(jnp.where(hm, q_ref[:, p * LANES:(p + 1) * LANES], 0) * QK_SCALE).astype(CDT) for p in range(3)]
        selb = selb_ref[:, kv * ns:(kv + 1) * ns]
        lane0 = HEAD_DIM if kv == 0 else 0
        groups.append((kv, hm, k_ref, q3, selb, lane0))
    m_ref[...] = jnp.full(m_ref.shape, NEG_INF, F32)
    l_ref[...] = jnp.zeros(l_ref.shape, F32)
    acc_ref[...] = jnp.zeros(acc_ref.shape, F32)

    def scores(j):
        k0 = pl.multiple_of(j * tk, tk)
        for kv, hm, k_ref, q3, selb, lane0 in groups:
            place = ((p_col >= lane0) & (p_col < lane0 + bpt) & (p_row == p_col - lane0 + j * bpt)).astype(CDT)
            sb = _dot(selb, place).astype(CDT)
            qp = jnp.concatenate([jnp.where(hm, q, sb) for q in q3], axis=0)
            s_ref[kv] = _nt_dot(k_ref[pl.ds(k0, tk), :], qp)

    def softmax_pv(j, diagonal):
        k0 = pl.multiple_of(j * tk, tk)
        for kv in range(2):
            for c0 in range(0, n_q, LANES):
                cols = slice(c0, c0 + LANES)
                t_lane = i * tq + (c0 % tq) + lax.broadcasted_iota(jnp.int32, (1, LANES), 1)

                def chunk(r0):
                    s = s_ref[kv, r0:r0 + rc, cols]
                    if diagonal:
                        kpos = k0 + r0 + lax.broadcasted_iota(jnp.int32, (rc, LANES), 0)
                        s = jnp.where(kpos <= t_lane, s, NEG_INF)
                    return s

                m8 = m_ref[kv, :, cols]
                for r0 in range(0, tk, rc):
                    m8 = jnp.maximum(m8, jnp.max(chunk(r0).reshape(rc // 8, 8, LANES), axis=0))
                m_new = jnp.max(m8, axis=0, keepdims=True)
                alpha = jnp.exp(m_ref[kv, :, cols] - m_new)
                l8 = jnp.zeros((8, LANES), F32)
                for r0 in range(0, tk, rc):
                    e = jnp.exp(chunk(r0) - m_new)
                    l8 = l8 + jnp.sum(e.reshape(rc // 8, 8, LANES), axis=0)
                    e_ref[kv, r0:r0 + rc, cols] = e.astype(CDT)
                l_ref[kv, :, cols] = alpha * l_ref[kv, :, cols] + jnp.sum(l8, axis=0, keepdims=True)
                m_ref[kv, :, cols] = jnp.broadcast_to(m_new, (8, LANES))
                a_ref[kv, :, cols] = alpha
        for kv in range(2):
            acc_ref[kv] = a_ref[kv, 0:1, :] * acc_ref[kv] + _dot(vt_ref[j], e_ref[kv])

    def tile(j, diagonal):
        scores(j)
        softmax_pv(j, diagonal)

    lax.fori_loop(0, n_tiles - 1, lambda j, c: (tile(j, False), c)[1], 0)
    tile(n_tiles - 1, True)
    o0 = (acc_ref[0] / l_ref[0, 0:1, :]).T
    o1 = (acc_ref[1] / l_ref[1, 0:1, :]).T
    for p in range(3):
        o_ref[:, p * LANES:(p + 1) * LANES] = jnp.where(
            lo, o0[p * tq:(p + 1) * tq], o1[p * tq:(p + 1) * tq]).astype(o_ref.dtype)


def _slc(zr, qcol, ksc, vt, selb, tq=512):
    b, s, _ = zr.shape
    tq = min(tq, s)
    tk = vt.shape[2]
    assert tq % LANES == 0 and tk // SEL_BLOCK <= HEAD_DIM and s % tk == 0
    qw = 3 * LANES
    ns2 = selb.shape[2]
    pat = jax.nn.one_hot((jnp.arange(s) // SEL_BLOCK) % (tk // SEL_BLOCK), HEAD_DIM, dtype=ksc.dtype)
    pat = jnp.broadcast_to(pat[None], (b, s, HEAD_DIM))
    k0 = jnp.concatenate([ksc[..., :HEAD_DIM], pat], axis=-1)
    k1 = jnp.concatenate([pat, ksc[..., HEAD_DIM:]], axis=-1)
    full = lambda bi, i: (bi, 0, 0)
    return pl.pallas_call(
        functools.partial(_slc_kernel, tq=tq, tk=tk),
        grid=(b, s // tq),
        in_specs=[pl.BlockSpec((None, tq, qw), lambda bi, i: (bi, i, qcol)),
                  _single_buffered((None, s, LANES), full),
                  _single_buffered((None, s, LANES), full),
                  _single_buffered((s // tk, LANES, tk), lambda bi, i: (bi, 0, 0)),
                  pl.BlockSpec((None, tq, ns2), lambda bi, i: (bi, i, 0))],
        out_specs=pl.BlockSpec((None, tq, qw), lambda bi, i: (bi, i, 0)),
        out_shape=jax.ShapeDtypeStruct((b, s, qw), CDT),
        scratch_shapes=[pltpu.VMEM((2, tk, 3 * tq), F32), pltpu.VMEM((2, tk, 3 * tq), CDT),
                        pltpu.VMEM((2, 8, 3 * tq), F32), pltpu.VMEM((2, 8, 3 * tq), F32),
                        pltpu.VMEM((2, 8, 3 * tq), F32), pltpu.VMEM((2, LANES, 3 * tq), F32)],
        compiler_params=_cparams(("parallel", "arbitrary"), 14),
        name="nsa_slc",
    )(zr, k0, k1, vt, selb)


def _merge_kernel(x_ref, oa0, oa1, oa2, la0, la1, la2, ob, ocmp, oslc, owin, od,
                  wn_ref, wg_ref, bg_ref, wb_ref, wo_ref, g_ref, b_ref, o_ref, *scratch, alpha):
    x = x_ref[...]
    xb = x.astype(CDT)
    bw = BRANCH_WIDTH

    def token_rows(src_ref, dst_ref):
        dil = src_ref.shape[1] // bw
        if dil == 1:
            return src_ref[...].astype(F32)
        n_cb = bw // LANES
        for r in range(dil):
            for cb in range(n_cb):
                c0 = r * bw + cb * LANES
                dst_ref[cb, pl.ds(r, src_ref.shape[0], stride=dil), :] = src_ref[:, c0:c0 + LANES].astype(F32)
        return jnp.concatenate([dst_ref[cb] for cb in range(n_cb)], axis=1)

    o0, o1, o2 = (token_rows(s, d) for s, d in zip((oa0, oa1, oa2), scratch[0:3]))
    l0, l1, l2 = (token_rows(s, d) for s, d in zip((la0, la1, la2), scratch[3:6]))
    mx = jnp.maximum(jnp.maximum(l0, l1), l2)
    w0, w1, w2 = jnp.exp(l0 - mx), jnp.exp(l1 - mx), jnp.exp(l2 - mx)
    o_a = (w0 * o0 + w1 * o1 + w2 * o2) / (w0 + w1 + w2)
    gates = jax.nn.sigmoid(_dot(xb, wn_ref[...]))
    o_c = (gates[:, 0:bw] * ocmp[...].astype(F32) + gates[:, bw:2 * bw] * oslc[...].astype(F32)
           + gates[:, 2 * bw:3 * bw] * owin[...].astype(F32))
    branches = (o_a.astype(CDT), ob[...], o_c.astype(CDT), od[...])
    d = x.shape[1]
    merged = jnp.zeros(x.shape, F32)
    for m in range(N_BRANCH):
        gate = jax.nn.sigmoid(_dot(xb, wg_ref[:, m * d:(m + 1) * d]) + bg_ref[:, m * d:(m + 1) * d])
        merged = merged + gate * _dot(branches[m], wb_ref[m])
    r = alpha * x + _dot(merged.astype(CDT), wo_ref[...])
    o_ref[...] = _layer_norm(r, g_ref[...], b_ref[...])


def _merge(x2d, branch_inputs, wn, wg, bg, wb, wo, g, b, alpha, tm=256):
    t, d = x2d.shape
    tm = min(tm, t)
    row = lambda i: (i, 0)
    const2 = lambda i: (0, 0)
    in_specs = [pl.BlockSpec((tm, d), row)]
    in_specs += [pl.BlockSpec((tm * a.shape[0] // t, a.shape[1]), row) for a in branch_inputs]
    in_specs += [pl.BlockSpec(wn.shape, const2), pl.BlockSpec(wg.shape, const2), pl.BlockSpec(bg.shape, const2),
                 pl.BlockSpec(wb.shape, lambda i: (0, 0, 0)), pl.BlockSpec(wo.shape, const2),
                 pl.BlockSpec(g.shape, const2), pl.BlockSpec(b.shape, const2)]
    return pl.pallas_call(
        functools.partial(_merge_kernel, alpha=alpha),
        grid=(t // tm,),
        in_specs=in_specs,
        out_specs=pl.BlockSpec((tm, d), row),
        out_shape=jax.ShapeDtypeStruct((t, d), F32),
        scratch_shapes=[pltpu.VMEM((BRANCH_WIDTH // LANES, tm, LANES), F32) for _ in range(2 * N_DIL)],
        compiler_params=_cparams(("parallel",), 14),
        name="merge_ln",
    )(x2d, *branch_inputs, wn, wg, bg, wb, wo, g, b)


def _ple_ln(x, xb, f, p, plw_ref, pgw_ref, pgb_ref, g_ref, b_ref, alpha):
    ple = jax.nn.sigmoid(_dot(xb, pgw_ref[...]) + pgb_ref[...]) * _dot(p.astype(CDT), plw_ref[...])
    return _layer_norm(alpha * x + f + ple, g_ref[...], b_ref[...])


FFN_CHUNK = 512


def _ffn_kernel(x_ref, p_ref, wg_ref, wu_ref, wd_ref, plw_ref, pgw_ref, pgb_ref, g_ref, b_ref, o_ref, h_ref, *, alpha):
    x = x_ref[...]
    xb = x.astype(CDT)
    dff = wg_ref.shape[1]
    for c0 in range(0, dff, FFN_CHUNK):
        cols = slice(c0, min(c0 + FFN_CHUNK, dff))
        h_ref[:, cols] = (jax.nn.silu(_dot(xb, wg_ref[:, cols])) * _dot(xb, wu_ref[:, cols])).astype(CDT)
    f = _dot(h_ref[...], wd_ref[...])
    o_ref[...] = _ple_ln(x, xb, f, p_ref[...], plw_ref, pgw_ref, pgb_ref, g_ref, b_ref, alpha)


def _ffn(x2d, p2d, wg, wu, wd, plw, pgw, pgb, g, b, alpha, tm=512):
    t, d = x2d.shape
    tm = min(tm, t)
    dff = wg.shape[1]
    wg, wu, wd = wg.astype(CDT), wu.astype(CDT), wd.astype(CDT)
    row = lambda i: (i, 0)
    const = lambda shape: pl.BlockSpec(shape, lambda i: (0, 0), pipeline_mode=pl.Buffered(1))
    return pl.pallas_call(
        functools.partial(_ffn_kernel, alpha=alpha),
        grid=(t // tm,),
        in_specs=[pl.BlockSpec((tm, d), row), pl.BlockSpec((tm, p2d.shape[1]), row),
                  const(wg.shape), const(wu.shape), const(wd.shape),
                  const(plw.shape), const(pgw.shape), const(pgb.shape), const(g.shape), const(b.shape)],
        out_specs=pl.BlockSpec((tm, d), row),
        out_shape=jax.ShapeDtypeStruct((t, d), F32),
        scratch_shapes=[pltpu.VMEM((tm, dff), CDT)],
        compiler_params=_cparams(("parallel",), 12),
        name="ffn_ple_ln",
    )(x2d, p2d, wg, wu, wd, plw, pgw, pgb, g, b)


def _router_kernel(x_ref, wh_ref, wl_ref, b_ref, comb_ref, rank_ref, rank_t_ref, cnt_ref):
    x = x_ref[...]
    xh = x.astype(CDT)
    xl = (x - xh.astype(F32)).astype(CDT)
    logits = _dot(xh, wh_ref[...]) + _dot(xh, wl_ref[...]) + _dot(xl, wh_ref[...]) + b_ref[...]
    lane = lax.broadcasted_iota(jnp.int32, logits.shape, 1)
    v1 = jnp.max(logits, axis=-1, keepdims=True)
    i1 = jnp.min(jnp.where(logits == v1, lane, LANES), axis=-1, keepdims=True)
    rest = jnp.where(lane == i1, -jnp.inf, logits)
    v2 = jnp.max(rest, axis=-1, keepdims=True)
    i2 = jnp.min(jnp.where(rest == v2, lane, LANES), axis=-1, keepdims=True)
    e2 = jnp.exp(v2 - v1)
    comb_ref[...] = jnp.where(lane == i1, 1.0 / (1.0 + e2), 0.0) + jnp.where(lane == i2, e2 / (1.0 + e2), 0.0)
    routed = (lane == i1) | (lane == i2)
    mask = routed.astype(CDT)
    tm = x.shape[0]
    before = (lax.broadcasted_iota(jnp.int32, (tm, tm), 1) < lax.broadcasted_iota(jnp.int32, (tm, tm), 0)).astype(CDT)
    rank = jnp.where(routed, _dot(before, mask), -1.0)
    rank_ref[...] = rank
    rank_t_ref[...] = rank.T[0:rank_t_ref.shape[0], :]
    cnt_ref[...] = jnp.sum(routed.astype(F32), axis=0, keepdims=True).astype(jnp.int32)


def _router(x2d, w_router, b_router, tm):
    t, d = x2d.shape
    ne = w_router.shape[1]
    wp = jnp.zeros((d, LANES), F32).at[:, :ne].set(w_router)
    wh = wp.astype(CDT)
    wl = (wp - wh.astype(F32)).astype(CDT)
    bp = jnp.full((1, LANES), -BIG, F32).at[0, :ne].set(b_router)
    nt = t // tm
    row = lambda i: (i, 0)
    return pl.pallas_call(
        _router_kernel,
        grid=(nt,),
        in_specs=[pl.BlockSpec((tm, d), row), pl.BlockSpec(wh.shape, lambda i: (0, 0)),
                  pl.BlockSpec(wl.shape, lambda i: (0, 0)), pl.BlockSpec(bp.shape, lambda i: (0, 0))],
        out_specs=[pl.BlockSpec((tm, LANES), row), pl.BlockSpec((tm, LANES), row),
                   pl.BlockSpec((None, 8, tm), lambda i: (i, 0, 0)),
                   pl.BlockSpec((None, 1, LANES), lambda i: (i, 0, 0))],
        out_shape=[jax.ShapeDtypeStruct((t, LANES), F32), jax.ShapeDtypeStruct((t, LANES), F32),
                   jax.ShapeDtypeStruct((nt, 8, tm), F32), jax.ShapeDtypeStruct((nt, 1, LANES), jnp.int32)],
        compiler_params=_cparams(("parallel",), 10),
        name="moe_router",
    )(x2d, wh, wl, bp)


def _moe_kernel(cnt_ref, x_ref, comb_ref, rank_ref, rank_t_ref, p_ref, wg_ref, wu_ref, wd_ref, plw_ref, pgw_ref,
                pgb_ref, g_ref, b_ref, o_ref, xe_ref, ye_ref, *, alpha, rs, seg):
    i = pl.program_id(0)
    e = pl.program_id(1)
    c = pl.program_id(2)
    n_seg = x_ref.shape[0] // seg
    last_chunk = c == pl.num_programs(2) - 1

    first = slice(0, rs)
    toks = [slice(sg * seg, (sg + 1) * seg) for sg in range(n_seg)]
    n_groups = [(cnt_ref[(i * n_seg + sg) * LANES + e] + rs - 1) // rs for sg in range(n_seg)]

    def later(sc):
        return pl.ds(pl.multiple_of(sc * rs, 8), rs), (sc * rs).astype(F32)

    @pl.when((e == 0) & (c == 0))
    def _():
        o_ref[...] = jnp.zeros_like(o_ref)

    @pl.when(c == 0)
    def _():
        row_id = lax.broadcasted_iota(jnp.int32, (rs, seg), 0).astype(F32)
        xbs = [x_ref[tok, :].astype(CDT) for tok in toks]
        rank_rows = [rank_t_ref[sg, pl.ds(e, 1), :] for sg in range(n_seg)]

        def gather(sg, rws, base):
            onehot = (rank_rows[sg] - base == row_id).astype(CDT)
            xe_ref[sg, rws, :] = _dot(onehot, xbs[sg]).astype(CDT)
            ye_ref[sg, rws, :] = jnp.zeros((rs, ye_ref.shape[2]), F32)

        for sg in range(n_seg):
            gather(sg, first, 0.0)
        for sg in range(n_seg):
            lax.fori_loop(1, n_groups[sg], lambda sc, _, sg=sg: (gather(sg, *later(sc)), 0)[1], 0)

    def expert(sg, rws):
        xs = xe_ref[sg, rws, :]
        h = jax.nn.silu(_dot(xs, wg_ref[0])) * _dot(xs, wu_ref[0])
        ye_ref[sg, rws, :] += _dot(h.astype(CDT), wd_ref[0])

    for sg in range(n_seg):
        expert(sg, first)
    for sg in range(n_seg):
        lax.fori_loop(1, n_groups[sg], lambda sc, _, sg=sg: (expert(sg, later(sc)[0]), 0)[1], 0)

    @pl.when(last_chunk)
    def _():
        lane = lax.broadcasted_iota(jnp.int32, (seg, LANES), 1)
        mine = lane == e
        col_id = lax.broadcasted_iota(jnp.int32, (seg, rs), 1).astype(F32)
        cws = [jnp.sum(jnp.where(mine, comb_ref[tok, :], 0.0), axis=-1, keepdims=True) for tok in toks]
        rank_cols = [jnp.sum(jnp.where(mine, rank_ref[tok, :], 0.0), axis=-1, keepdims=True) for tok in toks]

        def scatter(sg, rws, base):
            onehot = (rank_cols[sg] - base == col_id).astype(CDT)
            o_ref[toks[sg], :] += cws[sg] * _dot(onehot, ye_ref[sg, rws, :].astype(CDT))

        for sg in range(n_seg):
            scatter(sg, first, 0.0)
        for sg in range(n_seg):
            lax.fori_loop(1, n_groups[sg], lambda sc, _, sg=sg: (scatter(sg, *later(sc)), 0)[1], 0)

    @pl.when((e == pl.num_programs(1) - 1) & last_chunk)
    def _():
        for sg in range(n_seg):
            tok = slice(sg * seg, (sg + 1) * seg)
            x = x_ref[tok, :]
            o_ref[tok, :] = _ple_ln(x, x.astype(CDT), o_ref[tok, :], p_ref[tok, :], plw_ref, pgw_ref, pgb_ref,
                                    g_ref, b_ref, alpha)


MOE_CHUNK = 512
MOE_SEGMENT = 1024
MOE_SEGMENTS_PER_TILE = 2
MOE_ROW_GROUP = 288


def _single_buffered(shape, index_map):
    return pl.BlockSpec(shape, index_map, pipeline_mode=pl.Buffered(1))


def _moe(x2d, routing, p2d, wg, wu, wd, plw, pgw, pgb, g, b, alpha, seg):
    comb, rank, rank_t, cnt = routing
    t, d = x2d.shape
    ne, _, dff = wg.shape
    ck = min(MOE_CHUNK, dff)
    rs = min(MOE_ROW_GROUP, seg)
    n_seg = min(MOE_SEGMENTS_PER_TILE, t // seg)
    tm = n_seg * seg
    max_rows = -(-seg // rs) * rs
    wg, wu, wd = wg.astype(CDT), wu.astype(CDT), wd.astype(CDT)
    row = lambda i, e, c, cnt: (i, 0)
    c2 = lambda i, e, c, cnt: (0, 0)
    grid_spec = pltpu.PrefetchScalarGridSpec(
        num_scalar_prefetch=1,
        grid=(t // tm, ne, dff // ck),
        in_specs=[_single_buffered((tm, d), row), _single_buffered((tm, LANES), row),
                  _single_buffered((tm, LANES), row),
                  _single_buffered((n_seg, 8, seg), lambda i, e, c, cnt: (i, 0, 0)),
                  _single_buffered((tm, p2d.shape[1]), row),
                  pl.BlockSpec((1, d, ck), lambda i, e, c, cnt: (e, 0, c)),
                  pl.BlockSpec((1, d, ck), lambda i, e, c, cnt: (e, 0, c)),
                  pl.BlockSpec((1, ck, d), lambda i, e, c, cnt: (e, c, 0)),
                  _single_buffered(plw.shape, c2), _single_buffered(pgw.shape, c2), _single_buffered(pgb.shape, c2),
                  _single_buffered(g.shape, c2), _single_buffered(b.shape, c2)],
        out_specs=pl.BlockSpec((tm, d), row),
        scratch_shapes=[pltpu.VMEM((n_seg, max_rows, d), CDT), pltpu.VMEM((n_seg, max_rows, d), F32)],
    )
    return pl.pallas_call(
        functools.partial(_moe_kernel, alpha=alpha, rs=rs, seg=seg),
        grid_spec=grid_spec,
        out_shape=jax.ShapeDtypeStruct((t, d), F32),
        compiler_params=_cparams(("parallel", "arbitrary", "arbitrary"), 15),
        name="moe_ple_ln",
    )(cnt.reshape(-1), x2d, comb, rank, rank_t, p2d, wg, wu, wd, plw, pgw, pgb, g, b)


def _prep_in_weights(w_in):
    o = COL_OFF
    bw = BRANCH_WIDTH
    cols = lambda n: w_in[:, o[n]:o[n + 1]]
    qa, ka, va = cols(0), cols(1), cols(2)
    w_dil = jnp.concatenate(
        [t[:, g * bw:(g + 1) * bw] for g in range(N_DIL) for t in (qa, ka, va)], axis=1).astype(CDT)
    w_conv = jnp.concatenate([cols(3), cols(4), cols(5)], axis=1).astype(CDT)
    gn = cols(13)
    w_gate = jnp.concatenate([gn[:, br * NSA_Q_HEADS + GQA_COL_HEAD] for br in range(3)], axis=1).astype(CDT)
    w_rest = jnp.concatenate([cols(6)[:, GQA_COL_PERM], cols(14)[:, GQA_COL_PERM], cols(11),
                              cols(12), cols(15), cols(16), cols(9), cols(7), cols(8)], axis=1).astype(CDT)
    wt_vsc = cols(10).T.astype(CDT)
    return w_dil, w_conv, w_gate, w_rest, wt_vsc


ZR_Q_NSA, ZR_Q_SWA = 0, 1
ZR_KWC, ZR_VWC, ZR_KD, ZR_VD = 6, 7, 8, 9
ZR_WIDTH = 2 * BRANCH_WIDTH + 4 * LANES
SLC_KEY_TILE = 1024


def _token_mixers(x, w_in, conv_w, cmp_pos, cmp_w1, cmp_b1, cmp_w2, cmp_b2, sinks):
    b, s, d = x.shape
    x2d = x.reshape(b * s, d)
    w_dil, w_conv, w_gate, w_rest, wt_vsc = _prep_in_weights(w_in)
    gw = 3 * BRANCH_WIDTH

    z_dil = _linear(x2d, w_dil, [(g * gw, (g + 1) * gw) for g in range(N_DIL)], gw, "in_proj_dil",
                    dils=[dil for _, dil in DIL_PATTERNS])
    zr, ksc, kcc, vcc = _linear(x2d, w_rest, [(0, ZR_WIDTH)] + [(ZR_WIDTH + n * LANES, ZR_WIDTH + (n + 1) * LANES)
                                                           for n in range(3)], 256, "in_proj_rest")
    zr = zr.reshape(b, s, ZR_WIDTH)
    o_b = _conv(x, w_conv, conv_w)

    dil_o, dil_lse = [], []
    for g, (window, dil) in enumerate(DIL_PATTERNS):
        view = z_dil[g].reshape(b, s // dil, dil * gw)
        og, lg = _banded(view, view, view, nrep=dil,
                         qcol=lambda r: 3 * r, kcol=lambda r: 3 * r + 1, vcol=lambda r: 3 * r + 2,
                         kw=3 * LANES, window=window // dil, want_lse=True)
        dil_o.append(og.reshape(b * s // dil, dil * BRANCH_WIDTH))
        dil_lse.append(lg.reshape(b * s // dil, dil * BRANCH_WIDTH))

    kc = _compress(kcc.reshape(b, s, LANES), cmp_pos[0], cmp_w1[0], cmp_b1[0], cmp_w2[0], cmp_b2[0])
    vc = _compress(vcc.reshape(b, s, LANES), cmp_pos[1], cmp_w1[1], cmp_b1[1], cmp_w2[1], cmp_b2[1])
    o_cmp, selb = _cmp_select(zr, ZR_Q_NSA, kc, vc)
    vsc_t = _linear_t(x2d, wt_vsc, min(SLC_KEY_TILE, s), "in_proj_vsc_t")
    o_slc = _slc(zr, ZR_Q_NSA, ksc.reshape(b, s, LANES), vsc_t, selb)
    (o_win,) = _banded(zr, zr, zr, nrep=1, qcol=lambda r: ZR_Q_NSA, kcol=lambda r: ZR_KWC, vcol=lambda r: ZR_VWC,
                       kw=LANES, window=NSA_WINDOW - 1, want_lse=False, tq=512)

    sink_row = sinks.astype(F32)[GQA_COL_HEAD].reshape(1, BRANCH_WIDTH)
    (o_d,) = _banded(zr, zr, zr, nrep=1, qcol=lambda r: ZR_Q_SWA, kcol=lambda r: ZR_KD, vcol=lambda r: ZR_VD,
                     kw=LANES, window=SWA_WINDOW - 1, want_lse=False, sink_row=sink_row)

    t = b * s
    flat = lambda a: a.reshape(t, a.shape[-1])
    return [dil_o[0], dil_o[1], dil_o[2], dil_lse[0], dil_lse[1], dil_lse[2], flat(o_b), flat(o_cmp), flat(o_slc),
            flat(o_win), flat(o_d)], w_gate


def kernel(x, p, w_in, conv_w, cmp_pos, cmp_w1, cmp_b1, cmp_w2, cmp_b2, sinks, w_branch, w_merge_gate, b_merge_gate, w_out, ln_mix_g, ln_mix_b, ffn_w_gate, ffn_w_up, ffn_w_down, w_router, b_router, moe_w_gate, moe_w_up, moe_w_down, ple_w, ple_gate_w, ple_gate_b, ln_ffn_g, ln_ffn_b):
    depth, b, s, _ = p.shape
    d = x.shape[-1]
    t = b * s
    alpha = (2 * depth) ** 0.25
    row = lambda v: v.reshape(1, -1).astype(F32)
    for i in range(depth):
        branch_inputs, w_nsa_gate = _token_mixers(x, w_in[i], conv_w[i], cmp_pos[i], cmp_w1[i], cmp_b1[i], cmp_w2[i],
                                                  cmp_b2[i], sinks[i])
        wg = jnp.concatenate([w_merge_gate[i, m] for m in range(N_BRANCH)], axis=1).astype(CDT)
        bg = b_merge_gate[i].reshape(1, N_BRANCH * d).astype(F32)
        wb = jnp.stack([w_branch[i, 0], w_branch[i, 1], w_branch[i, 2][GQA_COL_PERM],
                        w_branch[i, 3][GQA_COL_PERM]]).astype(CDT)
        x1 = _merge(x.reshape(t, d), branch_inputs, w_nsa_gate, wg, bg, wb, w_out[i].astype(CDT),
                    row(ln_mix_g[i]), row(ln_mix_b[i]), alpha)
        p2d = p[i].reshape(t, -1)
        ple_args = (ple_w[i].astype(CDT), ple_gate_w[i].astype(CDT), row(ple_gate_b[i]),
                    row(ln_ffn_g[i]), row(ln_ffn_b[i]))
        j = i // 2
        if i % 2 == 0:
            x2 = _ffn(x1, p2d, ffn_w_gate[j], ffn_w_up[j], ffn_w_down[j], *ple_args, alpha)
        else:
            seg = min(MOE_SEGMENT, t)
            routing = _router(x1, w_router[j], b_router[j], seg)
            x2 = _moe(x1, routing, p2d, moe_w_gate[j], moe_w_up[j], moe_w_down[j], *ple_args, alpha, seg)
        x = x2.reshape(b, s, d)
    return x
```

```python
import functools

import numpy as np
import jax
import jax.numpy as jnp
from jax import lax
from jax.experimental import pallas as pl
from jax.experimental.pallas import tpu as pltpu

HEAD_DIM = 64
DIL_PATTERNS = ((128, 1), (512, 4), (2048, 16))
N_DIL = 3
DIL_HEADS = 6
CONV_WIDTH = 384
NSA_Q_HEADS = 6
NSA_KV_HEADS = 2
CMP_BLOCK = 32
CMP_STRIDE = 16
CMP_HIDDEN = 128
SEL_BLOCK = 64
N_SEL = 16
NSA_WINDOW = 512
SWA_Q_HEADS = 6
SWA_WINDOW = 128
BRANCH_WIDTH = 384
N_BRANCH = 4
LN_EPS = 1e-5
NEG_INF = -1e30
DIL_WIDTH = N_DIL * DIL_HEADS * HEAD_DIM
COLUMN_SIZES = (DIL_WIDTH, DIL_WIDTH, DIL_WIDTH, CONV_WIDTH, CONV_WIDTH, CONV_WIDTH,
                NSA_Q_HEADS * HEAD_DIM, 128, 128, 128, 128, 128, 128, 3 * NSA_Q_HEADS,
                SWA_Q_HEADS * HEAD_DIM, 128, 128)
COL_OFF = np.concatenate([[0], np.cumsum(COLUMN_SIZES)]).tolist()

LANES = 128
V7X_VMEM_BYTES = 64 * 1024 * 1024

CDT = jnp.bfloat16
F32 = jnp.float32
QK_SCALE = HEAD_DIM ** -0.5
SUB_Q = 128
BAND_ROWS = 64
BIG = 1e30

_GQA_HEAD_ORDER = (0, 3, 1, 4, 2, 5)
GQA_COL_PERM = np.concatenate([np.arange(h * HEAD_DIM, (h + 1) * HEAD_DIM) for h in _GQA_HEAD_ORDER])
GQA_COL_HEAD = GQA_COL_PERM // HEAD_DIM


def _cparams(sem, vmem_sixteenths):
    return pltpu.CompilerParams(dimension_semantics=sem, vmem_limit_bytes=V7X_VMEM_BYTES * vmem_sixteenths // 16)


def _nt_dot(a, b):
    return lax.dot_general(a, b, (((1,), (1,)), ((), ())), preferred_element_type=F32)


def _dot(a, b):
    return jnp.dot(a, b, preferred_element_type=F32)


def _layer_norm(r, g, b):
    mu = jnp.mean(r, axis=-1, keepdims=True)
    d = r - mu
    var = jnp.mean(d * d, axis=-1, keepdims=True)
    return d * lax.rsqrt(var + LN_EPS) * g + b


def _half_masks():
    lane = lax.broadcasted_iota(jnp.int32, (1, LANES), 1)
    return lane < HEAD_DIM


def _linear_kernel(x_ref, w_ref, *refs, splits, n_chunk, dils):
    o_refs, z_ref = refs[:len(splits)], refs[len(splits)]
    xb = x_ref[...].astype(CDT)
    tm = xb.shape[0]
    for o_ref, (c0, c1), dil in zip(o_refs, splits, dils):
        width = c1 - c0
        if dil == 1:
            for a in range(c0, c1, n_chunk):
                b = min(a + n_chunk, c1)
                o_ref[:, a - c0:b - c0] = _dot(xb, w_ref[:, a:b]).astype(o_ref.dtype)
        else:
            z = _dot(xb, w_ref[:, c0:c1])
            for cb in range(width // LANES):
                z_ref[cb] = z[:, cb * LANES:(cb + 1) * LANES]
            for r in range(dil):
                for cb in range(width // LANES):
                    o_ref[:, r * width + cb * LANES:r * width + (cb + 1) * LANES] = (
                        z_ref[cb, pl.ds(r, tm // dil, stride=dil), :].astype(o_ref.dtype))


def _linear(x2d, w, splits, n_chunk, name, dils=None, tm=512):
    t, k = x2d.shape
    tm = min(tm, t)
    n = w.shape[1]
    dils = tuple(dils) if dils is not None else (1,) * len(splits)
    widths = [c1 - c0 for c0, c1 in splits]
    assert all(tm % (16 * dl) == 0 for dl in dils)
    return pl.pallas_call(
        functools.partial(_linear_kernel, splits=tuple(splits), n_chunk=n_chunk, dils=dils),
        grid=(t // tm,),
        in_specs=[pl.BlockSpec((tm, k), lambda i: (i, 0)),
                  pl.BlockSpec((k, n), lambda i: (0, 0))],
        out_specs=[pl.BlockSpec((tm // dl, dl * wd), lambda i: (i, 0)) for wd, dl in zip(widths, dils)],
        out_shape=[jax.ShapeDtypeStruct((t // dl, dl * wd), CDT) for wd, dl in zip(widths, dils)],
        scratch_shapes=[pltpu.VMEM((max(widths) // LANES, tm, LANES), F32)],
        compiler_params=_cparams(("parallel",), 12),
        name=name,
    )(x2d, w)


def _value_t_kernel(x_ref, wt_ref, o0_ref, o1_ref):
    z = _nt_dot(wt_ref[...], x_ref[...].astype(CDT))
    row = lax.broadcasted_iota(jnp.int32, z.shape, 0)
    o0_ref[...] = jnp.where(row < HEAD_DIM, z, jnp.where(row == HEAD_DIM, 1.0, 0.0)).astype(o0_ref.dtype)
    o1_ref[...] = jnp.where(row >= HEAD_DIM, z, jnp.where(row == 0, 1.0, 0.0)).astype(o1_ref.dtype)


def _value_t(x2d, wt, tm, name):
    t, k = x2d.shape
    n = wt.shape[0]
    slab = pl.BlockSpec((None, n, tm), lambda i: (i, 0, 0))
    return pl.pallas_call(
        _value_t_kernel,
        grid=(t // tm,),
        in_specs=[pl.BlockSpec((tm, k), lambda i: (i, 0)), pl.BlockSpec((n, k), lambda i: (0, 0))],
        out_specs=[slab, slab],
        out_shape=[jax.ShapeDtypeStruct((t // tm, n, tm), CDT)] * 2,
        compiler_params=_cparams(("parallel",), 8),
        name=name,
    )(x2d, wt)


def _conv_kernel(x_ref, xh_ref, wc_ref, cw_ref, ob_ref, *, tm):
    i = pl.program_id(1)
    w = CONV_WIDTH
    xb = x_ref[...].astype(CDT)
    z = _dot(xb, wc_ref[...])
    u = z[:, w:2 * w] * z[:, 2 * w:3 * w]
    zh = _dot(xh_ref[...].astype(CDT), wc_ref[:, w:3 * w])
    uh = zh[:, :w] * zh[:, w:]
    uh = jnp.where(i == 0, 0.0, uh)
    row = lax.broadcasted_iota(jnp.int32, (tm, w), 0)
    u1 = jnp.where(row == 0, uh[7:8, :], pltpu.roll(u, 1, 0))
    u2 = jnp.where(row == 0, uh[6:7, :], jnp.where(row == 1, uh[7:8, :], pltpu.roll(u, 2, 0)))
    y = cw_ref[0:1, :] * u2 + cw_ref[1:2, :] * u1 + cw_ref[2:3, :] * u
    ob_ref[...] = (z[:, :w] * y).astype(ob_ref.dtype)


def _conv(x, wc, conv_w, tm=512):
    b, s, d = x.shape
    tm = min(tm, s)
    hb = tm // 8
    return pl.pallas_call(
        functools.partial(_conv_kernel, tm=tm),
        grid=(b, s // tm),
        in_specs=[pl.BlockSpec((None, tm, d), lambda bi, i: (bi, i, 0)),
                  pl.BlockSpec((None, 8, d), lambda bi, i: (bi, jnp.maximum(i * hb - 1, 0), 0)),
                  pl.BlockSpec(wc.shape, lambda bi, i: (0, 0)),
                  pl.BlockSpec(conv_w.shape, lambda bi, i: (0, 0))],
        out_specs=pl.BlockSpec((None, tm, CONV_WIDTH), lambda bi, i: (bi, i, 0)),
        out_shape=jax.ShapeDtypeStruct((b, s, CONV_WIDTH), CDT),
        compiler_params=_cparams(("parallel", "parallel"), 12),
        name="short_conv",
    )(x, x, wc, conv_w)


def _banded_kernel(*refs, window, pr, tq, kw, want_lse, has_sink):
    q_ref, kp_ref, kc_ref, vp_ref, vc_ref = refs[:5]
    n = 5
    sink_ref = None
    if has_sink:
        sink_ref = refs[n]
        n += 1
    o_ref = refs[n]
    n += 1
    lse_ref = None
    if want_lse:
        lse_ref = refs[n]
        n += 1
    kbuf, vbuf, s_ref, e_ref, m_ref, l_ref = refs[n:n + 6]

    i = pl.program_id(2)
    kbuf[0:pr, :] = kp_ref[...]
    kbuf[pr:pr + tq, :] = kc_ref[...]
    vbuf[0:pr, :] = vp_ref[...]
    vbuf[pr:pr + tq, :] = vc_ref[...]

    span = SUB_Q + pr
    qi = lax.broadcasted_iota(jnp.int32, (SUB_Q, span), 0)
    kj = lax.broadcasted_iota(jnp.int32, (SUB_Q, span), 1)
    dist = pr + qi - kj
    band = (dist >= 0) & (dist <= window)
    lo = _half_masks()
    halves = (lo, jnp.logical_not(lo))
    groups = ((0,), (1,), (2,)) if kw == 3 * LANES else ((0, 1, 2),)
    rb = BAND_ROWS
    for sb in range(tq // SUB_Q):
        r0 = sb * SUB_Q
        bias = jnp.where(band & (i * tq + r0 - pr + kj >= 0), 0.0, NEG_INF)
        for grp in groups:
            kc0 = grp[0] * LANES if kw == 3 * LANES else 0
            qs = jnp.concatenate(
                [(jnp.where(hm, q_ref[r0:r0 + SUB_Q, p * LANES:(p + 1) * LANES], 0) * QK_SCALE).astype(CDT)
                 for p in grp for hm in halves], axis=0)
            g0 = 2 * grp[0] * SUB_Q
            s_ref[g0:g0 + qs.shape[0], :] = _nt_dot(qs, kbuf[r0:r0 + span, kc0:kc0 + LANES])
        for c0 in range(0, 6 * SUB_Q, rb):
            rows = slice(c0, c0 + rb)
            s = s_ref[rows, :] + bias[c0 % SUB_Q:c0 % SUB_Q + rb, :]
            m = jnp.max(s, axis=-1, keepdims=True)
            e = jnp.exp(s - m)
            e_ref[rows, :] = e.astype(CDT)
            m_ref[rows, :] = jnp.broadcast_to(m, (rb, LANES))
            l_ref[rows, :] = jnp.broadcast_to(jnp.sum(e, axis=-1, keepdims=True), (rb, LANES))
        for grp in groups:
            kc0 = grp[0] * LANES if kw == 3 * LANES else 0
            g0 = 2 * grp[0] * SUB_Q
            g1 = g0 + 2 * len(grp) * SUB_Q
            l = l_ref[g0:g1, :]
            o = _dot(e_ref[g0:g1, :], vbuf[r0:r0 + span, kc0:kc0 + LANES]) / l
            lse = m_ref[g0:g1, :] + jnp.log(l)
            for n_p, p in enumerate(grp):
                a = 2 * n_p * SUB_Q
                o_pair = jnp.where(lo, o[a:a + SUB_Q], o[a + SUB_Q:a + 2 * SUB_Q])
                lse_pair = jnp.where(lo, lse[a:a + SUB_Q], lse[a + SUB_Q:a + 2 * SUB_Q])
                if has_sink:
                    o_pair = o_pair * jax.nn.sigmoid(lse_pair - sink_ref[:, p * LANES:(p + 1) * LANES])
                o_ref[r0:r0 + SUB_Q, p * LANES:(p + 1) * LANES] = o_pair.astype(o_ref.dtype)
                if want_lse:
                    lse_ref[r0:r0 + SUB_Q, p * LANES:(p + 1) * LANES] = lse_pair


def _banded(qa, ka, va, *, nrep, qcol, kcol, vcol, kw, window, want_lse, sink_row=None, tq=512):
    b, l, _ = qa.shape
    pr = -(-window // SUB_Q) * SUB_Q
    tq = min(max(tq, pr), l)
    assert tq % pr == 0 and l % tq == 0, (tq, pr, l)
    ratio = tq // pr
    qw = 3 * LANES
    in_specs = [
        pl.BlockSpec((None, tq, qw), lambda bi, r, i: (bi, i, qcol(r))),
        pl.BlockSpec((None, pr, kw), lambda bi, r, i: (bi, jnp.maximum(i * ratio - 1, 0), kcol(r))),
        pl.BlockSpec((None, tq, kw), lambda bi, r, i: (bi, i, kcol(r))),
        pl.BlockSpec((None, pr, kw), lambda bi, r, i: (bi, jnp.maximum(i * ratio - 1, 0), vcol(r))),
        pl.BlockSpec((None, tq, kw), lambda bi, r, i: (bi, i, vcol(r))),
    ]
    args = [qa, ka, ka, va, va]
    if sink_row is not None:
        in_specs.append(pl.BlockSpec(sink_row.shape, lambda bi, r, i: (0, 0)))
        args.append(sink_row)
    out_specs = [pl.BlockSpec((None, tq, qw), lambda bi, r, i: (bi, i, r))]
    out_shape = [jax.ShapeDtypeStruct((b, l, nrep * qw), CDT)]
    if want_lse:
        out_specs.append(pl.BlockSpec((None, tq, qw), lambda bi, r, i: (bi, i, r)))
        out_shape.append(jax.ShapeDtypeStruct((b, l, nrep * qw), F32))
    res = pl.pallas_call(
        functools.partial(_banded_kernel, window=window, pr=pr, tq=tq, kw=kw, want_lse=want_lse,
                          has_sink=sink_row is not None),
        grid=(b, nrep, l // tq),
        in_specs=in_specs,
        out_specs=out_specs,
        out_shape=out_shape,
        scratch_shapes=[pltpu.VMEM((pr + tq, kw), ka.dtype), pltpu.VMEM((pr + tq, kw), va.dtype),
                        pltpu.VMEM((6 * SUB_Q, SUB_Q + pr), F32), pltpu.VMEM((6 * SUB_Q, SUB_Q + pr), CDT),
                        pltpu.VMEM((6 * SUB_Q, LANES), F32), pltpu.VMEM((6 * SUB_Q, LANES), F32)],
        compiler_params=_cparams(("parallel", "parallel", "parallel"), 8),
        name=f"banded_w{window}_k{kw}_r{nrep}",
    )(*args)
    return res


def _gelu_tanh(x):
    return 0.5 * x * (1.0 + jnp.tanh(0.7978845608028654 * (x + 0.044715 * (x * x * x))))


def _compress_kernel(x_ref, pa_ref, pb_ref, w1a_ref, w1b_ref, b1_ref, w2_ref, b2_ref, o_ref):
    x = x_ref[...].astype(F32)
    n = x.shape[0]
    a = _dot((x + pa_ref[...]).astype(CDT), w1a_ref[...])
    bm = _dot((x + pb_ref[...]).astype(CDT), w1b_ref[...])
    h = a + pltpu.roll(bm, n - 1, 0) + b1_ref[...]
    o_ref[...] = (_dot(_gelu_tanh(h).astype(CDT), w2_ref[...]) + b2_ref[...]).astype(o_ref.dtype)


def _compress(t, pos, w1, b1, w2, b2):
    b, s, _ = t.shape
    nch = s // CMP_STRIDE
    xw = CMP_STRIDE * LANES
    x = t.reshape(b, nch, xw)
    eye = jnp.eye(NSA_KV_HEADS, dtype=F32)
    w1r = w1.reshape(CMP_BLOCK, HEAD_DIM, CMP_HIDDEN)

    def expand_w1(part):
        return jnp.einsum('tdj,kl->tkdlj', part, eye).reshape(xw, NSA_KV_HEADS * CMP_HIDDEN).astype(CDT)

    def expand_pos(part):
        return jnp.broadcast_to(part[:, None, :], (CMP_STRIDE, NSA_KV_HEADS, HEAD_DIM)).reshape(1, xw)

    w1a, w1b = expand_w1(w1r[:CMP_STRIDE]), expand_w1(w1r[CMP_STRIDE:])
    pa, pb = expand_pos(pos[:CMP_STRIDE]), expand_pos(pos[CMP_STRIDE:])
    b1e = jnp.tile(b1, NSA_KV_HEADS).reshape(1, -1)
    w2e = jnp.einsum('jd,kl->kjld', w2, eye).reshape(NSA_KV_HEADS * CMP_HIDDEN, LANES).astype(CDT)
    b2e = jnp.tile(b2, NSA_KV_HEADS).reshape(1, -1)
    consts = [pa, pb, w1a, w1b, b1e, w2e, b2e]
    return pl.pallas_call(
        _compress_kernel,
        grid=(b,),
        in_specs=[pl.BlockSpec((None, nch, xw), lambda bi: (bi, 0, 0))]
        + [pl.BlockSpec(c.shape, lambda bi: (0, 0)) for c in consts],
        out_specs=pl.BlockSpec((None, nch, LANES), lambda bi: (bi, 0, 0)),
        out_shape=jax.ShapeDtypeStruct((b, nch, LANES), CDT),
        compiler_params=_cparams(("parallel",), 12),
        name="nsa_compress",
    )(x, *consts)


CMP_ROWS = 16


def _cmp_select_kernel(q_ref, kc_ref, vc_ref, ov_ref, o_ref, sel_ref, s_ref, p_ref, hi_ref, lo_ref,
                       *, tq, n_sel, tile0):
    i = pl.program_id(1) + tile0
    ncp = kc_ref.shape[0]
    ns = ov_ref.shape[1]
    lo = _half_masks()
    t_col = i * tq + lax.broadcasted_iota(jnp.int32, (tq, 1), 0)
    blk = lax.broadcasted_iota(jnp.int32, (tq, ns), 1)
    blk_t = lax.broadcasted_iota(jnp.int32, (ns, tq), 0)
    cur = t_col // SEL_BLOCK
    causal = blk <= cur
    forced = (blk == 0) | (blk == cur) | (blk == cur - 1)
    rb = CMP_ROWS
    c_end = lax.broadcasted_iota(jnp.int32, (rb, ncp), 1) * CMP_STRIDE + (CMP_BLOCK - 1)
    o_kv, work_t = [], []
    for kv, hm in enumerate((lo, jnp.logical_not(lo))):
        qs = jnp.concatenate(
            [(jnp.where(hm, q_ref[:, p * LANES:(p + 1) * LANES], 0) * QK_SCALE).astype(CDT) for p in range(3)], axis=0)
        s_ref[...] = _nt_dot(qs, kc_ref[...])
        for r0 in range(0, tq, rb):
            t_rows = i * tq + r0 + lax.broadcasted_iota(jnp.int32, (rb, 1), 0)
            vis_bias = jnp.where(c_end <= t_rows, 0.0, NEG_INF)
            has_visible = t_rows >= CMP_BLOCK - 1
            psum = jnp.zeros((rb, ncp), F32)
            for h in range(3):
                rows = slice(h * tq + r0, h * tq + r0 + rb)
                s = s_ref[rows, :] + vis_bias
                e = jnp.exp(s - jnp.max(s, axis=-1, keepdims=True))
                inv = jnp.where(has_visible, 1.0 / jnp.maximum(jnp.sum(e, axis=-1, keepdims=True), 1e-30), 0.0)
                pn = e * inv
                p_ref[rows, :] = pn.astype(CDT)
                psum = psum + pn
            p_hi = psum.astype(CDT)
            hi_ref[r0:r0 + rb, :] = p_hi
            lo_ref[r0:r0 + rb, :] = (psum - p_hi.astype(F32)).astype(CDT)
        o = _dot(p_ref[...], vc_ref[...])
        o_kv.append([o[p * tq:(p + 1) * tq] for p in range(3)])
        imp = _dot(hi_ref[...], ov_ref[...]) + _dot(lo_ref[...], ov_ref[...])
        work_t.append(jnp.where(causal & jnp.logical_not(forced), imp, -BIG).T)

    blk_lanes = blk_t[:, 0:LANES]

    def pick(_, work):
        m = jnp.max(work, axis=0, keepdims=True)
        idx = jnp.min(jnp.where(work == m, blk_lanes, ns), axis=0, keepdims=True)
        return jnp.where(blk_lanes == idx, -2.0 * BIG, work)

    for kv, start in enumerate(work_t):
        done = jnp.concatenate([lax.fori_loop(0, n_sel - 3, pick, start[:, c0:c0 + LANES], unroll=True)
                                for c0 in range(0, tq, LANES)], axis=1)
        taken = jnp.where((done < -BIG) & (start > -BIG), 1.0, 0.0).T
        selb = jnp.where(forced | (taken > 0.5), 0.0, NEG_INF)
        sel_ref[:, kv * ns:(kv + 1) * ns] = selb.astype(sel_ref.dtype)
    for p in range(3):
        o_ref[:, p * LANES:(p + 1) * LANES] = jnp.where(lo, o_kv[0][p], o_kv[1][p]).astype(o_ref.dtype)


CMP_CAUSAL_SPLITS = 4


def _cmp_select(zr, qcol, kc, vc, tq=256):
    b, s, _ = zr.shape
    ncp = kc.shape[1]
    ns = s // SEL_BLOCK
    n_sel = min(N_SEL, ns)
    assert n_sel >= 3, "selection needs room for the three forced blocks"
    tq = min(tq, s)
    c = np.arange(ncp)[:, None] * CMP_STRIDE
    j = np.arange(ns)[None, :] * SEL_BLOCK
    overlap = ((c < j + SEL_BLOCK) & (c + CMP_BLOCK - 1 >= j)).astype(np.float32)
    overlap[ncp - 1:, :] = 0.0
    ov = jnp.asarray(overlap, CDT)
    qw = 3 * LANES
    n_split = CMP_CAUSAL_SPLITS if (s // tq) % CMP_CAUSAL_SPLITS == 0 and ncp % (16 * CMP_CAUSAL_SPLITS) == 0 else 1
    tiles = s // tq // n_split
    outs, sels = [], []
    for part in range(n_split):
        tile0 = part * tiles
        ncp_part = ncp * (part + 1) // n_split
        o_part, sel_part = pl.pallas_call(
            functools.partial(_cmp_select_kernel, tq=tq, n_sel=n_sel, tile0=tile0),
            grid=(b, tiles),
            in_specs=[pl.BlockSpec((None, tq, qw), lambda bi, i, tile0=tile0: (bi, i + tile0, qcol)),
                      pl.BlockSpec((None, ncp_part, LANES), lambda bi, i: (bi, 0, 0)),
                      pl.BlockSpec((None, ncp_part, LANES), lambda bi, i: (bi, 0, 0)),
                      pl.BlockSpec((ncp_part, ns), lambda bi, i: (0, 0))],
            out_specs=[pl.BlockSpec((None, tq, qw), lambda bi, i: (bi, i, 0)),
                       pl.BlockSpec((None, tq, 2 * ns), lambda bi, i: (bi, i, 0))],
            out_shape=[jax.ShapeDtypeStruct((b, tiles * tq, qw), CDT),
                       jax.ShapeDtypeStruct((b, tiles * tq, 2 * ns), CDT)],
            scratch_shapes=[pltpu.VMEM((3 * tq, ncp_part), F32), pltpu.VMEM((3 * tq, ncp_part), CDT),
                            pltpu.VMEM((tq, ncp_part), CDT), pltpu.VMEM((tq, ncp_part), CDT)],
            compiler_params=_cparams(("parallel", "parallel"), 12),
            name=f"nsa_cmp_select_p{part}",
        )(zr, kc, vc, ov)
        outs.append(o_part)
        sels.append(sel_part)
    return jnp.concatenate(outs, axis=1), jnp.concatenate(sels, axis=1)


SLC_KEY_CHUNK = 256


def _slc_kernel(q_ref, k0_ref, k1_ref, vt0_ref, vt1_ref, selb_ref, o_ref, s_ref, e_ref, m_ref, a_ref, acc_ref,
                *, tq, tk):
    i = pl.program_id(1)
    ns = selb_ref.shape[1] // 2
    vt_refs = (vt0_ref, vt1_ref)
    den_row = (HEAD_DIM, 0)
    bpt = tk // SEL_BLOCK
    n_q = 3 * tq
    rc = min(SLC_KEY_CHUNK, tk)
    lo = _half_masks()
    n_tiles = ((i + 1) * tq + tk - 1) // tk
    p_row = lax.broadcasted_iota(jnp.int32, (ns, LANES), 0)
    p_col = lax.broadcasted_iota(jnp.int32, (ns, LANES), 1)
    groups = []
    for kv, (hm, k_ref) in enumerate(((lo, k0_ref), (jnp.logical_not(lo), k1_ref))):
        q3 = [(jnp.where(hm, q_ref[:, p * LANES:(p + 1) * LANES], 0) * QK_SCALE).astype(CDT) for p in range(3)]
        selb = selb_ref[:, kv * ns:(kv + 1) * ns]
        lane0 = HEAD_DIM if kv == 0 else 0
        groups.append((kv, hm, k_ref, q3, selb, lane0))
    m_ref[...] = jnp.full(m_ref.shape, NEG_INF, F32)
    acc_ref[...] = jnp.zeros(acc_ref.shape, F32)

    def scores(j):
        k0 = pl.multiple_of(j * tk, tk)
        for kv, hm, k_ref, q3, selb, lane0 in groups:
            place = ((p_col >= lane0) & (p_col < lane0 + bpt) & (p_row == p_col - lane0 + j * bpt)).astype(CDT)
            sb = _dot(selb, place).astype(CDT)
            qp = jnp.concatenate([jnp.where(hm, q, sb) for q in q3], axis=0)
            s_ref[kv] = _nt_dot(k_ref[pl.ds(k0, tk), :], qp)

    def softmax_pv(j, diagonal):
        k0 = pl.multiple_of(j * tk, tk)
        for kv in range(2):
            for c0 in range(0, n_q, LANES):
                cols = slice(c0, c0 + LANES)
                t_lane = i * tq + (c0 % tq) + lax.broadcasted_iota(jnp.int32, (1, LANES), 1)

                def chunk(r0):
                    s = s_ref[kv, r0:r0 + rc, cols]
                    if diagonal:
                        kpos = k0 + r0 + lax.broadcasted_iota(jnp.int32, (rc, LANES), 0)
                        s = jnp.where(kpos <= t_lane, s, NEG_INF)
                    return s

                m8 = m_ref[kv, :, cols]
                for r0 in range(0, tk, rc):
                    m8 = jnp.maximum(m8, jnp.max(chunk(r0).reshape(rc // 8, 8, LANES), axis=0))
                m_new = jnp.max(m8, axis=0, keepdims=True)
                a_ref[kv, :, cols] = jnp.exp(m_ref[kv, :, cols] - m_new)
                m_ref[kv, :, cols] = jnp.broadcast_to(m_new, (8, LANES))
                for r0 in range(0, tk, rc):
                    e_ref[kv, r0:r0 + rc, cols] = jnp.exp(chunk(r0) - m_new).astype(CDT)
        for kv in range(2):
            acc_ref[kv] = a_ref[kv, 0:1, :] * acc_ref[kv] + _dot(vt_refs[kv][j], e_ref[kv])

    def tile(j, diagonal):
        scores(j)
        softmax_pv(j, diagonal)

    lax.fori_loop(0, n_tiles - 1, lambda j, c: (tile(j, False), c)[1], 0)
    tile(n_tiles - 1, True)
    o0, o1 = ((acc_ref[kv] / acc_ref[kv, den_row[kv]:den_row[kv] + 1, :]).T for kv in range(2))
    for p in range(3):
        o_ref[:, p * LANES:(p + 1) * LANES] = jnp.where(
            lo, o0[p * tq:(p + 1) * tq], o1[p * tq:(p + 1) * tq]).astype(o_ref.dtype)


def _slc(zr, qcol, ksc, vts, selb, tq=512):
    b, s, _ = zr.shape
    tq = min(tq, s)
    tk = vts[0].shape[2]
    assert tq % LANES == 0 and tk // SEL_BLOCK <= HEAD_DIM and s % tk == 0
    qw = 3 * LANES
    ns2 = selb.shape[2]
    pat = jax.nn.one_hot((jnp.arange(s) // SEL_BLOCK) % (tk // SEL_BLOCK), HEAD_DIM, dtype=ksc.dtype)
    pat = jnp.broadcast_to(pat[None], (b, s, HEAD_DIM))
    k0 = jnp.concatenate([ksc[..., :HEAD_DIM], pat], axis=-1)
    k1 = jnp.concatenate([pat, ksc[..., HEAD_DIM:]], axis=-1)
    full = lambda bi, i: (bi, 0, 0)
    return pl.pallas_call(
        functools.partial(_slc_kernel, tq=tq, tk=tk),
        grid=(b, s // tq),
        in_specs=[pl.BlockSpec((None, tq, qw), lambda bi, i: (bi, i, qcol)),
                  _single_buffered((None, s, LANES), full),
                  _single_buffered((None, s, LANES), full),
                  _single_buffered((s // tk, LANES, tk), lambda bi, i: (bi, 0, 0)),
                  _single_buffered((s // tk, LANES, tk), lambda bi, i: (bi, 0, 0)),
                  pl.BlockSpec((None, tq, ns2), lambda bi, i: (bi, i, 0))],
        out_specs=pl.BlockSpec((None, tq, qw), lambda bi, i: (bi, i, 0)),
        out_shape=jax.ShapeDtypeStruct((b, s, qw), CDT),
        scratch_shapes=[pltpu.VMEM((2, tk, 3 * tq), F32), pltpu.VMEM((2, tk, 3 * tq), CDT),
                        pltpu.VMEM((2, 8, 3 * tq), F32), pltpu.VMEM((2, 8, 3 * tq), F32),
                        pltpu.VMEM((2, LANES, 3 * tq), F32)],
        compiler_params=_cparams(("parallel", "arbitrary"), 15),
        name="nsa_slc",
    )(zr, k0, k1, vts[0], vts[1], selb)


def _merge_kernel(x_ref, oa0, oa1, oa2, la0, la1, la2, ob, ocmp, oslc, owin, od,
                  wn_ref, wg_ref, bg_ref, wb_ref, wo_ref, g_ref, b_ref, o_ref, *scratch, alpha):
    x = x_ref[...]
    xb = x.astype(CDT)
    bw = BRANCH_WIDTH

    def token_rows(src_ref, dst_ref):
        dil = src_ref.shape[1] // bw
        if dil == 1:
            return src_ref[...].astype(F32)
        n_cb = bw // LANES
        for r in range(dil):
            for cb in range(n_cb):
                c0 = r * bw + cb * LANES
                dst_ref[cb, pl.ds(r, src_ref.shape[0], stride=dil), :] = src_ref[:, c0:c0 + LANES].astype(F32)
        return jnp.concatenate([dst_ref[cb] for cb in range(n_cb)], axis=1)

    o0, o1, o2 = (token_rows(s, d) for s, d in zip((oa0, oa1, oa2), scratch[0:3]))
    l0, l1, l2 = (token_rows(s, d) for s, d in zip((la0, la1, la2), scratch[3:6]))
    mx = jnp.maximum(jnp.maximum(l0, l1), l2)
    w0, w1, w2 = jnp.exp(l0 - mx), jnp.exp(l1 - mx), jnp.exp(l2 - mx)
    o_a = (w0 * o0 + w1 * o1 + w2 * o2) / (w0 + w1 + w2)
    gates = jax.nn.sigmoid(_dot(xb, wn_ref[...]))
    o_c = (gates[:, 0:bw] * ocmp[...].astype(F32) + gates[:, bw:2 * bw] * oslc[...].astype(F32)
           + gates[:, 2 * bw:3 * bw] * owin[...].astype(F32))
    branches = (o_a.astype(CDT), ob[...], o_c.astype(CDT), od[...])
    d = x.shape[1]
    merged = jnp.zeros(x.shape, F32)
    for m in range(N_BRANCH):
        gate = jax.nn.sigmoid(_dot(xb, wg_ref[:, m * d:(m + 1) * d]) + bg_ref[:, m * d:(m + 1) * d])
        merged = merged + gate * _dot(branches[m], wb_ref[m])
    r = alpha * x + _dot(merged.astype(CDT), wo_ref[...])
    o_ref[...] = _layer_norm(r, g_ref[...], b_ref[...])


def _merge(x2d, branch_inputs, wn, wg, bg, wb, wo, g, b, alpha, tm=256):
    t, d = x2d.shape
    tm = min(tm, t)
    row = lambda i: (i, 0)
    const2 = lambda i: (0, 0)
    in_specs = [pl.BlockSpec((tm, d), row)]
    in_specs += [pl.BlockSpec((tm * a.shape[0] // t, a.shape[1]), row) for a in branch_inputs]
    in_specs += [pl.BlockSpec(wn.shape, const2), pl.BlockSpec(wg.shape, const2), pl.BlockSpec(bg.shape, const2),
                 pl.BlockSpec(wb.shape, lambda i: (0, 0, 0)), pl.BlockSpec(wo.shape, const2),
                 pl.BlockSpec(g.shape, const2), pl.BlockSpec(b.shape, const2)]
    return pl.pallas_call(
        functools.partial(_merge_kernel, alpha=alpha),
        grid=(t // tm,),
        in_specs=in_specs,
        out_specs=pl.BlockSpec((tm, d), row),
        out_shape=jax.ShapeDtypeStruct((t, d), F32),
        scratch_shapes=[pltpu.VMEM((BRANCH_WIDTH // LANES, tm, LANES), F32) for _ in range(2 * N_DIL)],
        compiler_params=_cparams(("parallel",), 14),
        name="merge_ln",
    )(x2d, *branch_inputs, wn, wg, bg, wb, wo, g, b)


def _ple_ln(x, xb, f, p, plw_ref, pgw_ref, pgb_ref, g_ref, b_ref, alpha):
    ple = jax.nn.sigmoid(_dot(xb, pgw_ref[...]) + pgb_ref[...]) * _dot(p.astype(CDT), plw_ref[...])
    return _layer_norm(alpha * x + f + ple, g_ref[...], b_ref[...])


FFN_CHUNK = 512


def _ffn_kernel(x_ref, p_ref, wg_ref, wu_ref, wd_ref, plw_ref, pgw_ref, pgb_ref, g_ref, b_ref, o_ref, h_ref, *, alpha):
    x = x_ref[...]
    xb = x.astype(CDT)
    dff = wg_ref.shape[1]
    for c0 in range(0, dff, FFN_CHUNK):
        cols = slice(c0, min(c0 + FFN_CHUNK, dff))
        h_ref[:, cols] = (jax.nn.silu(_dot(xb, wg_ref[:, cols])) * _dot(xb, wu_ref[:, cols])).astype(CDT)
    f = _dot(h_ref[...], wd_ref[...])
    o_ref[...] = _ple_ln(x, xb, f, p_ref[...], plw_ref, pgw_ref, pgb_ref, g_ref, b_ref, alpha)


def _ffn(x2d, p2d, wg, wu, wd, plw, pgw, pgb, g, b, alpha, tm=512):
    t, d = x2d.shape
    tm = min(tm, t)
    dff = wg.shape[1]
    wg, wu, wd = wg.astype(CDT), wu.astype(CDT), wd.astype(CDT)
    row = lambda i: (i, 0)
    const = lambda shape: pl.BlockSpec(shape, lambda i: (0, 0), pipeline_mode=pl.Buffered(1))
    return pl.pallas_call(
        functools.partial(_ffn_kernel, alpha=alpha),
        grid=(t // tm,),
        in_specs=[pl.BlockSpec((tm, d), row), pl.BlockSpec((tm, p2d.shape[1]), row),
                  const(wg.shape), const(wu.shape), const(wd.shape),
                  const(plw.shape), const(pgw.shape), const(pgb.shape), const(g.shape), const(b.shape)],
        out_specs=pl.BlockSpec((tm, d), row),
        out_shape=jax.ShapeDtypeStruct((t, d), F32),
        scratch_shapes=[pltpu.VMEM((tm, dff), CDT)],
        compiler_params=_cparams(("parallel",), 12),
        name="ffn_ple_ln",
    )(x2d, p2d, wg, wu, wd, plw, pgw, pgb, g, b)


def _router_kernel(x_ref, wh_ref, wl_ref, b_ref, comb_ref, rank_ref, rank_t_ref, cnt_ref):
    x = x_ref[...]
    xh = x.astype(CDT)
    xl = (x - xh.astype(F32)).astype(CDT)
    logits = _dot(xh, wh_ref[...]) + _dot(xh, wl_ref[...]) + _dot(xl, wh_ref[...]) + b_ref[...]
    lane = lax.broadcasted_iota(jnp.int32, logits.shape, 1)
    v1 = jnp.max(logits, axis=-1, keepdims=True)
    i1 = jnp.min(jnp.where(logits == v1, lane, LANES), axis=-1, keepdims=True)
    rest = jnp.where(lane == i1, -jnp.inf, logits)
    v2 = jnp.max(rest, axis=-1, keepdims=True)
    i2 = jnp.min(jnp.where(rest == v2, lane, LANES), axis=-1, keepdims=True)
    e2 = jnp.exp(v2 - v1)
    comb_ref[...] = jnp.where(lane == i1, 1.0 / (1.0 + e2), 0.0) + jnp.where(lane == i2, e2 / (1.0 + e2), 0.0)
    routed = (lane == i1) | (lane == i2)
    mask = routed.astype(CDT)
    tm = x.shape[0]
    before = (lax.broadcasted_iota(jnp.int32, (tm, tm), 1) < lax.broadcasted_iota(jnp.int32, (tm, tm), 0)).astype(CDT)
    rank = jnp.where(routed, _dot(before, mask), -1.0)
    rank_ref[...] = rank
    rank_t_ref[...] = rank.T[0:rank_t_ref.shape[0], :]
    cnt_ref[...] = jnp.sum(routed.astype(F32), axis=0, keepdims=True).astype(jnp.int32)


def _router(x2d, w_router, b_router, tm):
    t, d = x2d.shape
    ne = w_router.shape[1]
    wp = jnp.zeros((d, LANES), F32).at[:, :ne].set(w_router)
    wh = wp.astype(CDT)
    wl = (wp - wh.astype(F32)).astype(CDT)
    bp = jnp.full((1, LANES), -BIG, F32).at[0, :ne].set(b_router)
    nt = t // tm
    row = lambda i: (i, 0)
    return pl.pallas_call(
        _router_kernel,
        grid=(nt,),
        in_specs=[pl.BlockSpec((tm, d), row), pl.BlockSpec(wh.shape, lambda i: (0, 0)),
                  pl.BlockSpec(wl.shape, lambda i: (0, 0)), pl.BlockSpec(bp.shape, lambda i: (0, 0))],
        out_specs=[pl.BlockSpec((tm, LANES), row), pl.BlockSpec((tm, LANES), row),
                   pl.BlockSpec((None, 8, tm), lambda i: (i, 0, 0)),
                   pl.BlockSpec((None, 1, LANES), lambda i: (i, 0, 0))],
        out_shape=[jax.ShapeDtypeStruct((t, LANES), F32), jax.ShapeDtypeStruct((t, LANES), F32),
                   jax.ShapeDtypeStruct((nt, 8, tm), F32), jax.ShapeDtypeStruct((nt, 1, LANES), jnp.int32)],
        compiler_params=_cparams(("parallel",), 10),
        name="moe_router",
    )(x2d, wh, wl, bp)


def _moe_kernel(cnt_ref, x_ref, comb_ref, rank_ref, rank_t_ref, p_ref, wg_ref, wu_ref, wd_ref, plw_ref, pgw_ref,
                pgb_ref, g_ref, b_ref, o_ref, xe_ref, ye_ref, *, alpha, rs, seg):
    i = pl.program_id(0)
    e = pl.program_id(1)
    c = pl.program_id(2)
    n_seg = x_ref.shape[0] // seg
    last_chunk = c == pl.num_programs(2) - 1

    first = slice(0, rs)
    toks = [slice(sg * seg, (sg + 1) * seg) for sg in range(n_seg)]
    n_groups = [(cnt_ref[(i * n_seg + sg) * LANES + e] + rs - 1) // rs for sg in range(n_seg)]

    def later(sc):
        return pl.ds(pl.multiple_of(sc * rs, 8), rs), (sc * rs).astype(F32)

    @pl.when((e == 0) & (c == 0))
    def _():
        o_ref[...] = jnp.zeros_like(o_ref)

    @pl.when(c == 0)
    def _():
        row_id = lax.broadcasted_iota(jnp.int32, (rs, seg), 0).astype(F32)
        xbs = [x_ref[tok, :].astype(CDT) for tok in toks]
        rank_rows = [rank_t_ref[sg, pl.ds(e, 1), :] for sg in range(n_seg)]

        def gather(sg, rws, base):
            onehot = (rank_rows[sg] - base == row_id).astype(CDT)
            xe_ref[sg, rws, :] = _dot(onehot, xbs[sg]).astype(CDT)
            ye_ref[sg, rws, :] = jnp.zeros((rs, ye_ref.shape[2]), F32)

        for sg in range(n_seg):
            gather(sg, first, 0.0)
        for sg in range(n_seg):
            lax.fori_loop(1, n_groups[sg], lambda sc, _, sg=sg: (gather(sg, *later(sc)), 0)[1], 0)

    def expert(sg, rws):
        xs = xe_ref[sg, rws, :]
        h = jax.nn.silu(_dot(xs, wg_ref[0])) * _dot(xs, wu_ref[0])
        ye_ref[sg, rws, :] += _dot(h.astype(CDT), wd_ref[0])

    for sg in range(n_seg):
        expert(sg, first)
    for sg in range(n_seg):
        lax.fori_loop(1, n_groups[sg], lambda sc, _, sg=sg: (expert(sg, later(sc)[0]), 0)[1], 0)

    @pl.when(last_chunk)
    def _():
        lane = lax.broadcasted_iota(jnp.int32, (seg, LANES), 1)
        mine = lane == e
        col_id = lax.broadcasted_iota(jnp.int32, (seg, rs), 1).astype(F32)
        cws = [jnp.sum(jnp.where(mine, comb_ref[tok, :], 0.0), axis=-1, keepdims=True) for tok in toks]
        rank_cols = [jnp.sum(jnp.where(mine, rank_ref[tok, :], 0.0), axis=-1, keepdims=True) for tok in toks]

        def scatter(sg, rws, base):
            onehot = (rank_cols[sg] - base == col_id).astype(CDT)
            o_ref[toks[sg], :] += cws[sg] * _dot(onehot, ye_ref[sg, rws, :].astype(CDT))

        for sg in range(n_seg):
            scatter(sg, first, 0.0)
        for sg in range(n_seg):
            lax.fori_loop(1, n_groups[sg], lambda sc, _, sg=sg: (scatter(sg, *later(sc)), 0)[1], 0)

    @pl.when((e == pl.num_programs(1) - 1) & last_chunk)
    def _():
        for sg in range(n_seg):
            tok = slice(sg * seg, (sg + 1) * seg)
            x = x_ref[tok, :]
            o_ref[tok, :] = _ple_ln(x, x.astype(CDT), o_ref[tok, :], p_ref[tok, :], plw_ref, pgw_ref, pgb_ref,
                                    g_ref, b_ref, alpha)


MOE_CHUNK = 512
MOE_SEGMENT = 1024
MOE_SEGMENTS_PER_TILE = 2
MOE_ROW_GROUP = 288


def _single_buffered(shape, index_map):
    return pl.BlockSpec(shape, index_map, pipeline_mode=pl.Buffered(1))


def _moe(x2d, routing, p2d, wg, wu, wd, plw, pgw, pgb, g, b, alpha, seg):
    comb, rank, rank_t, cnt = routing
    t, d = x2d.shape
    ne, _, dff = wg.shape
    ck = min(MOE_CHUNK, dff)
    rs = min(MOE_ROW_GROUP, seg)
    n_seg = min(MOE_SEGMENTS_PER_TILE, t // seg)
    tm = n_seg * seg
    max_rows = -(-seg // rs) * rs
    wg, wu, wd = wg.astype(CDT), wu.astype(CDT), wd.astype(CDT)
    row = lambda i, e, c, cnt: (i, 0)
    c2 = lambda i, e, c, cnt: (0, 0)
    grid_spec = pltpu.PrefetchScalarGridSpec(
        num_scalar_prefetch=1,
        grid=(t // tm, ne, dff // ck),
        in_specs=[_single_buffered((tm, d), row), _single_buffered((tm, LANES), row),
                  _single_buffered((tm, LANES), row),
                  _single_buffered((n_seg, 8, seg), lambda i, e, c, cnt: (i, 0, 0)),
                  _single_buffered((tm, p2d.shape[1]), row),
                  pl.BlockSpec((1, d, ck), lambda i, e, c, cnt: (e, 0, c)),
                  pl.BlockSpec((1, d, ck), lambda i, e, c, cnt: (e, 0, c)),
                  pl.BlockSpec((1, ck, d), lambda i, e, c, cnt: (e, c, 0)),
                  _single_buffered(plw.shape, c2), _single_buffered(pgw.shape, c2), _single_buffered(pgb.shape, c2),
                  _single_buffered(g.shape, c2), _single_buffered(b.shape, c2)],
        out_specs=pl.BlockSpec((tm, d), row),
        scratch_shapes=[pltpu.VMEM((n_seg, max_rows, d), CDT), pltpu.VMEM((n_seg, max_rows, d), F32)],
    )
    return pl.pallas_call(
        functools.partial(_moe_kernel, alpha=alpha, rs=rs, seg=seg),
        grid_spec=grid_spec,
        out_shape=jax.ShapeDtypeStruct((t, d), F32),
        compiler_params=_cparams(("parallel", "arbitrary", "arbitrary"), 15),
        name="moe_ple_ln",
    )(cnt.reshape(-1), x2d, comb, rank, rank_t, p2d, wg, wu, wd, plw, pgw, pgb, g, b)


def _prep_in_weights(w_in):
    o = COL_OFF
    bw = BRANCH_WIDTH
    cols = lambda n: w_in[:, o[n]:o[n + 1]]
    qa, ka, va = cols(0), cols(1), cols(2)
    w_dil = jnp.concatenate(
        [t[:, g * bw:(g + 1) * bw] for g in range(N_DIL) for t in (qa, ka, va)], axis=1).astype(CDT)
    w_conv = jnp.concatenate([cols(3), cols(4), cols(5)], axis=1).astype(CDT)
    gn = cols(13)
    w_gate = jnp.concatenate([gn[:, br * NSA_Q_HEADS + GQA_COL_HEAD] for br in range(3)], axis=1).astype(CDT)
    w_rest = jnp.concatenate([cols(6)[:, GQA_COL_PERM], cols(14)[:, GQA_COL_PERM], cols(11),
                              cols(12), cols(15), cols(16), cols(9), cols(7), cols(8)], axis=1).astype(CDT)
    wt_vsc = cols(10).T.astype(CDT)
    return w_dil, w_conv, w_gate, w_rest, wt_vsc


ZR_Q_NSA, ZR_Q_SWA = 0, 1
ZR_KWC, ZR_VWC, ZR_KD, ZR_VD = 6, 7, 8, 9
ZR_WIDTH = 2 * BRANCH_WIDTH + 4 * LANES
SLC_KEY_TILE = 1024


def _token_mixers(x, w_in, conv_w, cmp_pos, cmp_w1, cmp_b1, cmp_w2, cmp_b2, sinks):
    b, s, d = x.shape
    x2d = x.reshape(b * s, d)
    w_dil, w_conv, w_gate, w_rest, wt_vsc = _prep_in_weights(w_in)
    gw = 3 * BRANCH_WIDTH

    z_dil = _linear(x2d, w_dil, [(g * gw, (g + 1) * gw) for g in range(N_DIL)], gw, "in_proj_dil",
                    dils=[dil for _, dil in DIL_PATTERNS])
    zr, ksc, kcc, vcc = _linear(x2d, w_rest, [(0, ZR_WIDTH)] + [(ZR_WIDTH + n * LANES, ZR_WIDTH + (n + 1) * LANES)
                                                           for n in range(3)], 256, "in_proj_rest")
    zr = zr.reshape(b, s, ZR_WIDTH)
    o_b = _conv(x, w_conv, conv_w)

    dil_o, dil_lse = [], []
    for g, (window, dil) in enumerate(DIL_PATTERNS):
        view = z_dil[g].reshape(b, s // dil, dil * gw)
        og, lg = _banded(view, view, view, nrep=dil,
                         qcol=lambda r: 3 * r, kcol=lambda r: 3 * r + 1, vcol=lambda r: 3 * r + 2,
                         kw=3 * LANES, window=window // dil, want_lse=True)
        dil_o.append(og.reshape(b * s // dil, dil * BRANCH_WIDTH))
        dil_lse.append(lg.reshape(b * s // dil, dil * BRANCH_WIDTH))

    kc = _compress(kcc.reshape(b, s, LANES), cmp_pos[0], cmp_w1[0], cmp_b1[0], cmp_w2[0], cmp_b2[0])
    vc = _compress(vcc.reshape(b, s, LANES), cmp_pos[1], cmp_w1[1], cmp_b1[1], cmp_w2[1], cmp_b2[1])
    o_cmp, selb = _cmp_select(zr, ZR_Q_NSA, kc, vc)
    vsc_t = _value_t(x2d, wt_vsc, min(SLC_KEY_TILE, s), "in_proj_vsc_t")
    o_slc = _slc(zr, ZR_Q_NSA, ksc.reshape(b, s, LANES), vsc_t, selb)
    (o_win,) = _banded(zr, zr, zr, nrep=1, qcol=lambda r: ZR_Q_NSA, kcol=lambda r: ZR_KWC, vcol=lambda r: ZR_VWC,
                       kw=LANES, window=NSA_WINDOW - 1, want_lse=False, tq=512)

    sink_row = sinks.astype(F32)[GQA_COL_HEAD].reshape(1, BRANCH_WIDTH)
    (o_d,) = _banded(zr, zr, zr, nrep=1, qcol=lambda r: ZR_Q_SWA, kcol=lambda r: ZR_KD, vcol=lambda r: ZR_VD,
                     kw=LANES, window=SWA_WINDOW - 1, want_lse=False, sink_row=sink_row)

    t = b * s
    flat = lambda a: a.reshape(t, a.shape[-1])
    return [dil_o[0], dil_o[1], dil_o[2], dil_lse[0], dil_lse[1], dil_lse[2], flat(o_b), flat(o_cmp), flat(o_slc),
            flat(o_win), flat(o_d)], w_gate


def kernel(x, p, w_in, conv_w, cmp_pos, cmp_w1, cmp_b1, cmp_w2, cmp_b2, sinks, w_branch, w_merge_gate, b_merge_gate, w_out, ln_mix_g, ln_mix_b, ffn_w_gate, ffn_w_up, ffn_w_down, w_router, b_router, moe_w_gate, moe_w_up, moe_w_down, ple_w, ple_gate_w, ple_gate_b, ln_ffn_g, ln_ffn_b):
    depth, b, s, _ = p.shape
    d = x.shape[-1]
    t = b * s
    alpha = (2 * depth) ** 0.25
    row = lambda v: v.reshape(1, -1).astype(F32)
    for i in range(depth):
        branch_inputs, w_nsa_gate = _token_mixers(x, w_in[i], conv_w[i], cmp_pos[i], cmp_w1[i], cmp_b1[i], cmp_w2[i],
                                                  cmp_b2[i], sinks[i])
        wg = jnp.concatenate([w_merge_gate[i, m] for m in range(N_BRANCH)], axis=1).astype(CDT)
        bg = b_merge_gate[i].reshape(1, N_BRANCH * d).astype(F32)
        wb = jnp.stack([w_branch[i, 0], w_branch[i, 1], w_branch[i, 2][GQA_COL_PERM],
                        w_branch[i, 3][GQA_COL_PERM]]).astype(CDT)
        x1 = _merge(x.reshape(t, d), branch_inputs, w_nsa_gate, wg, bg, wb, w_out[i].astype(CDT),
                    row(ln_mix_g[i]), row(ln_mix_b[i]), alpha)
        p2d = p[i].reshape(t, -1)
        ple_args = (ple_w[i].astype(CDT), ple_gate_w[i].astype(CDT), row(ple_gate_b[i]),
                    row(ln_ffn_g[i]), row(ln_ffn_b[i]))
        j = i // 2
        if i % 2 == 0:
            x2 = _ffn(x1, p2d, ffn_w_gate[j], ffn_w_up[j], ffn_w_down[j], *ple_args, alpha)
        else:
            seg = min(MOE_SEGMENT, t)
            routing = _router(x1, w_router[j], b_router[j], seg)
            x2 = _moe(x1, routing, p2d, moe_w_gate[j], moe_w_up[j], moe_w_down[j], *ple_args, alpha, seg)
        x = x2.reshape(b, s, d)
    return x
```

```python
import functools

import numpy as np
import jax
import jax.numpy as jnp
from jax import lax
from jax.experimental import pallas as pl
from jax.experimental.pallas import tpu as pltpu

HEAD_DIM = 64
DIL_PATTERNS = ((128, 1), (512, 4), (2048, 16))
N_DIL = 3
DIL_HEADS = 6
CONV_WIDTH = 384
NSA_Q_HEADS = 6
NSA_KV_HEADS = 2
CMP_BLOCK = 32
CMP_STRIDE = 16
CMP_HIDDEN = 128
SEL_BLOCK = 64
N_SEL = 16
NSA_WINDOW = 512
SWA_Q_HEADS = 6
SWA_WINDOW = 128
BRANCH_WIDTH = 384
N_BRANCH = 4
LN_EPS = 1e-5
NEG_INF = -1e30
DIL_WIDTH = N_DIL * DIL_HEADS * HEAD_DIM
COLUMN_SIZES = (DIL_WIDTH, DIL_WIDTH, DIL_WIDTH, CONV_WIDTH, CONV_WIDTH, CONV_WIDTH,
                NSA_Q_HEADS * HEAD_DIM, 128, 128, 128, 128, 128, 128, 3 * NSA_Q_HEADS,
                SWA_Q_HEADS * HEAD_DIM, 128, 128)
COL_OFF = np.concatenate([[0], np.cumsum(COLUMN_SIZES)]).tolist()

LANES = 128
V7X_VMEM_BYTES = 64 * 1024 * 1024

CDT = jnp.bfloat16
F32 = jnp.float32
QK_SCALE = HEAD_DIM ** -0.5
SUB_Q = 128
BAND_ROWS = 64
BIG = 1e30

_GQA_HEAD_ORDER = (0, 3, 1, 4, 2, 5)
GQA_COL_PERM = np.concatenate([np.arange(h * HEAD_DIM, (h + 1) * HEAD_DIM) for h in _GQA_HEAD_ORDER])
GQA_COL_HEAD = GQA_COL_PERM // HEAD_DIM


def _cparams(sem, vmem_sixteenths):
    return pltpu.CompilerParams(dimension_semantics=sem, vmem_limit_bytes=V7X_VMEM_BYTES * vmem_sixteenths // 16)


def _nt_dot(a, b):
    return lax.dot_general(a, b, (((1,), (1,)), ((), ())), preferred_element_type=F32)


def _dot(a, b):
    return jnp.dot(a, b, preferred_element_type=F32)


def _layer_norm(r, g, b):
    mu = jnp.mean(r, axis=-1, keepdims=True)
    d = r - mu
    var = jnp.mean(d * d, axis=-1, keepdims=True)
    return d * lax.rsqrt(var + LN_EPS) * g + b


def _half_masks():
    lane = lax.broadcasted_iota(jnp.int32, (1, LANES), 1)
    return lane < HEAD_DIM


def _linear_kernel(x_ref, w_ref, *refs, splits, n_chunk, dils):
    o_refs, z_ref = refs[:len(splits)], refs[len(splits)]
    xb = x_ref[...].astype(CDT)
    tm = xb.shape[0]
    for o_ref, (c0, c1), dil in zip(o_refs, splits, dils):
        width = c1 - c0
        if dil == 1:
            for a in range(c0, c1, n_chunk):
                b = min(a + n_chunk, c1)
                o_ref[:, a - c0:b - c0] = _dot(xb, w_ref[:, a:b]).astype(o_ref.dtype)
        else:
            z = _dot(xb, w_ref[:, c0:c1])
            for cb in range(width // LANES):
                z_ref[cb] = z[:, cb * LANES:(cb + 1) * LANES]
            for r in range(dil):
                for cb in range(width // LANES):
                    o_ref[:, r * width + cb * LANES:r * width + (cb + 1) * LANES] = (
                        z_ref[cb, pl.ds(r, tm // dil, stride=dil), :].astype(o_ref.dtype))


def _linear(x2d, w, splits, n_chunk, name, dils=None, tm=512):
    t, k = x2d.shape
    tm = min(tm, t)
    n = w.shape[1]
    dils = tuple(dils) if dils is not None else (1,) * len(splits)
    widths = [c1 - c0 for c0, c1 in splits]
    assert all(tm % (16 * dl) == 0 for dl in dils)
    return pl.pallas_call(
        functools.partial(_linear_kernel, splits=tuple(splits), n_chunk=n_chunk, dils=dils),
        grid=(t // tm,),
        in_specs=[pl.BlockSpec((tm, k), lambda i: (i, 0)),
                  pl.BlockSpec((k, n), lambda i: (0, 0))],
        out_specs=[pl.BlockSpec((tm // dl, dl * wd), lambda i: (i, 0)) for wd, dl in zip(widths, dils)],
        out_shape=[jax.ShapeDtypeStruct((t // dl, dl * wd), CDT) for wd, dl in zip(widths, dils)],
        scratch_shapes=[pltpu.VMEM((max(widths) // LANES, tm, LANES), F32)],
        compiler_params=_cparams(("parallel",), 12),
        name=name,
    )(x2d, w)


def _linear_t_kernel(x_ref, wt_ref, o_ref):
    o_ref[...] = _nt_dot(wt_ref[...], x_ref[...].astype(CDT)).astype(o_ref.dtype)


def _linear_t(x2d, wt, tm, name):
    t, k = x2d.shape
    n = wt.shape[0]
    return pl.pallas_call(
        _linear_t_kernel,
        grid=(t // tm,),
        in_specs=[pl.BlockSpec((tm, k), lambda i: (i, 0)), pl.BlockSpec((n, k), lambda i: (0, 0))],
        out_specs=pl.BlockSpec((None, n, tm), lambda i: (i, 0, 0)),
        out_shape=jax.ShapeDtypeStruct((t // tm, n, tm), CDT),
        compiler_params=_cparams(("parallel",), 8),
        name=name,
    )(x2d, wt)


def _conv_kernel(x_ref, xh_ref, wc_ref, cw_ref, ob_ref, *, tm):
    i = pl.program_id(1)
    w = CONV_WIDTH
    xb = x_ref[...].astype(CDT)
    z = _dot(xb, wc_ref[...])
    u = z[:, w:2 * w] * z[:, 2 * w:3 * w]
    zh = _dot(xh_ref[...].astype(CDT), wc_ref[:, w:3 * w])
    uh = zh[:, :w] * zh[:, w:]
    uh = jnp.where(i == 0, 0.0, uh)
    row = lax.broadcasted_iota(jnp.int32, (tm, w), 0)
    u1 = jnp.where(row == 0, uh[7:8, :], pltpu.roll(u, 1, 0))
    u2 = jnp.where(row == 0, uh[6:7, :], jnp.where(row == 1, uh[7:8, :], pltpu.roll(u, 2, 0)))
    y = cw_ref[0:1, :] * u2 + cw_ref[1:2, :] * u1 + cw_ref[2:3, :] * u
    ob_ref[...] = (z[:, :w] * y).astype(ob_ref.dtype)


def _conv(x, wc, conv_w, tm=512):
    b, s, d = x.shape
    tm = min(tm, s)
    hb = tm // 8
    return pl.pallas_call(
        functools.partial(_conv_kernel, tm=tm),
        grid=(b, s // tm),
        in_specs=[pl.BlockSpec((None, tm, d), lambda bi, i: (bi, i, 0)),
                  pl.BlockSpec((None, 8, d), lambda bi, i: (bi, jnp.maximum(i * hb - 1, 0), 0)),
                  pl.BlockSpec(wc.shape, lambda bi, i: (0, 0)),
                  pl.BlockSpec(conv_w.shape, lambda bi, i: (0, 0))],
        out_specs=pl.BlockSpec((None, tm, CONV_WIDTH), lambda bi, i: (bi, i, 0)),
        out_shape=jax.ShapeDtypeStruct((b, s, CONV_WIDTH), CDT),
        compiler_params=_cparams(("parallel", "parallel"), 12),
        name="short_conv",
    )(x, x, wc, conv_w)


def _banded_kernel(*refs, window, pr, tq, kw, want_lse, has_sink):
    q_ref, kp_ref, kc_ref, vp_ref, vc_ref = refs[:5]
    n = 5
    sink_ref = None
    if has_sink:
        sink_ref = refs[n]
        n += 1
    o_ref = refs[n]
    n += 1
    lse_ref = None
    if want_lse:
        lse_ref = refs[n]
        n += 1
    kbuf, vbuf, s_ref, e_ref, m_ref, l_ref = refs[n:n + 6]

    i = pl.program_id(2)
    kbuf[0:pr, :] = kp_ref[...]
    kbuf[pr:pr + tq, :] = kc_ref[...]
    vbuf[0:pr, :] = vp_ref[...]
    vbuf[pr:pr + tq, :] = vc_ref[...]

    span = SUB_Q + pr
    qi = lax.broadcasted_iota(jnp.int32, (SUB_Q, span), 0)
    kj = lax.broadcasted_iota(jnp.int32, (SUB_Q, span), 1)
    dist = pr + qi - kj
    band = (dist >= 0) & (dist <= window)
    lo = _half_masks()
    halves = (lo, jnp.logical_not(lo))
    groups = ((0,), (1,), (2,)) if kw == 3 * LANES else ((0, 1, 2),)
    rb = BAND_ROWS
    for sb in range(tq // SUB_Q):
        r0 = sb * SUB_Q
        bias = jnp.where(band & (i * tq + r0 - pr + kj >= 0), 0.0, NEG_INF)
        for grp in groups:
            kc0 = grp[0] * LANES if kw == 3 * LANES else 0
            qs = jnp.concatenate(
                [(jnp.where(hm, q_ref[r0:r0 + SUB_Q, p * LANES:(p + 1) * LANES], 0) * QK_SCALE).astype(CDT)
                 for p in grp for hm in halves], axis=0)
            g0 = 2 * grp[0] * SUB_Q
            s_ref[g0:g0 + qs.shape[0], :] = _nt_dot(qs, kbuf[r0:r0 + span, kc0:kc0 + LANES])
        for c0 in range(0, 6 * SUB_Q, rb):
            rows = slice(c0, c0 + rb)
            s = s_ref[rows, :] + bias[c0 % SUB_Q:c0 % SUB_Q + rb, :]
            m = jnp.max(s, axis=-1, keepdims=True)
            e = jnp.exp(s - m)
            e_ref[rows, :] = e.astype(CDT)
            m_ref[rows, :] = jnp.broadcast_to(m, (rb, LANES))
            l_ref[rows, :] = jnp.broadcast_to(jnp.sum(e, axis=-1, keepdims=True), (rb, LANES))
        for grp in groups:
            kc0 = grp[0] * LANES if kw == 3 * LANES else 0
            g0 = 2 * grp[0] * SUB_Q
            g1 = g0 + 2 * len(grp) * SUB_Q
            l = l_ref[g0:g1, :]
            o = _dot(e_ref[g0:g1, :], vbuf[r0:r0 + span, kc0:kc0 + LANES]) / l
            lse = m_ref[g0:g1, :] + jnp.log(l)
            for n_p, p in enumerate(grp):
                a = 2 * n_p * SUB_Q
                o_pair = jnp.where(lo, o[a:a + SUB_Q], o[a + SUB_Q:a + 2 * SUB_Q])
                lse_pair = jnp.where(lo, lse[a:a + SUB_Q], lse[a + SUB_Q:a + 2 * SUB_Q])
                if has_sink:
                    o_pair = o_pair * jax.nn.sigmoid(lse_pair - sink_ref[:, p * LANES:(p + 1) * LANES])
                o_ref[r0:r0 + SUB_Q, p * LANES:(p + 1) * LANES] = o_pair.astype(o_ref.dtype)
                if want_lse:
                    lse_ref[r0:r0 + SUB_Q, p * LANES:(p + 1) * LANES] = lse_pair


def _banded(qa, ka, va, *, nrep, qcol, kcol, vcol, kw, window, want_lse, sink_row=None, tq=512):
    b, l, _ = qa.shape
    pr = -(-window // SUB_Q) * SUB_Q
    tq = min(max(tq, pr), l)
    assert tq % pr == 0 and l % tq == 0, (tq, pr, l)
    ratio = tq // pr
    qw = 3 * LANES
    in_specs = [
        pl.BlockSpec((None, tq, qw), lambda bi, r, i: (bi, i, qcol(r))),
        pl.BlockSpec((None, pr, kw), lambda bi, r, i: (bi, jnp.maximum(i * ratio - 1, 0), kcol(r))),
        pl.BlockSpec((None, tq, kw), lambda bi, r, i: (bi, i, kcol(r))),
        pl.BlockSpec((None, pr, kw), lambda bi, r, i: (bi, jnp.maximum(i * ratio - 1, 0), vcol(r))),
        pl.BlockSpec((None, tq, kw), lambda bi, r, i: (bi, i, vcol(r))),
    ]
    args = [qa, ka, ka, va, va]
    if sink_row is not None:
        in_specs.append(pl.BlockSpec(sink_row.shape, lambda bi, r, i: (0, 0)))
        args.append(sink_row)
    out_specs = [pl.BlockSpec((None, tq, qw), lambda bi, r, i: (bi, i, r))]
    out_shape = [jax.ShapeDtypeStruct((b, l, nrep * qw), CDT)]
    if want_lse:
        out_specs.append(pl.BlockSpec((None, tq, qw), lambda bi, r, i: (bi, i, r)))
        out_shape.append(jax.ShapeDtypeStruct((b, l, nrep * qw), F32))
    res = pl.pallas_call(
        functools.partial(_banded_kernel, window=window, pr=pr, tq=tq, kw=kw, want_lse=want_lse,
                          has_sink=sink_row is not None),
        grid=(b, nrep, l // tq),
        in_specs=in_specs,
        out_specs=out_specs,
        out_shape=out_shape,
        scratch_shapes=[pltpu.VMEM((pr + tq, kw), ka.dtype), pltpu.VMEM((pr + tq, kw), va.dtype),
                        pltpu.VMEM((6 * SUB_Q, SUB_Q + pr), F32), pltpu.VMEM((6 * SUB_Q, SUB_Q + pr), CDT),
                        pltpu.VMEM((6 * SUB_Q, LANES), F32), pltpu.VMEM((6 * SUB_Q, LANES), F32)],
        compiler_params=_cparams(("parallel", "parallel", "parallel"), 8),
        name=f"banded_w{window}_k{kw}_r{nrep}",
    )(*args)
    return res


def _gelu_tanh(x):
    return 0.5 * x * (1.0 + jnp.tanh(0.7978845608028654 * (x + 0.044715 * (x * x * x))))


def _compress_kernel(x_ref, pa_ref, pb_ref, w1a_ref, w1b_ref, b1_ref, w2_ref, b2_ref, o_ref):
    x = x_ref[...].astype(F32)
    n = x.shape[0]
    a = _dot((x + pa_ref[...]).astype(CDT), w1a_ref[...])
    bm = _dot((x + pb_ref[...]).astype(CDT), w1b_ref[...])
    h = a + pltpu.roll(bm, n - 1, 0) + b1_ref[...]
    o_ref[...] = (_dot(_gelu_tanh(h).astype(CDT), w2_ref[...]) + b2_ref[...]).astype(o_ref.dtype)


def _compress(t, pos, w1, b1, w2, b2):
    b, s, _ = t.shape
    nch = s // CMP_STRIDE
    xw = CMP_STRIDE * LANES
    x = t.reshape(b, nch, xw)
    eye = jnp.eye(NSA_KV_HEADS, dtype=F32)
    w1r = w1.reshape(CMP_BLOCK, HEAD_DIM, CMP_HIDDEN)

    def expand_w1(part):
        return jnp.einsum('tdj,kl->tkdlj', part, eye).reshape(xw, NSA_KV_HEADS * CMP_HIDDEN).astype(CDT)

    def expand_pos(part):
        return jnp.broadcast_to(part[:, None, :], (CMP_STRIDE, NSA_KV_HEADS, HEAD_DIM)).reshape(1, xw)

    w1a, w1b = expand_w1(w1r[:CMP_STRIDE]), expand_w1(w1r[CMP_STRIDE:])
    pa, pb = expand_pos(pos[:CMP_STRIDE]), expand_pos(pos[CMP_STRIDE:])
    b1e = jnp.tile(b1, NSA_KV_HEADS).reshape(1, -1)
    w2e = jnp.einsum('jd,kl->kjld', w2, eye).reshape(NSA_KV_HEADS * CMP_HIDDEN, LANES).astype(CDT)
    b2e = jnp.tile(b2, NSA_KV_HEADS).reshape(1, -1)
    consts = [pa, pb, w1a, w1b, b1e, w2e, b2e]
    return pl.pallas_call(
        _compress_kernel,
        grid=(b,),
        in_specs=[pl.BlockSpec((None, nch, xw), lambda bi: (bi, 0, 0))]
        + [pl.BlockSpec(c.shape, lambda bi: (0, 0)) for c in consts],
        out_specs=pl.BlockSpec((None, nch, LANES), lambda bi: (bi, 0, 0)),
        out_shape=jax.ShapeDtypeStruct((b, nch, LANES), CDT),
        compiler_params=_cparams(("parallel",), 12),
        name="nsa_compress",
    )(x, *consts)


CMP_ROWS = 16


def _cmp_select_kernel(q_ref, kc_ref, vc_ref, ov_ref, o_ref, sel_ref, s_ref, p_ref, hi_ref, lo_ref,
                       *, tq, n_sel, tile0):
    i = pl.program_id(1) + tile0
    ncp = kc_ref.shape[0]
    ns = ov_ref.shape[1]
    lo = _half_masks()
    t_col = i * tq + lax.broadcasted_iota(jnp.int32, (tq, 1), 0)
    blk = lax.broadcasted_iota(jnp.int32, (tq, ns), 1)
    blk_t = lax.broadcasted_iota(jnp.int32, (ns, tq), 0)
    cur = t_col // SEL_BLOCK
    causal = blk <= cur
    forced = (blk == 0) | (blk == cur) | (blk == cur - 1)
    rb = CMP_ROWS
    c_end = lax.broadcasted_iota(jnp.int32, (rb, ncp), 1) * CMP_STRIDE + (CMP_BLOCK - 1)
    o_kv, work_t = [], []
    for kv, hm in enumerate((lo, jnp.logical_not(lo))):
        qs = jnp.concatenate(
            [(jnp.where(hm, q_ref[:, p * LANES:(p + 1) * LANES], 0) * QK_SCALE).astype(CDT) for p in range(3)], axis=0)
        s_ref[...] = _nt_dot(qs, kc_ref[...])
        for r0 in range(0, tq, rb):
            t_rows = i * tq + r0 + lax.broadcasted_iota(jnp.int32, (rb, 1), 0)
            vis_bias = jnp.where(c_end <= t_rows, 0.0, NEG_INF)
            has_visible = t_rows >= CMP_BLOCK - 1
            psum = jnp.zeros((rb, ncp), F32)
            for h in range(3):
                rows = slice(h * tq + r0, h * tq + r0 + rb)
                s = s_ref[rows, :] + vis_bias
                e = jnp.exp(s - jnp.max(s, axis=-1, keepdims=True))
                inv = jnp.where(has_visible, 1.0 / jnp.maximum(jnp.sum(e, axis=-1, keepdims=True), 1e-30), 0.0)
                pn = e * inv
                p_ref[rows, :] = pn.astype(CDT)
                psum = psum + pn
            p_hi = psum.astype(CDT)
            hi_ref[r0:r0 + rb, :] = p_hi
            lo_ref[r0:r0 + rb, :] = (psum - p_hi.astype(F32)).astype(CDT)
        o = _dot(p_ref[...], vc_ref[...])
        o_kv.append([o[p * tq:(p + 1) * tq] for p in range(3)])
        imp = _dot(hi_ref[...], ov_ref[...]) + _dot(lo_ref[...], ov_ref[...])
        work_t.append(jnp.where(causal & jnp.logical_not(forced), imp, -BIG).T)

    blk_lanes = blk_t[:, 0:LANES]

    def pick(_, work):
        m = jnp.max(work, axis=0, keepdims=True)
        idx = jnp.min(jnp.where(work == m, blk_lanes, ns), axis=0, keepdims=True)
        return jnp.where(blk_lanes == idx, -2.0 * BIG, work)

    for kv, start in enumerate(work_t):
        done = jnp.concatenate([lax.fori_loop(0, n_sel - 3, pick, start[:, c0:c0 + LANES], unroll=True)
                                for c0 in range(0, tq, LANES)], axis=1)
        taken = jnp.where((done < -BIG) & (start > -BIG), 1.0, 0.0).T
        selb = jnp.where(forced | (taken > 0.5), 0.0, NEG_INF)
        sel_ref[:, kv * ns:(kv + 1) * ns] = selb.astype(sel_ref.dtype)
    for p in range(3):
        o_ref[:, p * LANES:(p + 1) * LANES] = jnp.where(lo, o_kv[0][p], o_kv[1][p]).astype(o_ref.dtype)


CMP_CAUSAL_SPLITS = 4


def _cmp_select(zr, qcol, kc, vc, tq=256):
    b, s, _ = zr.shape
    ncp = kc.shape[1]
    ns = s // SEL_BLOCK
    n_sel = min(N_SEL, ns)
    assert n_sel >= 3, "selection needs room for the three forced blocks"
    tq = min(tq, s)
    c = np.arange(ncp)[:, None] * CMP_STRIDE
    j = np.arange(ns)[None, :] * SEL_BLOCK
    overlap = ((c < j + SEL_BLOCK) & (c + CMP_BLOCK - 1 >= j)).astype(np.float32)
    overlap[ncp - 1:, :] = 0.0
    ov = jnp.asarray(overlap, CDT)
    qw = 3 * LANES
    n_split = CMP_CAUSAL_SPLITS if (s // tq) % CMP_CAUSAL_SPLITS == 0 and ncp % (16 * CMP_CAUSAL_SPLITS) == 0 else 1
    tiles = s // tq // n_split
    outs, sels = [], []
    for part in range(n_split):
        tile0 = part * tiles
        ncp_part = ncp * (part + 1) // n_split
        o_part, sel_part = pl.pallas_call(
            functools.partial(_cmp_select_kernel, tq=tq, n_sel=n_sel, tile0=tile0),
            grid=(b, tiles),
            in_specs=[pl.BlockSpec((None, tq, qw), lambda bi, i, tile0=tile0: (bi, i + tile0, qcol)),
                      pl.BlockSpec((None, ncp_part, LANES), lambda bi, i: (bi, 0, 0)),
                      pl.BlockSpec((None, ncp_part, LANES), lambda bi, i: (bi, 0, 0)),
                      pl.BlockSpec((ncp_part, ns), lambda bi, i: (0, 0))],
            out_specs=[pl.BlockSpec((None, tq, qw), lambda bi, i: (bi, i, 0)),
                       pl.BlockSpec((None, tq, 2 * ns), lambda bi, i: (bi, i, 0))],
            out_shape=[jax.ShapeDtypeStruct((b, tiles * tq, qw), CDT),
                       jax.ShapeDtypeStruct((b, tiles * tq, 2 * ns), CDT)],
            scratch_shapes=[pltpu.VMEM((3 * tq, ncp_part), F32), pltpu.VMEM((3 * tq, ncp_part), CDT),
                            pltpu.VMEM((tq, ncp_part), CDT), pltpu.VMEM((tq, ncp_part), CDT)],
            compiler_params=_cparams(("parallel", "parallel"), 12),
            name=f"nsa_cmp_select_p{part}",
        )(zr, kc, vc, ov)
        outs.append(o_part)
        sels.append(sel_part)
    return jnp.concatenate(outs, axis=1), jnp.concatenate(sels, axis=1)


SLC_KEY_CHUNK = 256


def _slc_kernel(q_ref, k0_ref, k1_ref, vt_ref, selb_ref, o_ref, s_ref, e_ref, m_ref, l_ref, a_ref, acc_ref,
                *, tq, tk):
    i = pl.program_id(1)
    ns = selb_ref.shape[1] // 2
    bpt = tk // SEL_BLOCK
    n_q = 3 * tq
    rc = min(SLC_KEY_CHUNK, tk)
    lo = _half_masks()
    n_tiles = ((i + 1) * tq + tk - 1) // tk
    p_row = lax.broadcasted_iota(jnp.int32, (ns, LANES), 0)
    p_col = lax.broadcasted_iota(jnp.int32, (ns, LANES), 1)
    groups = []
    for kv, (hm, k_ref) in enumerate(((lo, k0_ref), (jnp.logical_not(lo), k1_ref))):
        q3 = [(jnp.where(hm, q_ref[:, p * LANES:(p + 1) * LANES], 0) * QK_SCALE).astype(CDT) for p in range(3)]
        selb = selb_ref[:, kv * ns:(kv + 1) * ns]
        lane0 = HEAD_DIM if kv == 0 else 0
        groups.append((kv, hm, k_ref, q3, selb, lane0))
    m_ref[...] = jnp.full(m_ref.shape, NEG_INF, F32)
    l_ref[...] = jnp.zeros(l_ref.shape, F32)
    acc_ref[...] = jnp.zeros(acc_ref.shape, F32)

    def scores(j, nk):
        k0 = pl.multiple_of(j * tk, tk)
        for kv, hm, k_ref, q3, selb, lane0 in groups:
            place = ((p_col >= lane0) & (p_col < lane0 + bpt) & (p_row == p_col - lane0 + j * bpt)).astype(CDT)
            sb = _dot(selb, place).astype(CDT)
            qp = jnp.concatenate([jnp.where(hm, q, sb) for q in q3], axis=0)
            s_ref[kv, 0:nk, :] = _nt_dot(k_ref[pl.ds(k0, nk), :], qp)

    def softmax_pv(j, diagonal, nk):
        k0 = pl.multiple_of(j * tk, tk)
        for kv in range(2):
            for c0 in range(0, n_q, LANES):
                cols = slice(c0, c0 + LANES)
                t_lane = i * tq + (c0 % tq) + lax.broadcasted_iota(jnp.int32, (1, LANES), 1)

                def chunk(r0):
                    s = s_ref[kv, r0:r0 + rc, cols]
                    if diagonal:
                        kpos = k0 + r0 + lax.broadcasted_iota(jnp.int32, (rc, LANES), 0)
                        s = jnp.where(kpos <= t_lane, s, NEG_INF)
                    return s

                m8 = m_ref[kv, :, cols]
                for r0 in range(0, nk, rc):
                    m8 = jnp.maximum(m8, jnp.max(chunk(r0).reshape(rc // 8, 8, LANES), axis=0))
                m_new = jnp.max(m8, axis=0, keepdims=True)
                alpha = jnp.exp(m_ref[kv, :, cols] - m_new)
                l8 = jnp.zeros((8, LANES), F32)
                for r0 in range(0, nk, rc):
                    e = jnp.exp(chunk(r0) - m_new)
                    l8 = l8 + jnp.sum(e.reshape(rc // 8, 8, LANES), axis=0)
                    e_ref[kv, r0:r0 + rc, cols] = e.astype(CDT)
                l_ref[kv, :, cols] = alpha * l_ref[kv, :, cols] + jnp.sum(l8, axis=0, keepdims=True)
                m_ref[kv, :, cols] = jnp.broadcast_to(m_new, (8, LANES))
                a_ref[kv, :, cols] = alpha
        for kv in range(2):
            acc_ref[kv] = a_ref[kv, 0:1, :] * acc_ref[kv] + _dot(vt_ref[j, :, 0:nk], e_ref[kv, 0:nk, :])

    def tile(j, diagonal, nk=tk):
        scores(j, nk)
        softmax_pv(j, diagonal, nk)

    lax.fori_loop(0, n_tiles - 1, lambda j, c: (tile(j, False), c)[1], 0)
    last = n_tiles - 1
    half = tk // 2
    first_half_only = (i + 1) * tq - last * tk <= half

    @pl.when(first_half_only)
    def _():
        tile(last, True, half)

    @pl.when(jnp.logical_not(first_half_only))
    def _():
        tile(last, True)
    o0 = (acc_ref[0] / l_ref[0, 0:1, :]).T
    o1 = (acc_ref[1] / l_ref[1, 0:1, :]).T
    for p in range(3):
        o_ref[:, p * LANES:(p + 1) * LANES] = jnp.where(
            lo, o0[p * tq:(p + 1) * tq], o1[p * tq:(p + 1) * tq]).astype(o_ref.dtype)


def _slc(zr, qcol, ksc, vt, selb, tq=512):
    b, s, _ = zr.shape
    tq = min(tq, s)
    tk = vt.shape[2]
    assert tq % LANES == 0 and tk // SEL_BLOCK <= HEAD_DIM and s % tk == 0
    qw = 3 * LANES
    ns2 = selb.shape[2]
    pat = jax.nn.one_hot((jnp.arange(s) // SEL_BLOCK) % (tk // SEL_BLOCK), HEAD_DIM, dtype=ksc.dtype)
    pat = jnp.broadcast_to(pat[None], (b, s, HEAD_DIM))
    k0 = jnp.concatenate([ksc[..., :HEAD_DIM], pat], axis=-1)
    k1 = jnp.concatenate([pat, ksc[..., HEAD_DIM:]], axis=-1)
    full = lambda bi, i: (bi, 0, 0)
    return pl.pallas_call(
        functools.partial(_slc_kernel, tq=tq, tk=tk),
        grid=(b, s // tq),
        in_specs=[pl.BlockSpec((None, tq, qw), lambda bi, i: (bi, i, qcol)),
                  _single_buffered((None, s, LANES), full),
                  _single_buffered((None, s, LANES), full),
                  _single_buffered((s // tk, LANES, tk), lambda bi, i: (bi, 0, 0)),
                  pl.BlockSpec((None, tq, ns2), lambda bi, i: (bi, i, 0))],
        out_specs=pl.BlockSpec((None, tq, qw), lambda bi, i: (bi, i, 0)),
        out_shape=jax.ShapeDtypeStruct((b, s, qw), CDT),
        scratch_shapes=[pltpu.VMEM((2, tk, 3 * tq), F32), pltpu.VMEM((2, tk, 3 * tq), CDT),
                        pltpu.VMEM((2, 8, 3 * tq), F32), pltpu.VMEM((2, 8, 3 * tq), F32),
                        pltpu.VMEM((2, 8, 3 * tq), F32), pltpu.VMEM((2, LANES, 3 * tq), F32)],
        compiler_params=_cparams(("parallel", "arbitrary"), 14),
        name="nsa_slc",
    )(zr, k0, k1, vt, selb)


def _merge_kernel(x_ref, oa0, oa1, oa2, la0, la1, la2, ob, ocmp, oslc, owin, od,
                  wn_ref, wg_ref, bg_ref, wb_ref, wo_ref, g_ref, b_ref, o_ref, *scratch, alpha):
    x = x_ref[...]
    xb = x.astype(CDT)
    bw = BRANCH_WIDTH

    def token_rows(src_ref, dst_ref):
        dil = src_ref.shape[1] // bw
        if dil == 1:
            return src_ref[...].astype(F32)
        n_cb = bw // LANES
        for r in range(dil):
            for cb in range(n_cb):
                c0 = r * bw + cb * LANES
                dst_ref[cb, pl.ds(r, src_ref.shape[0], stride=dil), :] = src_ref[:, c0:c0 + LANES].astype(F32)
        return jnp.concatenate([dst_ref[cb] for cb in range(n_cb)], axis=1)

    o0, o1, o2 = (token_rows(s, d) for s, d in zip((oa0, oa1, oa2), scratch[0:3]))
    l0, l1, l2 = (token_rows(s, d) for s, d in zip((la0, la1, la2), scratch[3:6]))
    mx = jnp.maximum(jnp.maximum(l0, l1), l2)
    w0, w1, w2 = jnp.exp(l0 - mx), jnp.exp(l1 - mx), jnp.exp(l2 - mx)
    o_a = (w0 * o0 + w1 * o1 + w2 * o2) / (w0 + w1 + w2)
    gates = jax.nn.sigmoid(_dot(xb, wn_ref[...]))
    o_c = (gates[:, 0:bw] * ocmp[...].astype(F32) + gates[:, bw:2 * bw] * oslc[...].astype(F32)
           + gates[:, 2 * bw:3 * bw] * owin[...].astype(F32))
    branches = (o_a.astype(CDT), ob[...], o_c.astype(CDT), od[...])
    d = x.shape[1]
    merged = jnp.zeros(x.shape, F32)
    for m in range(N_BRANCH):
        gate = jax.nn.sigmoid(_dot(xb, wg_ref[:, m * d:(m + 1) * d]) + bg_ref[:, m * d:(m + 1) * d])
        merged = merged + gate * _dot(branches[m], wb_ref[m])
    r = alpha * x + _dot(merged.astype(CDT), wo_ref[...])
    o_ref[...] = _layer_norm(r, g_ref[...], b_ref[...])


def _merge(x2d, branch_inputs, wn, wg, bg, wb, wo, g, b, alpha, tm=256):
    t, d = x2d.shape
    tm = min(tm, t)
    row = lambda i: (i, 0)
    const2 = lambda i: (0, 0)
    in_specs = [pl.BlockSpec((tm, d), row)]
    in_specs += [pl.BlockSpec((tm * a.shape[0] // t, a.shape[1]), row) for a in branch_inputs]
    in_specs += [pl.BlockSpec(wn.shape, const2), pl.BlockSpec(wg.shape, const2), pl.BlockSpec(bg.shape, const2),
                 pl.BlockSpec(wb.shape, lambda i: (0, 0, 0)), pl.BlockSpec(wo.shape, const2),
                 pl.BlockSpec(g.shape, const2), pl.BlockSpec(b.shape, const2)]
    return pl.pallas_call(
        functools.partial(_merge_kernel, alpha=alpha),
        grid=(t // tm,),
        in_specs=in_specs,
        out_specs=pl.BlockSpec((tm, d), row),
        out_shape=jax.ShapeDtypeStruct((t, d), F32),
        scratch_shapes=[pltpu.VMEM((BRANCH_WIDTH // LANES, tm, LANES), F32) for _ in range(2 * N_DIL)],
        compiler_params=_cparams(("parallel",), 14),
        name="merge_ln",
    )(x2d, *branch_inputs, wn, wg, bg, wb, wo, g, b)


def _ple_ln(x, xb, f, p, plw_ref, pgw_ref, pgb_ref, g_ref, b_ref, alpha):
    ple = jax.nn.sigmoid(_dot(xb, pgw_ref[...]) + pgb_ref[...]) * _dot(p.astype(CDT), plw_ref[...])
    return _layer_norm(alpha * x + f + ple, g_ref[...], b_ref[...])


FFN_CHUNK = 512


def _ffn_kernel(x_ref, p_ref, wg_ref, wu_ref, wd_ref, plw_ref, pgw_ref, pgb_ref, g_ref, b_ref, o_ref, h_ref, *, alpha):
    x = x_ref[...]
    xb = x.astype(CDT)
    dff = wg_ref.shape[1]
    for c0 in range(0, dff, FFN_CHUNK):
        cols = slice(c0, min(c0 + FFN_CHUNK, dff))
        h_ref[:, cols] = (jax.nn.silu(_dot(xb, wg_ref[:, cols])) * _dot(xb, wu_ref[:, cols])).astype(CDT)
    f = _dot(h_ref[...], wd_ref[...])
    o_ref[...] = _ple_ln(x, xb, f, p_ref[...], plw_ref, pgw_ref, pgb_ref, g_ref, b_ref, alpha)


def _ffn(x2d, p2d, wg, wu, wd, plw, pgw, pgb, g, b, alpha, tm=512):
    t, d = x2d.shape
    tm = min(tm, t)
    dff = wg.shape[1]
    wg, wu, wd = wg.astype(CDT), wu.astype(CDT), wd.astype(CDT)
    row = lambda i: (i, 0)
    const = lambda shape: pl.BlockSpec(shape, lambda i: (0, 0), pipeline_mode=pl.Buffered(1))
    return pl.pallas_call(
        functools.partial(_ffn_kernel, alpha=alpha),
        grid=(t // tm,),
        in_specs=[pl.BlockSpec((tm, d), row), pl.BlockSpec((tm, p2d.shape[1]), row),
                  const(wg.shape), const(wu.shape), const(wd.shape),
                  const(plw.shape), const(pgw.shape), const(pgb.shape), const(g.shape), const(b.shape)],
        out_specs=pl.BlockSpec((tm, d), row),
        out_shape=jax.ShapeDtypeStruct((t, d), F32),
        scratch_shapes=[pltpu.VMEM((tm, dff), CDT)],
        compiler_params=_cparams(("parallel",), 12),
        name="ffn_ple_ln",
    )(x2d, p2d, wg, wu, wd, plw, pgw, pgb, g, b)


def _router_kernel(x_ref, wh_ref, wl_ref, b_ref, comb_ref, rank_ref, rank_t_ref, cnt_ref):
    x = x_ref[...]
    xh = x.astype(CDT)
    xl = (x - xh.astype(F32)).astype(CDT)
    logits = _dot(xh, wh_ref[...]) + _dot(xh, wl_ref[...]) + _dot(xl, wh_ref[...]) + b_ref[...]
    lane = lax.broadcasted_iota(jnp.int32, logits.shape, 1)
    v1 = jnp.max(logits, axis=-1, keepdims=True)
    i1 = jnp.min(jnp.where(logits == v1, lane, LANES), axis=-1, keepdims=True)
    rest = jnp.where(lane == i1, -jnp.inf, logits)
    v2 = jnp.max(rest, axis=-1, keepdims=True)
    i2 = jnp.min(jnp.where(rest == v2, lane, LANES), axis=-1, keepdims=True)
    e2 = jnp.exp(v2 - v1)
    comb_ref[...] = jnp.where(lane == i1, 1.0 / (1.0 + e2), 0.0) + jnp.where(lane == i2, e2 / (1.0 + e2), 0.0)
    routed = (lane == i1) | (lane == i2)
    mask = routed.astype(CDT)
    tm = x.shape[0]
    before = (lax.broadcasted_iota(jnp.int32, (tm, tm), 1) < lax.broadcasted_iota(jnp.int32, (tm, tm), 0)).astype(CDT)
    rank = jnp.where(routed, _dot(before, mask), -1.0)
    rank_ref[...] = rank
    rank_t_ref[...] = rank.T[0:rank_t_ref.shape[0], :]
    cnt_ref[...] = jnp.sum(routed.astype(F32), axis=0, keepdims=True).astype(jnp.int32)


def _router(x2d, w_router, b_router, tm):
    t, d = x2d.shape
    ne = w_router.shape[1]
    wp = jnp.zeros((d, LANES), F32).at[:, :ne].set(w_router)
    wh = wp.astype(CDT)
    wl = (wp - wh.astype(F32)).astype(CDT)
    bp = jnp.full((1, LANES), -BIG, F32).at[0, :ne].set(b_router)
    nt = t // tm
    row = lambda i: (i, 0)
    return pl.pallas_call(
        _router_kernel,
        grid=(nt,),
        in_specs=[pl.BlockSpec((tm, d), row), pl.BlockSpec(wh.shape, lambda i: (0, 0)),
                  pl.BlockSpec(wl.shape, lambda i: (0, 0)), pl.BlockSpec(bp.shape, lambda i: (0, 0))],
        out_specs=[pl.BlockSpec((tm, LANES), row), pl.BlockSpec((tm, LANES), row),
                   pl.BlockSpec((None, 8, tm), lambda i: (i, 0, 0)),
                   pl.BlockSpec((None, 1, LANES), lambda i: (i, 0, 0))],
        out_shape=[jax.ShapeDtypeStruct((t, LANES), F32), jax.ShapeDtypeStruct((t, LANES), F32),
                   jax.ShapeDtypeStruct((nt, 8, tm), F32), jax.ShapeDtypeStruct((nt, 1, LANES), jnp.int32)],
        compiler_params=_cparams(("parallel",), 10),
        name="moe_router",
    )(x2d, wh, wl, bp)


def _moe_kernel(cnt_ref, x_ref, comb_ref, rank_ref, rank_t_ref, p_ref, wg_ref, wu_ref, wd_ref, plw_ref, pgw_ref,
                pgb_ref, g_ref, b_ref, o_ref, xe_ref, ye_ref, *, alpha, rs, seg):
    i = pl.program_id(0)
    e = pl.program_id(1)
    c = pl.program_id(2)
    n_seg = x_ref.shape[0] // seg
    last_chunk = c == pl.num_programs(2) - 1

    first = slice(0, rs)
    toks = [slice(sg * seg, (sg + 1) * seg) for sg in range(n_seg)]
    n_groups = [(cnt_ref[(i * n_seg + sg) * LANES + e] + rs - 1) // rs for sg in range(n_seg)]

    def later(sc):
        return pl.ds(pl.multiple_of(sc * rs, 8), rs), (sc * rs).astype(F32)

    @pl.when((e == 0) & (c == 0))
    def _():
        o_ref[...] = jnp.zeros_like(o_ref)

    @pl.when(c == 0)
    def _():
        row_id = lax.broadcasted_iota(jnp.int32, (rs, seg), 0).astype(F32)
        xbs = [x_ref[tok, :].astype(CDT) for tok in toks]
        rank_rows = [rank_t_ref[sg, pl.ds(e, 1), :] for sg in range(n_seg)]

        def gather(sg, rws, base):
            onehot = (rank_rows[sg] - base == row_id).astype(CDT)
            xe_ref[sg, rws, :] = _dot(onehot, xbs[sg]).astype(CDT)
            ye_ref[sg, rws, :] = jnp.zeros((rs, ye_ref.shape[2]), F32)

        for sg in range(n_seg):
            gather(sg, first, 0.0)
        for sg in range(n_seg):
            lax.fori_loop(1, n_groups[sg], lambda sc, _, sg=sg: (gather(sg, *later(sc)), 0)[1], 0)

    def expert(sg, rws):
        xs = xe_ref[sg, rws, :]
        h = jax.nn.silu(_dot(xs, wg_ref[0])) * _dot(xs, wu_ref[0])
        ye_ref[sg, rws, :] += _dot(h.astype(CDT), wd_ref[0])

    for sg in range(n_seg):
        expert(sg, first)
    for sg in range(n_seg):
        lax.fori_loop(1, n_groups[sg], lambda sc, _, sg=sg: (expert(sg, later(sc)[0]), 0)[1], 0)

    @pl.when(last_chunk)
    def _():
        lane = lax.broadcasted_iota(jnp.int32, (seg, LANES), 1)
        mine = lane == e
        col_id = lax.broadcasted_iota(jnp.int32, (seg, rs), 1).astype(F32)
        cws = [jnp.sum(jnp.where(mine, comb_ref[tok, :], 0.0), axis=-1, keepdims=True) for tok in toks]
        rank_cols = [jnp.sum(jnp.where(mine, rank_ref[tok, :], 0.0), axis=-1, keepdims=True) for tok in toks]

        def scatter(sg, rws, base):
            onehot = (rank_cols[sg] - base == col_id).astype(CDT)
            o_ref[toks[sg], :] += cws[sg] * _dot(onehot, ye_ref[sg, rws, :].astype(CDT))

        for sg in range(n_seg):
            scatter(sg, first, 0.0)
        for sg in range(n_seg):
            lax.fori_loop(1, n_groups[sg], lambda sc, _, sg=sg: (scatter(sg, *later(sc)), 0)[1], 0)

    @pl.when((e == pl.num_programs(1) - 1) & last_chunk)
    def _():
        for sg in range(n_seg):
            tok = slice(sg * seg, (sg + 1) * seg)
            x = x_ref[tok, :]
            o_ref[tok, :] = _ple_ln(x, x.astype(CDT), o_ref[tok, :], p_ref[tok, :], plw_ref, pgw_ref, pgb_ref,
                                    g_ref, b_ref, alpha)


MOE_CHUNK = 512
MOE_SEGMENT = 1024
MOE_SEGMENTS_PER_TILE = 2
MOE_ROW_GROUP = 288


def _single_buffered(shape, index_map):
    return pl.BlockSpec(shape, index_map, pipeline_mode=pl.Buffered(1))


def _moe(x2d, routing, p2d, wg, wu, wd, plw, pgw, pgb, g, b, alpha, seg):
    comb, rank, rank_t, cnt = routing
    t, d = x2d.shape
    ne, _, dff = wg.shape
    ck = min(MOE_CHUNK, dff)
    rs = min(MOE_ROW_GROUP, seg)
    n_seg = min(MOE_SEGMENTS_PER_TILE, t // seg)
    tm = n_seg * seg
    max_rows = -(-seg // rs) * rs
    wg, wu, wd = wg.astype(CDT), wu.astype(CDT), wd.astype(CDT)
    row = lambda i, e, c, cnt: (i, 0)
    c2 = lambda i, e, c, cnt: (0, 0)
    grid_spec = pltpu.PrefetchScalarGridSpec(
        num_scalar_prefetch=1,
        grid=(t // tm, ne, dff // ck),
        in_specs=[_single_buffered((tm, d), row), _single_buffered((tm, LANES), row),
                  _single_buffered((tm, LANES), row),
                  _single_buffered((n_seg, 8, seg), lambda i, e, c, cnt: (i, 0, 0)),
                  _single_buffered((tm, p2d.shape[1]), row),
                  pl.BlockSpec((1, d, ck), lambda i, e, c, cnt: (e, 0, c)),
                  pl.BlockSpec((1, d, ck), lambda i, e, c, cnt: (e, 0, c)),
                  pl.BlockSpec((1, ck, d), lambda i, e, c, cnt: (e, c, 0)),
                  _single_buffered(plw.shape, c2), _single_buffered(pgw.shape, c2), _single_buffered(pgb.shape, c2),
                  _single_buffered(g.shape, c2), _single_buffered(b.shape, c2)],
        out_specs=pl.BlockSpec((tm, d), row),
        scratch_shapes=[pltpu.VMEM((n_seg, max_rows, d), CDT), pltpu.VMEM((n_seg, max_rows, d), F32)],
    )
    return pl.pallas_call(
        functools.partial(_moe_kernel, alpha=alpha, rs=rs, seg=seg),
        grid_spec=grid_spec,
        out_shape=jax.ShapeDtypeStruct((t, d), F32),
        compiler_params=_cparams(("parallel", "arbitrary", "arbitrary"), 15),
        name="moe_ple_ln",
    )(cnt.reshape(-1), x2d, comb, rank, rank_t, p2d, wg, wu, wd, plw, pgw, pgb, g, b)


def _prep_in_weights(w_in):
    o = COL_OFF
    bw = BRANCH_WIDTH
    cols = lambda n: w_in[:, o[n]:o[n + 1]]
    qa, ka, va = cols(0), cols(1), cols(2)
    w_dil = jnp.concatenate(
        [t[:, g * bw:(g + 1) * bw] for g in range(N_DIL) for t in (qa, ka, va)], axis=1).astype(CDT)
    w_conv = jnp.concatenate([cols(3), cols(4), cols(5)], axis=1).astype(CDT)
    gn = cols(13)
    w_gate = jnp.concatenate([gn[:, br * NSA_Q_HEADS + GQA_COL_HEAD] for br in range(3)], axis=1).astype(CDT)
    w_rest = jnp.concatenate([cols(6)[:, GQA_COL_PERM], cols(14)[:, GQA_COL_PERM], cols(11),
                              cols(12), cols(15), cols(16), cols(9), cols(7), cols(8)], axis=1).astype(CDT)
    wt_vsc = cols(10).T.astype(CDT)
    return w_dil, w_conv, w_gate, w_rest, wt_vsc


ZR_Q_NSA, ZR_Q_SWA = 0, 1
ZR_KWC, ZR_VWC, ZR_KD, ZR_VD = 6, 7, 8, 9
ZR_WIDTH = 2 * BRANCH_WIDTH + 4 * LANES
SLC_KEY_TILE = 1024


def _token_mixers(x, w_in, conv_w, cmp_pos, cmp_w1, cmp_b1, cmp_w2, cmp_b2, sinks):
    b, s, d = x.shape
    x2d = x.reshape(b * s, d)
    w_dil, w_conv, w_gate, w_rest, wt_vsc = _prep_in_weights(w_in)
    gw = 3 * BRANCH_WIDTH

    z_dil = _linear(x2d, w_dil, [(g * gw, (g + 1) * gw) for g in range(N_DIL)], gw, "in_proj_dil",
                    dils=[dil for _, dil in DIL_PATTERNS])
    zr, ksc, kcc, vcc = _linear(x2d, w_rest, [(0, ZR_WIDTH)] + [(ZR_WIDTH + n * LANES, ZR_WIDTH + (n + 1) * LANES)
                                                           for n in range(3)], 256, "in_proj_rest")
    zr = zr.reshape(b, s, ZR_WIDTH)
    o_b = _conv(x, w_conv, conv_w)

    dil_o, dil_lse = [], []
    for g, (window, dil) in enumerate(DIL_PATTERNS):
        view = z_dil[g].reshape(b, s // dil, dil * gw)
        og, lg = _banded(view, view, view, nrep=dil,
                         qcol=lambda r: 3 * r, kcol=lambda r: 3 * r + 1, vcol=lambda r: 3 * r + 2,
                         kw=3 * LANES, window=window // dil, want_lse=True)
        dil_o.append(og.reshape(b * s // dil, dil * BRANCH_WIDTH))
        dil_lse.append(lg.reshape(b * s // dil, dil * BRANCH_WIDTH))

    kc = _compress(kcc.reshape(b, s, LANES), cmp_pos[0], cmp_w1[0], cmp_b1[0], cmp_w2[0], cmp_b2[0])
    vc = _compress(vcc.reshape(b, s, LANES), cmp_pos[1], cmp_w1[1], cmp_b1[1], cmp_w2[1], cmp_b2[1])
    o_cmp, selb = _cmp_select(zr, ZR_Q_NSA, kc, vc)
    vsc_t = _linear_t(x2d, wt_vsc, min(SLC_KEY_TILE, s), "in_proj_vsc_t")
    o_slc = _slc(zr, ZR_Q_NSA, ksc.reshape(b, s, LANES), vsc_t, selb)
    (o_win,) = _banded(zr, zr, zr, nrep=1, qcol=lambda r: ZR_Q_NSA, kcol=lambda r: ZR_KWC, vcol=lambda r: ZR_VWC,
                       kw=LANES, window=NSA_WINDOW - 1, want_lse=False, tq=512)

    sink_row = sinks.astype(F32)[GQA_COL_HEAD].reshape(1, BRANCH_WIDTH)
    (o_d,) = _banded(zr, zr, zr, nrep=1, qcol=lambda r: ZR_Q_SWA, kcol=lambda r: ZR_KD, vcol=lambda r: ZR_VD,
                     kw=LANES, window=SWA_WINDOW - 1, want_lse=False, sink_row=sink_row)

    t = b * s
    flat = lambda a: a.reshape(t, a.shape[-1])
    return [dil_o[0], dil_o[1], dil_o[2], dil_lse[0], dil_lse[1], dil_lse[2], flat(o_b), flat(o_cmp), flat(o_slc),
            flat(o_win), flat(o_d)], w_gate


def kernel(x, p, w_in, conv_w, cmp_pos, cmp_w1, cmp_b1, cmp_w2, cmp_b2, sinks, w_branch, w_merge_gate, b_merge_gate, w_out, ln_mix_g, ln_mix_b, ffn_w_gate, ffn_w_up, ffn_w_down, w_router, b_router, moe_w_gate, moe_w_up, moe_w_down, ple_w, ple_gate_w, ple_gate_b, ln_ffn_g, ln_ffn_b):
    depth, b, s, _ = p.shape
    d = x.shape[-1]
    t = b * s
    alpha = (2 * depth) ** 0.25
    row = lambda v: v.reshape(1, -1).astype(F32)
    for i in range(depth):
        branch_inputs, w_nsa_gate = _token_mixers(x, w_in[i], conv_w[i], cmp_pos[i], cmp_w1[i], cmp_b1[i], cmp_w2[i],
                                                  cmp_b2[i], sinks[i])
        wg = jnp.concatenate([w_merge_gate[i, m] for m in range(N_BRANCH)], axis=1).astype(CDT)
        bg = b_merge_gate[i].reshape(1, N_BRANCH * d).astype(F32)
        wb = jnp.stack([w_branch[i, 0], w_branch[i, 1], w_branch[i, 2][GQA_COL_PERM],
                        w_branch[i, 3][GQA_COL_PERM]]).astype(CDT)
        x1 = _merge(x.reshape(t, d), branch_inputs, w_nsa_gate, wg, bg, wb, w_out[i].astype(CDT),
                    row(ln_mix_g[i]), row(ln_mix_b[i]), alpha)
        p2d = p[i].reshape(t, -1)
        ple_args = (ple_w[i].astype(CDT), ple_gate_w[i].astype(CDT), row(ple_gate_b[i]),
                    row(ln_ffn_g[i]), row(ln_ffn_b[i]))
        j = i // 2
        if i % 2 == 0:
            x2 = _ffn(x1, p2d, ffn_w_gate[j], ffn_w_up[j], ffn_w_down[j], *ple_args, alpha)
        else:
            seg = min(MOE_SEGMENT, t)
            routing = _router(x1, w_router[j], b_router[j], seg)
            x2 = _moe(x1, routing, p2d, moe_w_gate[j], moe_w_up[j], moe_w_down[j], *ple_args, alpha, seg)
        x = x2.reshape(b, s, d)
    return x
```

```python
import functools

import numpy as np
import jax
import jax.numpy as jnp
from jax import lax
from jax.experimental import pallas as pl
from jax.experimental.pallas import tpu as pltpu

HEAD_DIM = 64
DIL_PATTERNS = ((128, 1), (512, 4), (2048, 16))
N_DIL = 3
DIL_HEADS = 6
CONV_WIDTH = 384
NSA_Q_HEADS = 6
NSA_KV_HEADS = 2
CMP_BLOCK = 32
CMP_STRIDE = 16
CMP_HIDDEN = 128
SEL_BLOCK = 64
N_SEL = 16
NSA_WINDOW = 512
SWA_Q_HEADS = 6
SWA_WINDOW = 128
BRANCH_WIDTH = 384
N_BRANCH = 4
LN_EPS = 1e-5
NEG_INF = -1e30
DIL_WIDTH = N_DIL * DIL_HEADS * HEAD_DIM
COLUMN_SIZES = (DIL_WIDTH, DIL_WIDTH, DIL_WIDTH, CONV_WIDTH, CONV_WIDTH, CONV_WIDTH,
                NSA_Q_HEADS * HEAD_DIM, 128, 128, 128, 128, 128, 128, 3 * NSA_Q_HEADS,
                SWA_Q_HEADS * HEAD_DIM, 128, 128)
COL_OFF = np.concatenate([[0], np.cumsum(COLUMN_SIZES)]).tolist()

LANES = 128
V7X_VMEM_BYTES = 64 * 1024 * 1024

CDT = jnp.bfloat16
F32 = jnp.float32
QK_SCALE = HEAD_DIM ** -0.5
SUB_Q = 256
BAND_ROWS = 64
BIG = 1e30

_GQA_HEAD_ORDER = (0, 3, 1, 4, 2, 5)
GQA_COL_PERM = np.concatenate([np.arange(h * HEAD_DIM, (h + 1) * HEAD_DIM) for h in _GQA_HEAD_ORDER])
GQA_COL_HEAD = GQA_COL_PERM // HEAD_DIM


def _cparams(sem, vmem_sixteenths):
    return pltpu.CompilerParams(dimension_semantics=sem, vmem_limit_bytes=V7X_VMEM_BYTES * vmem_sixteenths // 16)


def _nt_dot(a, b):
    return lax.dot_general(a, b, (((1,), (1,)), ((), ())), preferred_element_type=F32)


def _dot(a, b):
    return jnp.dot(a, b, preferred_element_type=F32)


def _layer_norm(r, g, b):
    mu = jnp.mean(r, axis=-1, keepdims=True)
    d = r - mu
    var = jnp.mean(d * d, axis=-1, keepdims=True)
    return d * lax.rsqrt(var + LN_EPS) * g + b


def _half_masks():
    lane = lax.broadcasted_iota(jnp.int32, (1, LANES), 1)
    return lane < HEAD_DIM


def _linear_kernel(x_ref, w_ref, *refs, splits, n_chunk, dils):
    o_refs, z_ref = refs[:len(splits)], refs[len(splits)]
    xb = x_ref[...].astype(CDT)
    tm = xb.shape[0]
    for o_ref, (c0, c1), dil in zip(o_refs, splits, dils):
        width = c1 - c0
        if dil == 1:
            for a in range(c0, c1, n_chunk):
                b = min(a + n_chunk, c1)
                o_ref[:, a - c0:b - c0] = _dot(xb, w_ref[:, a:b]).astype(o_ref.dtype)
        else:
            z = _dot(xb, w_ref[:, c0:c1])
            for cb in range(width // LANES):
                z_ref[cb] = z[:, cb * LANES:(cb + 1) * LANES]
            for r in range(dil):
                for cb in range(width // LANES):
                    o_ref[:, r * width + cb * LANES:r * width + (cb + 1) * LANES] = (
                        z_ref[cb, pl.ds(r, tm // dil, stride=dil), :].astype(o_ref.dtype))


def _linear(x2d, w, splits, n_chunk, name, dils=None, tm=512):
    t, k = x2d.shape
    tm = min(tm, t)
    n = w.shape[1]
    dils = tuple(dils) if dils is not None else (1,) * len(splits)
    widths = [c1 - c0 for c0, c1 in splits]
    assert all(tm % (16 * dl) == 0 for dl in dils)
    return pl.pallas_call(
        functools.partial(_linear_kernel, splits=tuple(splits), n_chunk=n_chunk, dils=dils),
        grid=(t // tm,),
        in_specs=[pl.BlockSpec((tm, k), lambda i: (i, 0)),
                  pl.BlockSpec((k, n), lambda i: (0, 0))],
        out_specs=[pl.BlockSpec((tm // dl, dl * wd), lambda i: (i, 0)) for wd, dl in zip(widths, dils)],
        out_shape=[jax.ShapeDtypeStruct((t // dl, dl * wd), CDT) for wd, dl in zip(widths, dils)],
        scratch_shapes=[pltpu.VMEM((max(widths) // LANES, tm, LANES), F32)],
        compiler_params=_cparams(("parallel",), 12),
        name=name,
    )(x2d, w)


def _linear_t_kernel(x_ref, wt_ref, o_ref):
    o_ref[...] = _nt_dot(wt_ref[...], x_ref[...].astype(CDT)).astype(o_ref.dtype)


def _linear_t(x2d, wt, tm, name):
    t, k = x2d.shape
    n = wt.shape[0]
    return pl.pallas_call(
        _linear_t_kernel,
        grid=(t // tm,),
        in_specs=[pl.BlockSpec((tm, k), lambda i: (i, 0)), pl.BlockSpec((n, k), lambda i: (0, 0))],
        out_specs=pl.BlockSpec((None, n, tm), lambda i: (i, 0, 0)),
        out_shape=jax.ShapeDtypeStruct((t // tm, n, tm), CDT),
        compiler_params=_cparams(("parallel",), 8),
        name=name,
    )(x2d, wt)


def _conv_kernel(x_ref, xh_ref, wc_ref, cw_ref, ob_ref, *, tm):
    i = pl.program_id(1)
    w = CONV_WIDTH
    xb = x_ref[...].astype(CDT)
    z = _dot(xb, wc_ref[...])
    u = z[:, w:2 * w] * z[:, 2 * w:3 * w]
    zh = _dot(xh_ref[...].astype(CDT), wc_ref[:, w:3 * w])
    uh = zh[:, :w] * zh[:, w:]
    uh = jnp.where(i == 0, 0.0, uh)
    row = lax.broadcasted_iota(jnp.int32, (tm, w), 0)
    u1 = jnp.where(row == 0, uh[7:8, :], pltpu.roll(u, 1, 0))
    u2 = jnp.where(row == 0, uh[6:7, :], jnp.where(row == 1, uh[7:8, :], pltpu.roll(u, 2, 0)))
    y = cw_ref[0:1, :] * u2 + cw_ref[1:2, :] * u1 + cw_ref[2:3, :] * u
    ob_ref[...] = (z[:, :w] * y).astype(ob_ref.dtype)


def _conv(x, wc, conv_w, tm=512):
    b, s, d = x.shape
    tm = min(tm, s)
    hb = tm // 8
    return pl.pallas_call(
        functools.partial(_conv_kernel, tm=tm),
        grid=(b, s // tm),
        in_specs=[pl.BlockSpec((None, tm, d), lambda bi, i: (bi, i, 0)),
                  pl.BlockSpec((None, 8, d), lambda bi, i: (bi, jnp.maximum(i * hb - 1, 0), 0)),
                  pl.BlockSpec(wc.shape, lambda bi, i: (0, 0)),
                  pl.BlockSpec(conv_w.shape, lambda bi, i: (0, 0))],
        out_specs=pl.BlockSpec((None, tm, CONV_WIDTH), lambda bi, i: (bi, i, 0)),
        out_shape=jax.ShapeDtypeStruct((b, s, CONV_WIDTH), CDT),
        compiler_params=_cparams(("parallel", "parallel"), 12),
        name="short_conv",
    )(x, x, wc, conv_w)


def _banded_kernel(*refs, window, pr, tq, sub_q, kw, want_lse, has_sink):
    SUB_Q = sub_q
    q_ref, kp_ref, kc_ref, vp_ref, vc_ref = refs[:5]
    n = 5
    sink_ref = None
    if has_sink:
        sink_ref = refs[n]
        n += 1
    o_ref = refs[n]
    n += 1
    lse_ref = None
    if want_lse:
        lse_ref = refs[n]
        n += 1
    kbuf, vbuf, s_ref, e_ref, m_ref, l_ref = refs[n:n + 6]

    i = pl.program_id(2)
    kbuf[0:pr, :] = kp_ref[...]
    kbuf[pr:pr + tq, :] = kc_ref[...]
    vbuf[0:pr, :] = vp_ref[...]
    vbuf[pr:pr + tq, :] = vc_ref[...]

    span = SUB_Q + pr
    qi = lax.broadcasted_iota(jnp.int32, (SUB_Q, span), 0)
    kj = lax.broadcasted_iota(jnp.int32, (SUB_Q, span), 1)
    dist = pr + qi - kj
    band = (dist >= 0) & (dist <= window)
    lo = _half_masks()
    halves = (lo, jnp.logical_not(lo))
    groups = ((0,), (1,), (2,)) if kw == 3 * LANES else ((0, 1, 2),)
    rb = BAND_ROWS
    for sb in range(tq // SUB_Q):
        r0 = sb * SUB_Q
        bias = jnp.where(band & (i * tq + r0 - pr + kj >= 0), 0.0, NEG_INF)
        for grp in groups:
            kc0 = grp[0] * LANES if kw == 3 * LANES else 0
            qs = jnp.concatenate(
                [(jnp.where(hm, q_ref[r0:r0 + SUB_Q, p * LANES:(p + 1) * LANES], 0) * QK_SCALE).astype(CDT)
                 for p in grp for hm in halves], axis=0)
            g0 = 2 * grp[0] * SUB_Q
            s_ref[g0:g0 + qs.shape[0], :] = _nt_dot(qs, kbuf[r0:r0 + span, kc0:kc0 + LANES])
        for c0 in range(0, 6 * SUB_Q, rb):
            rows = slice(c0, c0 + rb)
            s = s_ref[rows, :] + bias[c0 % SUB_Q:c0 % SUB_Q + rb, :]
            m = jnp.max(s, axis=-1, keepdims=True)
            e = jnp.exp(s - m)
            e_ref[rows, :] = e.astype(CDT)
            m_ref[rows, :] = jnp.broadcast_to(m, (rb, LANES))
            l_ref[rows, :] = jnp.broadcast_to(jnp.sum(e, axis=-1, keepdims=True), (rb, LANES))
        for grp in groups:
            kc0 = grp[0] * LANES if kw == 3 * LANES else 0
            g0 = 2 * grp[0] * SUB_Q
            g1 = g0 + 2 * len(grp) * SUB_Q
            l = l_ref[g0:g1, :]
            o = _dot(e_ref[g0:g1, :], vbuf[r0:r0 + span, kc0:kc0 + LANES]) / l
            lse = m_ref[g0:g1, :] + jnp.log(l)
            for n_p, p in enumerate(grp):
                a = 2 * n_p * SUB_Q
                o_pair = jnp.where(lo, o[a:a + SUB_Q], o[a + SUB_Q:a + 2 * SUB_Q])
                lse_pair = jnp.where(lo, lse[a:a + SUB_Q], lse[a + SUB_Q:a + 2 * SUB_Q])
                if has_sink:
                    o_pair = o_pair * jax.nn.sigmoid(lse_pair - sink_ref[:, p * LANES:(p + 1) * LANES])
                o_ref[r0:r0 + SUB_Q, p * LANES:(p + 1) * LANES] = o_pair.astype(o_ref.dtype)
                if want_lse:
                    lse_ref[r0:r0 + SUB_Q, p * LANES:(p + 1) * LANES] = lse_pair


def _banded(qa, ka, va, *, nrep, qcol, kcol, vcol, kw, window, want_lse, sink_row=None, tq=512):
    b, l, _ = qa.shape
    pr = -(-window // LANES) * LANES
    tq = min(max(tq, pr), l)
    sub_q = min(SUB_Q, tq)
    assert tq % pr == 0 and l % tq == 0 and tq % sub_q == 0, (tq, pr, l)
    ratio = tq // pr
    qw = 3 * LANES
    in_specs = [
        pl.BlockSpec((None, tq, qw), lambda bi, r, i: (bi, i, qcol(r))),
        pl.BlockSpec((None, pr, kw), lambda bi, r, i: (bi, jnp.maximum(i * ratio - 1, 0), kcol(r))),
        pl.BlockSpec((None, tq, kw), lambda bi, r, i: (bi, i, kcol(r))),
        pl.BlockSpec((None, pr, kw), lambda bi, r, i: (bi, jnp.maximum(i * ratio - 1, 0), vcol(r))),
        pl.BlockSpec((None, tq, kw), lambda bi, r, i: (bi, i, vcol(r))),
    ]
    args = [qa, ka, ka, va, va]
    if sink_row is not None:
        in_specs.append(pl.BlockSpec(sink_row.shape, lambda bi, r, i: (0, 0)))
        args.append(sink_row)
    out_specs = [pl.BlockSpec((None, tq, qw), lambda bi, r, i: (bi, i, r))]
    out_shape = [jax.ShapeDtypeStruct((b, l, nrep * qw), CDT)]
    if want_lse:
        out_specs.append(pl.BlockSpec((None, tq, qw), lambda bi, r, i: (bi, i, r)))
        out_shape.append(jax.ShapeDtypeStruct((b, l, nrep * qw), F32))
    res = pl.pallas_call(
        functools.partial(_banded_kernel, window=window, pr=pr, tq=tq, sub_q=sub_q, kw=kw, want_lse=want_lse,
                          has_sink=sink_row is not None),
        grid=(b, nrep, l // tq),
        in_specs=in_specs,
        out_specs=out_specs,
        out_shape=out_shape,
        scratch_shapes=[pltpu.VMEM((pr + tq, kw), ka.dtype), pltpu.VMEM((pr + tq, kw), va.dtype),
                        pltpu.VMEM((6 * sub_q, sub_q + pr), F32), pltpu.VMEM((6 * sub_q, sub_q + pr), CDT),
                        pltpu.VMEM((6 * sub_q, LANES), F32), pltpu.VMEM((6 * sub_q, LANES), F32)],
        compiler_params=_cparams(("parallel", "parallel", "parallel"), 8),
        name=f"banded_w{window}_k{kw}_r{nrep}",
    )(*args)
    return res


def _gelu_tanh(x):
    return 0.5 * x * (1.0 + jnp.tanh(0.7978845608028654 * (x + 0.044715 * (x * x * x))))


def _compress_kernel(x_ref, pa_ref, pb_ref, w1a_ref, w1b_ref, b1_ref, w2_ref, b2_ref, o_ref):
    x = x_ref[...].astype(F32)
    n = x.shape[0]
    a = _dot((x + pa_ref[...]).astype(CDT), w1a_ref[...])
    bm = _dot((x + pb_ref[...]).astype(CDT), w1b_ref[...])
    h = a + pltpu.roll(bm, n - 1, 0) + b1_ref[...]
    o_ref[...] = (_dot(_gelu_tanh(h).astype(CDT), w2_ref[...]) + b2_ref[...]).astype(o_ref.dtype)


def _compress(t, pos, w1, b1, w2, b2):
    b, s, _ = t.shape
    nch = s // CMP_STRIDE
    xw = CMP_STRIDE * LANES
    x = t.reshape(b, nch, xw)
    eye = jnp.eye(NSA_KV_HEADS, dtype=F32)
    w1r = w1.reshape(CMP_BLOCK, HEAD_DIM, CMP_HIDDEN)

    def expand_w1(part):
        return jnp.einsum('tdj,kl->tkdlj', part, eye).reshape(xw, NSA_KV_HEADS * CMP_HIDDEN).astype(CDT)

    def expand_pos(part):
        return jnp.broadcast_to(part[:, None, :], (CMP_STRIDE, NSA_KV_HEADS, HEAD_DIM)).reshape(1, xw)

    w1a, w1b = expand_w1(w1r[:CMP_STRIDE]), expand_w1(w1r[CMP_STRIDE:])
    pa, pb = expand_pos(pos[:CMP_STRIDE]), expand_pos(pos[CMP_STRIDE:])
    b1e = jnp.tile(b1, NSA_KV_HEADS).reshape(1, -1)
    w2e = jnp.einsum('jd,kl->kjld', w2, eye).reshape(NSA_KV_HEADS * CMP_HIDDEN, LANES).astype(CDT)
    b2e = jnp.tile(b2, NSA_KV_HEADS).reshape(1, -1)
    consts = [pa, pb, w1a, w1b, b1e, w2e, b2e]
    return pl.pallas_call(
        _compress_kernel,
        grid=(b,),
        in_specs=[pl.BlockSpec((None, nch, xw), lambda bi: (bi, 0, 0))]
        + [pl.BlockSpec(c.shape, lambda bi: (0, 0)) for c in consts],
        out_specs=pl.BlockSpec((None, nch, LANES), lambda bi: (bi, 0, 0)),
        out_shape=jax.ShapeDtypeStruct((b, nch, LANES), CDT),
        compiler_params=_cparams(("parallel",), 12),
        name="nsa_compress",
    )(x, *consts)


CMP_ROWS = 16


def _cmp_select_kernel(q_ref, kc_ref, vc_ref, ov_ref, o_ref, sel_ref, s_ref, p_ref, hi_ref, lo_ref,
                       *, tq, n_sel, tile0):
    i = pl.program_id(1) + tile0
    ncp = kc_ref.shape[0]
    ns = ov_ref.shape[1]
    lo = _half_masks()
    t_col = i * tq + lax.broadcasted_iota(jnp.int32, (tq, 1), 0)
    blk = lax.broadcasted_iota(jnp.int32, (tq, ns), 1)
    blk_t = lax.broadcasted_iota(jnp.int32, (ns, tq), 0)
    cur = t_col // SEL_BLOCK
    causal = blk <= cur
    forced = (blk == 0) | (blk == cur) | (blk == cur - 1)
    rb = CMP_ROWS
    c_end = lax.broadcasted_iota(jnp.int32, (rb, ncp), 1) * CMP_STRIDE + (CMP_BLOCK - 1)
    o_kv, work_t = [], []
    for kv, hm in enumerate((lo, jnp.logical_not(lo))):
        qs = jnp.concatenate(
            [(jnp.where(hm, q_ref[:, p * LANES:(p + 1) * LANES], 0) * QK_SCALE).astype(CDT) for p in range(3)], axis=0)
        s_ref[...] = _nt_dot(qs, kc_ref[...])
        for r0 in range(0, tq, rb):
            t_rows = i * tq + r0 + lax.broadcasted_iota(jnp.int32, (rb, 1), 0)
            vis_bias = jnp.where(c_end <= t_rows, 0.0, NEG_INF)
            has_visible = t_rows >= CMP_BLOCK - 1
            psum = jnp.zeros((rb, ncp), F32)
            for h in range(3):
                rows = slice(h * tq + r0, h * tq + r0 + rb)
                s = s_ref[rows, :] + vis_bias
                e = jnp.exp(s - jnp.max(s, axis=-1, keepdims=True))
                inv = jnp.where(has_visible, 1.0 / jnp.maximum(jnp.sum(e, axis=-1, keepdims=True), 1e-30), 0.0)
                pn = e * inv
                p_ref[rows, :] = pn.astype(CDT)
                psum = psum + pn
            p_hi = psum.astype(CDT)
            hi_ref[r0:r0 + rb, :] = p_hi
            lo_ref[r0:r0 + rb, :] = (psum - p_hi.astype(F32)).astype(CDT)
        o = _dot(p_ref[...], vc_ref[...])
        o_kv.append([o[p * tq:(p + 1) * tq] for p in range(3)])
        imp = _dot(hi_ref[...], ov_ref[...]) + _dot(lo_ref[...], ov_ref[...])
        work_t.append(jnp.where(causal & jnp.logical_not(forced), imp, -BIG).T)

    blk_lanes = blk_t[:, 0:LANES]

    def pick(_, work):
        m = jnp.max(work, axis=0, keepdims=True)
        idx = jnp.min(jnp.where(work == m, blk_lanes, ns), axis=0, keepdims=True)
        return jnp.where(blk_lanes == idx, -2.0 * BIG, work)

    for kv, start in enumerate(work_t):
        done = jnp.concatenate([lax.fori_loop(0, n_sel - 3, pick, start[:, c0:c0 + LANES], unroll=True)
                                for c0 in range(0, tq, LANES)], axis=1)
        taken = jnp.where((done < -BIG) & (start > -BIG), 1.0, 0.0).T
        selb = jnp.where(forced | (taken > 0.5), 0.0, NEG_INF)
        sel_ref[:, kv * ns:(kv + 1) * ns] = selb.astype(sel_ref.dtype)
    for p in range(3):
        o_ref[:, p * LANES:(p + 1) * LANES] = jnp.where(lo, o_kv[0][p], o_kv[1][p]).astype(o_ref.dtype)


CMP_CAUSAL_SPLITS = 4


def _cmp_select(zr, qcol, kc, vc, tq=256):
    b, s, _ = zr.shape
    ncp = kc.shape[1]
    ns = s // SEL_BLOCK
    n_sel = min(N_SEL, ns)
    assert n_sel >= 3, "selection needs room for the three forced blocks"
    tq = min(tq, s)
    c = np.arange(ncp)[:, None] * CMP_STRIDE
    j = np.arange(ns)[None, :] * SEL_BLOCK
    overlap = ((c < j + SEL_BLOCK) & (c + CMP_BLOCK - 1 >= j)).astype(np.float32)
    overlap[ncp - 1:, :] = 0.0
    ov = jnp.asarray(overlap, CDT)
    qw = 3 * LANES
    n_split = CMP_CAUSAL_SPLITS if (s // tq) % CMP_CAUSAL_SPLITS == 0 and ncp % (16 * CMP_CAUSAL_SPLITS) == 0 else 1
    tiles = s // tq // n_split
    outs, sels = [], []
    for part in range(n_split):
        tile0 = part * tiles
        ncp_part = ncp * (part + 1) // n_split
        o_part, sel_part = pl.pallas_call(
            functools.partial(_cmp_select_kernel, tq=tq, n_sel=n_sel, tile0=tile0),
            grid=(b, tiles),
            in_specs=[pl.BlockSpec((None, tq, qw), lambda bi, i, tile0=tile0: (bi, i + tile0, qcol)),
                      pl.BlockSpec((None, ncp_part, LANES), lambda bi, i: (bi, 0, 0)),
                      pl.BlockSpec((None, ncp_part, LANES), lambda bi, i: (bi, 0, 0)),
                      pl.BlockSpec((ncp_part, ns), lambda bi, i: (0, 0))],
            out_specs=[pl.BlockSpec((None, tq, qw), lambda bi, i: (bi, i, 0)),
                       pl.BlockSpec((None, tq, 2 * ns), lambda bi, i: (bi, i, 0))],
            out_shape=[jax.ShapeDtypeStruct((b, tiles * tq, qw), CDT),
                       jax.ShapeDtypeStruct((b, tiles * tq, 2 * ns), CDT)],
            scratch_shapes=[pltpu.VMEM((3 * tq, ncp_part), F32), pltpu.VMEM((3 * tq, ncp_part), CDT),
                            pltpu.VMEM((tq, ncp_part), CDT), pltpu.VMEM((tq, ncp_part), CDT)],
            compiler_params=_cparams(("parallel", "parallel"), 12),
            name=f"nsa_cmp_select_p{part}",
        )(zr, kc, vc, ov)
        outs.append(o_part)
        sels.append(sel_part)
    return jnp.concatenate(outs, axis=1), jnp.concatenate(sels, axis=1)


SLC_KEY_CHUNK = 256


def _slc_kernel(q_ref, k0_ref, k1_ref, vt_ref, selb_ref, o_ref, s_ref, e_ref, m_ref, l_ref, a_ref, acc_ref,
                *, tq, tk):
    i = pl.program_id(1)
    ns = selb_ref.shape[1] // 2
    bpt = tk // SEL_BLOCK
    n_q = 3 * tq
    rc = min(SLC_KEY_CHUNK, tk)
    lo = _half_masks()
    n_tiles = ((i + 1) * tq + tk - 1) // tk
    p_row = lax.broadcasted_iota(jnp.int32, (ns, LANES), 0)
    p_col = lax.broadcasted_iota(jnp.int32, (ns, LANES), 1)
    groups = []
    for kv, (hm, k_ref) in enumerate(((lo, k0_ref), (jnp.logical_not(lo), k1_ref))):
        q3 = [(jnp.where(hm, q_ref[:, p * LANES:(p + 1) * LANES], 0) * QK_SCALE).astype(CDT) for p in range(3)]
        selb = selb_ref[:, kv * ns:(kv + 1) * ns]
        lane0 = HEAD_DIM if kv == 0 else 0
        groups.append((kv, hm, k_ref, q3, selb, lane0))
    m_ref[...] = jnp.full(m_ref.shape, NEG_INF, F32)
    l_ref[...] = jnp.zeros(l_ref.shape, F32)
    acc_ref[...] = jnp.zeros(acc_ref.shape, F32)

    def scores(j, nk):
        k0 = pl.multiple_of(j * tk, tk)
        for kv, hm, k_ref, q3, selb, lane0 in groups:
            place = ((p_col >= lane0) & (p_col < lane0 + bpt) & (p_row == p_col - lane0 + j * bpt)).astype(CDT)
            sb = _dot(selb, place).astype(CDT)
            qp = jnp.concatenate([jnp.where(hm, q, sb) for q in q3], axis=0)
            s_ref[kv, 0:nk, :] = _nt_dot(k_ref[pl.ds(k0, nk), :], qp)

    def softmax_pv(j, diagonal, nk):
        k0 = pl.multiple_of(j * tk, tk)
        for kv in range(2):
            for c0 in range(0, n_q, LANES):
                cols = slice(c0, c0 + LANES)
                t_lane = i * tq + (c0 % tq) + lax.broadcasted_iota(jnp.int32, (1, LANES), 1)

                def chunk(r0):
                    s = s_ref[kv, r0:r0 + rc, cols]
                    if diagonal:
                        kpos = k0 + r0 + lax.broadcasted_iota(jnp.int32, (rc, LANES), 0)
                        s = jnp.where(kpos <= t_lane, s, NEG_INF)
                    return s

                m8 = m_ref[kv, :, cols]
                for r0 in range(0, nk, rc):
                    m8 = jnp.maximum(m8, jnp.max(chunk(r0).reshape(rc // 8, 8, LANES), axis=0))
                m_new = jnp.max(m8, axis=0, keepdims=True)
                alpha = jnp.exp(m_ref[kv, :, cols] - m_new)
                l8 = jnp.zeros((8, LANES), F32)
                for r0 in range(0, nk, rc):
                    e = jnp.exp(chunk(r0) - m_new)
                    l8 = l8 + jnp.sum(e.reshape(rc // 8, 8, LANES), axis=0)
                    e_ref[kv, r0:r0 + rc, cols] = e.astype(CDT)
                l_ref[kv, :, cols] = alpha * l_ref[kv, :, cols] + jnp.sum(l8, axis=0, keepdims=True)
                m_ref[kv, :, cols] = jnp.broadcast_to(m_new, (8, LANES))
                a_ref[kv, :, cols] = alpha
        for kv in range(2):
            acc_ref[kv] = a_ref[kv, 0:1, :] * acc_ref[kv] + _dot(vt_ref[j, :, 0:nk], e_ref[kv, 0:nk, :])

    def tile(j, diagonal, nk=tk):
        scores(j, nk)
        softmax_pv(j, diagonal, nk)

    lax.fori_loop(0, n_tiles - 1, lambda j, c: (tile(j, False), c)[1], 0)
    last = n_tiles - 1
    half = tk // 2
    first_half_only = (i + 1) * tq - last * tk <= half

    @pl.when(first_half_only)
    def _():
        tile(last, True, half)

    @pl.when(jnp.logical_not(first_half_only))
    def _():
        tile(last, True)
    o0 = (acc_ref[0] / l_ref[0, 0:1, :]).T
    o1 = (acc_ref[1] / l_ref[1, 0:1, :]).T
    for p in range(3):
        o_ref[:, p * LANES:(p + 1) * LANES] = jnp.where(
            lo, o0[p * tq:(p + 1) * tq], o1[p * tq:(p + 1) * tq]).astype(o_ref.dtype)


def _slc(zr, qcol, ksc, vt, selb, tq=512):
    b, s, _ = zr.shape
    tq = min(tq, s)
    tk = vt.shape[2]
    assert tq % LANES == 0 and tk // SEL_BLOCK <= HEAD_DIM and s % tk == 0
    qw = 3 * LANES
    ns2 = selb.shape[2]
    pat = jax.nn.one_hot((jnp.arange(s) // SEL_BLOCK) % (tk // SEL_BLOCK), HEAD_DIM, dtype=ksc.dtype)
    pat = jnp.broadcast_to(pat[None], (b, s, HEAD_DIM))
    k0 = jnp.concatenate([ksc[..., :HEAD_DIM], pat], axis=-1)
    k1 = jnp.concatenate([pat, ksc[..., HEAD_DIM:]], axis=-1)
    full = lambda bi, i: (bi, 0, 0)
    return pl.pallas_call(
        functools.partial(_slc_kernel, tq=tq, tk=tk),
        grid=(b, s // tq),
        in_specs=[pl.BlockSpec((None, tq, qw), lambda bi, i: (bi, i, qcol)),
                  _single_buffered((None, s, LANES), full),
                  _single_buffered((None, s, LANES), full),
                  _single_buffered((s // tk, LANES, tk), lambda bi, i: (bi, 0, 0)),
                  pl.BlockSpec((None, tq, ns2), lambda bi, i: (bi, i, 0))],
        out_specs=pl.BlockSpec((None, tq, qw), lambda bi, i: (bi, i, 0)),
        out_shape=jax.ShapeDtypeStruct((b, s, qw), CDT),
        scratch_shapes=[pltpu.VMEM((2, tk, 3 * tq), F32), pltpu.VMEM((2, tk, 3 * tq), CDT),
                        pltpu.VMEM((2, 8, 3 * tq), F32), pltpu.VMEM((2, 8, 3 * tq), F32),
                        pltpu.VMEM((2, 8, 3 * tq), F32), pltpu.VMEM((2, LANES, 3 * tq), F32)],
        compiler_params=_cparams(("parallel", "arbitrary"), 14),
        name="nsa_slc",
    )(zr, k0, k1, vt, selb)


def _merge_kernel(x_ref, oa0, oa1, oa2, la0, la1, la2, ob, ocmp, oslc, owin, od,
                  wn_ref, wg_ref, bg_ref, wb_ref, wo_ref, g_ref, b_ref, o_ref, *scratch, alpha):
    x = x_ref[...]
    xb = x.astype(CDT)
    bw = BRANCH_WIDTH

    def token_rows(src_ref, dst_ref):
        dil = src_ref.shape[1] // bw
        if dil == 1:
            return src_ref[...].astype(F32)
        n_cb = bw // LANES
        for r in range(dil):
            for cb in range(n_cb):
                c0 = r * bw + cb * LANES
                dst_ref[cb, pl.ds(r, src_ref.shape[0], stride=dil), :] = src_ref[:, c0:c0 + LANES].astype(F32)
        return jnp.concatenate([dst_ref[cb] for cb in range(n_cb)], axis=1)

    o0, o1, o2 = (token_rows(s, d) for s, d in zip((oa0, oa1, oa2), scratch[0:3]))
    l0, l1, l2 = (token_rows(s, d) for s, d in zip((la0, la1, la2), scratch[3:6]))
    mx = jnp.maximum(jnp.maximum(l0, l1), l2)
    w0, w1, w2 = jnp.exp(l0 - mx), jnp.exp(l1 - mx), jnp.exp(l2 - mx)
    o_a = (w0 * o0 + w1 * o1 + w2 * o2) / (w0 + w1 + w2)
    gates = jax.nn.sigmoid(_dot(xb, wn_ref[...]))
    o_c = (gates[:, 0:bw] * ocmp[...].astype(F32) + gates[:, bw:2 * bw] * oslc[...].astype(F32)
           + gates[:, 2 * bw:3 * bw] * owin[...].astype(F32))
    branches = (o_a.astype(CDT), ob[...], o_c.astype(CDT), od[...])
    d = x.shape[1]
    merged = jnp.zeros(x.shape, F32)
    for m in range(N_BRANCH):
        gate = jax.nn.sigmoid(_dot(xb, wg_ref[:, m * d:(m + 1) * d]) + bg_ref[:, m * d:(m + 1) * d])
        merged = merged + gate * _dot(branches[m], wb_ref[m])
    r = alpha * x + _dot(merged.astype(CDT), wo_ref[...])
    o_ref[...] = _layer_norm(r, g_ref[...], b_ref[...])


def _merge(x2d, branch_inputs, wn, wg, bg, wb, wo, g, b, alpha, tm=256):
    t, d = x2d.shape
    tm = min(tm, t)
    row = lambda i: (i, 0)
    const2 = lambda i: (0, 0)
    in_specs = [pl.BlockSpec((tm, d), row)]
    in_specs += [pl.BlockSpec((tm * a.shape[0] // t, a.shape[1]), row) for a in branch_inputs]
    in_specs += [pl.BlockSpec(wn.shape, const2), pl.BlockSpec(wg.shape, const2), pl.BlockSpec(bg.shape, const2),
                 pl.BlockSpec(wb.shape, lambda i: (0, 0, 0)), pl.BlockSpec(wo.shape, const2),
                 pl.BlockSpec(g.shape, const2), pl.BlockSpec(b.shape, const2)]
    return pl.pallas_call(
        functools.partial(_merge_kernel, alpha=alpha),
        grid=(t // tm,),
        in_specs=in_specs,
        out_specs=pl.BlockSpec((tm, d), row),
        out_shape=jax.ShapeDtypeStruct((t, d), F32),
        scratch_shapes=[pltpu.VMEM((BRANCH_WIDTH // LANES, tm, LANES), F32) for _ in range(2 * N_DIL)],
        compiler_params=_cparams(("parallel",), 14),
        name="merge_ln",
    )(x2d, *branch_inputs, wn, wg, bg, wb, wo, g, b)


def _ple_ln(x, xb, f, p, plw_ref, pgw_ref, pgb_ref, g_ref, b_ref, alpha):
    ple = jax.nn.sigmoid(_dot(xb, pgw_ref[...]) + pgb_ref[...]) * _dot(p.astype(CDT), plw_ref[...])
    return _layer_norm(alpha * x + f + ple, g_ref[...], b_ref[...])


FFN_CHUNK = 512


def _ffn_kernel(x_ref, p_ref, wg_ref, wu_ref, wd_ref, plw_ref, pgw_ref, pgb_ref, g_ref, b_ref, o_ref, h_ref, *, alpha):
    x = x_ref[...]
    xb = x.astype(CDT)
    dff = wg_ref.shape[1]
    for c0 in range(0, dff, FFN_CHUNK):
        cols = slice(c0, min(c0 + FFN_CHUNK, dff))
        h_ref[:, cols] = (jax.nn.silu(_dot(xb, wg_ref[:, cols])) * _dot(xb, wu_ref[:, cols])).astype(CDT)
    f = _dot(h_ref[...], wd_ref[...])
    o_ref[...] = _ple_ln(x, xb, f, p_ref[...], plw_ref, pgw_ref, pgb_ref, g_ref, b_ref, alpha)


def _ffn(x2d, p2d, wg, wu, wd, plw, pgw, pgb, g, b, alpha, tm=512):
    t, d = x2d.shape
    tm = min(tm, t)
    dff = wg.shape[1]
    wg, wu, wd = wg.astype(CDT), wu.astype(CDT), wd.astype(CDT)
    row = lambda i: (i, 0)
    const = lambda shape: pl.BlockSpec(shape, lambda i: (0, 0), pipeline_mode=pl.Buffered(1))
    return pl.pallas_call(
        functools.partial(_ffn_kernel, alpha=alpha),
        grid=(t // tm,),
        in_specs=[pl.BlockSpec((tm, d), row), pl.BlockSpec((tm, p2d.shape[1]), row),
                  const(wg.shape), const(wu.shape), const(wd.shape),
                  const(plw.shape), const(pgw.shape), const(pgb.shape), const(g.shape), const(b.shape)],
        out_specs=pl.BlockSpec((tm, d), row),
        out_shape=jax.ShapeDtypeStruct((t, d), F32),
        scratch_shapes=[pltpu.VMEM((tm, dff), CDT)],
        compiler_params=_cparams(("parallel",), 12),
        name="ffn_ple_ln",
    )(x2d, p2d, wg, wu, wd, plw, pgw, pgb, g, b)


def _router_kernel(x_ref, wh_ref, wl_ref, b_ref, comb_ref, rank_ref, rank_t_ref, cnt_ref):
    x = x_ref[...]
    xh = x.astype(CDT)
    xl = (x - xh.astype(F32)).astype(CDT)
    logits = _dot(xh, wh_ref[...]) + _dot(xh, wl_ref[...]) + _dot(xl, wh_ref[...]) + b_ref[...]
    lane = lax.broadcasted_iota(jnp.int32, logits.shape, 1)
    v1 = jnp.max(logits, axis=-1, keepdims=True)
    i1 = jnp.min(jnp.where(logits == v1, lane, LANES), axis=-1, keepdims=True)
    rest = jnp.where(lane == i1, -jnp.inf, logits)
    v2 = jnp.max(rest, axis=-1, keepdims=True)
    i2 = jnp.min(jnp.where(rest == v2, lane, LANES), axis=-1, keepdims=True)
    e2 = jnp.exp(v2 - v1)
    comb_ref[...] = jnp.where(lane == i1, 1.0 / (1.0 + e2), 0.0) + jnp.where(lane == i2, e2 / (1.0 + e2), 0.0)
    routed = (lane == i1) | (lane == i2)
    mask = routed.astype(CDT)
    tm = x.shape[0]
    before = (lax.broadcasted_iota(jnp.int32, (tm, tm), 1) < lax.broadcasted_iota(jnp.int32, (tm, tm), 0)).astype(CDT)
    rank = jnp.where(routed, _dot(before, mask), -1.0)
    rank_ref[...] = rank
    rank_t_ref[...] = rank.T[0:rank_t_ref.shape[0], :]
    cnt_ref[...] = jnp.sum(routed.astype(F32), axis=0, keepdims=True).astype(jnp.int32)


def _router(x2d, w_router, b_router, tm):
    t, d = x2d.shape
    ne = w_router.shape[1]
    wp = jnp.zeros((d, LANES), F32).at[:, :ne].set(w_router)
    wh = wp.astype(CDT)
    wl = (wp - wh.astype(F32)).astype(CDT)
    bp = jnp.full((1, LANES), -BIG, F32).at[0, :ne].set(b_router)
    nt = t // tm
    row = lambda i: (i, 0)
    return pl.pallas_call(
        _router_kernel,
        grid=(nt,),
        in_specs=[pl.BlockSpec((tm, d), row), pl.BlockSpec(wh.shape, lambda i: (0, 0)),
                  pl.BlockSpec(wl.shape, lambda i: (0, 0)), pl.BlockSpec(bp.shape, lambda i: (0, 0))],
        out_specs=[pl.BlockSpec((tm, LANES), row), pl.BlockSpec((tm, LANES), row),
                   pl.BlockSpec((None, 8, tm), lambda i: (i, 0, 0)),
                   pl.BlockSpec((None, 1, LANES), lambda i: (i, 0, 0))],
        out_shape=[jax.ShapeDtypeStruct((t, LANES), F32), jax.ShapeDtypeStruct((t, LANES), F32),
                   jax.ShapeDtypeStruct((nt, 8, tm), F32), jax.ShapeDtypeStruct((nt, 1, LANES), jnp.int32)],
        compiler_params=_cparams(("parallel",), 10),
        name="moe_router",
    )(x2d, wh, wl, bp)


def _moe_kernel(cnt_ref, x_ref, comb_ref, rank_ref, rank_t_ref, p_ref, wg_ref, wu_ref, wd_ref, plw_ref, pgw_ref,
                pgb_ref, g_ref, b_ref, o_ref, xe_ref, ye_ref, *, alpha, rs, seg):
    i = pl.program_id(0)
    e = pl.program_id(1)
    c = pl.program_id(2)
    n_seg = x_ref.shape[0] // seg
    last_chunk = c == pl.num_programs(2) - 1

    first = slice(0, rs)
    toks = [slice(sg * seg, (sg + 1) * seg) for sg in range(n_seg)]
    n_groups = [(cnt_ref[(i * n_seg + sg) * LANES + e] + rs - 1) // rs for sg in range(n_seg)]

    def later(sc):
        return pl.ds(pl.multiple_of(sc * rs, 8), rs), (sc * rs).astype(F32)

    @pl.when((e == 0) & (c == 0))
    def _():
        o_ref[...] = jnp.zeros_like(o_ref)

    @pl.when(c == 0)
    def _():
        row_id = lax.broadcasted_iota(jnp.int32, (rs, seg), 0).astype(F32)
        xbs = [x_ref[tok, :].astype(CDT) for tok in toks]
        rank_rows = [rank_t_ref[sg, pl.ds(e, 1), :] for sg in range(n_seg)]

        def gather(sg, rws, base):
            onehot = (rank_rows[sg] - base == row_id).astype(CDT)
            xe_ref[sg, rws, :] = _dot(onehot, xbs[sg]).astype(CDT)
            ye_ref[sg, rws, :] = jnp.zeros((rs, ye_ref.shape[2]), F32)

        for sg in range(n_seg):
            gather(sg, first, 0.0)
        for sg in range(n_seg):
            lax.fori_loop(1, n_groups[sg], lambda sc, _, sg=sg: (gather(sg, *later(sc)), 0)[1], 0)

    def expert(sg, rws):
        xs = xe_ref[sg, rws, :]
        h = jax.nn.silu(_dot(xs, wg_ref[0])) * _dot(xs, wu_ref[0])
        ye_ref[sg, rws, :] += _dot(h.astype(CDT), wd_ref[0])

    for sg in range(n_seg):
        expert(sg, first)
    for sg in range(n_seg):
        lax.fori_loop(1, n_groups[sg], lambda sc, _, sg=sg: (expert(sg, later(sc)[0]), 0)[1], 0)

    @pl.when(last_chunk)
    def _():
        lane = lax.broadcasted_iota(jnp.int32, (seg, LANES), 1)
        mine = lane == e
        col_id = lax.broadcasted_iota(jnp.int32, (seg, rs), 1).astype(F32)
        cws = [jnp.sum(jnp.where(mine, comb_ref[tok, :], 0.0), axis=-1, keepdims=True) for tok in toks]
        rank_cols = [jnp.sum(jnp.where(mine, rank_ref[tok, :], 0.0), axis=-1, keepdims=True) for tok in toks]

        def scatter(sg, rws, base):
            onehot = (rank_cols[sg] - base == col_id).astype(CDT)
            o_ref[toks[sg], :] += cws[sg] * _dot(onehot, ye_ref[sg, rws, :].astype(CDT))

        for sg in range(n_seg):
            scatter(sg, first, 0.0)
        for sg in range(n_seg):
            lax.fori_loop(1, n_groups[sg], lambda sc, _, sg=sg: (scatter(sg, *later(sc)), 0)[1], 0)

    @pl.when((e == pl.num_programs(1) - 1) & last_chunk)
    def _():
        for sg in range(n_seg):
            tok = slice(sg * seg, (sg + 1) * seg)
            x = x_ref[tok, :]
            o_ref[tok, :] = _ple_ln(x, x.astype(CDT), o_ref[tok, :], p_ref[tok, :], plw_ref, pgw_ref, pgb_ref,
                                    g_ref, b_ref, alpha)


MOE_CHUNK = 512
MOE_SEGMENT = 1024
MOE_SEGMENTS_PER_TILE = 2
MOE_ROW_GROUP = 288


def _single_buffered(shape, index_map):
    return pl.BlockSpec(shape, index_map, pipeline_mode=pl.Buffered(1))


def _moe(x2d, routing, p2d, wg, wu, wd, plw, pgw, pgb, g, b, alpha, seg):
    comb, rank, rank_t, cnt = routing
    t, d = x2d.shape
    ne, _, dff = wg.shape
    ck = min(MOE_CHUNK, dff)
    rs = min(MOE_ROW_GROUP, seg)
    n_seg = min(MOE_SEGMENTS_PER_TILE, t // seg)
    tm = n_seg * seg
    max_rows = -(-seg // rs) * rs
    wg, wu, wd = wg.astype(CDT), wu.astype(CDT), wd.astype(CDT)
    row = lambda i, e, c, cnt: (i, 0)
    c2 = lambda i, e, c, cnt: (0, 0)
    grid_spec = pltpu.PrefetchScalarGridSpec(
        num_scalar_prefetch=1,
        grid=(t // tm, ne, dff // ck),
        in_specs=[_single_buffered((tm, d), row), _single_buffered((tm, LANES), row),
                  _single_buffered((tm, LANES), row),
                  _single_buffered((n_seg, 8, seg), lambda i, e, c, cnt: (i, 0, 0)),
                  _single_buffered((tm, p2d.shape[1]), row),
                  pl.BlockSpec((1, d, ck), lambda i, e, c, cnt: (e, 0, c)),
                  pl.BlockSpec((1, d, ck), lambda i, e, c, cnt: (e, 0, c)),
                  pl.BlockSpec((1, ck, d), lambda i, e, c, cnt: (e, c, 0)),
                  _single_buffered(plw.shape, c2), _single_buffered(pgw.shape, c2), _single_buffered(pgb.shape, c2),
                  _single_buffered(g.shape, c2), _single_buffered(b.shape, c2)],
        out_specs=pl.BlockSpec((tm, d), row),
        scratch_shapes=[pltpu.VMEM((n_seg, max_rows, d), CDT), pltpu.VMEM((n_seg, max_rows, d), F32)],
    )
    return pl.pallas_call(
        functools.partial(_moe_kernel, alpha=alpha, rs=rs, seg=seg),
        grid_spec=grid_spec,
        out_shape=jax.ShapeDtypeStruct((t, d), F32),
        compiler_params=_cparams(("parallel", "arbitrary", "arbitrary"), 15),
        name="moe_ple_ln",
    )(cnt.reshape(-1), x2d, comb, rank, rank_t, p2d, wg, wu, wd, plw, pgw, pgb, g, b)


def _prep_in_weights(w_in):
    o = COL_OFF
    bw = BRANCH_WIDTH
    cols = lambda n: w_in[:, o[n]:o[n + 1]]
    qa, ka, va = cols(0), cols(1), cols(2)
    w_dil = jnp.concatenate(
        [t[:, g * bw:(g + 1) * bw] for g in range(N_DIL) for t in (qa, ka, va)], axis=1).astype(CDT)
    w_conv = jnp.concatenate([cols(3), cols(4), cols(5)], axis=1).astype(CDT)
    gn = cols(13)
    w_gate = jnp.concatenate([gn[:, br * NSA_Q_HEADS + GQA_COL_HEAD] for br in range(3)], axis=1).astype(CDT)
    w_rest = jnp.concatenate([cols(6)[:, GQA_COL_PERM], cols(14)[:, GQA_COL_PERM], cols(11),
                              cols(12), cols(15), cols(16), cols(9), cols(7), cols(8)], axis=1).astype(CDT)
    wt_vsc = cols(10).T.astype(CDT)
    return w_dil, w_conv, w_gate, w_rest, wt_vsc


ZR_Q_NSA, ZR_Q_SWA = 0, 1
ZR_KWC, ZR_VWC, ZR_KD, ZR_VD = 6, 7, 8, 9
ZR_WIDTH = 2 * BRANCH_WIDTH + 4 * LANES
SLC_KEY_TILE = 1024


def _token_mixers(x, w_in, conv_w, cmp_pos, cmp_w1, cmp_b1, cmp_w2, cmp_b2, sinks):
    b, s, d = x.shape
    x2d = x.reshape(b * s, d)
    w_dil, w_conv, w_gate, w_rest, wt_vsc = _prep_in_weights(w_in)
    gw = 3 * BRANCH_WIDTH

    z_dil = _linear(x2d, w_dil, [(g * gw, (g + 1) * gw) for g in range(N_DIL)], gw, "in_proj_dil",
                    dils=[dil for _, dil in DIL_PATTERNS])
    zr, ksc, kcc, vcc = _linear(x2d, w_rest, [(0, ZR_WIDTH)] + [(ZR_WIDTH + n * LANES, ZR_WIDTH + (n + 1) * LANES)
                                                           for n in range(3)], 256, "in_proj_rest")
    zr = zr.reshape(b, s, ZR_WIDTH)
    o_b = _conv(x, w_conv, conv_w)

    dil_o, dil_lse = [], []
    for g, (window, dil) in enumerate(DIL_PATTERNS):
        view = z_dil[g].reshape(b, s // dil, dil * gw)
        og, lg = _banded(view, view, view, nrep=dil,
                         qcol=lambda r: 3 * r, kcol=lambda r: 3 * r + 1, vcol=lambda r: 3 * r + 2,
                         kw=3 * LANES, window=window // dil, want_lse=True)
        dil_o.append(og.reshape(b * s // dil, dil * BRANCH_WIDTH))
        dil_lse.append(lg.reshape(b * s // dil, dil * BRANCH_WIDTH))

    kc = _compress(kcc.reshape(b, s, LANES), cmp_pos[0], cmp_w1[0], cmp_b1[0], cmp_w2[0], cmp_b2[0])
    vc = _compress(vcc.reshape(b, s, LANES), cmp_pos[1], cmp_w1[1], cmp_b1[1], cmp_w2[1], cmp_b2[1])
    o_cmp, selb = _cmp_select(zr, ZR_Q_NSA, kc, vc)
    vsc_t = _linear_t(x2d, wt_vsc, min(SLC_KEY_TILE, s), "in_proj_vsc_t")
    o_slc = _slc(zr, ZR_Q_NSA, ksc.reshape(b, s, LANES), vsc_t, selb)
    (o_win,) = _banded(zr, zr, zr, nrep=1, qcol=lambda r: ZR_Q_NSA, kcol=lambda r: ZR_KWC, vcol=lambda r: ZR_VWC,
                       kw=LANES, window=NSA_WINDOW - 1, want_lse=False, tq=512)

    sink_row = sinks.astype(F32)[GQA_COL_HEAD].reshape(1, BRANCH_WIDTH)
    (o_d,) = _banded(zr, zr, zr, nrep=1, qcol=lambda r: ZR_Q_SWA, kcol=lambda r: ZR_KD, vcol=lambda r: ZR_VD,
                     kw=LANES, window=SWA_WINDOW - 1, want_lse=False, sink_row=sink_row)

    t = b * s
    flat = lambda a: a.reshape(t, a.shape[-1])
    return [dil_o[0], dil_o[1], dil_o[2], dil_lse[0], dil_lse[1], dil_lse[2], flat(o_b), flat(o_cmp), flat(o_slc),
            flat(o_win), flat(o_d)], w_gate


def kernel(x, p, w_in, conv_w, cmp_pos, cmp_w1, cmp_b1, cmp_w2, cmp_b2, sinks, w_branch, w_merge_gate, b_merge_gate, w_out, ln_mix_g, ln_mix_b, ffn_w_gate, ffn_w_up, ffn_w_down, w_router, b_router, moe_w_gate, moe_w_up, moe_w_down, ple_w, ple_gate_w, ple_gate_b, ln_ffn_g, ln_ffn_b):
    depth, b, s, _ = p.shape
    d = x.shape[-1]
    t = b * s
    alpha = (2 * depth) ** 0.25
    row = lambda v: v.reshape(1, -1).astype(F32)
    for i in range(depth):
        branch_inputs, w_nsa_gate = _token_mixers(x, w_in[i], conv_w[i], cmp_pos[i], cmp_w1[i], cmp_b1[i], cmp_w2[i],
                                                  cmp_b2[i], sinks[i])
        wg = jnp.concatenate([w_merge_gate[i, m] for m in range(N_BRANCH)], axis=1).astype(CDT)
        bg = b_merge_gate[i].reshape(1, N_BRANCH * d).astype(F32)
        wb = jnp.stack([w_branch[i, 0], w_branch[i, 1], w_branch[i, 2][GQA_COL_PERM],
                        w_branch[i, 3][GQA_COL_PERM]]).astype(CDT)
        x1 = _merge(x.reshape(t, d), branch_inputs, w_nsa_gate, wg, bg, wb, w_out[i].astype(CDT),
                    row(ln_mix_g[i]), row(ln_mix_b[i]), alpha)
        p2d = p[i].reshape(t, -1)
        ple_args = (ple_w[i].astype(CDT), ple_gate_w[i].astype(CDT), row(ple_gate_b[i]),
                    row(ln_ffn_g[i]), row(ln_ffn_b[i]))
        j = i // 2
        if i % 2 == 0:
            x2 = _ffn(x1, p2d, ffn_w_gate[j], ffn_w_up[j], ffn_w_down[j], *ple_args, alpha)
        else:
            seg = min(MOE_SEGMENT, t)
            routing = _router(x1, w_router[j], b_router[j], seg)
            x2 = _moe(x1, routing, p2d, moe_w_gate[j], moe_w_up[j], moe_w_down[j], *ple_args, alpha, seg)
        x = x2.reshape(b, s, d)
    return x
```

```python
import functools

import numpy as np
import jax
import jax.numpy as jnp
from jax import lax
from jax.experimental import pallas as pl
from jax.experimental.pallas import tpu as pltpu

HEAD_DIM = 64
DIL_PATTERNS = ((128, 1), (512, 4), (2048, 16))
N_DIL = 3
DIL_HEADS = 6
CONV_WIDTH = 384
NSA_Q_HEADS = 6
NSA_KV_HEADS = 2
CMP_BLOCK = 32
CMP_STRIDE = 16
CMP_HIDDEN = 128
SEL_BLOCK = 64
N_SEL = 16
NSA_WINDOW = 512
SWA_Q_HEADS = 6
SWA_WINDOW = 128
BRANCH_WIDTH = 384
N_BRANCH = 4
LN_EPS = 1e-5
NEG_INF = -1e30
DIL_WIDTH = N_DIL * DIL_HEADS * HEAD_DIM
COLUMN_SIZES = (DIL_WIDTH, DIL_WIDTH, DIL_WIDTH, CONV_WIDTH, CONV_WIDTH, CONV_WIDTH,
                NSA_Q_HEADS * HEAD_DIM, 128, 128, 128, 128, 128, 128, 3 * NSA_Q_HEADS,
                SWA_Q_HEADS * HEAD_DIM, 128, 128)
COL_OFF = np.concatenate([[0], np.cumsum(COLUMN_SIZES)]).tolist()

LANES = 128
V7X_VMEM_BYTES = 64 * 1024 * 1024

CDT = jnp.bfloat16
F32 = jnp.float32
QK_SCALE = HEAD_DIM ** -0.5
SUB_Q = 128
BAND_ROWS = 64
BIG = 1e30

_GQA_HEAD_ORDER = (0, 3, 1, 4, 2, 5)
GQA_COL_PERM = np.concatenate([np.arange(h * HEAD_DIM, (h + 1) * HEAD_DIM) for h in _GQA_HEAD_ORDER])
GQA_COL_HEAD = GQA_COL_PERM // HEAD_DIM


def _cparams(sem, vmem_sixteenths):
    return pltpu.CompilerParams(dimension_semantics=sem, vmem_limit_bytes=V7X_VMEM_BYTES * vmem_sixteenths // 16)


def _nt_dot(a, b):
    return lax.dot_general(a, b, (((1,), (1,)), ((), ())), preferred_element_type=F32)


def _dot(a, b):
    return jnp.dot(a, b, preferred_element_type=F32)


def _layer_norm(r, g, b):
    mu = jnp.mean(r, axis=-1, keepdims=True)
    d = r - mu
    var = jnp.mean(d * d, axis=-1, keepdims=True)
    return d * lax.rsqrt(var + LN_EPS) * g + b


def _half_masks():
    lane = lax.broadcasted_iota(jnp.int32, (1, LANES), 1)
    return lane < HEAD_DIM


def _linear_kernel(x_ref, w_ref, *refs, splits, n_chunk, dils):
    o_refs, z_ref = refs[:len(splits)], refs[len(splits)]
    xb = x_ref[...].astype(CDT)
    tm = xb.shape[0]
    for o_ref, (c0, c1), dil in zip(o_refs, splits, dils):
        width = c1 - c0
        if dil == 1:
            for a in range(c0, c1, n_chunk):
                b = min(a + n_chunk, c1)
                o_ref[:, a - c0:b - c0] = _dot(xb, w_ref[:, a:b]).astype(o_ref.dtype)
        else:
            z = _dot(xb, w_ref[:, c0:c1])
            for cb in range(width // LANES):
                z_ref[cb] = z[:, cb * LANES:(cb + 1) * LANES]
            for r in range(dil):
                for cb in range(width // LANES):
                    o_ref[:, r * width + cb * LANES:r * width + (cb + 1) * LANES] = (
                        z_ref[cb, pl.ds(r, tm // dil, stride=dil), :].astype(o_ref.dtype))


def _linear(x2d, w, splits, n_chunk, name, dils=None, tm=512):
    t, k = x2d.shape
    tm = min(tm, t)
    n = w.shape[1]
    dils = tuple(dils) if dils is not None else (1,) * len(splits)
    widths = [c1 - c0 for c0, c1 in splits]
    assert all(tm % (16 * dl) == 0 for dl in dils)
    return pl.pallas_call(
        functools.partial(_linear_kernel, splits=tuple(splits), n_chunk=n_chunk, dils=dils),
        grid=(t // tm,),
        in_specs=[pl.BlockSpec((tm, k), lambda i: (i, 0)),
                  pl.BlockSpec((k, n), lambda i: (0, 0))],
        out_specs=[pl.BlockSpec((tm // dl, dl * wd), lambda i: (i, 0)) for wd, dl in zip(widths, dils)],
        out_shape=[jax.ShapeDtypeStruct((t // dl, dl * wd), CDT) for wd, dl in zip(widths, dils)],
        scratch_shapes=[pltpu.VMEM((max(widths) // LANES, tm, LANES), F32)],
        compiler_params=_cparams(("parallel",), 12),
        name=name,
    )(x2d, w)


def _linear_t_kernel(x_ref, wt_ref, o_ref):
    o_ref[...] = _nt_dot(wt_ref[...], x_ref[...].astype(CDT)).astype(o_ref.dtype)


def _linear_t(x2d, wt, tm, name):
    t, k = x2d.shape
    n = wt.shape[0]
    return pl.pallas_call(
        _linear_t_kernel,
        grid=(t // tm,),
        in_specs=[pl.BlockSpec((tm, k), lambda i: (i, 0)), pl.BlockSpec((n, k), lambda i: (0, 0))],
        out_specs=pl.BlockSpec((None, n, tm), lambda i: (i, 0, 0)),
        out_shape=jax.ShapeDtypeStruct((t // tm, n, tm), CDT),
        compiler_params=_cparams(("parallel",), 8),
        name=name,
    )(x2d, wt)


def _conv_kernel(x_ref, xh_ref, wc_ref, cw_ref, ob_ref, *, tm):
    i = pl.program_id(1)
    w = CONV_WIDTH
    xb = x_ref[...].astype(CDT)
    z = _dot(xb, wc_ref[...])
    u = z[:, w:2 * w] * z[:, 2 * w:3 * w]
    zh = _dot(xh_ref[...].astype(CDT), wc_ref[:, w:3 * w])
    uh = zh[:, :w] * zh[:, w:]
    uh = jnp.where(i == 0, 0.0, uh)
    row = lax.broadcasted_iota(jnp.int32, (tm, w), 0)
    u1 = jnp.where(row == 0, uh[7:8, :], pltpu.roll(u, 1, 0))
    u2 = jnp.where(row == 0, uh[6:7, :], jnp.where(row == 1, uh[7:8, :], pltpu.roll(u, 2, 0)))
    y = cw_ref[0:1, :] * u2 + cw_ref[1:2, :] * u1 + cw_ref[2:3, :] * u
    ob_ref[...] = (z[:, :w] * y).astype(ob_ref.dtype)


def _conv(x, wc, conv_w, tm=512):
    b, s, d = x.shape
    tm = min(tm, s)
    hb = tm // 8
    return pl.pallas_call(
        functools.partial(_conv_kernel, tm=tm),
        grid=(b, s // tm),
        in_specs=[pl.BlockSpec((None, tm, d), lambda bi, i: (bi, i, 0)),
                  pl.BlockSpec((None, 8, d), lambda bi, i: (bi, jnp.maximum(i * hb - 1, 0), 0)),
                  pl.BlockSpec(wc.shape, lambda bi, i: (0, 0)),
                  pl.BlockSpec(conv_w.shape, lambda bi, i: (0, 0))],
        out_specs=pl.BlockSpec((None, tm, CONV_WIDTH), lambda bi, i: (bi, i, 0)),
        out_shape=jax.ShapeDtypeStruct((b, s, CONV_WIDTH), CDT),
        compiler_params=_cparams(("parallel", "parallel"), 12),
        name="short_conv",
    )(x, x, wc, conv_w)


def _banded_kernel(*refs, window, pr, tq, kw, want_lse, has_sink):
    q_ref, kp_ref, kc_ref, vp_ref, vc_ref = refs[:5]
    n = 5
    sink_ref = None
    if has_sink:
        sink_ref = refs[n]
        n += 1
    o_ref = refs[n]
    n += 1
    lse_ref = None
    if want_lse:
        lse_ref = refs[n]
        n += 1
    kbuf, vbuf, s_ref, e_ref, m_ref, l_ref = refs[n:n + 6]

    i = pl.program_id(2)
    kbuf[0:pr, :] = kp_ref[...]
    kbuf[pr:pr + tq, :] = kc_ref[...]
    vbuf[0:pr, :] = vp_ref[...]
    vbuf[pr:pr + tq, :] = vc_ref[...]

    span = SUB_Q + pr
    qi = lax.broadcasted_iota(jnp.int32, (SUB_Q, span), 0)
    kj = lax.broadcasted_iota(jnp.int32, (SUB_Q, span), 1)
    dist = pr + qi - kj
    band = (dist >= 0) & (dist <= window)
    lo = _half_masks()
    halves = (lo, jnp.logical_not(lo))
    groups = ((0,), (1,), (2,)) if kw == 3 * LANES else ((0, 1, 2),)
    rb = BAND_ROWS
    for sb in range(tq // SUB_Q):
        r0 = sb * SUB_Q
        bias = jnp.where(band & (i * tq + r0 - pr + kj >= 0), 0.0, NEG_INF)
        for grp in groups:
            kc0 = grp[0] * LANES if kw == 3 * LANES else 0
            qs = jnp.concatenate(
                [(jnp.where(hm, q_ref[r0:r0 + SUB_Q, p * LANES:(p + 1) * LANES], 0) * QK_SCALE).astype(CDT)
                 for p in grp for hm in halves], axis=0)
            g0 = 2 * grp[0] * SUB_Q
            s_ref[g0:g0 + qs.shape[0], :] = _nt_dot(qs, kbuf[r0:r0 + span, kc0:kc0 + LANES])
        for c0 in range(0, 6 * SUB_Q, rb):
            rows = slice(c0, c0 + rb)
            s = s_ref[rows, :] + bias[c0 % SUB_Q:c0 % SUB_Q + rb, :]
            m = jnp.max(s, axis=-1, keepdims=True)
            e = jnp.exp(s - m)
            e_ref[rows, :] = e.astype(CDT)
            m_ref[rows, :] = jnp.broadcast_to(m, (rb, LANES))
            l_ref[rows, :] = jnp.broadcast_to(jnp.sum(e, axis=-1, keepdims=True), (rb, LANES))
        for grp in groups:
            kc0 = grp[0] * LANES if kw == 3 * LANES else 0
            g0 = 2 * grp[0] * SUB_Q
            g1 = g0 + 2 * len(grp) * SUB_Q
            l = l_ref[g0:g1, :]
            o = _dot(e_ref[g0:g1, :], vbuf[r0:r0 + span, kc0:kc0 + LANES]) / l
            lse = m_ref[g0:g1, :] + jnp.log(l)
            for n_p, p in enumerate(grp):
                a = 2 * n_p * SUB_Q
                o_pair = jnp.where(lo, o[a:a + SUB_Q], o[a + SUB_Q:a + 2 * SUB_Q])
                lse_pair = jnp.where(lo, lse[a:a + SUB_Q], lse[a + SUB_Q:a + 2 * SUB_Q])
                if has_sink:
                    o_pair = o_pair * jax.nn.sigmoid(lse_pair - sink_ref[:, p * LANES:(p + 1) * LANES])
                o_ref[r0:r0 + SUB_Q, p * LANES:(p + 1) * LANES] = o_pair.astype(o_ref.dtype)
                if want_lse:
                    lse_ref[r0:r0 + SUB_Q, p * LANES:(p + 1) * LANES] = lse_pair


def _banded(qa, ka, va, *, nrep, qcol, kcol, vcol, kw, window, want_lse, sink_row=None, tq=1024):
    b, l, _ = qa.shape
    pr = -(-window // SUB_Q) * SUB_Q
    tq = min(max(tq, pr), l)
    assert tq % pr == 0 and l % tq == 0, (tq, pr, l)
    ratio = tq // pr
    qw = 3 * LANES
    in_specs = [
        pl.BlockSpec((None, tq, qw), lambda bi, r, i: (bi, i, qcol(r))),
        pl.BlockSpec((None, pr, kw), lambda bi, r, i: (bi, jnp.maximum(i * ratio - 1, 0), kcol(r))),
        pl.BlockSpec((None, tq, kw), lambda bi, r, i: (bi, i, kcol(r))),
        pl.BlockSpec((None, pr, kw), lambda bi, r, i: (bi, jnp.maximum(i * ratio - 1, 0), vcol(r))),
        pl.BlockSpec((None, tq, kw), lambda bi, r, i: (bi, i, vcol(r))),
    ]
    args = [qa, ka, ka, va, va]
    if sink_row is not None:
        in_specs.append(pl.BlockSpec(sink_row.shape, lambda bi, r, i: (0, 0)))
        args.append(sink_row)
    out_specs = [pl.BlockSpec((None, tq, qw), lambda bi, r, i: (bi, i, r))]
    out_shape = [jax.ShapeDtypeStruct((b, l, nrep * qw), CDT)]
    if want_lse:
        out_specs.append(pl.BlockSpec((None, tq, qw), lambda bi, r, i: (bi, i, r)))
        out_shape.append(jax.ShapeDtypeStruct((b, l, nrep * qw), F32))
    res = pl.pallas_call(
        functools.partial(_banded_kernel, window=window, pr=pr, tq=tq, kw=kw, want_lse=want_lse,
                          has_sink=sink_row is not None),
        grid=(b, nrep, l // tq),
        in_specs=in_specs,
        out_specs=out_specs,
        out_shape=out_shape,
        scratch_shapes=[pltpu.VMEM((pr + tq, kw), ka.dtype), pltpu.VMEM((pr + tq, kw), va.dtype),
                        pltpu.VMEM((6 * SUB_Q, SUB_Q + pr), F32), pltpu.VMEM((6 * SUB_Q, SUB_Q + pr), CDT),
                        pltpu.VMEM((6 * SUB_Q, LANES), F32), pltpu.VMEM((6 * SUB_Q, LANES), F32)],
        compiler_params=_cparams(("parallel", "parallel", "parallel"), 8),
        name=f"banded_w{window}_k{kw}_r{nrep}",
    )(*args)
    return res


def _gelu_tanh(x):
    return 0.5 * x * (1.0 + jnp.tanh(0.7978845608028654 * (x + 0.044715 * (x * x * x))))


def _compress_kernel(x_ref, pa_ref, pb_ref, w1a_ref, w1b_ref, b1_ref, w2_ref, b2_ref, o_ref):
    x = x_ref[...].astype(F32)
    n = x.shape[0]
    a = _dot((x + pa_ref[...]).astype(CDT), w1a_ref[...])
    bm = _dot((x + pb_ref[...]).astype(CDT), w1b_ref[...])
    h = a + pltpu.roll(bm, n - 1, 0) + b1_ref[...]
    o_ref[...] = (_dot(_gelu_tanh(h).astype(CDT), w2_ref[...]) + b2_ref[...]).astype(o_ref.dtype)


def _compress(t, pos, w1, b1, w2, b2):
    b, s, _ = t.shape
    nch = s // CMP_STRIDE
    xw = CMP_STRIDE * LANES
    x = t.reshape(b, nch, xw)
    eye = jnp.eye(NSA_KV_HEADS, dtype=F32)
    w1r = w1.reshape(CMP_BLOCK, HEAD_DIM, CMP_HIDDEN)

    def expand_w1(part):
        return jnp.einsum('tdj,kl->tkdlj', part, eye).reshape(xw, NSA_KV_HEADS * CMP_HIDDEN).astype(CDT)

    def expand_pos(part):
        return jnp.broadcast_to(part[:, None, :], (CMP_STRIDE, NSA_KV_HEADS, HEAD_DIM)).reshape(1, xw)

    w1a, w1b = expand_w1(w1r[:CMP_STRIDE]), expand_w1(w1r[CMP_STRIDE:])
    pa, pb = expand_pos(pos[:CMP_STRIDE]), expand_pos(pos[CMP_STRIDE:])
    b1e = jnp.tile(b1, NSA_KV_HEADS).reshape(1, -1)
    w2e = jnp.einsum('jd,kl->kjld', w2, eye).reshape(NSA_KV_HEADS * CMP_HIDDEN, LANES).astype(CDT)
    b2e = jnp.tile(b2, NSA_KV_HEADS).reshape(1, -1)
    consts = [pa, pb, w1a, w1b, b1e, w2e, b2e]
    return pl.pallas_call(
        _compress_kernel,
        grid=(b,),
        in_specs=[pl.BlockSpec((None, nch, xw), lambda bi: (bi, 0, 0))]
        + [pl.BlockSpec(c.shape, lambda bi: (0, 0)) for c in consts],
        out_specs=pl.BlockSpec((None, nch, LANES), lambda bi: (bi, 0, 0)),
        out_shape=jax.ShapeDtypeStruct((b, nch, LANES), CDT),
        compiler_params=_cparams(("parallel",), 12),
        name="nsa_compress",
    )(x, *consts)


CMP_ROWS = 16


def _cmp_select_kernel(q_ref, kc_ref, vc_ref, ov_ref, o_ref, sel_ref, s_ref, p_ref, hi_ref, lo_ref,
                       *, tq, n_sel, tile0):
    i = pl.program_id(1) + tile0
    ncp = kc_ref.shape[0]
    ns = ov_ref.shape[1]
    lo = _half_masks()
    t_col = i * tq + lax.broadcasted_iota(jnp.int32, (tq, 1), 0)
    blk = lax.broadcasted_iota(jnp.int32, (tq, ns), 1)
    blk_t = lax.broadcasted_iota(jnp.int32, (ns, tq), 0)
    cur = t_col // SEL_BLOCK
    causal = blk <= cur
    forced = (blk == 0) | (blk == cur) | (blk == cur - 1)
    rb = CMP_ROWS
    c_end = lax.broadcasted_iota(jnp.int32, (rb, ncp), 1) * CMP_STRIDE + (CMP_BLOCK - 1)
    o_kv, work_t = [], []
    for kv, hm in enumerate((lo, jnp.logical_not(lo))):
        qs = jnp.concatenate(
            [(jnp.where(hm, q_ref[:, p * LANES:(p + 1) * LANES], 0) * QK_SCALE).astype(CDT) for p in range(3)], axis=0)
        s_ref[...] = _nt_dot(qs, kc_ref[...])
        for r0 in range(0, tq, rb):
            t_rows = i * tq + r0 + lax.broadcasted_iota(jnp.int32, (rb, 1), 0)
            vis_bias = jnp.where(c_end <= t_rows, 0.0, NEG_INF)
            has_visible = t_rows >= CMP_BLOCK - 1
            psum = jnp.zeros((rb, ncp), F32)
            for h in range(3):
                rows = slice(h * tq + r0, h * tq + r0 + rb)
                s = s_ref[rows, :] + vis_bias
                e = jnp.exp(s - jnp.max(s, axis=-1, keepdims=True))
                inv = jnp.where(has_visible, 1.0 / jnp.maximum(jnp.sum(e, axis=-1, keepdims=True), 1e-30), 0.0)
                pn = e * inv
                p_ref[rows, :] = pn.astype(CDT)
                psum = psum + pn
            p_hi = psum.astype(CDT)
            hi_ref[r0:r0 + rb, :] = p_hi
            lo_ref[r0:r0 + rb, :] = (psum - p_hi.astype(F32)).astype(CDT)
        o = _dot(p_ref[...], vc_ref[...])
        o_kv.append([o[p * tq:(p + 1) * tq] for p in range(3)])
        imp = _dot(hi_ref[...], ov_ref[...]) + _dot(lo_ref[...], ov_ref[...])
        work_t.append(jnp.where(causal & jnp.logical_not(forced), imp, -BIG).T)

    blk_lanes = blk_t[:, 0:LANES]

    def pick(_, work):
        m = jnp.max(work, axis=0, keepdims=True)
        idx = jnp.min(jnp.where(work == m, blk_lanes, ns), axis=0, keepdims=True)
        return jnp.where(blk_lanes == idx, -2.0 * BIG, work)

    for kv, start in enumerate(work_t):
        done = jnp.concatenate([lax.fori_loop(0, n_sel - 3, pick, start[:, c0:c0 + LANES], unroll=True)
                                for c0 in range(0, tq, LANES)], axis=1)
        taken = jnp.where((done < -BIG) & (start > -BIG), 1.0, 0.0).T
        selb = jnp.where(forced | (taken > 0.5), 0.0, NEG_INF)
        sel_ref[:, kv * ns:(kv + 1) * ns] = selb.astype(sel_ref.dtype)
    for p in range(3):
        o_ref[:, p * LANES:(p + 1) * LANES] = jnp.where(lo, o_kv[0][p], o_kv[1][p]).astype(o_ref.dtype)


CMP_CAUSAL_SPLITS = 4


def _cmp_select(zr, qcol, kc, vc, tq=256):
    b, s, _ = zr.shape
    ncp = kc.shape[1]
    ns = s // SEL_BLOCK
    n_sel = min(N_SEL, ns)
    assert n_sel >= 3, "selection needs room for the three forced blocks"
    tq = min(tq, s)
    c = np.arange(ncp)[:, None] * CMP_STRIDE
    j = np.arange(ns)[None, :] * SEL_BLOCK
    overlap = ((c < j + SEL_BLOCK) & (c + CMP_BLOCK - 1 >= j)).astype(np.float32)
    overlap[ncp - 1:, :] = 0.0
    ov = jnp.asarray(overlap, CDT)
    qw = 3 * LANES
    n_split = CMP_CAUSAL_SPLITS if (s // tq) % CMP_CAUSAL_SPLITS == 0 and ncp % (16 * CMP_CAUSAL_SPLITS) == 0 else 1
    tiles = s // tq // n_split
    outs, sels = [], []
    for part in range(n_split):
        tile0 = part * tiles
        ncp_part = ncp * (part + 1) // n_split
        o_part, sel_part = pl.pallas_call(
            functools.partial(_cmp_select_kernel, tq=tq, n_sel=n_sel, tile0=tile0),
            grid=(b, tiles),
            in_specs=[pl.BlockSpec((None, tq, qw), lambda bi, i, tile0=tile0: (bi, i + tile0, qcol)),
                      pl.BlockSpec((None, ncp_part, LANES), lambda bi, i: (bi, 0, 0)),
                      pl.BlockSpec((None, ncp_part, LANES), lambda bi, i: (bi, 0, 0)),
                      pl.BlockSpec((ncp_part, ns), lambda bi, i: (0, 0))],
            out_specs=[pl.BlockSpec((None, tq, qw), lambda bi, i: (bi, i, 0)),
                       pl.BlockSpec((None, tq, 2 * ns), lambda bi, i: (bi, i, 0))],
            out_shape=[jax.ShapeDtypeStruct((b, tiles * tq, qw), CDT),
                       jax.ShapeDtypeStruct((b, tiles * tq, 2 * ns), CDT)],
            scratch_shapes=[pltpu.VMEM((3 * tq, ncp_part), F32), pltpu.VMEM((3 * tq, ncp_part), CDT),
                            pltpu.VMEM((tq, ncp_part), CDT), pltpu.VMEM((tq, ncp_part), CDT)],
            compiler_params=_cparams(("parallel", "parallel"), 12),
            name=f"nsa_cmp_select_p{part}",
        )(zr, kc, vc, ov)
        outs.append(o_part)
        sels.append(sel_part)
    return jnp.concatenate(outs, axis=1), jnp.concatenate(sels, axis=1)


SLC_KEY_CHUNK = 128


def _slc_kernel(q_ref, k0_ref, k1_ref, vt_ref, selb_ref, o_ref, s_ref, e_ref, m_ref, l_ref, a_ref, acc_ref,
                *, tq, tk):
    i = pl.program_id(1)
    ns = selb_ref.shape[1] // 2
    bpt = tk // SEL_BLOCK
    n_q = 3 * tq
    rc = min(SLC_KEY_CHUNK, tk)
    lo = _half_masks()
    n_tiles = ((i + 1) * tq + tk - 1) // tk
    p_row = lax.broadcasted_iota(jnp.int32, (ns, LANES), 0)
    p_col = lax.broadcasted_iota(jnp.int32, (ns, LANES), 1)
    groups = []
    for kv, (hm, k_ref) in enumerate(((lo, k0_ref), (jnp.logical_not(lo), k1_ref))):
        q3 = [(jnp.where(hm, q_ref[:, p * LANES:(p + 1) * LANES], 0) * QK_SCALE).astype(CDT) for p in range(3)]
        selb = selb_ref[:, kv * ns:(kv + 1) * ns]
        lane0 = HEAD_DIM if kv == 0 else 0
        groups.append((kv, hm, k_ref, q3, selb, lane0))
    m_ref[...] = jnp.full(m_ref.shape, NEG_INF, F32)
    l_ref[...] = jnp.zeros(l_ref.shape, F32)
    acc_ref[...] = jnp.zeros(acc_ref.shape, F32)

    def scores(j, nk):
        k0 = pl.multiple_of(j * tk, tk)
        for kv, hm, k_ref, q3, selb, lane0 in groups:
            place = ((p_col >= lane0) & (p_col < lane0 + bpt) & (p_row == p_col - lane0 + j * bpt)).astype(CDT)
            sb = _dot(selb, place).astype(CDT)
            qp = jnp.concatenate([jnp.where(hm, q, sb) for q in q3], axis=0)
            s_ref[kv, 0:nk, :] = _nt_dot(k_ref[pl.ds(k0, nk), :], qp)

    def softmax_pv(j, diagonal, nk):
        k0 = pl.multiple_of(j * tk, tk)
        for kv in range(2):
            for c0 in range(0, n_q, LANES):
                cols = slice(c0, c0 + LANES)
                t_lane = i * tq + (c0 % tq) + lax.broadcasted_iota(jnp.int32, (1, LANES), 1)

                def chunk(r0):
                    s = s_ref[kv, r0:r0 + rc, cols]
                    if diagonal:
                        kpos = k0 + r0 + lax.broadcasted_iota(jnp.int32, (rc, LANES), 0)
                        s = jnp.where(kpos <= t_lane, s, NEG_INF)
                    return s

                m8 = m_ref[kv, :, cols]
                for r0 in range(0, nk, rc):
                    m8 = jnp.maximum(m8, jnp.max(chunk(r0).reshape(rc // 8, 8, LANES), axis=0))
                m_new = jnp.max(m8, axis=0, keepdims=True)
                alpha = jnp.exp(m_ref[kv, :, cols] - m_new)
                l8 = jnp.zeros((8, LANES), F32)
                for r0 in range(0, nk, rc):
                    e = jnp.exp(chunk(r0) - m_new)
                    l8 = l8 + jnp.sum(e.reshape(rc // 8, 8, LANES), axis=0)
                    e_ref[kv, r0:r0 + rc, cols] = e.astype(CDT)
                l_ref[kv, :, cols] = alpha * l_ref[kv, :, cols] + jnp.sum(l8, axis=0, keepdims=True)
                m_ref[kv, :, cols] = jnp.broadcast_to(m_new, (8, LANES))
                a_ref[kv, :, cols] = alpha
        for kv in range(2):
            acc_ref[kv] = a_ref[kv, 0:1, :] * acc_ref[kv] + _dot(vt_ref[j, :, 0:nk], e_ref[kv, 0:nk, :])

    def tile(j, diagonal, nk=tk):
        scores(j, nk)
        softmax_pv(j, diagonal, nk)

    lax.fori_loop(0, n_tiles - 1, lambda j, c: (tile(j, False), c)[1], 0)
    last = n_tiles - 1
    half = tk // 2
    first_half_only = (i + 1) * tq - last * tk <= half

    @pl.when(first_half_only)
    def _():
        tile(last, True, half)

    @pl.when(jnp.logical_not(first_half_only))
    def _():
        tile(last, True)
    o0 = (acc_ref[0] / l_ref[0, 0:1, :]).T
    o1 = (acc_ref[1] / l_ref[1, 0:1, :]).T
    for p in range(3):
        o_ref[:, p * LANES:(p + 1) * LANES] = jnp.where(
            lo, o0[p * tq:(p + 1) * tq], o1[p * tq:(p + 1) * tq]).astype(o_ref.dtype)


def _slc(zr, qcol, ksc, vt, selb, tq=512):
    b, s, _ = zr.shape
    tq = min(tq, s)
    tk = vt.shape[2]
    assert tq % LANES == 0 and tk // SEL_BLOCK <= HEAD_DIM and s % tk == 0
    qw = 3 * LANES
    ns2 = selb.shape[2]
    pat = jax.nn.one_hot((jnp.arange(s) // SEL_BLOCK) % (tk // SEL_BLOCK), HEAD_DIM, dtype=ksc.dtype)
    pat = jnp.broadcast_to(pat[None], (b, s, HEAD_DIM))
    k0 = jnp.concatenate([ksc[..., :HEAD_DIM], pat], axis=-1)
    k1 = jnp.concatenate([pat, ksc[..., HEAD_DIM:]], axis=-1)
    full = lambda bi, i: (bi, 0, 0)
    return pl.pallas_call(
        functools.partial(_slc_kernel, tq=tq, tk=tk),
        grid=(b, s // tq),
        in_specs=[pl.BlockSpec((None, tq, qw), lambda bi, i: (bi, i, qcol)),
                  _single_buffered((None, s, LANES), full),
                  _single_buffered((None, s, LANES), full),
                  _single_buffered((s // tk, LANES, tk), lambda bi, i: (bi, 0, 0)),
                  pl.BlockSpec((None, tq, ns2), lambda bi, i: (bi, i, 0))],
        out_specs=pl.BlockSpec((None, tq, qw), lambda bi, i: (bi, i, 0)),
        out_shape=jax.ShapeDtypeStruct((b, s, qw), CDT),
        scratch_shapes=[pltpu.VMEM((2, tk, 3 * tq), F32), pltpu.VMEM((2, tk, 3 * tq), CDT),
                        pltpu.VMEM((2, 8, 3 * tq), F32), pltpu.VMEM((2, 8, 3 * tq), F32),
                        pltpu.VMEM((2, 8, 3 * tq), F32), pltpu.VMEM((2, LANES, 3 * tq), F32)],
        compiler_params=_cparams(("parallel", "arbitrary"), 14),
        name="nsa_slc",
    )(zr, k0, k1, vt, selb)


def _merge_kernel(x_ref, oa0, oa1, oa2, la0, la1, la2, ob, ocmp, oslc, owin, od,
                  wn_ref, wg_ref, bg_ref, wb_ref, wo_ref, g_ref, b_ref, o_ref, *scratch, alpha):
    x = x_ref[...]
    xb = x.astype(CDT)
    bw = BRANCH_WIDTH

    def token_rows(src_ref, dst_ref):
        dil = src_ref.shape[1] // bw
        if dil == 1:
            return src_ref[...].astype(F32)
        n_cb = bw // LANES
        for r in range(dil):
            for cb in range(n_cb):
                c0 = r * bw + cb * LANES
                dst_ref[cb, pl.ds(r, src_ref.shape[0], stride=dil), :] = src_ref[:, c0:c0 + LANES].astype(F32)
        return jnp.concatenate([dst_ref[cb] for cb in range(n_cb)], axis=1)

    o0, o1, o2 = (token_rows(s, d) for s, d in zip((oa0, oa1, oa2), scratch[0:3]))
    l0, l1, l2 = (token_rows(s, d) for s, d in zip((la0, la1, la2), scratch[3:6]))
    mx = jnp.maximum(jnp.maximum(l0, l1), l2)
    w0, w1, w2 = jnp.exp(l0 - mx), jnp.exp(l1 - mx), jnp.exp(l2 - mx)
    o_a = (w0 * o0 + w1 * o1 + w2 * o2) / (w0 + w1 + w2)
    gates = jax.nn.sigmoid(_dot(xb, wn_ref[...]))
    o_c = (gates[:, 0:bw] * ocmp[...].astype(F32) + gates[:, bw:2 * bw] * oslc[...].astype(F32)
           + gates[:, 2 * bw:3 * bw] * owin[...].astype(F32))
    branches = (o_a.astype(CDT), ob[...], o_c.astype(CDT), od[...])
    d = x.shape[1]
    merged = jnp.zeros(x.shape, F32)
    for m in range(N_BRANCH):
        gate = jax.nn.sigmoid(_dot(xb, wg_ref[:, m * d:(m + 1) * d]) + bg_ref[:, m * d:(m + 1) * d])
        merged = merged + gate * _dot(branches[m], wb_ref[m])
    r = alpha * x + _dot(merged.astype(CDT), wo_ref[...])
    o_ref[...] = _layer_norm(r, g_ref[...], b_ref[...])


def _merge(x2d, branch_inputs, wn, wg, bg, wb, wo, g, b, alpha, tm=256):
    t, d = x2d.shape
    tm = min(tm, t)
    row = lambda i: (i, 0)
    const2 = lambda i: (0, 0)
    in_specs = [pl.BlockSpec((tm, d), row)]
    in_specs += [pl.BlockSpec((tm * a.shape[0] // t, a.shape[1]), row) for a in branch_inputs]
    in_specs += [pl.BlockSpec(wn.shape, const2), pl.BlockSpec(wg.shape, const2), pl.BlockSpec(bg.shape, const2),
                 pl.BlockSpec(wb.shape, lambda i: (0, 0, 0)), pl.BlockSpec(wo.shape, const2),
                 pl.BlockSpec(g.shape, const2), pl.BlockSpec(b.shape, const2)]
    return pl.pallas_call(
        functools.partial(_merge_kernel, alpha=alpha),
        grid=(t // tm,),
        in_specs=in_specs,
        out_specs=pl.BlockSpec((tm, d), row),
        out_shape=jax.ShapeDtypeStruct((t, d), F32),
        scratch_shapes=[pltpu.VMEM((BRANCH_WIDTH // LANES, tm, LANES), F32) for _ in range(2 * N_DIL)],
        compiler_params=_cparams(("parallel",), 14),
        name="merge_ln",
    )(x2d, *branch_inputs, wn, wg, bg, wb, wo, g, b)


def _ple_ln(x, xb, f, p, plw_ref, pgw_ref, pgb_ref, g_ref, b_ref, alpha):
    ple = jax.nn.sigmoid(_dot(xb, pgw_ref[...]) + pgb_ref[...]) * _dot(p.astype(CDT), plw_ref[...])
    return _layer_norm(alpha * x + f + ple, g_ref[...], b_ref[...])


FFN_CHUNK = 512


def _ffn_kernel(x_ref, p_ref, wg_ref, wu_ref, wd_ref, plw_ref, pgw_ref, pgb_ref, g_ref, b_ref, o_ref, h_ref, *, alpha):
    x = x_ref[...]
    xb = x.astype(CDT)
    dff = wg_ref.shape[1]
    for c0 in range(0, dff, FFN_CHUNK):
        cols = slice(c0, min(c0 + FFN_CHUNK, dff))
        h_ref[:, cols] = (jax.nn.silu(_dot(xb, wg_ref[:, cols])) * _dot(xb, wu_ref[:, cols])).astype(CDT)
    f = _dot(h_ref[...], wd_ref[...])
    o_ref[...] = _ple_ln(x, xb, f, p_ref[...], plw_ref, pgw_ref, pgb_ref, g_ref, b_ref, alpha)


def _ffn(x2d, p2d, wg, wu, wd, plw, pgw, pgb, g, b, alpha, tm=512):
    t, d = x2d.shape
    tm = min(tm, t)
    dff = wg.shape[1]
    wg, wu, wd = wg.astype(CDT), wu.astype(CDT), wd.astype(CDT)
    row = lambda i: (i, 0)
    const = lambda shape: pl.BlockSpec(shape, lambda i: (0, 0), pipeline_mode=pl.Buffered(1))
    return pl.pallas_call(
        functools.partial(_ffn_kernel, alpha=alpha),
        grid=(t // tm,),
        in_specs=[pl.BlockSpec((tm, d), row), pl.BlockSpec((tm, p2d.shape[1]), row),
                  const(wg.shape), const(wu.shape), const(wd.shape),
                  const(plw.shape), const(pgw.shape), const(pgb.shape), const(g.shape), const(b.shape)],
        out_specs=pl.BlockSpec((tm, d), row),
        out_shape=jax.ShapeDtypeStruct((t, d), F32),
        scratch_shapes=[pltpu.VMEM((tm, dff), CDT)],
        compiler_params=_cparams(("parallel",), 12),
        name="ffn_ple_ln",
    )(x2d, p2d, wg, wu, wd, plw, pgw, pgb, g, b)


def _router_kernel(x_ref, wh_ref, wl_ref, b_ref, comb_ref, rank_ref, rank_t_ref, cnt_ref):
    x = x_ref[...]
    xh = x.astype(CDT)
    xl = (x - xh.astype(F32)).astype(CDT)
    logits = _dot(xh, wh_ref[...]) + _dot(xh, wl_ref[...]) + _dot(xl, wh_ref[...]) + b_ref[...]
    lane = lax.broadcasted_iota(jnp.int32, logits.shape, 1)
    v1 = jnp.max(logits, axis=-1, keepdims=True)
    i1 = jnp.min(jnp.where(logits == v1, lane, LANES), axis=-1, keepdims=True)
    rest = jnp.where(lane == i1, -jnp.inf, logits)
    v2 = jnp.max(rest, axis=-1, keepdims=True)
    i2 = jnp.min(jnp.where(rest == v2, lane, LANES), axis=-1, keepdims=True)
    e2 = jnp.exp(v2 - v1)
    comb_ref[...] = jnp.where(lane == i1, 1.0 / (1.0 + e2), 0.0) + jnp.where(lane == i2, e2 / (1.0 + e2), 0.0)
    routed = (lane == i1) | (lane == i2)
    mask = routed.astype(CDT)
    tm = x.shape[0]
    before = (lax.broadcasted_iota(jnp.int32, (tm, tm), 1) < lax.broadcasted_iota(jnp.int32, (tm, tm), 0)).astype(CDT)
    rank = jnp.where(routed, _dot(before, mask), -1.0)
    rank_ref[...] = rank
    rank_t_ref[...] = rank.T[0:rank_t_ref.shape[0], :]
    cnt_ref[...] = jnp.sum(routed.astype(F32), axis=0, keepdims=True).astype(jnp.int32)


def _router(x2d, w_router, b_router, tm):
    t, d = x2d.shape
    ne = w_router.shape[1]
    wp = jnp.zeros((d, LANES), F32).at[:, :ne].set(w_router)
    wh = wp.astype(CDT)
    wl = (wp - wh.astype(F32)).astype(CDT)
    bp = jnp.full((1, LANES), -BIG, F32).at[0, :ne].set(b_router)
    nt = t // tm
    row = lambda i: (i, 0)
    return pl.pallas_call(
        _router_kernel,
        grid=(nt,),
        in_specs=[pl.BlockSpec((tm, d), row), pl.BlockSpec(wh.shape, lambda i: (0, 0)),
                  pl.BlockSpec(wl.shape, lambda i: (0, 0)), pl.BlockSpec(bp.shape, lambda i: (0, 0))],
        out_specs=[pl.BlockSpec((tm, LANES), row), pl.BlockSpec((tm, LANES), row),
                   pl.BlockSpec((None, 8, tm), lambda i: (i, 0, 0)),
                   pl.BlockSpec((None, 1, LANES), lambda i: (i, 0, 0))],
        out_shape=[jax.ShapeDtypeStruct((t, LANES), F32), jax.ShapeDtypeStruct((t, LANES), F32),
                   jax.ShapeDtypeStruct((nt, 8, tm), F32), jax.ShapeDtypeStruct((nt, 1, LANES), jnp.int32)],
        compiler_params=_cparams(("parallel",), 10),
        name="moe_router",
    )(x2d, wh, wl, bp)


def _moe_kernel(cnt_ref, x_ref, comb_ref, rank_ref, rank_t_ref, p_ref, wg_ref, wu_ref, wd_ref, plw_ref, pgw_ref,
                pgb_ref, g_ref, b_ref, o_ref, xe_ref, ye_ref, *, alpha, rs, seg):
    i = pl.program_id(0)
    e = pl.program_id(1)
    c = pl.program_id(2)
    n_seg = x_ref.shape[0] // seg
    last_chunk = c == pl.num_programs(2) - 1

    first = slice(0, rs)
    toks = [slice(sg * seg, (sg + 1) * seg) for sg in range(n_seg)]
    n_groups = [(cnt_ref[(i * n_seg + sg) * LANES + e] + rs - 1) // rs for sg in range(n_seg)]

    def later(sc):
        return pl.ds(pl.multiple_of(sc * rs, 8), rs), (sc * rs).astype(F32)

    @pl.when((e == 0) & (c == 0))
    def _():
        o_ref[...] = jnp.zeros_like(o_ref)

    @pl.when(c == 0)
    def _():
        row_id = lax.broadcasted_iota(jnp.int32, (rs, seg), 0).astype(F32)
        xbs = [x_ref[tok, :].astype(CDT) for tok in toks]
        rank_rows = [rank_t_ref[sg, pl.ds(e, 1), :] for sg in range(n_seg)]

        def gather(sg, rws, base):
            onehot = (rank_rows[sg] - base == row_id).astype(CDT)
            xe_ref[sg, rws, :] = _dot(onehot, xbs[sg]).astype(CDT)
            ye_ref[sg, rws, :] = jnp.zeros((rs, ye_ref.shape[2]), F32)

        for sg in range(n_seg):
            gather(sg, first, 0.0)
        for sg in range(n_seg):
            lax.fori_loop(1, n_groups[sg], lambda sc, _, sg=sg: (gather(sg, *later(sc)), 0)[1], 0)

    def expert(sg, rws):
        xs = xe_ref[sg, rws, :]
        h = jax.nn.silu(_dot(xs, wg_ref[0])) * _dot(xs, wu_ref[0])
        ye_ref[sg, rws, :] += _dot(h.astype(CDT), wd_ref[0])

    for sg in range(n_seg):
        expert(sg, first)
    for sg in range(n_seg):
        lax.fori_loop(1, n_groups[sg], lambda sc, _, sg=sg: (expert(sg, later(sc)[0]), 0)[1], 0)

    @pl.when(last_chunk)
    def _():
        lane = lax.broadcasted_iota(jnp.int32, (seg, LANES), 1)
        mine = lane == e
        col_id = lax.broadcasted_iota(jnp.int32, (seg, rs), 1).astype(F32)
        cws = [jnp.sum(jnp.where(mine, comb_ref[tok, :], 0.0), axis=-1, keepdims=True) for tok in toks]
        rank_cols = [jnp.sum(jnp.where(mine, rank_ref[tok, :], 0.0), axis=-1, keepdims=True) for tok in toks]

        def scatter(sg, rws, base):
            onehot = (rank_cols[sg] - base == col_id).astype(CDT)
            o_ref[toks[sg], :] += cws[sg] * _dot(onehot, ye_ref[sg, rws, :].astype(CDT))

        for sg in range(n_seg):
            scatter(sg, first, 0.0)
        for sg in range(n_seg):
            lax.fori_loop(1, n_groups[sg], lambda sc, _, sg=sg: (scatter(sg, *later(sc)), 0)[1], 0)

    @pl.when((e == pl.num_programs(1) - 1) & last_chunk)
    def _():
        for sg in range(n_seg):
            tok = slice(sg * seg, (sg + 1) * seg)
            x = x_ref[tok, :]
            o_ref[tok, :] = _ple_ln(x, x.astype(CDT), o_ref[tok, :], p_ref[tok, :], plw_ref, pgw_ref, pgb_ref,
                                    g_ref, b_ref, alpha)


MOE_CHUNK = 512
MOE_SEGMENT = 1024
MOE_SEGMENTS_PER_TILE = 2
MOE_ROW_GROUP = 288


def _single_buffered(shape, index_map):
    return pl.BlockSpec(shape, index_map, pipeline_mode=pl.Buffered(1))


def _moe(x2d, routing, p2d, wg, wu, wd, plw, pgw, pgb, g, b, alpha, seg):
    comb, rank, rank_t, cnt = routing
    t, d = x2d.shape
    ne, _, dff = wg.shape
    ck = min(MOE_CHUNK, dff)
    rs = min(MOE_ROW_GROUP, seg)
    n_seg = min(MOE_SEGMENTS_PER_TILE, t // seg)
    tm = n_seg * seg
    max_rows = -(-seg // rs) * rs
    wg, wu, wd = wg.astype(CDT), wu.astype(CDT), wd.astype(CDT)
    row = lambda i, e, c, cnt: (i, 0)
    c2 = lambda i, e, c, cnt: (0, 0)
    grid_spec = pltpu.PrefetchScalarGridSpec(
        num_scalar_prefetch=1,
        grid=(t // tm, ne, dff // ck),
        in_specs=[_single_buffered((tm, d), row), _single_buffered((tm, LANES), row),
                  _single_buffered((tm, LANES), row),
                  _single_buffered((n_seg, 8, seg), lambda i, e, c, cnt: (i, 0, 0)),
                  _single_buffered((tm, p2d.shape[1]), row),
                  pl.BlockSpec((1, d, ck), lambda i, e, c, cnt: (e, 0, c)),
                  pl.BlockSpec((1, d, ck), lambda i, e, c, cnt: (e, 0, c)),
                  pl.BlockSpec((1, ck, d), lambda i, e, c, cnt: (e, c, 0)),
                  _single_buffered(plw.shape, c2), _single_buffered(pgw.shape, c2), _single_buffered(pgb.shape, c2),
                  _single_buffered(g.shape, c2), _single_buffered(b.shape, c2)],
        out_specs=pl.BlockSpec((tm, d), row),
        scratch_shapes=[pltpu.VMEM((n_seg, max_rows, d), CDT), pltpu.VMEM((n_seg, max_rows, d), F32)],
    )
    return pl.pallas_call(
        functools.partial(_moe_kernel, alpha=alpha, rs=rs, seg=seg),
        grid_spec=grid_spec,
        out_shape=jax.ShapeDtypeStruct((t, d), F32),
        compiler_params=_cparams(("parallel", "arbitrary", "arbitrary"), 15),
        name="moe_ple_ln",
    )(cnt.reshape(-1), x2d, comb, rank, rank_t, p2d, wg, wu, wd, plw, pgw, pgb, g, b)


def _prep_in_weights(w_in):
    o = COL_OFF
    bw = BRANCH_WIDTH
    cols = lambda n: w_in[:, o[n]:o[n + 1]]
    qa, ka, va = cols(0), cols(1), cols(2)
    w_dil = jnp.concatenate(
        [t[:, g * bw:(g + 1) * bw] for g in range(N_DIL) for t in (qa, ka, va)], axis=1).astype(CDT)
    w_conv = jnp.concatenate([cols(3), cols(4), cols(5)], axis=1).astype(CDT)
    gn = cols(13)
    w_gate = jnp.concatenate([gn[:, br * NSA_Q_HEADS + GQA_COL_HEAD] for br in range(3)], axis=1).astype(CDT)
    w_rest = jnp.concatenate([cols(6)[:, GQA_COL_PERM], cols(14)[:, GQA_COL_PERM], cols(11),
                              cols(12), cols(15), cols(16), cols(9), cols(7), cols(8)], axis=1).astype(CDT)
    wt_vsc = cols(10).T.astype(CDT)
    return w_dil, w_conv, w_gate, w_rest, wt_vsc


ZR_Q_NSA, ZR_Q_SWA = 0, 1
ZR_KWC, ZR_VWC, ZR_KD, ZR_VD = 6, 7, 8, 9
ZR_WIDTH = 2 * BRANCH_WIDTH + 4 * LANES
SLC_KEY_TILE = 1024


def _token_mixers(x, w_in, conv_w, cmp_pos, cmp_w1, cmp_b1, cmp_w2, cmp_b2, sinks):
    b, s, d = x.shape
    x2d = x.reshape(b * s, d)
    w_dil, w_conv, w_gate, w_rest, wt_vsc = _prep_in_weights(w_in)
    gw = 3 * BRANCH_WIDTH

    z_dil = _linear(x2d, w_dil, [(g * gw, (g + 1) * gw) for g in range(N_DIL)], gw, "in_proj_dil",
                    dils=[dil for _, dil in DIL_PATTERNS])
    zr, ksc, kcc, vcc = _linear(x2d, w_rest, [(0, ZR_WIDTH)] + [(ZR_WIDTH + n * LANES, ZR_WIDTH + (n + 1) * LANES)
                                                           for n in range(3)], 256, "in_proj_rest")
    zr = zr.reshape(b, s, ZR_WIDTH)
    o_b = _conv(x, w_conv, conv_w)

    dil_o, dil_lse = [], []
    for g, (window, dil) in enumerate(DIL_PATTERNS):
        view = z_dil[g].reshape(b, s // dil, dil * gw)
        og, lg = _banded(view, view, view, nrep=dil,
                         qcol=lambda r: 3 * r, kcol=lambda r: 3 * r + 1, vcol=lambda r: 3 * r + 2,
                         kw=3 * LANES, window=window // dil, want_lse=True)
        dil_o.append(og.reshape(b * s // dil, dil * BRANCH_WIDTH))
        dil_lse.append(lg.reshape(b * s // dil, dil * BRANCH_WIDTH))

    kc = _compress(kcc.reshape(b, s, LANES), cmp_pos[0], cmp_w1[0], cmp_b1[0], cmp_w2[0], cmp_b2[0])
    vc = _compress(vcc.reshape(b, s, LANES), cmp_pos[1], cmp_w1[1], cmp_b1[1], cmp_w2[1], cmp_b2[1])
    o_cmp, selb = _cmp_select(zr, ZR_Q_NSA, kc, vc)
    vsc_t = _linear_t(x2d, wt_vsc, min(SLC_KEY_TILE, s), "in_proj_vsc_t")
    o_slc = _slc(zr, ZR_Q_NSA, ksc.reshape(b, s, LANES), vsc_t, selb)
    (o_win,) = _banded(zr, zr, zr, nrep=1, qcol=lambda r: ZR_Q_NSA, kcol=lambda r: ZR_KWC, vcol=lambda r: ZR_VWC,
                       kw=LANES, window=NSA_WINDOW - 1, want_lse=False, tq=512)

    sink_row = sinks.astype(F32)[GQA_COL_HEAD].reshape(1, BRANCH_WIDTH)
    (o_d,) = _banded(zr, zr, zr, nrep=1, qcol=lambda r: ZR_Q_SWA, kcol=lambda r: ZR_KD, vcol=lambda r: ZR_VD,
                     kw=LANES, window=SWA_WINDOW - 1, want_lse=False, sink_row=sink_row)

    t = b * s
    flat = lambda a: a.reshape(t, a.shape[-1])
    return [dil_o[0], dil_o[1], dil_o[2], dil_lse[0], dil_lse[1], dil_lse[2], flat(o_b), flat(o_cmp), flat(o_slc),
            flat(o_win), flat(o_d)], w_gate


def kernel(x, p, w_in, conv_w, cmp_pos, cmp_w1, cmp_b1, cmp_w2, cmp_b2, sinks, w_branch, w_merge_gate, b_merge_gate, w_out, ln_mix_g, ln_mix_b, ffn_w_gate, ffn_w_up, ffn_w_down, w_router, b_router, moe_w_gate, moe_w_up, moe_w_down, ple_w, ple_gate_w, ple_gate_b, ln_ffn_g, ln_ffn_b):
    depth, b, s, _ = p.shape
    d = x.shape[-1]
    t = b * s
    alpha = (2 * depth) ** 0.25
    row = lambda v: v.reshape(1, -1).astype(F32)
    for i in range(depth):
        branch_inputs, w_nsa_gate = _token_mixers(x, w_in[i], conv_w[i], cmp_pos[i], cmp_w1[i], cmp_b1[i], cmp_w2[i],
                                                  cmp_b2[i], sinks[i])
        wg = jnp.concatenate([w_merge_gate[i, m] for m in range(N_BRANCH)], axis=1).astype(CDT)
        bg = b_merge_gate[i].reshape(1, N_BRANCH * d).astype(F32)
        wb = jnp.stack([w_branch[i, 0], w_branch[i, 1], w_branch[i, 2][GQA_COL_PERM],
                        w_branch[i, 3][GQA_COL_PERM]]).astype(CDT)
        x1 = _merge(x.reshape(t, d), branch_inputs, w_nsa_gate, wg, bg, wb, w_out[i].astype(CDT),
                    row(ln_mix_g[i]), row(ln_mix_b[i]), alpha)
        p2d = p[i].reshape(t, -1)
        ple_args = (ple_w[i].astype(CDT), ple_gate_w[i].astype(CDT), row(ple_gate_b[i]),
                    row(ln_ffn_g[i]), row(ln_ffn_b[i]))
        j = i // 2
        if i % 2 == 0:
            x2 = _ffn(x1, p2d, ffn_w_gate[j], ffn_w_up[j], ffn_w_down[j], *ple_args, alpha)
        else:
            seg = min(MOE_SEGMENT, t)
            routing = _router(x1, w_router[j], b_router[j], seg)
            x2 = _moe(x1, routing, p2d, moe_w_gate[j], moe_w_up[j], moe_w_down[j], *ple_args, alpha, seg)
        x = x2.reshape(b, s, d)
    return x
```

```python
import functools

import numpy as np
import jax
import jax.numpy as jnp
from jax import lax
from jax.experimental import pallas as pl
from jax.experimental.pallas import tpu as pltpu

HEAD_DIM = 64
DIL_PATTERNS = ((128, 1), (512, 4), (2048, 16))
N_DIL = 3
DIL_HEADS = 6
CONV_WIDTH = 384
NSA_Q_HEADS = 6
NSA_KV_HEADS = 2
CMP_BLOCK = 32
CMP_STRIDE = 16
CMP_HIDDEN = 128
SEL_BLOCK = 64
N_SEL = 16
NSA_WINDOW = 512
SWA_Q_HEADS = 6
SWA_WINDOW = 128
BRANCH_WIDTH = 384
N_BRANCH = 4
LN_EPS = 1e-5
NEG_INF = -1e30
DIL_WIDTH = N_DIL * DIL_HEADS * HEAD_DIM
COLUMN_SIZES = (DIL_WIDTH, DIL_WIDTH, DIL_WIDTH, CONV_WIDTH, CONV_WIDTH, CONV_WIDTH,
                NSA_Q_HEADS * HEAD_DIM, 128, 128, 128, 128, 128, 128, 3 * NSA_Q_HEADS,
                SWA_Q_HEADS * HEAD_DIM, 128, 128)
COL_OFF = np.concatenate([[0], np.cumsum(COLUMN_SIZES)]).tolist()

LANES = 128
V7X_VMEM_BYTES = 64 * 1024 * 1024

CDT = jnp.bfloat16
F32 = jnp.float32
QK_SCALE = HEAD_DIM ** -0.5
SUB_Q = 128
BAND_ROWS = 32
BIG = 1e30

_GQA_HEAD_ORDER = (0, 3, 1, 4, 2, 5)
GQA_COL_PERM = np.concatenate([np.arange(h * HEAD_DIM, (h + 1) * HEAD_DIM) for h in _GQA_HEAD_ORDER])
GQA_COL_HEAD = GQA_COL_PERM // HEAD_DIM


def _cparams(sem, vmem_sixteenths):
    return pltpu.CompilerParams(dimension_semantics=sem, vmem_limit_bytes=V7X_VMEM_BYTES * vmem_sixteenths // 16)


def _nt_dot(a, b):
    return lax.dot_general(a, b, (((1,), (1,)), ((), ())), preferred_element_type=F32)


def _dot(a, b):
    return jnp.dot(a, b, preferred_element_type=F32)


def _layer_norm(r, g, b):
    mu = jnp.mean(r, axis=-1, keepdims=True)
    d = r - mu
    var = jnp.mean(d * d, axis=-1, keepdims=True)
    return d * lax.rsqrt(var + LN_EPS) * g + b


def _half_masks():
    lane = lax.broadcasted_iota(jnp.int32, (1, LANES), 1)
    return lane < HEAD_DIM


def _linear_kernel(x_ref, w_ref, *refs, splits, n_chunk, dils):
    o_refs, z_ref = refs[:len(splits)], refs[len(splits)]
    xb = x_ref[...].astype(CDT)
    tm = xb.shape[0]
    for o_ref, (c0, c1), dil in zip(o_refs, splits, dils):
        width = c1 - c0
        if dil == 1:
            for a in range(c0, c1, n_chunk):
                b = min(a + n_chunk, c1)
                o_ref[:, a - c0:b - c0] = _dot(xb, w_ref[:, a:b]).astype(o_ref.dtype)
        else:
            z = _dot(xb, w_ref[:, c0:c1])
            for cb in range(width // LANES):
                z_ref[cb] = z[:, cb * LANES:(cb + 1) * LANES]
            for r in range(dil):
                for cb in range(width // LANES):
                    o_ref[:, r * width + cb * LANES:r * width + (cb + 1) * LANES] = (
                        z_ref[cb, pl.ds(r, tm // dil, stride=dil), :].astype(o_ref.dtype))


def _linear(x2d, w, splits, n_chunk, name, dils=None, tm=512):
    t, k = x2d.shape
    tm = min(tm, t)
    n = w.shape[1]
    dils = tuple(dils) if dils is not None else (1,) * len(splits)
    widths = [c1 - c0 for c0, c1 in splits]
    assert all(tm % (16 * dl) == 0 for dl in dils)
    return pl.pallas_call(
        functools.partial(_linear_kernel, splits=tuple(splits), n_chunk=n_chunk, dils=dils),
        grid=(t // tm,),
        in_specs=[pl.BlockSpec((tm, k), lambda i: (i, 0)),
                  pl.BlockSpec((k, n), lambda i: (0, 0))],
        out_specs=[pl.BlockSpec((tm // dl, dl * wd), lambda i: (i, 0)) for wd, dl in zip(widths, dils)],
        out_shape=[jax.ShapeDtypeStruct((t // dl, dl * wd), CDT) for wd, dl in zip(widths, dils)],
        scratch_shapes=[pltpu.VMEM((max(widths) // LANES, tm, LANES), F32)],
        compiler_params=_cparams(("parallel",), 12),
        name=name,
    )(x2d, w)


def _linear_t_kernel(x_ref, wt_ref, o_ref):
    o_ref[...] = _nt_dot(wt_ref[...], x_ref[...].astype(CDT)).astype(o_ref.dtype)


def _linear_t(x2d, wt, tm, name):
    t, k = x2d.shape
    n = wt.shape[0]
    return pl.pallas_call(
        _linear_t_kernel,
        grid=(t // tm,),
        in_specs=[pl.BlockSpec((tm, k), lambda i: (i, 0)), pl.BlockSpec((n, k), lambda i: (0, 0))],
        out_specs=pl.BlockSpec((None, n, tm), lambda i: (i, 0, 0)),
        out_shape=jax.ShapeDtypeStruct((t // tm, n, tm), CDT),
        compiler_params=_cparams(("parallel",), 8),
        name=name,
    )(x2d, wt)


def _conv_kernel(x_ref, xh_ref, wc_ref, cw_ref, ob_ref, *, tm):
    i = pl.program_id(1)
    w = CONV_WIDTH
    xb = x_ref[...].astype(CDT)
    z = _dot(xb, wc_ref[...])
    u = z[:, w:2 * w] * z[:, 2 * w:3 * w]
    zh = _dot(xh_ref[...].astype(CDT), wc_ref[:, w:3 * w])
    uh = zh[:, :w] * zh[:, w:]
    uh = jnp.where(i == 0, 0.0, uh)
    row = lax.broadcasted_iota(jnp.int32, (tm, w), 0)
    u1 = jnp.where(row == 0, uh[7:8, :], pltpu.roll(u, 1, 0))
    u2 = jnp.where(row == 0, uh[6:7, :], jnp.where(row == 1, uh[7:8, :], pltpu.roll(u, 2, 0)))
    y = cw_ref[0:1, :] * u2 + cw_ref[1:2, :] * u1 + cw_ref[2:3, :] * u
    ob_ref[...] = (z[:, :w] * y).astype(ob_ref.dtype)


def _conv(x, wc, conv_w, tm=512):
    b, s, d = x.shape
    tm = min(tm, s)
    hb = tm // 8
    return pl.pallas_call(
        functools.partial(_conv_kernel, tm=tm),
        grid=(b, s // tm),
        in_specs=[pl.BlockSpec((None, tm, d), lambda bi, i: (bi, i, 0)),
                  pl.BlockSpec((None, 8, d), lambda bi, i: (bi, jnp.maximum(i * hb - 1, 0), 0)),
                  pl.BlockSpec(wc.shape, lambda bi, i: (0, 0)),
                  pl.BlockSpec(conv_w.shape, lambda bi, i: (0, 0))],
        out_specs=pl.BlockSpec((None, tm, CONV_WIDTH), lambda bi, i: (bi, i, 0)),
        out_shape=jax.ShapeDtypeStruct((b, s, CONV_WIDTH), CDT),
        compiler_params=_cparams(("parallel", "parallel"), 12),
        name="short_conv",
    )(x, x, wc, conv_w)


def _banded_kernel(*refs, window, pr, tq, kw, want_lse, has_sink):
    q_ref, kp_ref, kc_ref, vp_ref, vc_ref = refs[:5]
    n = 5
    sink_ref = None
    if has_sink:
        sink_ref = refs[n]
        n += 1
    o_ref = refs[n]
    n += 1
    lse_ref = None
    if want_lse:
        lse_ref = refs[n]
        n += 1
    kbuf, vbuf, s_ref, e_ref, m_ref, l_ref = refs[n:n + 6]

    i = pl.program_id(2)
    kbuf[0:pr, :] = kp_ref[...]
    kbuf[pr:pr + tq, :] = kc_ref[...]
    vbuf[0:pr, :] = vp_ref[...]
    vbuf[pr:pr + tq, :] = vc_ref[...]

    span = SUB_Q + pr
    qi = lax.broadcasted_iota(jnp.int32, (SUB_Q, span), 0)
    kj = lax.broadcasted_iota(jnp.int32, (SUB_Q, span), 1)
    dist = pr + qi - kj
    band = (dist >= 0) & (dist <= window)
    lo = _half_masks()
    halves = (lo, jnp.logical_not(lo))
    groups = ((0,), (1,), (2,)) if kw == 3 * LANES else ((0, 1, 2),)
    rb = BAND_ROWS
    for sb in range(tq // SUB_Q):
        r0 = sb * SUB_Q
        bias = jnp.where(band & (i * tq + r0 - pr + kj >= 0), 0.0, NEG_INF)
        for grp in groups:
            kc0 = grp[0] * LANES if kw == 3 * LANES else 0
            qs = jnp.concatenate(
                [(jnp.where(hm, q_ref[r0:r0 + SUB_Q, p * LANES:(p + 1) * LANES], 0) * QK_SCALE).astype(CDT)
                 for p in grp for hm in halves], axis=0)
            g0 = 2 * grp[0] * SUB_Q
            s_ref[g0:g0 + qs.shape[0], :] = _nt_dot(qs, kbuf[r0:r0 + span, kc0:kc0 + LANES])
        for c0 in range(0, 6 * SUB_Q, rb):
            rows = slice(c0, c0 + rb)
            s = s_ref[rows, :] + bias[c0 % SUB_Q:c0 % SUB_Q + rb, :]
            m = jnp.max(s, axis=-1, keepdims=True)
            e = jnp.exp(s - m)
            e_ref[rows, :] = e.astype(CDT)
            m_ref[rows, :] = jnp.broadcast_to(m, (rb, LANES))
            l_ref[rows, :] = jnp.broadcast_to(jnp.sum(e, axis=-1, keepdims=True), (rb, LANES))
        for grp in groups:
            kc0 = grp[0] * LANES if kw == 3 * LANES else 0
            g0 = 2 * grp[0] * SUB_Q
            g1 = g0 + 2 * len(grp) * SUB_Q
            l = l_ref[g0:g1, :]
            o = _dot(e_ref[g0:g1, :], vbuf[r0:r0 + span, kc0:kc0 + LANES]) / l
            lse = m_ref[g0:g1, :] + jnp.log(l)
            for n_p, p in enumerate(grp):
                a = 2 * n_p * SUB_Q
                o_pair = jnp.where(lo, o[a:a + SUB_Q], o[a + SUB_Q:a + 2 * SUB_Q])
                lse_pair = jnp.where(lo, lse[a:a + SUB_Q], lse[a + SUB_Q:a + 2 * SUB_Q])
                if has_sink:
                    o_pair = o_pair * jax.nn.sigmoid(lse_pair - sink_ref[:, p * LANES:(p + 1) * LANES])
                o_ref[r0:r0 + SUB_Q, p * LANES:(p + 1) * LANES] = o_pair.astype(o_ref.dtype)
                if want_lse:
                    lse_ref[r0:r0 + SUB_Q, p * LANES:(p + 1) * LANES] = lse_pair


def _banded(qa, ka, va, *, nrep, qcol, kcol, vcol, kw, window, want_lse, sink_row=None, tq=1024):
    b, l, _ = qa.shape
    pr = -(-window // SUB_Q) * SUB_Q
    tq = min(max(tq, pr), l)
    assert tq % pr == 0 and l % tq == 0, (tq, pr, l)
    ratio = tq // pr
    qw = 3 * LANES
    in_specs = [
        pl.BlockSpec((None, tq, qw), lambda bi, r, i: (bi, i, qcol(r))),
        pl.BlockSpec((None, pr, kw), lambda bi, r, i: (bi, jnp.maximum(i * ratio - 1, 0), kcol(r))),
        pl.BlockSpec((None, tq, kw), lambda bi, r, i: (bi, i, kcol(r))),
        pl.BlockSpec((None, pr, kw), lambda bi, r, i: (bi, jnp.maximum(i * ratio - 1, 0), vcol(r))),
        pl.BlockSpec((None, tq, kw), lambda bi, r, i: (bi, i, vcol(r))),
    ]
    args = [qa, ka, ka, va, va]
    if sink_row is not None:
        in_specs.append(pl.BlockSpec(sink_row.shape, lambda bi, r, i: (0, 0)))
        args.append(sink_row)
    out_specs = [pl.BlockSpec((None, tq, qw), lambda bi, r, i: (bi, i, r))]
    out_shape = [jax.ShapeDtypeStruct((b, l, nrep * qw), CDT)]
    if want_lse:
        out_specs.append(pl.BlockSpec((None, tq, qw), lambda bi, r, i: (bi, i, r)))
        out_shape.append(jax.ShapeDtypeStruct((b, l, nrep * qw), F32))
    res = pl.pallas_call(
        functools.partial(_banded_kernel, window=window, pr=pr, tq=tq, kw=kw, want_lse=want_lse,
                          has_sink=sink_row is not None),
        grid=(b, nrep, l // tq),
        in_specs=in_specs,
        out_specs=out_specs,
        out_shape=out_shape,
        scratch_shapes=[pltpu.VMEM((pr + tq, kw), ka.dtype), pltpu.VMEM((pr + tq, kw), va.dtype),
                        pltpu.VMEM((6 * SUB_Q, SUB_Q + pr), F32), pltpu.VMEM((6 * SUB_Q, SUB_Q + pr), CDT),
                        pltpu.VMEM((6 * SUB_Q, LANES), F32), pltpu.VMEM((6 * SUB_Q, LANES), F32)],
        compiler_params=_cparams(("parallel", "parallel", "parallel"), 8),
        name=f"banded_w{window}_k{kw}_r{nrep}",
    )(*args)
    return res


def _gelu_tanh(x):
    return 0.5 * x * (1.0 + jnp.tanh(0.7978845608028654 * (x + 0.044715 * (x * x * x))))


def _compress_kernel(x_ref, pa_ref, pb_ref, w1a_ref, w1b_ref, b1_ref, w2_ref, b2_ref, o_ref):
    x = x_ref[...].astype(F32)
    n = x.shape[0]
    a = _dot((x + pa_ref[...]).astype(CDT), w1a_ref[...])
    bm = _dot((x + pb_ref[...]).astype(CDT), w1b_ref[...])
    h = a + pltpu.roll(bm, n - 1, 0) + b1_ref[...]
    o_ref[...] = (_dot(_gelu_tanh(h).astype(CDT), w2_ref[...]) + b2_ref[...]).astype(o_ref.dtype)


def _compress(t, pos, w1, b1, w2, b2):
    b, s, _ = t.shape
    nch = s // CMP_STRIDE
    xw = CMP_STRIDE * LANES
    x = t.reshape(b, nch, xw)
    eye = jnp.eye(NSA_KV_HEADS, dtype=F32)
    w1r = w1.reshape(CMP_BLOCK, HEAD_DIM, CMP_HIDDEN)

    def expand_w1(part):
        return jnp.einsum('tdj,kl->tkdlj', part, eye).reshape(xw, NSA_KV_HEADS * CMP_HIDDEN).astype(CDT)

    def expand_pos(part):
        return jnp.broadcast_to(part[:, None, :], (CMP_STRIDE, NSA_KV_HEADS, HEAD_DIM)).reshape(1, xw)

    w1a, w1b = expand_w1(w1r[:CMP_STRIDE]), expand_w1(w1r[CMP_STRIDE:])
    pa, pb = expand_pos(pos[:CMP_STRIDE]), expand_pos(pos[CMP_STRIDE:])
    b1e = jnp.tile(b1, NSA_KV_HEADS).reshape(1, -1)
    w2e = jnp.einsum('jd,kl->kjld', w2, eye).reshape(NSA_KV_HEADS * CMP_HIDDEN, LANES).astype(CDT)
    b2e = jnp.tile(b2, NSA_KV_HEADS).reshape(1, -1)
    consts = [pa, pb, w1a, w1b, b1e, w2e, b2e]
    return pl.pallas_call(
        _compress_kernel,
        grid=(b,),
        in_specs=[pl.BlockSpec((None, nch, xw), lambda bi: (bi, 0, 0))]
        + [pl.BlockSpec(c.shape, lambda bi: (0, 0)) for c in consts],
        out_specs=pl.BlockSpec((None, nch, LANES), lambda bi: (bi, 0, 0)),
        out_shape=jax.ShapeDtypeStruct((b, nch, LANES), CDT),
        compiler_params=_cparams(("parallel",), 12),
        name="nsa_compress",
    )(x, *consts)


CMP_ROWS = 16


def _cmp_select_kernel(q_ref, kc_ref, vc_ref, ov_ref, o_ref, sel_ref, s_ref, p_ref, hi_ref, lo_ref,
                       *, tq, n_sel, tile0):
    i = pl.program_id(1) + tile0
    ncp = kc_ref.shape[0]
    ns = ov_ref.shape[1]
    lo = _half_masks()
    t_col = i * tq + lax.broadcasted_iota(jnp.int32, (tq, 1), 0)
    blk = lax.broadcasted_iota(jnp.int32, (tq, ns), 1)
    blk_t = lax.broadcasted_iota(jnp.int32, (ns, tq), 0)
    cur = t_col // SEL_BLOCK
    causal = blk <= cur
    forced = (blk == 0) | (blk == cur) | (blk == cur - 1)
    rb = CMP_ROWS
    c_end = lax.broadcasted_iota(jnp.int32, (rb, ncp), 1) * CMP_STRIDE + (CMP_BLOCK - 1)
    o_kv, work_t = [], []
    for kv, hm in enumerate((lo, jnp.logical_not(lo))):
        qs = jnp.concatenate(
            [(jnp.where(hm, q_ref[:, p * LANES:(p + 1) * LANES], 0) * QK_SCALE).astype(CDT) for p in range(3)], axis=0)
        s_ref[...] = _nt_dot(qs, kc_ref[...])
        for r0 in range(0, tq, rb):
            t_rows = i * tq + r0 + lax.broadcasted_iota(jnp.int32, (rb, 1), 0)
            vis_bias = jnp.where(c_end <= t_rows, 0.0, NEG_INF)
            has_visible = t_rows >= CMP_BLOCK - 1
            psum = jnp.zeros((rb, ncp), F32)
            for h in range(3):
                rows = slice(h * tq + r0, h * tq + r0 + rb)
                s = s_ref[rows, :] + vis_bias
                e = jnp.exp(s - jnp.max(s, axis=-1, keepdims=True))
                inv = jnp.where(has_visible, 1.0 / jnp.maximum(jnp.sum(e, axis=-1, keepdims=True), 1e-30), 0.0)
                pn = e * inv
                p_ref[rows, :] = pn.astype(CDT)
                psum = psum + pn
            p_hi = psum.astype(CDT)
            hi_ref[r0:r0 + rb, :] = p_hi
            lo_ref[r0:r0 + rb, :] = (psum - p_hi.astype(F32)).astype(CDT)
        o = _dot(p_ref[...], vc_ref[...])
        o_kv.append([o[p * tq:(p + 1) * tq] for p in range(3)])
        imp = _dot(hi_ref[...], ov_ref[...]) + _dot(lo_ref[...], ov_ref[...])
        work_t.append(jnp.where(causal & jnp.logical_not(forced), imp, -BIG).T)

    blk_lanes = blk_t[:, 0:LANES]

    def pick(_, work):
        m = jnp.max(work, axis=0, keepdims=True)
        idx = jnp.min(jnp.where(work == m, blk_lanes, ns), axis=0, keepdims=True)
        return jnp.where(blk_lanes == idx, -2.0 * BIG, work)

    for kv, start in enumerate(work_t):
        done = jnp.concatenate([lax.fori_loop(0, n_sel - 3, pick, start[:, c0:c0 + LANES], unroll=True)
                                for c0 in range(0, tq, LANES)], axis=1)
        taken = jnp.where((done < -BIG) & (start > -BIG), 1.0, 0.0).T
        selb = jnp.where(forced | (taken > 0.5), 0.0, NEG_INF)
        sel_ref[:, kv * ns:(kv + 1) * ns] = selb.astype(sel_ref.dtype)
    for p in range(3):
        o_ref[:, p * LANES:(p + 1) * LANES] = jnp.where(lo, o_kv[0][p], o_kv[1][p]).astype(o_ref.dtype)


CMP_CAUSAL_SPLITS = 4


def _cmp_select(zr, qcol, kc, vc, tq=256):
    b, s, _ = zr.shape
    ncp = kc.shape[1]
    ns = s // SEL_BLOCK
    n_sel = min(N_SEL, ns)
    assert n_sel >= 3, "selection needs room for the three forced blocks"
    tq = min(tq, s)
    c = np.arange(ncp)[:, None] * CMP_STRIDE
    j = np.arange(ns)[None, :] * SEL_BLOCK
    overlap = ((c < j + SEL_BLOCK) & (c + CMP_BLOCK - 1 >= j)).astype(np.float32)
    overlap[ncp - 1:, :] = 0.0
    ov = jnp.asarray(overlap, CDT)
    qw = 3 * LANES
    n_split = CMP_CAUSAL_SPLITS if (s // tq) % CMP_CAUSAL_SPLITS == 0 and ncp % (16 * CMP_CAUSAL_SPLITS) == 0 else 1
    tiles = s // tq // n_split
    outs, sels = [], []
    for part in range(n_split):
        tile0 = part * tiles
        ncp_part = ncp * (part + 1) // n_split
        o_part, sel_part = pl.pallas_call(
            functools.partial(_cmp_select_kernel, tq=tq, n_sel=n_sel, tile0=tile0),
            grid=(b, tiles),
            in_specs=[pl.BlockSpec((None, tq, qw), lambda bi, i, tile0=tile0: (bi, i + tile0, qcol)),
                      pl.BlockSpec((None, ncp_part, LANES), lambda bi, i: (bi, 0, 0)),
                      pl.BlockSpec((None, ncp_part, LANES), lambda bi, i: (bi, 0, 0)),
                      pl.BlockSpec((ncp_part, ns), lambda bi, i: (0, 0))],
            out_specs=[pl.BlockSpec((None, tq, qw), lambda bi, i: (bi, i, 0)),
                       pl.BlockSpec((None, tq, 2 * ns), lambda bi, i: (bi, i, 0))],
            out_shape=[jax.ShapeDtypeStruct((b, tiles * tq, qw), CDT),
                       jax.ShapeDtypeStruct((b, tiles * tq, 2 * ns), CDT)],
            scratch_shapes=[pltpu.VMEM((3 * tq, ncp_part), F32), pltpu.VMEM((3 * tq, ncp_part), CDT),
                            pltpu.VMEM((tq, ncp_part), CDT), pltpu.VMEM((tq, ncp_part), CDT)],
            compiler_params=_cparams(("parallel", "parallel"), 12),
            name=f"nsa_cmp_select_p{part}",
        )(zr, kc, vc, ov)
        outs.append(o_part)
        sels.append(sel_part)
    return jnp.concatenate(outs, axis=1), jnp.concatenate(sels, axis=1)


SLC_KEY_CHUNK = 64


def _slc_kernel(q_ref, k0_ref, k1_ref, vt_ref, selb_ref, o_ref, s_ref, e_ref, m_ref, l_ref, a_ref, acc_ref,
                *, tq, tk):
    i = pl.program_id(1)
    ns = selb_ref.shape[1] // 2
    bpt = tk // SEL_BLOCK
    n_q = 3 * tq
    rc = min(SLC_KEY_CHUNK, tk)
    lo = _half_masks()
    n_tiles = ((i + 1) * tq + tk - 1) // tk
    p_row = lax.broadcasted_iota(jnp.int32, (ns, LANES), 0)
    p_col = lax.broadcasted_iota(jnp.int32, (ns, LANES), 1)
    groups = []
    for kv, (hm, k_ref) in enumerate(((lo, k0_ref), (jnp.logical_not(lo), k1_ref))):
        q3 = [(jnp.where(hm, q_ref[:, p * LANES:(p + 1) * LANES], 0) * QK_SCALE).astype(CDT) for p in range(3)]
        selb = selb_ref[:, kv * ns:(kv + 1) * ns]
        lane0 = HEAD_DIM if kv == 0 else 0
        groups.append((kv, hm, k_ref, q3, selb, lane0))
    m_ref[...] = jnp.full(m_ref.shape, NEG_INF, F32)
    l_ref[...] = jnp.zeros(l_ref.shape, F32)
    acc_ref[...] = jnp.zeros(acc_ref.shape, F32)

    def scores(j, nk):
        k0 = pl.multiple_of(j * tk, tk)
        for kv, hm, k_ref, q3, selb, lane0 in groups:
            place = ((p_col >= lane0) & (p_col < lane0 + bpt) & (p_row == p_col - lane0 + j * bpt)).astype(CDT)
            sb = _dot(selb, place).astype(CDT)
            qp = jnp.concatenate([jnp.where(hm, q, sb) for q in q3], axis=0)
            s_ref[kv, 0:nk, :] = _nt_dot(k_ref[pl.ds(k0, nk), :], qp)

    def softmax_pv(j, diagonal, nk):
        k0 = pl.multiple_of(j * tk, tk)
        for kv in range(2):
            for c0 in range(0, n_q, LANES):
                cols = slice(c0, c0 + LANES)
                t_lane = i * tq + (c0 % tq) + lax.broadcasted_iota(jnp.int32, (1, LANES), 1)

                def chunk(r0):
                    s = s_ref[kv, r0:r0 + rc, cols]
                    if diagonal:
                        kpos = k0 + r0 + lax.broadcasted_iota(jnp.int32, (rc, LANES), 0)
                        s = jnp.where(kpos <= t_lane, s, NEG_INF)
                    return s

                m8 = m_ref[kv, :, cols]
                for r0 in range(0, nk, rc):
                    m8 = jnp.maximum(m8, jnp.max(chunk(r0).reshape(rc // 8, 8, LANES), axis=0))
                m_new = jnp.max(m8, axis=0, keepdims=True)
                alpha = jnp.exp(m_ref[kv, :, cols] - m_new)
                l8 = jnp.zeros((8, LANES), F32)
                for r0 in range(0, nk, rc):
                    e = jnp.exp(chunk(r0) - m_new)
                    l8 = l8 + jnp.sum(e.reshape(rc // 8, 8, LANES), axis=0)
                    e_ref[kv, r0:r0 + rc, cols] = e.astype(CDT)
                l_ref[kv, :, cols] = alpha * l_ref[kv, :, cols] + jnp.sum(l8, axis=0, keepdims=True)
                m_ref[kv, :, cols] = jnp.broadcast_to(m_new, (8, LANES))
                a_ref[kv, :, cols] = alpha
        for kv in range(2):
            acc_ref[kv] = a_ref[kv, 0:1, :] * acc_ref[kv] + _dot(vt_ref[j, :, 0:nk], e_ref[kv, 0:nk, :])

    def tile(j, diagonal, nk=tk):
        scores(j, nk)
        softmax_pv(j, diagonal, nk)

    lax.fori_loop(0, n_tiles - 1, lambda j, c: (tile(j, False), c)[1], 0)
    last = n_tiles - 1
    half = tk // 2
    first_half_only = (i + 1) * tq - last * tk <= half

    @pl.when(first_half_only)
    def _():
        tile(last, True, half)

    @pl.when(jnp.logical_not(first_half_only))
    def _():
        tile(last, True)
    o0 = (acc_ref[0] / l_ref[0, 0:1, :]).T
    o1 = (acc_ref[1] / l_ref[1, 0:1, :]).T
    for p in range(3):
        o_ref[:, p * LANES:(p + 1) * LANES] = jnp.where(
            lo, o0[p * tq:(p + 1) * tq], o1[p * tq:(p + 1) * tq]).astype(o_ref.dtype)


def _slc(zr, qcol, ksc, vt, selb, tq=512):
    b, s, _ = zr.shape
    tq = min(tq, s)
    tk = vt.shape[2]
    assert tq % LANES == 0 and tk // SEL_BLOCK <= HEAD_DIM and s % tk == 0
    qw = 3 * LANES
    ns2 = selb.shape[2]
    pat = jax.nn.one_hot((jnp.arange(s) // SEL_BLOCK) % (tk // SEL_BLOCK), HEAD_DIM, dtype=ksc.dtype)
    pat = jnp.broadcast_to(pat[None], (b, s, HEAD_DIM))
    k0 = jnp.concatenate([ksc[..., :HEAD_DIM], pat], axis=-1)
    k1 = jnp.concatenate([pat, ksc[..., HEAD_DIM:]], axis=-1)
    full = lambda bi, i: (bi, 0, 0)
    return pl.pallas_call(
        functools.partial(_slc_kernel, tq=tq, tk=tk),
        grid=(b, s // tq),
        in_specs=[pl.BlockSpec((None, tq, qw), lambda bi, i: (bi, i, qcol)),
                  _single_buffered((None, s, LANES), full),
                  _single_buffered((None, s, LANES), full),
                  _single_buffered((s // tk, LANES, tk), lambda bi, i: (bi, 0, 0)),
                  pl.BlockSpec((None, tq, ns2), lambda bi, i: (bi, i, 0))],
        out_specs=pl.BlockSpec((None, tq, qw), lambda bi, i: (bi, i, 0)),
        out_shape=jax.ShapeDtypeStruct((b, s, qw), CDT),
        scratch_shapes=[pltpu.VMEM((2, tk, 3 * tq), F32), pltpu.VMEM((2, tk, 3 * tq), CDT),
                        pltpu.VMEM((2, 8, 3 * tq), F32), pltpu.VMEM((2, 8, 3 * tq), F32),
                        pltpu.VMEM((2, 8, 3 * tq), F32), pltpu.VMEM((2, LANES, 3 * tq), F32)],
        compiler_params=_cparams(("parallel", "arbitrary"), 14),
        name="nsa_slc",
    )(zr, k0, k1, vt, selb)


def _merge_kernel(x_ref, oa0, oa1, oa2, la0, la1, la2, ob, ocmp, oslc, owin, od,
                  wn_ref, wg_ref, bg_ref, wb_ref, wo_ref, g_ref, b_ref, o_ref, *scratch, alpha):
    x = x_ref[...]
    xb = x.astype(CDT)
    bw = BRANCH_WIDTH

    def token_rows(src_ref, dst_ref):
        dil = src_ref.shape[1] // bw
        if dil == 1:
            return src_ref[...].astype(F32)
        n_cb = bw // LANES
        for r in range(dil):
            for cb in range(n_cb):
                c0 = r * bw + cb * LANES
                dst_ref[cb, pl.ds(r, src_ref.shape[0], stride=dil), :] = src_ref[:, c0:c0 + LANES].astype(F32)
        return jnp.concatenate([dst_ref[cb] for cb in range(n_cb)], axis=1)

    o0, o1, o2 = (token_rows(s, d) for s, d in zip((oa0, oa1, oa2), scratch[0:3]))
    l0, l1, l2 = (token_rows(s, d) for s, d in zip((la0, la1, la2), scratch[3:6]))
    mx = jnp.maximum(jnp.maximum(l0, l1), l2)
    w0, w1, w2 = jnp.exp(l0 - mx), jnp.exp(l1 - mx), jnp.exp(l2 - mx)
    o_a = (w0 * o0 + w1 * o1 + w2 * o2) / (w0 + w1 + w2)
    gates = jax.nn.sigmoid(_dot(xb, wn_ref[...]))
    o_c = (gates[:, 0:bw] * ocmp[...].astype(F32) + gates[:, bw:2 * bw] * oslc[...].astype(F32)
           + gates[:, 2 * bw:3 * bw] * owin[...].astype(F32))
    branches = (o_a.astype(CDT), ob[...], o_c.astype(CDT), od[...])
    d = x.shape[1]
    merged = jnp.zeros(x.shape, F32)
    for m in range(N_BRANCH):
        gate = jax.nn.sigmoid(_dot(xb, wg_ref[:, m * d:(m + 1) * d]) + bg_ref[:, m * d:(m + 1) * d])
        merged = merged + gate * _dot(branches[m], wb_ref[m])
    r = alpha * x + _dot(merged.astype(CDT), wo_ref[...])
    o_ref[...] = _layer_norm(r, g_ref[...], b_ref[...])


def _merge(x2d, branch_inputs, wn, wg, bg, wb, wo, g, b, alpha, tm=256):
    t, d = x2d.shape
    tm = min(tm, t)
    row = lambda i: (i, 0)
    const2 = lambda i: (0, 0)
    in_specs = [pl.BlockSpec((tm, d), row)]
    in_specs += [pl.BlockSpec((tm * a.shape[0] // t, a.shape[1]), row) for a in branch_inputs]
    in_specs += [pl.BlockSpec(wn.shape, const2), pl.BlockSpec(wg.shape, const2), pl.BlockSpec(bg.shape, const2),
                 pl.BlockSpec(wb.shape, lambda i: (0, 0, 0)), pl.BlockSpec(wo.shape, const2),
                 pl.BlockSpec(g.shape, const2), pl.BlockSpec(b.shape, const2)]
    return pl.pallas_call(
        functools.partial(_merge_kernel, alpha=alpha),
        grid=(t // tm,),
        in_specs=in_specs,
        out_specs=pl.BlockSpec((tm, d), row),
        out_shape=jax.ShapeDtypeStruct((t, d), F32),
        scratch_shapes=[pltpu.VMEM((BRANCH_WIDTH // LANES, tm, LANES), F32) for _ in range(2 * N_DIL)],
        compiler_params=_cparams(("parallel",), 14),
        name="merge_ln",
    )(x2d, *branch_inputs, wn, wg, bg, wb, wo, g, b)


def _ple_ln(x, xb, f, p, plw_ref, pgw_ref, pgb_ref, g_ref, b_ref, alpha):
    ple = jax.nn.sigmoid(_dot(xb, pgw_ref[...]) + pgb_ref[...]) * _dot(p.astype(CDT), plw_ref[...])
    return _layer_norm(alpha * x + f + ple, g_ref[...], b_ref[...])


FFN_CHUNK = 512


def _ffn_kernel(x_ref, p_ref, wg_ref, wu_ref, wd_ref, plw_ref, pgw_ref, pgb_ref, g_ref, b_ref, o_ref, h_ref, *, alpha):
    x = x_ref[...]
    xb = x.astype(CDT)
    dff = wg_ref.shape[1]
    for c0 in range(0, dff, FFN_CHUNK):
        cols = slice(c0, min(c0 + FFN_CHUNK, dff))
        h_ref[:, cols] = (jax.nn.silu(_dot(xb, wg_ref[:, cols])) * _dot(xb, wu_ref[:, cols])).astype(CDT)
    f = _dot(h_ref[...], wd_ref[...])
    o_ref[...] = _ple_ln(x, xb, f, p_ref[...], plw_ref, pgw_ref, pgb_ref, g_ref, b_ref, alpha)


def _ffn(x2d, p2d, wg, wu, wd, plw, pgw, pgb, g, b, alpha, tm=512):
    t, d = x2d.shape
    tm = min(tm, t)
    dff = wg.shape[1]
    wg, wu, wd = wg.astype(CDT), wu.astype(CDT), wd.astype(CDT)
    row = lambda i: (i, 0)
    const = lambda shape: pl.BlockSpec(shape, lambda i: (0, 0), pipeline_mode=pl.Buffered(1))
    return pl.pallas_call(
        functools.partial(_ffn_kernel, alpha=alpha),
        grid=(t // tm,),
        in_specs=[pl.BlockSpec((tm, d), row), pl.BlockSpec((tm, p2d.shape[1]), row),
                  const(wg.shape), const(wu.shape), const(wd.shape),
                  const(plw.shape), const(pgw.shape), const(pgb.shape), const(g.shape), const(b.shape)],
        out_specs=pl.BlockSpec((tm, d), row),
        out_shape=jax.ShapeDtypeStruct((t, d), F32),
        scratch_shapes=[pltpu.VMEM((tm, dff), CDT)],
        compiler_params=_cparams(("parallel",), 12),
        name="ffn_ple_ln",
    )(x2d, p2d, wg, wu, wd, plw, pgw, pgb, g, b)


def _router_kernel(x_ref, wh_ref, wl_ref, b_ref, comb_ref, rank_ref, rank_t_ref, cnt_ref):
    x = x_ref[...]
    xh = x.astype(CDT)
    xl = (x - xh.astype(F32)).astype(CDT)
    logits = _dot(xh, wh_ref[...]) + _dot(xh, wl_ref[...]) + _dot(xl, wh_ref[...]) + b_ref[...]
    lane = lax.broadcasted_iota(jnp.int32, logits.shape, 1)
    v1 = jnp.max(logits, axis=-1, keepdims=True)
    i1 = jnp.min(jnp.where(logits == v1, lane, LANES), axis=-1, keepdims=True)
    rest = jnp.where(lane == i1, -jnp.inf, logits)
    v2 = jnp.max(rest, axis=-1, keepdims=True)
    i2 = jnp.min(jnp.where(rest == v2, lane, LANES), axis=-1, keepdims=True)
    e2 = jnp.exp(v2 - v1)
    comb_ref[...] = jnp.where(lane == i1, 1.0 / (1.0 + e2), 0.0) + jnp.where(lane == i2, e2 / (1.0 + e2), 0.0)
    routed = (lane == i1) | (lane == i2)
    mask = routed.astype(CDT)
    tm = x.shape[0]
    before = (lax.broadcasted_iota(jnp.int32, (tm, tm), 1) < lax.broadcasted_iota(jnp.int32, (tm, tm), 0)).astype(CDT)
    rank = jnp.where(routed, _dot(before, mask), -1.0)
    rank_ref[...] = rank
    rank_t_ref[...] = rank.T[0:rank_t_ref.shape[0], :]
    cnt_ref[...] = jnp.sum(routed.astype(F32), axis=0, keepdims=True).astype(jnp.int32)


def _router(x2d, w_router, b_router, tm):
    t, d = x2d.shape
    ne = w_router.shape[1]
    wp = jnp.zeros((d, LANES), F32).at[:, :ne].set(w_router)
    wh = wp.astype(CDT)
    wl = (wp - wh.astype(F32)).astype(CDT)
    bp = jnp.full((1, LANES), -BIG, F32).at[0, :ne].set(b_router)
    nt = t // tm
    row = lambda i: (i, 0)
    return pl.pallas_call(
        _router_kernel,
        grid=(nt,),
        in_specs=[pl.BlockSpec((tm, d), row), pl.BlockSpec(wh.shape, lambda i: (0, 0)),
                  pl.BlockSpec(wl.shape, lambda i: (0, 0)), pl.BlockSpec(bp.shape, lambda i: (0, 0))],
        out_specs=[pl.BlockSpec((tm, LANES), row), pl.BlockSpec((tm, LANES), row),
                   pl.BlockSpec((None, 8, tm), lambda i: (i, 0, 0)),
                   pl.BlockSpec((None, 1, LANES), lambda i: (i, 0, 0))],
        out_shape=[jax.ShapeDtypeStruct((t, LANES), F32), jax.ShapeDtypeStruct((t, LANES), F32),
                   jax.ShapeDtypeStruct((nt, 8, tm), F32), jax.ShapeDtypeStruct((nt, 1, LANES), jnp.int32)],
        compiler_params=_cparams(("parallel",), 10),
        name="moe_router",
    )(x2d, wh, wl, bp)


def _moe_kernel(cnt_ref, x_ref, comb_ref, rank_ref, rank_t_ref, p_ref, wg_ref, wu_ref, wd_ref, plw_ref, pgw_ref,
                pgb_ref, g_ref, b_ref, o_ref, xe_ref, ye_ref, *, alpha, rs, seg):
    i = pl.program_id(0)
    e = pl.program_id(1)
    c = pl.program_id(2)
    n_seg = x_ref.shape[0] // seg
    last_chunk = c == pl.num_programs(2) - 1

    first = slice(0, rs)
    toks = [slice(sg * seg, (sg + 1) * seg) for sg in range(n_seg)]
    n_groups = [(cnt_ref[(i * n_seg + sg) * LANES + e] + rs - 1) // rs for sg in range(n_seg)]

    def later(sc):
        return pl.ds(pl.multiple_of(sc * rs, 8), rs), (sc * rs).astype(F32)

    @pl.when((e == 0) & (c == 0))
    def _():
        o_ref[...] = jnp.zeros_like(o_ref)

    @pl.when(c == 0)
    def _():
        row_id = lax.broadcasted_iota(jnp.int32, (rs, seg), 0).astype(F32)
        xbs = [x_ref[tok, :].astype(CDT) for tok in toks]
        rank_rows = [rank_t_ref[sg, pl.ds(e, 1), :] for sg in range(n_seg)]

        def gather(sg, rws, base):
            onehot = (rank_rows[sg] - base == row_id).astype(CDT)
            xe_ref[sg, rws, :] = _dot(onehot, xbs[sg]).astype(CDT)
            ye_ref[sg, rws, :] = jnp.zeros((rs, ye_ref.shape[2]), F32)

        for sg in range(n_seg):
            gather(sg, first, 0.0)
        for sg in range(n_seg):
            lax.fori_loop(1, n_groups[sg], lambda sc, _, sg=sg: (gather(sg, *later(sc)), 0)[1], 0)

    def expert(sg, rws):
        xs = xe_ref[sg, rws, :]
        h = jax.nn.silu(_dot(xs, wg_ref[0])) * _dot(xs, wu_ref[0])
        ye_ref[sg, rws, :] += _dot(h.astype(CDT), wd_ref[0])

    for sg in range(n_seg):
        expert(sg, first)
    for sg in range(n_seg):
        lax.fori_loop(1, n_groups[sg], lambda sc, _, sg=sg: (expert(sg, later(sc)[0]), 0)[1], 0)

    @pl.when(last_chunk)
    def _():
        lane = lax.broadcasted_iota(jnp.int32, (seg, LANES), 1)
        mine = lane == e
        col_id = lax.broadcasted_iota(jnp.int32, (seg, rs), 1).astype(F32)
        cws = [jnp.sum(jnp.where(mine, comb_ref[tok, :], 0.0), axis=-1, keepdims=True) for tok in toks]
        rank_cols = [jnp.sum(jnp.where(mine, rank_ref[tok, :], 0.0), axis=-1, keepdims=True) for tok in toks]

        def scatter(sg, rws, base):
            onehot = (rank_cols[sg] - base == col_id).astype(CDT)
            o_ref[toks[sg], :] += cws[sg] * _dot(onehot, ye_ref[sg, rws, :].astype(CDT))

        for sg in range(n_seg):
            scatter(sg, first, 0.0)
        for sg in range(n_seg):
            lax.fori_loop(1, n_groups[sg], lambda sc, _, sg=sg: (scatter(sg, *later(sc)), 0)[1], 0)

    @pl.when((e == pl.num_programs(1) - 1) & last_chunk)
    def _():
        for sg in range(n_seg):
            tok = slice(sg * seg, (sg + 1) * seg)
            x = x_ref[tok, :]
            o_ref[tok, :] = _ple_ln(x, x.astype(CDT), o_ref[tok, :], p_ref[tok, :], plw_ref, pgw_ref, pgb_ref,
                                    g_ref, b_ref, alpha)


MOE_CHUNK = 512
MOE_SEGMENT = 1024
MOE_SEGMENTS_PER_TILE = 2
MOE_ROW_GROUP = 288


def _single_buffered(shape, index_map):
    return pl.BlockSpec(shape, index_map, pipeline_mode=pl.Buffered(1))


def _moe(x2d, routing, p2d, wg, wu, wd, plw, pgw, pgb, g, b, alpha, seg):
    comb, rank, rank_t, cnt = routing
    t, d = x2d.shape
    ne, _, dff = wg.shape
    ck = min(MOE_CHUNK, dff)
    rs = min(MOE_ROW_GROUP, seg)
    n_seg = min(MOE_SEGMENTS_PER_TILE, t // seg)
    tm = n_seg * seg
    max_rows = -(-seg // rs) * rs
    wg, wu, wd = wg.astype(CDT), wu.astype(CDT), wd.astype(CDT)
    row = lambda i, e, c, cnt: (i, 0)
    c2 = lambda i, e, c, cnt: (0, 0)
    grid_spec = pltpu.PrefetchScalarGridSpec(
        num_scalar_prefetch=1,
        grid=(t // tm, ne, dff // ck),
        in_specs=[_single_buffered((tm, d), row), _single_buffered((tm, LANES), row),
                  _single_buffered((tm, LANES), row),
                  _single_buffered((n_seg, 8, seg), lambda i, e, c, cnt: (i, 0, 0)),
                  _single_buffered((tm, p2d.shape[1]), row),
                  pl.BlockSpec((1, d, ck), lambda i, e, c, cnt: (e, 0, c)),
                  pl.BlockSpec((1, d, ck), lambda i, e, c, cnt: (e, 0, c)),
                  pl.BlockSpec((1, ck, d), lambda i, e, c, cnt: (e, c, 0)),
                  _single_buffered(plw.shape, c2), _single_buffered(pgw.shape, c2), _single_buffered(pgb.shape, c2),
                  _single_buffered(g.shape, c2), _single_buffered(b.shape, c2)],
        out_specs=pl.BlockSpec((tm, d), row),
        scratch_shapes=[pltpu.VMEM((n_seg, max_rows, d), CDT), pltpu.VMEM((n_seg, max_rows, d), F32)],
    )
    return pl.pallas_call(
        functools.partial(_moe_kernel, alpha=alpha, rs=rs, seg=seg),
        grid_spec=grid_spec,
        out_shape=jax.ShapeDtypeStruct((t, d), F32),
        compiler_params=_cparams(("parallel", "arbitrary", "arbitrary"), 15),
        name="moe_ple_ln",
    )(cnt.reshape(-1), x2d, comb, rank, rank_t, p2d, wg, wu, wd, plw, pgw, pgb, g, b)


def _prep_in_weights(w_in):
    o = COL_OFF
    bw = BRANCH_WIDTH
    cols = lambda n: w_in[:, o[n]:o[n + 1]]
    qa, ka, va = cols(0), cols(1), cols(2)
    w_dil = jnp.concatenate(
        [t[:, g * bw:(g + 1) * bw] for g in range(N_DIL) for t in (qa, ka, va)], axis=1).astype(CDT)
    w_conv = jnp.concatenate([cols(3), cols(4), cols(5)], axis=1).astype(CDT)
    gn = cols(13)
    w_gate = jnp.concatenate([gn[:, br * NSA_Q_HEADS + GQA_COL_HEAD] for br in range(3)], axis=1).astype(CDT)
    w_rest = jnp.concatenate([cols(6)[:, GQA_COL_PERM], cols(14)[:, GQA_COL_PERM], cols(11),
                              cols(12), cols(15), cols(16), cols(9), cols(7), cols(8)], axis=1).astype(CDT)
    wt_vsc = cols(10).T.astype(CDT)
    return w_dil, w_conv, w_gate, w_rest, wt_vsc


ZR_Q_NSA, ZR_Q_SWA = 0, 1
ZR_KWC, ZR_VWC, ZR_KD, ZR_VD = 6, 7, 8, 9
ZR_WIDTH = 2 * BRANCH_WIDTH + 4 * LANES
SLC_KEY_TILE = 1024


def _token_mixers(x, w_in, conv_w, cmp_pos, cmp_w1, cmp_b1, cmp_w2, cmp_b2, sinks):
    b, s, d = x.shape
    x2d = x.reshape(b * s, d)
    w_dil, w_conv, w_gate, w_rest, wt_vsc = _prep_in_weights(w_in)
    gw = 3 * BRANCH_WIDTH

    z_dil = _linear(x2d, w_dil, [(g * gw, (g + 1) * gw) for g in range(N_DIL)], gw, "in_proj_dil",
                    dils=[dil for _, dil in DIL_PATTERNS])
    zr, ksc, kcc, vcc = _linear(x2d, w_rest, [(0, ZR_WIDTH)] + [(ZR_WIDTH + n * LANES, ZR_WIDTH + (n + 1) * LANES)
                                                           for n in range(3)], 256, "in_proj_rest")
    zr = zr.reshape(b, s, ZR_WIDTH)
    o_b = _conv(x, w_conv, conv_w)

    dil_o, dil_lse = [], []
    for g, (window, dil) in enumerate(DIL_PATTERNS):
        view = z_dil[g].reshape(b, s // dil, dil * gw)
        og, lg = _banded(view, view, view, nrep=dil,
                         qcol=lambda r: 3 * r, kcol=lambda r: 3 * r + 1, vcol=lambda r: 3 * r + 2,
                         kw=3 * LANES, window=window // dil, want_lse=True)
        dil_o.append(og.reshape(b * s // dil, dil * BRANCH_WIDTH))
        dil_lse.append(lg.reshape(b * s // dil, dil * BRANCH_WIDTH))

    kc = _compress(kcc.reshape(b, s, LANES), cmp_pos[0], cmp_w1[0], cmp_b1[0], cmp_w2[0], cmp_b2[0])
    vc = _compress(vcc.reshape(b, s, LANES), cmp_pos[1], cmp_w1[1], cmp_b1[1], cmp_w2[1], cmp_b2[1])
    o_cmp, selb = _cmp_select(zr, ZR_Q_NSA, kc, vc)
    vsc_t = _linear_t(x2d, wt_vsc, min(SLC_KEY_TILE, s), "in_proj_vsc_t")
    o_slc = _slc(zr, ZR_Q_NSA, ksc.reshape(b, s, LANES), vsc_t, selb)
    (o_win,) = _banded(zr, zr, zr, nrep=1, qcol=lambda r: ZR_Q_NSA, kcol=lambda r: ZR_KWC, vcol=lambda r: ZR_VWC,
                       kw=LANES, window=NSA_WINDOW - 1, want_lse=False, tq=512)

    sink_row = sinks.astype(F32)[GQA_COL_HEAD].reshape(1, BRANCH_WIDTH)
    (o_d,) = _banded(zr, zr, zr, nrep=1, qcol=lambda r: ZR_Q_SWA, kcol=lambda r: ZR_KD, vcol=lambda r: ZR_VD,
                     kw=LANES, window=SWA_WINDOW - 1, want_lse=False, sink_row=sink_row)

    t = b * s
    flat = lambda a: a.reshape(t, a.shape[-1])
    return [dil_o[0], dil_o[1], dil_o[2], dil_lse[0], dil_lse[1], dil_lse[2], flat(o_b), flat(o_cmp), flat(o_slc),
            flat(o_win), flat(o_d)], w_gate


def kernel(x, p, w_in, conv_w, cmp_pos, cmp_w1, cmp_b1, cmp_w2, cmp_b2, sinks, w_branch, w_merge_gate, b_merge_gate, w_out, ln_mix_g, ln_mix_b, ffn_w_gate, ffn_w_up, ffn_w_down, w_router, b_router, moe_w_gate, moe_w_up, moe_w_down, ple_w, ple_gate_w, ple_gate_b, ln_ffn_g, ln_ffn_b):
    depth, b, s, _ = p.shape
    d = x.shape[-1]
    t = b * s
    alpha = (2 * depth) ** 0.25
    row = lambda v: v.reshape(1, -1).astype(F32)
    for i in range(depth):
        branch_inputs, w_nsa_gate = _token_mixers(x, w_in[i], conv_w[i], cmp_pos[i], cmp_w1[i], cmp_b1[i], cmp_w2[i],
                                                  cmp_b2[i], sinks[i])
        wg = jnp.concatenate([w_merge_gate[i, m] for m in range(N_BRANCH)], axis=1).astype(CDT)
        bg = b_merge_gate[i].reshape(1, N_BRANCH * d).astype(F32)
        wb = jnp.stack([w_branch[i, 0], w_branch[i, 1], w_branch[i, 2][GQA_COL_PERM],
                        w_branch[i, 3][GQA_COL_PERM]]).astype(CDT)
        x1 = _merge(x.reshape(t, d), branch_inputs, w_nsa_gate, wg, bg, wb, w_out[i].astype(CDT),
                    row(ln_mix_g[i]), row(ln_mix_b[i]), alpha)
        p2d = p[i].reshape(t, -1)
        ple_args = (ple_w[i].astype(CDT), ple_gate_w[i].astype(CDT), row(ple_gate_b[i]),
                    row(ln_ffn_g[i]), row(ln_ffn_b[i]))
        j = i // 2
        if i % 2 == 0:
            x2 = _ffn(x1, p2d, ffn_w_gate[j], ffn_w_up[j], ffn_w_down[j], *ple_args, alpha)
        else:
            seg = min(MOE_SEGMENT, t)
            routing = _router(x1, w_router[j], b_router[j], seg)
            x2 = _moe(x1, routing, p2d, moe_w_gate[j], moe_w_up[j], moe_w_down[j], *ple_args, alpha, seg)
        x = x2.reshape(b, s, d)
    return x
```

```python
import functools

import numpy as np
import jax
import jax.numpy as jnp
from jax import lax
from jax.experimental import pallas as pl
from jax.experimental.pallas import tpu as pltpu

HEAD_DIM = 64
DIL_PATTERNS = ((128, 1), (512, 4), (2048, 16))
N_DIL = 3
DIL_HEADS = 6
CONV_WIDTH = 384
NSA_Q_HEADS = 6
NSA_KV_HEADS = 2
CMP_BLOCK = 32
CMP_STRIDE = 16
CMP_HIDDEN = 128
SEL_BLOCK = 64
N_SEL = 16
NSA_WINDOW = 512
SWA_Q_HEADS = 6
SWA_WINDOW = 128
BRANCH_WIDTH = 384
N_BRANCH = 4
LN_EPS = 1e-5
NEG_INF = -1e30
DIL_WIDTH = N_DIL * DIL_HEADS * HEAD_DIM
COLUMN_SIZES = (DIL_WIDTH, DIL_WIDTH, DIL_WIDTH, CONV_WIDTH, CONV_WIDTH, CONV_WIDTH,
                NSA_Q_HEADS * HEAD_DIM, 128, 128, 128, 128, 128, 128, 3 * NSA_Q_HEADS,
                SWA_Q_HEADS * HEAD_DIM, 128, 128)
COL_OFF = np.concatenate([[0], np.cumsum(COLUMN_SIZES)]).tolist()

LANES = 128
V7X_VMEM_BYTES = 64 * 1024 * 1024

CDT = jnp.bfloat16
F32 = jnp.float32
QK_SCALE = HEAD_DIM ** -0.5
SUB_Q = 128
BAND_ROWS = 32
BIG = 1e30

_GQA_HEAD_ORDER = (0, 3, 1, 4, 2, 5)
GQA_COL_PERM = np.concatenate([np.arange(h * HEAD_DIM, (h + 1) * HEAD_DIM) for h in _GQA_HEAD_ORDER])
GQA_COL_HEAD = GQA_COL_PERM // HEAD_DIM


def _cparams(sem, vmem_sixteenths):
    return pltpu.CompilerParams(dimension_semantics=sem, vmem_limit_bytes=V7X_VMEM_BYTES * vmem_sixteenths // 16)


def _nt_dot(a, b):
    return lax.dot_general(a, b, (((1,), (1,)), ((), ())), preferred_element_type=F32)


def _dot(a, b):
    return jnp.dot(a, b, preferred_element_type=F32)


def _layer_norm(r, g, b):
    mu = jnp.mean(r, axis=-1, keepdims=True)
    d = r - mu
    var = jnp.mean(d * d, axis=-1, keepdims=True)
    return d * lax.rsqrt(var + LN_EPS) * g + b


def _half_masks():
    lane = lax.broadcasted_iota(jnp.int32, (1, LANES), 1)
    return lane < HEAD_DIM


def _linear_kernel(x_ref, w_ref, *refs, splits, n_chunk, dils):
    o_refs, z_ref = refs[:len(splits)], refs[len(splits)]
    xb = x_ref[...].astype(CDT)
    tm = xb.shape[0]
    for o_ref, (c0, c1), dil in zip(o_refs, splits, dils):
        width = c1 - c0
        if dil == 1:
            for a in range(c0, c1, n_chunk):
                b = min(a + n_chunk, c1)
                o_ref[:, a - c0:b - c0] = _dot(xb, w_ref[:, a:b]).astype(o_ref.dtype)
        else:
            z = _dot(xb, w_ref[:, c0:c1])
            for cb in range(width // LANES):
                z_ref[cb] = z[:, cb * LANES:(cb + 1) * LANES]
            for r in range(dil):
                for cb in range(width // LANES):
                    o_ref[:, r * width + cb * LANES:r * width + (cb + 1) * LANES] = (
                        z_ref[cb, pl.ds(r, tm // dil, stride=dil), :].astype(o_ref.dtype))


def _linear(x2d, w, splits, n_chunk, name, dils=None, tm=512):
    t, k = x2d.shape
    tm = min(tm, t)
    n = w.shape[1]
    dils = tuple(dils) if dils is not None else (1,) * len(splits)
    widths = [c1 - c0 for c0, c1 in splits]
    assert all(tm % (16 * dl) == 0 for dl in dils)
    return pl.pallas_call(
        functools.partial(_linear_kernel, splits=tuple(splits), n_chunk=n_chunk, dils=dils),
        grid=(t // tm,),
        in_specs=[pl.BlockSpec((tm, k), lambda i: (i, 0)),
                  pl.BlockSpec((k, n), lambda i: (0, 0))],
        out_specs=[pl.BlockSpec((tm // dl, dl * wd), lambda i: (i, 0)) for wd, dl in zip(widths, dils)],
        out_shape=[jax.ShapeDtypeStruct((t // dl, dl * wd), CDT) for wd, dl in zip(widths, dils)],
        scratch_shapes=[pltpu.VMEM((max(widths) // LANES, tm, LANES), F32)],
        compiler_params=_cparams(("parallel",), 12),
        name=name,
    )(x2d, w)


def _linear_t_kernel(x_ref, wt_ref, o_ref):
    o_ref[...] = _nt_dot(wt_ref[...], x_ref[...].astype(CDT)).astype(o_ref.dtype)


def _linear_t(x2d, wt, tm, name):
    t, k = x2d.shape
    n = wt.shape[0]
    return pl.pallas_call(
        _linear_t_kernel,
        grid=(t // tm,),
        in_specs=[pl.BlockSpec((tm, k), lambda i: (i, 0)), pl.BlockSpec((n, k), lambda i: (0, 0))],
        out_specs=pl.BlockSpec((None, n, tm), lambda i: (i, 0, 0)),
        out_shape=jax.ShapeDtypeStruct((t // tm, n, tm), CDT),
        compiler_params=_cparams(("parallel",), 8),
        name=name,
    )(x2d, wt)


def _conv_kernel(x_ref, xh_ref, wc_ref, cw_ref, ob_ref, *, tm):
    i = pl.program_id(1)
    w = CONV_WIDTH
    xb = x_ref[...].astype(CDT)
    z = _dot(xb, wc_ref[...])
    u = z[:, w:2 * w] * z[:, 2 * w:3 * w]
    zh = _dot(xh_ref[...].astype(CDT), wc_ref[:, w:3 * w])
    uh = zh[:, :w] * zh[:, w:]
    uh = jnp.where(i == 0, 0.0, uh)
    row = lax.broadcasted_iota(jnp.int32, (tm, w), 0)
    u1 = jnp.where(row == 0, uh[7:8, :], pltpu.roll(u, 1, 0))
    u2 = jnp.where(row == 0, uh[6:7, :], jnp.where(row == 1, uh[7:8, :], pltpu.roll(u, 2, 0)))
    y = cw_ref[0:1, :] * u2 + cw_ref[1:2, :] * u1 + cw_ref[2:3, :] * u
    ob_ref[...] = (z[:, :w] * y).astype(ob_ref.dtype)


def _conv(x, wc, conv_w, tm=512):
    b, s, d = x.shape
    tm = min(tm, s)
    hb = tm // 8
    return pl.pallas_call(
        functools.partial(_conv_kernel, tm=tm),
        grid=(b, s // tm),
        in_specs=[pl.BlockSpec((None, tm, d), lambda bi, i: (bi, i, 0)),
                  pl.BlockSpec((None, 8, d), lambda bi, i: (bi, jnp.maximum(i * hb - 1, 0), 0)),
                  pl.BlockSpec(wc.shape, lambda bi, i: (0, 0)),
                  pl.BlockSpec(conv_w.shape, lambda bi, i: (0, 0))],
        out_specs=pl.BlockSpec((None, tm, CONV_WIDTH), lambda bi, i: (bi, i, 0)),
        out_shape=jax.ShapeDtypeStruct((b, s, CONV_WIDTH), CDT),
        compiler_params=_cparams(("parallel", "parallel"), 12),
        name="short_conv",
    )(x, x, wc, conv_w)


def _banded_kernel(*refs, window, pr, tq, kw, want_lse, has_sink):
    q_ref, kp_ref, kc_ref, vp_ref, vc_ref = refs[:5]
    n = 5
    sink_ref = None
    if has_sink:
        sink_ref = refs[n]
        n += 1
    o_ref = refs[n]
    n += 1
    lse_ref = None
    if want_lse:
        lse_ref = refs[n]
        n += 1
    kbuf, vbuf, s_ref, e_ref, m_ref, l_ref = refs[n:n + 6]

    i = pl.program_id(2)
    kbuf[0:pr, :] = kp_ref[...]
    kbuf[pr:pr + tq, :] = kc_ref[...]
    vbuf[0:pr, :] = vp_ref[...]
    vbuf[pr:pr + tq, :] = vc_ref[...]

    span = SUB_Q + pr
    qi = lax.broadcasted_iota(jnp.int32, (SUB_Q, span), 0)
    kj = lax.broadcasted_iota(jnp.int32, (SUB_Q, span), 1)
    dist = pr + qi - kj
    band = (dist >= 0) & (dist <= window)
    lo = _half_masks()
    halves = (lo, jnp.logical_not(lo))
    groups = ((0,), (1,), (2,)) if kw == 3 * LANES else ((0, 1, 2),)
    rb = BAND_ROWS
    for sb in range(tq // SUB_Q):
        r0 = sb * SUB_Q
        bias = jnp.where(band & (i * tq + r0 - pr + kj >= 0), 0.0, NEG_INF)
        for grp in groups:
            kc0 = grp[0] * LANES if kw == 3 * LANES else 0
            qs = jnp.concatenate(
                [(jnp.where(hm, q_ref[r0:r0 + SUB_Q, p * LANES:(p + 1) * LANES], 0) * QK_SCALE).astype(CDT)
                 for p in grp for hm in halves], axis=0)
            g0 = 2 * grp[0] * SUB_Q
            s_ref[g0:g0 + qs.shape[0], :] = _nt_dot(qs, kbuf[r0:r0 + span, kc0:kc0 + LANES])
        for c0 in range(0, 6 * SUB_Q, rb):
            rows = slice(c0, c0 + rb)
            s = s_ref[rows, :] + bias[c0 % SUB_Q:c0 % SUB_Q + rb, :]
            m = jnp.max(s, axis=-1, keepdims=True)
            e = jnp.exp(s - m)
            e_ref[rows, :] = e.astype(CDT)
            m_ref[rows, :] = jnp.broadcast_to(m, (rb, LANES))
            l_ref[rows, :] = jnp.broadcast_to(jnp.sum(e, axis=-1, keepdims=True), (rb, LANES))
        for grp in groups:
            kc0 = grp[0] * LANES if kw == 3 * LANES else 0
            g0 = 2 * grp[0] * SUB_Q
            g1 = g0 + 2 * len(grp) * SUB_Q
            l = l_ref[g0:g1, :]
            o = _dot(e_ref[g0:g1, :], vbuf[r0:r0 + span, kc0:kc0 + LANES]) / l
            lse = m_ref[g0:g1, :] + jnp.log(l)
            for n_p, p in enumerate(grp):
                a = 2 * n_p * SUB_Q
                o_pair = jnp.where(lo, o[a:a + SUB_Q], o[a + SUB_Q:a + 2 * SUB_Q])
                lse_pair = jnp.where(lo, lse[a:a + SUB_Q], lse[a + SUB_Q:a + 2 * SUB_Q])
                if has_sink:
                    o_pair = o_pair * jax.nn.sigmoid(lse_pair - sink_ref[:, p * LANES:(p + 1) * LANES])
                o_ref[r0:r0 + SUB_Q, p * LANES:(p + 1) * LANES] = o_pair.astype(o_ref.dtype)
                if want_lse:
                    lse_ref[r0:r0 + SUB_Q, p * LANES:(p + 1) * LANES] = lse_pair


def _banded(qa, ka, va, *, nrep, qcol, kcol, vcol, kw, window, want_lse, sink_row=None, tq=1024):
    b, l, _ = qa.shape
    pr = -(-window // SUB_Q) * SUB_Q
    tq = min(max(tq, pr), l)
    assert tq % pr == 0 and l % tq == 0, (tq, pr, l)
    ratio = tq // pr
    qw = 3 * LANES
    in_specs = [
        pl.BlockSpec((None, tq, qw), lambda bi, r, i: (bi, i, qcol(r))),
        pl.BlockSpec((None, pr, kw), lambda bi, r, i: (bi, jnp.maximum(i * ratio - 1, 0), kcol(r))),
        pl.BlockSpec((None, tq, kw), lambda bi, r, i: (bi, i, kcol(r))),
        pl.BlockSpec((None, pr, kw), lambda bi, r, i: (bi, jnp.maximum(i * ratio - 1, 0), vcol(r))),
        pl.BlockSpec((None, tq, kw), lambda bi, r, i: (bi, i, vcol(r))),
    ]
    args = [qa, ka, ka, va, va]
    if sink_row is not None:
        in_specs.append(pl.BlockSpec(sink_row.shape, lambda bi, r, i: (0, 0)))
        args.append(sink_row)
    out_specs = [pl.BlockSpec((None, tq, qw), lambda bi, r, i: (bi, i, r))]
    out_shape = [jax.ShapeDtypeStruct((b, l, nrep * qw), CDT)]
    if want_lse:
        out_specs.append(pl.BlockSpec((None, tq, qw), lambda bi, r, i: (bi, i, r)))
        out_shape.append(jax.ShapeDtypeStruct((b, l, nrep * qw), F32))
    res = pl.pallas_call(
        functools.partial(_banded_kernel, window=window, pr=pr, tq=tq, kw=kw, want_lse=want_lse,
                          has_sink=sink_row is not None),
        grid=(b, nrep, l // tq),
        in_specs=in_specs,
        out_specs=out_specs,
        out_shape=out_shape,
        scratch_shapes=[pltpu.VMEM((pr + tq, kw), ka.dtype), pltpu.VMEM((pr + tq, kw), va.dtype),
                        pltpu.VMEM((6 * SUB_Q, SUB_Q + pr), F32), pltpu.VMEM((6 * SUB_Q, SUB_Q + pr), CDT),
                        pltpu.VMEM((6 * SUB_Q, LANES), F32), pltpu.VMEM((6 * SUB_Q, LANES), F32)],
        compiler_params=_cparams(("parallel", "parallel", "parallel"), 8),
        name=f"banded_w{window}_k{kw}_r{nrep}",
    )(*args)
    return res


def _gelu_tanh(x):
    return 0.5 * x * (1.0 + jnp.tanh(0.7978845608028654 * (x + 0.044715 * (x * x * x))))


def _compress_kernel(x_ref, pa_ref, pb_ref, w1a_ref, w1b_ref, b1_ref, w2_ref, b2_ref, o_ref):
    x = x_ref[...].astype(F32)
    n = x.shape[0]
    a = _dot((x + pa_ref[...]).astype(CDT), w1a_ref[...])
    bm = _dot((x + pb_ref[...]).astype(CDT), w1b_ref[...])
    h = a + pltpu.roll(bm, n - 1, 0) + b1_ref[...]
    o_ref[...] = (_dot(_gelu_tanh(h).astype(CDT), w2_ref[...]) + b2_ref[...]).astype(o_ref.dtype)


def _compress(t, pos, w1, b1, w2, b2):
    b, s, _ = t.shape
    nch = s // CMP_STRIDE
    xw = CMP_STRIDE * LANES
    x = t.reshape(b, nch, xw)
    eye = jnp.eye(NSA_KV_HEADS, dtype=F32)
    w1r = w1.reshape(CMP_BLOCK, HEAD_DIM, CMP_HIDDEN)

    def expand_w1(part):
        return jnp.einsum('tdj,kl->tkdlj', part, eye).reshape(xw, NSA_KV_HEADS * CMP_HIDDEN).astype(CDT)

    def expand_pos(part):
        return jnp.broadcast_to(part[:, None, :], (CMP_STRIDE, NSA_KV_HEADS, HEAD_DIM)).reshape(1, xw)

    w1a, w1b = expand_w1(w1r[:CMP_STRIDE]), expand_w1(w1r[CMP_STRIDE:])
    pa, pb = expand_pos(pos[:CMP_STRIDE]), expand_pos(pos[CMP_STRIDE:])
    b1e = jnp.tile(b1, NSA_KV_HEADS).reshape(1, -1)
    w2e = jnp.einsum('jd,kl->kjld', w2, eye).reshape(NSA_KV_HEADS * CMP_HIDDEN, LANES).astype(CDT)
    b2e = jnp.tile(b2, NSA_KV_HEADS).reshape(1, -1)
    consts = [pa, pb, w1a, w1b, b1e, w2e, b2e]
    return pl.pallas_call(
        _compress_kernel,
        grid=(b,),
        in_specs=[pl.BlockSpec((None, nch, xw), lambda bi: (bi, 0, 0))]
        + [pl.BlockSpec(c.shape, lambda bi: (0, 0)) for c in consts],
        out_specs=pl.BlockSpec((None, nch, LANES), lambda bi: (bi, 0, 0)),
        out_shape=jax.ShapeDtypeStruct((b, nch, LANES), CDT),
        compiler_params=_cparams(("parallel",), 12),
        name="nsa_compress",
    )(x, *consts)


CMP_ROWS = 16


def _cmp_select_kernel(q_ref, kc_ref, vc_ref, ov_ref, o_ref, sel_ref, s_ref, p_ref, hi_ref, lo_ref,
                       *, tq, n_sel, tile0):
    i = pl.program_id(1) + tile0
    ncp = kc_ref.shape[0]
    ns = ov_ref.shape[1]
    lo = _half_masks()
    t_col = i * tq + lax.broadcasted_iota(jnp.int32, (tq, 1), 0)
    blk = lax.broadcasted_iota(jnp.int32, (tq, ns), 1)
    blk_t = lax.broadcasted_iota(jnp.int32, (ns, tq), 0)
    cur = t_col // SEL_BLOCK
    causal = blk <= cur
    forced = (blk == 0) | (blk == cur) | (blk == cur - 1)
    rb = CMP_ROWS
    c_end = lax.broadcasted_iota(jnp.int32, (rb, ncp), 1) * CMP_STRIDE + (CMP_BLOCK - 1)
    o_kv, work_t = [], []
    for kv, hm in enumerate((lo, jnp.logical_not(lo))):
        qs = jnp.concatenate(
            [(jnp.where(hm, q_ref[:, p * LANES:(p + 1) * LANES], 0) * QK_SCALE).astype(CDT) for p in range(3)], axis=0)
        s_ref[...] = _nt_dot(qs, kc_ref[...])
        for r0 in range(0, tq, rb):
            t_rows = i * tq + r0 + lax.broadcasted_iota(jnp.int32, (rb, 1), 0)
            vis_bias = jnp.where(c_end <= t_rows, 0.0, NEG_INF)
            has_visible = t_rows >= CMP_BLOCK - 1
            psum = jnp.zeros((rb, ncp), F32)
            for h in range(3):
                rows = slice(h * tq + r0, h * tq + r0 + rb)
                s = s_ref[rows, :] + vis_bias
                e = jnp.exp(s - jnp.max(s, axis=-1, keepdims=True))
                inv = jnp.where(has_visible, 1.0 / jnp.maximum(jnp.sum(e, axis=-1, keepdims=True), 1e-30), 0.0)
                pn = e * inv
                p_ref[rows, :] = pn.astype(CDT)
                psum = psum + pn
            p_hi = psum.astype(CDT)
            hi_ref[r0:r0 + rb, :] = p_hi
            lo_ref[r0:r0 + rb, :] = (psum - p_hi.astype(F32)).astype(CDT)
        o = _dot(p_ref[...], vc_ref[...])
        o_kv.append([o[p * tq:(p + 1) * tq] for p in range(3)])
        imp = _dot(hi_ref[...], ov_ref[...]) + _dot(lo_ref[...], ov_ref[...])
        work_t.append(jnp.where(causal & jnp.logical_not(forced), imp, -BIG).T)

    blk_lanes = blk_t[:, 0:LANES]

    def pick(_, work):
        m = jnp.max(work, axis=0, keepdims=True)
        idx = jnp.min(jnp.where(work == m, blk_lanes, ns), axis=0, keepdims=True)
        return jnp.where(blk_lanes == idx, -2.0 * BIG, work)

    for kv, start in enumerate(work_t):
        done = jnp.concatenate([lax.fori_loop(0, n_sel - 3, pick, start[:, c0:c0 + LANES], unroll=True)
                                for c0 in range(0, tq, LANES)], axis=1)
        taken = jnp.where((done < -BIG) & (start > -BIG), 1.0, 0.0).T
        selb = jnp.where(forced | (taken > 0.5), 0.0, NEG_INF)
        sel_ref[:, kv * ns:(kv + 1) * ns] = selb.astype(sel_ref.dtype)
    for p in range(3):
        o_ref[:, p * LANES:(p + 1) * LANES] = jnp.where(lo, o_kv[0][p], o_kv[1][p]).astype(o_ref.dtype)


CMP_CAUSAL_SPLITS = 4


def _cmp_select(zr, qcol, kc, vc, tq=256):
    b, s, _ = zr.shape
    ncp = kc.shape[1]
    ns = s // SEL_BLOCK
    n_sel = min(N_SEL, ns)
    assert n_sel >= 3, "selection needs room for the three forced blocks"
    tq = min(tq, s)
    c = np.arange(ncp)[:, None] * CMP_STRIDE
    j = np.arange(ns)[None, :] * SEL_BLOCK
    overlap = ((c < j + SEL_BLOCK) & (c + CMP_BLOCK - 1 >= j)).astype(np.float32)
    overlap[ncp - 1:, :] = 0.0
    ov = jnp.asarray(overlap, CDT)
    qw = 3 * LANES
    n_split = CMP_CAUSAL_SPLITS if (s // tq) % CMP_CAUSAL_SPLITS == 0 and ncp % (16 * CMP_CAUSAL_SPLITS) == 0 else 1
    tiles = s // tq // n_split
    outs, sels = [], []
    for part in range(n_split):
        tile0 = part * tiles
        ncp_part = ncp * (part + 1) // n_split
        o_part, sel_part = pl.pallas_call(
            functools.partial(_cmp_select_kernel, tq=tq, n_sel=n_sel, tile0=tile0),
            grid=(b, tiles),
            in_specs=[pl.BlockSpec((None, tq, qw), lambda bi, i, tile0=tile0: (bi, i + tile0, qcol)),
                      pl.BlockSpec((None, ncp_part, LANES), lambda bi, i: (bi, 0, 0)),
                      pl.BlockSpec((None, ncp_part, LANES), lambda bi, i: (bi, 0, 0)),
                      pl.BlockSpec((ncp_part, ns), lambda bi, i: (0, 0))],
            out_specs=[pl.BlockSpec((None, tq, qw), lambda bi, i: (bi, i, 0)),
                       pl.BlockSpec((None, tq, 2 * ns), lambda bi, i: (bi, i, 0))],
            out_shape=[jax.ShapeDtypeStruct((b, tiles * tq, qw), CDT),
                       jax.ShapeDtypeStruct((b, tiles * tq, 2 * ns), CDT)],
            scratch_shapes=[pltpu.VMEM((3 * tq, ncp_part), F32), pltpu.VMEM((3 * tq, ncp_part), CDT),
                            pltpu.VMEM((tq, ncp_part), CDT), pltpu.VMEM((tq, ncp_part), CDT)],
            compiler_params=_cparams(("parallel", "parallel"), 12),
            name=f"nsa_cmp_select_p{part}",
        )(zr, kc, vc, ov)
        outs.append(o_part)
        sels.append(sel_part)
    return jnp.concatenate(outs, axis=1), jnp.concatenate(sels, axis=1)


SLC_KEY_CHUNK = 64


def _slc_kernel(q_ref, k0_ref, k1_ref, vt_ref, selb_ref, o_ref, s_ref, e_ref, m_ref, l_ref, a_ref, acc_ref,
                *, tq, tk):
    i = pl.program_id(1)
    ns = selb_ref.shape[1] // 2
    bpt = tk // SEL_BLOCK
    n_q = 3 * tq
    rc = min(SLC_KEY_CHUNK, tk)
    lo = _half_masks()
    n_tiles = ((i + 1) * tq + tk - 1) // tk
    p_row = lax.broadcasted_iota(jnp.int32, (ns, LANES), 0)
    p_col = lax.broadcasted_iota(jnp.int32, (ns, LANES), 1)
    groups = []
    for kv, (hm, k_ref) in enumerate(((lo, k0_ref), (jnp.logical_not(lo), k1_ref))):
        q3 = [(jnp.where(hm, q_ref[:, p * LANES:(p + 1) * LANES], 0) * QK_SCALE).astype(CDT) for p in range(3)]
        selb = selb_ref[:, kv * ns:(kv + 1) * ns]
        lane0 = HEAD_DIM if kv == 0 else 0
        groups.append((kv, hm, k_ref, q3, selb, lane0))
    m_ref[...] = jnp.full(m_ref.shape, NEG_INF, F32)
    l_ref[...] = jnp.zeros(l_ref.shape, F32)
    acc_ref[...] = jnp.zeros(acc_ref.shape, F32)

    def scores(j, nk):
        k0 = pl.multiple_of(j * tk, tk)
        for kv, hm, k_ref, q3, selb, lane0 in groups:
            place = ((p_col >= lane0) & (p_col < lane0 + bpt) & (p_row == p_col - lane0 + j * bpt)).astype(CDT)
            sb = _dot(selb, place).astype(CDT)
            qp = jnp.concatenate([jnp.where(hm, q, sb) for q in q3], axis=0)
            s_ref[kv, 0:nk, :] = _nt_dot(k_ref[pl.ds(k0, nk), :], qp)

    def softmax_pv(j, diagonal, nk):
        k0 = pl.multiple_of(j * tk, tk)
        for kv in range(2):
            for c0 in range(0, n_q, LANES):
                cols = slice(c0, c0 + LANES)
                t_lane = i * tq + (c0 % tq) + lax.broadcasted_iota(jnp.int32, (1, LANES), 1)

                def chunk(r0):
                    s = s_ref[kv, r0:r0 + rc, cols]
                    if diagonal:
                        kpos = k0 + r0 + lax.broadcasted_iota(jnp.int32, (rc, LANES), 0)
                        s = jnp.where(kpos <= t_lane, s, NEG_INF)
                    return s

                m8 = m_ref[kv, :, cols]
                for r0 in range(0, nk, rc):
                    m8 = jnp.maximum(m8, jnp.max(chunk(r0).reshape(rc // 8, 8, LANES), axis=0))
                m_new = jnp.max(m8, axis=0, keepdims=True)
                alpha = jnp.exp(m_ref[kv, :, cols] - m_new)
                l8 = jnp.zeros((8, LANES), F32)
                for r0 in range(0, nk, rc):
                    e = jnp.exp(chunk(r0) - m_new)
                    l8 = l8 + jnp.sum(e.reshape(rc // 8, 8, LANES), axis=0)
                    e_ref[kv, r0:r0 + rc, cols] = e.astype(CDT)
                l_ref[kv, :, cols] = alpha * l_ref[kv, :, cols] + jnp.sum(l8, axis=0, keepdims=True)
                m_ref[kv, :, cols] = jnp.broadcast_to(m_new, (8, LANES))
                a_ref[kv, :, cols] = alpha
        for kv in range(2):
            acc_ref[kv] = a_ref[kv, 0:1, :] * acc_ref[kv] + _dot(vt_ref[j, :, 0:nk], e_ref[kv, 0:nk, :])

    def tile(j, diagonal, nk=tk):
        scores(j, nk)
        softmax_pv(j, diagonal, nk)

    lax.fori_loop(0, n_tiles - 1, lambda j, c: (tile(j, False), c)[1], 0)
    last = n_tiles - 1
    half = tk // 2
    first_half_only = (i + 1) * tq - last * tk <= half

    @pl.when(first_half_only)
    def _():
        tile(last, True, half)

    @pl.when(jnp.logical_not(first_half_only))
    def _():
        tile(last, True)
    o0 = (acc_ref[0] / l_ref[0, 0:1, :]).T
    o1 = (acc_ref[1] / l_ref[1, 0:1, :]).T
    for p in range(3):
        o_ref[:, p * LANES:(p + 1) * LANES] = jnp.where(
            lo, o0[p * tq:(p + 1) * tq], o1[p * tq:(p + 1) * tq]).astype(o_ref.dtype)


def _slc(zr, qcol, ksc, vt, selb, tq=512):
    b, s, _ = zr.shape
    tq = min(tq, s)
    tk = vt.shape[2]
    assert tq % LANES == 0 and tk // SEL_BLOCK <= HEAD_DIM and s % tk == 0
    qw = 3 * LANES
    ns2 = selb.shape[2]
    pat = jax.nn.one_hot((jnp.arange(s) // SEL_BLOCK) % (tk // SEL_BLOCK), HEAD_DIM, dtype=ksc.dtype)
    pat = jnp.broadcast_to(pat[None], (b, s, HEAD_DIM))
    k0 = jnp.concatenate([ksc[..., :HEAD_DIM], pat], axis=-1)
    k1 = jnp.concatenate([pat, ksc[..., HEAD_DIM:]], axis=-1)
    full = lambda bi, i: (bi, 0, 0)
    return pl.pallas_call(
        functools.partial(_slc_kernel, tq=tq, tk=tk),
        grid=(b, s // tq),
        in_specs=[pl.BlockSpec((None, tq, qw), lambda bi, i: (bi, i, qcol)),
                  _single_buffered((None, s, LANES), full),
                  _single_buffered((None, s, LANES), full),
                  _single_buffered((s // tk, LANES, tk), lambda bi, i: (bi, 0, 0)),
                  pl.BlockSpec((None, tq, ns2), lambda bi, i: (bi, i, 0))],
        out_specs=pl.BlockSpec((None, tq, qw), lambda bi, i: (bi, i, 0)),
        out_shape=jax.ShapeDtypeStruct((b, s, qw), CDT),
        scratch_shapes=[pltpu.VMEM((2, tk, 3 * tq), F32), pltpu.VMEM((2, tk, 3 * tq), CDT),
                        pltpu.VMEM((2, 8, 3 * tq), F32), pltpu.VMEM((2, 8, 3 * tq), F32),
                        pltpu.VMEM((2, 8, 3 * tq), F32), pltpu.VMEM((2, LANES, 3 * tq), F32)],
        compiler_params=_cparams(("parallel", "arbitrary"), 14),
        name="nsa_slc",
    )(zr, k0, k1, vt, selb)


def _merge_kernel(x_ref, oa0, oa1, oa2, la0, la1, la2, ob, ocmp, oslc, owin, od,
                  wn_ref, wg_ref, bg_ref, wb_ref, wo_ref, g_ref, b_ref, o_ref, *scratch, alpha):
    x = x_ref[...]
    xb = x.astype(CDT)
    bw = BRANCH_WIDTH

    def token_rows(src_ref, dst_ref):
        dil = src_ref.shape[1] // bw
        if dil == 1:
            return src_ref[...].astype(F32)
        n_cb = bw // LANES
        for r in range(dil):
            for cb in range(n_cb):
                c0 = r * bw + cb * LANES
                dst_ref[cb, pl.ds(r, src_ref.shape[0], stride=dil), :] = src_ref[:, c0:c0 + LANES].astype(F32)
        return jnp.concatenate([dst_ref[cb] for cb in range(n_cb)], axis=1)

    o0, o1, o2 = (token_rows(s, d) for s, d in zip((oa0, oa1, oa2), scratch[0:3]))
    l0, l1, l2 = (token_rows(s, d) for s, d in zip((la0, la1, la2), scratch[3:6]))
    mx = jnp.maximum(jnp.maximum(l0, l1), l2)
    w0, w1, w2 = jnp.exp(l0 - mx), jnp.exp(l1 - mx), jnp.exp(l2 - mx)
    o_a = (w0 * o0 + w1 * o1 + w2 * o2) / (w0 + w1 + w2)
    gates = jax.nn.sigmoid(_dot(xb, wn_ref[...]))
    o_c = (gates[:, 0:bw] * ocmp[...].astype(F32) + gates[:, bw:2 * bw] * oslc[...].astype(F32)
           + gates[:, 2 * bw:3 * bw] * owin[...].astype(F32))
    branches = (o_a.astype(CDT), ob[...], o_c.astype(CDT), od[...])
    d = x.shape[1]
    merged_ref = scratch[6]
    for c0 in range(0, d, MERGE_COLS):
        acc = jnp.zeros((x.shape[0], MERGE_COLS), F32)
        for m in range(N_BRANCH):
            gcols = slice(m * d + c0, m * d + c0 + MERGE_COLS)
            gate = jax.nn.sigmoid(_dot(xb, wg_ref[:, gcols]) + bg_ref[:, gcols])
            acc = acc + gate * _dot(branches[m], wb_ref[m, :, c0:c0 + MERGE_COLS])
        merged_ref[:, c0:c0 + MERGE_COLS] = acc.astype(CDT)
    r = alpha * x + _dot(merged_ref[...], wo_ref[...])
    o_ref[...] = _layer_norm(r, g_ref[...], b_ref[...])


MERGE_COLS = 256


def _merge(x2d, branch_inputs, wn, wg, bg, wb, wo, g, b, alpha, tm=256):
    t, d = x2d.shape
    tm = min(tm, t)
    row = lambda i: (i, 0)
    const2 = lambda i: (0, 0)
    in_specs = [pl.BlockSpec((tm, d), row)]
    in_specs += [pl.BlockSpec((tm * a.shape[0] // t, a.shape[1]), row) for a in branch_inputs]
    in_specs += [pl.BlockSpec(wn.shape, const2), pl.BlockSpec(wg.shape, const2), pl.BlockSpec(bg.shape, const2),
                 pl.BlockSpec(wb.shape, lambda i: (0, 0, 0)), pl.BlockSpec(wo.shape, const2),
                 pl.BlockSpec(g.shape, const2), pl.BlockSpec(b.shape, const2)]
    return pl.pallas_call(
        functools.partial(_merge_kernel, alpha=alpha),
        grid=(t // tm,),
        in_specs=in_specs,
        out_specs=pl.BlockSpec((tm, d), row),
        out_shape=jax.ShapeDtypeStruct((t, d), F32),
        scratch_shapes=[pltpu.VMEM((BRANCH_WIDTH // LANES, tm, LANES), F32) for _ in range(2 * N_DIL)]
        + [pltpu.VMEM((tm, d), CDT)],
        compiler_params=_cparams(("parallel",), 14),
        name="merge_ln",
    )(x2d, *branch_inputs, wn, wg, bg, wb, wo, g, b)


def _ple_ln(x, xb, f, p, plw_ref, pgw_ref, pgb_ref, g_ref, b_ref, alpha):
    ple = jax.nn.sigmoid(_dot(xb, pgw_ref[...]) + pgb_ref[...]) * _dot(p.astype(CDT), plw_ref[...])
    return _layer_norm(alpha * x + f + ple, g_ref[...], b_ref[...])


FFN_CHUNK = 512


def _ffn_kernel(x_ref, p_ref, wg_ref, wu_ref, wd_ref, plw_ref, pgw_ref, pgb_ref, g_ref, b_ref, o_ref, h_ref, *, alpha):
    x = x_ref[...]
    xb = x.astype(CDT)
    dff = wg_ref.shape[1]
    for c0 in range(0, dff, FFN_CHUNK):
        cols = slice(c0, min(c0 + FFN_CHUNK, dff))
        h_ref[:, cols] = (jax.nn.silu(_dot(xb, wg_ref[:, cols])) * _dot(xb, wu_ref[:, cols])).astype(CDT)
    f = _dot(h_ref[...], wd_ref[...])
    o_ref[...] = _ple_ln(x, xb, f, p_ref[...], plw_ref, pgw_ref, pgb_ref, g_ref, b_ref, alpha)


def _ffn(x2d, p2d, wg, wu, wd, plw, pgw, pgb, g, b, alpha, tm=512):
    t, d = x2d.shape
    tm = min(tm, t)
    dff = wg.shape[1]
    wg, wu, wd = wg.astype(CDT), wu.astype(CDT), wd.astype(CDT)
    row = lambda i: (i, 0)
    const = lambda shape: pl.BlockSpec(shape, lambda i: (0, 0), pipeline_mode=pl.Buffered(1))
    return pl.pallas_call(
        functools.partial(_ffn_kernel, alpha=alpha),
        grid=(t // tm,),
        in_specs=[pl.BlockSpec((tm, d), row), pl.BlockSpec((tm, p2d.shape[1]), row),
                  const(wg.shape), const(wu.shape), const(wd.shape),
                  const(plw.shape), const(pgw.shape), const(pgb.shape), const(g.shape), const(b.shape)],
        out_specs=pl.BlockSpec((tm, d), row),
        out_shape=jax.ShapeDtypeStruct((t, d), F32),
        scratch_shapes=[pltpu.VMEM((tm, dff), CDT)],
        compiler_params=_cparams(("parallel",), 12),
        name="ffn_ple_ln",
    )(x2d, p2d, wg, wu, wd, plw, pgw, pgb, g, b)


def _router_kernel(x_ref, wh_ref, wl_ref, b_ref, comb_ref, rank_ref, rank_t_ref, cnt_ref):
    x = x_ref[...]
    xh = x.astype(CDT)
    xl = (x - xh.astype(F32)).astype(CDT)
    logits = _dot(xh, wh_ref[...]) + _dot(xh, wl_ref[...]) + _dot(xl, wh_ref[...]) + b_ref[...]
    lane = lax.broadcasted_iota(jnp.int32, logits.shape, 1)
    v1 = jnp.max(logits, axis=-1, keepdims=True)
    i1 = jnp.min(jnp.where(logits == v1, lane, LANES), axis=-1, keepdims=True)
    rest = jnp.where(lane == i1, -jnp.inf, logits)
    v2 = jnp.max(rest, axis=-1, keepdims=True)
    i2 = jnp.min(jnp.where(rest == v2, lane, LANES), axis=-1, keepdims=True)
    e2 = jnp.exp(v2 - v1)
    comb_ref[...] = jnp.where(lane == i1, 1.0 / (1.0 + e2), 0.0) + jnp.where(lane == i2, e2 / (1.0 + e2), 0.0)
    routed = (lane == i1) | (lane == i2)
    mask = routed.astype(CDT)
    tm = x.shape[0]
    before = (lax.broadcasted_iota(jnp.int32, (tm, tm), 1) < lax.broadcasted_iota(jnp.int32, (tm, tm), 0)).astype(CDT)
    rank = jnp.where(routed, _dot(before, mask), -1.0)
    rank_ref[...] = rank
    rank_t_ref[...] = rank.T[0:rank_t_ref.shape[0], :]
    cnt_ref[...] = jnp.sum(routed.astype(F32), axis=0, keepdims=True).astype(jnp.int32)


def _router(x2d, w_router, b_router, tm):
    t, d = x2d.shape
    ne = w_router.shape[1]
    wp = jnp.zeros((d, LANES), F32).at[:, :ne].set(w_router)
    wh = wp.astype(CDT)
    wl = (wp - wh.astype(F32)).astype(CDT)
    bp = jnp.full((1, LANES), -BIG, F32).at[0, :ne].set(b_router)
    nt = t // tm
    row = lambda i: (i, 0)
    return pl.pallas_call(
        _router_kernel,
        grid=(nt,),
        in_specs=[pl.BlockSpec((tm, d), row), pl.BlockSpec(wh.shape, lambda i: (0, 0)),
                  pl.BlockSpec(wl.shape, lambda i: (0, 0)), pl.BlockSpec(bp.shape, lambda i: (0, 0))],
        out_specs=[pl.BlockSpec((tm, LANES), row), pl.BlockSpec((tm, LANES), row),
                   pl.BlockSpec((None, 8, tm), lambda i: (i, 0, 0)),
                   pl.BlockSpec((None, 1, LANES), lambda i: (i, 0, 0))],
        out_shape=[jax.ShapeDtypeStruct((t, LANES), F32), jax.ShapeDtypeStruct((t, LANES), F32),
                   jax.ShapeDtypeStruct((nt, 8, tm), F32), jax.ShapeDtypeStruct((nt, 1, LANES), jnp.int32)],
        compiler_params=_cparams(("parallel",), 10),
        name="moe_router",
    )(x2d, wh, wl, bp)


def _moe_kernel(cnt_ref, x_ref, comb_ref, rank_ref, rank_t_ref, p_ref, wg_ref, wu_ref, wd_ref, plw_ref, pgw_ref,
                pgb_ref, g_ref, b_ref, o_ref, xe_ref, ye_ref, *, alpha, rs, seg):
    i = pl.program_id(0)
    e = pl.program_id(1)
    c = pl.program_id(2)
    n_seg = x_ref.shape[0] // seg
    last_chunk = c == pl.num_programs(2) - 1

    first = slice(0, rs)
    toks = [slice(sg * seg, (sg + 1) * seg) for sg in range(n_seg)]
    n_groups = [(cnt_ref[(i * n_seg + sg) * LANES + e] + rs - 1) // rs for sg in range(n_seg)]

    def later(sc):
        return pl.ds(pl.multiple_of(sc * rs, 8), rs), (sc * rs).astype(F32)

    @pl.when((e == 0) & (c == 0))
    def _():
        o_ref[...] = jnp.zeros_like(o_ref)

    @pl.when(c == 0)
    def _():
        row_id = lax.broadcasted_iota(jnp.int32, (rs, seg), 0).astype(F32)
        xbs = [x_ref[tok, :].astype(CDT) for tok in toks]
        rank_rows = [rank_t_ref[sg, pl.ds(e, 1), :] for sg in range(n_seg)]

        def gather(sg, rws, base):
            onehot = (rank_rows[sg] - base == row_id).astype(CDT)
            xe_ref[sg, rws, :] = _dot(onehot, xbs[sg]).astype(CDT)
            ye_ref[sg, rws, :] = jnp.zeros((rs, ye_ref.shape[2]), F32)

        for sg in range(n_seg):
            gather(sg, first, 0.0)
        for sg in range(n_seg):
            lax.fori_loop(1, n_groups[sg], lambda sc, _, sg=sg: (gather(sg, *later(sc)), 0)[1], 0)

    def expert(sg, rws):
        xs = xe_ref[sg, rws, :]
        h = jax.nn.silu(_dot(xs, wg_ref[0])) * _dot(xs, wu_ref[0])
        ye_ref[sg, rws, :] += _dot(h.astype(CDT), wd_ref[0])

    for sg in range(n_seg):
        expert(sg, first)
    for sg in range(n_seg):
        lax.fori_loop(1, n_groups[sg], lambda sc, _, sg=sg: (expert(sg, later(sc)[0]), 0)[1], 0)

    @pl.when(last_chunk)
    def _():
        lane = lax.broadcasted_iota(jnp.int32, (seg, LANES), 1)
        mine = lane == e
        col_id = lax.broadcasted_iota(jnp.int32, (seg, rs), 1).astype(F32)
        cws = [jnp.sum(jnp.where(mine, comb_ref[tok, :], 0.0), axis=-1, keepdims=True) for tok in toks]
        rank_cols = [jnp.sum(jnp.where(mine, rank_ref[tok, :], 0.0), axis=-1, keepdims=True) for tok in toks]

        def scatter(sg, rws, base):
            onehot = (rank_cols[sg] - base == col_id).astype(CDT)
            o_ref[toks[sg], :] += cws[sg] * _dot(onehot, ye_ref[sg, rws, :].astype(CDT))

        for sg in range(n_seg):
            scatter(sg, first, 0.0)
        for sg in range(n_seg):
            lax.fori_loop(1, n_groups[sg], lambda sc, _, sg=sg: (scatter(sg, *later(sc)), 0)[1], 0)

    @pl.when((e == pl.num_programs(1) - 1) & last_chunk)
    def _():
        for sg in range(n_seg):
            tok = slice(sg * seg, (sg + 1) * seg)
            x = x_ref[tok, :]
            o_ref[tok, :] = _ple_ln(x, x.astype(CDT), o_ref[tok, :], p_ref[tok, :], plw_ref, pgw_ref, pgb_ref,
                                    g_ref, b_ref, alpha)


MOE_CHUNK = 512
MOE_SEGMENT = 1024
MOE_SEGMENTS_PER_TILE = 2
MOE_ROW_GROUP = 288


def _single_buffered(shape, index_map):
    return pl.BlockSpec(shape, index_map, pipeline_mode=pl.Buffered(1))


def _moe(x2d, routing, p2d, wg, wu, wd, plw, pgw, pgb, g, b, alpha, seg):
    comb, rank, rank_t, cnt = routing
    t, d = x2d.shape
    ne, _, dff = wg.shape
    ck = min(MOE_CHUNK, dff)
    rs = min(MOE_ROW_GROUP, seg)
    n_seg = min(MOE_SEGMENTS_PER_TILE, t // seg)
    tm = n_seg * seg
    max_rows = -(-seg // rs) * rs
    wg, wu, wd = wg.astype(CDT), wu.astype(CDT), wd.astype(CDT)
    row = lambda i, e, c, cnt: (i, 0)
    c2 = lambda i, e, c, cnt: (0, 0)
    grid_spec = pltpu.PrefetchScalarGridSpec(
        num_scalar_prefetch=1,
        grid=(t // tm, ne, dff // ck),
        in_specs=[_single_buffered((tm, d), row), _single_buffered((tm, LANES), row),
                  _single_buffered((tm, LANES), row),
                  _single_buffered((n_seg, 8, seg), lambda i, e, c, cnt: (i, 0, 0)),
                  _single_buffered((tm, p2d.shape[1]), row),
                  pl.BlockSpec((1, d, ck), lambda i, e, c, cnt: (e, 0, c)),
                  pl.BlockSpec((1, d, ck), lambda i, e, c, cnt: (e, 0, c)),
                  pl.BlockSpec((1, ck, d), lambda i, e, c, cnt: (e, c, 0)),
                  _single_buffered(plw.shape, c2), _single_buffered(pgw.shape, c2), _single_buffered(pgb.shape, c2),
                  _single_buffered(g.shape, c2), _single_buffered(b.shape, c2)],
        out_specs=pl.BlockSpec((tm, d), row),
        scratch_shapes=[pltpu.VMEM((n_seg, max_rows, d), CDT), pltpu.VMEM((n_seg, max_rows, d), F32)],
    )
    return pl.pallas_call(
        functools.partial(_moe_kernel, alpha=alpha, rs=rs, seg=seg),
        grid_spec=grid_spec,
        out_shape=jax.ShapeDtypeStruct((t, d), F32),
        compiler_params=_cparams(("parallel", "arbitrary", "arbitrary"), 15),
        name="moe_ple_ln",
    )(cnt.reshape(-1), x2d, comb, rank, rank_t, p2d, wg, wu, wd, plw, pgw, pgb, g, b)


def _prep_in_weights(w_in):
    o = COL_OFF
    bw = BRANCH_WIDTH
    cols = lambda n: w_in[:, o[n]:o[n + 1]]
    qa, ka, va = cols(0), cols(1), cols(2)
    w_dil = jnp.concatenate(
        [t[:, g * bw:(g + 1) * bw] for g in range(N_DIL) for t in (qa, ka, va)], axis=1).astype(CDT)
    w_conv = jnp.concatenate([cols(3), cols(4), cols(5)], axis=1).astype(CDT)
    gn = cols(13)
    w_gate = jnp.concatenate([gn[:, br * NSA_Q_HEADS + GQA_COL_HEAD] for br in range(3)], axis=1).astype(CDT)
    w_rest = jnp.concatenate([cols(6)[:, GQA_COL_PERM], cols(14)[:, GQA_COL_PERM], cols(11),
                              cols(12), cols(15), cols(16), cols(9), cols(7), cols(8)], axis=1).astype(CDT)
    wt_vsc = cols(10).T.astype(CDT)
    return w_dil, w_conv, w_gate, w_rest, wt_vsc


ZR_Q_NSA, ZR_Q_SWA = 0, 1
ZR_KWC, ZR_VWC, ZR_KD, ZR_VD = 6, 7, 8, 9
ZR_WIDTH = 2 * BRANCH_WIDTH + 4 * LANES
SLC_KEY_TILE = 1024


def _token_mixers(x, w_in, conv_w, cmp_pos, cmp_w1, cmp_b1, cmp_w2, cmp_b2, sinks):
    b, s, d = x.shape
    x2d = x.reshape(b * s, d)
    w_dil, w_conv, w_gate, w_rest, wt_vsc = _prep_in_weights(w_in)
    gw = 3 * BRANCH_WIDTH

    z_dil = _linear(x2d, w_dil, [(g * gw, (g + 1) * gw) for g in range(N_DIL)], gw, "in_proj_dil",
                    dils=[dil for _, dil in DIL_PATTERNS])
    zr, ksc, kcc, vcc = _linear(x2d, w_rest, [(0, ZR_WIDTH)] + [(ZR_WIDTH + n * LANES, ZR_WIDTH + (n + 1) * LANES)
                                                           for n in range(3)], 256, "in_proj_rest")
    zr = zr.reshape(b, s, ZR_WIDTH)
    o_b = _conv(x, w_conv, conv_w)

    dil_o, dil_lse = [], []
    for g, (window, dil) in enumerate(DIL_PATTERNS):
        view = z_dil[g].reshape(b, s // dil, dil * gw)
        og, lg = _banded(view, view, view, nrep=dil,
                         qcol=lambda r: 3 * r, kcol=lambda r: 3 * r + 1, vcol=lambda r: 3 * r + 2,
                         kw=3 * LANES, window=window // dil, want_lse=True)
        dil_o.append(og.reshape(b * s // dil, dil * BRANCH_WIDTH))
        dil_lse.append(lg.reshape(b * s // dil, dil * BRANCH_WIDTH))

    kc = _compress(kcc.reshape(b, s, LANES), cmp_pos[0], cmp_w1[0], cmp_b1[0], cmp_w2[0], cmp_b2[0])
    vc = _compress(vcc.reshape(b, s, LANES), cmp_pos[1], cmp_w1[1], cmp_b1[1], cmp_w2[1], cmp_b2[1])
    o_cmp, selb = _cmp_select(zr, ZR_Q_NSA, kc, vc)
    vsc_t = _linear_t(x2d, wt_vsc, min(SLC_KEY_TILE, s), "in_proj_vsc_t")
    o_slc = _slc(zr, ZR_Q_NSA, ksc.reshape(b, s, LANES), vsc_t, selb)
    (o_win,) = _banded(zr, zr, zr, nrep=1, qcol=lambda r: ZR_Q_NSA, kcol=lambda r: ZR_KWC, vcol=lambda r: ZR_VWC,
                       kw=LANES, window=NSA_WINDOW - 1, want_lse=False, tq=512)

    sink_row = sinks.astype(F32)[GQA_COL_HEAD].reshape(1, BRANCH_WIDTH)
    (o_d,) = _banded(zr, zr, zr, nrep=1, qcol=lambda r: ZR_Q_SWA, kcol=lambda r: ZR_KD, vcol=lambda r: ZR_VD,
                     kw=LANES, window=SWA_WINDOW - 1, want_lse=False, sink_row=sink_row)

    t = b * s
    flat = lambda a: a.reshape(t, a.shape[-1])
    return [dil_o[0], dil_o[1], dil_o[2], dil_lse[0], dil_lse[1], dil_lse[2], flat(o_b), flat(o_cmp), flat(o_slc),
            flat(o_win), flat(o_d)], w_gate


def kernel(x, p, w_in, conv_w, cmp_pos, cmp_w1, cmp_b1, cmp_w2, cmp_b2, sinks, w_branch, w_merge_gate, b_merge_gate, w_out, ln_mix_g, ln_mix_b, ffn_w_gate, ffn_w_up, ffn_w_down, w_router, b_router, moe_w_gate, moe_w_up, moe_w_down, ple_w, ple_gate_w, ple_gate_b, ln_ffn_g, ln_ffn_b):
    depth, b, s, _ = p.shape
    d = x.shape[-1]
    t = b * s
    alpha = (2 * depth) ** 0.25
    row = lambda v: v.reshape(1, -1).astype(F32)
    for i in range(depth):
        branch_inputs, w_nsa_gate = _token_mixers(x, w_in[i], conv_w[i], cmp_pos[i], cmp_w1[i], cmp_b1[i], cmp_w2[i],
                                                  cmp_b2[i], sinks[i])
        wg = jnp.concatenate([w_merge_gate[i, m] for m in range(N_BRANCH)], axis=1).astype(CDT)
        bg = b_merge_gate[i].reshape(1, N_BRANCH * d).astype(F32)
        wb = jnp.stack([w_branch[i, 0], w_branch[i, 1], w_branch[i, 2][GQA_COL_PERM],
                        w_branch[i, 3][GQA_COL_PERM]]).astype(CDT)
        x1 = _merge(x.reshape(t, d), branch_inputs, w_nsa_gate, wg, bg, wb, w_out[i].astype(CDT),
                    row(ln_mix_g[i]), row(ln_mix_b[i]), alpha)
        p2d = p[i].reshape(t, -1)
        ple_args = (ple_w[i].astype(CDT), ple_gate_w[i].astype(CDT), row(ple_gate_b[i]),
                    row(ln_ffn_g[i]), row(ln_ffn_b[i]))
        j = i // 2
        if i % 2 == 0:
            x2 = _ffn(x1, p2d, ffn_w_gate[j], ffn_w_up[j], ffn_w_down[j], *ple_args, alpha)
        else:
            seg = min(MOE_SEGMENT, t)
            routing = _router(x1, w_router[j], b_router[j], seg)
            x2 = _moe(x1, routing, p2d, moe_w_gate[j], moe_w_up[j], moe_w_down[j], *ple_args, alpha, seg)
        x = x2.reshape(b, s, d)
    return x
```

```python
import functools

import numpy as np
import jax
import jax.numpy as jnp
from jax import lax
from jax.experimental import pallas as pl
from jax.experimental.pallas import tpu as pltpu

HEAD_DIM = 64
DIL_PATTERNS = ((128, 1), (512, 4), (2048, 16))
N_DIL = 3
DIL_HEADS = 6
CONV_WIDTH = 384
NSA_Q_HEADS = 6
NSA_KV_HEADS = 2
CMP_BLOCK = 32
CMP_STRIDE = 16
CMP_HIDDEN = 128
SEL_BLOCK = 64
N_SEL = 16
NSA_WINDOW = 512
SWA_Q_HEADS = 6
SWA_WINDOW = 128
BRANCH_WIDTH = 384
N_BRANCH = 4
LN_EPS = 1e-5
NEG_INF = -1e30
DIL_WIDTH = N_DIL * DIL_HEADS * HEAD_DIM
COLUMN_SIZES = (DIL_WIDTH, DIL_WIDTH, DIL_WIDTH, CONV_WIDTH, CONV_WIDTH, CONV_WIDTH,
                NSA_Q_HEADS * HEAD_DIM, 128, 128, 128, 128, 128, 128, 3 * NSA_Q_HEADS,
                SWA_Q_HEADS * HEAD_DIM, 128, 128)
COL_OFF = np.concatenate([[0], np.cumsum(COLUMN_SIZES)]).tolist()

LANES = 128
V7X_VMEM_BYTES = 64 * 1024 * 1024

CDT = jnp.bfloat16
F32 = jnp.float32
QK_SCALE = HEAD_DIM ** -0.5
SUB_Q = 128
BAND_ROWS = 32
BIG = 1e30

_GQA_HEAD_ORDER = (0, 3, 1, 4, 2, 5)
GQA_COL_PERM = np.concatenate([np.arange(h * HEAD_DIM, (h + 1) * HEAD_DIM) for h in _GQA_HEAD_ORDER])
GQA_COL_HEAD = GQA_COL_PERM // HEAD_DIM


def _cparams(sem, vmem_sixteenths):
    return pltpu.CompilerParams(dimension_semantics=sem, vmem_limit_bytes=V7X_VMEM_BYTES * vmem_sixteenths // 16)


def _nt_dot(a, b):
    return lax.dot_general(a, b, (((1,), (1,)), ((), ())), preferred_element_type=F32)


def _dot(a, b):
    return jnp.dot(a, b, preferred_element_type=F32)


def _layer_norm(r, g, b):
    mu = jnp.mean(r, axis=-1, keepdims=True)
    d = r - mu
    var = jnp.mean(d * d, axis=-1, keepdims=True)
    return d * lax.rsqrt(var + LN_EPS) * g + b


def _half_masks():
    lane = lax.broadcasted_iota(jnp.int32, (1, LANES), 1)
    return lane < HEAD_DIM


def _linear_kernel(x_ref, w_ref, *refs, splits, n_chunk, dils):
    o_refs, z_ref = refs[:len(splits)], refs[len(splits)]
    xb = x_ref[...].astype(CDT)
    tm = xb.shape[0]
    for o_ref, (c0, c1), dil in zip(o_refs, splits, dils):
        width = c1 - c0
        if dil == 1:
            for a in range(c0, c1, n_chunk):
                b = min(a + n_chunk, c1)
                o_ref[:, a - c0:b - c0] = _dot(xb, w_ref[:, a:b]).astype(o_ref.dtype)
        else:
            z = _dot(xb, w_ref[:, c0:c1])
            for cb in range(width // LANES):
                z_ref[cb] = z[:, cb * LANES:(cb + 1) * LANES]
            for r in range(dil):
                for cb in range(width // LANES):
                    o_ref[:, r * width + cb * LANES:r * width + (cb + 1) * LANES] = (
                        z_ref[cb, pl.ds(r, tm // dil, stride=dil), :].astype(o_ref.dtype))


def _linear(x2d, w, splits, n_chunk, name, dils=None, tm=512):
    t, k = x2d.shape
    tm = min(tm, t)
    n = w.shape[1]
    dils = tuple(dils) if dils is not None else (1,) * len(splits)
    widths = [c1 - c0 for c0, c1 in splits]
    assert all(tm % (16 * dl) == 0 for dl in dils)
    return pl.pallas_call(
        functools.partial(_linear_kernel, splits=tuple(splits), n_chunk=n_chunk, dils=dils),
        grid=(t // tm,),
        in_specs=[pl.BlockSpec((tm, k), lambda i: (i, 0)),
                  pl.BlockSpec((k, n), lambda i: (0, 0))],
        out_specs=[pl.BlockSpec((tm // dl, dl * wd), lambda i: (i, 0)) for wd, dl in zip(widths, dils)],
        out_shape=[jax.ShapeDtypeStruct((t // dl, dl * wd), CDT) for wd, dl in zip(widths, dils)],
        scratch_shapes=[pltpu.VMEM((max(widths) // LANES, tm, LANES), F32)],
        compiler_params=_cparams(("parallel",), 12),
        name=name,
    )(x2d, w)


def _linear_t_kernel(x_ref, wt_ref, o_ref):
    o_ref[...] = _nt_dot(wt_ref[...], x_ref[...].astype(CDT)).astype(o_ref.dtype)


def _linear_t(x2d, wt, tm, name):
    t, k = x2d.shape
    n = wt.shape[0]
    return pl.pallas_call(
        _linear_t_kernel,
        grid=(t // tm,),
        in_specs=[pl.BlockSpec((tm, k), lambda i: (i, 0)), pl.BlockSpec((n, k), lambda i: (0, 0))],
        out_specs=pl.BlockSpec((None, n, tm), lambda i: (i, 0, 0)),
        out_shape=jax.ShapeDtypeStruct((t // tm, n, tm), CDT),
        compiler_params=_cparams(("parallel",), 8),
        name=name,
    )(x2d, wt)


def _conv_kernel(x_ref, xh_ref, wc_ref, cw_ref, ob_ref, *, tm):
    i = pl.program_id(1)
    w = CONV_WIDTH
    xb = x_ref[...].astype(CDT)
    z = _dot(xb, wc_ref[...])
    u = z[:, w:2 * w] * z[:, 2 * w:3 * w]
    zh = _dot(xh_ref[...].astype(CDT), wc_ref[:, w:3 * w])
    uh = zh[:, :w] * zh[:, w:]
    uh = jnp.where(i == 0, 0.0, uh)
    row = lax.broadcasted_iota(jnp.int32, (tm, w), 0)
    u1 = jnp.where(row == 0, uh[7:8, :], pltpu.roll(u, 1, 0))
    u2 = jnp.where(row == 0, uh[6:7, :], jnp.where(row == 1, uh[7:8, :], pltpu.roll(u, 2, 0)))
    y = cw_ref[0:1, :] * u2 + cw_ref[1:2, :] * u1 + cw_ref[2:3, :] * u
    ob_ref[...] = (z[:, :w] * y).astype(ob_ref.dtype)


def _conv(x, wc, conv_w, tm=512):
    b, s, d = x.shape
    tm = min(tm, s)
    hb = tm // 8
    return pl.pallas_call(
        functools.partial(_conv_kernel, tm=tm),
        grid=(b, s // tm),
        in_specs=[pl.BlockSpec((None, tm, d), lambda bi, i: (bi, i, 0)),
                  pl.BlockSpec((None, 8, d), lambda bi, i: (bi, jnp.maximum(i * hb - 1, 0), 0)),
                  pl.BlockSpec(wc.shape, lambda bi, i: (0, 0)),
                  pl.BlockSpec(conv_w.shape, lambda bi, i: (0, 0))],
        out_specs=pl.BlockSpec((None, tm, CONV_WIDTH), lambda bi, i: (bi, i, 0)),
        out_shape=jax.ShapeDtypeStruct((b, s, CONV_WIDTH), CDT),
        compiler_params=_cparams(("parallel", "parallel"), 12),
        name="short_conv",
    )(x, x, wc, conv_w)


def _banded_kernel(*refs, window, pr, tq, kw, want_lse, has_sink):
    q_ref, kp_ref, kc_ref, vp_ref, vc_ref = refs[:5]
    n = 5
    sink_ref = None
    if has_sink:
        sink_ref = refs[n]
        n += 1
    o_ref = refs[n]
    n += 1
    lse_ref = None
    if want_lse:
        lse_ref = refs[n]
        n += 1
    kbuf, vbuf, s_ref, e_ref, m_ref, l_ref = refs[n:n + 6]

    i = pl.program_id(2)
    kbuf[0:pr, :] = kp_ref[...]
    kbuf[pr:pr + tq, :] = kc_ref[...]
    vbuf[0:pr, :] = vp_ref[...]
    vbuf[pr:pr + tq, :] = vc_ref[...]

    span = SUB_Q + pr
    qi = lax.broadcasted_iota(jnp.int32, (SUB_Q, span), 0)
    kj = lax.broadcasted_iota(jnp.int32, (SUB_Q, span), 1)
    dist = pr + qi - kj
    band = (dist >= 0) & (dist <= window)
    lo = _half_masks()
    halves = (lo, jnp.logical_not(lo))
    groups = ((0,), (1,), (2,)) if kw == 3 * LANES else ((0, 1, 2),)
    rb = BAND_ROWS
    for sb in range(tq // SUB_Q):
        r0 = sb * SUB_Q
        bias = jnp.where(band & (i * tq + r0 - pr + kj >= 0), 0.0, NEG_INF)
        for grp in groups:
            kc0 = grp[0] * LANES if kw == 3 * LANES else 0
            qs = jnp.concatenate(
                [(jnp.where(hm, q_ref[r0:r0 + SUB_Q, p * LANES:(p + 1) * LANES], 0) * QK_SCALE).astype(CDT)
                 for p in grp for hm in halves], axis=0)
            g0 = 2 * grp[0] * SUB_Q
            s_ref[g0:g0 + qs.shape[0], :] = _nt_dot(qs, kbuf[r0:r0 + span, kc0:kc0 + LANES])
        for c0 in range(0, 6 * SUB_Q, rb):
            rows = slice(c0, c0 + rb)
            s = s_ref[rows, :] + bias[c0 % SUB_Q:c0 % SUB_Q + rb, :]
            m = jnp.max(s, axis=-1, keepdims=True)
            e = jnp.exp(s - m)
            e_ref[rows, :] = e.astype(CDT)
            m_ref[rows, :] = jnp.broadcast_to(m, (rb, LANES))
            l_ref[rows, :] = jnp.broadcast_to(jnp.sum(e, axis=-1, keepdims=True), (rb, LANES))
        for grp in groups:
            kc0 = grp[0] * LANES if kw == 3 * LANES else 0
            g0 = 2 * grp[0] * SUB_Q
            g1 = g0 + 2 * len(grp) * SUB_Q
            l = l_ref[g0:g1, :]
            o = _dot(e_ref[g0:g1, :], vbuf[r0:r0 + span, kc0:kc0 + LANES]) / l
            lse = m_ref[g0:g1, :] + jnp.log(l)
            for n_p, p in enumerate(grp):
                a = 2 * n_p * SUB_Q
                o_pair = jnp.where(lo, o[a:a + SUB_Q], o[a + SUB_Q:a + 2 * SUB_Q])
                lse_pair = jnp.where(lo, lse[a:a + SUB_Q], lse[a + SUB_Q:a + 2 * SUB_Q])
                if has_sink:
                    o_pair = o_pair * jax.nn.sigmoid(lse_pair - sink_ref[:, p * LANES:(p + 1) * LANES])
                o_ref[r0:r0 + SUB_Q, p * LANES:(p + 1) * LANES] = o_pair.astype(o_ref.dtype)
                if want_lse:
                    lse_ref[r0:r0 + SUB_Q, p * LANES:(p + 1) * LANES] = lse_pair


def _banded(qa, ka, va, *, nrep, qcol, kcol, vcol, kw, window, want_lse, sink_row=None, tq=1024):
    b, l, _ = qa.shape
    pr = -(-window // SUB_Q) * SUB_Q
    tq = min(max(tq, pr), l)
    assert tq % pr == 0 and l % tq == 0, (tq, pr, l)
    ratio = tq // pr
    qw = 3 * LANES
    in_specs = [
        pl.BlockSpec((None, tq, qw), lambda bi, r, i: (bi, i, qcol(r))),
        pl.BlockSpec((None, pr, kw), lambda bi, r, i: (bi, jnp.maximum(i * ratio - 1, 0), kcol(r))),
        pl.BlockSpec((None, tq, kw), lambda bi, r, i: (bi, i, kcol(r))),
        pl.BlockSpec((None, pr, kw), lambda bi, r, i: (bi, jnp.maximum(i * ratio - 1, 0), vcol(r))),
        pl.BlockSpec((None, tq, kw), lambda bi, r, i: (bi, i, vcol(r))),
    ]
    args = [qa, ka, ka, va, va]
    if sink_row is not None:
        in_specs.append(pl.BlockSpec(sink_row.shape, lambda bi, r, i: (0, 0)))
        args.append(sink_row)
    out_specs = [pl.BlockSpec((None, tq, qw), lambda bi, r, i: (bi, i, r))]
    out_shape = [jax.ShapeDtypeStruct((b, l, nrep * qw), CDT)]
    if want_lse:
        out_specs.append(pl.BlockSpec((None, tq, qw), lambda bi, r, i: (bi, i, r)))
        out_shape.append(jax.ShapeDtypeStruct((b, l, nrep * qw), F32))
    res = pl.pallas_call(
        functools.partial(_banded_kernel, window=window, pr=pr, tq=tq, kw=kw, want_lse=want_lse,
                          has_sink=sink_row is not None),
        grid=(b, nrep, l // tq),
        in_specs=in_specs,
        out_specs=out_specs,
        out_shape=out_shape,
        scratch_shapes=[pltpu.VMEM((pr + tq, kw), ka.dtype), pltpu.VMEM((pr + tq, kw), va.dtype),
                        pltpu.VMEM((6 * SUB_Q, SUB_Q + pr), F32), pltpu.VMEM((6 * SUB_Q, SUB_Q + pr), CDT),
                        pltpu.VMEM((6 * SUB_Q, LANES), F32), pltpu.VMEM((6 * SUB_Q, LANES), F32)],
        compiler_params=_cparams(("parallel", "parallel", "parallel"), 8),
        name=f"banded_w{window}_k{kw}_r{nrep}",
    )(*args)
    return res


def _gelu_tanh(x):
    return 0.5 * x * (1.0 + jnp.tanh(0.7978845608028654 * (x + 0.044715 * (x * x * x))))


def _compress_kernel(x_ref, pa_ref, pb_ref, w1a_ref, w1b_ref, b1_ref, w2_ref, b2_ref, o_ref):
    x = x_ref[...].astype(F32)
    n = x.shape[0]
    a = _dot((x + pa_ref[...]).astype(CDT), w1a_ref[...])
    bm = _dot((x + pb_ref[...]).astype(CDT), w1b_ref[...])
    h = a + pltpu.roll(bm, n - 1, 0) + b1_ref[...]
    o_ref[...] = (_dot(_gelu_tanh(h).astype(CDT), w2_ref[...]) + b2_ref[...]).astype(o_ref.dtype)


def _compress(t, pos, w1, b1, w2, b2):
    b, s, _ = t.shape
    nch = s // CMP_STRIDE
    xw = CMP_STRIDE * LANES
    x = t.reshape(b, nch, xw)
    eye = jnp.eye(NSA_KV_HEADS, dtype=F32)
    w1r = w1.reshape(CMP_BLOCK, HEAD_DIM, CMP_HIDDEN)

    def expand_w1(part):
        return jnp.einsum('tdj,kl->tkdlj', part, eye).reshape(xw, NSA_KV_HEADS * CMP_HIDDEN).astype(CDT)

    def expand_pos(part):
        return jnp.broadcast_to(part[:, None, :], (CMP_STRIDE, NSA_KV_HEADS, HEAD_DIM)).reshape(1, xw)

    w1a, w1b = expand_w1(w1r[:CMP_STRIDE]), expand_w1(w1r[CMP_STRIDE:])
    pa, pb = expand_pos(pos[:CMP_STRIDE]), expand_pos(pos[CMP_STRIDE:])
    b1e = jnp.tile(b1, NSA_KV_HEADS).reshape(1, -1)
    w2e = jnp.einsum('jd,kl->kjld', w2, eye).reshape(NSA_KV_HEADS * CMP_HIDDEN, LANES).astype(CDT)
    b2e = jnp.tile(b2, NSA_KV_HEADS).reshape(1, -1)
    consts = [pa, pb, w1a, w1b, b1e, w2e, b2e]
    return pl.pallas_call(
        _compress_kernel,
        grid=(b,),
        in_specs=[pl.BlockSpec((None, nch, xw), lambda bi: (bi, 0, 0))]
        + [pl.BlockSpec(c.shape, lambda bi: (0, 0)) for c in consts],
        out_specs=pl.BlockSpec((None, nch, LANES), lambda bi: (bi, 0, 0)),
        out_shape=jax.ShapeDtypeStruct((b, nch, LANES), CDT),
        compiler_params=_cparams(("parallel",), 12),
        name="nsa_compress",
    )(x, *consts)


CMP_ROWS = 8


def _cmp_select_kernel(q_ref, kc_ref, vc_ref, ov_ref, o_ref, sel_ref, s_ref, p_ref, hi_ref, lo_ref,
                       *, tq, n_sel, tile0):
    i = pl.program_id(1) + tile0
    ncp = kc_ref.shape[0]
    ns = ov_ref.shape[1]
    lo = _half_masks()
    t_col = i * tq + lax.broadcasted_iota(jnp.int32, (tq, 1), 0)
    blk = lax.broadcasted_iota(jnp.int32, (tq, ns), 1)
    blk_t = lax.broadcasted_iota(jnp.int32, (ns, tq), 0)
    cur = t_col // SEL_BLOCK
    causal = blk <= cur
    forced = (blk == 0) | (blk == cur) | (blk == cur - 1)
    rb = CMP_ROWS
    c_end = lax.broadcasted_iota(jnp.int32, (rb, ncp), 1) * CMP_STRIDE + (CMP_BLOCK - 1)
    o_kv, work_t = [], []
    for kv, hm in enumerate((lo, jnp.logical_not(lo))):
        qs = jnp.concatenate(
            [(jnp.where(hm, q_ref[:, p * LANES:(p + 1) * LANES], 0) * QK_SCALE).astype(CDT) for p in range(3)], axis=0)
        s_ref[...] = _nt_dot(qs, kc_ref[...])
        for r0 in range(0, tq, rb):
            t_rows = i * tq + r0 + lax.broadcasted_iota(jnp.int32, (rb, 1), 0)
            vis_bias = jnp.where(c_end <= t_rows, 0.0, NEG_INF)
            has_visible = t_rows >= CMP_BLOCK - 1
            psum = jnp.zeros((rb, ncp), F32)
            for h in range(3):
                rows = slice(h * tq + r0, h * tq + r0 + rb)
                s = s_ref[rows, :] + vis_bias
                e = jnp.exp(s - jnp.max(s, axis=-1, keepdims=True))
                inv = jnp.where(has_visible, 1.0 / jnp.maximum(jnp.sum(e, axis=-1, keepdims=True), 1e-30), 0.0)
                pn = e * inv
                p_ref[rows, :] = pn.astype(CDT)
                psum = psum + pn
            p_hi = psum.astype(CDT)
            hi_ref[r0:r0 + rb, :] = p_hi
            lo_ref[r0:r0 + rb, :] = (psum - p_hi.astype(F32)).astype(CDT)
        o = _dot(p_ref[...], vc_ref[...])
        o_kv.append([o[p * tq:(p + 1) * tq] for p in range(3)])
        imp = _dot(hi_ref[...], ov_ref[...]) + _dot(lo_ref[...], ov_ref[...])
        work_t.append(jnp.where(causal & jnp.logical_not(forced), imp, -BIG).T)

    blk_lanes = blk_t[:, 0:LANES]

    def pick(_, work):
        m = jnp.max(work, axis=0, keepdims=True)
        idx = jnp.min(jnp.where(work == m, blk_lanes, ns), axis=0, keepdims=True)
        return jnp.where(blk_lanes == idx, -2.0 * BIG, work)

    for kv, start in enumerate(work_t):
        done = jnp.concatenate([lax.fori_loop(0, n_sel - 3, pick, start[:, c0:c0 + LANES], unroll=True)
                                for c0 in range(0, tq, LANES)], axis=1)
        taken = jnp.where((done < -BIG) & (start > -BIG), 1.0, 0.0).T
        selb = jnp.where(forced | (taken > 0.5), 0.0, NEG_INF)
        sel_ref[:, kv * ns:(kv + 1) * ns] = selb.astype(sel_ref.dtype)
    for p in range(3):
        o_ref[:, p * LANES:(p + 1) * LANES] = jnp.where(lo, o_kv[0][p], o_kv[1][p]).astype(o_ref.dtype)


CMP_CAUSAL_SPLITS = 4


def _cmp_select(zr, qcol, kc, vc, tq=256):
    b, s, _ = zr.shape
    ncp = kc.shape[1]
    ns = s // SEL_BLOCK
    n_sel = min(N_SEL, ns)
    assert n_sel >= 3, "selection needs room for the three forced blocks"
    tq = min(tq, s)
    c = np.arange(ncp)[:, None] * CMP_STRIDE
    j = np.arange(ns)[None, :] * SEL_BLOCK
    overlap = ((c < j + SEL_BLOCK) & (c + CMP_BLOCK - 1 >= j)).astype(np.float32)
    overlap[ncp - 1:, :] = 0.0
    ov = jnp.asarray(overlap, CDT)
    qw = 3 * LANES
    n_split = CMP_CAUSAL_SPLITS if (s // tq) % CMP_CAUSAL_SPLITS == 0 and ncp % (16 * CMP_CAUSAL_SPLITS) == 0 else 1
    tiles = s // tq // n_split
    outs, sels = [], []
    for part in range(n_split):
        tile0 = part * tiles
        ncp_part = ncp * (part + 1) // n_split
        o_part, sel_part = pl.pallas_call(
            functools.partial(_cmp_select_kernel, tq=tq, n_sel=n_sel, tile0=tile0),
            grid=(b, tiles),
            in_specs=[pl.BlockSpec((None, tq, qw), lambda bi, i, tile0=tile0: (bi, i + tile0, qcol)),
                      pl.BlockSpec((None, ncp_part, LANES), lambda bi, i: (bi, 0, 0)),
                      pl.BlockSpec((None, ncp_part, LANES), lambda bi, i: (bi, 0, 0)),
                      pl.BlockSpec((ncp_part, ns), lambda bi, i: (0, 0))],
            out_specs=[pl.BlockSpec((None, tq, qw), lambda bi, i: (bi, i, 0)),
                       pl.BlockSpec((None, tq, 2 * ns), lambda bi, i: (bi, i, 0))],
            out_shape=[jax.ShapeDtypeStruct((b, tiles * tq, qw), CDT),
                       jax.ShapeDtypeStruct((b, tiles * tq, 2 * ns), CDT)],
            scratch_shapes=[pltpu.VMEM((3 * tq, ncp_part), F32), pltpu.VMEM((3 * tq, ncp_part), CDT),
                            pltpu.VMEM((tq, ncp_part), CDT), pltpu.VMEM((tq, ncp_part), CDT)],
            compiler_params=_cparams(("parallel", "parallel"), 12),
            name=f"nsa_cmp_select_p{part}",
        )(zr, kc, vc, ov)
        outs.append(o_part)
        sels.append(sel_part)
    return jnp.concatenate(outs, axis=1), jnp.concatenate(sels, axis=1)


SLC_KEY_CHUNK = 32


def _slc_kernel(q_ref, k0_ref, k1_ref, vt_ref, selb_ref, o_ref, s_ref, e_ref, m_ref, l_ref, a_ref, acc_ref,
                *, tq, tk):
    i = pl.program_id(1)
    ns = selb_ref.shape[1] // 2
    bpt = tk // SEL_BLOCK
    n_q = 3 * tq
    rc = min(SLC_KEY_CHUNK, tk)
    lo = _half_masks()
    n_tiles = ((i + 1) * tq + tk - 1) // tk
    p_row = lax.broadcasted_iota(jnp.int32, (ns, LANES), 0)
    p_col = lax.broadcasted_iota(jnp.int32, (ns, LANES), 1)
    groups = []
    for kv, (hm, k_ref) in enumerate(((lo, k0_ref), (jnp.logical_not(lo), k1_ref))):
        q3 = [(jnp.where(hm, q_ref[:, p * LANES:(p + 1) * LANES], 0) * QK_SCALE).astype(CDT) for p in range(3)]
        selb = selb_ref[:, kv * ns:(kv + 1) * ns]
        lane0 = HEAD_DIM if kv == 0 else 0
        groups.append((kv, hm, k_ref, q3, selb, lane0))
    m_ref[...] = jnp.full(m_ref.shape, NEG_INF, F32)
    l_ref[...] = jnp.zeros(l_ref.shape, F32)
    acc_ref[...] = jnp.zeros(acc_ref.shape, F32)

    def scores(j, nk):
        k0 = pl.multiple_of(j * tk, tk)
        for kv, hm, k_ref, q3, selb, lane0 in groups:
            place = ((p_col >= lane0) & (p_col < lane0 + bpt) & (p_row == p_col - lane0 + j * bpt)).astype(CDT)
            sb = _dot(selb, place).astype(CDT)
            qp = jnp.concatenate([jnp.where(hm, q, sb) for q in q3], axis=0)
            s_ref[kv, 0:nk, :] = _nt_dot(k_ref[pl.ds(k0, nk), :], qp)

    def softmax_pv(j, diagonal, nk):
        k0 = pl.multiple_of(j * tk, tk)
        for kv in range(2):
            for c0 in range(0, n_q, LANES):
                cols = slice(c0, c0 + LANES)
                t_lane = i * tq + (c0 % tq) + lax.broadcasted_iota(jnp.int32, (1, LANES), 1)

                def chunk(r0):
                    s = s_ref[kv, r0:r0 + rc, cols]
                    if diagonal:
                        kpos = k0 + r0 + lax.broadcasted_iota(jnp.int32, (rc, LANES), 0)
                        s = jnp.where(kpos <= t_lane, s, NEG_INF)
                    return s

                m8 = m_ref[kv, :, cols]
                for r0 in range(0, nk, rc):
                    m8 = jnp.maximum(m8, jnp.max(chunk(r0).reshape(rc // 8, 8, LANES), axis=0))
                m_new = jnp.max(m8, axis=0, keepdims=True)
                alpha = jnp.exp(m_ref[kv, :, cols] - m_new)
                l8 = jnp.zeros((8, LANES), F32)
                for r0 in range(0, nk, rc):
                    e = jnp.exp(chunk(r0) - m_new)
                    l8 = l8 + jnp.sum(e.reshape(rc // 8, 8, LANES), axis=0)
                    e_ref[kv, r0:r0 + rc, cols] = e.astype(CDT)
                l_ref[kv, :, cols] = alpha * l_ref[kv, :, cols] + jnp.sum(l8, axis=0, keepdims=True)
                m_ref[kv, :, cols] = jnp.broadcast_to(m_new, (8, LANES))
                a_ref[kv, :, cols] = alpha
        for kv in range(2):
            acc_ref[kv] = a_ref[kv, 0:1, :] * acc_ref[kv] + _dot(vt_ref[j, :, 0:nk], e_ref[kv, 0:nk, :])

    def tile(j, diagonal, nk=tk):
        scores(j, nk)
        softmax_pv(j, diagonal, nk)

    lax.fori_loop(0, n_tiles - 1, lambda j, c: (tile(j, False), c)[1], 0)
    last = n_tiles - 1
    half = tk // 2
    first_half_only = (i + 1) * tq - last * tk <= half

    @pl.when(first_half_only)
    def _():
        tile(last, True, half)

    @pl.when(jnp.logical_not(first_half_only))
    def _():
        tile(last, True)
    o0 = (acc_ref[0] / l_ref[0, 0:1, :]).T
    o1 = (acc_ref[1] / l_ref[1, 0:1, :]).T
    for p in range(3):
        o_ref[:, p * LANES:(p + 1) * LANES] = jnp.where(
            lo, o0[p * tq:(p + 1) * tq], o1[p * tq:(p + 1) * tq]).astype(o_ref.dtype)


def _slc(zr, qcol, ksc, vt, selb, tq=512):
    b, s, _ = zr.shape
    tq = min(tq, s)
    tk = vt.shape[2]
    assert tq % LANES == 0 and tk // SEL_BLOCK <= HEAD_DIM and s % tk == 0
    qw = 3 * LANES
    ns2 = selb.shape[2]
    pat = jax.nn.one_hot((jnp.arange(s) // SEL_BLOCK) % (tk // SEL_BLOCK), HEAD_DIM, dtype=ksc.dtype)
    pat = jnp.broadcast_to(pat[None], (b, s, HEAD_DIM))
    k0 = jnp.concatenate([ksc[..., :HEAD_DIM], pat], axis=-1)
    k1 = jnp.concatenate([pat, ksc[..., HEAD_DIM:]], axis=-1)
    full = lambda bi, i: (bi, 0, 0)
    return pl.pallas_call(
        functools.partial(_slc_kernel, tq=tq, tk=tk),
        grid=(b, s // tq),
        in_specs=[pl.BlockSpec((None, tq, qw), lambda bi, i: (bi, i, qcol)),
                  _single_buffered((None, s, LANES), full),
                  _single_buffered((None, s, LANES), full),
                  _single_buffered((s // tk, LANES, tk), lambda bi, i: (bi, 0, 0)),
                  pl.BlockSpec((None, tq, ns2), lambda bi, i: (bi, i, 0))],
        out_specs=pl.BlockSpec((None, tq, qw), lambda bi, i: (bi, i, 0)),
        out_shape=jax.ShapeDtypeStruct((b, s, qw), CDT),
        scratch_shapes=[pltpu.VMEM((2, tk, 3 * tq), F32), pltpu.VMEM((2, tk, 3 * tq), CDT),
                        pltpu.VMEM((2, 8, 3 * tq), F32), pltpu.VMEM((2, 8, 3 * tq), F32),
                        pltpu.VMEM((2, 8, 3 * tq), F32), pltpu.VMEM((2, LANES, 3 * tq), F32)],
        compiler_params=_cparams(("parallel", "arbitrary"), 14),
        name="nsa_slc",
    )(zr, k0, k1, vt, selb)


def _merge_kernel(x_ref, oa0, oa1, oa2, la0, la1, la2, ob, ocmp, oslc, owin, od,
                  wn_ref, wg_ref, bg_ref, wb_ref, wo_ref, g_ref, b_ref, o_ref, *scratch, alpha):
    x = x_ref[...]
    xb = x.astype(CDT)
    bw = BRANCH_WIDTH

    def token_rows(src_ref, dst_ref):
        dil = src_ref.shape[1] // bw
        if dil == 1:
            return src_ref[...].astype(F32)
        n_cb = bw // LANES
        for r in range(dil):
            for cb in range(n_cb):
                c0 = r * bw + cb * LANES
                dst_ref[cb, pl.ds(r, src_ref.shape[0], stride=dil), :] = src_ref[:, c0:c0 + LANES].astype(F32)
        return jnp.concatenate([dst_ref[cb] for cb in range(n_cb)], axis=1)

    o0, o1, o2 = (token_rows(s, d) for s, d in zip((oa0, oa1, oa2), scratch[0:3]))
    l0, l1, l2 = (token_rows(s, d) for s, d in zip((la0, la1, la2), scratch[3:6]))
    mx = jnp.maximum(jnp.maximum(l0, l1), l2)
    w0, w1, w2 = jnp.exp(l0 - mx), jnp.exp(l1 - mx), jnp.exp(l2 - mx)
    o_a = (w0 * o0 + w1 * o1 + w2 * o2) / (w0 + w1 + w2)
    gates = jax.nn.sigmoid(_dot(xb, wn_ref[...]))
    o_c = (gates[:, 0:bw] * ocmp[...].astype(F32) + gates[:, bw:2 * bw] * oslc[...].astype(F32)
           + gates[:, 2 * bw:3 * bw] * owin[...].astype(F32))
    branches = (o_a.astype(CDT), ob[...], o_c.astype(CDT), od[...])
    d = x.shape[1]
    merged = jnp.zeros(x.shape, F32)
    for m in range(N_BRANCH):
        gate = jax.nn.sigmoid(_dot(xb, wg_ref[:, m * d:(m + 1) * d]) + bg_ref[:, m * d:(m + 1) * d])
        merged = merged + gate * _dot(branches[m], wb_ref[m])
    r = alpha * x + _dot(merged.astype(CDT), wo_ref[...])
    o_ref[...] = _layer_norm(r, g_ref[...], b_ref[...])


def _merge(x2d, branch_inputs, wn, wg, bg, wb, wo, g, b, alpha, tm=256):
    t, d = x2d.shape
    tm = min(tm, t)
    row = lambda i: (i, 0)
    const2 = lambda i: (0, 0)
    in_specs = [pl.BlockSpec((tm, d), row)]
    in_specs += [pl.BlockSpec((tm * a.shape[0] // t, a.shape[1]), row) for a in branch_inputs]
    in_specs += [pl.BlockSpec(wn.shape, const2), pl.BlockSpec(wg.shape, const2), pl.BlockSpec(bg.shape, const2),
                 pl.BlockSpec(wb.shape, lambda i: (0, 0, 0)), pl.BlockSpec(wo.shape, const2),
                 pl.BlockSpec(g.shape, const2), pl.BlockSpec(b.shape, const2)]
    return pl.pallas_call(
        functools.partial(_merge_kernel, alpha=alpha),
        grid=(t // tm,),
        in_specs=in_specs,
        out_specs=pl.BlockSpec((tm, d), row),
        out_shape=jax.ShapeDtypeStruct((t, d), F32),
        scratch_shapes=[pltpu.VMEM((BRANCH_WIDTH // LANES, tm, LANES), F32) for _ in range(2 * N_DIL)],
        compiler_params=_cparams(("parallel",), 14),
        name="merge_ln",
    )(x2d, *branch_inputs, wn, wg, bg, wb, wo, g, b)


def _ple_ln(x, xb, f, p, plw_ref, pgw_ref, pgb_ref, g_ref, b_ref, alpha):
    ple = jax.nn.sigmoid(_dot(xb, pgw_ref[...]) + pgb_ref[...]) * _dot(p.astype(CDT), plw_ref[...])
    return _layer_norm(alpha * x + f + ple, g_ref[...], b_ref[...])


FFN_CHUNK = 512


def _ffn_kernel(x_ref, p_ref, wg_ref, wu_ref, wd_ref, plw_ref, pgw_ref, pgb_ref, g_ref, b_ref, o_ref, h_ref, *, alpha):
    x = x_ref[...]
    xb = x.astype(CDT)
    dff = wg_ref.shape[1]
    for c0 in range(0, dff, FFN_CHUNK):
        cols = slice(c0, min(c0 + FFN_CHUNK, dff))
        h_ref[:, cols] = (jax.nn.silu(_dot(xb, wg_ref[:, cols])) * _dot(xb, wu_ref[:, cols])).astype(CDT)
    f = _dot(h_ref[...], wd_ref[...])
    o_ref[...] = _ple_ln(x, xb, f, p_ref[...], plw_ref, pgw_ref, pgb_ref, g_ref, b_ref, alpha)


def _ffn(x2d, p2d, wg, wu, wd, plw, pgw, pgb, g, b, alpha, tm=512):
    t, d = x2d.shape
    tm = min(tm, t)
    dff = wg.shape[1]
    wg, wu, wd = wg.astype(CDT), wu.astype(CDT), wd.astype(CDT)
    row = lambda i: (i, 0)
    const = lambda shape: pl.BlockSpec(shape, lambda i: (0, 0), pipeline_mode=pl.Buffered(1))
    return pl.pallas_call(
        functools.partial(_ffn_kernel, alpha=alpha),
        grid=(t // tm,),
        in_specs=[pl.BlockSpec((tm, d), row), pl.BlockSpec((tm, p2d.shape[1]), row),
                  const(wg.shape), const(wu.shape), const(wd.shape),
                  const(plw.shape), const(pgw.shape), const(pgb.shape), const(g.shape), const(b.shape)],
        out_specs=pl.BlockSpec((tm, d), row),
        out_shape=jax.ShapeDtypeStruct((t, d), F32),
        scratch_shapes=[pltpu.VMEM((tm, dff), CDT)],
        compiler_params=_cparams(("parallel",), 12),
        name="ffn_ple_ln",
    )(x2d, p2d, wg, wu, wd, plw, pgw, pgb, g, b)


def _router_kernel(x_ref, wh_ref, wl_ref, b_ref, comb_ref, rank_ref, rank_t_ref, cnt_ref):
    x = x_ref[...]
    xh = x.astype(CDT)
    xl = (x - xh.astype(F32)).astype(CDT)
    logits = _dot(xh, wh_ref[...]) + _dot(xh, wl_ref[...]) + _dot(xl, wh_ref[...]) + b_ref[...]
    lane = lax.broadcasted_iota(jnp.int32, logits.shape, 1)
    v1 = jnp.max(logits, axis=-1, keepdims=True)
    i1 = jnp.min(jnp.where(logits == v1, lane, LANES), axis=-1, keepdims=True)
    rest = jnp.where(lane == i1, -jnp.inf, logits)
    v2 = jnp.max(rest, axis=-1, keepdims=True)
    i2 = jnp.min(jnp.where(rest == v2, lane, LANES), axis=-1, keepdims=True)
    e2 = jnp.exp(v2 - v1)
    comb_ref[...] = jnp.where(lane == i1, 1.0 / (1.0 + e2), 0.0) + jnp.where(lane == i2, e2 / (1.0 + e2), 0.0)
    routed = (lane == i1) | (lane == i2)
    mask = routed.astype(CDT)
    tm = x.shape[0]
    before = (lax.broadcasted_iota(jnp.int32, (tm, tm), 1) < lax.broadcasted_iota(jnp.int32, (tm, tm), 0)).astype(CDT)
    rank = jnp.where(routed, _dot(before, mask), -1.0)
    rank_ref[...] = rank
    rank_t_ref[...] = rank.T[0:rank_t_ref.shape[0], :]
    cnt_ref[...] = jnp.sum(routed.astype(F32), axis=0, keepdims=True).astype(jnp.int32)


def _router(x2d, w_router, b_router, tm):
    t, d = x2d.shape
    ne = w_router.shape[1]
    wp = jnp.zeros((d, LANES), F32).at[:, :ne].set(w_router)
    wh = wp.astype(CDT)
    wl = (wp - wh.astype(F32)).astype(CDT)
    bp = jnp.full((1, LANES), -BIG, F32).at[0, :ne].set(b_router)
    nt = t // tm
    row = lambda i: (i, 0)
    return pl.pallas_call(
        _router_kernel,
        grid=(nt,),
        in_specs=[pl.BlockSpec((tm, d), row), pl.BlockSpec(wh.shape, lambda i: (0, 0)),
                  pl.BlockSpec(wl.shape, lambda i: (0, 0)), pl.BlockSpec(bp.shape, lambda i: (0, 0))],
        out_specs=[pl.BlockSpec((tm, LANES), row), pl.BlockSpec((tm, LANES), row),
                   pl.BlockSpec((None, 8, tm), lambda i: (i, 0, 0)),
                   pl.BlockSpec((None, 1, LANES), lambda i: (i, 0, 0))],
        out_shape=[jax.ShapeDtypeStruct((t, LANES), F32), jax.ShapeDtypeStruct((t, LANES), F32),
                   jax.ShapeDtypeStruct((nt, 8, tm), F32), jax.ShapeDtypeStruct((nt, 1, LANES), jnp.int32)],
        compiler_params=_cparams(("parallel",), 10),
        name="moe_router",
    )(x2d, wh, wl, bp)


def _moe_kernel(cnt_ref, x_ref, comb_ref, rank_ref, rank_t_ref, p_ref, wg_ref, wu_ref, wd_ref, plw_ref, pgw_ref,
                pgb_ref, g_ref, b_ref, o_ref, xe_ref, ye_ref, *, alpha, rs, seg):
    i = pl.program_id(0)
    e = pl.program_id(1)
    c = pl.program_id(2)
    n_seg = x_ref.shape[0] // seg
    last_chunk = c == pl.num_programs(2) - 1

    first = slice(0, rs)
    toks = [slice(sg * seg, (sg + 1) * seg) for sg in range(n_seg)]
    n_groups = [(cnt_ref[(i * n_seg + sg) * LANES + e] + rs - 1) // rs for sg in range(n_seg)]

    def later(sc):
        return pl.ds(pl.multiple_of(sc * rs, 8), rs), (sc * rs).astype(F32)

    @pl.when((e == 0) & (c == 0))
    def _():
        o_ref[...] = jnp.zeros_like(o_ref)

    @pl.when(c == 0)
    def _():
        row_id = lax.broadcasted_iota(jnp.int32, (rs, seg), 0).astype(F32)
        xbs = [x_ref[tok, :].astype(CDT) for tok in toks]
        rank_rows = [rank_t_ref[sg, pl.ds(e, 1), :] for sg in range(n_seg)]

        def gather(sg, rws, base):
            onehot = (rank_rows[sg] - base == row_id).astype(CDT)
            xe_ref[sg, rws, :] = _dot(onehot, xbs[sg]).astype(CDT)
            ye_ref[sg, rws, :] = jnp.zeros((rs, ye_ref.shape[2]), F32)

        for sg in range(n_seg):
            gather(sg, first, 0.0)
        for sg in range(n_seg):
            lax.fori_loop(1, n_groups[sg], lambda sc, _, sg=sg: (gather(sg, *later(sc)), 0)[1], 0)

    def expert(sg, rws):
        xs = xe_ref[sg, rws, :]
        h = jax.nn.silu(_dot(xs, wg_ref[0])) * _dot(xs, wu_ref[0])
        ye_ref[sg, rws, :] += _dot(h.astype(CDT), wd_ref[0])

    for sg in range(n_seg):
        expert(sg, first)
    for sg in range(n_seg):
        lax.fori_loop(1, n_groups[sg], lambda sc, _, sg=sg: (expert(sg, later(sc)[0]), 0)[1], 0)

    @pl.when(last_chunk)
    def _():
        lane = lax.broadcasted_iota(jnp.int32, (seg, LANES), 1)
        mine = lane == e
        col_id = lax.broadcasted_iota(jnp.int32, (seg, rs), 1).astype(F32)
        cws = [jnp.sum(jnp.where(mine, comb_ref[tok, :], 0.0), axis=-1, keepdims=True) for tok in toks]
        rank_cols = [jnp.sum(jnp.where(mine, rank_ref[tok, :], 0.0), axis=-1, keepdims=True) for tok in toks]

        def scatter(sg, rws, base):
            onehot = (rank_cols[sg] - base == col_id).astype(CDT)
            o_ref[toks[sg], :] += cws[sg] * _dot(onehot, ye_ref[sg, rws, :].astype(CDT))

        for sg in range(n_seg):
            scatter(sg, first, 0.0)
        for sg in range(n_seg):
            lax.fori_loop(1, n_groups[sg], lambda sc, _, sg=sg: (scatter(sg, *later(sc)), 0)[1], 0)

    @pl.when((e == pl.num_programs(1) - 1) & last_chunk)
    def _():
        for sg in range(n_seg):
            tok = slice(sg * seg, (sg + 1) * seg)
            x = x_ref[tok, :]
            o_ref[tok, :] = _ple_ln(x, x.astype(CDT), o_ref[tok, :], p_ref[tok, :], plw_ref, pgw_ref, pgb_ref,
                                    g_ref, b_ref, alpha)


MOE_CHUNK = 512
MOE_SEGMENT = 1024
MOE_SEGMENTS_PER_TILE = 2
MOE_ROW_GROUP = 288


def _single_buffered(shape, index_map):
    return pl.BlockSpec(shape, index_map, pipeline_mode=pl.Buffered(1))


def _moe(x2d, routing, p2d, wg, wu, wd, plw, pgw, pgb, g, b, alpha, seg):
    comb, rank, rank_t, cnt = routing
    t, d = x2d.shape
    ne, _, dff = wg.shape
    ck = min(MOE_CHUNK, dff)
    rs = min(MOE_ROW_GROUP, seg)
    n_seg = min(MOE_SEGMENTS_PER_TILE, t // seg)
    tm = n_seg * seg
    max_rows = -(-seg // rs) * rs
    wg, wu, wd = wg.astype(CDT), wu.astype(CDT), wd.astype(CDT)
    row = lambda i, e, c, cnt: (i, 0)
    c2 = lambda i, e, c, cnt: (0, 0)
    grid_spec = pltpu.PrefetchScalarGridSpec(
        num_scalar_prefetch=1,
        grid=(t // tm, ne, dff // ck),
        in_specs=[_single_buffered((tm, d), row), _single_buffered((tm, LANES), row),
                  _single_buffered((tm, LANES), row),
                  _single_buffered((n_seg, 8, seg), lambda i, e, c, cnt: (i, 0, 0)),
                  _single_buffered((tm, p2d.shape[1]), row),
                  pl.BlockSpec((1, d, ck), lambda i, e, c, cnt: (e, 0, c)),
                  pl.BlockSpec((1, d, ck), lambda i, e, c, cnt: (e, 0, c)),
                  pl.BlockSpec((1, ck, d), lambda i, e, c, cnt: (e, c, 0)),
                  _single_buffered(plw.shape, c2), _single_buffered(pgw.shape, c2), _single_buffered(pgb.shape, c2),
                  _single_buffered(g.shape, c2), _single_buffered(b.shape, c2)],
        out_specs=pl.BlockSpec((tm, d), row),
        scratch_shapes=[pltpu.VMEM((n_seg, max_rows, d), CDT), pltpu.VMEM((n_seg, max_rows, d), F32)],
    )
    return pl.pallas_call(
        functools.partial(_moe_kernel, alpha=alpha, rs=rs, seg=seg),
        grid_spec=grid_spec,
        out_shape=jax.ShapeDtypeStruct((t, d), F32),
        compiler_params=_cparams(("parallel", "arbitrary", "arbitrary"), 15),
        name="moe_ple_ln",
    )(cnt.reshape(-1), x2d, comb, rank, rank_t, p2d, wg, wu, wd, plw, pgw, pgb, g, b)


def _prep_in_weights(w_in):
    o = COL_OFF
    bw = BRANCH_WIDTH
    cols = lambda n: w_in[:, o[n]:o[n + 1]]
    qa, ka, va = cols(0), cols(1), cols(2)
    w_dil = jnp.concatenate(
        [t[:, g * bw:(g + 1) * bw] for g in range(N_DIL) for t in (qa, ka, va)], axis=1).astype(CDT)
    w_conv = jnp.concatenate([cols(3), cols(4), cols(5)], axis=1).astype(CDT)
    gn = cols(13)
    w_gate = jnp.concatenate([gn[:, br * NSA_Q_HEADS + GQA_COL_HEAD] for br in range(3)], axis=1).astype(CDT)
    w_rest = jnp.concatenate([cols(6)[:, GQA_COL_PERM], cols(14)[:, GQA_COL_PERM], cols(11),
                              cols(12), cols(15), cols(16), cols(9), cols(7), cols(8)], axis=1).astype(CDT)
    wt_vsc = cols(10).T.astype(CDT)
    return w_dil, w_conv, w_gate, w_rest, wt_vsc


ZR_Q_NSA, ZR_Q_SWA = 0, 1
ZR_KWC, ZR_VWC, ZR_KD, ZR_VD = 6, 7, 8, 9
ZR_WIDTH = 2 * BRANCH_WIDTH + 4 * LANES
SLC_KEY_TILE = 1024


def _token_mixers(x, w_in, conv_w, cmp_pos, cmp_w1, cmp_b1, cmp_w2, cmp_b2, sinks):
    b, s, d = x.shape
    x2d = x.reshape(b * s, d)
    w_dil, w_conv, w_gate, w_rest, wt_vsc = _prep_in_weights(w_in)
    gw = 3 * BRANCH_WIDTH

    z_dil = _linear(x2d, w_dil, [(g * gw, (g + 1) * gw) for g in range(N_DIL)], gw, "in_proj_dil",
                    dils=[dil for _, dil in DIL_PATTERNS])
    zr, ksc, kcc, vcc = _linear(x2d, w_rest, [(0, ZR_WIDTH)] + [(ZR_WIDTH + n * LANES, ZR_WIDTH + (n + 1) * LANES)
                                                           for n in range(3)], 256, "in_proj_rest")
    zr = zr.reshape(b, s, ZR_WIDTH)
    o_b = _conv(x, w_conv, conv_w)

    dil_o, dil_lse = [], []
    for g, (window, dil) in enumerate(DIL_PATTERNS):
        view = z_dil[g].reshape(b, s // dil, dil * gw)
        og, lg = _banded(view, view, view, nrep=dil,
                         qcol=lambda r: 3 * r, kcol=lambda r: 3 * r + 1, vcol=lambda r: 3 * r + 2,
                         kw=3 * LANES, window=window // dil, want_lse=True)
        dil_o.append(og.reshape(b * s // dil, dil * BRANCH_WIDTH))
        dil_lse.append(lg.reshape(b * s // dil, dil * BRANCH_WIDTH))

    kc = _compress(kcc.reshape(b, s, LANES), cmp_pos[0], cmp_w1[0], cmp_b1[0], cmp_w2[0], cmp_b2[0])
    vc = _compress(vcc.reshape(b, s, LANES), cmp_pos[1], cmp_w1[1], cmp_b1[1], cmp_w2[1], cmp_b2[1])
    o_cmp, selb = _cmp_select(zr, ZR_Q_NSA, kc, vc)
    vsc_t = _linear_t(x2d, wt_vsc, min(SLC_KEY_TILE, s), "in_proj_vsc_t")
    o_slc = _slc(zr, ZR_Q_NSA, ksc.reshape(b, s, LANES), vsc_t, selb)
    (o_win,) = _banded(zr, zr, zr, nrep=1, qcol=lambda r: ZR_Q_NSA, kcol=lambda r: ZR_KWC, vcol=lambda r: ZR_VWC,
                       kw=LANES, window=NSA_WINDOW - 1, want_lse=False, tq=512)

    sink_row = sinks.astype(F32)[GQA_COL_HEAD].reshape(1, BRANCH_WIDTH)
    (o_d,) = _banded(zr, zr, zr, nrep=1, qcol=lambda r: ZR_Q_SWA, kcol=lambda r: ZR_KD, vcol=lambda r: ZR_VD,
                     kw=LANES, window=SWA_WINDOW - 1, want_lse=False, sink_row=sink_row)

    t = b * s
    flat = lambda a: a.reshape(t, a.shape[-1])
    return [dil_o[0], dil_o[1], dil_o[2], dil_lse[0], dil_lse[1], dil_lse[2], flat(o_b), flat(o_cmp), flat(o_slc),
            flat(o_win), flat(o_d)], w_gate


def kernel(x, p, w_in, conv_w, cmp_pos, cmp_w1, cmp_b1, cmp_w2, cmp_b2, sinks, w_branch, w_merge_gate, b_merge_gate, w_out, ln_mix_g, ln_mix_b, ffn_w_gate, ffn_w_up, ffn_w_down, w_router, b_router, moe_w_gate, moe_w_up, moe_w_down, ple_w, ple_gate_w, ple_gate_b, ln_ffn_g, ln_ffn_b):
    depth, b, s, _ = p.shape
    d = x.shape[-1]
    t = b * s
    alpha = (2 * depth) ** 0.25
    row = lambda v: v.reshape(1, -1).astype(F32)
    for i in range(depth):
        branch_inputs, w_nsa_gate = _token_mixers(x, w_in[i], conv_w[i], cmp_pos[i], cmp_w1[i], cmp_b1[i], cmp_w2[i],
                                                  cmp_b2[i], sinks[i])
        wg = jnp.concatenate([w_merge_gate[i, m] for m in range(N_BRANCH)], axis=1).astype(CDT)
        bg = b_merge_gate[i].reshape(1, N_BRANCH * d).astype(F32)
        wb = jnp.stack([w_branch[i, 0], w_branch[i, 1], w_branch[i, 2][GQA_COL_PERM],
                        w_branch[i, 3][GQA_COL_PERM]]).astype(CDT)
        x1 = _merge(x.reshape(t, d), branch_inputs, w_nsa_gate, wg, bg, wb, w_out[i].astype(CDT),
                    row(ln_mix_g[i]), row(ln_mix_b[i]), alpha)
        p2d = p[i].reshape(t, -1)
        ple_args = (ple_w[i].astype(CDT), ple_gate_w[i].astype(CDT), row(ple_gate_b[i]),
                    row(ln_ffn_g[i]), row(ln_ffn_b[i]))
        j = i // 2
        if i % 2 == 0:
            x2 = _ffn(x1, p2d, ffn_w_gate[j], ffn_w_up[j], ffn_w_down[j], *ple_args, alpha)
        else:
            seg = min(MOE_SEGMENT, t)
            routing = _router(x1, w_router[j], b_router[j], seg)
            x2 = _moe(x1, routing, p2d, moe_w_gate[j], moe_w_up[j], moe_w_down[j], *ple_args, alpha, seg)
        x = x2.reshape(b, s, d)
    return x
```

```python
import functools

import numpy as np
import jax
import jax.numpy as jnp
from jax import lax
from jax.experimental import pallas as pl
from jax.experimental.pallas import tpu as pltpu

HEAD_DIM = 64
DIL_PATTERNS = ((128, 1), (512, 4), (2048, 16))
N_DIL = 3
DIL_HEADS = 6
CONV_WIDTH = 384
NSA_Q_HEADS = 6
NSA_KV_HEADS = 2
CMP_BLOCK = 32
CMP_STRIDE = 16
CMP_HIDDEN = 128
SEL_BLOCK = 64
N_SEL = 16
NSA_WINDOW = 512
SWA_Q_HEADS = 6
SWA_WINDOW = 128
BRANCH_WIDTH = 384
N_BRANCH = 4
LN_EPS = 1e-5
NEG_INF = -1e30
DIL_WIDTH = N_DIL * DIL_HEADS * HEAD_DIM
COLUMN_SIZES = (DIL_WIDTH, DIL_WIDTH, DIL_WIDTH, CONV_WIDTH, CONV_WIDTH, CONV_WIDTH,
                NSA_Q_HEADS * HEAD_DIM, 128, 128, 128, 128, 128, 128, 3 * NSA_Q_HEADS,
                SWA_Q_HEADS * HEAD_DIM, 128, 128)
COL_OFF = np.concatenate([[0], np.cumsum(COLUMN_SIZES)]).tolist()

LANES = 128
V7X_VMEM_BYTES = 64 * 1024 * 1024

CDT = jnp.bfloat16
F32 = jnp.float32
QK_SCALE = HEAD_DIM ** -0.5
SUB_Q = 128
BAND_ROWS = 32
BIG = 1e30

_GQA_HEAD_ORDER = (0, 3, 1, 4, 2, 5)
GQA_COL_PERM = np.concatenate([np.arange(h * HEAD_DIM, (h + 1) * HEAD_DIM) for h in _GQA_HEAD_ORDER])
GQA_COL_HEAD = GQA_COL_PERM // HEAD_DIM


def _cparams(sem, vmem_sixteenths):
    return pltpu.CompilerParams(dimension_semantics=sem, vmem_limit_bytes=V7X_VMEM_BYTES * vmem_sixteenths // 16)


def _nt_dot(a, b):
    return lax.dot_general(a, b, (((1,), (1,)), ((), ())), preferred_element_type=F32)


def _dot(a, b):
    return jnp.dot(a, b, preferred_element_type=F32)


def _layer_norm(r, g, b):
    mu = jnp.mean(r, axis=-1, keepdims=True)
    d = r - mu
    var = jnp.mean(d * d, axis=-1, keepdims=True)
    return d * lax.rsqrt(var + LN_EPS) * g + b


def _half_masks():
    lane = lax.broadcasted_iota(jnp.int32, (1, LANES), 1)
    return lane < HEAD_DIM


def _linear_kernel(x_ref, w_ref, *refs, splits, n_chunk, dils):
    o_refs, z_ref = refs[:len(splits)], refs[len(splits)]
    xb = x_ref[...].astype(CDT)
    tm = xb.shape[0]
    for o_ref, (c0, c1), dil in zip(o_refs, splits, dils):
        width = c1 - c0
        if dil == 1:
            for a in range(c0, c1, n_chunk):
                b = min(a + n_chunk, c1)
                o_ref[:, a - c0:b - c0] = _dot(xb, w_ref[:, a:b]).astype(o_ref.dtype)
        else:
            z = _dot(xb, w_ref[:, c0:c1])
            for cb in range(width // LANES):
                z_ref[cb] = z[:, cb * LANES:(cb + 1) * LANES]
            for r in range(dil):
                for cb in range(width // LANES):
                    o_ref[:, r * width + cb * LANES:r * width + (cb + 1) * LANES] = (
                        z_ref[cb, pl.ds(r, tm // dil, stride=dil), :].astype(o_ref.dtype))


def _linear(x2d, w, splits, n_chunk, name, dils=None, tm=512):
    t, k = x2d.shape
    tm = min(tm, t)
    n = w.shape[1]
    dils = tuple(dils) if dils is not None else (1,) * len(splits)
    widths = [c1 - c0 for c0, c1 in splits]
    assert all(tm % (16 * dl) == 0 for dl in dils)
    return pl.pallas_call(
        functools.partial(_linear_kernel, splits=tuple(splits), n_chunk=n_chunk, dils=dils),
        grid=(t // tm,),
        in_specs=[pl.BlockSpec((tm, k), lambda i: (i, 0)),
                  pl.BlockSpec((k, n), lambda i: (0, 0))],
        out_specs=[pl.BlockSpec((tm // dl, dl * wd), lambda i: (i, 0)) for wd, dl in zip(widths, dils)],
        out_shape=[jax.ShapeDtypeStruct((t // dl, dl * wd), CDT) for wd, dl in zip(widths, dils)],
        scratch_shapes=[pltpu.VMEM((max(widths) // LANES, tm, LANES), F32)],
        compiler_params=_cparams(("parallel",), 12),
        name=name,
    )(x2d, w)


def _linear_t_kernel(x_ref, wt_ref, o_ref):
    o_ref[...] = _nt_dot(wt_ref[...], x_ref[...].astype(CDT)).astype(o_ref.dtype)


def _linear_t(x2d, wt, tm, name):
    t, k = x2d.shape
    n = wt.shape[0]
    return pl.pallas_call(
        _linear_t_kernel,
        grid=(t // tm,),
        in_specs=[pl.BlockSpec((tm, k), lambda i: (i, 0)), pl.BlockSpec((n, k), lambda i: (0, 0))],
        out_specs=pl.BlockSpec((None, n, tm), lambda i: (i, 0, 0)),
        out_shape=jax.ShapeDtypeStruct((t // tm, n, tm), CDT),
        compiler_params=_cparams(("parallel",), 8),
        name=name,
    )(x2d, wt)


def _conv_kernel(x_ref, xh_ref, wc_ref, cw_ref, ob_ref, *, tm):
    i = pl.program_id(1)
    w = CONV_WIDTH
    xb = x_ref[...].astype(CDT)
    z = _dot(xb, wc_ref[...])
    u = z[:, w:2 * w] * z[:, 2 * w:3 * w]
    zh = _dot(xh_ref[...].astype(CDT), wc_ref[:, w:3 * w])
    uh = zh[:, :w] * zh[:, w:]
    uh = jnp.where(i == 0, 0.0, uh)
    row = lax.broadcasted_iota(jnp.int32, (tm, w), 0)
    u1 = jnp.where(row == 0, uh[7:8, :], pltpu.roll(u, 1, 0))
    u2 = jnp.where(row == 0, uh[6:7, :], jnp.where(row == 1, uh[7:8, :], pltpu.roll(u, 2, 0)))
    y = cw_ref[0:1, :] * u2 + cw_ref[1:2, :] * u1 + cw_ref[2:3, :] * u
    ob_ref[...] = (z[:, :w] * y).astype(ob_ref.dtype)


def _conv(x, wc, conv_w, tm=512):
    b, s, d = x.shape
    tm = min(tm, s)
    hb = tm // 8
    return pl.pallas_call(
        functools.partial(_conv_kernel, tm=tm),
        grid=(b, s // tm),
        in_specs=[pl.BlockSpec((None, tm, d), lambda bi, i: (bi, i, 0)),
                  pl.BlockSpec((None, 8, d), lambda bi, i: (bi, jnp.maximum(i * hb - 1, 0), 0)),
                  pl.BlockSpec(wc.shape, lambda bi, i: (0, 0)),
                  pl.BlockSpec(conv_w.shape, lambda bi, i: (0, 0))],
        out_specs=pl.BlockSpec((None, tm, CONV_WIDTH), lambda bi, i: (bi, i, 0)),
        out_shape=jax.ShapeDtypeStruct((b, s, CONV_WIDTH), CDT),
        compiler_params=_cparams(("parallel", "parallel"), 12),
        name="short_conv",
    )(x, x, wc, conv_w)


def _banded_kernel(*refs, window, pr, tq, kw, want_lse, has_sink):
    q_ref, kp_ref, kc_ref, vp_ref, vc_ref = refs[:5]
    n = 5
    sink_ref = None
    if has_sink:
        sink_ref = refs[n]
        n += 1
    o_ref = refs[n]
    n += 1
    lse_ref = None
    if want_lse:
        lse_ref = refs[n]
        n += 1
    kbuf, vbuf, s_ref, e_ref, m_ref, l_ref = refs[n:n + 6]

    i = pl.program_id(2)
    kbuf[0:pr, :] = kp_ref[...]
    kbuf[pr:pr + tq, :] = kc_ref[...]
    vbuf[0:pr, :] = vp_ref[...]
    vbuf[pr:pr + tq, :] = vc_ref[...]

    span = SUB_Q + pr
    qi = lax.broadcasted_iota(jnp.int32, (SUB_Q, span), 0)
    kj = lax.broadcasted_iota(jnp.int32, (SUB_Q, span), 1)
    dist = pr + qi - kj
    band = (dist >= 0) & (dist <= window)
    lo = _half_masks()
    halves = (lo, jnp.logical_not(lo))
    groups = ((0,), (1,), (2,)) if kw == 3 * LANES else ((0, 1, 2),)
    rb = BAND_ROWS
    for sb in range(tq // SUB_Q):
        r0 = sb * SUB_Q
        bias = jnp.where(band & (i * tq + r0 - pr + kj >= 0), 0.0, NEG_INF)
        for grp in groups:
            kc0 = grp[0] * LANES if kw == 3 * LANES else 0
            qs = jnp.concatenate(
                [(jnp.where(hm, q_ref[r0:r0 + SUB_Q, p * LANES:(p + 1) * LANES], 0) * QK_SCALE).astype(CDT)
                 for p in grp for hm in halves], axis=0)
            g0 = 2 * grp[0] * SUB_Q
            s_ref[g0:g0 + qs.shape[0], :] = _nt_dot(qs, kbuf[r0:r0 + span, kc0:kc0 + LANES])
        for c0 in range(0, 6 * SUB_Q, rb):
            rows = slice(c0, c0 + rb)
            s = s_ref[rows, :] + bias[c0 % SUB_Q:c0 % SUB_Q + rb, :]
            m = jnp.max(s, axis=-1, keepdims=True)
            e = jnp.exp(s - m)
            e_ref[rows, :] = e.astype(CDT)
            m_ref[rows, :] = jnp.broadcast_to(m, (rb, LANES))
            l_ref[rows, :] = jnp.broadcast_to(jnp.sum(e, axis=-1, keepdims=True), (rb, LANES))
        for grp in groups:
            kc0 = grp[0] * LANES if kw == 3 * LANES else 0
            g0 = 2 * grp[0] * SUB_Q
            g1 = g0 + 2 * len(grp) * SUB_Q
            l = l_ref[g0:g1, :]
            o = _dot(e_ref[g0:g1, :], vbuf[r0:r0 + span, kc0:kc0 + LANES]) / l
            lse = m_ref[g0:g1, :] + jnp.log(l)
            for n_p, p in enumerate(grp):
                a = 2 * n_p * SUB_Q
                o_pair = jnp.where(lo, o[a:a + SUB_Q], o[a + SUB_Q:a + 2 * SUB_Q])
                lse_pair = jnp.where(lo, lse[a:a + SUB_Q], lse[a + SUB_Q:a + 2 * SUB_Q])
                if has_sink:
                    o_pair = o_pair * jax.nn.sigmoid(lse_pair - sink_ref[:, p * LANES:(p + 1) * LANES])
                o_ref[r0:r0 + SUB_Q, p * LANES:(p + 1) * LANES] = o_pair.astype(o_ref.dtype)
                if want_lse:
                    lse_ref[r0:r0 + SUB_Q, p * LANES:(p + 1) * LANES] = lse_pair


def _banded(qa, ka, va, *, nrep, qcol, kcol, vcol, kw, window, want_lse, sink_row=None, tq=1024):
    b, l, _ = qa.shape
    pr = -(-window // SUB_Q) * SUB_Q
    tq = min(max(tq, pr), l)
    assert tq % pr == 0 and l % tq == 0, (tq, pr, l)
    ratio = tq // pr
    qw = 3 * LANES
    in_specs = [
        pl.BlockSpec((None, tq, qw), lambda bi, r, i: (bi, i, qcol(r))),
        pl.BlockSpec((None, pr, kw), lambda bi, r, i: (bi, jnp.maximum(i * ratio - 1, 0), kcol(r))),
        pl.BlockSpec((None, tq, kw), lambda bi, r, i: (bi, i, kcol(r))),
        pl.BlockSpec((None, pr, kw), lambda bi, r, i: (bi, jnp.maximum(i * ratio - 1, 0), vcol(r))),
        pl.BlockSpec((None, tq, kw), lambda bi, r, i: (bi, i, vcol(r))),
    ]
    args = [qa, ka, ka, va, va]
    if sink_row is not None:
        in_specs.append(pl.BlockSpec(sink_row.shape, lambda bi, r, i: (0, 0)))
        args.append(sink_row)
    out_specs = [pl.BlockSpec((None, tq, qw), lambda bi, r, i: (bi, i, r))]
    out_shape = [jax.ShapeDtypeStruct((b, l, nrep * qw), CDT)]
    if want_lse:
        out_specs.append(pl.BlockSpec((None, tq, qw), lambda bi, r, i: (bi, i, r)))
        out_shape.append(jax.ShapeDtypeStruct((b, l, nrep * qw), F32))
    res = pl.pallas_call(
        functools.partial(_banded_kernel, window=window, pr=pr, tq=tq, kw=kw, want_lse=want_lse,
                          has_sink=sink_row is not None),
        grid=(b, nrep, l // tq),
        in_specs=in_specs,
        out_specs=out_specs,
        out_shape=out_shape,
        scratch_shapes=[pltpu.VMEM((pr + tq, kw), ka.dtype), pltpu.VMEM((pr + tq, kw), va.dtype),
                        pltpu.VMEM((6 * SUB_Q, SUB_Q + pr), F32), pltpu.VMEM((6 * SUB_Q, SUB_Q + pr), CDT),
                        pltpu.VMEM((6 * SUB_Q, LANES), F32), pltpu.VMEM((6 * SUB_Q, LANES), F32)],
        compiler_params=_cparams(("parallel", "parallel", "parallel"), 8),
        name=f"banded_w{window}_k{kw}_r{nrep}",
    )(*args)
    return res


def _gelu_tanh(x):
    return 0.5 * x * (1.0 + jnp.tanh(0.7978845608028654 * (x + 0.044715 * (x * x * x))))


def _compress_kernel(x_ref, pa_ref, pb_ref, w1a_ref, w1b_ref, b1_ref, w2_ref, b2_ref, o_ref):
    x = x_ref[...].astype(F32)
    n = x.shape[0]
    a = _dot((x + pa_ref[...]).astype(CDT), w1a_ref[...])
    bm = _dot((x + pb_ref[...]).astype(CDT), w1b_ref[...])
    h = a + pltpu.roll(bm, n - 1, 0) + b1_ref[...]
    o_ref[...] = (_dot(_gelu_tanh(h).astype(CDT), w2_ref[...]) + b2_ref[...]).astype(o_ref.dtype)


def _compress(t, pos, w1, b1, w2, b2):
    b, s, _ = t.shape
    nch = s // CMP_STRIDE
    xw = CMP_STRIDE * LANES
    x = t.reshape(b, nch, xw)
    eye = jnp.eye(NSA_KV_HEADS, dtype=F32)
    w1r = w1.reshape(CMP_BLOCK, HEAD_DIM, CMP_HIDDEN)

    def expand_w1(part):
        return jnp.einsum('tdj,kl->tkdlj', part, eye).reshape(xw, NSA_KV_HEADS * CMP_HIDDEN).astype(CDT)

    def expand_pos(part):
        return jnp.broadcast_to(part[:, None, :], (CMP_STRIDE, NSA_KV_HEADS, HEAD_DIM)).reshape(1, xw)

    w1a, w1b = expand_w1(w1r[:CMP_STRIDE]), expand_w1(w1r[CMP_STRIDE:])
    pa, pb = expand_pos(pos[:CMP_STRIDE]), expand_pos(pos[CMP_STRIDE:])
    b1e = jnp.tile(b1, NSA_KV_HEADS).reshape(1, -1)
    w2e = jnp.einsum('jd,kl->kjld', w2, eye).reshape(NSA_KV_HEADS * CMP_HIDDEN, LANES).astype(CDT)
    b2e = jnp.tile(b2, NSA_KV_HEADS).reshape(1, -1)
    consts = [pa, pb, w1a, w1b, b1e, w2e, b2e]
    return pl.pallas_call(
        _compress_kernel,
        grid=(b,),
        in_specs=[pl.BlockSpec((None, nch, xw), lambda bi: (bi, 0, 0))]
        + [pl.BlockSpec(c.shape, lambda bi: (0, 0)) for c in consts],
        out_specs=pl.BlockSpec((None, nch, LANES), lambda bi: (bi, 0, 0)),
        out_shape=jax.ShapeDtypeStruct((b, nch, LANES), CDT),
        compiler_params=_cparams(("parallel",), 12),
        name="nsa_compress",
    )(x, *consts)


CMP_ROWS = 16


def _cmp_select_kernel(q_ref, kc_ref, vc_ref, ov_ref, o_ref, sel_ref, s_ref, p_ref, hi_ref, lo_ref,
                       *, tq, n_sel, tile0):
    i = pl.program_id(1) + tile0
    ncp = kc_ref.shape[0]
    ns = ov_ref.shape[1]
    lo = _half_masks()
    t_col = i * tq + lax.broadcasted_iota(jnp.int32, (tq, 1), 0)
    blk = lax.broadcasted_iota(jnp.int32, (tq, ns), 1)
    blk_t = lax.broadcasted_iota(jnp.int32, (ns, tq), 0)
    cur = t_col // SEL_BLOCK
    causal = blk <= cur
    forced = (blk == 0) | (blk == cur) | (blk == cur - 1)
    rb = CMP_ROWS
    c_end = lax.broadcasted_iota(jnp.int32, (rb, ncp), 1) * CMP_STRIDE + (CMP_BLOCK - 1)
    o_kv, work_t = [], []
    for kv, hm in enumerate((lo, jnp.logical_not(lo))):
        qs = jnp.concatenate(
            [(jnp.where(hm, q_ref[:, p * LANES:(p + 1) * LANES], 0) * QK_SCALE).astype(CDT) for p in range(3)], axis=0)
        s_ref[...] = _nt_dot(qs, kc_ref[...])
        for r0 in range(0, tq, rb):
            t_rows = i * tq + r0 + lax.broadcasted_iota(jnp.int32, (rb, 1), 0)
            vis_bias = jnp.where(c_end <= t_rows, 0.0, NEG_INF)
            has_visible = t_rows >= CMP_BLOCK - 1
            psum = jnp.zeros((rb, ncp), F32)
            for h in range(3):
                rows = slice(h * tq + r0, h * tq + r0 + rb)
                s = s_ref[rows, :] + vis_bias
                e = jnp.exp(s - jnp.max(s, axis=-1, keepdims=True))
                inv = jnp.where(has_visible, 1.0 / jnp.maximum(jnp.sum(e, axis=-1, keepdims=True), 1e-30), 0.0)
                pn = e * inv
                p_ref[rows, :] = pn.astype(CDT)
                psum = psum + pn
            p_hi = psum.astype(CDT)
            hi_ref[r0:r0 + rb, :] = p_hi
            lo_ref[r0:r0 + rb, :] = (psum - p_hi.astype(F32)).astype(CDT)
        o = _dot(p_ref[...], vc_ref[...])
        o_kv.append([o[p * tq:(p + 1) * tq] for p in range(3)])
        imp = _dot(hi_ref[...], ov_ref[...]) + _dot(lo_ref[...], ov_ref[...])
        work_t.append(jnp.where(causal & jnp.logical_not(forced), imp, -BIG).T)

    blk_lanes = blk_t[:, 0:LANES]

    def pick(_, work):
        m = jnp.max(work, axis=0, keepdims=True)
        idx = jnp.min(jnp.where(work == m, blk_lanes, ns), axis=0, keepdims=True)
        return jnp.where(blk_lanes == idx, -2.0 * BIG, work)

    for kv, start in enumerate(work_t):
        done = jnp.concatenate([lax.fori_loop(0, n_sel - 3, pick, start[:, c0:c0 + LANES], unroll=True)
                                for c0 in range(0, tq, LANES)], axis=1)
        taken = jnp.where((done < -BIG) & (start > -BIG), 1.0, 0.0).T
        selb = jnp.where(forced | (taken > 0.5), 0.0, NEG_INF)
        sel_ref[:, kv * ns:(kv + 1) * ns] = selb.astype(sel_ref.dtype)
    for p in range(3):
        o_ref[:, p * LANES:(p + 1) * LANES] = jnp.where(lo, o_kv[0][p], o_kv[1][p]).astype(o_ref.dtype)


CMP_CAUSAL_SPLITS = 4


def _cmp_select(zr, qcol, kc, vc, tq=256):
    b, s, _ = zr.shape
    ncp = kc.shape[1]
    ns = s // SEL_BLOCK
    n_sel = min(N_SEL, ns)
    assert n_sel >= 3, "selection needs room for the three forced blocks"
    tq = min(tq, s)
    c = np.arange(ncp)[:, None] * CMP_STRIDE
    j = np.arange(ns)[None, :] * SEL_BLOCK
    overlap = ((c < j + SEL_BLOCK) & (c + CMP_BLOCK - 1 >= j)).astype(np.float32)
    overlap[ncp - 1:, :] = 0.0
    ov = jnp.asarray(overlap, CDT)
    qw = 3 * LANES
    n_split = CMP_CAUSAL_SPLITS if (s // tq) % CMP_CAUSAL_SPLITS == 0 and ncp % (16 * CMP_CAUSAL_SPLITS) == 0 else 1
    tiles = s // tq // n_split
    outs, sels = [], []
    for part in range(n_split):
        tile0 = part * tiles
        ncp_part = ncp * (part + 1) // n_split
        o_part, sel_part = pl.pallas_call(
            functools.partial(_cmp_select_kernel, tq=tq, n_sel=n_sel, tile0=tile0),
            grid=(b, tiles),
            in_specs=[pl.BlockSpec((None, tq, qw), lambda bi, i, tile0=tile0: (bi, i + tile0, qcol)),
                      pl.BlockSpec((None, ncp_part, LANES), lambda bi, i: (bi, 0, 0)),
                      pl.BlockSpec((None, ncp_part, LANES), lambda bi, i: (bi, 0, 0)),
                      pl.BlockSpec((ncp_part, ns), lambda bi, i: (0, 0))],
            out_specs=[pl.BlockSpec((None, tq, qw), lambda bi, i: (bi, i, 0)),
                       pl.BlockSpec((None, tq, 2 * ns), lambda bi, i: (bi, i, 0))],
            out_shape=[jax.ShapeDtypeStruct((b, tiles * tq, qw), CDT),
                       jax.ShapeDtypeStruct((b, tiles * tq, 2 * ns), CDT)],
            scratch_shapes=[pltpu.VMEM((3 * tq, ncp_part), F32), pltpu.VMEM((3 * tq, ncp_part), CDT),
                            pltpu.VMEM((tq, ncp_part), CDT), pltpu.VMEM((tq, ncp_part), CDT)],
            compiler_params=_cparams(("parallel", "parallel"), 12),
            name=f"nsa_cmp_select_p{part}",
        )(zr, kc, vc, ov)
        outs.append(o_part)
        sels.append(sel_part)
    return jnp.concatenate(outs, axis=1), jnp.concatenate(sels, axis=1)


SLC_KEY_CHUNK = 64


def _slc_kernel(q_ref, k0_ref, k1_ref, vt_ref, selb_ref, o_ref, s_ref, e_ref, m_ref, l_ref, a_ref, acc_ref,
                *, tq, tk):
    i = pl.program_id(1)
    ns = selb_ref.shape[1] // 2
    bpt = tk // SEL_BLOCK
    n_q = 3 * tq
    rc = min(SLC_KEY_CHUNK, tk)
    lo = _half_masks()
    n_tiles = ((i + 1) * tq + tk - 1) // tk
    p_row = lax.broadcasted_iota(jnp.int32, (ns, LANES), 0)
    p_col = lax.broadcasted_iota(jnp.int32, (ns, LANES), 1)
    groups = []
    for kv, (hm, k_ref) in enumerate(((lo, k0_ref), (jnp.logical_not(lo), k1_ref))):
        q3 = [(jnp.where(hm, q_ref[:, p * LANES:(p + 1) * LANES], 0) * QK_SCALE).astype(CDT) for p in range(3)]
        selb = selb_ref[:, kv * ns:(kv + 1) * ns]
        lane0 = HEAD_DIM if kv == 0 else 0
        groups.append((kv, hm, k_ref, q3, selb, lane0))
    m_ref[...] = jnp.full(m_ref.shape, NEG_INF, F32)
    l_ref[...] = jnp.zeros(l_ref.shape, F32)
    acc_ref[...] = jnp.zeros(acc_ref.shape, F32)

    def scores(j, nk):
        k0 = pl.multiple_of(j * tk, tk)
        for kv, hm, k_ref, q3, selb, lane0 in groups:
            place = ((p_col >= lane0) & (p_col < lane0 + bpt) & (p_row == p_col - lane0 + j * bpt)).astype(CDT)
            sb = _dot(selb, place).astype(CDT)
            qp = jnp.concatenate([jnp.where(hm, q, sb) for q in q3], axis=0)
            s_ref[kv, 0:nk, :] = _nt_dot(k_ref[pl.ds(k0, nk), :], qp)

    def softmax_pv(j, diagonal, nk):
        k0 = pl.multiple_of(j * tk, tk)
        for kv in range(2):
            for c0 in range(0, n_q, LANES):
                cols = slice(c0, c0 + LANES)
                t_lane = i * tq + (c0 % tq) + lax.broadcasted_iota(jnp.int32, (1, LANES), 1)

                def chunk(r0):
                    s = s_ref[kv, r0:r0 + rc, cols]
                    if diagonal:
                        kpos = k0 + r0 + lax.broadcasted_iota(jnp.int32, (rc, LANES), 0)
                        s = jnp.where(kpos <= t_lane, s, NEG_INF)
                    return s

                m8 = m_ref[kv, :, cols]
                for r0 in range(0, nk, rc):
                    m8 = jnp.maximum(m8, jnp.max(chunk(r0).reshape(rc // 8, 8, LANES), axis=0))
                m_new = jnp.max(m8, axis=0, keepdims=True)
                alpha = jnp.exp(m_ref[kv, :, cols] - m_new)
                l8 = jnp.zeros((8, LANES), F32)
                for r0 in range(0, nk, rc):
                    e = jnp.exp(chunk(r0) - m_new)
                    l8 = l8 + jnp.sum(e.reshape(rc // 8, 8, LANES), axis=0)
                    e_ref[kv, r0:r0 + rc, cols] = e.astype(CDT)
                l_ref[kv, :, cols] = alpha * l_ref[kv, :, cols] + jnp.sum(l8, axis=0, keepdims=True)
                m_ref[kv, :, cols] = jnp.broadcast_to(m_new, (8, LANES))
                a_ref[kv, :, cols] = alpha
        for kv in range(2):
            acc_ref[kv] = a_ref[kv, 0:1, :] * acc_ref[kv] + _dot(vt_ref[j, :, 0:nk], e_ref[kv, 0:nk, :])

    def tile(j, diagonal, nk=tk):
        scores(j, nk)
        softmax_pv(j, diagonal, nk)

    lax.fori_loop(0, n_tiles - 1, lambda j, c: (tile(j, False), c)[1], 0)
    last = n_tiles - 1
    half = tk // 2
    first_half_only = (i + 1) * tq - last * tk <= half

    @pl.when(first_half_only)
    def _():
        tile(last, True, half)

    @pl.when(jnp.logical_not(first_half_only))
    def _():
        tile(last, True)
    o0 = (acc_ref[0] / l_ref[0, 0:1, :]).T
    o1 = (acc_ref[1] / l_ref[1, 0:1, :]).T
    for p in range(3):
        o_ref[:, p * LANES:(p + 1) * LANES] = jnp.where(
            lo, o0[p * tq:(p + 1) * tq], o1[p * tq:(p + 1) * tq]).astype(o_ref.dtype)


def _slc(zr, qcol, ksc, vt, selb, tq=512):
    b, s, _ = zr.shape
    tq = min(tq, s)
    tk = vt.shape[2]
    assert tq % LANES == 0 and tk // SEL_BLOCK <= HEAD_DIM and s % tk == 0
    qw = 3 * LANES
    ns2 = selb.shape[2]
    pat = jax.nn.one_hot((jnp.arange(s) // SEL_BLOCK) % (tk // SEL_BLOCK), HEAD_DIM, dtype=ksc.dtype)
    pat = jnp.broadcast_to(pat[None], (b, s, HEAD_DIM))
    k0 = jnp.concatenate([ksc[..., :HEAD_DIM], pat], axis=-1)
    k1 = jnp.concatenate([pat, ksc[..., HEAD_DIM:]], axis=-1)
    full = lambda bi, i: (bi, 0, 0)
    return pl.pallas_call(
        functools.partial(_slc_kernel, tq=tq, tk=tk),
        grid=(b, s // tq),
        in_specs=[pl.BlockSpec((None, tq, qw), lambda bi, i: (bi, i, qcol)),
                  _single_buffered((None, s, LANES), full),
                  _single_buffered((None, s, LANES), full),
                  _single_buffered((s // tk, LANES, tk), lambda bi, i: (bi, 0, 0)),
                  pl.BlockSpec((None, tq, ns2), lambda bi, i: (bi, i, 0))],
        out_specs=pl.BlockSpec((None, tq, qw), lambda bi, i: (bi, i, 0)),
        out_shape=jax.ShapeDtypeStruct((b, s, qw), CDT),
        scratch_shapes=[pltpu.VMEM((2, tk, 3 * tq), F32), pltpu.VMEM((2, tk, 3 * tq), CDT),
                        pltpu.VMEM((2, 8, 3 * tq), F32), pltpu.VMEM((2, 8, 3 * tq), F32),
                        pltpu.VMEM((2, 8, 3 * tq), F32), pltpu.VMEM((2, LANES, 3 * tq), F32)],
        compiler_params=_cparams(("parallel", "arbitrary"), 14),
        name="nsa_slc",
    )(zr, k0, k1, vt, selb)


def _merge_kernel(x_ref, oa0, oa1, oa2, la0, la1, la2, ob, ocmp, oslc, owin, od,
                  wn_ref, wg_ref, bg_ref, wb_ref, wo_ref, g_ref, b_ref, o_ref, *scratch, alpha):
    x = x_ref[...]
    xb = x.astype(CDT)
    bw = BRANCH_WIDTH

    def token_rows(src_ref, dst_ref):
        dil = src_ref.shape[1] // bw
        if dil == 1:
            return src_ref[...].astype(F32)
        n_cb = bw // LANES
        for r in range(dil):
            for cb in range(n_cb):
                c0 = r * bw + cb * LANES
                dst_ref[cb, pl.ds(r, src_ref.shape[0], stride=dil), :] = src_ref[:, c0:c0 + LANES].astype(F32)
        return jnp.concatenate([dst_ref[cb] for cb in range(n_cb)], axis=1)

    o0, o1, o2 = (token_rows(s, d) for s, d in zip((oa0, oa1, oa2), scratch[0:3]))
    l0, l1, l2 = (token_rows(s, d) for s, d in zip((la0, la1, la2), scratch[3:6]))
    mx = jnp.maximum(jnp.maximum(l0, l1), l2)
    w0, w1, w2 = jnp.exp(l0 - mx), jnp.exp(l1 - mx), jnp.exp(l2 - mx)
    o_a = (w0 * o0 + w1 * o1 + w2 * o2) / (w0 + w1 + w2)
    gates = jax.nn.sigmoid(_dot(xb, wn_ref[...]))
    o_c = (gates[:, 0:bw] * ocmp[...].astype(F32) + gates[:, bw:2 * bw] * oslc[...].astype(F32)
           + gates[:, 2 * bw:3 * bw] * owin[...].astype(F32))
    branches = (o_a.astype(CDT), ob[...], o_c.astype(CDT), od[...])
    d = x.shape[1]
    merged = jnp.zeros(x.shape, F32)
    for m in range(N_BRANCH):
        gate = jax.nn.sigmoid(_dot(xb, wg_ref[:, m * d:(m + 1) * d]) + bg_ref[:, m * d:(m + 1) * d])
        merged = merged + gate * _dot(branches[m], wb_ref[m])
    r = alpha * x + _dot(merged.astype(CDT), wo_ref[...])
    o_ref[...] = _layer_norm(r, g_ref[...], b_ref[...])


def _merge(x2d, branch_inputs, wn, wg, bg, wb, wo, g, b, alpha, tm=256):
    t, d = x2d.shape
    tm = min(tm, t)
    row = lambda i: (i, 0)
    const2 = lambda i: (0, 0)
    in_specs = [pl.BlockSpec((tm, d), row)]
    in_specs += [pl.BlockSpec((tm * a.shape[0] // t, a.shape[1]), row) for a in branch_inputs]
    in_specs += [pl.BlockSpec(wn.shape, const2), pl.BlockSpec(wg.shape, const2), pl.BlockSpec(bg.shape, const2),
                 pl.BlockSpec(wb.shape, lambda i: (0, 0, 0)), pl.BlockSpec(wo.shape, const2),
                 pl.BlockSpec(g.shape, const2), pl.BlockSpec(b.shape, const2)]
    return pl.pallas_call(
        functools.partial(_merge_kernel, alpha=alpha),
        grid=(t // tm,),
        in_specs=in_specs,
        out_specs=pl.BlockSpec((tm, d), row),
        out_shape=jax.ShapeDtypeStruct((t, d), F32),
        scratch_shapes=[pltpu.VMEM((BRANCH_WIDTH // LANES, tm, LANES), F32) for _ in range(2 * N_DIL)],
        compiler_params=_cparams(("parallel",), 14),
        name="merge_ln",
    )(x2d, *branch_inputs, wn, wg, bg, wb, wo, g, b)


def _ple_ln(x, xb, f, p, plw_ref, pgw_ref, pgb_ref, g_ref, b_ref, alpha):
    ple = jax.nn.sigmoid(_dot(xb, pgw_ref[...]) + pgb_ref[...]) * _dot(p.astype(CDT), plw_ref[...])
    return _layer_norm(alpha * x + f + ple, g_ref[...], b_ref[...])


FFN_CHUNK = 512


def _ffn_kernel(x_ref, p_ref, wg_ref, wu_ref, wd_ref, plw_ref, pgw_ref, pgb_ref, g_ref, b_ref, o_ref, h_ref, *, alpha):
    x = x_ref[...]
    xb = x.astype(CDT)
    dff = wg_ref.shape[1]
    for c0 in range(0, dff, FFN_CHUNK):
        cols = slice(c0, min(c0 + FFN_CHUNK, dff))
        h_ref[:, cols] = (jax.nn.silu(_dot(xb, wg_ref[:, cols])) * _dot(xb, wu_ref[:, cols])).astype(CDT)
    f = _dot(h_ref[...], wd_ref[...])
    o_ref[...] = _ple_ln(x, xb, f, p_ref[...], plw_ref, pgw_ref, pgb_ref, g_ref, b_ref, alpha)


def _ffn(x2d, p2d, wg, wu, wd, plw, pgw, pgb, g, b, alpha, tm=512):
    t, d = x2d.shape
    tm = min(tm, t)
    dff = wg.shape[1]
    wg, wu, wd = wg.astype(CDT), wu.astype(CDT), wd.astype(CDT)
    row = lambda i: (i, 0)
    const = lambda shape: pl.BlockSpec(shape, lambda i: (0, 0), pipeline_mode=pl.Buffered(1))
    return pl.pallas_call(
        functools.partial(_ffn_kernel, alpha=alpha),
        grid=(t // tm,),
        in_specs=[pl.BlockSpec((tm, d), row), pl.BlockSpec((tm, p2d.shape[1]), row),
                  const(wg.shape), const(wu.shape), const(wd.shape),
                  const(plw.shape), const(pgw.shape), const(pgb.shape), const(g.shape), const(b.shape)],
        out_specs=pl.BlockSpec((tm, d), row),
        out_shape=jax.ShapeDtypeStruct((t, d), F32),
        scratch_shapes=[pltpu.VMEM((tm, dff), CDT)],
        compiler_params=_cparams(("parallel",), 12),
        name="ffn_ple_ln",
    )(x2d, p2d, wg, wu, wd, plw, pgw, pgb, g, b)


def _router_kernel(x_ref, wh_ref, wl_ref, b_ref, comb_ref, rank_ref, rank_t_ref, cnt_ref):
    x = x_ref[...]
    xh = x.astype(CDT)
    xl = (x - xh.astype(F32)).astype(CDT)
    logits = _dot(xh, wh_ref[...]) + _dot(xh, wl_ref[...]) + _dot(xl, wh_ref[...]) + b_ref[...]
    lane = lax.broadcasted_iota(jnp.int32, logits.shape, 1)
    v1 = jnp.max(logits, axis=-1, keepdims=True)
    i1 = jnp.min(jnp.where(logits == v1, lane, LANES), axis=-1, keepdims=True)
    rest = jnp.where(lane == i1, -jnp.inf, logits)
    v2 = jnp.max(rest, axis=-1, keepdims=True)
    i2 = jnp.min(jnp.where(rest == v2, lane, LANES), axis=-1, keepdims=True)
    e2 = jnp.exp(v2 - v1)
    comb_ref[...] = jnp.where(lane == i1, 1.0 / (1.0 + e2), 0.0) + jnp.where(lane == i2, e2 / (1.0 + e2), 0.0)
    routed = (lane == i1) | (lane == i2)
    mask = routed.astype(CDT)
    tm = x.shape[0]
    before = (lax.broadcasted_iota(jnp.int32, (tm, tm), 1) < lax.broadcasted_iota(jnp.int32, (tm, tm), 0)).astype(CDT)
    rank = jnp.where(routed, _dot(before, mask), -1.0)
    rank_ref[...] = rank
    rank_t_ref[...] = rank.T[0:rank_t_ref.shape[0], :]
    cnt_ref[...] = jnp.sum(routed.astype(F32), axis=0, keepdims=True).astype(jnp.int32)


def _router(x2d, w_router, b_router, tm):
    t, d = x2d.shape
    ne = w_router.shape[1]
    wp = jnp.zeros((d, LANES), F32).at[:, :ne].set(w_router)
    wh = wp.astype(CDT)
    wl = (wp - wh.astype(F32)).astype(CDT)
    bp = jnp.full((1, LANES), -BIG, F32).at[0, :ne].set(b_router)
    nt = t // tm
    row = lambda i: (i, 0)
    return pl.pallas_call(
        _router_kernel,
        grid=(nt,),
        in_specs=[pl.BlockSpec((tm, d), row), pl.BlockSpec(wh.shape, lambda i: (0, 0)),
                  pl.BlockSpec(wl.shape, lambda i: (0, 0)), pl.BlockSpec(bp.shape, lambda i: (0, 0))],
        out_specs=[pl.BlockSpec((tm, LANES), row), pl.BlockSpec((tm, LANES), row),
                   pl.BlockSpec((None, 8, tm), lambda i: (i, 0, 0)),
                   pl.BlockSpec((None, 1, LANES), lambda i: (i, 0, 0))],
        out_shape=[jax.ShapeDtypeStruct((t, LANES), F32), jax.ShapeDtypeStruct((t, LANES), F32),
                   jax.ShapeDtypeStruct((nt, 8, tm), F32), jax.ShapeDtypeStruct((nt, 1, LANES), jnp.int32)],
        compiler_params=_cparams(("parallel",), 10),
        name="moe_router",
    )(x2d, wh, wl, bp)


def _moe_kernel(cnt_ref, x_ref, comb_ref, rank_ref, rank_t_ref, p_ref, wg_ref, wu_ref, wd_ref, plw_ref, pgw_ref,
                pgb_ref, g_ref, b_ref, o_ref, xe_ref, ye_ref, *, alpha, rs, seg):
    i = pl.program_id(0)
    e = pl.program_id(1)
    c = pl.program_id(2)
    n_seg = x_ref.shape[0] // seg
    last_chunk = c == pl.num_programs(2) - 1

    first = slice(0, rs)
    toks = [slice(sg * seg, (sg + 1) * seg) for sg in range(n_seg)]
    n_groups = [(cnt_ref[(i * n_seg + sg) * LANES + e] + rs - 1) // rs for sg in range(n_seg)]

    def later(sc):
        return pl.ds(pl.multiple_of(sc * rs, 8), rs), (sc * rs).astype(F32)

    @pl.when((e == 0) & (c == 0))
    def _():
        o_ref[...] = jnp.zeros_like(o_ref)

    @pl.when(c == 0)
    def _():
        row_id = lax.broadcasted_iota(jnp.int32, (rs, seg), 0).astype(F32)
        xbs = [x_ref[tok, :].astype(CDT) for tok in toks]
        rank_rows = [rank_t_ref[sg, pl.ds(e, 1), :] for sg in range(n_seg)]

        def gather(sg, rws, base):
            onehot = (rank_rows[sg] - base == row_id).astype(CDT)
            xe_ref[sg, rws, :] = _dot(onehot, xbs[sg]).astype(CDT)
            ye_ref[sg, rws, :] = jnp.zeros((rs, ye_ref.shape[2]), F32)

        for sg in range(n_seg):
            gather(sg, first, 0.0)
        for sg in range(n_seg):
            lax.fori_loop(1, n_groups[sg], lambda sc, _, sg=sg: (gather(sg, *later(sc)), 0)[1], 0)

    def expert(sg, rws):
        xs = xe_ref[sg, rws, :]
        h = jax.nn.silu(_dot(xs, wg_ref[0])) * _dot(xs, wu_ref[0])
        ye_ref[sg, rws, :] += _dot(h.astype(CDT), wd_ref[0])

    for sg in range(n_seg):
        for r0 in range(0, rs, rs // 2):
            expert(sg, slice(r0, r0 + rs // 2))
    for sg in range(n_seg):
        lax.fori_loop(1, n_groups[sg], lambda sc, _, sg=sg: (expert(sg, later(sc)[0]), 0)[1], 0)

    @pl.when(last_chunk)
    def _():
        lane = lax.broadcasted_iota(jnp.int32, (seg, LANES), 1)
        mine = lane == e
        col_id = lax.broadcasted_iota(jnp.int32, (seg, rs), 1).astype(F32)
        cws = [jnp.sum(jnp.where(mine, comb_ref[tok, :], 0.0), axis=-1, keepdims=True) for tok in toks]
        rank_cols = [jnp.sum(jnp.where(mine, rank_ref[tok, :], 0.0), axis=-1, keepdims=True) for tok in toks]

        def scatter(sg, rws, base):
            onehot = (rank_cols[sg] - base == col_id).astype(CDT)
            o_ref[toks[sg], :] += cws[sg] * _dot(onehot, ye_ref[sg, rws, :].astype(CDT))

        for sg in range(n_seg):
            scatter(sg, first, 0.0)
        for sg in range(n_seg):
            lax.fori_loop(1, n_groups[sg], lambda sc, _, sg=sg: (scatter(sg, *later(sc)), 0)[1], 0)

    @pl.when((e == pl.num_programs(1) - 1) & last_chunk)
    def _():
        for sg in range(n_seg):
            tok = slice(sg * seg, (sg + 1) * seg)
            x = x_ref[tok, :]
            o_ref[tok, :] = _ple_ln(x, x.astype(CDT), o_ref[tok, :], p_ref[tok, :], plw_ref, pgw_ref, pgb_ref,
                                    g_ref, b_ref, alpha)


MOE_CHUNK = 512
MOE_SEGMENT = 1024
MOE_SEGMENTS_PER_TILE = 2
MOE_ROW_GROUP = 288


def _single_buffered(shape, index_map):
    return pl.BlockSpec(shape, index_map, pipeline_mode=pl.Buffered(1))


def _moe(x2d, routing, p2d, wg, wu, wd, plw, pgw, pgb, g, b, alpha, seg):
    comb, rank, rank_t, cnt = routing
    t, d = x2d.shape
    ne, _, dff = wg.shape
    ck = min(MOE_CHUNK, dff)
    rs = min(MOE_ROW_GROUP, seg)
    n_seg = min(MOE_SEGMENTS_PER_TILE, t // seg)
    tm = n_seg * seg
    max_rows = -(-seg // rs) * rs
    wg, wu, wd = wg.astype(CDT), wu.astype(CDT), wd.astype(CDT)
    row = lambda i, e, c, cnt: (i, 0)
    c2 = lambda i, e, c, cnt: (0, 0)
    grid_spec = pltpu.PrefetchScalarGridSpec(
        num_scalar_prefetch=1,
        grid=(t // tm, ne, dff // ck),
        in_specs=[_single_buffered((tm, d), row), _single_buffered((tm, LANES), row),
                  _single_buffered((tm, LANES), row),
                  _single_buffered((n_seg, 8, seg), lambda i, e, c, cnt: (i, 0, 0)),
                  _single_buffered((tm, p2d.shape[1]), row),
                  pl.BlockSpec((1, d, ck), lambda i, e, c, cnt: (e, 0, c)),
                  pl.BlockSpec((1, d, ck), lambda i, e, c, cnt: (e, 0, c)),
                  pl.BlockSpec((1, ck, d), lambda i, e, c, cnt: (e, c, 0)),
                  _single_buffered(plw.shape, c2), _single_buffered(pgw.shape, c2), _single_buffered(pgb.shape, c2),
                  _single_buffered(g.shape, c2), _single_buffered(b.shape, c2)],
        out_specs=pl.BlockSpec((tm, d), row),
        scratch_shapes=[pltpu.VMEM((n_seg, max_rows, d), CDT), pltpu.VMEM((n_seg, max_rows, d), F32)],
    )
    return pl.pallas_call(
        functools.partial(_moe_kernel, alpha=alpha, rs=rs, seg=seg),
        grid_spec=grid_spec,
        out_shape=jax.ShapeDtypeStruct((t, d), F32),
        compiler_params=_cparams(("parallel", "arbitrary", "arbitrary"), 15),
        name="moe_ple_ln",
    )(cnt.reshape(-1), x2d, comb, rank, rank_t, p2d, wg, wu, wd, plw, pgw, pgb, g, b)


def _prep_in_weights(w_in):
    o = COL_OFF
    bw = BRANCH_WIDTH
    cols = lambda n: w_in[:, o[n]:o[n + 1]]
    qa, ka, va = cols(0), cols(1), cols(2)
    w_dil = jnp.concatenate(
        [t[:, g * bw:(g + 1) * bw] for g in range(N_DIL) for t in (qa, ka, va)], axis=1).astype(CDT)
    w_conv = jnp.concatenate([cols(3), cols(4), cols(5)], axis=1).astype(CDT)
    gn = cols(13)
    w_gate = jnp.concatenate([gn[:, br * NSA_Q_HEADS + GQA_COL_HEAD] for br in range(3)], axis=1).astype(CDT)
    w_rest = jnp.concatenate([cols(6)[:, GQA_COL_PERM], cols(14)[:, GQA_COL_PERM], cols(11),
                              cols(12), cols(15), cols(16), cols(9), cols(7), cols(8)], axis=1).astype(CDT)
    wt_vsc = cols(10).T.astype(CDT)
    return w_dil, w_conv, w_gate, w_rest, wt_vsc


ZR_Q_NSA, ZR_Q_SWA = 0, 1
ZR_KWC, ZR_VWC, ZR_KD, ZR_VD = 6, 7, 8, 9
ZR_WIDTH = 2 * BRANCH_WIDTH + 4 * LANES
SLC_KEY_TILE = 1024


def _token_mixers(x, w_in, conv_w, cmp_pos, cmp_w1, cmp_b1, cmp_w2, cmp_b2, sinks):
    b, s, d = x.shape
    x2d = x.reshape(b * s, d)
    w_dil, w_conv, w_gate, w_rest, wt_vsc = _prep_in_weights(w_in)
    gw = 3 * BRANCH_WIDTH

    z_dil = _linear(x2d, w_dil, [(g * gw, (g + 1) * gw) for g in range(N_DIL)], gw, "in_proj_dil",
                    dils=[dil for _, dil in DIL_PATTERNS])
    zr, ksc, kcc, vcc = _linear(x2d, w_rest, [(0, ZR_WIDTH)] + [(ZR_WIDTH + n * LANES, ZR_WIDTH + (n + 1) * LANES)
                                                           for n in range(3)], 256, "in_proj_rest")
    zr = zr.reshape(b, s, ZR_WIDTH)
    o_b = _conv(x, w_conv, conv_w)

    dil_o, dil_lse = [], []
    for g, (window, dil) in enumerate(DIL_PATTERNS):
        view = z_dil[g].reshape(b, s // dil, dil * gw)
        og, lg = _banded(view, view, view, nrep=dil,
                         qcol=lambda r: 3 * r, kcol=lambda r: 3 * r + 1, vcol=lambda r: 3 * r + 2,
                         kw=3 * LANES, window=window // dil, want_lse=True)
        dil_o.append(og.reshape(b * s // dil, dil * BRANCH_WIDTH))
        dil_lse.append(lg.reshape(b * s // dil, dil * BRANCH_WIDTH))

    kc = _compress(kcc.reshape(b, s, LANES), cmp_pos[0], cmp_w1[0], cmp_b1[0], cmp_w2[0], cmp_b2[0])
    vc = _compress(vcc.reshape(b, s, LANES), cmp_pos[1], cmp_w1[1], cmp_b1[1], cmp_w2[1], cmp_b2[1])
    o_cmp, selb = _cmp_select(zr, ZR_Q_NSA, kc, vc)
    vsc_t = _linear_t(x2d, wt_vsc, min(SLC_KEY_TILE, s), "in_proj_vsc_t")
    o_slc = _slc(zr, ZR_Q_NSA, ksc.reshape(b, s, LANES), vsc_t, selb)
    (o_win,) = _banded(zr, zr, zr, nrep=1, qcol=lambda r: ZR_Q_NSA, kcol=lambda r: ZR_KWC, vcol=lambda r: ZR_VWC,
                       kw=LANES, window=NSA_WINDOW - 1, want_lse=False, tq=512)

    sink_row = sinks.astype(F32)[GQA_COL_HEAD].reshape(1, BRANCH_WIDTH)
    (o_d,) = _banded(zr, zr, zr, nrep=1, qcol=lambda r: ZR_Q_SWA, kcol=lambda r: ZR_KD, vcol=lambda r: ZR_VD,
                     kw=LANES, window=SWA_WINDOW - 1, want_lse=False, sink_row=sink_row)

    t = b * s
    flat = lambda a: a.reshape(t, a.shape[-1])
    return [dil_o[0], dil_o[1], dil_o[2], dil_lse[0], dil_lse[1], dil_lse[2], flat(o_b), flat(o_cmp), flat(o_slc),
            flat(o_win), flat(o_d)], w_gate


def kernel(x, p, w_in, conv_w, cmp_pos, cmp_w1, cmp_b1, cmp_w2, cmp_b2, sinks, w_branch, w_merge_gate, b_merge_gate, w_out, ln_mix_g, ln_mix_b, ffn_w_gate, ffn_w_up, ffn_w_down, w_router, b_router, moe_w_gate, moe_w_up, moe_w_down, ple_w, ple_gate_w, ple_gate_b, ln_ffn_g, ln_ffn_b):
    depth, b, s, _ = p.shape
    d = x.shape[-1]
    t = b * s
    alpha = (2 * depth) ** 0.25
    row = lambda v: v.reshape(1, -1).astype(F32)
    for i in range(depth):
        branch_inputs, w_nsa_gate = _token_mixers(x, w_in[i], conv_w[i], cmp_pos[i], cmp_w1[i], cmp_b1[i], cmp_w2[i],
                                                  cmp_b2[i], sinks[i])
        wg = jnp.concatenate([w_merge_gate[i, m] for m in range(N_BRANCH)], axis=1).astype(CDT)
        bg = b_merge_gate[i].reshape(1, N_BRANCH * d).astype(F32)
        wb = jnp.stack([w_branch[i, 0], w_branch[i, 1], w_branch[i, 2][GQA_COL_PERM],
                        w_branch[i, 3][GQA_COL_PERM]]).astype(CDT)
        x1 = _merge(x.reshape(t, d), branch_inputs, w_nsa_gate, wg, bg, wb, w_out[i].astype(CDT),
                    row(ln_mix_g[i]), row(ln_mix_b[i]), alpha)
        p2d = p[i].reshape(t, -1)
        ple_args = (ple_w[i].astype(CDT), ple_gate_w[i].astype(CDT), row(ple_gate_b[i]),
                    row(ln_ffn_g[i]), row(ln_ffn_b[i]))
        j = i // 2
        if i % 2 == 0:
            x2 = _ffn(x1, p2d, ffn_w_gate[j], ffn_w_up[j], ffn_w_down[j], *ple_args, alpha)
        else:
            seg = min(MOE_SEGMENT, t)
            routing = _router(x1, w_router[j], b_router[j], seg)
            x2 = _moe(x1, routing, p2d, moe_w_gate[j], moe_w_up[j], moe_w_down[j], *ple_args, alpha, seg)
        x = x2.reshape(b, s, d)
    return x
```
